```python
import math
import jax, jax.numpy as jnp
from jax import lax
import numpy as np

D_MODEL = 1024
BATCH = 8
SEQ = 16384
DEPTH = 2

N_Q = 8
N_KV = 2
GROUP = N_Q // N_KV
HEAD_DIM = 64
ATTN_W = N_Q * HEAD_DIM
KV_W = N_KV * HEAD_DIM
WINDOW = 128
BLOCK = 128
CONV_C = D_MODEL // 2
CONV_K = 31
D_FF = 4 * D_MODEL
IN_W = ATTN_W + 2 * KV_W + 2 * CONV_C + 2 * D_MODEL
EPS = 1e-6
NEG = -1e30

kernel_name = "hybrid_swa_sink_alibi_conformer_conv_gated_block"


def rms_norm(x, g):
    xf = x.astype(jnp.float32)
    y = xf * lax.rsqrt(jnp.mean(xf * xf, axis=-1, keepdims=True) + EPS)
    return (y * g.astype(jnp.float32)).astype(x.dtype)


def layer_norm(x, g, b):
    xf = x.astype(jnp.float32)
    mu = jnp.mean(xf, axis=-1, keepdims=True)
    xc = xf - mu
    var = jnp.mean(xc * xc, axis=-1, keepdims=True)
    y = xc * lax.rsqrt(var + EPS) * g.astype(jnp.float32) + b.astype(jnp.float32)
    return y.astype(x.dtype)


def alibi_slopes():
    return jnp.asarray(2.0 ** (-8.0 * np.arange(1, N_Q + 1) / N_Q), dtype=jnp.float32)


def sliding_window_attention(q, k, v, sinks):
    B, S = q.shape[0], q.shape[1]
    nb = S // BLOCK
    qb = q.reshape(B, nb, BLOCK, N_KV, GROUP, HEAD_DIM)

    def band(t):
        t = t.reshape(B, S, N_KV, HEAD_DIM)
        tp = jnp.pad(t, ((0, 0), (BLOCK, 0), (0, 0), (0, 0))).reshape(B, nb + 1, BLOCK, N_KV, HEAD_DIM)
        return jnp.concatenate([tp[:, :-1], tp[:, 1:]], axis=2)

    kb, vb = band(k), band(v)
    scale = 1.0 / math.sqrt(HEAD_DIM)
    s = jnp.einsum('bnqkgd,bnskd->bkgnqs', qb, kb).astype(jnp.float32) * scale

    qi = jnp.arange(BLOCK)[:, None] + BLOCK
    kj = jnp.arange(2 * BLOCK)[None, :]
    dist = qi - kj
    key_pos = jnp.arange(nb)[:, None, None] * BLOCK + kj[None] - BLOCK
    valid = (dist >= 0)[None] & (dist < WINDOW)[None] & (key_pos >= 0)

    slopes = alibi_slopes().reshape(N_KV, GROUP)
    s = s - slopes[None, :, :, None, None, None] * dist.astype(jnp.float32)
    s = jnp.where(valid, s, NEG)

    sink = sinks.astype(jnp.float32).reshape(N_KV, GROUP)[None, :, :, None, None]
    m = jnp.maximum(jnp.max(s, axis=-1), sink)
    p = jnp.exp(s - m[..., None])
    denom = jnp.sum(p, axis=-1) + jnp.exp(sink - m)
    p = p / denom[..., None]
    o = jnp.einsum('bkgnqs,bnskd->bnqkgd', p.astype(v.dtype), vb)
    return o.reshape(B, S, ATTN_W)


def causal_depthwise_conv(u, w, b):
    y = lax.conv_general_dilated(
        u, w[:, None, :].astype(u.dtype), window_strides=(1,), padding=[(CONV_K - 1, 0)],
        dimension_numbers=('NWC', 'WIO', 'NWC'), feature_group_count=CONV_C)
    return y + b


def _fwd_setup_inputs(seed: int = 0) -> dict:
    key = jax.random.key(seed)
    ks = jax.random.split(key, 20)
    f = jnp.float32
    nrm = lambda k, shp, s: jax.random.normal(k, shp, f) * s
    return {
        "x": nrm(ks[0], (BATCH, SEQ, D_MODEL), 1.0),
        "mix_norm_g": 1.0 + nrm(ks[1], (DEPTH, D_MODEL), 0.01),
        "w_in": nrm(ks[2], (DEPTH, D_MODEL, IN_W), D_MODEL ** -0.5),
        "b_in": nrm(ks[3], (DEPTH, IN_W), 0.02),
        "sinks": nrm(ks[4], (DEPTH, N_Q), 0.5),
        "conv_w": nrm(ks[5], (DEPTH, CONV_K, CONV_C), CONV_K ** -0.5),
        "conv_b": nrm(ks[6], (DEPTH, CONV_C), 0.02),
        "conv_ln_g": 1.0 + nrm(ks[7], (DEPTH, CONV_C), 0.01),
        "conv_ln_b": nrm(ks[8], (DEPTH, CONV_C), 0.01),
        "w_attn_proj": nrm(ks[9], (DEPTH, ATTN_W, D_MODEL), ATTN_W ** -0.5),
        "w_conv_proj": nrm(ks[10], (DEPTH, CONV_C, D_MODEL), CONV_C ** -0.5),
        "b_conv_proj": nrm(ks[11], (DEPTH, D_MODEL), 0.02),
        "w_out": nrm(ks[12], (DEPTH, D_MODEL, D_MODEL), D_MODEL ** -0.5),
        "mlp_norm_g": 1.0 + nrm(ks[13], (DEPTH, D_MODEL), 0.01),
        "w_mlp1": nrm(ks[14], (DEPTH, D_MODEL, D_FF), D_MODEL ** -0.5),
        "w_mlp2": nrm(ks[15], (DEPTH, D_FF, D_MODEL), D_FF ** -0.5),
        "final_norm_g": 1.0 + nrm(ks[16], (D_MODEL,), 0.01),
    }


def _fwd_reference(x, mix_norm_g, w_in, b_in, sinks, conv_w, conv_b, conv_ln_g, conv_ln_b,
              w_attn_proj, w_conv_proj, b_conv_proj, w_out, mlp_norm_g, w_mlp1, w_mlp2,
              final_norm_g):
    splits = np.cumsum([ATTN_W, KV_W, KV_W, CONV_C, CONV_C, D_MODEL]).tolist()
    for l in range(DEPTH):
        h = rms_norm(x, mix_norm_g[l])
        proj = jnp.einsum('bsd,de->bse', h, w_in[l]) + b_in[l]
        q, k, v, glu_a, glu_b, gate_a, gate_c = jnp.split(proj, splits, axis=-1)

        attn = sliding_window_attention(q, k, v, sinks[l])
        br_a = jnp.einsum('bse,ed->bsd', attn, w_attn_proj[l])

        u = glu_a * jax.nn.sigmoid(glu_b)
        u = causal_depthwise_conv(u, conv_w[l], conv_b[l])
        u = jax.nn.silu(layer_norm(u, conv_ln_g[l], conv_ln_b[l]))
        br_c = jnp.einsum('bsc,cd->bsd', u, w_conv_proj[l]) + b_conv_proj[l]

        merged = jax.nn.sigmoid(gate_a) * br_a + jax.nn.sigmoid(gate_c) * br_c
        x = x + jnp.einsum('bsd,de->bse', merged, w_out[l])

        h2 = rms_norm(x, mlp_norm_g[l])
        a = jnp.square(jax.nn.relu(jnp.einsum('bsd,df->bsf', h2, w_mlp1[l])))
        x = x + jnp.einsum('bsf,fd->bsd', a, w_mlp2[l])
    return rms_norm(x, final_norm_g)


import jax as _jax
import jax.numpy as _jnp

TWIN_FORMAT = 'train_step'
FWD_PARAMS = ['x', 'mix_norm_g', 'w_in', 'b_in', 'sinks', 'conv_w', 'conv_b', 'conv_ln_g', 'conv_ln_b', 'w_attn_proj', 'w_conv_proj', 'b_conv_proj', 'w_out', 'mlp_norm_g', 'w_mlp1', 'w_mlp2', 'final_norm_g']
TWIN_WEIGHTS = ['mix_norm_g', 'w_in', 'b_in', 'sinks', 'conv_w', 'conv_b', 'conv_ln_g', 'conv_ln_b', 'w_attn_proj', 'w_conv_proj', 'b_conv_proj', 'w_out', 'mlp_norm_g', 'w_mlp1', 'w_mlp2', 'final_norm_g']
TWIN_DIFF_INPUT = 'x'
TWIN_INPUTS = ['x', 'mix_norm_g', 'w_in', 'b_in', 'sinks', 'conv_w', 'conv_b', 'conv_ln_g', 'conv_ln_b', 'w_attn_proj', 'w_conv_proj', 'b_conv_proj', 'w_out', 'mlp_norm_g', 'w_mlp1', 'w_mlp2', 'final_norm_g', 'loss_target', 'm_mix_norm_g', 'm_w_in', 'm_b_in', 'm_sinks', 'm_conv_w', 'm_conv_b', 'm_conv_ln_g', 'm_conv_ln_b', 'm_w_attn_proj', 'm_w_conv_proj', 'm_b_conv_proj', 'm_w_out', 'm_mlp_norm_g', 'm_w_mlp1', 'm_w_mlp2', 'm_final_norm_g', 'v_mix_norm_g', 'v_w_in', 'v_b_in', 'v_sinks', 'v_conv_w', 'v_conv_b', 'v_conv_ln_g', 'v_conv_ln_b', 'v_w_attn_proj', 'v_w_conv_proj', 'v_b_conv_proj', 'v_w_out', 'v_mlp_norm_g', 'v_w_mlp1', 'v_w_mlp2', 'v_final_norm_g']
TWIN_OUTPUTS = ['loss', 'grad_x', 'grad_mix_norm_g', 'grad_w_in', 'grad_b_in', 'grad_sinks', 'grad_conv_w', 'grad_conv_b', 'grad_conv_ln_g', 'grad_conv_ln_b', 'grad_w_attn_proj', 'grad_w_conv_proj', 'grad_b_conv_proj', 'grad_w_out', 'grad_mlp_norm_g', 'grad_w_mlp1', 'grad_w_mlp2', 'grad_final_norm_g', 'delta_mix_norm_g', 'delta_w_in', 'delta_b_in', 'delta_sinks', 'delta_conv_w', 'delta_conv_b', 'delta_conv_ln_g', 'delta_conv_ln_b', 'delta_w_attn_proj', 'delta_w_conv_proj', 'delta_b_conv_proj', 'delta_w_out', 'delta_mlp_norm_g', 'delta_w_mlp1', 'delta_w_mlp2', 'delta_final_norm_g', 'new_m_mix_norm_g', 'new_m_w_in', 'new_m_b_in', 'new_m_sinks', 'new_m_conv_w', 'new_m_conv_b', 'new_m_conv_ln_g', 'new_m_conv_ln_b', 'new_m_w_attn_proj', 'new_m_w_conv_proj', 'new_m_b_conv_proj', 'new_m_w_out', 'new_m_mlp_norm_g', 'new_m_w_mlp1', 'new_m_w_mlp2', 'new_m_final_norm_g', 'new_v_mix_norm_g', 'new_v_w_in', 'new_v_b_in', 'new_v_sinks', 'new_v_conv_w', 'new_v_conv_b', 'new_v_conv_ln_g', 'new_v_conv_ln_b', 'new_v_w_attn_proj', 'new_v_w_conv_proj', 'new_v_b_conv_proj', 'new_v_w_out', 'new_v_mlp_norm_g', 'new_v_w_mlp1', 'new_v_w_mlp2', 'new_v_final_norm_g']
TWIN_LEAF_KINDS = {'loss': 'loss', 'grad_x': 'grad_x', 'grad_mix_norm_g': 'grad_w', 'grad_w_in': 'grad_w', 'grad_b_in': 'grad_w', 'grad_sinks': 'grad_w', 'grad_conv_w': 'grad_w', 'grad_conv_b': 'grad_w', 'grad_conv_ln_g': 'grad_w', 'grad_conv_ln_b': 'grad_w', 'grad_w_attn_proj': 'grad_w', 'grad_w_conv_proj': 'grad_w', 'grad_b_conv_proj': 'grad_w', 'grad_w_out': 'grad_w', 'grad_mlp_norm_g': 'grad_w', 'grad_w_mlp1': 'grad_w', 'grad_w_mlp2': 'grad_w', 'grad_final_norm_g': 'grad_w', 'delta_mix_norm_g': 'delta_w', 'delta_w_in': 'delta_w', 'delta_b_in': 'delta_w', 'delta_sinks': 'delta_w', 'delta_conv_w': 'delta_w', 'delta_conv_b': 'delta_w', 'delta_conv_ln_g': 'delta_w', 'delta_conv_ln_b': 'delta_w', 'delta_w_attn_proj': 'delta_w', 'delta_w_conv_proj': 'delta_w', 'delta_b_conv_proj': 'delta_w', 'delta_w_out': 'delta_w', 'delta_mlp_norm_g': 'delta_w', 'delta_w_mlp1': 'delta_w', 'delta_w_mlp2': 'delta_w', 'delta_final_norm_g': 'delta_w', 'new_m_mix_norm_g': 'new_m', 'new_m_w_in': 'new_m', 'new_m_b_in': 'new_m', 'new_m_sinks': 'new_m', 'new_m_conv_w': 'new_m', 'new_m_conv_b': 'new_m', 'new_m_conv_ln_g': 'new_m', 'new_m_conv_ln_b': 'new_m', 'new_m_w_attn_proj': 'new_m', 'new_m_w_conv_proj': 'new_m', 'new_m_b_conv_proj': 'new_m', 'new_m_w_out': 'new_m', 'new_m_mlp_norm_g': 'new_m', 'new_m_w_mlp1': 'new_m', 'new_m_w_mlp2': 'new_m', 'new_m_final_norm_g': 'new_m', 'new_v_mix_norm_g': 'new_v', 'new_v_w_in': 'new_v', 'new_v_b_in': 'new_v', 'new_v_sinks': 'new_v', 'new_v_conv_w': 'new_v', 'new_v_conv_b': 'new_v', 'new_v_conv_ln_g': 'new_v', 'new_v_conv_ln_b': 'new_v', 'new_v_w_attn_proj': 'new_v', 'new_v_w_conv_proj': 'new_v', 'new_v_b_conv_proj': 'new_v', 'new_v_w_out': 'new_v', 'new_v_mlp_norm_g': 'new_v', 'new_v_w_mlp1': 'new_v', 'new_v_w_mlp2': 'new_v', 'new_v_final_norm_g': 'new_v'}


def _forward(args):
    return _fwd_reference(*[args[k] for k in FWD_PARAMS])


def _output_shape():
    def fwd():
        inp = _fwd_setup_inputs(0)
        return _fwd_reference(*[inp[k] for k in FWD_PARAMS])
    out = _jax.eval_shape(fwd)
    return out.shape, out.dtype

N_MICROBATCH = 1
ADAM_LR = 0.001
ADAM_B1 = 0.9
ADAM_B2 = 0.999
ADAM_EPS = 1e-08
ADAM_WD = 0.01
ADAM_STEP = 10
PER_EXAMPLE_BATCH_AXIS = {'x': 0, 'loss_target': 0}
SHARED_INPUTS = []
_WEIGHT_DTYPES = {'mix_norm_g': _jnp.float32, 'w_in': _jnp.float32, 'b_in': _jnp.float32, 'sinks': _jnp.float32, 'conv_w': _jnp.float32, 'conv_b': _jnp.float32, 'conv_ln_g': _jnp.float32, 'conv_ln_b': _jnp.float32, 'w_attn_proj': _jnp.float32, 'w_conv_proj': _jnp.float32, 'b_conv_proj': _jnp.float32, 'w_out': _jnp.float32, 'mlp_norm_g': _jnp.float32, 'w_mlp1': _jnp.float32, 'w_mlp2': _jnp.float32, 'final_norm_g': _jnp.float32}
MOMENT_SCALE = {'mix_norm_g': 1.518794e-01, 'w_in': 7.622861e-02, 'b_in': 1.468251e-01, 'sinks': 8.123164e-02, 'conv_w': 1.515888e-01, 'conv_b': 3.307332e-01, 'conv_ln_g': 1.824792e-01, 'conv_ln_b': 1.717914e-01, 'w_attn_proj': 5.805001e-02, 'w_conv_proj': 1.079951e-01, 'b_conv_proj': 2.327085e-01, 'w_out': 1.222370e-01, 'mlp_norm_g': 2.885128e-01, 'w_mlp1': 1.460712e-01, 'w_mlp2': 2.820686e-01, 'final_norm_g': 1.305220e+02}


def _to_microbatches(a, axis):
    t = _jnp.moveaxis(a, axis, 0)
    t = t.reshape((N_MICROBATCH, t.shape[0] // N_MICROBATCH) + t.shape[1:])
    return _jnp.moveaxis(t, 1, axis + 1)


def setup_inputs(seed: int = 0) -> dict:
    inp = _fwd_setup_inputs(seed)
    key = _jax.random.fold_in(_jax.random.key(seed), 7919)
    shape, _ = _output_shape()
    out = dict(inp)
    out["loss_target"] = _jax.random.normal(_jax.random.fold_in(key, 0), shape, _jnp.float32)
    for i, name in enumerate(TWIN_WEIGHTS):
        w = inp[name].astype(_jnp.float32)
        if MOMENT_SCALE is None:
            s = _jnp.sqrt(_jnp.mean(_jnp.square(w)) + 1e-30)
        else:
            s = MOMENT_SCALE[name]
        km, kv = _jax.random.split(_jax.random.fold_in(key, i + 1))
        out[name] = w
        out["m_" + name] = s * _jax.random.normal(km, w.shape, _jnp.float32)
        out["v_" + name] = (s * s) * _jax.random.uniform(kv, w.shape, _jnp.float32, 0.5, 1.5)
    if N_MICROBATCH > 1:
        for name, axis in PER_EXAMPLE_BATCH_AXIS.items():
            out[name] = _to_microbatches(out[name], axis)
    return {'x': out['x'], 'mix_norm_g': out['mix_norm_g'], 'w_in': out['w_in'], 'b_in': out['b_in'], 'sinks': out['sinks'], 'conv_w': out['conv_w'], 'conv_b': out['conv_b'], 'conv_ln_g': out['conv_ln_g'], 'conv_ln_b': out['conv_ln_b'], 'w_attn_proj': out['w_attn_proj'], 'w_conv_proj': out['w_conv_proj'], 'b_conv_proj': out['b_conv_proj'], 'w_out': out['w_out'], 'mlp_norm_g': out['mlp_norm_g'], 'w_mlp1': out['w_mlp1'], 'w_mlp2': out['w_mlp2'], 'final_norm_g': out['final_norm_g'], 'loss_target': out['loss_target'], 'm_mix_norm_g': out['m_mix_norm_g'], 'm_w_in': out['m_w_in'], 'm_b_in': out['m_b_in'], 'm_sinks': out['m_sinks'], 'm_conv_w': out['m_conv_w'], 'm_conv_b': out['m_conv_b'], 'm_conv_ln_g': out['m_conv_ln_g'], 'm_conv_ln_b': out['m_conv_ln_b'], 'm_w_attn_proj': out['m_w_attn_proj'], 'm_w_conv_proj': out['m_w_conv_proj'], 'm_b_conv_proj': out['m_b_conv_proj'], 'm_w_out': out['m_w_out'], 'm_mlp_norm_g': out['m_mlp_norm_g'], 'm_w_mlp1': out['m_w_mlp1'], 'm_w_mlp2': out['m_w_mlp2'], 'm_final_norm_g': out['m_final_norm_g'], 'v_mix_norm_g': out['v_mix_norm_g'], 'v_w_in': out['v_w_in'], 'v_b_in': out['v_b_in'], 'v_sinks': out['v_sinks'], 'v_conv_w': out['v_conv_w'], 'v_conv_b': out['v_conv_b'], 'v_conv_ln_g': out['v_conv_ln_g'], 'v_conv_ln_b': out['v_conv_ln_b'], 'v_w_attn_proj': out['v_w_attn_proj'], 'v_w_conv_proj': out['v_w_conv_proj'], 'v_b_conv_proj': out['v_b_conv_proj'], 'v_w_out': out['v_w_out'], 'v_mlp_norm_g': out['v_mlp_norm_g'], 'v_w_mlp1': out['v_w_mlp1'], 'v_w_mlp2': out['v_w_mlp2'], 'v_final_norm_g': out['v_final_norm_g']}


def _loss(weights, diff, rest, loss_target):
    with _jax.named_scope("forward"):
        args = {**rest, TWIN_DIFF_INPUT: diff, **{k: w.astype(_WEIGHT_DTYPES[k]) for k, w in weights.items()}}
        y = _forward(args)
    with _jax.named_scope("loss_head"):
        err = _jnp.square(y.astype(_jnp.float32) - loss_target)
        return 0.5 * _jnp.sum(_jnp.mean(err, axis=-1)) if err.ndim else 0.5 * err


def _adamw(w, g, m, v):
    m = ADAM_B1 * m + (1.0 - ADAM_B1) * g
    v = ADAM_B2 * v + (1.0 - ADAM_B2) * _jnp.square(g)
    m_hat = m / (1.0 - ADAM_B1 ** ADAM_STEP)
    v_hat = v / (1.0 - ADAM_B2 ** ADAM_STEP)
    delta = -ADAM_LR * (m_hat / (_jnp.sqrt(v_hat) + ADAM_EPS) + ADAM_WD * w)
    return delta, m, v


def reference(x, mix_norm_g, w_in, b_in, sinks, conv_w, conv_b, conv_ln_g, conv_ln_b, w_attn_proj, w_conv_proj, b_conv_proj, w_out, mlp_norm_g, w_mlp1, w_mlp2, final_norm_g, loss_target, m_mix_norm_g, m_w_in, m_b_in, m_sinks, m_conv_w, m_conv_b, m_conv_ln_g, m_conv_ln_b, m_w_attn_proj, m_w_conv_proj, m_b_conv_proj, m_w_out, m_mlp_norm_g, m_w_mlp1, m_w_mlp2, m_final_norm_g, v_mix_norm_g, v_w_in, v_b_in, v_sinks, v_conv_w, v_conv_b, v_conv_ln_g, v_conv_ln_b, v_w_attn_proj, v_w_conv_proj, v_b_conv_proj, v_w_out, v_mlp_norm_g, v_w_mlp1, v_w_mlp2, v_final_norm_g):
    given = dict(x=x, mix_norm_g=mix_norm_g, w_in=w_in, b_in=b_in, sinks=sinks, conv_w=conv_w, conv_b=conv_b, conv_ln_g=conv_ln_g, conv_ln_b=conv_ln_b, w_attn_proj=w_attn_proj, w_conv_proj=w_conv_proj, b_conv_proj=b_conv_proj, w_out=w_out, mlp_norm_g=mlp_norm_g, w_mlp1=w_mlp1, w_mlp2=w_mlp2, final_norm_g=final_norm_g, loss_target=loss_target, m_mix_norm_g=m_mix_norm_g, m_w_in=m_w_in, m_b_in=m_b_in, m_sinks=m_sinks, m_conv_w=m_conv_w, m_conv_b=m_conv_b, m_conv_ln_g=m_conv_ln_g, m_conv_ln_b=m_conv_ln_b, m_w_attn_proj=m_w_attn_proj, m_w_conv_proj=m_w_conv_proj, m_b_conv_proj=m_b_conv_proj, m_w_out=m_w_out, m_mlp_norm_g=m_mlp_norm_g, m_w_mlp1=m_w_mlp1, m_w_mlp2=m_w_mlp2, m_final_norm_g=m_final_norm_g, v_mix_norm_g=v_mix_norm_g, v_w_in=v_w_in, v_b_in=v_b_in, v_sinks=v_sinks, v_conv_w=v_conv_w, v_conv_b=v_conv_b, v_conv_ln_g=v_conv_ln_g, v_conv_ln_b=v_conv_ln_b, v_w_attn_proj=v_w_attn_proj, v_w_conv_proj=v_w_conv_proj, v_b_conv_proj=v_b_conv_proj, v_w_out=v_w_out, v_mlp_norm_g=v_mlp_norm_g, v_w_mlp1=v_w_mlp1, v_w_mlp2=v_w_mlp2, v_final_norm_g=v_final_norm_g)
    weights = {n: given[n] for n in TWIN_WEIGHTS}
    shared = {n: given[n] for n in SHARED_INPUTS}
    per_example = {n: given[n] for n in ['x']}
    grad_fn = _jax.value_and_grad(_loss, argnums=(0, 1))

    def one_microbatch(ex, loss_target):
        ex = dict(ex)
        diff = ex.pop(TWIN_DIFF_INPUT)
        return grad_fn(weights, diff, {**shared, **ex}, loss_target)

    if N_MICROBATCH == 1:
        loss, (grad_w, grad_x) = one_microbatch(per_example, given["loss_target"])
    else:
        def body(carry, xs):
            loss_sum, grad_sum = carry
            l_k, (gw_k, gx_k) = one_microbatch(xs[0], xs[1])
            with _jax.named_scope("update"):
                return (loss_sum + l_k, _jax.tree.map(_jnp.add, grad_sum, gw_k)), gx_k

        init = (_jnp.zeros((), _jnp.float32), _jax.tree.map(_jnp.zeros_like, weights))
        (loss, grad_w), grad_x = _jax.lax.scan(body, init, (per_example, given["loss_target"]))
    with _jax.named_scope("update"):
        delta_w, new_m, new_v = {}, {}, {}
        for n in TWIN_WEIGHTS:
            delta_w[n], new_m[n], new_v[n] = _adamw(weights[n], grad_w[n], given["m_" + n], given["v_" + n])
    return (loss, grad_x, *[grad_w[n] for n in TWIN_WEIGHTS], *[delta_w[n] for n in TWIN_WEIGHTS],
            *[new_m[n] for n in TWIN_WEIGHTS], *[new_v[n] for n in TWIN_WEIGHTS])
```

```python
import functools
import math

import jax
import jax.numpy as jnp
import numpy as np
from jax import lax
from jax.experimental import pallas as pl
from jax.experimental.pallas import tpu as pltpu

F32 = jnp.float32
CDT = jnp.bfloat16

D = 1024
DEPTH = 2
N_Q = 8
HEAD_DIM = 64
ATTN_W = 512
KV_W = 128
BLOCK = 128
CONV_C = 512
CONV_K = 31
D_FF = 4096
IN_W = 3840
QKV_W = ATTN_W + 2 * KV_W
REST_W = IN_W - QKV_W
EPS = 1e-6
NEG = -1e30
SCALE = 1.0 / math.sqrt(HEAD_DIM)
SLOPES = [float(2.0 ** (-8.0 * (h + 1) / N_Q)) for h in range(N_Q)]
HALO = 32

ADAM_LR = 0.001
ADAM_B1 = 0.9
ADAM_B2 = 0.999
ADAM_EPS = 1e-08
ADAM_WD = 0.01
ADAM_STEP = 10

VMEM_LIMIT = 56 * 1024 * 1024


def _cp(*sem):
    return pltpu.CompilerParams(dimension_semantics=sem, vmem_limit_bytes=VMEM_LIMIT)


def _dot(a, b):
    return jnp.dot(a, b, preferred_element_type=F32)


def _dot_nt(a, b):
    return lax.dot_general(a, b, (((1,), (1,)), ((), ())), preferred_element_type=F32)


def _dot_tn(a, b):
    return lax.dot_general(a, b, (((0,), (0,)), ((), ())), preferred_element_type=F32)


def _sig(x):
    return 1.0 / (1.0 + jnp.exp(-x))


def _colsum(v):
    return jnp.sum(v, axis=0, keepdims=True)


def _const(shape):
    return pl.BlockSpec(shape, lambda *_: (0,) * len(shape))


def rms_inproj(x, g, w, b, *, tm=512):
    T = x.shape[0]

    def body(x_ref, g_ref, w_ref, b_ref, qkv_ref, rest_ref):
        xv = x_ref[...]
        r = lax.rsqrt(jnp.mean(xv * xv, axis=-1, keepdims=True) + EPS)
        h = (xv * r * g_ref[...]).astype(CDT)
        qkv_ref[...] = (_dot(h, w_ref[:, 0:QKV_W]) + b_ref[:, 0:QKV_W]).astype(qkv_ref.dtype)
        for j in range(REST_W // D):
            c0 = QKV_W + D * j
            rest_ref[:, D * j:D * (j + 1)] = _dot(h, w_ref[:, c0:c0 + D]) + b_ref[:, c0:c0 + D]

    return pl.pallas_call(
        body, name="rms_inproj", grid=(T // tm,),
        in_specs=[pl.BlockSpec((tm, D), lambda i: (i, 0)), _const((1, D)), _const((D, IN_W)), _const((1, IN_W))],
        out_specs=[pl.BlockSpec((tm, QKV_W), lambda i: (i, 0)), pl.BlockSpec((tm, REST_W), lambda i: (i, 0))],
        out_shape=[jax.ShapeDtypeStruct((T, QKV_W), CDT), jax.ShapeDtypeStruct((T, REST_W), F32)],
        compiler_params=_cp("parallel"),
    )(x, g, w, b)


def _lane_halves(shape):
    lane = lax.broadcasted_iota(jnp.int32, shape, 1)
    return lane < HEAD_DIM, lane >= HEAD_DIM


def _swap_halves(v):
    return pltpu.roll(v.astype(F32), HEAD_DIM, axis=1).astype(v.dtype)


def _attn_masks(first):
    row = lax.broadcasted_iota(jnp.int32, (BLOCK, 2 * BLOCK), 0)
    col = lax.broadcasted_iota(jnp.int32, (BLOCK, 2 * BLOCK), 1)
    dist = row + BLOCK - col
    valid = (dist >= 0) & (dist < BLOCK) & ((col >= BLOCK) | jnp.logical_not(first))
    return valid, dist.astype(F32)


def _qkv_specs(tq):
    nb = tq // BLOCK
    return [
        pl.BlockSpec((tq, ATTN_W), lambda i: (i, 0)),
        pl.BlockSpec((BLOCK, KV_W), lambda i: (jnp.maximum(i * nb - 1, 0), ATTN_W // KV_W)),
        pl.BlockSpec((tq, KV_W), lambda i: (i, ATTN_W // KV_W)),
        pl.BlockSpec((BLOCK, KV_W), lambda i: (jnp.maximum(i * nb - 1, 0), ATTN_W // KV_W + 1)),
        pl.BlockSpec((tq, KV_W), lambda i: (i, ATTN_W // KV_W + 1)),
    ]


def attn_fwd(qkv, sinks, *, tq=512):
    T = qkv.shape[0]
    nb = tq // BLOCK

    def body(sink_ref, q_ref, kp_ref, kc_ref, vp_ref, vc_ref, o_ref, lse_ref, kext, vext):
        i = pl.program_id(0)
        kext[0:BLOCK, :] = kp_ref[...]
        kext[BLOCK:, :] = kc_ref[...]
        vext[0:BLOCK, :] = vp_ref[...]
        vext[BLOCK:, :] = vc_ref[...]
        lo_q, hi_q = _lane_halves((BLOCK, 2 * HEAD_DIM))
        lo_k, hi_k = _lane_halves((2 * BLOCK, 2 * HEAD_DIM))
        lane_l = lax.broadcasted_iota(jnp.int32, (BLOCK, 128), 1)

        def blk(b, carry):
            r0 = pl.multiple_of(b * BLOCK, BLOCK)
            valid, distf = _attn_masks(jnp.logical_and(i == 0, b == 0))
            kc = kext[pl.ds(r0, 2 * BLOCK), :]
            vc = vext[pl.ds(r0, 2 * BLOCK), :]
            ks = (kc, _swap_halves(kc))
            vsw = _swap_halves(vc)
            zero = jnp.zeros_like(vc)
            lse_t = jnp.zeros((BLOCK, 128), F32)
            for pair in range(N_Q // 2):
                qp = q_ref[pl.ds(r0, BLOCK), pair * 128:(pair + 1) * 128]
                kv = pair // 2
                o_pair = jnp.zeros((BLOCK, 128), F32)
                for half in range(2):
                    h = 2 * pair + half
                    qm = jnp.where(lo_q if half == 0 else hi_q, qp, jnp.zeros_like(qp))
                    same = kv == half
                    s = _dot_nt(qm, ks[0] if same else ks[1]) * SCALE - SLOPES[h] * distf
                    s = jnp.where(valid, s, NEG)
                    sink = sink_ref[h]
                    m = jnp.maximum(jnp.max(s, axis=-1, keepdims=True), sink)
                    p = jnp.exp(s - m)
                    denom = jnp.sum(p, axis=-1, keepdims=True) + jnp.exp(sink - m)
                    p = p / denom
                    vv = jnp.where(lo_k if half == 0 else hi_k, vc if same else vsw, zero)
                    o_pair = o_pair + _dot(p.astype(CDT), vv)
                    lse_t = jnp.where(lane_l == h, m + jnp.log(denom), lse_t)
                o_ref[pl.ds(r0, BLOCK), pair * 128:(pair + 1) * 128] = o_pair.astype(o_ref.dtype)
            lse_ref[pl.ds(r0, BLOCK), :] = lse_t
            return carry

        lax.fori_loop(0, nb, blk, 0)

    return pl.pallas_call(
        body, name="attn_fwd", grid=(T // tq,),
        in_specs=[pl.BlockSpec(memory_space=pltpu.SMEM)] + _qkv_specs(tq),
        out_specs=[pl.BlockSpec((tq, ATTN_W), lambda i: (i, 0)), pl.BlockSpec((tq, 128), lambda i: (i, 0))],
        out_shape=[jax.ShapeDtypeStruct((T, ATTN_W), CDT), jax.ShapeDtypeStruct((T, 128), F32)],
        scratch_shapes=[pltpu.VMEM((tq + BLOCK, KV_W), CDT), pltpu.VMEM((tq + BLOCK, KV_W), CDT)],
        compiler_params=_cp("parallel"),
    )(sinks, qkv, qkv, qkv, qkv, qkv)


def _halo_before(tm, width, col):
    return pl.BlockSpec((HALO, width), lambda i: (jnp.maximum(i * (tm // HALO) - 1, 0), col))


def _fill_u0(ext, a_ref, b_ref, ha_ref, hb_ref, first):
    hu = ha_ref[...] * _sig(hb_ref[...])
    ext[0:HALO, :] = jnp.where(first, jnp.zeros_like(hu), hu)
    ext[HALO:, :] = a_ref[...] * _sig(b_ref[...])


def _conv_rows(ext, w_ref, r0, rc):
    acc = jnp.zeros((rc, CONV_C), F32)
    for k in range(CONV_K):
        off = r0 + HALO - (CONV_K - 1) + k
        acc = acc + w_ref[k:k + 1, :] * ext[off:off + rc, :]
    return acc


def _layer_norm(u1, g, b):
    mu = jnp.mean(u1, axis=-1, keepdims=True)
    xc = u1 - mu
    rstd = lax.rsqrt(jnp.mean(xc * xc, axis=-1, keepdims=True) + EPS)
    n = xc * rstd
    return n, rstd, n * g + b


CONV_RC = 32


def conv_fwd(rest, cw, cb, lg, lb, *, tm=512):
    T = rest.shape[0]

    def body(a_ref, b_ref, ha_ref, hb_ref, w_ref, cb_ref, lg_ref, lb_ref, o_ref, ext):
        _fill_u0(ext, a_ref, b_ref, ha_ref, hb_ref, pl.program_id(0) == 0)
        for c in range(tm // CONV_RC):
            r0 = c * CONV_RC
            u1 = _conv_rows(ext, w_ref, r0, CONV_RC) + cb_ref[...]
            _, _, u2 = _layer_norm(u1, lg_ref[...], lb_ref[...])
            o_ref[r0:r0 + CONV_RC, :] = (u2 * _sig(u2)).astype(o_ref.dtype)

    return pl.pallas_call(
        body, name="conv_fwd", grid=(T // tm,),
        in_specs=[pl.BlockSpec((tm, CONV_C), lambda i: (i, 0)), pl.BlockSpec((tm, CONV_C), lambda i: (i, 1)),
                  _halo_before(tm, CONV_C, 0), _halo_before(tm, CONV_C, 1),
                  _const((CONV_K, CONV_C)), _const((1, CONV_C)), _const((1, CONV_C)), _const((1, CONV_C))],
        out_specs=pl.BlockSpec((tm, CONV_C), lambda i: (i, 0)),
        out_shape=jax.ShapeDtypeStruct((T, CONV_C), CDT),
        scratch_shapes=[pltpu.VMEM((tm + HALO, CONV_C), F32)],
        compiler_params=_cp("parallel"),
    )(rest, rest, rest, rest, cw, cb, lg, lb)


def merge_out(x, attn, u3, rest, wa, wc, bc, wo, *, tm=512):
    T = x.shape[0]

    def body(x_ref, at_ref, u_ref, ga_ref, gc_ref, wa_ref, wc_ref, bc_ref, wo_ref, o_ref):
        br_a = _dot(at_ref[...], wa_ref[...])
        br_c = _dot(u_ref[...], wc_ref[...]) + bc_ref[...]
        merged = _sig(ga_ref[...]) * br_a + _sig(gc_ref[...]) * br_c
        o_ref[...] = x_ref[...] + _dot(merged.astype(CDT), wo_ref[...])

    return pl.pallas_call(
        body, name="merge_out", grid=(T // tm,),
        in_specs=[pl.BlockSpec((tm, D), lambda i: (i, 0)), pl.BlockSpec((tm, ATTN_W), lambda i: (i, 0)),
                  pl.BlockSpec((tm, CONV_C), lambda i: (i, 0)),
                  pl.BlockSpec((tm, D), lambda i: (i, 1)), pl.BlockSpec((tm, D), lambda i: (i, 2)),
                  _const((ATTN_W, D)), _const((CONV_C, D)), _const((1, D)), _const((D, D))],
        out_specs=pl.BlockSpec((tm, D), lambda i: (i, 0)),
        out_shape=jax.ShapeDtypeStruct((T, D), F32),
        compiler_params=_cp("parallel"),
    )(x, attn, u3, rest, rest, wa, wc, bc, wo)


def mlp_fwd(x, g, w1, w2, *, tm=512, tf=1024):
    T = x.shape[0]
    nf = D_FF // tf

    def body(x_ref, g_ref, w1_ref, w2_ref, o_ref, pre_ref, h_s, acc_s):
        f = pl.program_id(1)

        @pl.when(f == 0)
        def _():
            xv = x_ref[...]
            r = lax.rsqrt(jnp.mean(xv * xv, axis=-1, keepdims=True) + EPS)
            h_s[...] = (xv * r * g_ref[...]).astype(CDT)
            acc_s[...] = jnp.zeros_like(acc_s)

        pre = _dot(h_s[...], w1_ref[...])
        pre_ref[...] = pre
        a = jnp.square(jnp.maximum(pre, 0.0))
        acc_s[...] += _dot(a.astype(CDT), w2_ref[...])

        @pl.when(f == nf - 1)
        def _():
            o_ref[...] = x_ref[...] + acc_s[...]

    return pl.pallas_call(
        body, name="mlp_fwd", grid=(T // tm, nf),
        in_specs=[pl.BlockSpec((tm, D), lambda i, f: (i, 0)), _const((1, D)),
                  pl.BlockSpec((D, tf), lambda i, f: (0, f)), pl.BlockSpec((tf, D), lambda i, f: (f, 0))],
        out_specs=[pl.BlockSpec((tm, D), lambda i, f: (i, 0)), pl.BlockSpec((tm, tf), lambda i, f: (i, f))],
        out_shape=[jax.ShapeDtypeStruct((T, D), F32), jax.ShapeDtypeStruct((T, D_FF), F32)],
        scratch_shapes=[pltpu.VMEM((tm, D), CDT), pltpu.VMEM((tm, D), F32)],
        compiler_params=_cp("parallel", "arbitrary"),
    )(x, g, w1, w2)


def _rms_bwd(xv, g, dh):
    r = lax.rsqrt(jnp.mean(xv * xv, axis=-1, keepdims=True) + EPS)
    xhat = xv * r
    dxh = dh * g
    dx = r * (dxh - xhat * jnp.mean(dxh * xhat, axis=-1, keepdims=True))
    return dx, dh * xhat


def loss_head(x, g, tgt, *, tm=512):
    T = x.shape[0]

    def body(x_ref, g_ref, t_ref, dx_ref, dg_ref, loss_ref):
        @pl.when(pl.program_id(0) == 0)
        def _():
            dg_ref[...] = jnp.zeros_like(dg_ref)
            loss_ref[...] = jnp.zeros_like(loss_ref)

        xv = x_ref[...]
        gv = g_ref[...]
        r = lax.rsqrt(jnp.mean(xv * xv, axis=-1, keepdims=True) + EPS)
        e = xv * r * gv - t_ref[...]
        loss_ref[...] += 0.5 * jnp.sum(jnp.mean(e * e, axis=-1, keepdims=True), axis=0, keepdims=True)
        dx, dg_rows = _rms_bwd(xv, gv, e * (1.0 / D))
        dx_ref[...] = dx
        dg_ref[...] += _colsum(dg_rows)

    return pl.pallas_call(
        body, name="loss_head", grid=(T // tm,),
        in_specs=[pl.BlockSpec((tm, D), lambda i: (i, 0)), _const((1, D)), pl.BlockSpec((tm, D), lambda i: (i, 0))],
        out_specs=[pl.BlockSpec((tm, D), lambda i: (i, 0)), _const((1, D)), _const((1, 128))],
        out_shape=[jax.ShapeDtypeStruct((T, D), F32), jax.ShapeDtypeStruct((1, D), F32),
                   jax.ShapeDtypeStruct((1, 128), F32)],
        compiler_params=_cp("arbitrary"),
    )(x, g, tgt)


def mlp_bwd(dy, x, g, pre, w1, w2, *, tm=512, tf=1024):
    T = x.shape[0]
    nf = D_FF // tf

    def body(dy_ref, x_ref, g_ref, pre_ref, w1_ref, w2_ref, dx_ref, dg_ref, h_ref, a_ref, dpre_ref, dyb_s, acc_s):
        i, f = pl.program_id(0), pl.program_id(1)

        @pl.when(jnp.logical_and(i == 0, f == 0))
        def _():
            dg_ref[...] = jnp.zeros_like(dg_ref)

        @pl.when(f == 0)
        def _():
            dyb_s[...] = dy_ref[...].astype(CDT)
            acc_s[...] = jnp.zeros_like(acc_s)

        pre = pre_ref[...]
        rl = jnp.maximum(pre, 0.0)
        a_ref[...] = (rl * rl).astype(CDT)
        da = _dot_nt(dyb_s[...], w2_ref[...])
        dpre = (da * (2.0 * rl)).astype(CDT)
        dpre_ref[...] = dpre
        acc_s[...] += _dot_nt(dpre, w1_ref[...])

        @pl.when(f == nf - 1)
        def _():
            xv = x_ref[...]
            gv = g_ref[...]
            dxn, dg_rows = _rms_bwd(xv, gv, acc_s[...])
            dx_ref[...] = dy_ref[...] + dxn
            dg_ref[...] += _colsum(dg_rows)
            r = lax.rsqrt(jnp.mean(xv * xv, axis=-1, keepdims=True) + EPS)
            h_ref[...] = (xv * r * gv).astype(CDT)

    row = lambda i, f: (i, 0)
    return pl.pallas_call(
        body, name="mlp_bwd", grid=(T // tm, nf),
        in_specs=[pl.BlockSpec((tm, D), row), pl.BlockSpec((tm, D), row), _const((1, D)),
                  pl.BlockSpec((tm, tf), lambda i, f: (i, f)),
                  pl.BlockSpec((D, tf), lambda i, f: (0, f)), pl.BlockSpec((tf, D), lambda i, f: (f, 0))],
        out_specs=[pl.BlockSpec((tm, D), row), _const((1, D)), pl.BlockSpec((tm, D), row),
                   pl.BlockSpec((tm, tf), lambda i, f: (i, f)), pl.BlockSpec((tm, tf), lambda i, f: (i, f))],
        out_shape=[jax.ShapeDtypeStruct((T, D), F32), jax.ShapeDtypeStruct((1, D), F32),
                   jax.ShapeDtypeStruct((T, D), CDT), jax.ShapeDtypeStruct((T, D_FF), CDT),
                   jax.ShapeDtypeStruct((T, D_FF), CDT)],
        scratch_shapes=[pltpu.VMEM((tm, D), CDT), pltpu.VMEM((tm, D), F32)],
        compiler_params=_cp("arbitrary", "arbitrary"),
    )(dy, x, g, pre, w1, w2)


def tn_matmul(a, b, *, tm, tn, tk=512, name):
    T, M = a.shape
    N = b.shape[1]
    nk = T // tk

    def body(a_ref, b_ref, o_ref):
        @pl.when(pl.program_id(2) == 0)
        def _():
            o_ref[...] = jnp.zeros_like(o_ref)

        o_ref[...] += _dot_tn(a_ref[...].astype(CDT), b_ref[...].astype(CDT))

    return pl.pallas_call(
        body, name=name, grid=(M // tm, N // tn, nk),
        in_specs=[pl.BlockSpec((tk, tm), lambda i, j, k: (k, i)), pl.BlockSpec((tk, tn), lambda i, j, k: (k, j))],
        out_specs=pl.BlockSpec((tm, tn), lambda i, j, k: (i, j)),
        out_shape=jax.ShapeDtypeStruct((M, N), F32),
        compiler_params=_cp("parallel", "parallel", "arbitrary"),
    )(a, b)


def merge_bwd(dx1, attn, u3, rest, wa, wc, bc, wo, *, tm=256):
    T = dx1.shape[0]

    def body(dx_ref, at_ref, u_ref, ga_ref, gc_ref, wa_ref, wc_ref, bc_ref, wo_ref,
             mg_ref, dba_ref, dbc_ref, dat_ref, du_ref, dgate_ref, dbias_ref):
        @pl.when(pl.program_id(0) == 0)
        def _():
            dbias_ref[...] = jnp.zeros_like(dbias_ref)

        br_a = _dot(at_ref[...], wa_ref[...])
        br_c = _dot(u_ref[...], wc_ref[...]) + bc_ref[...]
        sa = _sig(ga_ref[...])
        sc = _sig(gc_ref[...])
        mg_ref[...] = (sa * br_a + sc * br_c).astype(CDT)
        dm = _dot_nt(dx_ref[...].astype(CDT), wo_ref[...])
        dba = dm * sa
        dbc = dm * sc
        dgate_ref[:, 0:D] = dm * br_a * sa * (1.0 - sa)
        dgate_ref[:, D:2 * D] = dm * br_c * sc * (1.0 - sc)
        dbias_ref[...] += _colsum(dbc)
        dba_b = dba.astype(CDT)
        dbc_b = dbc.astype(CDT)
        dba_ref[...] = dba_b
        dbc_ref[...] = dbc_b
        dat_ref[...] = _dot_nt(dba_b, wa_ref[...]).astype(CDT)
        du_ref[...] = _dot_nt(dbc_b, wc_ref[...])

    row = lambda i: (i, 0)
    return pl.pallas_call(
        body, name="merge_bwd", grid=(T // tm,),
        in_specs=[pl.BlockSpec((tm, D), row), pl.BlockSpec((tm, ATTN_W), row), pl.BlockSpec((tm, CONV_C), row),
                  pl.BlockSpec((tm, D), lambda i: (i, 1)), pl.BlockSpec((tm, D), lambda i: (i, 2)),
                  _const((ATTN_W, D)), _const((CONV_C, D)), _const((1, D)), _const((D, D))],
        out_specs=[pl.BlockSpec((tm, D), row), pl.BlockSpec((tm, D), row), pl.BlockSpec((tm, D), row),
                   pl.BlockSpec((tm, ATTN_W), row), pl.BlockSpec((tm, CONV_C), row),
                   pl.BlockSpec((tm, 2 * D), row), _const((1, D))],
        out_shape=[jax.ShapeDtypeStruct((T, D), CDT), jax.ShapeDtypeStruct((T, D), CDT),
                   jax.ShapeDtypeStruct((T, D), CDT), jax.ShapeDtypeStruct((T, ATTN_W), CDT),
                   jax.ShapeDtypeStruct((T, CONV_C), F32), jax.ShapeDtypeStruct((T, 2 * D), F32),
                   jax.ShapeDtypeStruct((1, D), F32)],
        compiler_params=_cp("arbitrary"),
    )(dx1, attn, u3, rest, rest, wa, wc, bc, wo)


def conv_bwd_ln(du3, rest, cw, cb, lg, lb, *, tm=512):
    T = du3.shape[0]

    def body(du_ref, a_ref, b_ref, ha_ref, hb_ref, w_ref, cb_ref, lg_ref, lb_ref,
             du1_ref, dlg_ref, dlb_ref, dcb_ref, ext):
        @pl.when(pl.program_id(0) == 0)
        def _():
            dlg_ref[...] = jnp.zeros_like(dlg_ref)
            dlb_ref[...] = jnp.zeros_like(dlb_ref)
            dcb_ref[...] = jnp.zeros_like(dcb_ref)

        _fill_u0(ext, a_ref, b_ref, ha_ref, hb_ref, pl.program_id(0) == 0)
        dlg = jnp.zeros((1, CONV_C), F32)
        dlb = jnp.zeros((1, CONV_C), F32)
        dcb = jnp.zeros((1, CONV_C), F32)
        for c in range(tm // CONV_RC):
            r0 = c * CONV_RC
            u1 = _conv_rows(ext, w_ref, r0, CONV_RC) + cb_ref[...]
            n, rstd, u2 = _layer_norm(u1, lg_ref[...], lb_ref[...])
            s = _sig(u2)
            du2 = du_ref[r0:r0 + CONV_RC, :] * (s + u2 * s * (1.0 - s))
            dn = du2 * lg_ref[...]
            du1 = rstd * (dn - jnp.mean(dn, axis=-1, keepdims=True) - n * jnp.mean(dn * n, axis=-1, keepdims=True))
            du1_ref[r0:r0 + CONV_RC, :] = du1
            dlg = dlg + _colsum(du2 * n)
            dlb = dlb + _colsum(du2)
            dcb = dcb + _colsum(du1)
        dlg_ref[...] += dlg
        dlb_ref[...] += dlb
        dcb_ref[...] += dcb

    row = lambda i: (i, 0)
    vec = jax.ShapeDtypeStruct((1, CONV_C), F32)
    return pl.pallas_call(
        body, name="conv_bwd_ln", grid=(T // tm,),
        in_specs=[pl.BlockSpec((tm, CONV_C), row), pl.BlockSpec((tm, CONV_C), row), pl.BlockSpec((tm, CONV_C), lambda i: (i, 1)),
                  _halo_before(tm, CONV_C, 0), _halo_before(tm, CONV_C, 1),
                  _const((CONV_K, CONV_C)), _const((1, CONV_C)), _const((1, CONV_C)), _const((1, CONV_C))],
        out_specs=[pl.BlockSpec((tm, CONV_C), row), _const((1, CONV_C)), _const((1, CONV_C)), _const((1, CONV_C))],
        out_shape=[jax.ShapeDtypeStruct((T, CONV_C), F32), vec, vec, vec],
        scratch_shapes=[pltpu.VMEM((tm + HALO, CONV_C), F32)],
        compiler_params=_cp("arbitrary"),
    )(du3, rest, rest, rest, rest, cw, cb, lg, lb)


def conv_bwd_taps(du1, rest, cw, *, tm=512):
    T = du1.shape[0]
    nt = T // tm

    def body(d_ref, hd_ref, a_ref, b_ref, ha_ref, hb_ref, w_ref, dglu_ref, dw_ref, ext, dext, dwacc):
        i = pl.program_id(0)

        @pl.when(i == 0)
        def _():
            dwacc[...] = jnp.zeros_like(dwacc)

        _fill_u0(ext, a_ref, b_ref, ha_ref, hb_ref, i == 0)
        dext[0:tm, :] = d_ref[...]
        hd = hd_ref[...]
        dext[tm:, :] = jnp.where(i == nt - 1, jnp.zeros_like(hd), hd)
        for c in range(tm // CONV_RC):
            r0 = c * CONV_RC
            dv = dext[r0:r0 + CONV_RC, :]
            du0 = jnp.zeros((CONV_RC, CONV_C), F32)
            for k in range(CONV_K):
                fo = r0 + HALO - (CONV_K - 1) + k
                prod = dv * ext[fo:fo + CONV_RC, :]
                dwacc[8 * k:8 * k + 8, :] += jnp.sum(prod.reshape(CONV_RC // 8, 8, CONV_C), axis=0)
                bo = r0 + (CONV_K - 1) - k
                du0 = du0 + w_ref[k:k + 1, :] * dext[bo:bo + CONV_RC, :]
            av = a_ref[r0:r0 + CONV_RC, :]
            sb = _sig(b_ref[r0:r0 + CONV_RC, :])
            dglu_ref[r0:r0 + CONV_RC, 0:CONV_C] = du0 * sb
            dglu_ref[r0:r0 + CONV_RC, CONV_C:2 * CONV_C] = du0 * av * sb * (1.0 - sb)

        @pl.when(i == nt - 1)
        def _():
            dw_ref[...] = jnp.zeros_like(dw_ref)
            for k in range(CONV_K):
                dw_ref[k:k + 1, :] = _colsum(dwacc[8 * k:8 * k + 8, :])

    row = lambda i: (i, 0)
    return pl.pallas_call(
        body, name="conv_bwd_taps", grid=(nt,),
        in_specs=[pl.BlockSpec((tm, CONV_C), row),
                  pl.BlockSpec((HALO, CONV_C), lambda i: (jnp.minimum((i + 1) * (tm // HALO), T // HALO - 1), 0)),
                  pl.BlockSpec((tm, CONV_C), row), pl.BlockSpec((tm, CONV_C), lambda i: (i, 1)),
                  _halo_before(tm, CONV_C, 0), _halo_before(tm, CONV_C, 1), _const((CONV_K, CONV_C))],
        out_specs=[pl.BlockSpec((tm, 2 * CONV_C), row), _const((HALO, CONV_C))],
        out_shape=[jax.ShapeDtypeStruct((T, 2 * CONV_C), F32), jax.ShapeDtypeStruct((HALO, CONV_C), F32)],
        scratch_shapes=[pltpu.VMEM((tm + HALO, CONV_C), F32), pltpu.VMEM((tm + HALO, CONV_C), F32),
                        pltpu.VMEM((8 * CONV_K, CONV_C), F32)],
        compiler_params=_cp("arbitrary"),
    )(du1, du1, rest, rest, rest, rest, cw)


def attn_bwd(qkv, do, lse, sinks, *, tq=512):
    T = qkv.shape[0]
    nb = tq // BLOCK

    def body(sink_ref, q_ref, kp_ref, kc_ref, vp_ref, vc_ref, do_ref, lse_ref,
             dq_ref, dkv_ref, spill_ref, dsink_ref, kext, vext, dkext, dvext):
        i = pl.program_id(0)

        @pl.when(i == 0)
        def _():
            dsink_ref[...] = jnp.zeros_like(dsink_ref)

        kext[0:BLOCK, :] = kp_ref[...]
        kext[BLOCK:, :] = kc_ref[...]
        vext[0:BLOCK, :] = vp_ref[...]
        vext[BLOCK:, :] = vc_ref[...]
        dkext[...] = jnp.zeros_like(dkext)
        dvext[...] = jnp.zeros_like(dvext)
        lo_q, hi_q = _lane_halves((BLOCK, 2 * HEAD_DIM))
        lo_k, hi_k = _lane_halves((2 * BLOCK, 2 * HEAD_DIM))
        lane_l = lax.broadcasted_iota(jnp.int32, (BLOCK, 128), 1)

        def blk(b, dsink):
            r0 = pl.multiple_of(b * BLOCK, BLOCK)
            valid, distf = _attn_masks(jnp.logical_and(i == 0, b == 0))
            kc = kext[pl.ds(r0, 2 * BLOCK), :]
            vc = vext[pl.ds(r0, 2 * BLOCK), :]
            ksw = _swap_halves(kc)
            vsw = _swap_halves(vc)
            zero = jnp.zeros_like(kc)
            lse_t = lse_ref[pl.ds(r0, BLOCK), :]
            dk_same = jnp.zeros((2 * BLOCK, KV_W), F32)
            dk_swap = jnp.zeros((2 * BLOCK, KV_W), F32)
            dv_same = jnp.zeros((2 * BLOCK, KV_W), F32)
            dv_swap = jnp.zeros((2 * BLOCK, KV_W), F32)
            for pair in range(N_Q // 2):
                qp = q_ref[pl.ds(r0, BLOCK), pair * 128:(pair + 1) * 128]
                dop = do_ref[pl.ds(r0, BLOCK), pair * 128:(pair + 1) * 128]
                kv = pair // 2
                dq_pair = jnp.zeros((BLOCK, 128), F32)
                for half in range(2):
                    h = 2 * pair + half
                    mq = lo_q if half == 0 else hi_q
                    mk = lo_k if half == 0 else hi_k
                    same = kv == half
                    qm = jnp.where(mq, qp, jnp.zeros_like(qp))
                    dom = jnp.where(mq, dop, jnp.zeros_like(dop))
                    kk = kc if same else ksw
                    vv = vc if same else vsw
                    s = _dot_nt(qm, kk) * SCALE - SLOPES[h] * distf
                    s = jnp.where(valid, s, NEG)
                    lse_h = jnp.sum(jnp.where(lane_l == h, lse_t, 0.0), axis=-1, keepdims=True)
                    p = jnp.exp(s - lse_h)
                    dp = _dot_nt(dom, vv)
                    dd = jnp.sum(p * dp, axis=-1, keepdims=True)
                    ds = (p * (dp - dd)).astype(CDT)
                    dq_pair = dq_pair + _dot(ds, jnp.where(mk, kk, zero))
                    dk_h = _dot_tn(ds, qm)
                    dv_h = _dot_tn(p.astype(CDT), dom)
                    if same:
                        dk_same, dv_same = dk_same + dk_h, dv_same + dv_h
                    else:
                        dk_swap, dv_swap = dk_swap + dk_h, dv_swap + dv_h
                    psink = jnp.exp(sink_ref[h] - lse_h)
                    dsink = dsink - jnp.where(lane_l[0:1, :] == h, jnp.sum(psink * dd, axis=0, keepdims=True), 0.0)
                dq_ref[pl.ds(r0, BLOCK), pair * 128:(pair + 1) * 128] = dq_pair * SCALE
            dkext[pl.ds(r0, 2 * BLOCK), :] += (dk_same + _swap_halves(dk_swap)) * SCALE
            dvext[pl.ds(r0, 2 * BLOCK), :] += dv_same + _swap_halves(dv_swap)
            return dsink

        dsink_ref[...] += lax.fori_loop(0, nb, blk, jnp.zeros((1, 128), F32))
        dkv_ref[:, 0:KV_W] = dkext[BLOCK:, :]
        dkv_ref[:, KV_W:2 * KV_W] = dvext[BLOCK:, :]
        spill_ref[:, 0:KV_W] = dkext[0:BLOCK, :]
        spill_ref[:, KV_W:2 * KV_W] = dvext[0:BLOCK, :]

    row = lambda i: (i, 0)
    return pl.pallas_call(
        body, name="attn_bwd", grid=(T // tq,),
        in_specs=[pl.BlockSpec(memory_space=pltpu.SMEM)] + _qkv_specs(tq)
        + [pl.BlockSpec((tq, ATTN_W), row), pl.BlockSpec((tq, 128), row)],
        out_specs=[pl.BlockSpec((tq, ATTN_W), row), pl.BlockSpec((tq, 2 * KV_W), row),
                   pl.BlockSpec((BLOCK, 2 * KV_W), row), _const((1, 128))],
        out_shape=[jax.ShapeDtypeStruct((T, ATTN_W), F32), jax.ShapeDtypeStruct((T, 2 * KV_W), F32),
                   jax.ShapeDtypeStruct((T // tq * BLOCK, 2 * KV_W), F32), jax.ShapeDtypeStruct((1, 128), F32)],
        scratch_shapes=[pltpu.VMEM((tq + BLOCK, KV_W), CDT), pltpu.VMEM((tq + BLOCK, KV_W), CDT),
                        pltpu.VMEM((tq + BLOCK, KV_W), F32), pltpu.VMEM((tq + BLOCK, KV_W), F32)],
        compiler_params=_cp("arbitrary"),
    )(sinks, qkv, qkv, qkv, qkv, qkv, do, lse)


def inproj_bwd(dres, x, g, w, dq, dkv, spill, dglu, dgate, *, tm=512):
    T = x.shape[0]
    nt = T // tm

    def body(dr_ref, x_ref, g_ref, w_ref, dq_ref, dkv_ref, sp_ref, dglu_ref, dgate_ref,
             dx_ref, dp_ref, h_ref, dg_ref, db_ref):
        i = pl.program_id(0)

        @pl.when(i == 0)
        def _():
            dg_ref[...] = jnp.zeros_like(dg_ref)
            db_ref[...] = jnp.zeros_like(db_ref)

        sp = sp_ref[...]
        sp = jnp.where(i == nt - 1, jnp.zeros_like(sp), sp)
        pieces = ((0, ATTN_W, dq_ref), (QKV_W, 2 * CONV_C, dglu_ref), (QKV_W + 2 * CONV_C, 2 * D, dgate_ref))
        for c0, wd, ref in pieces:
            v = ref[...]
            db_ref[:, c0:c0 + wd] += _colsum(v)
            dp_ref[:, c0:c0 + wd] = v.astype(CDT)
        dkv = dkv_ref[...]
        db_ref[:, ATTN_W:QKV_W] += _colsum(dkv) + _colsum(sp)
        dp_ref[0:tm - BLOCK, ATTN_W:QKV_W] = dkv[0:tm - BLOCK, :].astype(CDT)
        dp_ref[tm - BLOCK:tm, ATTN_W:QKV_W] = (dkv[tm - BLOCK:tm, :] + sp).astype(CDT)
        dh = _dot_nt(dp_ref[...], w_ref[...])
        xv = x_ref[...]
        gv = g_ref[...]
        dxn, dg_rows = _rms_bwd(xv, gv, dh)
        dx_ref[...] = dr_ref[...] + dxn
        dg_ref[...] += _colsum(dg_rows)
        r = lax.rsqrt(jnp.mean(xv * xv, axis=-1, keepdims=True) + EPS)
        h_ref[...] = (xv * r * gv).astype(CDT)

    row = lambda i: (i, 0)
    return pl.pallas_call(
        body, name="inproj_bwd", grid=(nt,),
        in_specs=[pl.BlockSpec((tm, D), row), pl.BlockSpec((tm, D), row), _const((1, D)), _const((D, IN_W)),
                  pl.BlockSpec((tm, ATTN_W), row), pl.BlockSpec((tm, 2 * KV_W), row),
                  pl.BlockSpec((BLOCK, 2 * KV_W), lambda i: (jnp.minimum(i + 1, nt - 1), 0)),
                  pl.BlockSpec((tm, 2 * CONV_C), row), pl.BlockSpec((tm, 2 * D), row)],
        out_specs=[pl.BlockSpec((tm, D), row), pl.BlockSpec((tm, IN_W), row), pl.BlockSpec((tm, D), row),
                   _const((1, D)), _const((1, IN_W))],
        out_shape=[jax.ShapeDtypeStruct((T, D), F32), jax.ShapeDtypeStruct((T, IN_W), CDT),
                   jax.ShapeDtypeStruct((T, D), CDT), jax.ShapeDtypeStruct((1, D), F32),
                   jax.ShapeDtypeStruct((1, IN_W), F32)],
        compiler_params=_cp("arbitrary"),
    )(dres, x, g, w, dq, dkv, spill, dglu, dgate)


ATTN_TILE = 256


def local_grads(x, tgt, p):
    saved = []
    for l in range(DEPTH):
        qkv, rest = rms_inproj(x, p["mix_norm_g"][l], p["w_in"][l], p["b_in"][l])
        attn, lse = attn_fwd(qkv, p["sinks"][l], tq=ATTN_TILE)
        u3 = conv_fwd(rest, p["conv_w"][l], p["conv_b"][l], p["conv_ln_g"][l], p["conv_ln_b"][l])
        x1 = merge_out(x, attn, u3, rest, p["w_attn_proj"][l], p["w_conv_proj"][l], p["b_conv_proj"][l], p["w_out"][l])
        x2, pre = mlp_fwd(x1, p["mlp_norm_g"][l], p["w_mlp1"][l], p["w_mlp2"][l])
        saved.append((x, qkv, rest, attn, lse, u3, x1, pre))
        x = x2
    dx, dgf, loss = loss_head(x, p["final_norm_g"], tgt)
    names = ("mix_norm_g", "w_in", "b_in", "sinks", "conv_w", "conv_b", "conv_ln_g", "conv_ln_b", "w_attn_proj",
             "w_conv_proj", "b_conv_proj", "w_out", "mlp_norm_g", "w_mlp1", "w_mlp2")
    grads = {n: [None] * DEPTH for n in names}
    grads["final_norm_g"] = dgf
    for l in reversed(range(DEPTH)):
        x0, qkv, rest, attn, lse, u3, x1, pre = saved[l]
        dx1, dg2, h2, a, dpre = mlp_bwd(dx, x1, p["mlp_norm_g"][l], pre, p["w_mlp1"][l], p["w_mlp2"][l])
        grads["mlp_norm_g"][l] = dg2
        grads["w_mlp1"][l] = tn_matmul(h2, dpre, tm=1024, tn=1024, name="tn_mlp1")
        grads["w_mlp2"][l] = tn_matmul(a, dx, tm=1024, tn=1024, name="tn_mlp2")
        merged, dba, dbc, dattn, du3, dgate, dbcp = merge_bwd(
            dx1, attn, u3, rest, p["w_attn_proj"][l], p["w_conv_proj"][l], p["b_conv_proj"][l], p["w_out"][l])
        grads["b_conv_proj"][l] = dbcp
        grads["w_out"][l] = tn_matmul(merged, dx1, tm=1024, tn=1024, name="tn_out")
        grads["w_attn_proj"][l] = tn_matmul(attn, dba, tm=512, tn=1024, name="tn_attn_proj")
        grads["w_conv_proj"][l] = tn_matmul(u3, dbc, tm=512, tn=1024, name="tn_conv_proj")
        du1, dlg, dlb, dcb = conv_bwd_ln(du3, rest, p["conv_w"][l], p["conv_b"][l], p["conv_ln_g"][l], p["conv_ln_b"][l])
        grads["conv_ln_g"][l], grads["conv_ln_b"][l], grads["conv_b"][l] = dlg, dlb, dcb
        dglu, dcw = conv_bwd_taps(du1, rest, p["conv_w"][l])
        grads["conv_w"][l] = dcw[0:CONV_K]
        dq, dkv, spill, dsink = attn_bwd(qkv, dattn, lse, p["sinks"][l], tq=ATTN_TILE)
        grads["sinks"][l] = dsink[0, 0:N_Q]
        dx, dproj, h, dg, db = inproj_bwd(dx1, x0, p["mix_norm_g"][l], p["w_in"][l], dq, dkv, spill, dglu, dgate,
                                          tm=ATTN_TILE)
        grads["mix_norm_g"][l], grads["b_in"][l] = dg, db
        grads["w_in"][l] = tn_matmul(h, dproj, tm=1024, tn=768, name="tn_in")
    return loss, dx, grads


MESH = pl.DeviceIdType.MESH
N_CHIPS = 4
N_DEV = 8
FLAT_W = 1024
FLAT_PARTS = (("w_in", 960), ("w_attn_proj", 128), ("w_conv_proj", 128), ("w_out", 256), ("w_mlp1", 1024), ("w_mlp2", 1024))
FLAT_ROWS = sum(r for _, r in FLAT_PARTS)
FLAT_TILE = 704


def _place():
    x, y, c = lax.axis_index("x"), lax.axis_index("y"), lax.axis_index("c")
    return x, y, c, 2 * x + y


def _peer_chips(x, y, j):
    return [((x, 1 - y), j ^ 1), ((1 - x, y), j ^ 2), ((1 - x, 1 - y), j ^ 3)]


def _any():
    return pl.BlockSpec(memory_space=pl.ANY)


def allgather_weights(wsh):
    shape = (N_CHIPS,) + wsh.shape

    def body(w_ref, out_ref, send_sems, recv_sems, local_sem):
        x, y, c, j = _place()
        peers = _peer_chips(x, y, j)

        def remote(src, dst, k, to):
            return pltpu.make_async_remote_copy(src_ref=src, dst_ref=dst, send_sem=send_sems.at[k],
                                                recv_sem=recv_sems.at[k], device_id=to, device_id_type=MESH)

        mine = pltpu.make_async_copy(w_ref, out_ref.at[j], local_sem)
        mine.start()
        first = [remote(w_ref.at[c], out_ref.at[j, c], k, (*chip, c)) for k, (chip, _) in enumerate(peers)]
        for cp in first:
            cp.start()
        passed = [remote(out_ref.at[pj, c], out_ref.at[pj, c], 3 + k, (x, y, 1 - c)) for k, (_, pj) in enumerate(peers)]
        for k, (_, pj) in enumerate(peers):
            remote(w_ref.at[c], out_ref.at[pj, c], k, (x, y, c)).wait_recv()
            passed[k].start()
        for k, (_, pj) in enumerate(peers):
            remote(w_ref.at[c], out_ref.at[pj, 1 - c], 3 + k, (x, y, c)).wait_recv()
        for cp in first + passed:
            cp.wait_send()
        mine.wait()

    return pl.pallas_call(
        body, name="allgather_weights", out_shape=jax.ShapeDtypeStruct(shape, wsh.dtype),
        in_specs=[_any()], out_specs=_any(),
        scratch_shapes=[pltpu.SemaphoreType.DMA((6,)), pltpu.SemaphoreType.DMA((6,)), pltpu.SemaphoreType.DMA],
    )(wsh)


def swap_other_layer(g):
    def body(g_ref, got_ref, send_sem, recv_sem):
        x, y, c, _ = _place()
        cp = pltpu.make_async_remote_copy(src_ref=g_ref.at[1 - c], dst_ref=got_ref, send_sem=send_sem, recv_sem=recv_sem,
                                          device_id=(x, y, 1 - c), device_id_type=MESH)
        cp.start()
        cp.wait()

    return pl.pallas_call(
        body, name="swap_other_layer", out_shape=jax.ShapeDtypeStruct(g.shape[1:], g.dtype),
        in_specs=[_any()], out_specs=_any(),
        scratch_shapes=[pltpu.SemaphoreType.DMA, pltpu.SemaphoreType.DMA],
    )(g)


def pair_sum(g, got, place):
    _, nj, R, W = g.shape

    def body(s_ref, g_ref, got_ref, pb_ref, own_ref):
        v = g_ref[...] + got_ref[...]
        pb_ref[...] = v.astype(pb_ref.dtype)

        @pl.when(pl.program_id(1) == s_ref[1])
        def _():
            own_ref[...] = v

    return pl.pallas_call(
        body, name="pair_sum",
        grid_spec=pltpu.PrefetchScalarGridSpec(
            num_scalar_prefetch=1, grid=(R // FLAT_TILE, nj),
            in_specs=[pl.BlockSpec((None, None, FLAT_TILE, W), lambda r, j, s: (s[0], j, r, 0)),
                      pl.BlockSpec((None, FLAT_TILE, W), lambda r, j, s: (j, r, 0))],
            out_specs=[pl.BlockSpec((None, FLAT_TILE, W), lambda r, j, s: (j, r, 0)),
                       pl.BlockSpec((FLAT_TILE, W), lambda r, j, s: (r, 0))]),
        out_shape=[jax.ShapeDtypeStruct((nj, R, W), CDT), jax.ShapeDtypeStruct((R, W), F32)],
        compiler_params=_cp("arbitrary", "arbitrary"),
    )(place, g, got)


def exchange_partials(pb):
    def body(pb_ref, got_ref, send_sems, recv_sems):
        x, y, c, j = _place()
        peers = _peer_chips(x, y, j)
        sends = [pltpu.make_async_remote_copy(src_ref=pb_ref.at[pj], dst_ref=got_ref.at[j], send_sem=send_sems.at[k],
                                              recv_sem=recv_sems.at[k], device_id=(*chip, c), device_id_type=MESH)
                 for k, (chip, pj) in enumerate(peers)]
        for cp in sends:
            cp.start()
        for k, (_, pj) in enumerate(peers):
            pltpu.make_async_remote_copy(src_ref=pb_ref.at[pj], dst_ref=got_ref.at[pj], send_sem=send_sems.at[k],
                                         recv_sem=recv_sems.at[k], device_id=(x, y, c), device_id_type=MESH).wait_recv()
        for cp in sends:
            cp.wait_send()

    return pl.pallas_call(
        body, name="exchange_partials", out_shape=jax.ShapeDtypeStruct(pb.shape, pb.dtype),
        in_specs=[_any()], out_specs=_any(),
        scratch_shapes=[pltpu.SemaphoreType.DMA((3,)), pltpu.SemaphoreType.DMA((3,))],
    )(pb)


def total_sum(own, got, place):
    R, W = own.shape

    def body(s_ref, own_ref, a_ref, b_ref, c_ref, o_ref):
        o_ref[...] = ((own_ref[...] + a_ref[...].astype(F32)) + b_ref[...].astype(F32)) + c_ref[...].astype(F32)

    def slab(k):
        return pl.BlockSpec((None, FLAT_TILE, W), lambda r, s: (s[1] ^ (k + 1), r, 0))

    return pl.pallas_call(
        body, name="total_sum",
        grid_spec=pltpu.PrefetchScalarGridSpec(
            num_scalar_prefetch=1, grid=(R // FLAT_TILE,),
            in_specs=[pl.BlockSpec((FLAT_TILE, W), lambda r, s: (r, 0)), slab(0), slab(1), slab(2)],
            out_specs=pl.BlockSpec((FLAT_TILE, W), lambda r, s: (r, 0))),
        out_shape=jax.ShapeDtypeStruct((R, W), F32),
        compiler_params=_cp("arbitrary"),
    )(place, own, got, got, got)


def share_totals(tot):
    def body(t_ref, out_ref, send_sem, recv_sem, local_sem):
        x, y, c, _ = _place()
        mine = pltpu.make_async_copy(t_ref, out_ref.at[c], local_sem)
        mine.start()
        cp = pltpu.make_async_remote_copy(src_ref=t_ref, dst_ref=out_ref.at[c], send_sem=send_sem, recv_sem=recv_sem,
                                          device_id=(x, y, 1 - c), device_id_type=MESH)
        cp.start()
        pltpu.make_async_remote_copy(src_ref=t_ref, dst_ref=out_ref.at[1 - c], send_sem=send_sem, recv_sem=recv_sem,
                                     device_id=(x, y, c), device_id_type=MESH).wait_recv()
        cp.wait_send()
        mine.wait()

    return pl.pallas_call(
        body, name="share_totals", out_shape=jax.ShapeDtypeStruct((2,) + tot.shape, tot.dtype),
        in_specs=[_any()], out_specs=_any(),
        scratch_shapes=[pltpu.SemaphoreType.DMA, pltpu.SemaphoreType.DMA, pltpu.SemaphoreType.DMA],
    )(tot)


def allreduce_small(v):
    rows, W = v.shape

    def body(v_ref, o_ref, buf, send_sems, recv_sems):
        x, y, c, _ = _place()
        me = 4 * x + 2 * y + c
        buf[me] = v_ref[...]
        sends = []
        for r in range(1, N_DEV):
            to = (x ^ (r >> 2), y ^ ((r >> 1) & 1), c ^ (r & 1))
            sends.append(pltpu.make_async_remote_copy(src_ref=v_ref, dst_ref=buf.at[me], send_sem=send_sems.at[r - 1],
                                                      recv_sem=recv_sems.at[r - 1], device_id=to, device_id_type=MESH))
        for cp in sends:
            cp.start()
        for r in range(1, N_DEV):
            pltpu.make_async_remote_copy(src_ref=v_ref, dst_ref=buf.at[me ^ r], send_sem=send_sems.at[r - 1],
                                         recv_sem=recv_sems.at[r - 1], device_id=(x, y, c), device_id_type=MESH).wait_recv()
        for cp in sends:
            cp.wait_send()
        acc = buf[0]
        for d in range(1, N_DEV):
            acc = acc + buf[d]
        o_ref[...] = acc

    vm = pl.BlockSpec(memory_space=pltpu.VMEM)
    return pl.pallas_call(
        body, name="allreduce_small", out_shape=jax.ShapeDtypeStruct(v.shape, v.dtype),
        in_specs=[vm], out_specs=vm,
        scratch_shapes=[pltpu.VMEM((N_DEV, rows, W), F32), pltpu.SemaphoreType.DMA((N_DEV - 1,)),
                        pltpu.SemaphoreType.DMA((N_DEV - 1,))],
    )(v)


def adamw(w, g, m, v, *, name):
    R, C = w.shape
    tr = R if R <= 512 else 512

    def body(w_ref, g_ref, m_ref, v_ref, d_ref, nm_ref, nv_ref):
        gv = g_ref[...]
        nm = ADAM_B1 * m_ref[...] + (1.0 - ADAM_B1) * gv
        nv = ADAM_B2 * v_ref[...] + (1.0 - ADAM_B2) * jnp.square(gv)
        m_hat = nm / (1.0 - ADAM_B1 ** ADAM_STEP)
        v_hat = nv / (1.0 - ADAM_B2 ** ADAM_STEP)
        d_ref[...] = -ADAM_LR * (m_hat / (jnp.sqrt(v_hat) + ADAM_EPS) + ADAM_WD * w_ref[...])
        nm_ref[...] = nm
        nv_ref[...] = nv

    spec = pl.BlockSpec((tr, C), lambda i: (i, 0))
    out = jax.ShapeDtypeStruct((R, C), F32)
    return pl.pallas_call(
        body, name=name, grid=(R // tr,), in_specs=[spec] * 4, out_specs=[spec] * 3, out_shape=[out] * 3,
        compiler_params=_cp("parallel"),
    )(w, g, m, v)


MATRICES = tuple(n for n, _ in FLAT_PARTS)
COL_SHARDED = ("w_in", "w_attn_proj", "w_conv_proj", "w_mlp1")
SMALL = ("mix_norm_g", "b_in", "sinks", "conv_w", "conv_b", "conv_ln_g", "conv_ln_b", "b_conv_proj", "mlp_norm_g")
SUBLANES = 8


def _tile_rows(v):
    flat = v.reshape(-1).astype(F32)
    rows = -(-flat.shape[0] // (SUBLANES * FLAT_W)) * SUBLANES
    return jnp.pad(flat, (0, rows * FLAT_W - flat.shape[0])).reshape(rows, FLAT_W)


def _pack_small(get, final):
    parts = [_tile_rows(get(n, l)) for l in range(DEPTH) for n in SMALL]
    return jnp.concatenate(parts + [_tile_rows(final)], axis=0)


def _unpack_small(packed, shapes):
    out = {n: [] for n in SMALL}
    r = 0
    for l in range(DEPTH):
        for n in SMALL:
            size = math.prod(shapes[n])
            rows = -(-size // (SUBLANES * FLAT_W)) * SUBLANES
            out[n].append(packed[r:r + rows].reshape(-1)[:size].reshape(shapes[n]))
            r += rows
    res = {n: jnp.stack(v) for n, v in out.items()}
    res["final_norm_g"] = packed[r:r + SUBLANES].reshape(-1)[:D]
    return res


def _flatten_shard(ws, l):
    return jnp.concatenate([ws[n][l].reshape(rows, FLAT_W) for n, rows in FLAT_PARTS], axis=0)


def _unflatten_full(wg, l):
    full_shapes = {"w_in": (D, IN_W), "w_attn_proj": (ATTN_W, D), "w_conv_proj": (CONV_C, D), "w_out": (D, D),
                   "w_mlp1": (D, D_FF), "w_mlp2": (D_FF, D)}
    out, r = {}, 0
    for n, rows in FLAT_PARTS:
        part = wg[:, l, r:r + rows]
        K, N = full_shapes[n]
        if n in COL_SHARDED:
            out[n] = part.reshape(N_CHIPS, K, N // N_CHIPS).transpose(1, 0, 2).reshape(K, N)
        else:
            out[n] = part.reshape(K, N)
        r += rows
    return out


def _flatten_grads(gr, l):
    parts = []
    for n, rows in FLAT_PARTS:
        g = gr[n][l]
        K, N = g.shape
        if n in COL_SHARDED:
            g = g.reshape(K, N_CHIPS, N // N_CHIPS).transpose(1, 0, 2)
        parts.append(g.reshape(N_CHIPS, rows, FLAT_W))
    return jnp.concatenate(parts, axis=1)


def _unflatten_shard(flat, shapes):
    out, r = {}, 0
    for n, rows in FLAT_PARTS:
        out[n] = flat[:, r:r + rows].reshape(shapes[n])
        r += rows
    return out


WEIGHTS = ("mix_norm_g", "w_in", "b_in", "sinks", "conv_w", "conv_b", "conv_ln_g", "conv_ln_b", "w_attn_proj",
           "w_conv_proj", "b_conv_proj", "w_out", "mlp_norm_g", "w_mlp1", "w_mlp2", "final_norm_g")


def kernel(x, mix_norm_g, w_in, b_in, sinks, conv_w, conv_b, conv_ln_g, conv_ln_b, w_attn_proj, w_conv_proj, b_conv_proj, w_out, mlp_norm_g, w_mlp1, w_mlp2, final_norm_g, loss_target, m_mix_norm_g, m_w_in, m_b_in, m_sinks, m_conv_w, m_conv_b, m_conv_ln_g, m_conv_ln_b, m_w_attn_proj, m_w_conv_proj, m_b_conv_proj, m_w_out, m_mlp_norm_g, m_w_mlp1, m_w_mlp2, m_final_norm_g, v_mix_norm_g, v_w_in, v_b_in, v_sinks, v_conv_w, v_conv_b, v_conv_ln_g, v_conv_ln_b, v_w_attn_proj, v_w_conv_proj, v_b_conv_proj, v_w_out, v_mlp_norm_g, v_w_mlp1, v_w_mlp2, v_final_norm_g):
    w = dict(zip(WEIGHTS, (mix_norm_g, w_in, b_in, sinks, conv_w, conv_b, conv_ln_g, conv_ln_b, w_attn_proj, w_conv_proj,
                           b_conv_proj, w_out, mlp_norm_g, w_mlp1, w_mlp2, final_norm_g)))
    m = dict(zip(WEIGHTS, (m_mix_norm_g, m_w_in, m_b_in, m_sinks, m_conv_w, m_conv_b, m_conv_ln_g, m_conv_ln_b, m_w_attn_proj,
                           m_w_conv_proj, m_b_conv_proj, m_w_out, m_mlp_norm_g, m_w_mlp1, m_w_mlp2, m_final_norm_g)))
    v = dict(zip(WEIGHTS, (v_mix_norm_g, v_w_in, v_b_in, v_sinks, v_conv_w, v_conv_b, v_conv_ln_g, v_conv_ln_b, v_w_attn_proj,
                           v_w_conv_proj, v_b_conv_proj, v_w_out, v_mlp_norm_g, v_w_mlp1, v_w_mlp2, v_final_norm_g)))
    xi, yi, ci = lax.axis_index("x"), lax.axis_index("y"), lax.axis_index("c")
    chip = 2 * xi + yi
    place = jnp.stack([ci, chip]).astype(jnp.int32)

    wsh = jnp.stack([_flatten_shard(w, l) for l in range(DEPTH)]).astype(CDT)
    wg = allgather_weights(wsh)
    taps = jnp.zeros((DEPTH, CONV_K, CONV_C), F32)
    taps = lax.dynamic_update_slice(taps, jnp.where(ci == 0, w["conv_w"], 0.0), (0, 0, chip * (CONV_C // N_CHIPS)))
    taps = allreduce_small(_tile_rows(taps)).reshape(-1)[:DEPTH * CONV_K * CONV_C].reshape(DEPTH, CONV_K, CONV_C)

    p = {n: [] for n in WEIGHTS if n != "final_norm_g"}
    for l in range(DEPTH):
        for n, mat in _unflatten_full(wg, l).items():
            p[n].append(mat)
        for n in ("mix_norm_g", "b_in", "conv_b", "conv_ln_g", "conv_ln_b", "b_conv_proj", "mlp_norm_g"):
            p[n].append(w[n][l].reshape(1, -1))
        p["sinks"].append(w["sinks"][l])
        p["conv_w"].append(taps[l])
    p["final_norm_g"] = w["final_norm_g"].reshape(1, D)

    loss, dx, gr = local_grads(x[0], loss_target[0], p)
    loss = lax.psum(loss[0, 0], ("x", "y", "c"))

    gflat = jnp.stack([_flatten_grads(gr, l) for l in range(DEPTH)])
    pb, own = pair_sum(gflat, swap_other_layer(gflat), place)
    tot = total_sum(own, exchange_partials(pb), place)
    gsh = _unflatten_shard(share_totals(tot), {n: w[n].shape for n in MATRICES})

    small_shapes = {n: w[n].shape[1:] for n in SMALL}
    small_shapes["conv_w"] = (CONV_K, CONV_C)
    gsmall = _unpack_small(allreduce_small(_pack_small(lambda n, l: gr[n][l], gr["final_norm_g"])), small_shapes)
    gsmall["conv_w"] = lax.dynamic_slice(gsmall["conv_w"], (0, 0, chip * (CONV_C // N_CHIPS)),
                                         (DEPTH, CONV_K, CONV_C // N_CHIPS))
    grads = {**gsh, **gsmall}

    delta, new_m, new_v = {}, {}, {}
    for n in MATRICES:
        shp = w[n].shape
        two_d = (shp[0] * shp[1], shp[2])
        d_, m_, v_ = adamw(w[n].reshape(two_d), grads[n].reshape(two_d), m[n].reshape(two_d), v[n].reshape(two_d),
                           name="adamw_" + n)
        delta[n], new_m[n], new_v[n] = d_.reshape(shp), m_.reshape(shp), v_.reshape(shp)
    small_shapes["conv_w"] = w["conv_w"].shape[1:]
    sm = [_pack_small(lambda n, l, t=t: t[n][l], t["final_norm_g"]) for t in (w, grads, m, v)]
    outs = adamw(*sm, name="adamw_small")
    for dst, packed in zip((delta, new_m, new_v), outs):
        dst.update(_unpack_small(packed, small_shapes))

    return (loss, dx[None], *[grads[n] for n in WEIGHTS], *[delta[n] for n in WEIGHTS],
            *[new_m[n] for n in WEIGHTS], *[new_v[n] for n in WEIGHTS])
```

```python
import functools
import math

import jax
import jax.numpy as jnp
import numpy as np
from jax import lax
from jax.experimental import pallas as pl
from jax.experimental.pallas import tpu as pltpu

F32 = jnp.float32
CDT = jnp.bfloat16

D = 1024
DEPTH = 2
N_Q = 8
HEAD_DIM = 64
ATTN_W = 512
KV_W = 128
BLOCK = 128
CONV_C = 512
CONV_K = 31
D_FF = 4096
IN_W = 3840
QKV_W = ATTN_W + 2 * KV_W
REST_W = IN_W - QKV_W
EPS = 1e-6
NEG = -1e30
SCALE = 1.0 / math.sqrt(HEAD_DIM)
SLOPES = [float(2.0 ** (-8.0 * (h + 1) / N_Q)) for h in range(N_Q)]
SUBLANES = 8
HALO = 32

ADAM_LR = 0.001
ADAM_B1 = 0.9
ADAM_B2 = 0.999
ADAM_EPS = 1e-08
ADAM_WD = 0.01
ADAM_STEP = 10

VMEM_LIMIT = 56 * 1024 * 1024


def _cp(*sem):
    return pltpu.CompilerParams(dimension_semantics=sem, vmem_limit_bytes=VMEM_LIMIT)


def _dot(a, b):
    return jnp.dot(a, b, preferred_element_type=F32)


def _dot_nt(a, b):
    return lax.dot_general(a, b, (((1,), (1,)), ((), ())), preferred_element_type=F32)


def _dot_tn(a, b):
    return lax.dot_general(a, b, (((0,), (0,)), ((), ())), preferred_element_type=F32)


def _sig(x):
    return 1.0 / (1.0 + jnp.exp(-x))


def _colsum(v):
    return jnp.sum(v, axis=0, keepdims=True)


def _const(shape):
    return pl.BlockSpec(shape, lambda *_: (0,) * len(shape))


def rms_inproj(x, g, w, b, *, tm=512):
    T = x.shape[0]

    def body(x_ref, g_ref, w_ref, b_ref, qkv_ref, rest_ref):
        xv = x_ref[...]
        r = lax.rsqrt(jnp.mean(xv * xv, axis=-1, keepdims=True) + EPS)
        h = (xv * r * g_ref[...]).astype(CDT)
        qkv_ref[...] = (_dot(h, w_ref[:, 0:QKV_W]) + b_ref[:, 0:QKV_W]).astype(qkv_ref.dtype)
        for j in range(REST_W // D):
            c0 = QKV_W + D * j
            rest_ref[:, D * j:D * (j + 1)] = _dot(h, w_ref[:, c0:c0 + D]) + b_ref[:, c0:c0 + D]

    return pl.pallas_call(
        body, name="rms_inproj", grid=(T // tm,),
        in_specs=[pl.BlockSpec((tm, D), lambda i: (i, 0)), _const((1, D)), _const((D, IN_W)), _const((1, IN_W))],
        out_specs=[pl.BlockSpec((tm, QKV_W), lambda i: (i, 0)), pl.BlockSpec((tm, REST_W), lambda i: (i, 0))],
        out_shape=[jax.ShapeDtypeStruct((T, QKV_W), CDT), jax.ShapeDtypeStruct((T, REST_W), F32)],
        compiler_params=_cp("parallel"),
    )(x, g, w, b)


def _lane_halves(shape):
    lane = lax.broadcasted_iota(jnp.int32, shape, 1)
    return lane < HEAD_DIM, lane >= HEAD_DIM


def _swap_halves(v):
    return pltpu.roll(v.astype(F32), HEAD_DIM, axis=1).astype(v.dtype)


def _attn_masks(first):
    row = lax.broadcasted_iota(jnp.int32, (BLOCK, 2 * BLOCK), 0)
    col = lax.broadcasted_iota(jnp.int32, (BLOCK, 2 * BLOCK), 1)
    dist = row + BLOCK - col
    valid = (dist >= 0) & (dist < BLOCK) & ((col >= BLOCK) | jnp.logical_not(first))
    return valid, dist.astype(F32)


def _qkv_specs(tq):
    nb = tq // BLOCK
    return [
        pl.BlockSpec((tq, ATTN_W), lambda i: (i, 0)),
        pl.BlockSpec((BLOCK, KV_W), lambda i: (jnp.maximum(i * nb - 1, 0), ATTN_W // KV_W)),
        pl.BlockSpec((tq, KV_W), lambda i: (i, ATTN_W // KV_W)),
        pl.BlockSpec((BLOCK, KV_W), lambda i: (jnp.maximum(i * nb - 1, 0), ATTN_W // KV_W + 1)),
        pl.BlockSpec((tq, KV_W), lambda i: (i, ATTN_W // KV_W + 1)),
    ]


def attn_fwd(qkv, sinks, *, tq=512):
    T = qkv.shape[0]
    nb = tq // BLOCK

    def body(sink_ref, q_ref, kp_ref, kc_ref, vp_ref, vc_ref, o_ref, lse_ref, kext, vext):
        i = pl.program_id(0)
        kext[0:BLOCK, :] = kp_ref[...]
        kext[BLOCK:, :] = kc_ref[...]
        vext[0:BLOCK, :] = vp_ref[...]
        vext[BLOCK:, :] = vc_ref[...]
        lo_q, hi_q = _lane_halves((BLOCK, 2 * HEAD_DIM))
        lo_k, hi_k = _lane_halves((2 * BLOCK, 2 * HEAD_DIM))
        lane_l = lax.broadcasted_iota(jnp.int32, (BLOCK, 128), 1)

        def blk(b, carry):
            r0 = pl.multiple_of(b * BLOCK, BLOCK)
            valid, distf = _attn_masks(jnp.logical_and(i == 0, b == 0))
            kc = kext[pl.ds(r0, 2 * BLOCK), :]
            vc = vext[pl.ds(r0, 2 * BLOCK), :]
            ks = (kc, _swap_halves(kc))
            vsw = _swap_halves(vc)
            zero = jnp.zeros_like(vc)
            lse_t = jnp.zeros((BLOCK, 128), F32)
            for pair in range(N_Q // 2):
                qp = q_ref[pl.ds(r0, BLOCK), pair * 128:(pair + 1) * 128]
                kv = pair // 2
                o_pair = jnp.zeros((BLOCK, 128), F32)
                for half in range(2):
                    h = 2 * pair + half
                    qm = jnp.where(lo_q if half == 0 else hi_q, qp, jnp.zeros_like(qp))
                    same = kv == half
                    s = _dot_nt(qm, ks[0] if same else ks[1]) * SCALE - SLOPES[h] * distf
                    s = jnp.where(valid, s, NEG)
                    sink = sink_ref[h]
                    m = jnp.maximum(jnp.max(s, axis=-1, keepdims=True), sink)
                    p = jnp.exp(s - m)
                    denom = jnp.sum(p, axis=-1, keepdims=True) + jnp.exp(sink - m)
                    p = p / denom
                    vv = jnp.where(lo_k if half == 0 else hi_k, vc if same else vsw, zero)
                    o_pair = o_pair + _dot(p.astype(CDT), vv)
                    lse_t = jnp.where(lane_l == h, m + jnp.log(denom), lse_t)
                o_ref[pl.ds(r0, BLOCK), pair * 128:(pair + 1) * 128] = o_pair.astype(o_ref.dtype)
            lse_ref[pl.ds(r0, BLOCK), :] = lse_t
            return carry

        lax.fori_loop(0, nb, blk, 0)

    return pl.pallas_call(
        body, name="attn_fwd", grid=(T // tq,),
        in_specs=[pl.BlockSpec(memory_space=pltpu.SMEM)] + _qkv_specs(tq),
        out_specs=[pl.BlockSpec((tq, ATTN_W), lambda i: (i, 0)), pl.BlockSpec((tq, 128), lambda i: (i, 0))],
        out_shape=[jax.ShapeDtypeStruct((T, ATTN_W), CDT), jax.ShapeDtypeStruct((T, 128), F32)],
        scratch_shapes=[pltpu.VMEM((tq + BLOCK, KV_W), CDT), pltpu.VMEM((tq + BLOCK, KV_W), CDT)],
        compiler_params=_cp("parallel"),
    )(sinks, qkv, qkv, qkv, qkv, qkv)


def _halo_before(tm, width, col):
    return pl.BlockSpec((HALO, width), lambda i: (jnp.maximum(i * (tm // HALO) - 1, 0), col))


def _fill_u0(ext, a_ref, b_ref, ha_ref, hb_ref, first):
    hu = ha_ref[...] * _sig(hb_ref[...])
    ext[0:HALO, :] = jnp.where(first, jnp.zeros_like(hu), hu)
    ext[HALO:, :] = a_ref[...] * _sig(b_ref[...])


def _shifted_taps(src, w_ref, base, rc, offsets):
    acc = jnp.zeros((rc, CONV_C), F32)
    for b in range(SUBLANES):
        taps = [(k, o - b) for k, o in enumerate(offsets) if o % SUBLANES == b]
        if not taps:
            continue
        rows = rc if b == 0 else rc + SUBLANES
        part = jnp.zeros((rows, CONV_C), F32)
        for k, o8 in taps:
            part = part + w_ref[k:k + 1, :] * src[base + o8:base + o8 + rows, :]
        acc = acc + (part if b == 0 else part[b:b + rc, :])
    return acc


def _conv_rows(ext, w_ref, r0, rc):
    return _shifted_taps(ext, w_ref, r0, rc, [HALO - (CONV_K - 1) + k for k in range(CONV_K)])


def _layer_norm(u1, g, b):
    mu = jnp.mean(u1, axis=-1, keepdims=True)
    xc = u1 - mu
    rstd = lax.rsqrt(jnp.mean(xc * xc, axis=-1, keepdims=True) + EPS)
    n = xc * rstd
    return n, rstd, n * g + b


CONV_RC = 32


def conv_fwd(rest, cw, cb, lg, lb, *, tm=512):
    T = rest.shape[0]

    def body(a_ref, b_ref, ha_ref, hb_ref, w_ref, cb_ref, lg_ref, lb_ref, o_ref, u1_ref, ext):
        _fill_u0(ext, a_ref, b_ref, ha_ref, hb_ref, pl.program_id(0) == 0)
        for c in range(tm // CONV_RC):
            r0 = c * CONV_RC
            u1 = _conv_rows(ext, w_ref, r0, CONV_RC) + cb_ref[...]
            u1_ref[r0:r0 + CONV_RC, :] = u1
            _, _, u2 = _layer_norm(u1, lg_ref[...], lb_ref[...])
            o_ref[r0:r0 + CONV_RC, :] = (u2 * _sig(u2)).astype(o_ref.dtype)

    row = lambda i: (i, 0)
    return pl.pallas_call(
        body, name="conv_fwd", grid=(T // tm,),
        in_specs=[pl.BlockSpec((tm, CONV_C), row), pl.BlockSpec((tm, CONV_C), lambda i: (i, 1)),
                  _halo_before(tm, CONV_C, 0), _halo_before(tm, CONV_C, 1),
                  _const((CONV_K, CONV_C)), _const((1, CONV_C)), _const((1, CONV_C)), _const((1, CONV_C))],
        out_specs=[pl.BlockSpec((tm, CONV_C), row), pl.BlockSpec((tm, CONV_C), row)],
        out_shape=[jax.ShapeDtypeStruct((T, CONV_C), CDT), jax.ShapeDtypeStruct((T, CONV_C), F32)],
        scratch_shapes=[pltpu.VMEM((tm + HALO, CONV_C), F32)],
        compiler_params=_cp("parallel"),
    )(rest, rest, rest, rest, cw, cb, lg, lb)


def merge_out(x, attn, u3, rest, wa, wc, bc, wo, *, tm=512):
    T = x.shape[0]

    def body(x_ref, at_ref, u_ref, ga_ref, gc_ref, wa_ref, wc_ref, bc_ref, wo_ref, o_ref):
        br_a = _dot(at_ref[...], wa_ref[...])
        br_c = _dot(u_ref[...], wc_ref[...]) + bc_ref[...]
        merged = _sig(ga_ref[...]) * br_a + _sig(gc_ref[...]) * br_c
        o_ref[...] = x_ref[...] + _dot(merged.astype(CDT), wo_ref[...])

    return pl.pallas_call(
        body, name="merge_out", grid=(T // tm,),
        in_specs=[pl.BlockSpec((tm, D), lambda i: (i, 0)), pl.BlockSpec((tm, ATTN_W), lambda i: (i, 0)),
                  pl.BlockSpec((tm, CONV_C), lambda i: (i, 0)),
                  pl.BlockSpec((tm, D), lambda i: (i, 1)), pl.BlockSpec((tm, D), lambda i: (i, 2)),
                  _const((ATTN_W, D)), _const((CONV_C, D)), _const((1, D)), _const((D, D))],
        out_specs=pl.BlockSpec((tm, D), lambda i: (i, 0)),
        out_shape=jax.ShapeDtypeStruct((T, D), F32),
        compiler_params=_cp("parallel"),
    )(x, attn, u3, rest, rest, wa, wc, bc, wo)


def mlp_fwd(x, g, w1, w2, *, tm=512, tf=1024):
    T = x.shape[0]
    nf = D_FF // tf

    def body(x_ref, g_ref, w1_ref, w2_ref, o_ref, pre_ref, h_s, acc_s):
        f = pl.program_id(1)

        @pl.when(f == 0)
        def _():
            xv = x_ref[...]
            r = lax.rsqrt(jnp.mean(xv * xv, axis=-1, keepdims=True) + EPS)
            h_s[...] = (xv * r * g_ref[...]).astype(CDT)
            acc_s[...] = jnp.zeros_like(acc_s)

        pre = _dot(h_s[...], w1_ref[...])
        pre_ref[...] = pre
        a = jnp.square(jnp.maximum(pre, 0.0))
        acc_s[...] += _dot(a.astype(CDT), w2_ref[...])

        @pl.when(f == nf - 1)
        def _():
            o_ref[...] = x_ref[...] + acc_s[...]

    return pl.pallas_call(
        body, name="mlp_fwd", grid=(T // tm, nf),
        in_specs=[pl.BlockSpec((tm, D), lambda i, f: (i, 0)), _const((1, D)),
                  pl.BlockSpec((D, tf), lambda i, f: (0, f)), pl.BlockSpec((tf, D), lambda i, f: (f, 0))],
        out_specs=[pl.BlockSpec((tm, D), lambda i, f: (i, 0)), pl.BlockSpec((tm, tf), lambda i, f: (i, f))],
        out_shape=[jax.ShapeDtypeStruct((T, D), F32), jax.ShapeDtypeStruct((T, D_FF), F32)],
        scratch_shapes=[pltpu.VMEM((tm, D), CDT), pltpu.VMEM((tm, D), F32)],
        compiler_params=_cp("parallel", "arbitrary"),
    )(x, g, w1, w2)


def _rms_bwd(xv, g, dh):
    r = lax.rsqrt(jnp.mean(xv * xv, axis=-1, keepdims=True) + EPS)
    xhat = xv * r
    dxh = dh * g
    dx = r * (dxh - xhat * jnp.mean(dxh * xhat, axis=-1, keepdims=True))
    return dx, dh * xhat


def loss_head(x, g, tgt, *, tm=512):
    T = x.shape[0]

    def body(x_ref, g_ref, t_ref, dx_ref, dg_ref, loss_ref):
        @pl.when(pl.program_id(0) == 0)
        def _():
            dg_ref[...] = jnp.zeros_like(dg_ref)
            loss_ref[...] = jnp.zeros_like(loss_ref)

        xv = x_ref[...]
        gv = g_ref[...]
        r = lax.rsqrt(jnp.mean(xv * xv, axis=-1, keepdims=True) + EPS)
        e = xv * r * gv - t_ref[...]
        loss_ref[...] += 0.5 * jnp.sum(jnp.mean(e * e, axis=-1, keepdims=True), axis=0, keepdims=True)
        dx, dg_rows = _rms_bwd(xv, gv, e * (1.0 / D))
        dx_ref[...] = dx
        dg_ref[...] += _colsum(dg_rows)

    return pl.pallas_call(
        body, name="loss_head", grid=(T // tm,),
        in_specs=[pl.BlockSpec((tm, D), lambda i: (i, 0)), _const((1, D)), pl.BlockSpec((tm, D), lambda i: (i, 0))],
        out_specs=[pl.BlockSpec((tm, D), lambda i: (i, 0)), _const((1, D)), _const((1, 128))],
        out_shape=[jax.ShapeDtypeStruct((T, D), F32), jax.ShapeDtypeStruct((1, D), F32),
                   jax.ShapeDtypeStruct((1, 128), F32)],
        compiler_params=_cp("arbitrary"),
    )(x, g, tgt)


def mlp_bwd(dy, x, g, pre, w1, w2, *, tm=512, tf=1024):
    T = x.shape[0]
    nf = D_FF // tf

    def body(dy_ref, x_ref, g_ref, pre_ref, w1_ref, w2_ref, dx_ref, dg_ref, h_ref, a_ref, dpre_ref, dyb_s, acc_s):
        i, f = pl.program_id(0), pl.program_id(1)

        @pl.when(jnp.logical_and(i == 0, f == 0))
        def _():
            dg_ref[...] = jnp.zeros_like(dg_ref)

        @pl.when(f == 0)
        def _():
            dyb_s[...] = dy_ref[...].astype(CDT)
            acc_s[...] = jnp.zeros_like(acc_s)

        pre = pre_ref[...]
        rl = jnp.maximum(pre, 0.0)
        a_ref[...] = (rl * rl).astype(CDT)
        da = _dot_nt(dyb_s[...], w2_ref[...])
        dpre = (da * (2.0 * rl)).astype(CDT)
        dpre_ref[...] = dpre
        acc_s[...] += _dot_nt(dpre, w1_ref[...])

        @pl.when(f == nf - 1)
        def _():
            xv = x_ref[...]
            gv = g_ref[...]
            dxn, dg_rows = _rms_bwd(xv, gv, acc_s[...])
            dx_ref[...] = dy_ref[...] + dxn
            dg_ref[...] += _colsum(dg_rows)
            r = lax.rsqrt(jnp.mean(xv * xv, axis=-1, keepdims=True) + EPS)
            h_ref[...] = (xv * r * gv).astype(CDT)

    row = lambda i, f: (i, 0)
    return pl.pallas_call(
        body, name="mlp_bwd", grid=(T // tm, nf),
        in_specs=[pl.BlockSpec((tm, D), row), pl.BlockSpec((tm, D), row), _const((1, D)),
                  pl.BlockSpec((tm, tf), lambda i, f: (i, f)),
                  pl.BlockSpec((D, tf), lambda i, f: (0, f)), pl.BlockSpec((tf, D), lambda i, f: (f, 0))],
        out_specs=[pl.BlockSpec((tm, D), row), _const((1, D)), pl.BlockSpec((tm, D), row),
                   pl.BlockSpec((tm, tf), lambda i, f: (i, f)), pl.BlockSpec((tm, tf), lambda i, f: (i, f))],
        out_shape=[jax.ShapeDtypeStruct((T, D), F32), jax.ShapeDtypeStruct((1, D), F32),
                   jax.ShapeDtypeStruct((T, D), CDT), jax.ShapeDtypeStruct((T, D_FF), CDT),
                   jax.ShapeDtypeStruct((T, D_FF), CDT)],
        scratch_shapes=[pltpu.VMEM((tm, D), CDT), pltpu.VMEM((tm, D), F32)],
        compiler_params=_cp("arbitrary", "arbitrary"),
    )(dy, x, g, pre, w1, w2)


def tn_matmul(a, b, *, tm, tn, tk=512, name):
    T, M = a.shape
    N = b.shape[1]
    nk = T // tk

    def body(a_ref, b_ref, o_ref):
        @pl.when(pl.program_id(2) == 0)
        def _():
            o_ref[...] = jnp.zeros_like(o_ref)

        o_ref[...] += _dot_tn(a_ref[...].astype(CDT), b_ref[...].astype(CDT))

    return pl.pallas_call(
        body, name=name, grid=(M // tm, N // tn, nk),
        in_specs=[pl.BlockSpec((tk, tm), lambda i, j, k: (k, i)), pl.BlockSpec((tk, tn), lambda i, j, k: (k, j))],
        out_specs=pl.BlockSpec((tm, tn), lambda i, j, k: (i, j)),
        out_shape=jax.ShapeDtypeStruct((M, N), F32),
        compiler_params=_cp("parallel", "parallel", "arbitrary"),
    )(a, b)


def merge_bwd(dx1, attn, u3, rest, wa, wc, bc, wo, *, tm=256):
    T = dx1.shape[0]

    def body(dx_ref, at_ref, u_ref, ga_ref, gc_ref, wa_ref, wc_ref, bc_ref, wo_ref,
             mg_ref, dba_ref, dbc_ref, dat_ref, du_ref, dgate_ref, dbias_ref):
        @pl.when(pl.program_id(0) == 0)
        def _():
            dbias_ref[...] = jnp.zeros_like(dbias_ref)

        br_a = _dot(at_ref[...], wa_ref[...])
        br_c = _dot(u_ref[...], wc_ref[...]) + bc_ref[...]
        sa = _sig(ga_ref[...])
        sc = _sig(gc_ref[...])
        mg_ref[...] = (sa * br_a + sc * br_c).astype(CDT)
        dm = _dot_nt(dx_ref[...].astype(CDT), wo_ref[...])
        dba = dm * sa
        dbc = dm * sc
        dgate_ref[:, 0:D] = dm * br_a * sa * (1.0 - sa)
        dgate_ref[:, D:2 * D] = dm * br_c * sc * (1.0 - sc)
        dbias_ref[...] += _colsum(dbc)
        dba_b = dba.astype(CDT)
        dbc_b = dbc.astype(CDT)
        dba_ref[...] = dba_b
        dbc_ref[...] = dbc_b
        dat_ref[...] = _dot_nt(dba_b, wa_ref[...]).astype(CDT)
        du_ref[...] = _dot_nt(dbc_b, wc_ref[...])

    row = lambda i: (i, 0)
    return pl.pallas_call(
        body, name="merge_bwd", grid=(T // tm,),
        in_specs=[pl.BlockSpec((tm, D), row), pl.BlockSpec((tm, ATTN_W), row), pl.BlockSpec((tm, CONV_C), row),
                  pl.BlockSpec((tm, D), lambda i: (i, 1)), pl.BlockSpec((tm, D), lambda i: (i, 2)),
                  _const((ATTN_W, D)), _const((CONV_C, D)), _const((1, D)), _const((D, D))],
        out_specs=[pl.BlockSpec((tm, D), row), pl.BlockSpec((tm, D), row), pl.BlockSpec((tm, D), row),
                   pl.BlockSpec((tm, ATTN_W), row), pl.BlockSpec((tm, CONV_C), row),
                   pl.BlockSpec((tm, 2 * D), row), _const((1, D))],
        out_shape=[jax.ShapeDtypeStruct((T, D), CDT), jax.ShapeDtypeStruct((T, D), CDT),
                   jax.ShapeDtypeStruct((T, D), CDT), jax.ShapeDtypeStruct((T, ATTN_W), CDT),
                   jax.ShapeDtypeStruct((T, CONV_C), F32), jax.ShapeDtypeStruct((T, 2 * D), F32),
                   jax.ShapeDtypeStruct((1, D), F32)],
        compiler_params=_cp("arbitrary"),
    )(dx1, attn, u3, rest, rest, wa, wc, bc, wo)


def conv_bwd_ln(du3, u1, lg, lb, *, tm=512):
    T = du3.shape[0]

    def body(du_ref, u1_ref, lg_ref, lb_ref, du1_ref, dlg_ref, dlb_ref, dcb_ref):
        @pl.when(pl.program_id(0) == 0)
        def _():
            dlg_ref[...] = jnp.zeros_like(dlg_ref)
            dlb_ref[...] = jnp.zeros_like(dlb_ref)
            dcb_ref[...] = jnp.zeros_like(dcb_ref)

        dlg = jnp.zeros((1, CONV_C), F32)
        dlb = jnp.zeros((1, CONV_C), F32)
        dcb = jnp.zeros((1, CONV_C), F32)
        for c in range(tm // CONV_RC):
            r0 = c * CONV_RC
            n, rstd, u2 = _layer_norm(u1_ref[r0:r0 + CONV_RC, :], lg_ref[...], lb_ref[...])
            s = _sig(u2)
            du2 = du_ref[r0:r0 + CONV_RC, :] * (s + u2 * s * (1.0 - s))
            dn = du2 * lg_ref[...]
            du1 = rstd * (dn - jnp.mean(dn, axis=-1, keepdims=True) - n * jnp.mean(dn * n, axis=-1, keepdims=True))
            du1_ref[r0:r0 + CONV_RC, :] = du1
            dlg = dlg + _colsum(du2 * n)
            dlb = dlb + _colsum(du2)
            dcb = dcb + _colsum(du1)
        dlg_ref[...] += dlg
        dlb_ref[...] += dlb
        dcb_ref[...] += dcb

    row = lambda i: (i, 0)
    vec = jax.ShapeDtypeStruct((1, CONV_C), F32)
    return pl.pallas_call(
        body, name="conv_bwd_ln", grid=(T // tm,),
        in_specs=[pl.BlockSpec((tm, CONV_C), row), pl.BlockSpec((tm, CONV_C), row), _const((1, CONV_C)), _const((1, CONV_C))],
        out_specs=[pl.BlockSpec((tm, CONV_C), row), _const((1, CONV_C)), _const((1, CONV_C)), _const((1, CONV_C))],
        out_shape=[jax.ShapeDtypeStruct((T, CONV_C), F32), vec, vec, vec],
        compiler_params=_cp("arbitrary"),
    )(du3, u1, lg, lb)


def conv_bwd_taps(du1, rest, cw, *, tm=512):
    T = du1.shape[0]
    nt = T // tm

    def body(d_ref, hd_ref, a_ref, b_ref, ha_ref, hb_ref, w_ref, dglu_ref, dw_ref, ext, dext, dwacc):
        i = pl.program_id(0)

        @pl.when(i == 0)
        def _():
            dwacc[...] = jnp.zeros_like(dwacc)

        _fill_u0(ext, a_ref, b_ref, ha_ref, hb_ref, i == 0)
        dext[0:SUBLANES, :] = jnp.zeros((SUBLANES, CONV_C), F32)
        dext[SUBLANES:SUBLANES + tm, :] = d_ref[...]
        hd = hd_ref[...]
        dext[SUBLANES + tm:, :] = jnp.where(i == nt - 1, jnp.zeros_like(hd), hd)
        qrow = lax.broadcasted_iota(jnp.int32, (CONV_RC + SUBLANES, CONV_C), 0)
        for c in range(tm // CONV_RC):
            r0 = c * CONV_RC
            du0 = _shifted_taps(dext, w_ref, r0 + SUBLANES, CONV_RC, [CONV_K - 1 - k for k in range(CONV_K)])
            for b in range(SUBLANES):
                taps = [(k, HALO - (CONV_K - 1) + k - b) for k in range(CONV_K) if (HALO - (CONV_K - 1) + k) % SUBLANES == b]
                if b == 0:
                    rows = CONV_RC
                    dsh = dext[r0 + SUBLANES:r0 + SUBLANES + rows, :]
                else:
                    rows = CONV_RC + SUBLANES
                    dsh = dext[r0 + SUBLANES - b:r0 + SUBLANES - b + rows, :]
                    dsh = jnp.where((qrow >= b) & (qrow < CONV_RC + b), dsh, 0.0)
                for k, o8 in taps:
                    prod = dsh * ext[r0 + o8:r0 + o8 + rows, :]
                    dwacc[8 * k:8 * k + 8, :] += jnp.sum(prod.reshape(rows // SUBLANES, SUBLANES, CONV_C), axis=0)
            av = a_ref[r0:r0 + CONV_RC, :]
            sb = _sig(b_ref[r0:r0 + CONV_RC, :])
            dglu_ref[r0:r0 + CONV_RC, 0:CONV_C] = du0 * sb
            dglu_ref[r0:r0 + CONV_RC, CONV_C:2 * CONV_C] = du0 * av * sb * (1.0 - sb)

        @pl.when(i == nt - 1)
        def _():
            dw_ref[...] = jnp.zeros_like(dw_ref)
            for k in range(CONV_K):
                dw_ref[k:k + 1, :] = _colsum(dwacc[8 * k:8 * k + 8, :])

    row = lambda i: (i, 0)
    return pl.pallas_call(
        body, name="conv_bwd_taps", grid=(nt,),
        in_specs=[pl.BlockSpec((tm, CONV_C), row),
                  pl.BlockSpec((HALO, CONV_C), lambda i: (jnp.minimum((i + 1) * (tm // HALO), T // HALO - 1), 0)),
                  pl.BlockSpec((tm, CONV_C), row), pl.BlockSpec((tm, CONV_C), lambda i: (i, 1)),
                  _halo_before(tm, CONV_C, 0), _halo_before(tm, CONV_C, 1), _const((CONV_K, CONV_C))],
        out_specs=[pl.BlockSpec((tm, 2 * CONV_C), row), _const((HALO, CONV_C))],
        out_shape=[jax.ShapeDtypeStruct((T, 2 * CONV_C), F32), jax.ShapeDtypeStruct((HALO, CONV_C), F32)],
        scratch_shapes=[pltpu.VMEM((tm + HALO, CONV_C), F32), pltpu.VMEM((SUBLANES + tm + HALO, CONV_C), F32),
                        pltpu.VMEM((8 * CONV_K, CONV_C), F32)],
        compiler_params=_cp("arbitrary"),
    )(du1, du1, rest, rest, rest, rest, cw)


def attn_bwd(qkv, do, lse, sinks, *, tq=512):
    T = qkv.shape[0]
    nb = tq // BLOCK

    def body(sink_ref, q_ref, kp_ref, kc_ref, vp_ref, vc_ref, do_ref, lse_ref,
             dq_ref, dkv_ref, spill_ref, dsink_ref, kext, vext, dkext, dvext):
        i = pl.program_id(0)

        @pl.when(i == 0)
        def _():
            dsink_ref[...] = jnp.zeros_like(dsink_ref)

        kext[0:BLOCK, :] = kp_ref[...]
        kext[BLOCK:, :] = kc_ref[...]
        vext[0:BLOCK, :] = vp_ref[...]
        vext[BLOCK:, :] = vc_ref[...]
        dkext[...] = jnp.zeros_like(dkext)
        dvext[...] = jnp.zeros_like(dvext)
        lo_q, hi_q = _lane_halves((BLOCK, 2 * HEAD_DIM))
        lo_k, hi_k = _lane_halves((2 * BLOCK, 2 * HEAD_DIM))
        lane_l = lax.broadcasted_iota(jnp.int32, (BLOCK, 128), 1)

        def blk(b, dsink):
            r0 = pl.multiple_of(b * BLOCK, BLOCK)
            valid, distf = _attn_masks(jnp.logical_and(i == 0, b == 0))
            kc = kext[pl.ds(r0, 2 * BLOCK), :]
            vc = vext[pl.ds(r0, 2 * BLOCK), :]
            ksw = _swap_halves(kc)
            vsw = _swap_halves(vc)
            zero = jnp.zeros_like(kc)
            lse_t = lse_ref[pl.ds(r0, BLOCK), :]
            dk_same = jnp.zeros((2 * BLOCK, KV_W), F32)
            dk_swap = jnp.zeros((2 * BLOCK, KV_W), F32)
            dv_same = jnp.zeros((2 * BLOCK, KV_W), F32)
            dv_swap = jnp.zeros((2 * BLOCK, KV_W), F32)
            for pair in range(N_Q // 2):
                qp = q_ref[pl.ds(r0, BLOCK), pair * 128:(pair + 1) * 128]
                dop = do_ref[pl.ds(r0, BLOCK), pair * 128:(pair + 1) * 128]
                kv = pair // 2
                dq_pair = jnp.zeros((BLOCK, 128), F32)
                for half in range(2):
                    h = 2 * pair + half
                    mq = lo_q if half == 0 else hi_q
                    mk = lo_k if half == 0 else hi_k
                    same = kv == half
                    qm = jnp.where(mq, qp, jnp.zeros_like(qp))
                    dom = jnp.where(mq, dop, jnp.zeros_like(dop))
                    kk = kc if same else ksw
                    vv = vc if same else vsw
                    s = _dot_nt(qm, kk) * SCALE - SLOPES[h] * distf
                    s = jnp.where(valid, s, NEG)
                    lse_h = jnp.sum(jnp.where(lane_l == h, lse_t, 0.0), axis=-1, keepdims=True)
                    p = jnp.exp(s - lse_h)
                    dp = _dot_nt(dom, vv)
                    dd = jnp.sum(p * dp, axis=-1, keepdims=True)
                    ds = (p * (dp - dd)).astype(CDT)
                    dq_pair = dq_pair + _dot(ds, jnp.where(mk, kk, zero))
                    dk_h = _dot_tn(ds, qm)
                    dv_h = _dot_tn(p.astype(CDT), dom)
                    if same:
                        dk_same, dv_same = dk_same + dk_h, dv_same + dv_h
                    else:
                        dk_swap, dv_swap = dk_swap + dk_h, dv_swap + dv_h
                    psink = jnp.exp(sink_ref[h] - lse_h)
                    dsink = dsink - jnp.where(lane_l[0:1, :] == h, jnp.sum(psink * dd, axis=0, keepdims=True), 0.0)
                dq_ref[pl.ds(r0, BLOCK), pair * 128:(pair + 1) * 128] = dq_pair * SCALE
            dkext[pl.ds(r0, 2 * BLOCK), :] += (dk_same + _swap_halves(dk_swap)) * SCALE
            dvext[pl.ds(r0, 2 * BLOCK), :] += dv_same + _swap_halves(dv_swap)
            return dsink

        dsink_ref[...] += lax.fori_loop(0, nb, blk, jnp.zeros((1, 128), F32))
        dkv_ref[:, 0:KV_W] = dkext[BLOCK:, :]
        dkv_ref[:, KV_W:2 * KV_W] = dvext[BLOCK:, :]
        spill_ref[:, 0:KV_W] = dkext[0:BLOCK, :]
        spill_ref[:, KV_W:2 * KV_W] = dvext[0:BLOCK, :]

    row = lambda i: (i, 0)
    return pl.pallas_call(
        body, name="attn_bwd", grid=(T // tq,),
        in_specs=[pl.BlockSpec(memory_space=pltpu.SMEM)] + _qkv_specs(tq)
        + [pl.BlockSpec((tq, ATTN_W), row), pl.BlockSpec((tq, 128), row)],
        out_specs=[pl.BlockSpec((tq, ATTN_W), row), pl.BlockSpec((tq, 2 * KV_W), row),
                   pl.BlockSpec((BLOCK, 2 * KV_W), row), _const((1, 128))],
        out_shape=[jax.ShapeDtypeStruct((T, ATTN_W), F32), jax.ShapeDtypeStruct((T, 2 * KV_W), F32),
                   jax.ShapeDtypeStruct((T // tq * BLOCK, 2 * KV_W), F32), jax.ShapeDtypeStruct((1, 128), F32)],
        scratch_shapes=[pltpu.VMEM((tq + BLOCK, KV_W), CDT), pltpu.VMEM((tq + BLOCK, KV_W), CDT),
                        pltpu.VMEM((tq + BLOCK, KV_W), F32), pltpu.VMEM((tq + BLOCK, KV_W), F32)],
        compiler_params=_cp("arbitrary"),
    )(sinks, qkv, qkv, qkv, qkv, qkv, do, lse)


def inproj_bwd(dres, x, g, w, dq, dkv, spill, dglu, dgate, *, tm=512):
    T = x.shape[0]
    nt = T // tm

    def body(dr_ref, x_ref, g_ref, w_ref, dq_ref, dkv_ref, sp_ref, dglu_ref, dgate_ref,
             dx_ref, dp_ref, h_ref, dg_ref, db_ref):
        i = pl.program_id(0)

        @pl.when(i == 0)
        def _():
            dg_ref[...] = jnp.zeros_like(dg_ref)
            db_ref[...] = jnp.zeros_like(db_ref)

        sp = sp_ref[...]
        sp = jnp.where(i == nt - 1, jnp.zeros_like(sp), sp)
        pieces = ((0, ATTN_W, dq_ref), (QKV_W, 2 * CONV_C, dglu_ref), (QKV_W + 2 * CONV_C, 2 * D, dgate_ref))
        for c0, wd, ref in pieces:
            v = ref[...]
            db_ref[:, c0:c0 + wd] += _colsum(v)
            dp_ref[:, c0:c0 + wd] = v.astype(CDT)
        dkv = dkv_ref[...]
        db_ref[:, ATTN_W:QKV_W] += _colsum(dkv) + _colsum(sp)
        dp_ref[0:tm - BLOCK, ATTN_W:QKV_W] = dkv[0:tm - BLOCK, :].astype(CDT)
        dp_ref[tm - BLOCK:tm, ATTN_W:QKV_W] = (dkv[tm - BLOCK:tm, :] + sp).astype(CDT)
        dh = _dot_nt(dp_ref[...], w_ref[...])
        xv = x_ref[...]
        gv = g_ref[...]
        dxn, dg_rows = _rms_bwd(xv, gv, dh)
        dx_ref[...] = dr_ref[...] + dxn
        dg_ref[...] += _colsum(dg_rows)
        r = lax.rsqrt(jnp.mean(xv * xv, axis=-1, keepdims=True) + EPS)
        h_ref[...] = (xv * r * gv).astype(CDT)

    row = lambda i: (i, 0)
    return pl.pallas_call(
        body, name="inproj_bwd", grid=(nt,),
        in_specs=[pl.BlockSpec((tm, D), row), pl.BlockSpec((tm, D), row), _const((1, D)), _const((D, IN_W)),
                  pl.BlockSpec((tm, ATTN_W), row), pl.BlockSpec((tm, 2 * KV_W), row),
                  pl.BlockSpec((BLOCK, 2 * KV_W), lambda i: (jnp.minimum(i + 1, nt - 1), 0)),
                  pl.BlockSpec((tm, 2 * CONV_C), row), pl.BlockSpec((tm, 2 * D), row)],
        out_specs=[pl.BlockSpec((tm, D), row), pl.BlockSpec((tm, IN_W), row), pl.BlockSpec((tm, D), row),
                   _const((1, D)), _const((1, IN_W))],
        out_shape=[jax.ShapeDtypeStruct((T, D), F32), jax.ShapeDtypeStruct((T, IN_W), CDT),
                   jax.ShapeDtypeStruct((T, D), CDT), jax.ShapeDtypeStruct((1, D), F32),
                   jax.ShapeDtypeStruct((1, IN_W), F32)],
        compiler_params=_cp("arbitrary"),
    )(dres, x, g, w, dq, dkv, spill, dglu, dgate)


ATTN_TILE = 256


def local_grads(x, tgt, p):
    saved = []
    for l in range(DEPTH):
        qkv, rest = rms_inproj(x, p["mix_norm_g"][l], p["w_in"][l], p["b_in"][l])
        attn, lse = attn_fwd(qkv, p["sinks"][l], tq=ATTN_TILE)
        u3, u1 = conv_fwd(rest, p["conv_w"][l], p["conv_b"][l], p["conv_ln_g"][l], p["conv_ln_b"][l])
        x1 = merge_out(x, attn, u3, rest, p["w_attn_proj"][l], p["w_conv_proj"][l], p["b_conv_proj"][l], p["w_out"][l])
        x2, pre = mlp_fwd(x1, p["mlp_norm_g"][l], p["w_mlp1"][l], p["w_mlp2"][l])
        saved.append((x, qkv, rest, attn, lse, u3, u1, x1, pre))
        x = x2
    dx, dgf, loss = loss_head(x, p["final_norm_g"], tgt)
    names = ("mix_norm_g", "w_in", "b_in", "sinks", "conv_w", "conv_b", "conv_ln_g", "conv_ln_b", "w_attn_proj",
             "w_conv_proj", "b_conv_proj", "w_out", "mlp_norm_g", "w_mlp1", "w_mlp2")
    grads = {n: [None] * DEPTH for n in names}
    grads["final_norm_g"] = dgf
    for l in reversed(range(DEPTH)):
        x0, qkv, rest, attn, lse, u3, u1, x1, pre = saved[l]
        dx1, dg2, h2, a, dpre = mlp_bwd(dx, x1, p["mlp_norm_g"][l], pre, p["w_mlp1"][l], p["w_mlp2"][l])
        grads["mlp_norm_g"][l] = dg2
        grads["w_mlp1"][l] = tn_matmul(h2, dpre, tm=1024, tn=1024, name="tn_mlp1")
        grads["w_mlp2"][l] = tn_matmul(a, dx, tm=1024, tn=1024, name="tn_mlp2")
        merged, dba, dbc, dattn, du3, dgate, dbcp = merge_bwd(
            dx1, attn, u3, rest, p["w_attn_proj"][l], p["w_conv_proj"][l], p["b_conv_proj"][l], p["w_out"][l])
        grads["b_conv_proj"][l] = dbcp
        grads["w_out"][l] = tn_matmul(merged, dx1, tm=1024, tn=1024, name="tn_out")
        grads["w_attn_proj"][l] = tn_matmul(attn, dba, tm=512, tn=1024, name="tn_attn_proj")
        grads["w_conv_proj"][l] = tn_matmul(u3, dbc, tm=512, tn=1024, name="tn_conv_proj")
        du1, dlg, dlb, dcb = conv_bwd_ln(du3, u1, p["conv_ln_g"][l], p["conv_ln_b"][l])
        grads["conv_ln_g"][l], grads["conv_ln_b"][l], grads["conv_b"][l] = dlg, dlb, dcb
        dglu, dcw = conv_bwd_taps(du1, rest, p["conv_w"][l])
        grads["conv_w"][l] = dcw[0:CONV_K]
        dq, dkv, spill, dsink = attn_bwd(qkv, dattn, lse, p["sinks"][l], tq=ATTN_TILE)
        grads["sinks"][l] = dsink[0, 0:N_Q]
        dx, dproj, h, dg, db = inproj_bwd(dx1, x0, p["mix_norm_g"][l], p["w_in"][l], dq, dkv, spill, dglu, dgate,
                                          tm=ATTN_TILE)
        grads["mix_norm_g"][l], grads["b_in"][l] = dg, db
        grads["w_in"][l] = tn_matmul(h, dproj, tm=1024, tn=768, name="tn_in")
    return loss, dx, grads


MESH = pl.DeviceIdType.MESH
N_CHIPS = 4
N_DEV = 8
FLAT_W = 1024
FLAT_PARTS = (("w_in", 960), ("w_attn_proj", 128), ("w_conv_proj", 128), ("w_out", 256), ("w_mlp1", 1024), ("w_mlp2", 1024))
FLAT_ROWS = sum(r for _, r in FLAT_PARTS)
FLAT_TILE = 704


def _place():
    x, y, c = lax.axis_index("x"), lax.axis_index("y"), lax.axis_index("c")
    return x, y, c, 2 * x + y


def _peer_chips(x, y, j):
    return [((x, 1 - y), j ^ 1), ((1 - x, y), j ^ 2), ((1 - x, 1 - y), j ^ 3)]


def _any():
    return pl.BlockSpec(memory_space=pl.ANY)


def allgather_weights(wsh):
    shape = (N_CHIPS,) + wsh.shape

    def body(w_ref, out_ref, send_sems, recv_sems):
        x, y, c, j = _place()
        peers = _peer_chips(x, y, j)

        def remote(src, dst, k, to):
            return pltpu.make_async_remote_copy(src_ref=src, dst_ref=dst, send_sem=send_sems.at[k],
                                                recv_sem=recv_sems.at[k], device_id=to, device_id_type=MESH)

        first = [remote(w_ref.at[c], out_ref.at[j, c], k, (*chip, c)) for k, (chip, _) in enumerate(peers)]
        for cp in first:
            cp.start()
        passed = [remote(out_ref.at[pj, c], out_ref.at[pj, c], 3 + k, (x, y, 1 - c)) for k, (_, pj) in enumerate(peers)]
        for k, (_, pj) in enumerate(peers):
            remote(w_ref.at[c], out_ref.at[pj, c], k, (x, y, c)).wait_recv()
            passed[k].start()
        for k, (_, pj) in enumerate(peers):
            remote(w_ref.at[c], out_ref.at[pj, 1 - c], 3 + k, (x, y, c)).wait_recv()
        for cp in first + passed:
            cp.wait_send()

    return pl.pallas_call(
        body, name="allgather_weights", out_shape=jax.ShapeDtypeStruct(shape, wsh.dtype),
        in_specs=[_any()], out_specs=_any(),
        scratch_shapes=[pltpu.SemaphoreType.DMA((6,)), pltpu.SemaphoreType.DMA((6,))],
    )(wsh)


def swap_other_layer(g):
    def body(g_ref, got_ref, send_sem, recv_sem):
        x, y, c, _ = _place()
        cp = pltpu.make_async_remote_copy(src_ref=g_ref.at[1 - c], dst_ref=got_ref, send_sem=send_sem, recv_sem=recv_sem,
                                          device_id=(x, y, 1 - c), device_id_type=MESH)
        cp.start()
        cp.wait()

    return pl.pallas_call(
        body, name="swap_other_layer", out_shape=jax.ShapeDtypeStruct(g.shape[1:], g.dtype),
        in_specs=[_any()], out_specs=_any(),
        scratch_shapes=[pltpu.SemaphoreType.DMA, pltpu.SemaphoreType.DMA],
    )(g)


def pair_sum(g, got, place):
    _, nj, R, W = g.shape

    def body(s_ref, g_ref, got_ref, pb_ref, own_ref):
        v = g_ref[...] + got_ref[...]
        pb_ref[...] = v.astype(pb_ref.dtype)

        @pl.when(pl.program_id(1) == s_ref[1])
        def _():
            own_ref[...] = v

    return pl.pallas_call(
        body, name="pair_sum",
        grid_spec=pltpu.PrefetchScalarGridSpec(
            num_scalar_prefetch=1, grid=(R // FLAT_TILE, nj),
            in_specs=[pl.BlockSpec((None, None, FLAT_TILE, W), lambda r, j, s: (s[0], j, r, 0)),
                      pl.BlockSpec((None, FLAT_TILE, W), lambda r, j, s: (j, r, 0))],
            out_specs=[pl.BlockSpec((None, FLAT_TILE, W), lambda r, j, s: (j, r, 0)),
                       pl.BlockSpec((FLAT_TILE, W), lambda r, j, s: (r, 0))]),
        out_shape=[jax.ShapeDtypeStruct((nj, R, W), CDT), jax.ShapeDtypeStruct((R, W), F32)],
        compiler_params=_cp("arbitrary", "arbitrary"),
    )(place, g, got)


def exchange_partials(pb):
    def body(pb_ref, got_ref, send_sems, recv_sems):
        x, y, c, j = _place()
        peers = _peer_chips(x, y, j)
        sends = [pltpu.make_async_remote_copy(src_ref=pb_ref.at[pj], dst_ref=got_ref.at[j], send_sem=send_sems.at[k],
                                              recv_sem=recv_sems.at[k], device_id=(*chip, c), device_id_type=MESH)
                 for k, (chip, pj) in enumerate(peers)]
        for cp in sends:
            cp.start()
        for k, (_, pj) in enumerate(peers):
            pltpu.make_async_remote_copy(src_ref=pb_ref.at[pj], dst_ref=got_ref.at[pj], send_sem=send_sems.at[k],
                                         recv_sem=recv_sems.at[k], device_id=(x, y, c), device_id_type=MESH).wait_recv()
        for cp in sends:
            cp.wait_send()

    return pl.pallas_call(
        body, name="exchange_partials", out_shape=jax.ShapeDtypeStruct(pb.shape, pb.dtype),
        in_specs=[_any()], out_specs=_any(),
        scratch_shapes=[pltpu.SemaphoreType.DMA((3,)), pltpu.SemaphoreType.DMA((3,))],
    )(pb)


def total_sum(own, got, place):
    R, W = own.shape

    def body(s_ref, own_ref, a_ref, b_ref, c_ref, o_ref):
        o_ref[...] = ((own_ref[...] + a_ref[...].astype(F32)) + b_ref[...].astype(F32)) + c_ref[...].astype(F32)

    def slab(k):
        return pl.BlockSpec((None, FLAT_TILE, W), lambda r, s: (s[1] ^ (k + 1), r, 0))

    return pl.pallas_call(
        body, name="total_sum",
        grid_spec=pltpu.PrefetchScalarGridSpec(
            num_scalar_prefetch=1, grid=(R // FLAT_TILE,),
            in_specs=[pl.BlockSpec((FLAT_TILE, W), lambda r, s: (r, 0)), slab(0), slab(1), slab(2)],
            out_specs=pl.BlockSpec((FLAT_TILE, W), lambda r, s: (r, 0))),
        out_shape=jax.ShapeDtypeStruct((R, W), F32),
        compiler_params=_cp("arbitrary"),
    )(place, own, got, got, got)


def share_totals(tot):
    def body(t_ref, got_ref, send_sem, recv_sem):
        x, y, c, _ = _place()
        cp = pltpu.make_async_remote_copy(src_ref=t_ref, dst_ref=got_ref, send_sem=send_sem, recv_sem=recv_sem,
                                          device_id=(x, y, 1 - c), device_id_type=MESH)
        cp.start()
        cp.wait()

    return pl.pallas_call(
        body, name="share_totals", out_shape=jax.ShapeDtypeStruct(tot.shape, tot.dtype),
        in_specs=[_any()], out_specs=_any(),
        scratch_shapes=[pltpu.SemaphoreType.DMA, pltpu.SemaphoreType.DMA],
    )(tot)


def allreduce_small(v):
    rows, W = v.shape

    def body(v_ref, o_ref, buf, send_sems, recv_sems):
        x, y, c, _ = _place()
        me = 4 * x + 2 * y + c
        buf[me] = v_ref[...]
        sends = []
        for r in range(1, N_DEV):
            to = (x ^ (r >> 2), y ^ ((r >> 1) & 1), c ^ (r & 1))
            sends.append(pltpu.make_async_remote_copy(src_ref=v_ref, dst_ref=buf.at[me], send_sem=send_sems.at[r - 1],
                                                      recv_sem=recv_sems.at[r - 1], device_id=to, device_id_type=MESH))
        for cp in sends:
            cp.start()
        for r in range(1, N_DEV):
            pltpu.make_async_remote_copy(src_ref=v_ref, dst_ref=buf.at[me ^ r], send_sem=send_sems.at[r - 1],
                                         recv_sem=recv_sems.at[r - 1], device_id=(x, y, c), device_id_type=MESH).wait_recv()
        for cp in sends:
            cp.wait_send()
        acc = buf[0]
        for d in range(1, N_DEV):
            acc = acc + buf[d]
        o_ref[...] = acc

    vm = pl.BlockSpec(memory_space=pltpu.VMEM)
    return pl.pallas_call(
        body, name="allreduce_small", out_shape=jax.ShapeDtypeStruct(v.shape, v.dtype),
        in_specs=[vm], out_specs=vm,
        scratch_shapes=[pltpu.VMEM((N_DEV, rows, W), F32), pltpu.SemaphoreType.DMA((N_DEV - 1,)),
                        pltpu.SemaphoreType.DMA((N_DEV - 1,))],
    )(v)


def adamw(w, g, m, v, *, name):
    R, C = w.shape
    tr = R if R <= 512 else 512

    def body(w_ref, g_ref, m_ref, v_ref, d_ref, nm_ref, nv_ref):
        gv = g_ref[...]
        nm = ADAM_B1 * m_ref[...] + (1.0 - ADAM_B1) * gv
        nv = ADAM_B2 * v_ref[...] + (1.0 - ADAM_B2) * jnp.square(gv)
        m_hat = nm / (1.0 - ADAM_B1 ** ADAM_STEP)
        v_hat = nv / (1.0 - ADAM_B2 ** ADAM_STEP)
        d_ref[...] = -ADAM_LR * (m_hat / (jnp.sqrt(v_hat) + ADAM_EPS) + ADAM_WD * w_ref[...])
        nm_ref[...] = nm
        nv_ref[...] = nv

    spec = pl.BlockSpec((tr, C), lambda i: (i, 0))
    out = jax.ShapeDtypeStruct((R, C), F32)
    return pl.pallas_call(
        body, name=name, grid=(R // tr,), in_specs=[spec] * 4, out_specs=[spec] * 3, out_shape=[out] * 3,
        compiler_params=_cp("parallel"),
    )(w, g, m, v)


MATRICES = tuple(n for n, _ in FLAT_PARTS)
COL_SHARDED = ("w_in", "w_attn_proj", "w_conv_proj", "w_mlp1")
SMALL = ("mix_norm_g", "b_in", "sinks", "conv_w", "conv_b", "conv_ln_g", "conv_ln_b", "b_conv_proj", "mlp_norm_g")


def _tile_rows(v):
    flat = v.reshape(-1).astype(F32)
    rows = -(-flat.shape[0] // (SUBLANES * FLAT_W)) * SUBLANES
    return jnp.pad(flat, (0, rows * FLAT_W - flat.shape[0])).reshape(rows, FLAT_W)


def _pack_small(get, final):
    parts = [_tile_rows(get(n, l)) for l in range(DEPTH) for n in SMALL]
    return jnp.concatenate(parts + [_tile_rows(final)], axis=0)


def _unpack_small(packed, shapes):
    out = {n: [] for n in SMALL}
    r = 0
    for l in range(DEPTH):
        for n in SMALL:
            size = math.prod(shapes[n])
            rows = -(-size // (SUBLANES * FLAT_W)) * SUBLANES
            out[n].append(packed[r:r + rows].reshape(-1)[:size].reshape(shapes[n]))
            r += rows
    res = {n: jnp.stack(v) for n, v in out.items()}
    res["final_norm_g"] = packed[r:r + SUBLANES].reshape(-1)[:D]
    return res


def _flatten_shard(ws, l):
    return jnp.concatenate([ws[n][l].reshape(rows, FLAT_W) for n, rows in FLAT_PARTS], axis=0)


def _unflatten_full(wg, l):
    full_shapes = {"w_in": (D, IN_W), "w_attn_proj": (ATTN_W, D), "w_conv_proj": (CONV_C, D), "w_out": (D, D),
                   "w_mlp1": (D, D_FF), "w_mlp2": (D_FF, D)}
    out, r = {}, 0
    for n, rows in FLAT_PARTS:
        part = wg[:, l, r:r + rows]
        K, N = full_shapes[n]
        if n in COL_SHARDED:
            out[n] = part.reshape(N_CHIPS, K, N // N_CHIPS).transpose(1, 0, 2).reshape(K, N)
        else:
            out[n] = part.reshape(K, N)
        r += rows
    return out


def _flatten_grads(gr, l):
    parts = []
    for n, rows in FLAT_PARTS:
        g = gr[n][l]
        K, N = g.shape
        if n in COL_SHARDED:
            g = g.reshape(K, N_CHIPS, N // N_CHIPS).transpose(1, 0, 2)
        parts.append(g.reshape(N_CHIPS, rows, FLAT_W))
    return jnp.concatenate(parts, axis=1)


def _unflatten_shard(flat, shapes):
    out, r = {}, 0
    for n, rows in FLAT_PARTS:
        out[n] = flat[:, r:r + rows].reshape(shapes[n])
        r += rows
    return out


WEIGHTS = ("mix_norm_g", "w_in", "b_in", "sinks", "conv_w", "conv_b", "conv_ln_g", "conv_ln_b", "w_attn_proj",
           "w_conv_proj", "b_conv_proj", "w_out", "mlp_norm_g", "w_mlp1", "w_mlp2", "final_norm_g")


def kernel(x, mix_norm_g, w_in, b_in, sinks, conv_w, conv_b, conv_ln_g, conv_ln_b, w_attn_proj, w_conv_proj, b_conv_proj, w_out, mlp_norm_g, w_mlp1, w_mlp2, final_norm_g, loss_target, m_mix_norm_g, m_w_in, m_b_in, m_sinks, m_conv_w, m_conv_b, m_conv_ln_g, m_conv_ln_b, m_w_attn_proj, m_w_conv_proj, m_b_conv_proj, m_w_out, m_mlp_norm_g, m_w_mlp1, m_w_mlp2, m_final_norm_g, v_mix_norm_g, v_w_in, v_b_in, v_sinks, v_conv_w, v_conv_b, v_conv_ln_g, v_conv_ln_b, v_w_attn_proj, v_w_conv_proj, v_b_conv_proj, v_w_out, v_mlp_norm_g, v_w_mlp1, v_w_mlp2, v_final_norm_g):
    w = dict(zip(WEIGHTS, (mix_norm_g, w_in, b_in, sinks, conv_w, conv_b, conv_ln_g, conv_ln_b, w_attn_proj, w_conv_proj,
                           b_conv_proj, w_out, mlp_norm_g, w_mlp1, w_mlp2, final_norm_g)))
    m = dict(zip(WEIGHTS, (m_mix_norm_g, m_w_in, m_b_in, m_sinks, m_conv_w, m_conv_b, m_conv_ln_g, m_conv_ln_b, m_w_attn_proj,
                           m_w_conv_proj, m_b_conv_proj, m_w_out, m_mlp_norm_g, m_w_mlp1, m_w_mlp2, m_final_norm_g)))
    v = dict(zip(WEIGHTS, (v_mix_norm_g, v_w_in, v_b_in, v_sinks, v_conv_w, v_conv_b, v_conv_ln_g, v_conv_ln_b, v_w_attn_proj,
                           v_w_conv_proj, v_b_conv_proj, v_w_out, v_mlp_norm_g, v_w_mlp1, v_w_mlp2, v_final_norm_g)))
    xi, yi, ci = lax.axis_index("x"), lax.axis_index("y"), lax.axis_index("c")
    chip = 2 * xi + yi
    place = jnp.stack([ci, chip]).astype(jnp.int32)

    wsh = jnp.stack([_flatten_shard(w, l) for l in range(DEPTH)]).astype(CDT)
    wg = lax.dynamic_update_slice(allgather_weights(wsh), wsh[None], (chip, 0, 0, 0))
    taps = jnp.zeros((DEPTH, CONV_K, CONV_C), F32)
    taps = lax.dynamic_update_slice(taps, jnp.where(ci == 0, w["conv_w"], 0.0), (0, 0, chip * (CONV_C // N_CHIPS)))
    taps = allreduce_small(_tile_rows(taps)).reshape(-1)[:DEPTH * CONV_K * CONV_C].reshape(DEPTH, CONV_K, CONV_C)

    p = {n: [] for n in WEIGHTS if n != "final_norm_g"}
    for l in range(DEPTH):
        for n, mat in _unflatten_full(wg, l).items():
            p[n].append(mat)
        for n in ("mix_norm_g", "b_in", "conv_b", "conv_ln_g", "conv_ln_b", "b_conv_proj", "mlp_norm_g"):
            p[n].append(w[n][l].reshape(1, -1))
        p["sinks"].append(w["sinks"][l])
        p["conv_w"].append(taps[l])
    p["final_norm_g"] = w["final_norm_g"].reshape(1, D)

    loss, dx, gr = local_grads(x[0], loss_target[0], p)
    loss = lax.psum(loss[0, 0], ("x", "y", "c"))

    gflat = jnp.stack([_flatten_grads(gr, l) for l in range(DEPTH)])
    pb, own = pair_sum(gflat, swap_other_layer(gflat), place)
    tot = total_sum(own, exchange_partials(pb), place)
    other = share_totals(tot)
    both = jnp.where(ci == 0, jnp.stack([tot, other]), jnp.stack([other, tot]))
    gsh = _unflatten_shard(both, {n: w[n].shape for n in MATRICES})

    small_shapes = {n: w[n].shape[1:] for n in SMALL}
    small_shapes["conv_w"] = (CONV_K, CONV_C)
    gsmall = _unpack_small(allreduce_small(_pack_small(lambda n, l: gr[n][l], gr["final_norm_g"])), small_shapes)
    gsmall["conv_w"] = lax.dynamic_slice(gsmall["conv_w"], (0, 0, chip * (CONV_C // N_CHIPS)),
                                         (DEPTH, CONV_K, CONV_C // N_CHIPS))
    grads = {**gsh, **gsmall}

    delta, new_m, new_v = {}, {}, {}
    for n in MATRICES:
        shp = w[n].shape
        two_d = (shp[0] * shp[1], shp[2])
        d_, m_, v_ = adamw(w[n].reshape(two_d), grads[n].reshape(two_d), m[n].reshape(two_d), v[n].reshape(two_d),
                           name="adamw_" + n)
        delta[n], new_m[n], new_v[n] = d_.reshape(shp), m_.reshape(shp), v_.reshape(shp)
    small_shapes["conv_w"] = w["conv_w"].shape[1:]
    sm = [_pack_small(lambda n, l, t=t: t[n][l], t["final_norm_g"]) for t in (w, grads, m, v)]
    outs = adamw(*sm, name="adamw_small")
    for dst, packed in zip((delta, new_m, new_v), outs):
        dst.update(_unpack_small(packed, small_shapes))

    return (loss, dx[None], *[grads[n] for n in WEIGHTS], *[delta[n] for n in WEIGHTS],
            *[new_m[n] for n in WEIGHTS], *[new_v[n] for n in WEIGHTS])
```

```python
import functools
import math

import jax
import jax.numpy as jnp
import numpy as np
from jax import lax
from jax.experimental import pallas as pl
from jax.experimental.pallas import tpu as pltpu

F32 = jnp.float32
CDT = jnp.bfloat16

D = 1024
DEPTH = 2
N_Q = 8
HEAD_DIM = 64
ATTN_W = 512
KV_W = 128
BLOCK = 128
CONV_C = 512
CONV_K = 31
D_FF = 4096
IN_W = 3840
QKV_W = ATTN_W + 2 * KV_W
REST_W = IN_W - QKV_W
EPS = 1e-6
NEG = -1e30
SCALE = 1.0 / math.sqrt(HEAD_DIM)
SLOPES = [float(2.0 ** (-8.0 * (h + 1) / N_Q)) for h in range(N_Q)]
SUBLANES = 8
HALO = 32

ADAM_LR = 0.001
ADAM_B1 = 0.9
ADAM_B2 = 0.999
ADAM_EPS = 1e-08
ADAM_WD = 0.01
ADAM_STEP = 10

VMEM_LIMIT = 56 * 1024 * 1024


def _cp(*sem):
    return pltpu.CompilerParams(dimension_semantics=sem, vmem_limit_bytes=VMEM_LIMIT)


def _dot(a, b):
    return jnp.dot(a, b, preferred_element_type=F32)


def _dot_nt(a, b):
    return lax.dot_general(a, b, (((1,), (1,)), ((), ())), preferred_element_type=F32)


def _dot_tn(a, b):
    return lax.dot_general(a, b, (((0,), (0,)), ((), ())), preferred_element_type=F32)


def _sig(x):
    return 1.0 / (1.0 + jnp.exp(-x))


def _colsum(v):
    return jnp.sum(v, axis=0, keepdims=True)


def _const(shape, buffers=None):
    mode = {} if buffers is None else {"pipeline_mode": pl.Buffered(buffers)}
    return pl.BlockSpec(shape, lambda *_: (0,) * len(shape), **mode)


def rms_inproj(x, g, w, b, *, tm=512):
    T = x.shape[0]

    def body(x_ref, g_ref, w_ref, b_ref, qkv_ref, rest_ref):
        xv = x_ref[...]
        r = lax.rsqrt(jnp.mean(xv * xv, axis=-1, keepdims=True) + EPS)
        h = (xv * r * g_ref[...]).astype(CDT)
        qkv_ref[...] = (_dot(h, w_ref[:, 0:QKV_W]) + b_ref[:, 0:QKV_W]).astype(qkv_ref.dtype)
        for j in range(REST_W // D):
            c0 = QKV_W + D * j
            rest_ref[:, D * j:D * (j + 1)] = _dot(h, w_ref[:, c0:c0 + D]) + b_ref[:, c0:c0 + D]

    return pl.pallas_call(
        body, name="rms_inproj", grid=(T // tm,),
        in_specs=[pl.BlockSpec((tm, D), lambda i: (i, 0)), _const((1, D)), _const((D, IN_W), 1), _const((1, IN_W))],
        out_specs=[pl.BlockSpec((tm, QKV_W), lambda i: (i, 0)), pl.BlockSpec((tm, REST_W), lambda i: (i, 0))],
        out_shape=[jax.ShapeDtypeStruct((T, QKV_W), CDT), jax.ShapeDtypeStruct((T, REST_W), F32)],
        compiler_params=_cp("parallel"),
    )(x, g, w, b)


def _lane_halves(shape):
    lane = lax.broadcasted_iota(jnp.int32, shape, 1)
    return lane < HEAD_DIM, lane >= HEAD_DIM


def _swap_halves(v):
    return pltpu.roll(v.astype(F32), HEAD_DIM, axis=1).astype(v.dtype)


N_KV = KV_W // HEAD_DIM
GROUP = N_Q // N_KV
STACK = GROUP * BLOCK


def _attn_masks(first):
    row = lax.broadcasted_iota(jnp.int32, (STACK, 2 * BLOCK), 0) & (BLOCK - 1)
    col = lax.broadcasted_iota(jnp.int32, (STACK, 2 * BLOCK), 1)
    dist = row + BLOCK - col
    valid = (dist >= 0) & (dist < BLOCK) & ((col >= BLOCK) | jnp.logical_not(first))
    return valid, dist.astype(F32)


def _per_head_column(vals):
    row = lax.broadcasted_iota(jnp.int32, (STACK, 1), 0)
    col = jnp.full((STACK, 1), vals[GROUP - 1], F32)
    for i in reversed(range(GROUP - 1)):
        col = jnp.where(row < (i + 1) * BLOCK, vals[i], col)
    return col


def _stack_heads(dst, src_ref, r0, g):
    lane = lax.broadcasted_iota(jnp.int32, (BLOCK, 2 * HEAD_DIM), 1)
    keep = (lane >= HEAD_DIM) if g else (lane < HEAD_DIM)
    for i in range(GROUP):
        h = GROUP * g + i
        tile = src_ref[pl.ds(r0, BLOCK), (h // 2) * 128:(h // 2 + 1) * 128]
        if h % 2 != g:
            tile = _swap_halves(tile)
        dst[i * BLOCK:(i + 1) * BLOCK, :] = jnp.where(keep, tile, jnp.zeros_like(tile))


def _unstack_heads(dst_ref, stacked, r0, g):
    lane = lax.broadcasted_iota(jnp.int32, (BLOCK, 2 * HEAD_DIM), 1)
    for j in range(GROUP // 2):
        even = stacked[(2 * j) * BLOCK:(2 * j + 1) * BLOCK, :]
        odd = stacked[(2 * j + 1) * BLOCK:(2 * j + 2) * BLOCK, :]
        lo = _swap_halves(even) if g else even
        hi = odd if g else _swap_halves(odd)
        pair = (GROUP * g) // 2 + j
        dst_ref[pl.ds(r0, BLOCK), pair * 128:(pair + 1) * 128] = jnp.where(lane < HEAD_DIM, lo, hi).astype(dst_ref.dtype)


def _qkv_specs(tq):
    nb = tq // BLOCK
    return [
        pl.BlockSpec((tq, ATTN_W), lambda i: (i, 0)),
        pl.BlockSpec((BLOCK, KV_W), lambda i: (jnp.maximum(i * nb - 1, 0), ATTN_W // KV_W)),
        pl.BlockSpec((tq, KV_W), lambda i: (i, ATTN_W // KV_W)),
        pl.BlockSpec((BLOCK, KV_W), lambda i: (jnp.maximum(i * nb - 1, 0), ATTN_W // KV_W + 1)),
        pl.BlockSpec((tq, KV_W), lambda i: (i, ATTN_W // KV_W + 1)),
    ]


def attn_fwd(qkv, sinks, *, tq=512):
    T = qkv.shape[0]
    nb = tq // BLOCK

    def body(sink_ref, q_ref, kp_ref, kc_ref, vp_ref, vc_ref, o_ref, lse_ref, kext, vext, qs):
        i = pl.program_id(0)
        kext[0:BLOCK, :] = kp_ref[...]
        kext[BLOCK:, :] = kc_ref[...]
        vext[0:BLOCK, :] = vp_ref[...]
        vext[BLOCK:, :] = vc_ref[...]
        lane_l = lax.broadcasted_iota(jnp.int32, (BLOCK, 128), 1)

        def blk(b, carry):
            r0 = pl.multiple_of(b * BLOCK, BLOCK)
            valid, distf = _attn_masks(jnp.logical_and(i == 0, b == 0))
            kc = kext[pl.ds(r0, 2 * BLOCK), :]
            vc = vext[pl.ds(r0, 2 * BLOCK), :]
            lse_t = jnp.zeros((BLOCK, 128), F32)
            for g in range(N_KV):
                heads = range(GROUP * g, GROUP * (g + 1))
                _stack_heads(qs, q_ref, r0, g)
                s = _dot_nt(qs[...], kc) * SCALE - _per_head_column([SLOPES[h] for h in heads]) * distf
                s = jnp.where(valid, s, NEG)
                sink = _per_head_column([sink_ref[h] for h in heads])
                m = jnp.maximum(jnp.max(s, axis=-1, keepdims=True), sink)
                p = jnp.exp(s - m)
                denom = jnp.sum(p, axis=-1, keepdims=True) + jnp.exp(sink - m)
                p = p / denom
                _unstack_heads(o_ref, _dot(p.astype(CDT), vc), r0, g)
                lse = m + jnp.log(denom)
                for i_h, h in enumerate(heads):
                    lse_t = jnp.where(lane_l == h, lse[i_h * BLOCK:(i_h + 1) * BLOCK, :], lse_t)
            lse_ref[pl.ds(r0, BLOCK), :] = lse_t
            return carry

        lax.fori_loop(0, nb, blk, 0)

    return pl.pallas_call(
        body, name="attn_fwd", grid=(T // tq,),
        in_specs=[pl.BlockSpec(memory_space=pltpu.SMEM)] + _qkv_specs(tq),
        out_specs=[pl.BlockSpec((tq, ATTN_W), lambda i: (i, 0)), pl.BlockSpec((tq, 128), lambda i: (i, 0))],
        out_shape=[jax.ShapeDtypeStruct((T, ATTN_W), CDT), jax.ShapeDtypeStruct((T, 128), F32)],
        scratch_shapes=[pltpu.VMEM((tq + BLOCK, KV_W), CDT), pltpu.VMEM((tq + BLOCK, KV_W), CDT),
                        pltpu.VMEM((STACK, 2 * HEAD_DIM), CDT)],
        compiler_params=_cp("parallel"),
    )(sinks, qkv, qkv, qkv, qkv, qkv)


def _halo_before(tm, width, col):
    return pl.BlockSpec((HALO, width), lambda i: (jnp.maximum(i * (tm // HALO) - 1, 0), col))


def _fill_u0(ext, a_ref, b_ref, ha_ref, hb_ref, first):
    hu = ha_ref[...] * _sig(hb_ref[...])
    ext[0:HALO, :] = jnp.where(first, jnp.zeros_like(hu), hu)
    ext[HALO:, :] = a_ref[...] * _sig(b_ref[...])


def _shifted_taps(src, w_ref, base, rc, offsets):
    acc = jnp.zeros((rc, CONV_C), F32)
    for b in range(SUBLANES):
        taps = [(k, o - b) for k, o in enumerate(offsets) if o % SUBLANES == b]
        if not taps:
            continue
        rows = rc if b == 0 else rc + SUBLANES
        part = jnp.zeros((rows, CONV_C), F32)
        for k, o8 in taps:
            part = part + w_ref[k:k + 1, :] * src[base + o8:base + o8 + rows, :]
        acc = acc + (part if b == 0 else part[b:b + rc, :])
    return acc


def _conv_rows(ext, w_ref, r0, rc):
    return _shifted_taps(ext, w_ref, r0, rc, [HALO - (CONV_K - 1) + k for k in range(CONV_K)])


def _layer_norm(u1, g, b):
    mu = jnp.mean(u1, axis=-1, keepdims=True)
    xc = u1 - mu
    rstd = lax.rsqrt(jnp.mean(xc * xc, axis=-1, keepdims=True) + EPS)
    n = xc * rstd
    return n, rstd, n * g + b


CONV_RC = 32


def conv_fwd(rest, cw, cb, lg, lb, *, tm=512):
    T = rest.shape[0]

    def body(a_ref, b_ref, ha_ref, hb_ref, w_ref, cb_ref, lg_ref, lb_ref, o_ref, u1_ref, ext):
        _fill_u0(ext, a_ref, b_ref, ha_ref, hb_ref, pl.program_id(0) == 0)
        for c in range(tm // CONV_RC):
            r0 = c * CONV_RC
            u1 = _conv_rows(ext, w_ref, r0, CONV_RC) + cb_ref[...]
            u1_ref[r0:r0 + CONV_RC, :] = u1
            _, _, u2 = _layer_norm(u1, lg_ref[...], lb_ref[...])
            o_ref[r0:r0 + CONV_RC, :] = (u2 * _sig(u2)).astype(o_ref.dtype)

    row = lambda i: (i, 0)
    return pl.pallas_call(
        body, name="conv_fwd", grid=(T // tm,),
        in_specs=[pl.BlockSpec((tm, CONV_C), row), pl.BlockSpec((tm, CONV_C), lambda i: (i, 1)),
                  _halo_before(tm, CONV_C, 0), _halo_before(tm, CONV_C, 1),
                  _const((CONV_K, CONV_C)), _const((1, CONV_C)), _const((1, CONV_C)), _const((1, CONV_C))],
        out_specs=[pl.BlockSpec((tm, CONV_C), row), pl.BlockSpec((tm, CONV_C), row)],
        out_shape=[jax.ShapeDtypeStruct((T, CONV_C), CDT), jax.ShapeDtypeStruct((T, CONV_C), F32)],
        scratch_shapes=[pltpu.VMEM((tm + HALO, CONV_C), F32)],
        compiler_params=_cp("parallel"),
    )(rest, rest, rest, rest, cw, cb, lg, lb)


def merge_out(x, attn, u3, rest, wa, wc, bc, wo, *, tm=512):
    T = x.shape[0]

    def body(x_ref, at_ref, u_ref, ga_ref, gc_ref, wa_ref, wc_ref, bc_ref, wo_ref, o_ref):
        br_a = _dot(at_ref[...], wa_ref[...])
        br_c = _dot(u_ref[...], wc_ref[...]) + bc_ref[...]
        merged = _sig(ga_ref[...]) * br_a + _sig(gc_ref[...]) * br_c
        o_ref[...] = x_ref[...] + _dot(merged.astype(CDT), wo_ref[...])

    return pl.pallas_call(
        body, name="merge_out", grid=(T // tm,),
        in_specs=[pl.BlockSpec((tm, D), lambda i: (i, 0)), pl.BlockSpec((tm, ATTN_W), lambda i: (i, 0)),
                  pl.BlockSpec((tm, CONV_C), lambda i: (i, 0)),
                  pl.BlockSpec((tm, D), lambda i: (i, 1)), pl.BlockSpec((tm, D), lambda i: (i, 2)),
                  _const((ATTN_W, D), 1), _const((CONV_C, D), 1), _const((1, D)), _const((D, D), 1)],
        out_specs=pl.BlockSpec((tm, D), lambda i: (i, 0)),
        out_shape=jax.ShapeDtypeStruct((T, D), F32),
        compiler_params=_cp("parallel"),
    )(x, attn, u3, rest, rest, wa, wc, bc, wo)


def mlp_fwd(x, g, w1, w2, *, tm=256, tf=D_FF):
    T = x.shape[0]
    nf = D_FF // tf

    def body(x_ref, g_ref, w1_ref, w2_ref, o_ref, pre_ref, h_s, acc_s):
        f = pl.program_id(1)

        @pl.when(f == 0)
        def _():
            xv = x_ref[...]
            r = lax.rsqrt(jnp.mean(xv * xv, axis=-1, keepdims=True) + EPS)
            h_s[...] = (xv * r * g_ref[...]).astype(CDT)
            acc_s[...] = jnp.zeros_like(acc_s)

        pre = _dot(h_s[...], w1_ref[...])
        pre_ref[...] = pre
        a = jnp.square(jnp.maximum(pre, 0.0))
        acc_s[...] += _dot(a.astype(CDT), w2_ref[...])

        @pl.when(f == nf - 1)
        def _():
            o_ref[...] = x_ref[...] + acc_s[...]

    mode = {"pipeline_mode": pl.Buffered(1)} if nf == 1 else {}
    return pl.pallas_call(
        body, name="mlp_fwd", grid=(T // tm, nf),
        in_specs=[pl.BlockSpec((tm, D), lambda i, f: (i, 0)), _const((1, D)),
                  pl.BlockSpec((D, tf), lambda i, f: (0, f), **mode), pl.BlockSpec((tf, D), lambda i, f: (f, 0), **mode)],
        out_specs=[pl.BlockSpec((tm, D), lambda i, f: (i, 0)), pl.BlockSpec((tm, tf), lambda i, f: (i, f))],
        out_shape=[jax.ShapeDtypeStruct((T, D), F32), jax.ShapeDtypeStruct((T, D_FF), F32)],
        scratch_shapes=[pltpu.VMEM((tm, D), CDT), pltpu.VMEM((tm, D), F32)],
        compiler_params=_cp("parallel", "arbitrary"),
    )(x, g, w1, w2)


def _rms_bwd(xv, g, dh):
    r = lax.rsqrt(jnp.mean(xv * xv, axis=-1, keepdims=True) + EPS)
    xhat = xv * r
    dxh = dh * g
    dx = r * (dxh - xhat * jnp.mean(dxh * xhat, axis=-1, keepdims=True))
    return dx, dh * xhat


def loss_head(x, g, tgt, *, tm=512):
    T = x.shape[0]

    def body(x_ref, g_ref, t_ref, dx_ref, dg_ref, loss_ref):
        @pl.when(pl.program_id(0) == 0)
        def _():
            dg_ref[...] = jnp.zeros_like(dg_ref)
            loss_ref[...] = jnp.zeros_like(loss_ref)

        xv = x_ref[...]
        gv = g_ref[...]
        r = lax.rsqrt(jnp.mean(xv * xv, axis=-1, keepdims=True) + EPS)
        e = xv * r * gv - t_ref[...]
        loss_ref[...] += 0.5 * jnp.sum(jnp.mean(e * e, axis=-1, keepdims=True), axis=0, keepdims=True)
        dx, dg_rows = _rms_bwd(xv, gv, e * (1.0 / D))
        dx_ref[...] = dx
        dg_ref[...] += _colsum(dg_rows)

    return pl.pallas_call(
        body, name="loss_head", grid=(T // tm,),
        in_specs=[pl.BlockSpec((tm, D), lambda i: (i, 0)), _const((1, D)), pl.BlockSpec((tm, D), lambda i: (i, 0))],
        out_specs=[pl.BlockSpec((tm, D), lambda i: (i, 0)), _const((1, D)), _const((1, 128))],
        out_shape=[jax.ShapeDtypeStruct((T, D), F32), jax.ShapeDtypeStruct((1, D), F32),
                   jax.ShapeDtypeStruct((1, 128), F32)],
        compiler_params=_cp("arbitrary"),
    )(x, g, tgt)


def mlp_bwd(dy, x, g, pre, w1, w2, *, tm=256, tf=D_FF):
    T = x.shape[0]
    nf = D_FF // tf

    def body(dy_ref, x_ref, g_ref, pre_ref, w1_ref, w2_ref, dx_ref, dg_ref, h_ref, a_ref, dpre_ref, dyb_s, acc_s):
        i, f = pl.program_id(0), pl.program_id(1)

        @pl.when(jnp.logical_and(i == 0, f == 0))
        def _():
            dg_ref[...] = jnp.zeros_like(dg_ref)

        @pl.when(f == 0)
        def _():
            dyb_s[...] = dy_ref[...].astype(CDT)
            acc_s[...] = jnp.zeros_like(acc_s)

        pre = pre_ref[...]
        rl = jnp.maximum(pre, 0.0)
        a_ref[...] = (rl * rl).astype(CDT)
        da = _dot_nt(dyb_s[...], w2_ref[...])
        dpre = (da * (2.0 * rl)).astype(CDT)
        dpre_ref[...] = dpre
        acc_s[...] += _dot_nt(dpre, w1_ref[...])

        @pl.when(f == nf - 1)
        def _():
            xv = x_ref[...]
            gv = g_ref[...]
            dxn, dg_rows = _rms_bwd(xv, gv, acc_s[...])
            dx_ref[...] = dy_ref[...] + dxn
            dg_ref[...] += _colsum(dg_rows)
            r = lax.rsqrt(jnp.mean(xv * xv, axis=-1, keepdims=True) + EPS)
            h_ref[...] = (xv * r * gv).astype(CDT)

    row = lambda i, f: (i, 0)
    mode = {"pipeline_mode": pl.Buffered(1)} if nf == 1 else {}
    return pl.pallas_call(
        body, name="mlp_bwd", grid=(T // tm, nf),
        in_specs=[pl.BlockSpec((tm, D), row), pl.BlockSpec((tm, D), row), _const((1, D)),
                  pl.BlockSpec((tm, tf), lambda i, f: (i, f)),
                  pl.BlockSpec((D, tf), lambda i, f: (0, f), **mode), pl.BlockSpec((tf, D), lambda i, f: (f, 0), **mode)],
        out_specs=[pl.BlockSpec((tm, D), row), _const((1, D)), pl.BlockSpec((tm, D), row),
                   pl.BlockSpec((tm, tf), lambda i, f: (i, f)), pl.BlockSpec((tm, tf), lambda i, f: (i, f))],
        out_shape=[jax.ShapeDtypeStruct((T, D), F32), jax.ShapeDtypeStruct((1, D), F32),
                   jax.ShapeDtypeStruct((T, D), CDT), jax.ShapeDtypeStruct((T, D_FF), CDT),
                   jax.ShapeDtypeStruct((T, D_FF), CDT)],
        scratch_shapes=[pltpu.VMEM((tm, D), CDT), pltpu.VMEM((tm, D), F32)],
        compiler_params=_cp("arbitrary", "arbitrary"),
    )(dy, x, g, pre, w1, w2)


def tn_matmul(a, b, *, tm, tn, tk=2048, name):
    T, M = a.shape
    N = b.shape[1]
    tk = min(tk, T)
    nk = T // tk

    def body(a_ref, b_ref, o_ref):
        @pl.when(pl.program_id(2) == 0)
        def _():
            o_ref[...] = jnp.zeros_like(o_ref)

        o_ref[...] += _dot_tn(a_ref[...].astype(CDT), b_ref[...].astype(CDT))

    return pl.pallas_call(
        body, name=name, grid=(M // tm, N // tn, nk),
        in_specs=[pl.BlockSpec((tk, tm), lambda i, j, k: (k, i)), pl.BlockSpec((tk, tn), lambda i, j, k: (k, j))],
        out_specs=pl.BlockSpec((tm, tn), lambda i, j, k: (i, j)),
        out_shape=jax.ShapeDtypeStruct((M, N), F32),
        compiler_params=_cp("parallel", "parallel", "arbitrary"),
    )(a, b)


def merge_bwd(dx1, attn, u3, rest, wa, wc, bc, wo, *, tm=256):
    T = dx1.shape[0]

    def body(dx_ref, at_ref, u_ref, ga_ref, gc_ref, wa_ref, wc_ref, bc_ref, wo_ref,
             mg_ref, dba_ref, dbc_ref, dat_ref, du_ref, dgate_ref, dbias_ref):
        @pl.when(pl.program_id(0) == 0)
        def _():
            dbias_ref[...] = jnp.zeros_like(dbias_ref)

        br_a = _dot(at_ref[...], wa_ref[...])
        br_c = _dot(u_ref[...], wc_ref[...]) + bc_ref[...]
        sa = _sig(ga_ref[...])
        sc = _sig(gc_ref[...])
        mg_ref[...] = (sa * br_a + sc * br_c).astype(CDT)
        dm = _dot_nt(dx_ref[...].astype(CDT), wo_ref[...])
        dba = dm * sa
        dbc = dm * sc
        dgate_ref[:, 0:D] = dm * br_a * sa * (1.0 - sa)
        dgate_ref[:, D:2 * D] = dm * br_c * sc * (1.0 - sc)
        dbias_ref[...] += _colsum(dbc)
        dba_b = dba.astype(CDT)
        dbc_b = dbc.astype(CDT)
        dba_ref[...] = dba_b
        dbc_ref[...] = dbc_b
        dat_ref[...] = _dot_nt(dba_b, wa_ref[...]).astype(CDT)
        du_ref[...] = _dot_nt(dbc_b, wc_ref[...])

    row = lambda i: (i, 0)
    return pl.pallas_call(
        body, name="merge_bwd", grid=(T // tm,),
        in_specs=[pl.BlockSpec((tm, D), row), pl.BlockSpec((tm, ATTN_W), row), pl.BlockSpec((tm, CONV_C), row),
                  pl.BlockSpec((tm, D), lambda i: (i, 1)), pl.BlockSpec((tm, D), lambda i: (i, 2)),
                  _const((ATTN_W, D), 1), _const((CONV_C, D), 1), _const((1, D)), _const((D, D), 1)],
        out_specs=[pl.BlockSpec((tm, D), row), pl.BlockSpec((tm, D), row), pl.BlockSpec((tm, D), row),
                   pl.BlockSpec((tm, ATTN_W), row), pl.BlockSpec((tm, CONV_C), row),
                   pl.BlockSpec((tm, 2 * D), row), _const((1, D))],
        out_shape=[jax.ShapeDtypeStruct((T, D), CDT), jax.ShapeDtypeStruct((T, D), CDT),
                   jax.ShapeDtypeStruct((T, D), CDT), jax.ShapeDtypeStruct((T, ATTN_W), CDT),
                   jax.ShapeDtypeStruct((T, CONV_C), F32), jax.ShapeDtypeStruct((T, 2 * D), F32),
                   jax.ShapeDtypeStruct((1, D), F32)],
        compiler_params=_cp("arbitrary"),
    )(dx1, attn, u3, rest, rest, wa, wc, bc, wo)


def conv_bwd_ln(du3, u1, lg, lb, *, tm=512):
    T = du3.shape[0]

    def body(du_ref, u1_ref, lg_ref, lb_ref, du1_ref, dlg_ref, dlb_ref, dcb_ref):
        @pl.when(pl.program_id(0) == 0)
        def _():
            dlg_ref[...] = jnp.zeros_like(dlg_ref)
            dlb_ref[...] = jnp.zeros_like(dlb_ref)
            dcb_ref[...] = jnp.zeros_like(dcb_ref)

        dlg = jnp.zeros((1, CONV_C), F32)
        dlb = jnp.zeros((1, CONV_C), F32)
        dcb = jnp.zeros((1, CONV_C), F32)
        for c in range(tm // CONV_RC):
            r0 = c * CONV_RC
            n, rstd, u2 = _layer_norm(u1_ref[r0:r0 + CONV_RC, :], lg_ref[...], lb_ref[...])
            s = _sig(u2)
            du2 = du_ref[r0:r0 + CONV_RC, :] * (s + u2 * s * (1.0 - s))
            dn = du2 * lg_ref[...]
            du1 = rstd * (dn - jnp.mean(dn, axis=-1, keepdims=True) - n * jnp.mean(dn * n, axis=-1, keepdims=True))
            du1_ref[r0:r0 + CONV_RC, :] = du1
            dlg = dlg + _colsum(du2 * n)
            dlb = dlb + _colsum(du2)
            dcb = dcb + _colsum(du1)
        dlg_ref[...] += dlg
        dlb_ref[...] += dlb
        dcb_ref[...] += dcb

    row = lambda i: (i, 0)
    vec = jax.ShapeDtypeStruct((1, CONV_C), F32)
    return pl.pallas_call(
        body, name="conv_bwd_ln", grid=(T // tm,),
        in_specs=[pl.BlockSpec((tm, CONV_C), row), pl.BlockSpec((tm, CONV_C), row), _const((1, CONV_C)), _const((1, CONV_C))],
        out_specs=[pl.BlockSpec((tm, CONV_C), row), _const((1, CONV_C)), _const((1, CONV_C)), _const((1, CONV_C))],
        out_shape=[jax.ShapeDtypeStruct((T, CONV_C), F32), vec, vec, vec],
        compiler_params=_cp("arbitrary"),
    )(du3, u1, lg, lb)


def conv_bwd_taps(du1, rest, cw, *, tm=512):
    T = du1.shape[0]
    nt = T // tm

    def body(d_ref, hd_ref, a_ref, b_ref, ha_ref, hb_ref, w_ref, dglu_ref, dw_ref, ext, dext, dwacc):
        i = pl.program_id(0)

        @pl.when(i == 0)
        def _():
            dwacc[...] = jnp.zeros_like(dwacc)

        _fill_u0(ext, a_ref, b_ref, ha_ref, hb_ref, i == 0)
        dext[0:SUBLANES, :] = jnp.zeros((SUBLANES, CONV_C), F32)
        dext[SUBLANES:SUBLANES + tm, :] = d_ref[...]
        hd = hd_ref[...]
        dext[SUBLANES + tm:, :] = jnp.where(i == nt - 1, jnp.zeros_like(hd), hd)
        qrow = lax.broadcasted_iota(jnp.int32, (CONV_RC + SUBLANES, CONV_C), 0)
        for c in range(tm // CONV_RC):
            r0 = c * CONV_RC
            du0 = _shifted_taps(dext, w_ref, r0 + SUBLANES, CONV_RC, [CONV_K - 1 - k for k in range(CONV_K)])
            for b in range(SUBLANES):
                taps = [(k, HALO - (CONV_K - 1) + k - b) for k in range(CONV_K) if (HALO - (CONV_K - 1) + k) % SUBLANES == b]
                if b == 0:
                    rows = CONV_RC
                    dsh = dext[r0 + SUBLANES:r0 + SUBLANES + rows, :]
                else:
                    rows = CONV_RC + SUBLANES
                    dsh = dext[r0 + SUBLANES - b:r0 + SUBLANES - b + rows, :]
                    dsh = jnp.where((qrow >= b) & (qrow < CONV_RC + b), dsh, 0.0)
                for k, o8 in taps:
                    prod = dsh * ext[r0 + o8:r0 + o8 + rows, :]
                    dwacc[8 * k:8 * k + 8, :] += jnp.sum(prod.reshape(rows // SUBLANES, SUBLANES, CONV_C), axis=0)
            av = a_ref[r0:r0 + CONV_RC, :]
            sb = _sig(b_ref[r0:r0 + CONV_RC, :])
            dglu_ref[r0:r0 + CONV_RC, 0:CONV_C] = du0 * sb
            dglu_ref[r0:r0 + CONV_RC, CONV_C:2 * CONV_C] = du0 * av * sb * (1.0 - sb)

        @pl.when(i == nt - 1)
        def _():
            dw_ref[...] = jnp.zeros_like(dw_ref)
            for k in range(CONV_K):
                dw_ref[k:k + 1, :] = _colsum(dwacc[8 * k:8 * k + 8, :])

    row = lambda i: (i, 0)
    return pl.pallas_call(
        body, name="conv_bwd_taps", grid=(nt,),
        in_specs=[pl.BlockSpec((tm, CONV_C), row),
                  pl.BlockSpec((HALO, CONV_C), lambda i: (jnp.minimum((i + 1) * (tm // HALO), T // HALO - 1), 0)),
                  pl.BlockSpec((tm, CONV_C), row), pl.BlockSpec((tm, CONV_C), lambda i: (i, 1)),
                  _halo_before(tm, CONV_C, 0), _halo_before(tm, CONV_C, 1), _const((CONV_K, CONV_C))],
        out_specs=[pl.BlockSpec((tm, 2 * CONV_C), row), _const((HALO, CONV_C))],
        out_shape=[jax.ShapeDtypeStruct((T, 2 * CONV_C), F32), jax.ShapeDtypeStruct((HALO, CONV_C), F32)],
        scratch_shapes=[pltpu.VMEM((tm + HALO, CONV_C), F32), pltpu.VMEM((SUBLANES + tm + HALO, CONV_C), F32),
                        pltpu.VMEM((8 * CONV_K, CONV_C), F32)],
        compiler_params=_cp("arbitrary"),
    )(du1, du1, rest, rest, rest, rest, cw)


def attn_bwd(qkv, do, lse, sinks, *, tq=512):
    T = qkv.shape[0]
    nb = tq // BLOCK

    def body(sink_ref, q_ref, kp_ref, kc_ref, vp_ref, vc_ref, do_ref, lse_ref,
             dq_ref, dkv_ref, spill_ref, dsink_ref, kext, vext, dkext, dvext, qs, dos):
        i = pl.program_id(0)

        @pl.when(i == 0)
        def _():
            dsink_ref[...] = jnp.zeros_like(dsink_ref)

        kext[0:BLOCK, :] = kp_ref[...]
        kext[BLOCK:, :] = kc_ref[...]
        vext[0:BLOCK, :] = vp_ref[...]
        vext[BLOCK:, :] = vc_ref[...]
        dkext[...] = jnp.zeros_like(dkext)
        dvext[...] = jnp.zeros_like(dvext)
        lane_l = lax.broadcasted_iota(jnp.int32, (BLOCK, 128), 1)
        lane_k = lax.broadcasted_iota(jnp.int32, (2 * BLOCK, KV_W), 1)

        def blk(b, dsink):
            r0 = pl.multiple_of(b * BLOCK, BLOCK)
            valid, distf = _attn_masks(jnp.logical_and(i == 0, b == 0))
            kc = kext[pl.ds(r0, 2 * BLOCK), :]
            vc = vext[pl.ds(r0, 2 * BLOCK), :]
            lse_t = lse_ref[pl.ds(r0, BLOCK), :]
            dk = jnp.zeros((2 * BLOCK, KV_W), F32)
            dv = jnp.zeros((2 * BLOCK, KV_W), F32)
            for g in range(N_KV):
                heads = range(GROUP * g, GROUP * (g + 1))
                _stack_heads(qs, q_ref, r0, g)
                _stack_heads(dos, do_ref, r0, g)
                qv = qs[...]
                dov = dos[...]
                s = _dot_nt(qv, kc) * SCALE - _per_head_column([SLOPES[h] for h in heads]) * distf
                s = jnp.where(valid, s, NEG)
                lse = jnp.concatenate(
                    [jnp.sum(jnp.where(lane_l == h, lse_t, 0.0), axis=-1, keepdims=True) for h in heads], axis=0)
                p = jnp.exp(s - lse)
                dp = _dot_nt(dov, vc)
                dd = jnp.sum(p * dp, axis=-1, keepdims=True)
                ds = (p * (dp - dd)).astype(CDT)
                keep = (lane_k >= HEAD_DIM) if g else (lane_k < HEAD_DIM)
                _unstack_heads(dq_ref, _dot(ds, jnp.where(keep, kc, jnp.zeros_like(kc))) * SCALE, r0, g)
                dk = dk + _dot_tn(ds, qv)
                dv = dv + _dot_tn(p.astype(CDT), dov)
                wsink = jnp.exp(_per_head_column([sink_ref[h] for h in heads]) - lse) * dd
                for i_h, h in enumerate(heads):
                    part = jnp.sum(wsink[i_h * BLOCK:(i_h + 1) * BLOCK, :], axis=0, keepdims=True)
                    dsink = dsink - jnp.where(lane_l[0:1, :] == h, part, 0.0)
            dkext[pl.ds(r0, 2 * BLOCK), :] += dk * SCALE
            dvext[pl.ds(r0, 2 * BLOCK), :] += dv
            return dsink

        dsink_ref[...] += lax.fori_loop(0, nb, blk, jnp.zeros((1, 128), F32))
        dkv_ref[:, 0:KV_W] = dkext[BLOCK:, :]
        dkv_ref[:, KV_W:2 * KV_W] = dvext[BLOCK:, :]
        spill_ref[:, 0:KV_W] = dkext[0:BLOCK, :]
        spill_ref[:, KV_W:2 * KV_W] = dvext[0:BLOCK, :]

    row = lambda i: (i, 0)
    return pl.pallas_call(
        body, name="attn_bwd", grid=(T // tq,),
        in_specs=[pl.BlockSpec(memory_space=pltpu.SMEM)] + _qkv_specs(tq)
        + [pl.BlockSpec((tq, ATTN_W), row), pl.BlockSpec((tq, 128), row)],
        out_specs=[pl.BlockSpec((tq, ATTN_W), row), pl.BlockSpec((tq, 2 * KV_W), row),
                   pl.BlockSpec((BLOCK, 2 * KV_W), row), _const((1, 128))],
        out_shape=[jax.ShapeDtypeStruct((T, ATTN_W), F32), jax.ShapeDtypeStruct((T, 2 * KV_W), F32),
                   jax.ShapeDtypeStruct((T // tq * BLOCK, 2 * KV_W), F32), jax.ShapeDtypeStruct((1, 128), F32)],
        scratch_shapes=[pltpu.VMEM((tq + BLOCK, KV_W), CDT), pltpu.VMEM((tq + BLOCK, KV_W), CDT),
                        pltpu.VMEM((tq + BLOCK, KV_W), F32), pltpu.VMEM((tq + BLOCK, KV_W), F32),
                        pltpu.VMEM((STACK, 2 * HEAD_DIM), CDT), pltpu.VMEM((STACK, 2 * HEAD_DIM), CDT)],
        compiler_params=_cp("arbitrary"),
    )(sinks, qkv, qkv, qkv, qkv, qkv, do, lse)


def inproj_bwd(dres, x, g, w, dq, dkv, spill, dglu, dgate, *, tm=512):
    T = x.shape[0]
    nt = T // tm

    def body(dr_ref, x_ref, g_ref, w_ref, dq_ref, dkv_ref, sp_ref, dglu_ref, dgate_ref,
             dx_ref, dp_ref, h_ref, dg_ref, db_ref):
        i = pl.program_id(0)

        @pl.when(i == 0)
        def _():
            dg_ref[...] = jnp.zeros_like(dg_ref)
            db_ref[...] = jnp.zeros_like(db_ref)

        sp = sp_ref[...]
        sp = jnp.where(i == nt - 1, jnp.zeros_like(sp), sp)
        pieces = ((0, ATTN_W, dq_ref), (QKV_W, 2 * CONV_C, dglu_ref), (QKV_W + 2 * CONV_C, 2 * D, dgate_ref))
        for c0, wd, ref in pieces:
            v = ref[...]
            db_ref[:, c0:c0 + wd] += _colsum(v)
            dp_ref[:, c0:c0 + wd] = v.astype(CDT)
        dkv = dkv_ref[...]
        db_ref[:, ATTN_W:QKV_W] += _colsum(dkv) + _colsum(sp)
        dp_ref[0:tm - BLOCK, ATTN_W:QKV_W] = dkv[0:tm - BLOCK, :].astype(CDT)
        dp_ref[tm - BLOCK:tm, ATTN_W:QKV_W] = (dkv[tm - BLOCK:tm, :] + sp).astype(CDT)
        dh = _dot_nt(dp_ref[...], w_ref[...])
        xv = x_ref[...]
        gv = g_ref[...]
        dxn, dg_rows = _rms_bwd(xv, gv, dh)
        dx_ref[...] = dr_ref[...] + dxn
        dg_ref[...] += _colsum(dg_rows)
        r = lax.rsqrt(jnp.mean(xv * xv, axis=-1, keepdims=True) + EPS)
        h_ref[...] = (xv * r * gv).astype(CDT)

    row = lambda i: (i, 0)
    return pl.pallas_call(
        body, name="inproj_bwd", grid=(nt,),
        in_specs=[pl.BlockSpec((tm, D), row), pl.BlockSpec((tm, D), row), _const((1, D)), _const((D, IN_W), 1),
                  pl.BlockSpec((tm, ATTN_W), row), pl.BlockSpec((tm, 2 * KV_W), row),
                  pl.BlockSpec((BLOCK, 2 * KV_W), lambda i: (jnp.minimum(i + 1, nt - 1), 0)),
                  pl.BlockSpec((tm, 2 * CONV_C), row), pl.BlockSpec((tm, 2 * D), row)],
        out_specs=[pl.BlockSpec((tm, D), row), pl.BlockSpec((tm, IN_W), row), pl.BlockSpec((tm, D), row),
                   _const((1, D)), _const((1, IN_W))],
        out_shape=[jax.ShapeDtypeStruct((T, D), F32), jax.ShapeDtypeStruct((T, IN_W), CDT),
                   jax.ShapeDtypeStruct((T, D), CDT), jax.ShapeDtypeStruct((1, D), F32),
                   jax.ShapeDtypeStruct((1, IN_W), F32)],
        compiler_params=_cp("arbitrary"),
    )(dres, x, g, w, dq, dkv, spill, dglu, dgate)


ATTN_TILE = 256


def local_grads(x, tgt, p):
    saved = []
    for l in range(DEPTH):
        qkv, rest = rms_inproj(x, p["mix_norm_g"][l], p["w_in"][l], p["b_in"][l])
        attn, lse = attn_fwd(qkv, p["sinks"][l], tq=ATTN_TILE)
        u3, u1 = conv_fwd(rest, p["conv_w"][l], p["conv_b"][l], p["conv_ln_g"][l], p["conv_ln_b"][l])
        x1 = merge_out(x, attn, u3, rest, p["w_attn_proj"][l], p["w_conv_proj"][l], p["b_conv_proj"][l], p["w_out"][l])
        x2, pre = mlp_fwd(x1, p["mlp_norm_g"][l], p["w_mlp1"][l], p["w_mlp2"][l])
        saved.append((x, qkv, rest, attn, lse, u3, u1, x1, pre))
        x = x2
    dx, dgf, loss = loss_head(x, p["final_norm_g"], tgt)
    names = ("mix_norm_g", "w_in", "b_in", "sinks", "conv_w", "conv_b", "conv_ln_g", "conv_ln_b", "w_attn_proj",
             "w_conv_proj", "b_conv_proj", "w_out", "mlp_norm_g", "w_mlp1", "w_mlp2")
    grads = {n: [None] * DEPTH for n in names}
    grads["final_norm_g"] = dgf
    for l in reversed(range(DEPTH)):
        x0, qkv, rest, attn, lse, u3, u1, x1, pre = saved[l]
        dx1, dg2, h2, a, dpre = mlp_bwd(dx, x1, p["mlp_norm_g"][l], pre, p["w_mlp1"][l], p["w_mlp2"][l])
        grads["mlp_norm_g"][l] = dg2
        grads["w_mlp1"][l] = tn_matmul(h2, dpre, tm=1024, tn=1024, tk=4096, name="tn_mlp1")
        grads["w_mlp2"][l] = tn_matmul(a, dx, tm=1024, tn=1024, name="tn_mlp2")
        merged, dba, dbc, dattn, du3, dgate, dbcp = merge_bwd(
            dx1, attn, u3, rest, p["w_attn_proj"][l], p["w_conv_proj"][l], p["b_conv_proj"][l], p["w_out"][l])
        grads["b_conv_proj"][l] = dbcp
        grads["w_out"][l] = tn_matmul(merged, dx1, tm=1024, tn=1024, name="tn_out")
        grads["w_attn_proj"][l] = tn_matmul(attn, dba, tm=512, tn=1024, tk=4096, name="tn_attn_proj")
        grads["w_conv_proj"][l] = tn_matmul(u3, dbc, tm=512, tn=1024, tk=4096, name="tn_conv_proj")
        du1, dlg, dlb, dcb = conv_bwd_ln(du3, u1, p["conv_ln_g"][l], p["conv_ln_b"][l])
        grads["conv_ln_g"][l], grads["conv_ln_b"][l], grads["conv_b"][l] = dlg, dlb, dcb
        dglu, dcw = conv_bwd_taps(du1, rest, p["conv_w"][l])
        grads["conv_w"][l] = dcw[0:CONV_K]
        dq, dkv, spill, dsink = attn_bwd(qkv, dattn, lse, p["sinks"][l], tq=ATTN_TILE)
        grads["sinks"][l] = dsink[0, 0:N_Q]
        dx, dproj, h, dg, db = inproj_bwd(dx1, x0, p["mix_norm_g"][l], p["w_in"][l], dq, dkv, spill, dglu, dgate,
                                          tm=ATTN_TILE)
        grads["mix_norm_g"][l], grads["b_in"][l] = dg, db
        grads["w_in"][l] = tn_matmul(h, dproj, tm=1024, tn=768, tk=4096, name="tn_in")
    return loss, dx, grads


MESH = pl.DeviceIdType.MESH
N_CHIPS = 4
N_DEV = 8
FLAT_W = 1024
FLAT_PARTS = (("w_in", 960), ("w_attn_proj", 128), ("w_conv_proj", 128), ("w_out", 256), ("w_mlp1", 1024), ("w_mlp2", 1024))
FLAT_ROWS = sum(r for _, r in FLAT_PARTS)
FLAT_TILE = 704


def _place():
    x, y, c = lax.axis_index("x"), lax.axis_index("y"), lax.axis_index("c")
    return x, y, c, 2 * x + y


def _peer_chips(x, y, j):
    return [((x, 1 - y), j ^ 1), ((1 - x, y), j ^ 2), ((1 - x, 1 - y), j ^ 3)]


def _any():
    return pl.BlockSpec(memory_space=pl.ANY)


def allgather_weights(wsh):
    shape = (N_CHIPS,) + wsh.shape

    def body(w_ref, out_ref, send_sems, recv_sems):
        x, y, c, j = _place()
        peers = _peer_chips(x, y, j)

        def remote(src, dst, k, to):
            return pltpu.make_async_remote_copy(src_ref=src, dst_ref=dst, send_sem=send_sems.at[k],
                                                recv_sem=recv_sems.at[k], device_id=to, device_id_type=MESH)

        first = [remote(w_ref.at[c], out_ref.at[j, c], k, (*chip, c)) for k, (chip, _) in enumerate(peers)]
        for cp in first:
            cp.start()
        passed = [remote(out_ref.at[pj, c], out_ref.at[pj, c], 3 + k, (x, y, 1 - c)) for k, (_, pj) in enumerate(peers)]
        for k, (_, pj) in enumerate(peers):
            remote(w_ref.at[c], out_ref.at[pj, c], k, (x, y, c)).wait_recv()
            passed[k].start()
        for k, (_, pj) in enumerate(peers):
            remote(w_ref.at[c], out_ref.at[pj, 1 - c], 3 + k, (x, y, c)).wait_recv()
        for cp in first + passed:
            cp.wait_send()

    return pl.pallas_call(
        body, name="allgather_weights", out_shape=jax.ShapeDtypeStruct(shape, wsh.dtype),
        in_specs=[_any()], out_specs=_any(),
        scratch_shapes=[pltpu.SemaphoreType.DMA((6,)), pltpu.SemaphoreType.DMA((6,))],
    )(wsh)


def swap_other_layer(g):
    def body(g_ref, got_ref, send_sem, recv_sem):
        x, y, c, _ = _place()
        cp = pltpu.make_async_remote_copy(src_ref=g_ref.at[1 - c], dst_ref=got_ref, send_sem=send_sem, recv_sem=recv_sem,
                                          device_id=(x, y, 1 - c), device_id_type=MESH)
        cp.start()
        cp.wait()

    return pl.pallas_call(
        body, name="swap_other_layer", out_shape=jax.ShapeDtypeStruct(g.shape[1:], g.dtype),
        in_specs=[_any()], out_specs=_any(),
        scratch_shapes=[pltpu.SemaphoreType.DMA, pltpu.SemaphoreType.DMA],
    )(g)


def pair_sum(g, got, place):
    _, nj, R, W = g.shape

    def body(s_ref, g_ref, got_ref, pb_ref, own_ref):
        v = g_ref[...] + got_ref[...]
        pb_ref[...] = v.astype(pb_ref.dtype)

        @pl.when(pl.program_id(1) == s_ref[1])
        def _():
            own_ref[...] = v

    return pl.pallas_call(
        body, name="pair_sum",
        grid_spec=pltpu.PrefetchScalarGridSpec(
            num_scalar_prefetch=1, grid=(R // FLAT_TILE, nj),
            in_specs=[pl.BlockSpec((None, None, FLAT_TILE, W), lambda r, j, s: (s[0], j, r, 0)),
                      pl.BlockSpec((None, FLAT_TILE, W), lambda r, j, s: (j, r, 0))],
            out_specs=[pl.BlockSpec((None, FLAT_TILE, W), lambda r, j, s: (j, r, 0)),
                       pl.BlockSpec((FLAT_TILE, W), lambda r, j, s: (r, 0))]),
        out_shape=[jax.ShapeDtypeStruct((nj, R, W), CDT), jax.ShapeDtypeStruct((R, W), F32)],
        compiler_params=_cp("arbitrary", "arbitrary"),
    )(place, g, got)


def exchange_partials(pb):
    def body(pb_ref, got_ref, send_sems, recv_sems):
        x, y, c, j = _place()
        peers = _peer_chips(x, y, j)
        sends = [pltpu.make_async_remote_copy(src_ref=pb_ref.at[pj], dst_ref=got_ref.at[j], send_sem=send_sems.at[k],
                                              recv_sem=recv_sems.at[k], device_id=(*chip, c), device_id_type=MESH)
                 for k, (chip, pj) in enumerate(peers)]
        for cp in sends:
            cp.start()
        for k, (_, pj) in enumerate(peers):
            pltpu.make_async_remote_copy(src_ref=pb_ref.at[pj], dst_ref=got_ref.at[pj], send_sem=send_sems.at[k],
                                         recv_sem=recv_sems.at[k], device_id=(x, y, c), device_id_type=MESH).wait_recv()
        for cp in sends:
            cp.wait_send()

    return pl.pallas_call(
        body, name="exchange_partials", out_shape=jax.ShapeDtypeStruct(pb.shape, pb.dtype),
        in_specs=[_any()], out_specs=_any(),
        scratch_shapes=[pltpu.SemaphoreType.DMA((3,)), pltpu.SemaphoreType.DMA((3,))],
    )(pb)


def total_sum(own, got, place):
    R, W = own.shape

    def body(s_ref, own_ref, a_ref, b_ref, c_ref, o_ref):
        o_ref[...] = ((own_ref[...] + a_ref[...].astype(F32)) + b_ref[...].astype(F32)) + c_ref[...].astype(F32)

    def slab(k):
        return pl.BlockSpec((None, FLAT_TILE, W), lambda r, s: (s[1] ^ (k + 1), r, 0))

    return pl.pallas_call(
        body, name="total_sum",
        grid_spec=pltpu.PrefetchScalarGridSpec(
            num_scalar_prefetch=1, grid=(R // FLAT_TILE,),
            in_specs=[pl.BlockSpec((FLAT_TILE, W), lambda r, s: (r, 0)), slab(0), slab(1), slab(2)],
            out_specs=pl.BlockSpec((FLAT_TILE, W), lambda r, s: (r, 0))),
        out_shape=jax.ShapeDtypeStruct((R, W), F32),
        compiler_params=_cp("arbitrary"),
    )(place, own, got, got, got)


def share_totals(tot):
    def body(t_ref, got_ref, send_sem, recv_sem):
        x, y, c, _ = _place()
        cp = pltpu.make_async_remote_copy(src_ref=t_ref, dst_ref=got_ref, send_sem=send_sem, recv_sem=recv_sem,
                                          device_id=(x, y, 1 - c), device_id_type=MESH)
        cp.start()
        cp.wait()

    return pl.pallas_call(
        body, name="share_totals", out_shape=jax.ShapeDtypeStruct(tot.shape, tot.dtype),
        in_specs=[_any()], out_specs=_any(),
        scratch_shapes=[pltpu.SemaphoreType.DMA, pltpu.SemaphoreType.DMA],
    )(tot)


def allreduce_small(v):
    rows, W = v.shape

    def body(v_ref, o_ref, buf, send_sems, recv_sems):
        x, y, c, _ = _place()
        me = 4 * x + 2 * y + c
        buf[me] = v_ref[...]
        sends = []
        for r in range(1, N_DEV):
            to = (x ^ (r >> 2), y ^ ((r >> 1) & 1), c ^ (r & 1))
            sends.append(pltpu.make_async_remote_copy(src_ref=v_ref, dst_ref=buf.at[me], send_sem=send_sems.at[r - 1],
                                                      recv_sem=recv_sems.at[r - 1], device_id=to, device_id_type=MESH))
        for cp in sends:
            cp.start()
        for r in range(1, N_DEV):
            pltpu.make_async_remote_copy(src_ref=v_ref, dst_ref=buf.at[me ^ r], send_sem=send_sems.at[r - 1],
                                         recv_sem=recv_sems.at[r - 1], device_id=(x, y, c), device_id_type=MESH).wait_recv()
        for cp in sends:
            cp.wait_send()
        acc = buf[0]
        for d in range(1, N_DEV):
            acc = acc + buf[d]
        o_ref[...] = acc

    vm = pl.BlockSpec(memory_space=pltpu.VMEM)
    return pl.pallas_call(
        body, name="allreduce_small", out_shape=jax.ShapeDtypeStruct(v.shape, v.dtype),
        in_specs=[vm], out_specs=vm,
        scratch_shapes=[pltpu.VMEM((N_DEV, rows, W), F32), pltpu.SemaphoreType.DMA((N_DEV - 1,)),
                        pltpu.SemaphoreType.DMA((N_DEV - 1,))],
    )(v)


def adamw(w, g, m, v, *, name):
    R, C = w.shape
    tr = R if R <= 512 else 512

    def body(w_ref, g_ref, m_ref, v_ref, d_ref, nm_ref, nv_ref):
        gv = g_ref[...]
        nm = ADAM_B1 * m_ref[...] + (1.0 - ADAM_B1) * gv
        nv = ADAM_B2 * v_ref[...] + (1.0 - ADAM_B2) * jnp.square(gv)
        m_hat = nm / (1.0 - ADAM_B1 ** ADAM_STEP)
        v_hat = nv / (1.0 - ADAM_B2 ** ADAM_STEP)
        d_ref[...] = -ADAM_LR * (m_hat / (jnp.sqrt(v_hat) + ADAM_EPS) + ADAM_WD * w_ref[...])
        nm_ref[...] = nm
        nv_ref[...] = nv

    spec = pl.BlockSpec((tr, C), lambda i: (i, 0))
    out = jax.ShapeDtypeStruct((R, C), F32)
    return pl.pallas_call(
        body, name=name, grid=(R // tr,), in_specs=[spec] * 4, out_specs=[spec] * 3, out_shape=[out] * 3,
        compiler_params=_cp("parallel"),
    )(w, g, m, v)


MATRICES = tuple(n for n, _ in FLAT_PARTS)
COL_SHARDED = ("w_in", "w_attn_proj", "w_conv_proj", "w_mlp1")
SMALL = ("mix_norm_g", "b_in", "sinks", "conv_w", "conv_b", "conv_ln_g", "conv_ln_b", "b_conv_proj", "mlp_norm_g")


def _tile_rows(v):
    flat = v.reshape(-1).astype(F32)
    rows = -(-flat.shape[0] // (SUBLANES * FLAT_W)) * SUBLANES
    return jnp.pad(flat, (0, rows * FLAT_W - flat.shape[0])).reshape(rows, FLAT_W)


def _pack_small(get, final):
    parts = [_tile_rows(get(n, l)) for l in range(DEPTH) for n in SMALL]
    return jnp.concatenate(parts + [_tile_rows(final)], axis=0)


def _unpack_small(packed, shapes):
    out = {n: [] for n in SMALL}
    r = 0
    for l in range(DEPTH):
        for n in SMALL:
            size = math.prod(shapes[n])
            rows = -(-size // (SUBLANES * FLAT_W)) * SUBLANES
            out[n].append(packed[r:r + rows].reshape(-1)[:size].reshape(shapes[n]))
            r += rows
    res = {n: jnp.stack(v) for n, v in out.items()}
    res["final_norm_g"] = packed[r:r + SUBLANES].reshape(-1)[:D]
    return res


def _flatten_shard(ws, l):
    return jnp.concatenate([ws[n][l].reshape(rows, FLAT_W) for n, rows in FLAT_PARTS], axis=0)


def _unflatten_full(wg, l):
    full_shapes = {"w_in": (D, IN_W), "w_attn_proj": (ATTN_W, D), "w_conv_proj": (CONV_C, D), "w_out": (D, D),
                   "w_mlp1": (D, D_FF), "w_mlp2": (D_FF, D)}
    out, r = {}, 0
    for n, rows in FLAT_PARTS:
        part = wg[:, l, r:r + rows]
        K, N = full_shapes[n]
        if n in COL_SHARDED:
            out[n] = part.reshape(N_CHIPS, K, N // N_CHIPS).transpose(1, 0, 2).reshape(K, N)
        else:
            out[n] = part.reshape(K, N)
        r += rows
    return out


def _flatten_grads(gr, l):
    parts = []
    for n, rows in FLAT_PARTS:
        g = gr[n][l]
        K, N = g.shape
        if n in COL_SHARDED:
            g = g.reshape(K, N_CHIPS, N // N_CHIPS).transpose(1, 0, 2)
        parts.append(g.reshape(N_CHIPS, rows, FLAT_W))
    return jnp.concatenate(parts, axis=1)


def _unflatten_shard(flat, shapes):
    out, r = {}, 0
    for n, rows in FLAT_PARTS:
        out[n] = flat[:, r:r + rows].reshape(shapes[n])
        r += rows
    return out


WEIGHTS = ("mix_norm_g", "w_in", "b_in", "sinks", "conv_w", "conv_b", "conv_ln_g", "conv_ln_b", "w_attn_proj",
           "w_conv_proj", "b_conv_proj", "w_out", "mlp_norm_g", "w_mlp1", "w_mlp2", "final_norm_g")


def kernel(x, mix_norm_g, w_in, b_in, sinks, conv_w, conv_b, conv_ln_g, conv_ln_b, w_attn_proj, w_conv_proj, b_conv_proj, w_out, mlp_norm_g, w_mlp1, w_mlp2, final_norm_g, loss_target, m_mix_norm_g, m_w_in, m_b_in, m_sinks, m_conv_w, m_conv_b, m_conv_ln_g, m_conv_ln_b, m_w_attn_proj, m_w_conv_proj, m_b_conv_proj, m_w_out, m_mlp_norm_g, m_w_mlp1, m_w_mlp2, m_final_norm_g, v_mix_norm_g, v_w_in, v_b_in, v_sinks, v_conv_w, v_conv_b, v_conv_ln_g, v_conv_ln_b, v_w_attn_proj, v_w_conv_proj, v_b_conv_proj, v_w_out, v_mlp_norm_g, v_w_mlp1, v_w_mlp2, v_final_norm_g):
    w = dict(zip(WEIGHTS, (mix_norm_g, w_in, b_in, sinks, conv_w, conv_b, conv_ln_g, conv_ln_b, w_attn_proj, w_conv_proj,
                           b_conv_proj, w_out, mlp_norm_g, w_mlp1, w_mlp2, final_norm_g)))
    m = dict(zip(WEIGHTS, (m_mix_norm_g, m_w_in, m_b_in, m_sinks, m_conv_w, m_conv_b, m_conv_ln_g, m_conv_ln_b, m_w_attn_proj,
                           m_w_conv_proj, m_b_conv_proj, m_w_out, m_mlp_norm_g, m_w_mlp1, m_w_mlp2, m_final_norm_g)))
    v = dict(zip(WEIGHTS, (v_mix_norm_g, v_w_in, v_b_in, v_sinks, v_conv_w, v_conv_b, v_conv_ln_g, v_conv_ln_b, v_w_attn_proj,
                           v_w_conv_proj, v_b_conv_proj, v_w_out, v_mlp_norm_g, v_w_mlp1, v_w_mlp2, v_final_norm_g)))
    xi, yi, ci = lax.axis_index("x"), lax.axis_index("y"), lax.axis_index("c")
    chip = 2 * xi + yi
    place = jnp.stack([ci, chip]).astype(jnp.int32)

    wsh = jnp.stack([_flatten_shard(w, l) for l in range(DEPTH)]).astype(CDT)
    wg = lax.dynamic_update_slice(allgather_weights(wsh), wsh[None], (chip, 0, 0, 0))
    taps = jnp.zeros((DEPTH, CONV_K, CONV_C), F32)
    taps = lax.dynamic_update_slice(taps, jnp.where(ci == 0, w["conv_w"], 0.0), (0, 0, chip * (CONV_C // N_CHIPS)))
    taps = allreduce_small(_tile_rows(taps)).reshape(-1)[:DEPTH * CONV_K * CONV_C].reshape(DEPTH, CONV_K, CONV_C)

    p = {n: [] for n in WEIGHTS if n != "final_norm_g"}
    for l in range(DEPTH):
        for n, mat in _unflatten_full(wg, l).items():
            p[n].append(mat)
        for n in ("mix_norm_g", "b_in", "conv_b", "conv_ln_g", "conv_ln_b", "b_conv_proj", "mlp_norm_g"):
            p[n].append(w[n][l].reshape(1, -1))
        p["sinks"].append(w["sinks"][l])
        p["conv_w"].append(taps[l])
    p["final_norm_g"] = w["final_norm_g"].reshape(1, D)

    loss, dx, gr = local_grads(x[0], loss_target[0], p)
    loss = lax.psum(loss[0, 0], ("x", "y", "c"))

    gflat = jnp.stack([_flatten_grads(gr, l) for l in range(DEPTH)])
    pb, own = pair_sum(gflat, swap_other_layer(gflat), place)
    tot = total_sum(own, exchange_partials(pb), place)
    other = share_totals(tot)
    both = jnp.where(ci == 0, jnp.stack([tot, other]), jnp.stack([other, tot]))
    gsh = _unflatten_shard(both, {n: w[n].shape for n in MATRICES})

    small_shapes = {n: w[n].shape[1:] for n in SMALL}
    small_shapes["conv_w"] = (CONV_K, CONV_C)
    gsmall = _unpack_small(allreduce_small(_pack_small(lambda n, l: gr[n][l], gr["final_norm_g"])), small_shapes)
    gsmall["conv_w"] = lax.dynamic_slice(gsmall["conv_w"], (0, 0, chip * (CONV_C // N_CHIPS)),
                                         (DEPTH, CONV_K, CONV_C // N_CHIPS))
    grads = {**gsh, **gsmall}

    delta, new_m, new_v = {}, {}, {}
    for n in MATRICES:
        shp = w[n].shape
        two_d = (shp[0] * shp[1], shp[2])
        d_, m_, v_ = adamw(w[n].reshape(two_d), grads[n].reshape(two_d), m[n].reshape(two_d), v[n].reshape(two_d),
                           name="adamw_" + n)
        delta[n], new_m[n], new_v[n] = d_.reshape(shp), m_.reshape(shp), v_.reshape(shp)
    small_shapes["conv_w"] = w["conv_w"].shape[1:]
    sm = [_pack_small(lambda n, l, t=t: t[n][l], t["final_norm_g"]) for t in (w, grads, m, v)]
    outs = adamw(*sm, name="adamw_small")
    for dst, packed in zip((delta, new_m, new_v), outs):
        dst.update(_unpack_small(packed, small_shapes))

    return (loss, dx[None], *[grads[n] for n in WEIGHTS], *[delta[n] for n in WEIGHTS],
            *[new_m[n] for n in WEIGHTS], *[new_v[n] for n in WEIGHTS])
```

```python
import functools
import math
from typing import Callable, NamedTuple

import jax
import jax.numpy as jnp
import numpy as np
from jax import lax
from jax.experimental import pallas as pl
from jax.experimental.pallas import tpu as pltpu

F32 = jnp.float32
CDT = jnp.bfloat16

D = 1024
DEPTH = 2
N_Q = 8
HEAD_DIM = 64
ATTN_W = 512
KV_W = 128
BLOCK = 128
CONV_C = 512
CONV_K = 31
D_FF = 4096
IN_W = 3840
QKV_W = ATTN_W + 2 * KV_W
REST_W = IN_W - QKV_W
EPS = 1e-6
NEG = -1e30
SCALE = 1.0 / math.sqrt(HEAD_DIM)
SLOPES = [float(2.0 ** (-8.0 * (h + 1) / N_Q)) for h in range(N_Q)]
SUBLANES = 8
HALO = 32

ADAM_LR = 0.001
ADAM_B1 = 0.9
ADAM_B2 = 0.999
ADAM_EPS = 1e-08
ADAM_WD = 0.01
ADAM_STEP = 10

VMEM_LIMIT = 56 * 1024 * 1024


def _cp(*sem):
    return pltpu.CompilerParams(dimension_semantics=sem, vmem_limit_bytes=VMEM_LIMIT)


def _dot(a, b):
    return jnp.dot(a, b, preferred_element_type=F32)


def _dot_nt(a, b):
    return lax.dot_general(a, b, (((1,), (1,)), ((), ())), preferred_element_type=F32)


def _dot_tn(a, b):
    return lax.dot_general(a, b, (((0,), (0,)), ((), ())), preferred_element_type=F32)


def _sig(x):
    return 1.0 / (1.0 + jnp.exp(-x))


def _colsum(v):
    return jnp.sum(v, axis=0, keepdims=True)


def _const(shape, buffers=None):
    mode = {} if buffers is None else {"pipeline_mode": pl.Buffered(buffers)}
    return pl.BlockSpec(shape, lambda *_: (0,) * len(shape), **mode)


class Rider(NamedTuple):
    ins: tuple
    outs: tuple
    n_sems: int
    start: Callable
    finish: Callable


def _any():
    return pl.BlockSpec(memory_space=pl.ANY)


def _run(body, rider, *, name, grid, in_specs, out_specs, out_shape, args, sem, scratch_shapes=()):
    if rider is None:
        return pl.pallas_call(body, name=name, grid=grid, in_specs=list(in_specs), out_specs=list(out_specs),
                              out_shape=list(out_shape), scratch_shapes=list(scratch_shapes),
                              compiler_params=_cp(*sem))(*args)
    n_in, n_out, n_sc = len(in_specs), len(out_specs), len(scratch_shapes)
    r_in, r_out = len(rider.ins), len(rider.outs)

    def riding(*refs):
        ins, rins = refs[:n_in], refs[n_in:n_in + r_in]
        o0 = n_in + r_in
        outs, routs = refs[o0:o0 + n_out], refs[o0 + n_out:o0 + n_out + r_out]
        s0 = o0 + n_out + r_out
        scratch, sems = refs[s0:s0 + n_sc], refs[s0 + n_sc]
        first = functools.reduce(jnp.logical_and, [pl.program_id(a) == 0 for a in range(len(grid))])
        last = functools.reduce(jnp.logical_and, [pl.program_id(a) == grid[a] - 1 for a in range(len(grid))])

        @pl.when(first)
        def _():
            rider.start(rins, routs, sems)

        body(*ins, *outs, *scratch)

        @pl.when(last)
        def _():
            rider.finish(rins, routs, sems)

    res = pl.pallas_call(
        riding, name=name, grid=grid, in_specs=list(in_specs) + [_any()] * r_in,
        out_specs=list(out_specs) + [_any()] * r_out, out_shape=list(out_shape) + list(rider.outs),
        scratch_shapes=list(scratch_shapes) + [pltpu.SemaphoreType.DMA((rider.n_sems,))],
        compiler_params=_cp(*["arbitrary"] * len(grid)))(*args, *rider.ins)
    return res[:n_out], res[n_out:]


def _run_alone(rider, name):
    def body(*refs):
        r_in, r_out = len(rider.ins), len(rider.outs)
        rins, routs, sems = refs[:r_in], refs[r_in:r_in + r_out], refs[r_in + r_out]
        rider.start(rins, routs, sems)
        rider.finish(rins, routs, sems)

    return pl.pallas_call(
        body, name=name, in_specs=[_any()] * len(rider.ins), out_specs=[_any()] * len(rider.outs),
        out_shape=list(rider.outs), scratch_shapes=[pltpu.SemaphoreType.DMA((rider.n_sems,))])(*rider.ins)


def rms_inproj(x, g, w, b, *, tm=512, rider=None):
    T = x.shape[0]

    def body(x_ref, g_ref, w_ref, b_ref, qkv_ref, rest_ref):
        xv = x_ref[...]
        r = lax.rsqrt(jnp.mean(xv * xv, axis=-1, keepdims=True) + EPS)
        h = (xv * r * g_ref[...]).astype(CDT)
        qkv_ref[...] = (_dot(h, w_ref[:, 0:QKV_W]) + b_ref[:, 0:QKV_W]).astype(qkv_ref.dtype)
        for j in range(REST_W // D):
            c0 = QKV_W + D * j
            rest_ref[:, D * j:D * (j + 1)] = _dot(h, w_ref[:, c0:c0 + D]) + b_ref[:, c0:c0 + D]

    return _run(
        body, rider, name="rms_inproj", grid=(T // tm,),
        in_specs=[pl.BlockSpec((tm, D), lambda i: (i, 0)), _const((1, D)), _const((D, IN_W), 1), _const((1, IN_W))],
        out_specs=[pl.BlockSpec((tm, QKV_W), lambda i: (i, 0)), pl.BlockSpec((tm, REST_W), lambda i: (i, 0))],
        out_shape=[jax.ShapeDtypeStruct((T, QKV_W), CDT), jax.ShapeDtypeStruct((T, REST_W), F32)],
        sem=("parallel",), args=(x, g, w, b))


def _lane_halves(shape):
    lane = lax.broadcasted_iota(jnp.int32, shape, 1)
    return lane < HEAD_DIM, lane >= HEAD_DIM


def _swap_halves(v):
    return pltpu.roll(v.astype(F32), HEAD_DIM, axis=1).astype(v.dtype)


N_KV = KV_W // HEAD_DIM
GROUP = N_Q // N_KV
STACK = GROUP * BLOCK


def _attn_masks(first):
    row = lax.broadcasted_iota(jnp.int32, (STACK, 2 * BLOCK), 0) & (BLOCK - 1)
    col = lax.broadcasted_iota(jnp.int32, (STACK, 2 * BLOCK), 1)
    dist = row + BLOCK - col
    valid = (dist >= 0) & (dist < BLOCK) & ((col >= BLOCK) | jnp.logical_not(first))
    return valid, dist.astype(F32)


def _per_head_column(vals):
    row = lax.broadcasted_iota(jnp.int32, (STACK, 1), 0)
    col = jnp.full((STACK, 1), vals[GROUP - 1], F32)
    for i in reversed(range(GROUP - 1)):
        col = jnp.where(row < (i + 1) * BLOCK, vals[i], col)
    return col


def _stack_heads(dst, src_ref, r0, g):
    lane = lax.broadcasted_iota(jnp.int32, (BLOCK, 2 * HEAD_DIM), 1)
    keep = (lane >= HEAD_DIM) if g else (lane < HEAD_DIM)
    for i in range(GROUP):
        h = GROUP * g + i
        tile = src_ref[pl.ds(r0, BLOCK), (h // 2) * 128:(h // 2 + 1) * 128]
        if h % 2 != g:
            tile = _swap_halves(tile)
        dst[i * BLOCK:(i + 1) * BLOCK, :] = jnp.where(keep, tile, jnp.zeros_like(tile))


def _unstack_heads(dst_ref, stacked, r0, g):
    lane = lax.broadcasted_iota(jnp.int32, (BLOCK, 2 * HEAD_DIM), 1)
    for j in range(GROUP // 2):
        even = stacked[(2 * j) * BLOCK:(2 * j + 1) * BLOCK, :]
        odd = stacked[(2 * j + 1) * BLOCK:(2 * j + 2) * BLOCK, :]
        lo = _swap_halves(even) if g else even
        hi = odd if g else _swap_halves(odd)
        pair = (GROUP * g) // 2 + j
        dst_ref[pl.ds(r0, BLOCK), pair * 128:(pair + 1) * 128] = jnp.where(lane < HEAD_DIM, lo, hi).astype(dst_ref.dtype)


def _qkv_specs(tq):
    nb = tq // BLOCK
    return [
        pl.BlockSpec((tq, ATTN_W), lambda i: (i, 0)),
        pl.BlockSpec((BLOCK, KV_W), lambda i: (jnp.maximum(i * nb - 1, 0), ATTN_W // KV_W)),
        pl.BlockSpec((tq, KV_W), lambda i: (i, ATTN_W // KV_W)),
        pl.BlockSpec((BLOCK, KV_W), lambda i: (jnp.maximum(i * nb - 1, 0), ATTN_W // KV_W + 1)),
        pl.BlockSpec((tq, KV_W), lambda i: (i, ATTN_W // KV_W + 1)),
    ]


def attn_fwd(qkv, sinks, *, tq=512):
    T = qkv.shape[0]
    nb = tq // BLOCK

    def body(sink_ref, q_ref, kp_ref, kc_ref, vp_ref, vc_ref, o_ref, lse_ref, kext, vext, qs):
        i = pl.program_id(0)
        kext[0:BLOCK, :] = kp_ref[...]
        kext[BLOCK:, :] = kc_ref[...]
        vext[0:BLOCK, :] = vp_ref[...]
        vext[BLOCK:, :] = vc_ref[...]
        lane_l = lax.broadcasted_iota(jnp.int32, (BLOCK, 128), 1)

        def blk(b, carry):
            r0 = pl.multiple_of(b * BLOCK, BLOCK)
            valid, distf = _attn_masks(jnp.logical_and(i == 0, b == 0))
            kc = kext[pl.ds(r0, 2 * BLOCK), :]
            vc = vext[pl.ds(r0, 2 * BLOCK), :]
            lse_t = jnp.zeros((BLOCK, 128), F32)
            for g in range(N_KV):
                heads = range(GROUP * g, GROUP * (g + 1))
                _stack_heads(qs, q_ref, r0, g)
                s = _dot_nt(qs[...], kc) * SCALE - _per_head_column([SLOPES[h] for h in heads]) * distf
                s = jnp.where(valid, s, NEG)
                sink = _per_head_column([sink_ref[h] for h in heads])
                m = jnp.maximum(jnp.max(s, axis=-1, keepdims=True), sink)
                p = jnp.exp(s - m)
                denom = jnp.sum(p, axis=-1, keepdims=True) + jnp.exp(sink - m)
                p = p / denom
                _unstack_heads(o_ref, _dot(p.astype(CDT), vc), r0, g)
                lse = m + jnp.log(denom)
                for i_h, h in enumerate(heads):
                    lse_t = jnp.where(lane_l == h, lse[i_h * BLOCK:(i_h + 1) * BLOCK, :], lse_t)
            lse_ref[pl.ds(r0, BLOCK), :] = lse_t
            return carry

        lax.fori_loop(0, nb, blk, 0)

    return pl.pallas_call(
        body, name="attn_fwd", grid=(T // tq,),
        in_specs=[pl.BlockSpec(memory_space=pltpu.SMEM)] + _qkv_specs(tq),
        out_specs=[pl.BlockSpec((tq, ATTN_W), lambda i: (i, 0)), pl.BlockSpec((tq, 128), lambda i: (i, 0))],
        out_shape=[jax.ShapeDtypeStruct((T, ATTN_W), CDT), jax.ShapeDtypeStruct((T, 128), F32)],
        scratch_shapes=[pltpu.VMEM((tq + BLOCK, KV_W), CDT), pltpu.VMEM((tq + BLOCK, KV_W), CDT),
                        pltpu.VMEM((STACK, 2 * HEAD_DIM), CDT)],
        compiler_params=_cp("parallel"),
    )(sinks, qkv, qkv, qkv, qkv, qkv)


def _halo_before(tm, width, col):
    return pl.BlockSpec((HALO, width), lambda i: (jnp.maximum(i * (tm // HALO) - 1, 0), col))


def _fill_u0(ext, a_ref, b_ref, ha_ref, hb_ref, first):
    hu = ha_ref[...] * _sig(hb_ref[...])
    ext[0:HALO, :] = jnp.where(first, jnp.zeros_like(hu), hu)
    ext[HALO:, :] = a_ref[...] * _sig(b_ref[...])


def _shifted_taps(src, w_ref, base, rc, offsets):
    acc = jnp.zeros((rc, CONV_C), F32)
    for b in range(SUBLANES):
        taps = [(k, o - b) for k, o in enumerate(offsets) if o % SUBLANES == b]
        if not taps:
            continue
        rows = rc if b == 0 else rc + SUBLANES
        part = jnp.zeros((rows, CONV_C), F32)
        for k, o8 in taps:
            part = part + w_ref[k:k + 1, :] * src[base + o8:base + o8 + rows, :]
        acc = acc + (part if b == 0 else part[b:b + rc, :])
    return acc


def _conv_rows(ext, w_ref, r0, rc):
    return _shifted_taps(ext, w_ref, r0, rc, [HALO - (CONV_K - 1) + k for k in range(CONV_K)])


def _layer_norm(u1, g, b):
    mu = jnp.mean(u1, axis=-1, keepdims=True)
    xc = u1 - mu
    rstd = lax.rsqrt(jnp.mean(xc * xc, axis=-1, keepdims=True) + EPS)
    n = xc * rstd
    return n, rstd, n * g + b


CONV_RC = 32


def conv_fwd(rest, cw, cb, lg, lb, *, tm=512):
    T = rest.shape[0]

    def body(a_ref, b_ref, ha_ref, hb_ref, w_ref, cb_ref, lg_ref, lb_ref, o_ref, u1_ref, ext):
        _fill_u0(ext, a_ref, b_ref, ha_ref, hb_ref, pl.program_id(0) == 0)
        for c in range(tm // CONV_RC):
            r0 = c * CONV_RC
            u1 = _conv_rows(ext, w_ref, r0, CONV_RC) + cb_ref[...]
            u1_ref[r0:r0 + CONV_RC, :] = u1
            _, _, u2 = _layer_norm(u1, lg_ref[...], lb_ref[...])
            o_ref[r0:r0 + CONV_RC, :] = (u2 * _sig(u2)).astype(o_ref.dtype)

    row = lambda i: (i, 0)
    return pl.pallas_call(
        body, name="conv_fwd", grid=(T // tm,),
        in_specs=[pl.BlockSpec((tm, CONV_C), row), pl.BlockSpec((tm, CONV_C), lambda i: (i, 1)),
                  _halo_before(tm, CONV_C, 0), _halo_before(tm, CONV_C, 1),
                  _const((CONV_K, CONV_C)), _const((1, CONV_C)), _const((1, CONV_C)), _const((1, CONV_C))],
        out_specs=[pl.BlockSpec((tm, CONV_C), row), pl.BlockSpec((tm, CONV_C), row)],
        out_shape=[jax.ShapeDtypeStruct((T, CONV_C), CDT), jax.ShapeDtypeStruct((T, CONV_C), F32)],
        scratch_shapes=[pltpu.VMEM((tm + HALO, CONV_C), F32)],
        compiler_params=_cp("parallel"),
    )(rest, rest, rest, rest, cw, cb, lg, lb)


def merge_out(x, attn, u3, rest, wa, wc, bc, wo, *, tm=512):
    T = x.shape[0]

    def body(x_ref, at_ref, u_ref, ga_ref, gc_ref, wa_ref, wc_ref, bc_ref, wo_ref, o_ref):
        br_a = _dot(at_ref[...], wa_ref[...])
        br_c = _dot(u_ref[...], wc_ref[...]) + bc_ref[...]
        merged = _sig(ga_ref[...]) * br_a + _sig(gc_ref[...]) * br_c
        o_ref[...] = x_ref[...] + _dot(merged.astype(CDT), wo_ref[...])

    return pl.pallas_call(
        body, name="merge_out", grid=(T // tm,),
        in_specs=[pl.BlockSpec((tm, D), lambda i: (i, 0)), pl.BlockSpec((tm, ATTN_W), lambda i: (i, 0)),
                  pl.BlockSpec((tm, CONV_C), lambda i: (i, 0)),
                  pl.BlockSpec((tm, D), lambda i: (i, 1)), pl.BlockSpec((tm, D), lambda i: (i, 2)),
                  _const((ATTN_W, D), 1), _const((CONV_C, D), 1), _const((1, D)), _const((D, D), 1)],
        out_specs=pl.BlockSpec((tm, D), lambda i: (i, 0)),
        out_shape=jax.ShapeDtypeStruct((T, D), F32),
        compiler_params=_cp("parallel"),
    )(x, attn, u3, rest, rest, wa, wc, bc, wo)


def mlp_fwd(x, g, w1, w2, *, tm=256, tf=D_FF, rider=None):
    T = x.shape[0]
    nf = D_FF // tf

    def body(x_ref, g_ref, w1_ref, w2_ref, o_ref, pre_ref, h_s, acc_s):
        f = pl.program_id(1)

        @pl.when(f == 0)
        def _():
            xv = x_ref[...]
            r = lax.rsqrt(jnp.mean(xv * xv, axis=-1, keepdims=True) + EPS)
            h_s[...] = (xv * r * g_ref[...]).astype(CDT)
            acc_s[...] = jnp.zeros_like(acc_s)

        pre = _dot(h_s[...], w1_ref[...])
        pre_ref[...] = pre
        a = jnp.square(jnp.maximum(pre, 0.0))
        acc_s[...] += _dot(a.astype(CDT), w2_ref[...])

        @pl.when(f == nf - 1)
        def _():
            o_ref[...] = x_ref[...] + acc_s[...]

    mode = {"pipeline_mode": pl.Buffered(1)} if nf == 1 else {}
    return _run(
        body, rider, name="mlp_fwd", grid=(T // tm, nf),
        in_specs=[pl.BlockSpec((tm, D), lambda i, f: (i, 0)), _const((1, D)),
                  pl.BlockSpec((D, tf), lambda i, f: (0, f), **mode), pl.BlockSpec((tf, D), lambda i, f: (f, 0), **mode)],
        out_specs=[pl.BlockSpec((tm, D), lambda i, f: (i, 0)), pl.BlockSpec((tm, tf), lambda i, f: (i, f))],
        out_shape=[jax.ShapeDtypeStruct((T, D), F32), jax.ShapeDtypeStruct((T, D_FF), F32)],
        scratch_shapes=[pltpu.VMEM((tm, D), CDT), pltpu.VMEM((tm, D), F32)],
        sem=("parallel", "arbitrary"), args=(x, g, w1, w2))


def _rms_bwd(xv, g, dh):
    r = lax.rsqrt(jnp.mean(xv * xv, axis=-1, keepdims=True) + EPS)
    xhat = xv * r
    dxh = dh * g
    dx = r * (dxh - xhat * jnp.mean(dxh * xhat, axis=-1, keepdims=True))
    return dx, dh * xhat


def loss_head(x, g, tgt, *, tm=512):
    T = x.shape[0]

    def body(x_ref, g_ref, t_ref, dx_ref, dg_ref, loss_ref):
        @pl.when(pl.program_id(0) == 0)
        def _():
            dg_ref[...] = jnp.zeros_like(dg_ref)
            loss_ref[...] = jnp.zeros_like(loss_ref)

        xv = x_ref[...]
        gv = g_ref[...]
        r = lax.rsqrt(jnp.mean(xv * xv, axis=-1, keepdims=True) + EPS)
        e = xv * r * gv - t_ref[...]
        loss_ref[...] += 0.5 * jnp.sum(jnp.mean(e * e, axis=-1, keepdims=True), axis=0, keepdims=True)
        dx, dg_rows = _rms_bwd(xv, gv, e * (1.0 / D))
        dx_ref[...] = dx
        dg_ref[...] += _colsum(dg_rows)

    return pl.pallas_call(
        body, name="loss_head", grid=(T // tm,),
        in_specs=[pl.BlockSpec((tm, D), lambda i: (i, 0)), _const((1, D)), pl.BlockSpec((tm, D), lambda i: (i, 0))],
        out_specs=[pl.BlockSpec((tm, D), lambda i: (i, 0)), _const((1, D)), _const((1, 128))],
        out_shape=[jax.ShapeDtypeStruct((T, D), F32), jax.ShapeDtypeStruct((1, D), F32),
                   jax.ShapeDtypeStruct((1, 128), F32)],
        compiler_params=_cp("arbitrary"),
    )(x, g, tgt)


def mlp_bwd(dy, x, g, pre, w1, w2, *, tm=256, tf=D_FF, rider=None):
    T = x.shape[0]
    nf = D_FF // tf

    def body(dy_ref, x_ref, g_ref, pre_ref, w1_ref, w2_ref, dx_ref, dg_ref, h_ref, a_ref, dpre_ref, dyb_s, acc_s):
        i, f = pl.program_id(0), pl.program_id(1)

        @pl.when(jnp.logical_and(i == 0, f == 0))
        def _():
            dg_ref[...] = jnp.zeros_like(dg_ref)

        @pl.when(f == 0)
        def _():
            dyb_s[...] = dy_ref[...].astype(CDT)
            acc_s[...] = jnp.zeros_like(acc_s)

        pre = pre_ref[...]
        rl = jnp.maximum(pre, 0.0)
        a_ref[...] = (rl * rl).astype(CDT)
        da = _dot_nt(dyb_s[...], w2_ref[...])
        dpre = (da * (2.0 * rl)).astype(CDT)
        dpre_ref[...] = dpre
        acc_s[...] += _dot_nt(dpre, w1_ref[...])

        @pl.when(f == nf - 1)
        def _():
            xv = x_ref[...]
            gv = g_ref[...]
            dxn, dg_rows = _rms_bwd(xv, gv, acc_s[...])
            dx_ref[...] = dy_ref[...] + dxn
            dg_ref[...] += _colsum(dg_rows)
            r = lax.rsqrt(jnp.mean(xv * xv, axis=-1, keepdims=True) + EPS)
            h_ref[...] = (xv * r * gv).astype(CDT)

    row = lambda i, f: (i, 0)
    mode = {"pipeline_mode": pl.Buffered(1)} if nf == 1 else {}
    return _run(
        body, rider, name="mlp_bwd", grid=(T // tm, nf),
        in_specs=[pl.BlockSpec((tm, D), row), pl.BlockSpec((tm, D), row), _const((1, D)),
                  pl.BlockSpec((tm, tf), lambda i, f: (i, f)),
                  pl.BlockSpec((D, tf), lambda i, f: (0, f), **mode), pl.BlockSpec((tf, D), lambda i, f: (f, 0), **mode)],
        out_specs=[pl.BlockSpec((tm, D), row), _const((1, D)), pl.BlockSpec((tm, D), row),
                   pl.BlockSpec((tm, tf), lambda i, f: (i, f)), pl.BlockSpec((tm, tf), lambda i, f: (i, f))],
        out_shape=[jax.ShapeDtypeStruct((T, D), F32), jax.ShapeDtypeStruct((1, D), F32),
                   jax.ShapeDtypeStruct((T, D), CDT), jax.ShapeDtypeStruct((T, D_FF), CDT),
                   jax.ShapeDtypeStruct((T, D_FF), CDT)],
        scratch_shapes=[pltpu.VMEM((tm, D), CDT), pltpu.VMEM((tm, D), F32)],
        sem=("arbitrary", "arbitrary"), args=(dy, x, g, pre, w1, w2))


def tn_matmul(a, b, *, tm, tn, tk=2048, name, by_chip=False, rider=None):
    T, M = a.shape
    N = b.shape[1]
    tk = min(tk, T)
    nk = T // tk

    def body(a_ref, b_ref, o_ref):
        @pl.when(pl.program_id(2) == 0)
        def _():
            o_ref[...] = jnp.zeros_like(o_ref)

        o_ref[...] += _dot_tn(a_ref[...].astype(CDT), b_ref[...].astype(CDT))

    if by_chip:
        out_spec = pl.BlockSpec((None, tm, tn), lambda i, j, k: (j, i, 0))
        out_shape = jax.ShapeDtypeStruct((N // tn, M, tn), F32)
    else:
        out_spec = pl.BlockSpec((tm, tn), lambda i, j, k: (i, j))
        out_shape = jax.ShapeDtypeStruct((M, N), F32)
    res = _run(
        body, rider, name=name, grid=(M // tm, N // tn, nk),
        in_specs=[pl.BlockSpec((tk, tm), lambda i, j, k: (k, i)), pl.BlockSpec((tk, tn), lambda i, j, k: (k, j))],
        out_specs=[out_spec], out_shape=[out_shape], sem=("parallel", "parallel", "arbitrary"), args=(a, b))
    return res[0] if rider is None else (res[0][0], res[1])


def merge_bwd(dx1, attn, u3, rest, wa, wc, bc, wo, *, tm=256):
    T = dx1.shape[0]

    def body(dx_ref, at_ref, u_ref, ga_ref, gc_ref, wa_ref, wc_ref, bc_ref, wo_ref,
             mg_ref, dba_ref, dbc_ref, dat_ref, du_ref, dgate_ref, dbias_ref):
        @pl.when(pl.program_id(0) == 0)
        def _():
            dbias_ref[...] = jnp.zeros_like(dbias_ref)

        br_a = _dot(at_ref[...], wa_ref[...])
        br_c = _dot(u_ref[...], wc_ref[...]) + bc_ref[...]
        sa = _sig(ga_ref[...])
        sc = _sig(gc_ref[...])
        mg_ref[...] = (sa * br_a + sc * br_c).astype(CDT)
        dm = _dot_nt(dx_ref[...].astype(CDT), wo_ref[...])
        dba = dm * sa
        dbc = dm * sc
        dgate_ref[:, 0:D] = dm * br_a * sa * (1.0 - sa)
        dgate_ref[:, D:2 * D] = dm * br_c * sc * (1.0 - sc)
        dbias_ref[...] += _colsum(dbc)
        dba_b = dba.astype(CDT)
        dbc_b = dbc.astype(CDT)
        dba_ref[...] = dba_b
        dbc_ref[...] = dbc_b
        dat_ref[...] = _dot_nt(dba_b, wa_ref[...]).astype(CDT)
        du_ref[...] = _dot_nt(dbc_b, wc_ref[...])

    row = lambda i: (i, 0)
    return pl.pallas_call(
        body, name="merge_bwd", grid=(T // tm,),
        in_specs=[pl.BlockSpec((tm, D), row), pl.BlockSpec((tm, ATTN_W), row), pl.BlockSpec((tm, CONV_C), row),
                  pl.BlockSpec((tm, D), lambda i: (i, 1)), pl.BlockSpec((tm, D), lambda i: (i, 2)),
                  _const((ATTN_W, D), 1), _const((CONV_C, D), 1), _const((1, D)), _const((D, D), 1)],
        out_specs=[pl.BlockSpec((tm, D), row), pl.BlockSpec((tm, D), row), pl.BlockSpec((tm, D), row),
                   pl.BlockSpec((tm, ATTN_W), row), pl.BlockSpec((tm, CONV_C), row),
                   pl.BlockSpec((tm, 2 * D), row), _const((1, D))],
        out_shape=[jax.ShapeDtypeStruct((T, D), CDT), jax.ShapeDtypeStruct((T, D), CDT),
                   jax.ShapeDtypeStruct((T, D), CDT), jax.ShapeDtypeStruct((T, ATTN_W), CDT),
                   jax.ShapeDtypeStruct((T, CONV_C), F32), jax.ShapeDtypeStruct((T, 2 * D), F32),
                   jax.ShapeDtypeStruct((1, D), F32)],
        compiler_params=_cp("arbitrary"),
    )(dx1, attn, u3, rest, rest, wa, wc, bc, wo)


def conv_bwd_ln(du3, u1, lg, lb, *, tm=512, rider=None):
    T = du3.shape[0]

    def body(du_ref, u1_ref, lg_ref, lb_ref, du1_ref, dlg_ref, dlb_ref, dcb_ref):
        @pl.when(pl.program_id(0) == 0)
        def _():
            dlg_ref[...] = jnp.zeros_like(dlg_ref)
            dlb_ref[...] = jnp.zeros_like(dlb_ref)
            dcb_ref[...] = jnp.zeros_like(dcb_ref)

        dlg = jnp.zeros((1, CONV_C), F32)
        dlb = jnp.zeros((1, CONV_C), F32)
        dcb = jnp.zeros((1, CONV_C), F32)
        for c in range(tm // CONV_RC):
            r0 = c * CONV_RC
            n, rstd, u2 = _layer_norm(u1_ref[r0:r0 + CONV_RC, :], lg_ref[...], lb_ref[...])
            s = _sig(u2)
            du2 = du_ref[r0:r0 + CONV_RC, :] * (s + u2 * s * (1.0 - s))
            dn = du2 * lg_ref[...]
            du1 = rstd * (dn - jnp.mean(dn, axis=-1, keepdims=True) - n * jnp.mean(dn * n, axis=-1, keepdims=True))
            du1_ref[r0:r0 + CONV_RC, :] = du1
            dlg = dlg + _colsum(du2 * n)
            dlb = dlb + _colsum(du2)
            dcb = dcb + _colsum(du1)
        dlg_ref[...] += dlg
        dlb_ref[...] += dlb
        dcb_ref[...] += dcb

    row = lambda i: (i, 0)
    vec = jax.ShapeDtypeStruct((1, CONV_C), F32)
    return _run(
        body, rider, name="conv_bwd_ln", grid=(T // tm,),
        in_specs=[pl.BlockSpec((tm, CONV_C), row), pl.BlockSpec((tm, CONV_C), row), _const((1, CONV_C)), _const((1, CONV_C))],
        out_specs=[pl.BlockSpec((tm, CONV_C), row), _const((1, CONV_C)), _const((1, CONV_C)), _const((1, CONV_C))],
        out_shape=[jax.ShapeDtypeStruct((T, CONV_C), F32), vec, vec, vec],
        sem=("arbitrary",), args=(du3, u1, lg, lb))


def conv_bwd_taps(du1, rest, cw, *, tm=512):
    T = du1.shape[0]
    nt = T // tm

    def body(d_ref, hd_ref, a_ref, b_ref, ha_ref, hb_ref, w_ref, dglu_ref, dw_ref, ext, dext, dwacc):
        i = pl.program_id(0)

        @pl.when(i == 0)
        def _():
            dwacc[...] = jnp.zeros_like(dwacc)

        _fill_u0(ext, a_ref, b_ref, ha_ref, hb_ref, i == 0)
        dext[0:SUBLANES, :] = jnp.zeros((SUBLANES, CONV_C), F32)
        dext[SUBLANES:SUBLANES + tm, :] = d_ref[...]
        hd = hd_ref[...]
        dext[SUBLANES + tm:, :] = jnp.where(i == nt - 1, jnp.zeros_like(hd), hd)
        qrow = lax.broadcasted_iota(jnp.int32, (CONV_RC + SUBLANES, CONV_C), 0)
        for c in range(tm // CONV_RC):
            r0 = c * CONV_RC
            du0 = _shifted_taps(dext, w_ref, r0 + SUBLANES, CONV_RC, [CONV_K - 1 - k for k in range(CONV_K)])
            for b in range(SUBLANES):
                taps = [(k, HALO - (CONV_K - 1) + k - b) for k in range(CONV_K) if (HALO - (CONV_K - 1) + k) % SUBLANES == b]
                if b == 0:
                    rows = CONV_RC
                    dsh = dext[r0 + SUBLANES:r0 + SUBLANES + rows, :]
                else:
                    rows = CONV_RC + SUBLANES
                    dsh = dext[r0 + SUBLANES - b:r0 + SUBLANES - b + rows, :]
                    dsh = jnp.where((qrow >= b) & (qrow < CONV_RC + b), dsh, 0.0)
                for k, o8 in taps:
                    prod = dsh * ext[r0 + o8:r0 + o8 + rows, :]
                    dwacc[8 * k:8 * k + 8, :] += jnp.sum(prod.reshape(rows // SUBLANES, SUBLANES, CONV_C), axis=0)
            av = a_ref[r0:r0 + CONV_RC, :]
            sb = _sig(b_ref[r0:r0 + CONV_RC, :])
            dglu_ref[r0:r0 + CONV_RC, 0:CONV_C] = du0 * sb
            dglu_ref[r0:r0 + CONV_RC, CONV_C:2 * CONV_C] = du0 * av * sb * (1.0 - sb)

        @pl.when(i == nt - 1)
        def _():
            dw_ref[...] = jnp.zeros_like(dw_ref)
            for k in range(CONV_K):
                dw_ref[k:k + 1, :] = _colsum(dwacc[8 * k:8 * k + 8, :])

    row = lambda i: (i, 0)
    return pl.pallas_call(
        body, name="conv_bwd_taps", grid=(nt,),
        in_specs=[pl.BlockSpec((tm, CONV_C), row),
                  pl.BlockSpec((HALO, CONV_C), lambda i: (jnp.minimum((i + 1) * (tm // HALO), T // HALO - 1), 0)),
                  pl.BlockSpec((tm, CONV_C), row), pl.BlockSpec((tm, CONV_C), lambda i: (i, 1)),
                  _halo_before(tm, CONV_C, 0), _halo_before(tm, CONV_C, 1), _const((CONV_K, CONV_C))],
        out_specs=[pl.BlockSpec((tm, 2 * CONV_C), row), _const((HALO, CONV_C))],
        out_shape=[jax.ShapeDtypeStruct((T, 2 * CONV_C), F32), jax.ShapeDtypeStruct((HALO, CONV_C), F32)],
        scratch_shapes=[pltpu.VMEM((tm + HALO, CONV_C), F32), pltpu.VMEM((SUBLANES + tm + HALO, CONV_C), F32),
                        pltpu.VMEM((8 * CONV_K, CONV_C), F32)],
        compiler_params=_cp("arbitrary"),
    )(du1, du1, rest, rest, rest, rest, cw)


def attn_bwd(qkv, do, lse, sinks, *, tq=512, rider=None):
    T = qkv.shape[0]
    nb = tq // BLOCK

    def body(sink_ref, q_ref, kp_ref, kc_ref, vp_ref, vc_ref, do_ref, lse_ref,
             dq_ref, dkv_ref, spill_ref, dsink_ref, kext, vext, dkext, dvext, qs, dos):
        i = pl.program_id(0)

        @pl.when(i == 0)
        def _():
            dsink_ref[...] = jnp.zeros_like(dsink_ref)

        kext[0:BLOCK, :] = kp_ref[...]
        kext[BLOCK:, :] = kc_ref[...]
        vext[0:BLOCK, :] = vp_ref[...]
        vext[BLOCK:, :] = vc_ref[...]
        dkext[...] = jnp.zeros_like(dkext)
        dvext[...] = jnp.zeros_like(dvext)
        lane_l = lax.broadcasted_iota(jnp.int32, (BLOCK, 128), 1)
        lane_k = lax.broadcasted_iota(jnp.int32, (2 * BLOCK, KV_W), 1)

        def blk(b, dsink):
            r0 = pl.multiple_of(b * BLOCK, BLOCK)
            valid, distf = _attn_masks(jnp.logical_and(i == 0, b == 0))
            kc = kext[pl.ds(r0, 2 * BLOCK), :]
            vc = vext[pl.ds(r0, 2 * BLOCK), :]
            lse_t = lse_ref[pl.ds(r0, BLOCK), :]
            dk = jnp.zeros((2 * BLOCK, KV_W), F32)
            dv = jnp.zeros((2 * BLOCK, KV_W), F32)
            for g in range(N_KV):
                heads = range(GROUP * g, GROUP * (g + 1))
                _stack_heads(qs, q_ref, r0, g)
                _stack_heads(dos, do_ref, r0, g)
                qv = qs[...]
                dov = dos[...]
                s = _dot_nt(qv, kc) * SCALE - _per_head_column([SLOPES[h] for h in heads]) * distf
                s = jnp.where(valid, s, NEG)
                lse = jnp.concatenate(
                    [jnp.sum(jnp.where(lane_l == h, lse_t, 0.0), axis=-1, keepdims=True) for h in heads], axis=0)
                p = jnp.exp(s - lse)
                dp = _dot_nt(dov, vc)
                dd = jnp.sum(p * dp, axis=-1, keepdims=True)
                ds = (p * (dp - dd)).astype(CDT)
                keep = (lane_k >= HEAD_DIM) if g else (lane_k < HEAD_DIM)
                _unstack_heads(dq_ref, _dot(ds, jnp.where(keep, kc, jnp.zeros_like(kc))) * SCALE, r0, g)
                dk = dk + _dot_tn(ds, qv)
                dv = dv + _dot_tn(p.astype(CDT), dov)
                wsink = jnp.exp(_per_head_column([sink_ref[h] for h in heads]) - lse) * dd
                for i_h, h in enumerate(heads):
                    part = jnp.sum(wsink[i_h * BLOCK:(i_h + 1) * BLOCK, :], axis=0, keepdims=True)
                    dsink = dsink - jnp.where(lane_l[0:1, :] == h, part, 0.0)
            dkext[pl.ds(r0, 2 * BLOCK), :] += dk * SCALE
            dvext[pl.ds(r0, 2 * BLOCK), :] += dv
            return dsink

        dsink_ref[...] += lax.fori_loop(0, nb, blk, jnp.zeros((1, 128), F32))
        dkv_ref[:, 0:KV_W] = dkext[BLOCK:, :]
        dkv_ref[:, KV_W:2 * KV_W] = dvext[BLOCK:, :]
        spill_ref[:, 0:KV_W] = dkext[0:BLOCK, :]
        spill_ref[:, KV_W:2 * KV_W] = dvext[0:BLOCK, :]

    row = lambda i: (i, 0)
    return _run(
        body, rider, name="attn_bwd", grid=(T // tq,),
        in_specs=[pl.BlockSpec(memory_space=pltpu.SMEM)] + _qkv_specs(tq)
        + [pl.BlockSpec((tq, ATTN_W), row), pl.BlockSpec((tq, 128), row)],
        out_specs=[pl.BlockSpec((tq, ATTN_W), row), pl.BlockSpec((tq, 2 * KV_W), row),
                   pl.BlockSpec((BLOCK, 2 * KV_W), row), _const((1, 128))],
        out_shape=[jax.ShapeDtypeStruct((T, ATTN_W), F32), jax.ShapeDtypeStruct((T, 2 * KV_W), F32),
                   jax.ShapeDtypeStruct((T // tq * BLOCK, 2 * KV_W), F32), jax.ShapeDtypeStruct((1, 128), F32)],
        scratch_shapes=[pltpu.VMEM((tq + BLOCK, KV_W), CDT), pltpu.VMEM((tq + BLOCK, KV_W), CDT),
                        pltpu.VMEM((tq + BLOCK, KV_W), F32), pltpu.VMEM((tq + BLOCK, KV_W), F32),
                        pltpu.VMEM((STACK, 2 * HEAD_DIM), CDT), pltpu.VMEM((STACK, 2 * HEAD_DIM), CDT)],
        sem=("arbitrary",), args=(sinks, qkv, qkv, qkv, qkv, qkv, do, lse))


def inproj_bwd(dres, x, g, w, dq, dkv, spill, dglu, dgate, *, tm=512):
    T = x.shape[0]
    nt = T // tm

    def body(dr_ref, x_ref, g_ref, w_ref, dq_ref, dkv_ref, sp_ref, dglu_ref, dgate_ref,
             dx_ref, dp_ref, h_ref, dg_ref, db_ref):
        i = pl.program_id(0)

        @pl.when(i == 0)
        def _():
            dg_ref[...] = jnp.zeros_like(dg_ref)
            db_ref[...] = jnp.zeros_like(db_ref)

        sp = sp_ref[...]
        sp = jnp.where(i == nt - 1, jnp.zeros_like(sp), sp)
        pieces = ((0, ATTN_W, dq_ref), (QKV_W, 2 * CONV_C, dglu_ref), (QKV_W + 2 * CONV_C, 2 * D, dgate_ref))
        for c0, wd, ref in pieces:
            v = ref[...]
            db_ref[:, c0:c0 + wd] += _colsum(v)
            dp_ref[:, c0:c0 + wd] = v.astype(CDT)
        dkv = dkv_ref[...]
        db_ref[:, ATTN_W:QKV_W] += _colsum(dkv) + _colsum(sp)
        dp_ref[0:tm - BLOCK, ATTN_W:QKV_W] = dkv[0:tm - BLOCK, :].astype(CDT)
        dp_ref[tm - BLOCK:tm, ATTN_W:QKV_W] = (dkv[tm - BLOCK:tm, :] + sp).astype(CDT)
        dh = _dot_nt(dp_ref[...], w_ref[...])
        xv = x_ref[...]
        gv = g_ref[...]
        dxn, dg_rows = _rms_bwd(xv, gv, dh)
        dx_ref[...] = dr_ref[...] + dxn
        dg_ref[...] += _colsum(dg_rows)
        r = lax.rsqrt(jnp.mean(xv * xv, axis=-1, keepdims=True) + EPS)
        h_ref[...] = (xv * r * gv).astype(CDT)

    row = lambda i: (i, 0)
    return pl.pallas_call(
        body, name="inproj_bwd", grid=(nt,),
        in_specs=[pl.BlockSpec((tm, D), row), pl.BlockSpec((tm, D), row), _const((1, D)), _const((D, IN_W), 1),
                  pl.BlockSpec((tm, ATTN_W), row), pl.BlockSpec((tm, 2 * KV_W), row),
                  pl.BlockSpec((BLOCK, 2 * KV_W), lambda i: (jnp.minimum(i + 1, nt - 1), 0)),
                  pl.BlockSpec((tm, 2 * CONV_C), row), pl.BlockSpec((tm, 2 * D), row)],
        out_specs=[pl.BlockSpec((tm, D), row), pl.BlockSpec((tm, IN_W), row), pl.BlockSpec((tm, D), row),
                   _const((1, D)), _const((1, IN_W))],
        out_shape=[jax.ShapeDtypeStruct((T, D), F32), jax.ShapeDtypeStruct((T, IN_W), CDT),
                   jax.ShapeDtypeStruct((T, D), CDT), jax.ShapeDtypeStruct((1, D), F32),
                   jax.ShapeDtypeStruct((1, IN_W), F32)],
        compiler_params=_cp("arbitrary"),
    )(dres, x, g, w, dq, dkv, spill, dglu, dgate)


ATTN_TILE = 256
MATRICES = ("w_in", "w_attn_proj", "w_conv_proj", "w_out", "w_mlp1", "w_mlp2")
SMALL = ("mix_norm_g", "b_in", "sinks", "conv_w", "conv_b", "conv_ln_g", "conv_ln_b", "b_conv_proj", "mlp_norm_g")


def forward_backward(x, tgt, hooks):
    def call(fn, kernel, l, *args, **kw):
        rider = hooks.rider(kernel, l)
        if rider is None:
            return fn(*args, **kw)
        outs, landed = fn(*args, rider=rider, **kw)
        hooks.landed(kernel, l, landed)
        return outs

    vec = hooks.vec
    saved = []
    for l in range(DEPTH):
        qkv, rest = call(rms_inproj, "rms_inproj", l, x, vec("mix_norm_g", l), hooks.w_in(l), vec("b_in", l))
        m = hooks.mats(l)
        attn, lse = attn_fwd(qkv, hooks.sinks(l), tq=ATTN_TILE)
        u3, u1 = conv_fwd(rest, hooks.taps(l), vec("conv_b", l), vec("conv_ln_g", l), vec("conv_ln_b", l))
        x1 = merge_out(x, attn, u3, rest, m["w_attn_proj"], m["w_conv_proj"], vec("b_conv_proj", l), m["w_out"])
        x2, pre = call(mlp_fwd, "mlp_fwd", l, x1, vec("mlp_norm_g", l), m["w_mlp1"], m["w_mlp2"])
        saved.append((x, qkv, rest, attn, lse, u3, u1, x1, pre))
        x = x2
    dx, dgf, loss = loss_head(x, hooks.final_g, tgt)
    small = {n: [None] * DEPTH for n in SMALL}
    small["final_norm_g"] = dgf
    for l in reversed(range(DEPTH)):
        x0, qkv, rest, attn, lse, u3, u1, x1, pre = saved[l]
        m = hooks.mats(l)
        dx1, dg2, h2, a, dpre = call(mlp_bwd, "mlp_bwd", l, dx, x1, vec("mlp_norm_g", l), pre, m["w_mlp1"], m["w_mlp2"])
        small["mlp_norm_g"][l] = dg2
        group = {}
        group["w_mlp1"] = call(tn_matmul, "tn_mlp1", l, h2, dpre, tm=1024, tn=1024, tk=4096, name="tn_mlp1", by_chip=True)
        group["w_mlp2"] = call(tn_matmul, "tn_mlp2", l, a, dx, tm=1024, tn=1024, name="tn_mlp2")
        merged, dba, dbc, dattn, du3, dgate, dbcp = merge_bwd(
            dx1, attn, u3, rest, m["w_attn_proj"], m["w_conv_proj"], vec("b_conv_proj", l), m["w_out"])
        small["b_conv_proj"][l] = dbcp
        group["w_out"] = tn_matmul(merged, dx1, tm=1024, tn=1024, name="tn_out")
        group["w_attn_proj"] = tn_matmul(attn, dba, tm=512, tn=256, tk=4096, name="tn_attn_proj", by_chip=True)
        group["w_conv_proj"] = tn_matmul(u3, dbc, tm=512, tn=256, tk=4096, name="tn_conv_proj", by_chip=True)
        hooks.grads(l, "A", group)
        du1, dlg, dlb, dcb = call(conv_bwd_ln, "conv_bwd_ln", l, du3, u1, vec("conv_ln_g", l), vec("conv_ln_b", l))
        small["conv_ln_g"][l], small["conv_ln_b"][l], small["conv_b"][l] = dlg, dlb, dcb
        dglu, dcw = conv_bwd_taps(du1, rest, hooks.taps(l))
        small["conv_w"][l] = dcw[0:CONV_K]
        dq, dkv, spill, dsink = call(attn_bwd, "attn_bwd", l, qkv, dattn, lse, hooks.sinks(l), tq=ATTN_TILE)
        small["sinks"][l] = dsink[0, 0:N_Q]
        dx, dproj, h, dg, db = inproj_bwd(dx1, x0, vec("mix_norm_g", l), hooks.w_in(l), dq, dkv, spill, dglu, dgate,
                                          tm=ATTN_TILE)
        small["mix_norm_g"][l], small["b_in"][l] = dg, db
        hooks.grads(l, "B", {"w_in": call(tn_matmul, "tn_in", l, h, dproj, tm=1024, tn=768, tk=4096, name="tn_in")})
    return loss, dx, small


class _LocalHooks:
    def __init__(self, p):
        self.p = p
        self.final_g = p["final_norm_g"]
        self.got = {n: [None] * DEPTH for n in MATRICES}

    def w_in(self, l):
        return self.p["w_in"][l]

    def mats(self, l):
        return {n: self.p[n][l] for n in MATRICES}

    def vec(self, n, l):
        return self.p[n][l]

    def sinks(self, l):
        return self.p["sinks"][l]

    def taps(self, l):
        return self.p["conv_w"][l]

    def rider(self, kernel, l):
        return None

    def grads(self, l, group, g):
        for n, v in g.items():
            if v.ndim == 3:
                v = v.transpose(1, 0, 2).reshape(v.shape[1], -1)
            self.got[n][l] = v


def local_grads(x, tgt, p):
    hooks = _LocalHooks(p)
    loss, dx, small = forward_backward(x, tgt, hooks)
    return loss, dx, {**small, **hooks.got}


MESH = pl.DeviceIdType.MESH
N_CHIPS = 4
N_DEV = 8
FLAT_W = 1024
FLAT_PARTS = (("w_in", 960), ("w_attn_proj", 128), ("w_conv_proj", 128), ("w_out", 256), ("w_mlp1", 1024), ("w_mlp2", 1024))
FLAT_ROWS = sum(r for _, r in FLAT_PARTS)
W_IN_ROWS = FLAT_PARTS[0][1]
GROUP_A = (("w_mlp1", 1024), ("w_mlp2", 1024), ("w_out", 256), ("w_attn_proj", 128), ("w_conv_proj", 128))
COL_SHARDED = ("w_in", "w_attn_proj", "w_conv_proj", "w_mlp1")
FULL_SHAPES = {"w_in": (D, IN_W), "w_attn_proj": (ATTN_W, D), "w_conv_proj": (CONV_C, D), "w_out": (D, D),
               "w_mlp1": (D, D_FF), "w_mlp2": (D_FF, D)}


def _place():
    x, y, c = lax.axis_index("x"), lax.axis_index("y"), lax.axis_index("c")
    return x, y, c, 2 * x + y


def _peer_chips(x, y, j):
    return [((x, 1 - y), j ^ 1), ((1 - x, y), j ^ 2), ((1 - x, 1 - y), j ^ 3)]


def _remote(src, dst, sems, k, n, to):
    return pltpu.make_async_remote_copy(src_ref=src, dst_ref=dst, send_sem=sems.at[k], recv_sem=sems.at[n + k],
                                        device_id=to, device_id_type=MESH)


def _half(c, rows):
    h = rows // 2
    return pl.ds(pl.multiple_of(c * h, 16), h)


def gather_rider(wsh):
    R = wsh.shape[0]

    def plan(rins, routs, sems):
        (w_ref,), (out_ref,) = rins, routs
        x, y, c, j = _place()
        peers = _peer_chips(x, y, j)
        mine, other = _half(c, R), _half(1 - c, R)
        sent = [_remote(w_ref.at[mine], out_ref.at[j, mine], sems, k, 6, (*chip, c)) for k, (chip, _) in enumerate(peers)]
        landed = [_remote(w_ref.at[mine], out_ref.at[pj, mine], sems, k, 6, (x, y, c)) for k, (_, pj) in enumerate(peers)]
        passed = [_remote(out_ref.at[pj, mine], out_ref.at[pj, mine], sems, 3 + k, 6, (x, y, 1 - c))
                  for k, (_, pj) in enumerate(peers)]
        handed = [_remote(w_ref.at[mine], out_ref.at[pj, other], sems, 3 + k, 6, (x, y, c)) for k, (_, pj) in enumerate(peers)]
        return sent, landed, passed, handed

    def start(rins, routs, sems):
        for cp in plan(rins, routs, sems)[0]:
            cp.start()

    def finish(rins, routs, sems):
        sent, landed, passed, handed = plan(rins, routs, sems)
        for k in range(3):
            landed[k].wait_recv()
            passed[k].start()
        for cp in handed:
            cp.wait_recv()
        for cp in sent + passed:
            cp.wait_send()

    return Rider((wsh,), (jax.ShapeDtypeStruct((N_CHIPS,) + wsh.shape, wsh.dtype),), 12, start, finish)


def swap_rider(g):
    R = g.shape[1]

    def plan(rins, routs, sems):
        (g_ref,), (got_ref,) = rins, routs
        x, y, c, _ = _place()
        return _remote(g_ref.at[:, _half(1 - c, R), :], got_ref, sems, 0, 1, (x, y, 1 - c))

    def start(rins, routs, sems):
        plan(rins, routs, sems).start()

    def finish(rins, routs, sems):
        plan(rins, routs, sems).wait()

    return Rider((g,), (jax.ShapeDtypeStruct((N_CHIPS, R // 2, FLAT_W), g.dtype),), 2, start, finish)


def exchange_rider(pb):
    def plan(rins, routs, sems):
        (pb_ref,), (got_ref,) = rins, routs
        x, y, c, j = _place()
        peers = _peer_chips(x, y, j)
        sent = [_remote(pb_ref.at[pj], got_ref.at[j], sems, k, 3, (*chip, c)) for k, (chip, pj) in enumerate(peers)]
        landed = [_remote(pb_ref.at[pj], got_ref.at[pj], sems, k, 3, (x, y, c)) for k, (_, pj) in enumerate(peers)]
        return sent, landed

    def start(rins, routs, sems):
        for cp in plan(rins, routs, sems)[0]:
            cp.start()

    def finish(rins, routs, sems):
        sent, landed = plan(rins, routs, sems)
        for cp in landed:
            cp.wait_recv()
        for cp in sent:
            cp.wait_send()

    return Rider((pb,), (jax.ShapeDtypeStruct(pb.shape, pb.dtype),), 6, start, finish)


def share_rider(tot):
    def plan(rins, routs, sems):
        (t_ref,), (got_ref,) = rins, routs
        x, y, c, _ = _place()
        return _remote(t_ref, got_ref, sems, 0, 1, (x, y, 1 - c))

    def start(rins, routs, sems):
        plan(rins, routs, sems).start()

    def finish(rins, routs, sems):
        plan(rins, routs, sems).wait()

    return Rider((tot,), (jax.ShapeDtypeStruct(tot.shape, tot.dtype),), 2, start, finish)


def pair_sum(g, got, place):
    nj, R, W = g.shape
    h = R // 2
    tile = h // 2

    def body(s_ref, g_ref, got_ref, pb_ref, own_ref):
        v = g_ref[...] + got_ref[...]
        pb_ref[...] = v.astype(pb_ref.dtype)

        @pl.when(pl.program_id(1) == s_ref[1])
        def _():
            own_ref[...] = v

    return pl.pallas_call(
        body, name="pair_sum",
        grid_spec=pltpu.PrefetchScalarGridSpec(
            num_scalar_prefetch=1, grid=(h // tile, nj),
            in_specs=[pl.BlockSpec((None, tile, W), lambda r, j, s: (j, s[0] * (h // tile) + r, 0)),
                      pl.BlockSpec((None, tile, W), lambda r, j, s: (j, r, 0))],
            out_specs=[pl.BlockSpec((None, tile, W), lambda r, j, s: (j, r, 0)),
                       pl.BlockSpec((tile, W), lambda r, j, s: (r, 0))]),
        out_shape=[jax.ShapeDtypeStruct((nj, h, W), CDT), jax.ShapeDtypeStruct((h, W), F32)],
        compiler_params=_cp("arbitrary", "arbitrary"),
    )(place, g, got)


def total_sum(own, got, place):
    R, W = own.shape
    tile = R // 2

    def body(s_ref, own_ref, a_ref, b_ref, c_ref, o_ref):
        o_ref[...] = ((own_ref[...] + a_ref[...].astype(F32)) + b_ref[...].astype(F32)) + c_ref[...].astype(F32)

    def slab(k):
        return pl.BlockSpec((None, tile, W), lambda r, s: (s[1] ^ (k + 1), r, 0))

    return pl.pallas_call(
        body, name="total_sum",
        grid_spec=pltpu.PrefetchScalarGridSpec(
            num_scalar_prefetch=1, grid=(R // tile,),
            in_specs=[pl.BlockSpec((tile, W), lambda r, s: (r, 0)), slab(0), slab(1), slab(2)],
            out_specs=pl.BlockSpec((tile, W), lambda r, s: (r, 0))),
        out_shape=jax.ShapeDtypeStruct((R, W), F32),
        compiler_params=_cp("arbitrary"),
    )(place, own, got, got, got)


def allreduce_small(v):
    rows, W = v.shape

    def body(v_ref, o_ref, buf, send_sems, recv_sems):
        x, y, c, _ = _place()
        me = 4 * x + 2 * y + c
        buf[me] = v_ref[...]
        sends = []
        for r in range(1, N_DEV):
            to = (x ^ (r >> 2), y ^ ((r >> 1) & 1), c ^ (r & 1))
            sends.append(pltpu.make_async_remote_copy(src_ref=v_ref, dst_ref=buf.at[me], send_sem=send_sems.at[r - 1],
                                                      recv_sem=recv_sems.at[r - 1], device_id=to, device_id_type=MESH))
        for cp in sends:
            cp.start()
        for r in range(1, N_DEV):
            pltpu.make_async_remote_copy(src_ref=v_ref, dst_ref=buf.at[me ^ r], send_sem=send_sems.at[r - 1],
                                         recv_sem=recv_sems.at[r - 1], device_id=(x, y, c), device_id_type=MESH).wait_recv()
        for cp in sends:
            cp.wait_send()
        acc = buf[0]
        for d in range(1, N_DEV):
            acc = acc + buf[d]
        o_ref[...] = acc

    vm = pl.BlockSpec(memory_space=pltpu.VMEM)
    return pl.pallas_call(
        body, name="allreduce_small", out_shape=jax.ShapeDtypeStruct(v.shape, v.dtype),
        in_specs=[vm], out_specs=vm,
        scratch_shapes=[pltpu.VMEM((N_DEV, rows, W), F32), pltpu.SemaphoreType.DMA((N_DEV - 1,)),
                        pltpu.SemaphoreType.DMA((N_DEV - 1,))],
    )(v)


def adamw(w, g, m, v, *, name):
    R, C = w.shape
    tr = R if R <= 512 else 512

    def body(w_ref, g_ref, m_ref, v_ref, d_ref, nm_ref, nv_ref):
        gv = g_ref[...]
        nm = ADAM_B1 * m_ref[...] + (1.0 - ADAM_B1) * gv
        nv = ADAM_B2 * v_ref[...] + (1.0 - ADAM_B2) * jnp.square(gv)
        m_hat = nm / (1.0 - ADAM_B1 ** ADAM_STEP)
        v_hat = nv / (1.0 - ADAM_B2 ** ADAM_STEP)
        d_ref[...] = -ADAM_LR * (m_hat / (jnp.sqrt(v_hat) + ADAM_EPS) + ADAM_WD * w_ref[...])
        nm_ref[...] = nm
        nv_ref[...] = nv

    spec = pl.BlockSpec((tr, C), lambda i: (i, 0))
    out = jax.ShapeDtypeStruct((R, C), F32)
    return pl.pallas_call(
        body, name=name, grid=(R // tr,), in_specs=[spec] * 4, out_specs=[spec] * 3, out_shape=[out] * 3,
        compiler_params=_cp("parallel"),
    )(w, g, m, v)


def _tile_rows(v):
    flat = v.reshape(-1).astype(F32)
    rows = -(-flat.shape[0] // (SUBLANES * FLAT_W)) * SUBLANES
    return jnp.pad(flat, (0, rows * FLAT_W - flat.shape[0])).reshape(rows, FLAT_W)


def _pack_small(get, final):
    parts = [_tile_rows(get(n, l)) for l in range(DEPTH) for n in SMALL]
    return jnp.concatenate(parts + [_tile_rows(final)], axis=0)


def _unpack_small(packed, shapes):
    out = {n: [] for n in SMALL}
    r = 0
    for l in range(DEPTH):
        for n in SMALL:
            size = math.prod(shapes[n])
            rows = -(-size // (SUBLANES * FLAT_W)) * SUBLANES
            out[n].append(packed[r:r + rows].reshape(-1)[:size].reshape(shapes[n]))
            r += rows
    res = {n: jnp.stack(v) for n, v in out.items()}
    res["final_norm_g"] = packed[r:r + SUBLANES].reshape(-1)[:D]
    return res


def _full_matrix(slabs, name):
    K, N = FULL_SHAPES[name]
    if name in COL_SHARDED:
        return slabs.reshape(N_CHIPS, K, N // N_CHIPS).transpose(1, 0, 2).reshape(K, N)
    return slabs.reshape(K, N)


def _first_row(parts, name):
    r = 0
    for n, rows in parts:
        if n == name:
            return r, rows
        r += rows
    raise KeyError(name)


class _Exchange:
    CARRIERS = {
        ("conv_bwd_ln", 1): ((1, "A"), "swap"), ("attn_bwd", 1): ((1, "A"), "exchange"), ("tn_in", 1): ((1, "A"), "share"),
        ("mlp_bwd", 0): ((1, "B"), "swap"), ("tn_mlp1", 0): ((1, "B"), "exchange"), ("tn_mlp2", 0): ((1, "B"), "share"),
        ("conv_bwd_ln", 0): ((0, "A"), "swap"), ("attn_bwd", 0): ((0, "A"), "exchange"), ("tn_in", 0): ((0, "A"), "share"),
    }

    def __init__(self, w, ci, chip):
        self.w, self.ci, self.chip = w, ci, chip
        self.place = jnp.stack([ci, chip]).astype(jnp.int32)
        self.wsh = [jnp.concatenate([w[n][l].reshape(rows, FLAT_W) for n, rows in FLAT_PARTS], axis=0).astype(CDT)
                    for l in range(DEPTH)]
        self.final_g = w["final_norm_g"].reshape(1, D)
        self.slabs = {}
        self.full = {}
        self.units = {}
        self.reduced = {}
        self._landed_weights(0, 0, W_IN_ROWS, _run_alone(gather_rider(self.wsh[0][:W_IN_ROWS]), "gather_w_in")[0])
        taps = jnp.zeros((DEPTH, CONV_K, CONV_C), F32)
        taps = lax.dynamic_update_slice(taps, jnp.where(ci == 0, w["conv_w"], 0.0), (0, 0, chip * (CONV_C // N_CHIPS)))
        self.all_taps = allreduce_small(_tile_rows(taps)).reshape(-1)[:taps.size].reshape(taps.shape)

    def _landed_weights(self, l, r0, r1, buf):
        own = self.wsh[l][r0:r1]
        self.slabs.setdefault(l, []).append((r0, lax.dynamic_update_slice(buf, own[None], (self.chip, 0, 0))))

    def _matrix(self, l, name):
        if (l, name) not in self.full:
            r, rows = _first_row(FLAT_PARTS, name)
            r0, buf = next((r0, buf) for r0, buf in self.slabs[l] if r0 <= r < r0 + buf.shape[1])
            self.full[(l, name)] = _full_matrix(buf[:, r - r0:r - r0 + rows], name)
        return self.full[(l, name)]

    def w_in(self, l):
        return self._matrix(l, "w_in")

    def mats(self, l):
        return {n: self._matrix(l, n) for n in MATRICES if n != "w_in"}

    def vec(self, n, l):
        return self.w[n][l].reshape(1, -1)

    def sinks(self, l):
        return self.w["sinks"][l]

    def taps(self, l):
        return self.all_taps[l]

    def rider(self, kernel, l):
        if (kernel, l) == ("rms_inproj", 0):
            return gather_rider(self.wsh[0][W_IN_ROWS:])
        if (kernel, l) == ("mlp_fwd", 0):
            return gather_rider(self.wsh[1])
        if (kernel, l) in self.CARRIERS:
            return self._stage(*self.CARRIERS[(kernel, l)])
        return None

    def landed(self, kernel, l, bufs):
        if (kernel, l) == ("rms_inproj", 0):
            self._landed_weights(0, W_IN_ROWS, FLAT_ROWS, bufs[0])
        elif (kernel, l) == ("mlp_fwd", 0):
            self._landed_weights(1, 0, FLAT_ROWS, bufs[0])
        else:
            self._stage_landed(*self.CARRIERS[(kernel, l)], bufs[0])

    def grads(self, l, group, g):
        if group == "A":
            flat = jnp.concatenate([g[n].reshape(N_CHIPS, rows, FLAT_W) for n, rows in GROUP_A], axis=1)
        else:
            flat = g["w_in"].reshape(D, N_CHIPS, IN_W // N_CHIPS).transpose(1, 0, 2).reshape(N_CHIPS, W_IN_ROWS, FLAT_W)
        self.units[(l, group)] = {"g": flat}

    def _stage(self, key, stage):
        u = self.units[key]
        if stage == "swap":
            return swap_rider(u["g"])
        if stage == "exchange":
            u["pb"], u["own"] = pair_sum(u["g"], u["swap"], self.place)
            return exchange_rider(u["pb"])
        u["tot"] = total_sum(u["own"], u["exchange"], self.place)
        return share_rider(u["tot"])

    def _stage_landed(self, key, stage, buf):
        u = self.units[key]
        u[stage] = buf
        if stage == "share":
            tot = u["tot"]
            self.reduced[key] = jnp.where(self.ci == 0, jnp.concatenate([tot, buf]), jnp.concatenate([buf, tot]))

    def finish(self):
        key = (0, "B")
        for stage in ("swap", "exchange", "share"):
            self._stage_landed(key, stage, _run_alone(self._stage(key, stage), stage + "_last")[0])
        out = {}
        for n in MATRICES:
            per_layer = []
            for l in range(DEPTH):
                if n == "w_in":
                    flat = self.reduced[(l, "B")]
                else:
                    r, rows = _first_row(GROUP_A, n)
                    flat = self.reduced[(l, "A")][r:r + rows]
                per_layer.append(flat.reshape(self.w[n].shape[1:]))
            out[n] = jnp.stack(per_layer)
        return out


WEIGHTS = ("mix_norm_g", "w_in", "b_in", "sinks", "conv_w", "conv_b", "conv_ln_g", "conv_ln_b", "w_attn_proj",
           "w_conv_proj", "b_conv_proj", "w_out", "mlp_norm_g", "w_mlp1", "w_mlp2", "final_norm_g")


def kernel(x, mix_norm_g, w_in, b_in, sinks, conv_w, conv_b, conv_ln_g, conv_ln_b, w_attn_proj, w_conv_proj, b_conv_proj, w_out, mlp_norm_g, w_mlp1, w_mlp2, final_norm_g, loss_target, m_mix_norm_g, m_w_in, m_b_in, m_sinks, m_conv_w, m_conv_b, m_conv_ln_g, m_conv_ln_b, m_w_attn_proj, m_w_conv_proj, m_b_conv_proj, m_w_out, m_mlp_norm_g, m_w_mlp1, m_w_mlp2, m_final_norm_g, v_mix_norm_g, v_w_in, v_b_in, v_sinks, v_conv_w, v_conv_b, v_conv_ln_g, v_conv_ln_b, v_w_attn_proj, v_w_conv_proj, v_b_conv_proj, v_w_out, v_mlp_norm_g, v_w_mlp1, v_w_mlp2, v_final_norm_g):
    w = dict(zip(WEIGHTS, (mix_norm_g, w_in, b_in, sinks, conv_w, conv_b, conv_ln_g, conv_ln_b, w_attn_proj, w_conv_proj,
                           b_conv_proj, w_out, mlp_norm_g, w_mlp1, w_mlp2, final_norm_g)))
    m = dict(zip(WEIGHTS, (m_mix_norm_g, m_w_in, m_b_in, m_sinks, m_conv_w, m_conv_b, m_conv_ln_g, m_conv_ln_b, m_w_attn_proj,
                           m_w_conv_proj, m_b_conv_proj, m_w_out, m_mlp_norm_g, m_w_mlp1, m_w_mlp2, m_final_norm_g)))
    v = dict(zip(WEIGHTS, (v_mix_norm_g, v_w_in, v_b_in, v_sinks, v_conv_w, v_conv_b, v_conv_ln_g, v_conv_ln_b, v_w_attn_proj,
                           v_w_conv_proj, v_b_conv_proj, v_w_out, v_mlp_norm_g, v_w_mlp1, v_w_mlp2, v_final_norm_g)))
    xi, yi, ci = lax.axis_index("x"), lax.axis_index("y"), lax.axis_index("c")
    chip = 2 * xi + yi

    hooks = _Exchange(w, ci, chip)
    loss, dx, gsm = forward_backward(x[0], loss_target[0], hooks)
    loss = lax.psum(loss[0, 0], ("x", "y", "c"))
    grads = hooks.finish()

    small_shapes = {n: w[n].shape[1:] for n in SMALL}
    small_shapes["conv_w"] = (CONV_K, CONV_C)
    gsmall = _unpack_small(allreduce_small(_pack_small(lambda n, l: gsm[n][l], gsm["final_norm_g"])), small_shapes)
    gsmall["conv_w"] = lax.dynamic_slice(gsmall["conv_w"], (0, 0, chip * (CONV_C // N_CHIPS)),
                                         (DEPTH, CONV_K, CONV_C // N_CHIPS))
    grads.update(gsmall)

    delta, new_m, new_v = {}, {}, {}
    for n in MATRICES:
        shp = w[n].shape
        two_d = (shp[0] * shp[1], shp[2])
        d_, m_, v_ = adamw(w[n].reshape(two_d), grads[n].reshape(two_d), m[n].reshape(two_d), v[n].reshape(two_d),
                           name="adamw_" + n)
        delta[n], new_m[n], new_v[n] = d_.reshape(shp), m_.reshape(shp), v_.reshape(shp)
    small_shapes["conv_w"] = w["conv_w"].shape[1:]
    sm = [_pack_small(lambda n, l, t=t: t[n][l], t["final_norm_g"]) for t in (w, grads, m, v)]
    outs = adamw(*sm, name="adamw_small")
    for dst, packed in zip((delta, new_m, new_v), outs):
        dst.update(_unpack_small(packed, small_shapes))

    return (loss, dx[None], *[grads[n] for n in WEIGHTS], *[delta[n] for n in WEIGHTS],
            *[new_m[n] for n in WEIGHTS], *[new_v[n] for n in WEIGHTS])
```

```python
import functools
import math
from typing import Callable, NamedTuple

import jax
import jax.numpy as jnp
import numpy as np
from jax import lax
from jax.experimental import pallas as pl
from jax.experimental.pallas import tpu as pltpu

F32 = jnp.float32
CDT = jnp.bfloat16

D = 1024
DEPTH = 2
N_Q = 8
HEAD_DIM = 64
ATTN_W = 512
KV_W = 128
BLOCK = 128
CONV_C = 512
CONV_K = 31
D_FF = 4096
IN_W = 3840
QKV_W = ATTN_W + 2 * KV_W
REST_W = IN_W - QKV_W
EPS = 1e-6
NEG = -1e30
SCALE = 1.0 / math.sqrt(HEAD_DIM)
SLOPES = [float(2.0 ** (-8.0 * (h + 1) / N_Q)) for h in range(N_Q)]
SUBLANES = 8
HALO = 32

ADAM_LR = 0.001
ADAM_B1 = 0.9
ADAM_B2 = 0.999
ADAM_EPS = 1e-08
ADAM_WD = 0.01
ADAM_STEP = 10

VMEM_LIMIT = 56 * 1024 * 1024


def _cp(*sem):
    return pltpu.CompilerParams(dimension_semantics=sem, vmem_limit_bytes=VMEM_LIMIT)


def _dot(a, b):
    return jnp.dot(a, b, preferred_element_type=F32)


def _dot_nt(a, b):
    return lax.dot_general(a, b, (((1,), (1,)), ((), ())), preferred_element_type=F32)


def _dot_tn(a, b):
    return lax.dot_general(a, b, (((0,), (0,)), ((), ())), preferred_element_type=F32)


def _sig(x):
    return 1.0 / (1.0 + jnp.exp(-x))


def _colsum(v):
    return jnp.sum(v, axis=0, keepdims=True)


def _const(shape, buffers=None):
    mode = {} if buffers is None else {"pipeline_mode": pl.Buffered(buffers)}
    return pl.BlockSpec(shape, lambda *_: (0,) * len(shape), **mode)


class Rider(NamedTuple):
    ins: tuple
    outs: tuple
    n_sems: int
    start: Callable
    finish: Callable


def _any():
    return pl.BlockSpec(memory_space=pl.ANY)


def _run(body, rider, *, name, grid, in_specs, out_specs, out_shape, args, sem, scratch_shapes=()):
    if rider is None:
        return pl.pallas_call(body, name=name, grid=grid, in_specs=list(in_specs), out_specs=list(out_specs),
                              out_shape=list(out_shape), scratch_shapes=list(scratch_shapes),
                              compiler_params=_cp(*sem))(*args)
    n_in, n_out, n_sc = len(in_specs), len(out_specs), len(scratch_shapes)
    r_in, r_out = len(rider.ins), len(rider.outs)

    def riding(*refs):
        ins, rins = refs[:n_in], refs[n_in:n_in + r_in]
        o0 = n_in + r_in
        outs, routs = refs[o0:o0 + n_out], refs[o0 + n_out:o0 + n_out + r_out]
        s0 = o0 + n_out + r_out
        scratch, sems = refs[s0:s0 + n_sc], refs[s0 + n_sc]
        first = functools.reduce(jnp.logical_and, [pl.program_id(a) == 0 for a in range(len(grid))])
        last = functools.reduce(jnp.logical_and, [pl.program_id(a) == grid[a] - 1 for a in range(len(grid))])

        @pl.when(first)
        def _():
            rider.start(rins, routs, sems)

        body(*ins, *outs, *scratch)

        @pl.when(last)
        def _():
            rider.finish(rins, routs, sems)

    res = pl.pallas_call(
        riding, name=name, grid=grid, in_specs=list(in_specs) + [_any()] * r_in,
        out_specs=list(out_specs) + [_any()] * r_out, out_shape=list(out_shape) + list(rider.outs),
        scratch_shapes=list(scratch_shapes) + [pltpu.SemaphoreType.DMA((rider.n_sems,))],
        compiler_params=_cp(*["arbitrary"] * len(grid)))(*args, *rider.ins)
    return res[:n_out], res[n_out:]


def _run_alone(rider, name):
    def body(*refs):
        r_in, r_out = len(rider.ins), len(rider.outs)
        rins, routs, sems = refs[:r_in], refs[r_in:r_in + r_out], refs[r_in + r_out]
        rider.start(rins, routs, sems)
        rider.finish(rins, routs, sems)

    return pl.pallas_call(
        body, name=name, in_specs=[_any()] * len(rider.ins), out_specs=[_any()] * len(rider.outs),
        out_shape=list(rider.outs), scratch_shapes=[pltpu.SemaphoreType.DMA((rider.n_sems,))])(*rider.ins)


def rms_inproj(x, g, w, b, *, tm=512, rider=None):
    T = x.shape[0]

    def body(x_ref, g_ref, w_ref, b_ref, qkv_ref, rest_ref):
        xv = x_ref[...]
        r = lax.rsqrt(jnp.mean(xv * xv, axis=-1, keepdims=True) + EPS)
        h = (xv * r * g_ref[...]).astype(CDT)
        qkv_ref[...] = (_dot(h, w_ref[:, 0:QKV_W]) + b_ref[:, 0:QKV_W]).astype(qkv_ref.dtype)
        for j in range(REST_W // D):
            c0 = QKV_W + D * j
            rest_ref[:, D * j:D * (j + 1)] = _dot(h, w_ref[:, c0:c0 + D]) + b_ref[:, c0:c0 + D]

    return _run(
        body, rider, name="rms_inproj", grid=(T // tm,),
        in_specs=[pl.BlockSpec((tm, D), lambda i: (i, 0)), _const((1, D)), _const((D, IN_W), 1), _const((1, IN_W))],
        out_specs=[pl.BlockSpec((tm, QKV_W), lambda i: (i, 0)), pl.BlockSpec((tm, REST_W), lambda i: (i, 0))],
        out_shape=[jax.ShapeDtypeStruct((T, QKV_W), CDT), jax.ShapeDtypeStruct((T, REST_W), F32)],
        sem=("parallel",), args=(x, g, w, b))


def _lane_halves(shape):
    lane = lax.broadcasted_iota(jnp.int32, shape, 1)
    return lane < HEAD_DIM, lane >= HEAD_DIM


def _swap_halves(v):
    return pltpu.roll(v.astype(F32), HEAD_DIM, axis=1).astype(v.dtype)


N_KV = KV_W // HEAD_DIM
GROUP = N_Q // N_KV
STACK = GROUP * BLOCK


def _attn_masks(first):
    row = lax.broadcasted_iota(jnp.int32, (STACK, 2 * BLOCK), 0) & (BLOCK - 1)
    col = lax.broadcasted_iota(jnp.int32, (STACK, 2 * BLOCK), 1)
    dist = row + BLOCK - col
    valid = (dist >= 0) & (dist < BLOCK) & ((col >= BLOCK) | jnp.logical_not(first))
    return valid, dist.astype(F32)


def _per_head_column(vals):
    row = lax.broadcasted_iota(jnp.int32, (STACK, 1), 0)
    col = jnp.full((STACK, 1), vals[GROUP - 1], F32)
    for i in reversed(range(GROUP - 1)):
        col = jnp.where(row < (i + 1) * BLOCK, vals[i], col)
    return col


def _stack_heads(dst, src_ref, r0, g):
    lane = lax.broadcasted_iota(jnp.int32, (BLOCK, 2 * HEAD_DIM), 1)
    keep = (lane >= HEAD_DIM) if g else (lane < HEAD_DIM)
    for i in range(GROUP):
        h = GROUP * g + i
        tile = src_ref[pl.ds(r0, BLOCK), (h // 2) * 128:(h // 2 + 1) * 128]
        if h % 2 != g:
            tile = _swap_halves(tile)
        dst[i * BLOCK:(i + 1) * BLOCK, :] = jnp.where(keep, tile, jnp.zeros_like(tile))


def _unstack_heads(dst_ref, stacked, r0, g):
    lane = lax.broadcasted_iota(jnp.int32, (BLOCK, 2 * HEAD_DIM), 1)
    for j in range(GROUP // 2):
        even = stacked[(2 * j) * BLOCK:(2 * j + 1) * BLOCK, :]
        odd = stacked[(2 * j + 1) * BLOCK:(2 * j + 2) * BLOCK, :]
        lo = _swap_halves(even) if g else even
        hi = odd if g else _swap_halves(odd)
        pair = (GROUP * g) // 2 + j
        dst_ref[pl.ds(r0, BLOCK), pair * 128:(pair + 1) * 128] = jnp.where(lane < HEAD_DIM, lo, hi).astype(dst_ref.dtype)


def _qkv_specs(tq):
    nb = tq // BLOCK
    return [
        pl.BlockSpec((tq, ATTN_W), lambda i: (i, 0)),
        pl.BlockSpec((BLOCK, KV_W), lambda i: (jnp.maximum(i * nb - 1, 0), ATTN_W // KV_W)),
        pl.BlockSpec((tq, KV_W), lambda i: (i, ATTN_W // KV_W)),
        pl.BlockSpec((BLOCK, KV_W), lambda i: (jnp.maximum(i * nb - 1, 0), ATTN_W // KV_W + 1)),
        pl.BlockSpec((tq, KV_W), lambda i: (i, ATTN_W // KV_W + 1)),
    ]


def attn_fwd(qkv, sinks, *, tq=512):
    T = qkv.shape[0]
    nb = tq // BLOCK

    def body(sink_ref, q_ref, kp_ref, kc_ref, vp_ref, vc_ref, o_ref, lse_ref, kext, vext, qs):
        i = pl.program_id(0)
        kext[0:BLOCK, :] = kp_ref[...]
        kext[BLOCK:, :] = kc_ref[...]
        vext[0:BLOCK, :] = vp_ref[...]
        vext[BLOCK:, :] = vc_ref[...]
        lane_l = lax.broadcasted_iota(jnp.int32, (BLOCK, 128), 1)

        def blk(b, carry):
            r0 = pl.multiple_of(b * BLOCK, BLOCK)
            valid, distf = _attn_masks(jnp.logical_and(i == 0, b == 0))
            kc = kext[pl.ds(r0, 2 * BLOCK), :]
            vc = vext[pl.ds(r0, 2 * BLOCK), :]
            lse_t = jnp.zeros((BLOCK, 128), F32)
            for g in range(N_KV):
                heads = range(GROUP * g, GROUP * (g + 1))
                _stack_heads(qs, q_ref, r0, g)
                s = _dot_nt(qs[...], kc) * SCALE - _per_head_column([SLOPES[h] for h in heads]) * distf
                s = jnp.where(valid, s, NEG)
                sink = _per_head_column([sink_ref[h] for h in heads])
                m = jnp.maximum(jnp.max(s, axis=-1, keepdims=True), sink)
                p = jnp.exp(s - m)
                denom = jnp.sum(p, axis=-1, keepdims=True) + jnp.exp(sink - m)
                p = p / denom
                _unstack_heads(o_ref, _dot(p.astype(CDT), vc), r0, g)
                lse = m + jnp.log(denom)
                for i_h, h in enumerate(heads):
                    lse_t = jnp.where(lane_l == h, lse[i_h * BLOCK:(i_h + 1) * BLOCK, :], lse_t)
            lse_ref[pl.ds(r0, BLOCK), :] = lse_t
            return carry

        lax.fori_loop(0, nb, blk, 0)

    return pl.pallas_call(
        body, name="attn_fwd", grid=(T // tq,),
        in_specs=[pl.BlockSpec(memory_space=pltpu.SMEM)] + _qkv_specs(tq),
        out_specs=[pl.BlockSpec((tq, ATTN_W), lambda i: (i, 0)), pl.BlockSpec((tq, 128), lambda i: (i, 0))],
        out_shape=[jax.ShapeDtypeStruct((T, ATTN_W), CDT), jax.ShapeDtypeStruct((T, 128), F32)],
        scratch_shapes=[pltpu.VMEM((tq + BLOCK, KV_W), CDT), pltpu.VMEM((tq + BLOCK, KV_W), CDT),
                        pltpu.VMEM((STACK, 2 * HEAD_DIM), CDT)],
        compiler_params=_cp("parallel"),
    )(sinks, qkv, qkv, qkv, qkv, qkv)


def _halo_before(tm, width, col):
    return pl.BlockSpec((HALO, width), lambda i: (jnp.maximum(i * (tm // HALO) - 1, 0), col))


def _fill_u0(ext, a_ref, b_ref, ha_ref, hb_ref, first):
    hu = ha_ref[...] * _sig(hb_ref[...])
    ext[0:HALO, :] = jnp.where(first, jnp.zeros_like(hu), hu)
    ext[HALO:, :] = a_ref[...] * _sig(b_ref[...])


def _shifted_taps(src, w_ref, base, rc, offsets):
    acc = jnp.zeros((rc, CONV_C), F32)
    for b in range(SUBLANES):
        taps = [(k, o - b) for k, o in enumerate(offsets) if o % SUBLANES == b]
        if not taps:
            continue
        rows = rc if b == 0 else rc + SUBLANES
        part = jnp.zeros((rows, CONV_C), F32)
        for k, o8 in taps:
            part = part + w_ref[k:k + 1, :] * src[base + o8:base + o8 + rows, :]
        acc = acc + (part if b == 0 else part[b:b + rc, :])
    return acc


def _conv_rows(ext, w_ref, r0, rc):
    return _shifted_taps(ext, w_ref, r0, rc, [HALO - (CONV_K - 1) + k for k in range(CONV_K)])


def _layer_norm(u1, g, b):
    mu = jnp.mean(u1, axis=-1, keepdims=True)
    xc = u1 - mu
    rstd = lax.rsqrt(jnp.mean(xc * xc, axis=-1, keepdims=True) + EPS)
    n = xc * rstd
    return n, rstd, n * g + b


CONV_RC = 32


def conv_fwd(rest, cw, cb, lg, lb, *, tm=512):
    T = rest.shape[0]

    def body(a_ref, b_ref, ha_ref, hb_ref, w_ref, cb_ref, lg_ref, lb_ref, o_ref, u1_ref, ext):
        _fill_u0(ext, a_ref, b_ref, ha_ref, hb_ref, pl.program_id(0) == 0)
        for c in range(tm // CONV_RC):
            r0 = c * CONV_RC
            u1 = _conv_rows(ext, w_ref, r0, CONV_RC) + cb_ref[...]
            u1_ref[r0:r0 + CONV_RC, :] = u1
            _, _, u2 = _layer_norm(u1, lg_ref[...], lb_ref[...])
            o_ref[r0:r0 + CONV_RC, :] = (u2 * _sig(u2)).astype(o_ref.dtype)

    row = lambda i: (i, 0)
    return pl.pallas_call(
        body, name="conv_fwd", grid=(T // tm,),
        in_specs=[pl.BlockSpec((tm, CONV_C), row), pl.BlockSpec((tm, CONV_C), lambda i: (i, 1)),
                  _halo_before(tm, CONV_C, 0), _halo_before(tm, CONV_C, 1),
                  _const((CONV_K, CONV_C)), _const((1, CONV_C)), _const((1, CONV_C)), _const((1, CONV_C))],
        out_specs=[pl.BlockSpec((tm, CONV_C), row), pl.BlockSpec((tm, CONV_C), row)],
        out_shape=[jax.ShapeDtypeStruct((T, CONV_C), CDT), jax.ShapeDtypeStruct((T, CONV_C), F32)],
        scratch_shapes=[pltpu.VMEM((tm + HALO, CONV_C), F32)],
        compiler_params=_cp("parallel"),
    )(rest, rest, rest, rest, cw, cb, lg, lb)


def merge_out(x, attn, u3, rest, wa, wc, bc, wo, *, tm=512):
    T = x.shape[0]

    def body(x_ref, at_ref, u_ref, ga_ref, gc_ref, wa_ref, wc_ref, bc_ref, wo_ref, o_ref):
        br_a = _dot(at_ref[...], wa_ref[...])
        br_c = _dot(u_ref[...], wc_ref[...]) + bc_ref[...]
        merged = _sig(ga_ref[...]) * br_a + _sig(gc_ref[...]) * br_c
        o_ref[...] = x_ref[...] + _dot(merged.astype(CDT), wo_ref[...])

    return pl.pallas_call(
        body, name="merge_out", grid=(T // tm,),
        in_specs=[pl.BlockSpec((tm, D), lambda i: (i, 0)), pl.BlockSpec((tm, ATTN_W), lambda i: (i, 0)),
                  pl.BlockSpec((tm, CONV_C), lambda i: (i, 0)),
                  pl.BlockSpec((tm, D), lambda i: (i, 1)), pl.BlockSpec((tm, D), lambda i: (i, 2)),
                  _const((ATTN_W, D), 1), _const((CONV_C, D), 1), _const((1, D)), _const((D, D), 1)],
        out_specs=pl.BlockSpec((tm, D), lambda i: (i, 0)),
        out_shape=jax.ShapeDtypeStruct((T, D), F32),
        compiler_params=_cp("parallel"),
    )(x, attn, u3, rest, rest, wa, wc, bc, wo)


def mlp_fwd(x, g, w1, w2, *, tm=256, tf=D_FF, rider=None):
    T = x.shape[0]
    nf = D_FF // tf

    def body(x_ref, g_ref, w1_ref, w2_ref, o_ref, pre_ref, h_s, acc_s):
        f = pl.program_id(1)

        @pl.when(f == 0)
        def _():
            xv = x_ref[...]
            r = lax.rsqrt(jnp.mean(xv * xv, axis=-1, keepdims=True) + EPS)
            h_s[...] = (xv * r * g_ref[...]).astype(CDT)
            acc_s[...] = jnp.zeros_like(acc_s)

        pre = _dot(h_s[...], w1_ref[...])
        pre_ref[...] = pre
        a = jnp.square(jnp.maximum(pre, 0.0))
        acc_s[...] += _dot(a.astype(CDT), w2_ref[...])

        @pl.when(f == nf - 1)
        def _():
            o_ref[...] = x_ref[...] + acc_s[...]

    mode = {"pipeline_mode": pl.Buffered(1)} if nf == 1 else {}
    return _run(
        body, rider, name="mlp_fwd", grid=(T // tm, nf),
        in_specs=[pl.BlockSpec((tm, D), lambda i, f: (i, 0)), _const((1, D)),
                  pl.BlockSpec((D, tf), lambda i, f: (0, f), **mode), pl.BlockSpec((tf, D), lambda i, f: (f, 0), **mode)],
        out_specs=[pl.BlockSpec((tm, D), lambda i, f: (i, 0)), pl.BlockSpec((tm, tf), lambda i, f: (i, f))],
        out_shape=[jax.ShapeDtypeStruct((T, D), F32), jax.ShapeDtypeStruct((T, D_FF), F32)],
        scratch_shapes=[pltpu.VMEM((tm, D), CDT), pltpu.VMEM((tm, D), F32)],
        sem=("parallel", "arbitrary"), args=(x, g, w1, w2))


def _rms_bwd(xv, g, dh):
    r = lax.rsqrt(jnp.mean(xv * xv, axis=-1, keepdims=True) + EPS)
    xhat = xv * r
    dxh = dh * g
    dx = r * (dxh - xhat * jnp.mean(dxh * xhat, axis=-1, keepdims=True))
    return dx, dh * xhat


def loss_head(x, g, tgt, *, tm=512):
    T = x.shape[0]

    def body(x_ref, g_ref, t_ref, dx_ref, dg_ref, loss_ref):
        @pl.when(pl.program_id(0) == 0)
        def _():
            dg_ref[...] = jnp.zeros_like(dg_ref)
            loss_ref[...] = jnp.zeros_like(loss_ref)

        xv = x_ref[...]
        gv = g_ref[...]
        r = lax.rsqrt(jnp.mean(xv * xv, axis=-1, keepdims=True) + EPS)
        e = xv * r * gv - t_ref[...]
        loss_ref[...] += 0.5 * jnp.sum(jnp.mean(e * e, axis=-1, keepdims=True), axis=0, keepdims=True)
        dx, dg_rows = _rms_bwd(xv, gv, e * (1.0 / D))
        dx_ref[...] = dx
        dg_ref[...] += _colsum(dg_rows)

    return pl.pallas_call(
        body, name="loss_head", grid=(T // tm,),
        in_specs=[pl.BlockSpec((tm, D), lambda i: (i, 0)), _const((1, D)), pl.BlockSpec((tm, D), lambda i: (i, 0))],
        out_specs=[pl.BlockSpec((tm, D), lambda i: (i, 0)), _const((1, D)), _const((1, 128))],
        out_shape=[jax.ShapeDtypeStruct((T, D), F32), jax.ShapeDtypeStruct((1, D), F32),
                   jax.ShapeDtypeStruct((1, 128), F32)],
        compiler_params=_cp("arbitrary"),
    )(x, g, tgt)


def mlp_bwd(dy, x, g, pre, w1, w2, *, tm=256, tf=D_FF, rider=None):
    T = x.shape[0]
    nf = D_FF // tf

    def body(dy_ref, x_ref, g_ref, pre_ref, w1_ref, w2_ref, dx_ref, dg_ref, h_ref, a_ref, dpre_ref, dyb_s, acc_s):
        i, f = pl.program_id(0), pl.program_id(1)

        @pl.when(jnp.logical_and(i == 0, f == 0))
        def _():
            dg_ref[...] = jnp.zeros_like(dg_ref)

        @pl.when(f == 0)
        def _():
            dyb_s[...] = dy_ref[...].astype(CDT)
            acc_s[...] = jnp.zeros_like(acc_s)

        pre = pre_ref[...]
        rl = jnp.maximum(pre, 0.0)
        a_ref[...] = (rl * rl).astype(CDT)
        da = _dot_nt(dyb_s[...], w2_ref[...])
        dpre = (da * (2.0 * rl)).astype(CDT)
        dpre_ref[...] = dpre
        acc_s[...] += _dot_nt(dpre, w1_ref[...])

        @pl.when(f == nf - 1)
        def _():
            xv = x_ref[...]
            gv = g_ref[...]
            dxn, dg_rows = _rms_bwd(xv, gv, acc_s[...])
            dx_ref[...] = dy_ref[...] + dxn
            dg_ref[...] += _colsum(dg_rows)
            r = lax.rsqrt(jnp.mean(xv * xv, axis=-1, keepdims=True) + EPS)
            h_ref[...] = (xv * r * gv).astype(CDT)

    row = lambda i, f: (i, 0)
    mode = {"pipeline_mode": pl.Buffered(1)} if nf == 1 else {}
    return _run(
        body, rider, name="mlp_bwd", grid=(T // tm, nf),
        in_specs=[pl.BlockSpec((tm, D), row), pl.BlockSpec((tm, D), row), _const((1, D)),
                  pl.BlockSpec((tm, tf), lambda i, f: (i, f)),
                  pl.BlockSpec((D, tf), lambda i, f: (0, f), **mode), pl.BlockSpec((tf, D), lambda i, f: (f, 0), **mode)],
        out_specs=[pl.BlockSpec((tm, D), row), _const((1, D)), pl.BlockSpec((tm, D), row),
                   pl.BlockSpec((tm, tf), lambda i, f: (i, f)), pl.BlockSpec((tm, tf), lambda i, f: (i, f))],
        out_shape=[jax.ShapeDtypeStruct((T, D), F32), jax.ShapeDtypeStruct((1, D), F32),
                   jax.ShapeDtypeStruct((T, D), CDT), jax.ShapeDtypeStruct((T, D_FF), CDT),
                   jax.ShapeDtypeStruct((T, D_FF), CDT)],
        scratch_shapes=[pltpu.VMEM((tm, D), CDT), pltpu.VMEM((tm, D), F32)],
        sem=("arbitrary", "arbitrary"), args=(dy, x, g, pre, w1, w2))


def tn_matmul(a, b, *, tm, tn, tk=2048, name, by_chip=False, rider=None):
    T, M = a.shape
    N = b.shape[1]
    tk = min(tk, T)
    nk = T // tk

    def body(a_ref, b_ref, o_ref):
        @pl.when(pl.program_id(2) == 0)
        def _():
            o_ref[...] = jnp.zeros_like(o_ref)

        o_ref[...] += _dot_tn(a_ref[...].astype(CDT), b_ref[...].astype(CDT))

    if by_chip:
        out_spec = pl.BlockSpec((None, tm, tn), lambda i, j, k: (j, i, 0))
        out_shape = jax.ShapeDtypeStruct((N // tn, M, tn), F32)
    else:
        out_spec = pl.BlockSpec((tm, tn), lambda i, j, k: (i, j))
        out_shape = jax.ShapeDtypeStruct((M, N), F32)
    res = _run(
        body, rider, name=name, grid=(M // tm, N // tn, nk),
        in_specs=[pl.BlockSpec((tk, tm), lambda i, j, k: (k, i)), pl.BlockSpec((tk, tn), lambda i, j, k: (k, j))],
        out_specs=[out_spec], out_shape=[out_shape], sem=("parallel", "parallel", "arbitrary"), args=(a, b))
    return res[0] if rider is None else (res[0][0], res[1])


def merge_bwd(dx1, attn, u3, rest, wa, wc, bc, wo, *, tm=512):
    T = dx1.shape[0]

    def body(dx_ref, at_ref, u_ref, ga_ref, gc_ref, wa_ref, wc_ref, bc_ref, wo_ref,
             mg_ref, dba_ref, dbc_ref, dat_ref, du_ref, dgate_ref, dbias_ref):
        @pl.when(pl.program_id(0) == 0)
        def _():
            dbias_ref[...] = jnp.zeros_like(dbias_ref)

        br_a = _dot(at_ref[...], wa_ref[...])
        br_c = _dot(u_ref[...], wc_ref[...]) + bc_ref[...]
        sa = _sig(ga_ref[...])
        sc = _sig(gc_ref[...])
        mg_ref[...] = (sa * br_a + sc * br_c).astype(CDT)
        dm = _dot_nt(dx_ref[...].astype(CDT), wo_ref[...])
        dba = dm * sa
        dbc = dm * sc
        dgate_ref[:, 0:D] = dm * br_a * sa * (1.0 - sa)
        dgate_ref[:, D:2 * D] = dm * br_c * sc * (1.0 - sc)
        dbias_ref[...] += _colsum(dbc)
        dba_b = dba.astype(CDT)
        dbc_b = dbc.astype(CDT)
        dba_ref[...] = dba_b
        dbc_ref[...] = dbc_b
        dat_ref[...] = _dot_nt(dba_b, wa_ref[...]).astype(CDT)
        du_ref[...] = _dot_nt(dbc_b, wc_ref[...])

    row = lambda i: (i, 0)
    return pl.pallas_call(
        body, name="merge_bwd", grid=(T // tm,),
        in_specs=[pl.BlockSpec((tm, D), row), pl.BlockSpec((tm, ATTN_W), row), pl.BlockSpec((tm, CONV_C), row),
                  pl.BlockSpec((tm, D), lambda i: (i, 1)), pl.BlockSpec((tm, D), lambda i: (i, 2)),
                  _const((ATTN_W, D), 1), _const((CONV_C, D), 1), _const((1, D)), _const((D, D), 1)],
        out_specs=[pl.BlockSpec((tm, D), row), pl.BlockSpec((tm, D), row), pl.BlockSpec((tm, D), row),
                   pl.BlockSpec((tm, ATTN_W), row), pl.BlockSpec((tm, CONV_C), row),
                   pl.BlockSpec((tm, 2 * D), row), _const((1, D))],
        out_shape=[jax.ShapeDtypeStruct((T, D), CDT), jax.ShapeDtypeStruct((T, D), CDT),
                   jax.ShapeDtypeStruct((T, D), CDT), jax.ShapeDtypeStruct((T, ATTN_W), CDT),
                   jax.ShapeDtypeStruct((T, CONV_C), F32), jax.ShapeDtypeStruct((T, 2 * D), F32),
                   jax.ShapeDtypeStruct((1, D), F32)],
        compiler_params=_cp("arbitrary"),
    )(dx1, attn, u3, rest, rest, wa, wc, bc, wo)


def conv_bwd_ln(du3, u1, lg, lb, *, tm=512, rider=None):
    T = du3.shape[0]

    def body(du_ref, u1_ref, lg_ref, lb_ref, du1_ref, dlg_ref, dlb_ref, dcb_ref):
        @pl.when(pl.program_id(0) == 0)
        def _():
            dlg_ref[...] = jnp.zeros_like(dlg_ref)
            dlb_ref[...] = jnp.zeros_like(dlb_ref)
            dcb_ref[...] = jnp.zeros_like(dcb_ref)

        dlg = jnp.zeros((1, CONV_C), F32)
        dlb = jnp.zeros((1, CONV_C), F32)
        dcb = jnp.zeros((1, CONV_C), F32)
        for c in range(tm // CONV_RC):
            r0 = c * CONV_RC
            n, rstd, u2 = _layer_norm(u1_ref[r0:r0 + CONV_RC, :], lg_ref[...], lb_ref[...])
            s = _sig(u2)
            du2 = du_ref[r0:r0 + CONV_RC, :] * (s + u2 * s * (1.0 - s))
            dn = du2 * lg_ref[...]
            du1 = rstd * (dn - jnp.mean(dn, axis=-1, keepdims=True) - n * jnp.mean(dn * n, axis=-1, keepdims=True))
            du1_ref[r0:r0 + CONV_RC, :] = du1
            dlg = dlg + _colsum(du2 * n)
            dlb = dlb + _colsum(du2)
            dcb = dcb + _colsum(du1)
        dlg_ref[...] += dlg
        dlb_ref[...] += dlb
        dcb_ref[...] += dcb

    row = lambda i: (i, 0)
    vec = jax.ShapeDtypeStruct((1, CONV_C), F32)
    return _run(
        body, rider, name="conv_bwd_ln", grid=(T // tm,),
        in_specs=[pl.BlockSpec((tm, CONV_C), row), pl.BlockSpec((tm, CONV_C), row), _const((1, CONV_C)), _const((1, CONV_C))],
        out_specs=[pl.BlockSpec((tm, CONV_C), row), _const((1, CONV_C)), _const((1, CONV_C)), _const((1, CONV_C))],
        out_shape=[jax.ShapeDtypeStruct((T, CONV_C), F32), vec, vec, vec],
        sem=("arbitrary",), args=(du3, u1, lg, lb))


def conv_bwd_taps(du1, rest, cw, *, tm=512):
    T = du1.shape[0]
    nt = T // tm

    def body(d_ref, hd_ref, a_ref, b_ref, ha_ref, hb_ref, w_ref, dglu_ref, dw_ref, ext, dext, dwacc):
        i = pl.program_id(0)

        @pl.when(i == 0)
        def _():
            dwacc[...] = jnp.zeros_like(dwacc)

        _fill_u0(ext, a_ref, b_ref, ha_ref, hb_ref, i == 0)
        dext[0:SUBLANES, :] = jnp.zeros((SUBLANES, CONV_C), F32)
        dext[SUBLANES:SUBLANES + tm, :] = d_ref[...]
        hd = hd_ref[...]
        dext[SUBLANES + tm:, :] = jnp.where(i == nt - 1, jnp.zeros_like(hd), hd)
        qrow = lax.broadcasted_iota(jnp.int32, (CONV_RC + SUBLANES, CONV_C), 0)
        for c in range(tm // CONV_RC):
            r0 = c * CONV_RC
            du0 = _shifted_taps(dext, w_ref, r0 + SUBLANES, CONV_RC, [CONV_K - 1 - k for k in range(CONV_K)])
            for b in range(SUBLANES):
                taps = [(k, HALO - (CONV_K - 1) + k - b) for k in range(CONV_K) if (HALO - (CONV_K - 1) + k) % SUBLANES == b]
                if b == 0:
                    rows = CONV_RC
                    dsh = dext[r0 + SUBLANES:r0 + SUBLANES + rows, :]
                else:
                    rows = CONV_RC + SUBLANES
                    dsh = dext[r0 + SUBLANES - b:r0 + SUBLANES - b + rows, :]
                    dsh = jnp.where((qrow >= b) & (qrow < CONV_RC + b), dsh, 0.0)
                for k, o8 in taps:
                    prod = dsh * ext[r0 + o8:r0 + o8 + rows, :]
                    dwacc[8 * k:8 * k + 8, :] += jnp.sum(prod.reshape(rows // SUBLANES, SUBLANES, CONV_C), axis=0)
            av = a_ref[r0:r0 + CONV_RC, :]
            sb = _sig(b_ref[r0:r0 + CONV_RC, :])
            dglu_ref[r0:r0 + CONV_RC, 0:CONV_C] = du0 * sb
            dglu_ref[r0:r0 + CONV_RC, CONV_C:2 * CONV_C] = du0 * av * sb * (1.0 - sb)

        @pl.when(i == nt - 1)
        def _():
            dw_ref[...] = jnp.zeros_like(dw_ref)
            for k in range(CONV_K):
                dw_ref[k:k + 1, :] = _colsum(dwacc[8 * k:8 * k + 8, :])

    row = lambda i: (i, 0)
    return pl.pallas_call(
        body, name="conv_bwd_taps", grid=(nt,),
        in_specs=[pl.BlockSpec((tm, CONV_C), row),
                  pl.BlockSpec((HALO, CONV_C), lambda i: (jnp.minimum((i + 1) * (tm // HALO), T // HALO - 1), 0)),
                  pl.BlockSpec((tm, CONV_C), row), pl.BlockSpec((tm, CONV_C), lambda i: (i, 1)),
                  _halo_before(tm, CONV_C, 0), _halo_before(tm, CONV_C, 1), _const((CONV_K, CONV_C))],
        out_specs=[pl.BlockSpec((tm, 2 * CONV_C), row), _const((HALO, CONV_C))],
        out_shape=[jax.ShapeDtypeStruct((T, 2 * CONV_C), F32), jax.ShapeDtypeStruct((HALO, CONV_C), F32)],
        scratch_shapes=[pltpu.VMEM((tm + HALO, CONV_C), F32), pltpu.VMEM((SUBLANES + tm + HALO, CONV_C), F32),
                        pltpu.VMEM((8 * CONV_K, CONV_C), F32)],
        compiler_params=_cp("arbitrary"),
    )(du1, du1, rest, rest, rest, rest, cw)


def attn_bwd(qkv, do, lse, sinks, *, tq=512, rider=None):
    T = qkv.shape[0]
    nb = tq // BLOCK

    def body(sink_ref, q_ref, kp_ref, kc_ref, vp_ref, vc_ref, do_ref, lse_ref,
             dq_ref, dkv_ref, spill_ref, dsink_ref, kext, vext, dkext, dvext, qs, dos):
        i = pl.program_id(0)

        @pl.when(i == 0)
        def _():
            dsink_ref[...] = jnp.zeros_like(dsink_ref)

        kext[0:BLOCK, :] = kp_ref[...]
        kext[BLOCK:, :] = kc_ref[...]
        vext[0:BLOCK, :] = vp_ref[...]
        vext[BLOCK:, :] = vc_ref[...]
        dkext[...] = jnp.zeros_like(dkext)
        dvext[...] = jnp.zeros_like(dvext)
        lane_l = lax.broadcasted_iota(jnp.int32, (BLOCK, 128), 1)
        lane_k = lax.broadcasted_iota(jnp.int32, (2 * BLOCK, KV_W), 1)

        def blk(b, dsink):
            r0 = pl.multiple_of(b * BLOCK, BLOCK)
            valid, distf = _attn_masks(jnp.logical_and(i == 0, b == 0))
            kc = kext[pl.ds(r0, 2 * BLOCK), :]
            vc = vext[pl.ds(r0, 2 * BLOCK), :]
            lse_t = lse_ref[pl.ds(r0, BLOCK), :]
            dk = jnp.zeros((2 * BLOCK, KV_W), F32)
            dv = jnp.zeros((2 * BLOCK, KV_W), F32)
            for g in range(N_KV):
                heads = range(GROUP * g, GROUP * (g + 1))
                _stack_heads(qs, q_ref, r0, g)
                _stack_heads(dos, do_ref, r0, g)
                qv = qs[...]
                dov = dos[...]
                s = _dot_nt(qv, kc) * SCALE - _per_head_column([SLOPES[h] for h in heads]) * distf
                s = jnp.where(valid, s, NEG)
                lse = jnp.concatenate(
                    [jnp.sum(jnp.where(lane_l == h, lse_t, 0.0), axis=-1, keepdims=True) for h in heads], axis=0)
                p = jnp.exp(s - lse)
                dp = _dot_nt(dov, vc)
                dd = jnp.sum(p * dp, axis=-1, keepdims=True)
                ds = (p * (dp - dd)).astype(CDT)
                keep = (lane_k >= HEAD_DIM) if g else (lane_k < HEAD_DIM)
                _unstack_heads(dq_ref, _dot(ds, jnp.where(keep, kc, jnp.zeros_like(kc))) * SCALE, r0, g)
                dk = dk + _dot_tn(ds, qv)
                dv = dv + _dot_tn(p.astype(CDT), dov)
                wsink = jnp.exp(_per_head_column([sink_ref[h] for h in heads]) - lse) * dd
                for i_h, h in enumerate(heads):
                    part = jnp.sum(wsink[i_h * BLOCK:(i_h + 1) * BLOCK, :], axis=0, keepdims=True)
                    dsink = dsink - jnp.where(lane_l[0:1, :] == h, part, 0.0)
            dkext[pl.ds(r0, 2 * BLOCK), :] += dk * SCALE
            dvext[pl.ds(r0, 2 * BLOCK), :] += dv
            return dsink

        dsink_ref[...] += lax.fori_loop(0, nb, blk, jnp.zeros((1, 128), F32))
        dkv_ref[:, 0:KV_W] = dkext[BLOCK:, :]
        dkv_ref[:, KV_W:2 * KV_W] = dvext[BLOCK:, :]
        spill_ref[:, 0:KV_W] = dkext[0:BLOCK, :]
        spill_ref[:, KV_W:2 * KV_W] = dvext[0:BLOCK, :]

    row = lambda i: (i, 0)
    return _run(
        body, rider, name="attn_bwd", grid=(T // tq,),
        in_specs=[pl.BlockSpec(memory_space=pltpu.SMEM)] + _qkv_specs(tq)
        + [pl.BlockSpec((tq, ATTN_W), row), pl.BlockSpec((tq, 128), row)],
        out_specs=[pl.BlockSpec((tq, ATTN_W), row), pl.BlockSpec((tq, 2 * KV_W), row),
                   pl.BlockSpec((BLOCK, 2 * KV_W), row), _const((1, 128))],
        out_shape=[jax.ShapeDtypeStruct((T, ATTN_W), F32), jax.ShapeDtypeStruct((T, 2 * KV_W), F32),
                   jax.ShapeDtypeStruct((T // tq * BLOCK, 2 * KV_W), F32), jax.ShapeDtypeStruct((1, 128), F32)],
        scratch_shapes=[pltpu.VMEM((tq + BLOCK, KV_W), CDT), pltpu.VMEM((tq + BLOCK, KV_W), CDT),
                        pltpu.VMEM((tq + BLOCK, KV_W), F32), pltpu.VMEM((tq + BLOCK, KV_W), F32),
                        pltpu.VMEM((STACK, 2 * HEAD_DIM), CDT), pltpu.VMEM((STACK, 2 * HEAD_DIM), CDT)],
        sem=("arbitrary",), args=(sinks, qkv, qkv, qkv, qkv, qkv, do, lse))


def inproj_bwd(dres, x, g, w, dq, dkv, spill, dglu, dgate, *, tm=512):
    T = x.shape[0]
    nt = T // tm

    def body(dr_ref, x_ref, g_ref, w_ref, dq_ref, dkv_ref, sp_ref, dglu_ref, dgate_ref,
             dx_ref, dp_ref, h_ref, dg_ref, db_ref):
        i = pl.program_id(0)

        @pl.when(i == 0)
        def _():
            dg_ref[...] = jnp.zeros_like(dg_ref)
            db_ref[...] = jnp.zeros_like(db_ref)

        sp = sp_ref[...]
        sp = jnp.where(i == nt - 1, jnp.zeros_like(sp), sp)
        pieces = ((0, ATTN_W, dq_ref), (QKV_W, 2 * CONV_C, dglu_ref), (QKV_W + 2 * CONV_C, 2 * D, dgate_ref))
        for c0, wd, ref in pieces:
            v = ref[...]
            db_ref[:, c0:c0 + wd] += _colsum(v)
            dp_ref[:, c0:c0 + wd] = v.astype(CDT)
        dkv = dkv_ref[...]
        db_ref[:, ATTN_W:QKV_W] += _colsum(dkv) + _colsum(sp)
        dp_ref[0:tm - BLOCK, ATTN_W:QKV_W] = dkv[0:tm - BLOCK, :].astype(CDT)
        dp_ref[tm - BLOCK:tm, ATTN_W:QKV_W] = (dkv[tm - BLOCK:tm, :] + sp).astype(CDT)
        dh = _dot_nt(dp_ref[...], w_ref[...])
        xv = x_ref[...]
        gv = g_ref[...]
        dxn, dg_rows = _rms_bwd(xv, gv, dh)
        dx_ref[...] = dr_ref[...] + dxn
        dg_ref[...] += _colsum(dg_rows)
        r = lax.rsqrt(jnp.mean(xv * xv, axis=-1, keepdims=True) + EPS)
        h_ref[...] = (xv * r * gv).astype(CDT)

    row = lambda i: (i, 0)
    return pl.pallas_call(
        body, name="inproj_bwd", grid=(nt,),
        in_specs=[pl.BlockSpec((tm, D), row), pl.BlockSpec((tm, D), row), _const((1, D)), _const((D, IN_W), 1),
                  pl.BlockSpec((tm, ATTN_W), row), pl.BlockSpec((tm, 2 * KV_W), row),
                  pl.BlockSpec((BLOCK, 2 * KV_W), lambda i: (jnp.minimum(i + 1, nt - 1), 0)),
                  pl.BlockSpec((tm, 2 * CONV_C), row), pl.BlockSpec((tm, 2 * D), row)],
        out_specs=[pl.BlockSpec((tm, D), row), pl.BlockSpec((tm, IN_W), row), pl.BlockSpec((tm, D), row),
                   _const((1, D)), _const((1, IN_W))],
        out_shape=[jax.ShapeDtypeStruct((T, D), F32), jax.ShapeDtypeStruct((T, IN_W), CDT),
                   jax.ShapeDtypeStruct((T, D), CDT), jax.ShapeDtypeStruct((1, D), F32),
                   jax.ShapeDtypeStruct((1, IN_W), F32)],
        compiler_params=_cp("arbitrary"),
    )(dres, x, g, w, dq, dkv, spill, dglu, dgate)


ATTN_TILE = 256
MATRICES = ("w_in", "w_attn_proj", "w_conv_proj", "w_out", "w_mlp1", "w_mlp2")
SMALL = ("mix_norm_g", "b_in", "sinks", "conv_w", "conv_b", "conv_ln_g", "conv_ln_b", "b_conv_proj", "mlp_norm_g")


def forward_backward(x, tgt, hooks):
    def call(fn, kernel, l, *args, **kw):
        rider = hooks.rider(kernel, l)
        if rider is None:
            return fn(*args, **kw)
        outs, landed = fn(*args, rider=rider, **kw)
        hooks.landed(kernel, l, landed)
        return outs

    vec = hooks.vec
    saved = []
    for l in range(DEPTH):
        qkv, rest = call(rms_inproj, "rms_inproj", l, x, vec("mix_norm_g", l), hooks.w_in(l), vec("b_in", l))
        m = hooks.mats(l)
        attn, lse = attn_fwd(qkv, hooks.sinks(l), tq=ATTN_TILE)
        u3, u1 = conv_fwd(rest, hooks.taps(l), vec("conv_b", l), vec("conv_ln_g", l), vec("conv_ln_b", l))
        x1 = merge_out(x, attn, u3, rest, m["w_attn_proj"], m["w_conv_proj"], vec("b_conv_proj", l), m["w_out"])
        x2, pre = call(mlp_fwd, "mlp_fwd", l, x1, vec("mlp_norm_g", l), m["w_mlp1"], m["w_mlp2"])
        saved.append((x, qkv, rest, attn, lse, u3, u1, x1, pre))
        x = x2
    dx, dgf, loss = loss_head(x, hooks.final_g, tgt)
    small = {n: [None] * DEPTH for n in SMALL}
    small["final_norm_g"] = dgf
    for l in reversed(range(DEPTH)):
        x0, qkv, rest, attn, lse, u3, u1, x1, pre = saved[l]
        m = hooks.mats(l)
        dx1, dg2, h2, a, dpre = call(mlp_bwd, "mlp_bwd", l, dx, x1, vec("mlp_norm_g", l), pre, m["w_mlp1"], m["w_mlp2"])
        small["mlp_norm_g"][l] = dg2
        group = {}
        group["w_mlp1"] = call(tn_matmul, "tn_mlp1", l, h2, dpre, tm=1024, tn=1024, tk=4096, name="tn_mlp1", by_chip=True)
        group["w_mlp2"] = call(tn_matmul, "tn_mlp2", l, a, dx, tm=1024, tn=1024, name="tn_mlp2")
        merged, dba, dbc, dattn, du3, dgate, dbcp = merge_bwd(
            dx1, attn, u3, rest, m["w_attn_proj"], m["w_conv_proj"], vec("b_conv_proj", l), m["w_out"])
        small["b_conv_proj"][l] = dbcp
        group["w_out"] = tn_matmul(merged, dx1, tm=1024, tn=1024, name="tn_out")
        group["w_attn_proj"] = tn_matmul(attn, dba, tm=512, tn=256, tk=4096, name="tn_attn_proj", by_chip=True)
        group["w_conv_proj"] = tn_matmul(u3, dbc, tm=512, tn=256, tk=4096, name="tn_conv_proj", by_chip=True)
        hooks.grads(l, "A", group)
        du1, dlg, dlb, dcb = call(conv_bwd_ln, "conv_bwd_ln", l, du3, u1, vec("conv_ln_g", l), vec("conv_ln_b", l))
        small["conv_ln_g"][l], small["conv_ln_b"][l], small["conv_b"][l] = dlg, dlb, dcb
        dglu, dcw = conv_bwd_taps(du1, rest, hooks.taps(l))
        small["conv_w"][l] = dcw
        dq, dkv, spill, dsink = call(attn_bwd, "attn_bwd", l, qkv, dattn, lse, hooks.sinks(l), tq=ATTN_TILE)
        small["sinks"][l] = dsink
        dx, dproj, h, dg, db = inproj_bwd(dx1, x0, vec("mix_norm_g", l), hooks.w_in(l), dq, dkv, spill, dglu, dgate,
                                          tm=ATTN_TILE)
        small["mix_norm_g"][l], small["b_in"][l] = dg, db
        hooks.grads(l, "B", {"w_in": call(tn_matmul, "tn_in", l, h, dproj, tm=1024, tn=768, tk=4096, name="tn_in")})
    return loss, dx, small


class _LocalHooks:
    def __init__(self, p):
        self.p = p
        self.final_g = p["final_norm_g"]
        self.got = {n: [None] * DEPTH for n in MATRICES}

    def w_in(self, l):
        return self.p["w_in"][l]

    def mats(self, l):
        return {n: self.p[n][l] for n in MATRICES}

    def vec(self, n, l):
        return self.p[n][l]

    def sinks(self, l):
        return self.p["sinks"][l]

    def taps(self, l):
        return self.p["conv_w"][l]

    def rider(self, kernel, l):
        return None

    def grads(self, l, group, g):
        for n, v in g.items():
            if v.ndim == 3:
                v = v.transpose(1, 0, 2).reshape(v.shape[1], -1)
            self.got[n][l] = v


def local_grads(x, tgt, p):
    hooks = _LocalHooks(p)
    loss, dx, small = forward_backward(x, tgt, hooks)
    small["conv_w"] = [g[0:CONV_K] for g in small["conv_w"]]
    small["sinks"] = [g[0, 0:N_Q] for g in small["sinks"]]
    return loss, dx, {**small, **hooks.got}


MESH = pl.DeviceIdType.MESH
N_CHIPS = 4
N_DEV = 8
FLAT_W = 1024
FLAT_PARTS = (("w_in", 960), ("w_attn_proj", 128), ("w_conv_proj", 128), ("w_out", 256), ("w_mlp1", 1024), ("w_mlp2", 1024))
FLAT_ROWS = sum(r for _, r in FLAT_PARTS)
W_IN_ROWS = FLAT_PARTS[0][1]
GROUP_A = (("w_mlp1", 1024), ("w_mlp2", 1024), ("w_out", 256), ("w_attn_proj", 128), ("w_conv_proj", 128))
COL_SHARDED = ("w_in", "w_attn_proj", "w_conv_proj", "w_mlp1")
FULL_SHAPES = {"w_in": (D, IN_W), "w_attn_proj": (ATTN_W, D), "w_conv_proj": (CONV_C, D), "w_out": (D, D),
               "w_mlp1": (D, D_FF), "w_mlp2": (D_FF, D)}


def _place():
    x, y, c = lax.axis_index("x"), lax.axis_index("y"), lax.axis_index("c")
    return x, y, c, 2 * x + y


def _peer_chips(x, y, j):
    return [((x, 1 - y), j ^ 1), ((1 - x, y), j ^ 2), ((1 - x, 1 - y), j ^ 3)]


def _remote(src, dst, sems, k, n, to):
    return pltpu.make_async_remote_copy(src_ref=src, dst_ref=dst, send_sem=sems.at[k], recv_sem=sems.at[n + k],
                                        device_id=to, device_id_type=MESH)


def _half(c, rows):
    h = rows // 2
    return pl.ds(pl.multiple_of(c * h, 16), h)


def gather_rider(wsh):
    R = wsh.shape[0]

    def plan(rins, routs, sems):
        (w_ref,), (out_ref,) = rins, routs
        x, y, c, j = _place()
        peers = _peer_chips(x, y, j)
        mine, other = _half(c, R), _half(1 - c, R)
        sent = [_remote(w_ref.at[mine], out_ref.at[j, mine], sems, k, 6, (*chip, c)) for k, (chip, _) in enumerate(peers)]
        landed = [_remote(w_ref.at[mine], out_ref.at[pj, mine], sems, k, 6, (x, y, c)) for k, (_, pj) in enumerate(peers)]
        passed = [_remote(out_ref.at[pj, mine], out_ref.at[pj, mine], sems, 3 + k, 6, (x, y, 1 - c))
                  for k, (_, pj) in enumerate(peers)]
        handed = [_remote(w_ref.at[mine], out_ref.at[pj, other], sems, 3 + k, 6, (x, y, c)) for k, (_, pj) in enumerate(peers)]
        return sent, landed, passed, handed

    def start(rins, routs, sems):
        for cp in plan(rins, routs, sems)[0]:
            cp.start()

    def finish(rins, routs, sems):
        sent, landed, passed, handed = plan(rins, routs, sems)
        for k in range(3):
            landed[k].wait_recv()
            passed[k].start()
        for cp in handed:
            cp.wait_recv()
        for cp in sent + passed:
            cp.wait_send()

    return Rider((wsh,), (jax.ShapeDtypeStruct((N_CHIPS,) + wsh.shape, wsh.dtype),), 12, start, finish)


def swap_rider(g):
    R = g.shape[1]

    def plan(rins, routs, sems):
        (g_ref,), (got_ref,) = rins, routs
        x, y, c, _ = _place()
        return _remote(g_ref.at[:, _half(1 - c, R), :], got_ref, sems, 0, 1, (x, y, 1 - c))

    def start(rins, routs, sems):
        plan(rins, routs, sems).start()

    def finish(rins, routs, sems):
        plan(rins, routs, sems).wait()

    return Rider((g,), (jax.ShapeDtypeStruct((N_CHIPS, R // 2, FLAT_W), g.dtype),), 2, start, finish)


def exchange_rider(pb):
    def plan(rins, routs, sems):
        (pb_ref,), (got_ref,) = rins, routs
        x, y, c, j = _place()
        peers = _peer_chips(x, y, j)
        sent = [_remote(pb_ref.at[pj], got_ref.at[j], sems, k, 3, (*chip, c)) for k, (chip, pj) in enumerate(peers)]
        landed = [_remote(pb_ref.at[pj], got_ref.at[pj], sems, k, 3, (x, y, c)) for k, (_, pj) in enumerate(peers)]
        return sent, landed

    def start(rins, routs, sems):
        for cp in plan(rins, routs, sems)[0]:
            cp.start()

    def finish(rins, routs, sems):
        sent, landed = plan(rins, routs, sems)
        for cp in landed:
            cp.wait_recv()
        for cp in sent:
            cp.wait_send()

    return Rider((pb,), (jax.ShapeDtypeStruct(pb.shape, pb.dtype),), 6, start, finish)


def share_rider(tot):
    def plan(rins, routs, sems):
        (t_ref,), (got_ref,) = rins, routs
        x, y, c, _ = _place()
        return _remote(t_ref, got_ref, sems, 0, 1, (x, y, 1 - c))

    def start(rins, routs, sems):
        plan(rins, routs, sems).start()

    def finish(rins, routs, sems):
        plan(rins, routs, sems).wait()

    return Rider((tot,), (jax.ShapeDtypeStruct(tot.shape, tot.dtype),), 2, start, finish)


def pair_sum(g, got):
    nj, R, W = g.shape
    h = R // 2
    tile = h // 2

    def body(g_ref, got_ref, pb_ref, own_ref):
        v = g_ref[...] + got_ref[...]
        pb_ref[...] = v.astype(pb_ref.dtype)

        @pl.when(pl.program_id(1) == _place()[3])
        def _():
            own_ref[...] = v

    return pl.pallas_call(
        body, name="pair_sum", grid=(h // tile, nj),
        in_specs=[pl.BlockSpec((None, tile, W), lambda r, j: (j, lax.axis_index("c") * (h // tile) + r, 0)),
                  pl.BlockSpec((None, tile, W), lambda r, j: (j, r, 0))],
        out_specs=[pl.BlockSpec((None, tile, W), lambda r, j: (j, r, 0)), pl.BlockSpec((tile, W), lambda r, j: (r, 0))],
        out_shape=[jax.ShapeDtypeStruct((nj, h, W), CDT), jax.ShapeDtypeStruct((h, W), F32)],
        compiler_params=_cp("arbitrary", "arbitrary"),
    )(g, got)


def total_sum(own, got):
    R, W = own.shape
    tile = R // 2

    def body(own_ref, a_ref, b_ref, c_ref, o_ref):
        o_ref[...] = ((own_ref[...] + a_ref[...].astype(F32)) + b_ref[...].astype(F32)) + c_ref[...].astype(F32)

    def slab(k):
        return pl.BlockSpec((None, tile, W), lambda r: (_place()[3] ^ (k + 1), r, 0))

    return pl.pallas_call(
        body, name="total_sum", grid=(R // tile,),
        in_specs=[pl.BlockSpec((tile, W), lambda r: (r, 0)), slab(0), slab(1), slab(2)],
        out_specs=pl.BlockSpec((tile, W), lambda r: (r, 0)),
        out_shape=jax.ShapeDtypeStruct((R, W), F32),
        compiler_params=_cp("arbitrary"),
    )(own, got, got, got)


def _all_peers(x, y, c):
    return [(x ^ (r >> 2), y ^ ((r >> 1) & 1), c ^ (r & 1)) for r in range(1, N_DEV)]


ROW_ITEMS = (("mix_norm_g", D), ("b_in", IN_W), ("sinks", N_Q), ("conv_b", CONV_C), ("conv_ln_g", CONV_C),
             ("conv_ln_b", CONV_C), ("b_conv_proj", D), ("mlp_norm_g", D))
TAPS_ROW = 16
TAPS_ROWS = 32
LAYER_ROWS = TAPS_ROW + TAPS_ROWS
FINAL_ROW = DEPTH * LAYER_ROWS
SMALL_ROWS = FINAL_ROW + SUBLANES


def _row_chunks():
    out, r = {}, 0
    for n, width in ROW_ITEMS:
        out[n] = [(r + i, FLAT_W * i, min(FLAT_W, width - FLAT_W * i)) for i in range(-(-width // FLAT_W))]
        r += len(out[n])
    assert r <= TAPS_ROW
    return out


def sum_small(gsm):
    chunks = _row_chunks()
    ins = []
    for l in range(DEPTH):
        ins += [gsm[n][l] for n, _ in ROW_ITEMS] + [gsm["conv_w"][l]]
    ins.append(gsm["final_norm_g"])
    n_in = len(ins)

    def body(*refs):
        in_refs, o_ref, buf, send_sems, recv_sems = refs[:n_in], refs[n_in], refs[n_in + 1], refs[n_in + 2], refs[n_in + 3]
        x, y, c, _ = _place()
        me = 4 * x + 2 * y + c
        mine = buf.at[me]
        mine[...] = jnp.zeros((SMALL_ROWS, FLAT_W), F32)
        k = 0
        for l in range(DEPTH):
            for n, _ in ROW_ITEMS:
                for r, c0, wd in chunks[n]:
                    mine[l * LAYER_ROWS + r:l * LAYER_ROWS + r + 1, 0:wd] = in_refs[k][:, c0:c0 + wd]
                k += 1
            mine[l * LAYER_ROWS + TAPS_ROW:(l + 1) * LAYER_ROWS, 0:CONV_C] = in_refs[k][...]
            k += 1
        mine[FINAL_ROW:FINAL_ROW + 1, :] = in_refs[k][...]
        peers = _all_peers(x, y, c)
        sends = [pltpu.make_async_remote_copy(src_ref=mine, dst_ref=mine, send_sem=send_sems.at[r], recv_sem=recv_sems.at[r],
                                              device_id=to, device_id_type=MESH) for r, to in enumerate(peers)]
        for cp in sends:
            cp.start()
        for r in range(N_DEV - 1):
            pltpu.make_async_remote_copy(src_ref=mine, dst_ref=buf.at[me ^ (r + 1)], send_sem=send_sems.at[r],
                                         recv_sem=recv_sems.at[r], device_id=(x, y, c), device_id_type=MESH).wait_recv()
        for cp in sends:
            cp.wait_send()
        acc = buf[0]
        for d in range(1, N_DEV):
            acc = acc + buf[d]
        o_ref[...] = acc

    vm = pl.BlockSpec(memory_space=pltpu.VMEM)
    return pl.pallas_call(
        body, name="sum_small", out_shape=jax.ShapeDtypeStruct((SMALL_ROWS, FLAT_W), F32),
        in_specs=[vm] * n_in, out_specs=vm,
        scratch_shapes=[pltpu.VMEM((N_DEV, SMALL_ROWS, FLAT_W), F32), pltpu.SemaphoreType.DMA((N_DEV - 1,)),
                        pltpu.SemaphoreType.DMA((N_DEV - 1,))],
    )(*ins)


def gather_taps(taps):
    shard = taps.shape[2]

    def body(t_ref, o_ref, buf, send_sems, recv_sems):
        x, y, c, j = _place()
        peers = _peer_chips(x, y, j)
        buf[j] = t_ref[...]
        sends = [pltpu.make_async_remote_copy(src_ref=t_ref, dst_ref=buf.at[j], send_sem=send_sems.at[k],
                                              recv_sem=recv_sems.at[k], device_id=(*chip, c), device_id_type=MESH)
                 for k, (chip, _) in enumerate(peers)]
        for cp in sends:
            cp.start()
        for k, (_, pj) in enumerate(peers):
            pltpu.make_async_remote_copy(src_ref=t_ref, dst_ref=buf.at[pj], send_sem=send_sems.at[k],
                                         recv_sem=recv_sems.at[k], device_id=(x, y, c), device_id_type=MESH).wait_recv()
        for cp in sends:
            cp.wait_send()
        for jj in range(N_CHIPS):
            o_ref[:, :, jj * shard:(jj + 1) * shard] = buf[jj]

    vm = pl.BlockSpec(memory_space=pltpu.VMEM)
    return pl.pallas_call(
        body, name="gather_taps", out_shape=jax.ShapeDtypeStruct(taps.shape[:2] + (N_CHIPS * shard,), taps.dtype),
        in_specs=[vm], out_specs=vm,
        scratch_shapes=[pltpu.VMEM((N_CHIPS,) + taps.shape, taps.dtype), pltpu.SemaphoreType.DMA((3,)),
                        pltpu.SemaphoreType.DMA((3,))],
    )(taps)


def _adam_math(w, g, m, v):
    nm = ADAM_B1 * m + (1.0 - ADAM_B1) * g
    nv = ADAM_B2 * v + (1.0 - ADAM_B2) * jnp.square(g)
    m_hat = nm / (1.0 - ADAM_B1 ** ADAM_STEP)
    v_hat = nv / (1.0 - ADAM_B2 ** ADAM_STEP)
    return -ADAM_LR * (m_hat / (jnp.sqrt(v_hat) + ADAM_EPS) + ADAM_WD * w), nm, nv


def adamw(w, g, m, v, *, name):
    L, R, C = w.shape
    tr = min(R, 512)

    def body(w_ref, g_ref, m_ref, v_ref, d_ref, nm_ref, nv_ref):
        d_ref[...], nm_ref[...], nv_ref[...] = _adam_math(w_ref[...], g_ref[...], m_ref[...], v_ref[...])

    spec = pl.BlockSpec((None, tr, C), lambda l, i: (l, i, 0))
    out = jax.ShapeDtypeStruct((L, R, C), F32)
    return pl.pallas_call(
        body, name=name, grid=(L, R // tr), in_specs=[spec] * 4, out_specs=[spec] * 3, out_shape=[out] * 3,
        compiler_params=_cp("parallel", "parallel"),
    )(w, g, m, v)


def adamw_small(packed, w, m, v):
    chunks = _row_chunks()
    names = SMALL + ("final_norm_g",)
    as_2d = lambda a: a.reshape(1, -1) if a.ndim == 1 else a
    ins = [as_2d(t[n]) for n in names for t in (w, m, v)]
    shapes = [jax.ShapeDtypeStruct(as_2d(w[n]).shape, F32) for n in names for _ in range(4)]
    n_in = len(ins)

    def body(p_ref, *refs):
        in_refs, out_refs = refs[:n_in], refs[n_in:]
        chip = _place()[3]
        for i, n in enumerate(names):
            w_ref, m_ref, v_ref = in_refs[3 * i:3 * i + 3]
            outs = out_refs[4 * i:4 * i + 4]

            def step(at, g):
                res = (g,) + _adam_math(w_ref[at], g, m_ref[at], v_ref[at])
                for o_ref, val in zip(outs, res):
                    o_ref[at] = val

            if n == "final_norm_g":
                step((slice(None), slice(None)), p_ref[FINAL_ROW:FINAL_ROW + 1, :])
                continue
            for l in range(DEPTH):
                if n == "conv_w":
                    r0 = l * LAYER_ROWS + TAPS_ROW
                    shard = CONV_C // N_CHIPS
                    g = jnp.zeros((CONV_K, shard), F32)
                    for j in range(N_CHIPS):
                        g = jnp.where(chip == j, p_ref[r0:r0 + CONV_K, j * shard:(j + 1) * shard], g)
                    step((l,), g)
                else:
                    for r, c0, wd in chunks[n]:
                        step((slice(l, l + 1), slice(c0, c0 + wd)),
                             p_ref[l * LAYER_ROWS + r:l * LAYER_ROWS + r + 1, 0:wd])

    vm = pl.BlockSpec(memory_space=pltpu.VMEM)
    res = pl.pallas_call(
        body, name="adamw_small", out_shape=shapes,
        in_specs=[vm] + [vm] * n_in, out_specs=[vm] * len(shapes),
    )(packed, *ins)
    dicts = ({}, {}, {}, {})
    for i, n in enumerate(names):
        for d, val in zip(dicts, res[4 * i:4 * i + 4]):
            d[n] = val.reshape(w[n].shape)
    return dicts


def _full_matrix(slabs, name):
    K, N = FULL_SHAPES[name]
    if name in COL_SHARDED:
        return slabs.reshape(N_CHIPS, K, N // N_CHIPS).transpose(1, 0, 2).reshape(K, N)
    return slabs.reshape(K, N)


def _first_row(parts, name):
    r = 0
    for n, rows in parts:
        if n == name:
            return r, rows
        r += rows
    raise KeyError(name)


class _Exchange:
    CARRIERS = {
        ("conv_bwd_ln", 1): ((1, "A"), "swap"), ("attn_bwd", 1): ((1, "A"), "exchange"), ("tn_in", 1): ((1, "A"), "share"),
        ("mlp_bwd", 0): ((1, "B"), "swap"), ("tn_mlp1", 0): ((1, "B"), "exchange"), ("tn_mlp2", 0): ((1, "B"), "share"),
        ("conv_bwd_ln", 0): ((0, "A"), "swap"), ("attn_bwd", 0): ((0, "A"), "exchange"), ("tn_in", 0): ((0, "A"), "share"),
    }

    def __init__(self, w, ci, chip):
        self.w, self.ci, self.chip = w, ci, chip
        self.wsh = [jnp.concatenate([w[n][l].reshape(rows, FLAT_W) for n, rows in FLAT_PARTS], axis=0).astype(CDT)
                    for l in range(DEPTH)]
        self.final_g = w["final_norm_g"].reshape(1, D)
        self.slabs = {}
        self.full = {}
        self.units = {}
        self.reduced = {}
        self._landed_weights(0, 0, W_IN_ROWS, _run_alone(gather_rider(self.wsh[0][:W_IN_ROWS]), "gather_w_in")[0])
        self.all_taps = gather_taps(w["conv_w"])

    def _landed_weights(self, l, r0, r1, buf):
        own = self.wsh[l][r0:r1]
        self.slabs.setdefault(l, []).append((r0, lax.dynamic_update_slice(buf, own[None], (self.chip, 0, 0))))

    def _matrix(self, l, name):
        if (l, name) not in self.full:
            r, rows = _first_row(FLAT_PARTS, name)
            r0, buf = next((r0, buf) for r0, buf in self.slabs[l] if r0 <= r < r0 + buf.shape[1])
            self.full[(l, name)] = _full_matrix(buf[:, r - r0:r - r0 + rows], name)
        return self.full[(l, name)]

    def w_in(self, l):
        return self._matrix(l, "w_in")

    def mats(self, l):
        return {n: self._matrix(l, n) for n in MATRICES if n != "w_in"}

    def vec(self, n, l):
        return self.w[n][l].reshape(1, -1)

    def sinks(self, l):
        return self.w["sinks"][l]

    def taps(self, l):
        return self.all_taps[l]

    def rider(self, kernel, l):
        if (kernel, l) == ("rms_inproj", 0):
            return gather_rider(self.wsh[0][W_IN_ROWS:])
        if (kernel, l) == ("mlp_fwd", 0):
            return gather_rider(self.wsh[1])
        if (kernel, l) in self.CARRIERS:
            return self._stage(*self.CARRIERS[(kernel, l)])
        return None

    def landed(self, kernel, l, bufs):
        if (kernel, l) == ("rms_inproj", 0):
            self._landed_weights(0, W_IN_ROWS, FLAT_ROWS, bufs[0])
        elif (kernel, l) == ("mlp_fwd", 0):
            self._landed_weights(1, 0, FLAT_ROWS, bufs[0])
        else:
            self._stage_landed(*self.CARRIERS[(kernel, l)], bufs[0])

    def grads(self, l, group, g):
        if group == "A":
            flat = jnp.concatenate([g[n].reshape(N_CHIPS, rows, FLAT_W) for n, rows in GROUP_A], axis=1)
        else:
            flat = g["w_in"].reshape(D, N_CHIPS, IN_W // N_CHIPS).transpose(1, 0, 2).reshape(N_CHIPS, W_IN_ROWS, FLAT_W)
        self.units[(l, group)] = {"g": flat}

    def _stage(self, key, stage):
        u = self.units[key]
        if stage == "swap":
            return swap_rider(u["g"])
        if stage == "exchange":
            u["pb"], u["own"] = pair_sum(u["g"], u["swap"])
            return exchange_rider(u["pb"])
        u["tot"] = total_sum(u["own"], u["exchange"])
        return share_rider(u["tot"])

    def _stage_landed(self, key, stage, buf):
        u = self.units[key]
        u[stage] = buf
        if stage == "share":
            tot = u["tot"]
            self.reduced[key] = jnp.where(self.ci == 0, jnp.concatenate([tot, buf]), jnp.concatenate([buf, tot]))

    def finish(self):
        key = (0, "B")
        for stage in ("swap", "exchange", "share"):
            self._stage_landed(key, stage, _run_alone(self._stage(key, stage), stage + "_last")[0])
        out = {}
        for n in MATRICES:
            per_layer = []
            for l in range(DEPTH):
                if n == "w_in":
                    flat = self.reduced[(l, "B")]
                else:
                    r, rows = _first_row(GROUP_A, n)
                    flat = self.reduced[(l, "A")][r:r + rows]
                per_layer.append(flat.reshape(self.w[n].shape[1:]))
            out[n] = jnp.stack(per_layer)
        return out


WEIGHTS = ("mix_norm_g", "w_in", "b_in", "sinks", "conv_w", "conv_b", "conv_ln_g", "conv_ln_b", "w_attn_proj",
           "w_conv_proj", "b_conv_proj", "w_out", "mlp_norm_g", "w_mlp1", "w_mlp2", "final_norm_g")


def kernel(x, mix_norm_g, w_in, b_in, sinks, conv_w, conv_b, conv_ln_g, conv_ln_b, w_attn_proj, w_conv_proj, b_conv_proj, w_out, mlp_norm_g, w_mlp1, w_mlp2, final_norm_g, loss_target, m_mix_norm_g, m_w_in, m_b_in, m_sinks, m_conv_w, m_conv_b, m_conv_ln_g, m_conv_ln_b, m_w_attn_proj, m_w_conv_proj, m_b_conv_proj, m_w_out, m_mlp_norm_g, m_w_mlp1, m_w_mlp2, m_final_norm_g, v_mix_norm_g, v_w_in, v_b_in, v_sinks, v_conv_w, v_conv_b, v_conv_ln_g, v_conv_ln_b, v_w_attn_proj, v_w_conv_proj, v_b_conv_proj, v_w_out, v_mlp_norm_g, v_w_mlp1, v_w_mlp2, v_final_norm_g):
    w = dict(zip(WEIGHTS, (mix_norm_g, w_in, b_in, sinks, conv_w, conv_b, conv_ln_g, conv_ln_b, w_attn_proj, w_conv_proj,
                           b_conv_proj, w_out, mlp_norm_g, w_mlp1, w_mlp2, final_norm_g)))
    m = dict(zip(WEIGHTS, (m_mix_norm_g, m_w_in, m_b_in, m_sinks, m_conv_w, m_conv_b, m_conv_ln_g, m_conv_ln_b, m_w_attn_proj,
                           m_w_conv_proj, m_b_conv_proj, m_w_out, m_mlp_norm_g, m_w_mlp1, m_w_mlp2, m_final_norm_g)))
    v = dict(zip(WEIGHTS, (v_mix_norm_g, v_w_in, v_b_in, v_sinks, v_conv_w, v_conv_b, v_conv_ln_g, v_conv_ln_b, v_w_attn_proj,
                           v_w_conv_proj, v_b_conv_proj, v_w_out, v_mlp_norm_g, v_w_mlp1, v_w_mlp2, v_final_norm_g)))
    xi, yi, ci = lax.axis_index("x"), lax.axis_index("y"), lax.axis_index("c")
    chip = 2 * xi + yi

    hooks = _Exchange(w, ci, chip)
    loss, dx, gsm = forward_backward(x[0], loss_target[0], hooks)
    loss = lax.psum(loss[0, 0], ("x", "y", "c"))
    grads = hooks.finish()

    gsmall, delta, new_m, new_v = adamw_small(sum_small(gsm), w, m, v)
    grads.update(gsmall)
    for n in MATRICES:
        delta[n], new_m[n], new_v[n] = adamw(w[n], grads[n], m[n], v[n], name="adamw_" + n)

    return (loss, dx[None], *[grads[n] for n in WEIGHTS], *[delta[n] for n in WEIGHTS],
            *[new_m[n] for n in WEIGHTS], *[new_v[n] for n in WEIGHTS])
```

```python
import functools
import math
from typing import Callable, NamedTuple

import jax
import jax.numpy as jnp
import numpy as np
from jax import lax
from jax.experimental import pallas as pl
from jax.experimental.pallas import tpu as pltpu

F32 = jnp.float32
CDT = jnp.bfloat16

D = 1024
DEPTH = 2
N_Q = 8
HEAD_DIM = 64
ATTN_W = 512
KV_W = 128
BLOCK = 128
CONV_C = 512
CONV_K = 31
D_FF = 4096
IN_W = 3840
QKV_W = ATTN_W + 2 * KV_W
REST_W = IN_W - QKV_W
EPS = 1e-6
NEG = -1e30
SCALE = 1.0 / math.sqrt(HEAD_DIM)
SLOPES = [float(2.0 ** (-8.0 * (h + 1) / N_Q)) for h in range(N_Q)]
SUBLANES = 8
HALO = 32

ADAM_LR = 0.001
ADAM_B1 = 0.9
ADAM_B2 = 0.999
ADAM_EPS = 1e-08
ADAM_WD = 0.01
ADAM_STEP = 10

VMEM_LIMIT = 56 * 1024 * 1024


def _cp(*sem):
    return pltpu.CompilerParams(dimension_semantics=sem, vmem_limit_bytes=VMEM_LIMIT)


def _dot(a, b):
    return jnp.dot(a, b, preferred_element_type=F32)


def _dot_nt(a, b):
    return lax.dot_general(a, b, (((1,), (1,)), ((), ())), preferred_element_type=F32)


def _dot_tn(a, b):
    return lax.dot_general(a, b, (((0,), (0,)), ((), ())), preferred_element_type=F32)


def _sig(x):
    return 1.0 / (1.0 + jnp.exp(-x))


def _colsum(v):
    return jnp.sum(v, axis=0, keepdims=True)


def _const(shape, buffers=None):
    mode = {} if buffers is None else {"pipeline_mode": pl.Buffered(buffers)}
    return pl.BlockSpec(shape, lambda *_: (0,) * len(shape), **mode)


class Rider(NamedTuple):
    ins: tuple
    outs: tuple
    n_sems: int
    start: Callable
    finish: Callable


def _any():
    return pl.BlockSpec(memory_space=pl.ANY)


def _run(body, rider, *, name, grid, in_specs, out_specs, out_shape, args, sem, scratch_shapes=()):
    if rider is None:
        return pl.pallas_call(body, name=name, grid=grid, in_specs=list(in_specs), out_specs=list(out_specs),
                              out_shape=list(out_shape), scratch_shapes=list(scratch_shapes),
                              compiler_params=_cp(*sem))(*args)
    n_in, n_out, n_sc = len(in_specs), len(out_specs), len(scratch_shapes)
    r_in, r_out = len(rider.ins), len(rider.outs)

    def riding(*refs):
        ins, rins = refs[:n_in], refs[n_in:n_in + r_in]
        o0 = n_in + r_in
        outs, routs = refs[o0:o0 + n_out], refs[o0 + n_out:o0 + n_out + r_out]
        s0 = o0 + n_out + r_out
        scratch, sems = refs[s0:s0 + n_sc], refs[s0 + n_sc]
        first = functools.reduce(jnp.logical_and, [pl.program_id(a) == 0 for a in range(len(grid))])
        last = functools.reduce(jnp.logical_and, [pl.program_id(a) == grid[a] - 1 for a in range(len(grid))])

        @pl.when(first)
        def _():
            rider.start(rins, routs, sems)

        body(*ins, *outs, *scratch)

        @pl.when(last)
        def _():
            rider.finish(rins, routs, sems)

    res = pl.pallas_call(
        riding, name=name, grid=grid, in_specs=list(in_specs) + [_any()] * r_in,
        out_specs=list(out_specs) + [_any()] * r_out, out_shape=list(out_shape) + list(rider.outs),
        scratch_shapes=list(scratch_shapes) + [pltpu.SemaphoreType.DMA((rider.n_sems,))],
        compiler_params=_cp(*["arbitrary"] * len(grid)))(*args, *rider.ins)
    return res[:n_out], res[n_out:]


def _run_alone(rider, name):
    def body(*refs):
        r_in, r_out = len(rider.ins), len(rider.outs)
        rins, routs, sems = refs[:r_in], refs[r_in:r_in + r_out], refs[r_in + r_out]
        rider.start(rins, routs, sems)
        rider.finish(rins, routs, sems)

    return pl.pallas_call(
        body, name=name, in_specs=[_any()] * len(rider.ins), out_specs=[_any()] * len(rider.outs),
        out_shape=list(rider.outs), scratch_shapes=[pltpu.SemaphoreType.DMA((rider.n_sems,))])(*rider.ins)


def rms_inproj(x, g, wt, b, *, tm=512, rider=None):
    T = x.shape[0]

    def body(x_ref, g_ref, w_ref, b_ref, qkv_ref, rest_ref):
        xv = x_ref[...]
        r = lax.rsqrt(jnp.mean(xv * xv, axis=-1, keepdims=True) + EPS)
        h = (xv * r * g_ref[...]).astype(CDT)
        qkv_ref[...] = (_dot_nt(h, w_ref[0:QKV_W, :]) + b_ref[:, 0:QKV_W]).astype(qkv_ref.dtype)
        for j in range(REST_W // D):
            c0 = QKV_W + D * j
            rest_ref[:, D * j:D * (j + 1)] = _dot_nt(h, w_ref[c0:c0 + D, :]) + b_ref[:, c0:c0 + D]

    return _run(
        body, rider, name="rms_inproj", grid=(T // tm,),
        in_specs=[pl.BlockSpec((tm, D), lambda i: (i, 0)), _const((1, D)), _const((IN_W, D), 1), _const((1, IN_W))],
        out_specs=[pl.BlockSpec((tm, QKV_W), lambda i: (i, 0)), pl.BlockSpec((tm, REST_W), lambda i: (i, 0))],
        out_shape=[jax.ShapeDtypeStruct((T, QKV_W), CDT), jax.ShapeDtypeStruct((T, REST_W), F32)],
        sem=("parallel",), args=(x, g, wt, b))


def _lane_halves(shape):
    lane = lax.broadcasted_iota(jnp.int32, shape, 1)
    return lane < HEAD_DIM, lane >= HEAD_DIM


def _swap_halves(v):
    return pltpu.roll(v.astype(F32), HEAD_DIM, axis=1).astype(v.dtype)


N_KV = KV_W // HEAD_DIM
GROUP = N_Q // N_KV
STACK = GROUP * BLOCK


def _attn_masks(first):
    row = lax.broadcasted_iota(jnp.int32, (STACK, 2 * BLOCK), 0) & (BLOCK - 1)
    col = lax.broadcasted_iota(jnp.int32, (STACK, 2 * BLOCK), 1)
    dist = row + BLOCK - col
    valid = (dist >= 0) & (dist < BLOCK) & ((col >= BLOCK) | jnp.logical_not(first))
    return valid, dist.astype(F32)


def _per_head_column(vals):
    row = lax.broadcasted_iota(jnp.int32, (STACK, 1), 0)
    col = jnp.full((STACK, 1), vals[GROUP - 1], F32)
    for i in reversed(range(GROUP - 1)):
        col = jnp.where(row < (i + 1) * BLOCK, vals[i], col)
    return col


def _stack_heads(dst, src_ref, r0, g):
    lane = lax.broadcasted_iota(jnp.int32, (BLOCK, 2 * HEAD_DIM), 1)
    keep = (lane >= HEAD_DIM) if g else (lane < HEAD_DIM)
    for i in range(GROUP):
        h = GROUP * g + i
        tile = src_ref[pl.ds(r0, BLOCK), (h // 2) * 128:(h // 2 + 1) * 128]
        if h % 2 != g:
            tile = _swap_halves(tile)
        dst[i * BLOCK:(i + 1) * BLOCK, :] = jnp.where(keep, tile, jnp.zeros_like(tile))


def _unstack_heads(dst_ref, stacked, r0, g):
    lane = lax.broadcasted_iota(jnp.int32, (BLOCK, 2 * HEAD_DIM), 1)
    for j in range(GROUP // 2):
        even = stacked[(2 * j) * BLOCK:(2 * j + 1) * BLOCK, :]
        odd = stacked[(2 * j + 1) * BLOCK:(2 * j + 2) * BLOCK, :]
        lo = _swap_halves(even) if g else even
        hi = odd if g else _swap_halves(odd)
        pair = (GROUP * g) // 2 + j
        dst_ref[pl.ds(r0, BLOCK), pair * 128:(pair + 1) * 128] = jnp.where(lane < HEAD_DIM, lo, hi).astype(dst_ref.dtype)


def _qkv_specs(tq):
    nb = tq // BLOCK
    return [
        pl.BlockSpec((tq, ATTN_W), lambda i: (i, 0)),
        pl.BlockSpec((BLOCK, KV_W), lambda i: (jnp.maximum(i * nb - 1, 0), ATTN_W // KV_W)),
        pl.BlockSpec((tq, KV_W), lambda i: (i, ATTN_W // KV_W)),
        pl.BlockSpec((BLOCK, KV_W), lambda i: (jnp.maximum(i * nb - 1, 0), ATTN_W // KV_W + 1)),
        pl.BlockSpec((tq, KV_W), lambda i: (i, ATTN_W // KV_W + 1)),
    ]


def attn_fwd(qkv, sinks, *, tq=512):
    T = qkv.shape[0]
    nb = tq // BLOCK

    def body(sink_ref, q_ref, kp_ref, kc_ref, vp_ref, vc_ref, o_ref, lse_ref, kext, vext, qs):
        i = pl.program_id(0)
        kext[0:BLOCK, :] = kp_ref[...]
        kext[BLOCK:, :] = kc_ref[...]
        vext[0:BLOCK, :] = vp_ref[...]
        vext[BLOCK:, :] = vc_ref[...]
        lane_l = lax.broadcasted_iota(jnp.int32, (BLOCK, 128), 1)

        def blk(b, carry):
            r0 = pl.multiple_of(b * BLOCK, BLOCK)
            valid, distf = _attn_masks(jnp.logical_and(i == 0, b == 0))
            kc = kext[pl.ds(r0, 2 * BLOCK), :]
            vc = vext[pl.ds(r0, 2 * BLOCK), :]
            lse_t = jnp.zeros((BLOCK, 128), F32)
            for g in range(N_KV):
                heads = range(GROUP * g, GROUP * (g + 1))
                _stack_heads(qs, q_ref, r0, g)
                s = _dot_nt(qs[...], kc) * SCALE - _per_head_column([SLOPES[h] for h in heads]) * distf
                s = jnp.where(valid, s, NEG)
                sink = _per_head_column([sink_ref[h] for h in heads])
                m = jnp.maximum(jnp.max(s, axis=-1, keepdims=True), sink)
                p = jnp.exp(s - m)
                denom = jnp.sum(p, axis=-1, keepdims=True) + jnp.exp(sink - m)
                p = p / denom
                _unstack_heads(o_ref, _dot(p.astype(CDT), vc), r0, g)
                lse = m + jnp.log(denom)
                for i_h, h in enumerate(heads):
                    lse_t = jnp.where(lane_l == h, lse[i_h * BLOCK:(i_h + 1) * BLOCK, :], lse_t)
            lse_ref[pl.ds(r0, BLOCK), :] = lse_t
            return carry

        lax.fori_loop(0, nb, blk, 0)

    return pl.pallas_call(
        body, name="attn_fwd", grid=(T // tq,),
        in_specs=[pl.BlockSpec(memory_space=pltpu.SMEM)] + _qkv_specs(tq),
        out_specs=[pl.BlockSpec((tq, ATTN_W), lambda i: (i, 0)), pl.BlockSpec((tq, 128), lambda i: (i, 0))],
        out_shape=[jax.ShapeDtypeStruct((T, ATTN_W), CDT), jax.ShapeDtypeStruct((T, 128), F32)],
        scratch_shapes=[pltpu.VMEM((tq + BLOCK, KV_W), CDT), pltpu.VMEM((tq + BLOCK, KV_W), CDT),
                        pltpu.VMEM((STACK, 2 * HEAD_DIM), CDT)],
        compiler_params=_cp("parallel"),
    )(sinks, qkv, qkv, qkv, qkv, qkv)


def _halo_before(tm, width, col):
    return pl.BlockSpec((HALO, width), lambda i: (jnp.maximum(i * (tm // HALO) - 1, 0), col))


def _fill_u0(ext, a_ref, b_ref, ha_ref, hb_ref, first):
    hu = ha_ref[...] * _sig(hb_ref[...])
    ext[0:HALO, :] = jnp.where(first, jnp.zeros_like(hu), hu)
    ext[HALO:, :] = a_ref[...] * _sig(b_ref[...])


def _shifted_taps(src, w_ref, base, rc, offsets):
    acc = jnp.zeros((rc, CONV_C), F32)
    for b in range(SUBLANES):
        taps = [(k, o - b) for k, o in enumerate(offsets) if o % SUBLANES == b]
        if not taps:
            continue
        rows = rc if b == 0 else rc + SUBLANES
        part = jnp.zeros((rows, CONV_C), F32)
        for k, o8 in taps:
            part = part + w_ref[k:k + 1, :] * src[base + o8:base + o8 + rows, :]
        acc = acc + (part if b == 0 else part[b:b + rc, :])
    return acc


def _conv_rows(ext, w_ref, r0, rc):
    return _shifted_taps(ext, w_ref, r0, rc, [HALO - (CONV_K - 1) + k for k in range(CONV_K)])


def _layer_norm(u1, g, b):
    mu = jnp.mean(u1, axis=-1, keepdims=True)
    xc = u1 - mu
    rstd = lax.rsqrt(jnp.mean(xc * xc, axis=-1, keepdims=True) + EPS)
    n = xc * rstd
    return n, rstd, n * g + b


CONV_RC = 32


def conv_fwd(rest, cw, cb, lg, lb, *, tm=512):
    T = rest.shape[0]

    def body(a_ref, b_ref, ha_ref, hb_ref, w_ref, cb_ref, lg_ref, lb_ref, o_ref, u1_ref, ext):
        _fill_u0(ext, a_ref, b_ref, ha_ref, hb_ref, pl.program_id(0) == 0)
        for c in range(tm // CONV_RC):
            r0 = c * CONV_RC
            u1 = _conv_rows(ext, w_ref, r0, CONV_RC) + cb_ref[...]
            u1_ref[r0:r0 + CONV_RC, :] = u1
            _, _, u2 = _layer_norm(u1, lg_ref[...], lb_ref[...])
            o_ref[r0:r0 + CONV_RC, :] = (u2 * _sig(u2)).astype(o_ref.dtype)

    row = lambda i: (i, 0)
    return pl.pallas_call(
        body, name="conv_fwd", grid=(T // tm,),
        in_specs=[pl.BlockSpec((tm, CONV_C), row), pl.BlockSpec((tm, CONV_C), lambda i: (i, 1)),
                  _halo_before(tm, CONV_C, 0), _halo_before(tm, CONV_C, 1),
                  _const((CONV_K, CONV_C)), _const((1, CONV_C)), _const((1, CONV_C)), _const((1, CONV_C))],
        out_specs=[pl.BlockSpec((tm, CONV_C), row), pl.BlockSpec((tm, CONV_C), row)],
        out_shape=[jax.ShapeDtypeStruct((T, CONV_C), CDT), jax.ShapeDtypeStruct((T, CONV_C), F32)],
        scratch_shapes=[pltpu.VMEM((tm + HALO, CONV_C), F32)],
        compiler_params=_cp("parallel"),
    )(rest, rest, rest, rest, cw, cb, lg, lb)


def merge_out(x, attn, u3, rest, wa, wc, bc, wo, *, tm=512):
    T = x.shape[0]

    def body(x_ref, at_ref, u_ref, ga_ref, gc_ref, wa_ref, wc_ref, bc_ref, wo_ref, o_ref):
        br_a = _dot(at_ref[...], wa_ref[...])
        br_c = _dot(u_ref[...], wc_ref[...]) + bc_ref[...]
        merged = _sig(ga_ref[...]) * br_a + _sig(gc_ref[...]) * br_c
        o_ref[...] = x_ref[...] + _dot(merged.astype(CDT), wo_ref[...])

    return pl.pallas_call(
        body, name="merge_out", grid=(T // tm,),
        in_specs=[pl.BlockSpec((tm, D), lambda i: (i, 0)), pl.BlockSpec((tm, ATTN_W), lambda i: (i, 0)),
                  pl.BlockSpec((tm, CONV_C), lambda i: (i, 0)),
                  pl.BlockSpec((tm, D), lambda i: (i, 1)), pl.BlockSpec((tm, D), lambda i: (i, 2)),
                  _const((ATTN_W, D), 1), _const((CONV_C, D), 1), _const((1, D)), _const((D, D), 1)],
        out_specs=pl.BlockSpec((tm, D), lambda i: (i, 0)),
        out_shape=jax.ShapeDtypeStruct((T, D), F32),
        compiler_params=_cp("parallel"),
    )(x, attn, u3, rest, rest, wa, wc, bc, wo)


def mlp_fwd(x, g, w1, w2, *, tm=256, tf=D_FF, rider=None):
    T = x.shape[0]
    nf = D_FF // tf

    def body(x_ref, g_ref, w1_ref, w2_ref, o_ref, pre_ref, h_s, acc_s):
        f = pl.program_id(1)

        @pl.when(f == 0)
        def _():
            xv = x_ref[...]
            r = lax.rsqrt(jnp.mean(xv * xv, axis=-1, keepdims=True) + EPS)
            h_s[...] = (xv * r * g_ref[...]).astype(CDT)
            acc_s[...] = jnp.zeros_like(acc_s)

        pre = _dot(h_s[...], w1_ref[...])
        pre_ref[...] = pre
        a = jnp.square(jnp.maximum(pre, 0.0))
        acc_s[...] += _dot(a.astype(CDT), w2_ref[...])

        @pl.when(f == nf - 1)
        def _():
            o_ref[...] = x_ref[...] + acc_s[...]

    mode = {"pipeline_mode": pl.Buffered(1)} if nf == 1 else {}
    return _run(
        body, rider, name="mlp_fwd", grid=(T // tm, nf),
        in_specs=[pl.BlockSpec((tm, D), lambda i, f: (i, 0)), _const((1, D)),
                  pl.BlockSpec((D, tf), lambda i, f: (0, f), **mode), pl.BlockSpec((tf, D), lambda i, f: (f, 0), **mode)],
        out_specs=[pl.BlockSpec((tm, D), lambda i, f: (i, 0)), pl.BlockSpec((tm, tf), lambda i, f: (i, f))],
        out_shape=[jax.ShapeDtypeStruct((T, D), F32), jax.ShapeDtypeStruct((T, D_FF), F32)],
        scratch_shapes=[pltpu.VMEM((tm, D), CDT), pltpu.VMEM((tm, D), F32)],
        sem=("parallel", "arbitrary"), args=(x, g, w1, w2))


def _rms_bwd(xv, g, dh):
    r = lax.rsqrt(jnp.mean(xv * xv, axis=-1, keepdims=True) + EPS)
    xhat = xv * r
    dxh = dh * g
    dx = r * (dxh - xhat * jnp.mean(dxh * xhat, axis=-1, keepdims=True))
    return dx, dh * xhat


def loss_head(x, g, tgt, *, tm=512):
    T = x.shape[0]

    def body(x_ref, g_ref, t_ref, dx_ref, dg_ref, loss_ref):
        @pl.when(pl.program_id(0) == 0)
        def _():
            dg_ref[...] = jnp.zeros_like(dg_ref)
            loss_ref[...] = jnp.zeros_like(loss_ref)

        xv = x_ref[...]
        gv = g_ref[...]
        r = lax.rsqrt(jnp.mean(xv * xv, axis=-1, keepdims=True) + EPS)
        e = xv * r * gv - t_ref[...]
        loss_ref[...] += 0.5 * jnp.sum(jnp.mean(e * e, axis=-1, keepdims=True), axis=0, keepdims=True)
        dx, dg_rows = _rms_bwd(xv, gv, e * (1.0 / D))
        dx_ref[...] = dx
        dg_ref[...] += _colsum(dg_rows)

    return pl.pallas_call(
        body, name="loss_head", grid=(T // tm,),
        in_specs=[pl.BlockSpec((tm, D), lambda i: (i, 0)), _const((1, D)), pl.BlockSpec((tm, D), lambda i: (i, 0))],
        out_specs=[pl.BlockSpec((tm, D), lambda i: (i, 0)), _const((1, D)), _const((1, 128))],
        out_shape=[jax.ShapeDtypeStruct((T, D), F32), jax.ShapeDtypeStruct((1, D), F32),
                   jax.ShapeDtypeStruct((1, 128), F32)],
        compiler_params=_cp("arbitrary"),
    )(x, g, tgt)


def mlp_bwd(dy, x, g, pre, w1, w2, *, tm=256, tf=D_FF, rider=None):
    T = x.shape[0]
    nf = D_FF // tf

    def body(dy_ref, x_ref, g_ref, pre_ref, w1_ref, w2_ref, dx_ref, dg_ref, h_ref, a_ref, dpre_ref, dyb_s, acc_s):
        i, f = pl.program_id(0), pl.program_id(1)

        @pl.when(jnp.logical_and(i == 0, f == 0))
        def _():
            dg_ref[...] = jnp.zeros_like(dg_ref)

        @pl.when(f == 0)
        def _():
            dyb_s[...] = dy_ref[...].astype(CDT)
            acc_s[...] = jnp.zeros_like(acc_s)

        pre = pre_ref[...]
        rl = jnp.maximum(pre, 0.0)
        a_ref[...] = (rl * rl).astype(CDT)
        da = _dot_nt(dyb_s[...], w2_ref[...])
        dpre = (da * (2.0 * rl)).astype(CDT)
        dpre_ref[...] = dpre
        acc_s[...] += _dot_nt(dpre, w1_ref[...])

        @pl.when(f == nf - 1)
        def _():
            xv = x_ref[...]
            gv = g_ref[...]
            dxn, dg_rows = _rms_bwd(xv, gv, acc_s[...])
            dx_ref[...] = dy_ref[...] + dxn
            dg_ref[...] += _colsum(dg_rows)
            r = lax.rsqrt(jnp.mean(xv * xv, axis=-1, keepdims=True) + EPS)
            h_ref[...] = (xv * r * gv).astype(CDT)

    row = lambda i, f: (i, 0)
    mode = {"pipeline_mode": pl.Buffered(1)} if nf == 1 else {}
    return _run(
        body, rider, name="mlp_bwd", grid=(T // tm, nf),
        in_specs=[pl.BlockSpec((tm, D), row), pl.BlockSpec((tm, D), row), _const((1, D)),
                  pl.BlockSpec((tm, tf), lambda i, f: (i, f)),
                  pl.BlockSpec((D, tf), lambda i, f: (0, f), **mode), pl.BlockSpec((tf, D), lambda i, f: (f, 0), **mode)],
        out_specs=[pl.BlockSpec((tm, D), row), _const((1, D)), pl.BlockSpec((tm, D), row),
                   pl.BlockSpec((tm, tf), lambda i, f: (i, f)), pl.BlockSpec((tm, tf), lambda i, f: (i, f))],
        out_shape=[jax.ShapeDtypeStruct((T, D), F32), jax.ShapeDtypeStruct((1, D), F32),
                   jax.ShapeDtypeStruct((T, D), CDT), jax.ShapeDtypeStruct((T, D_FF), CDT),
                   jax.ShapeDtypeStruct((T, D_FF), CDT)],
        scratch_shapes=[pltpu.VMEM((tm, D), CDT), pltpu.VMEM((tm, D), F32)],
        sem=("arbitrary", "arbitrary"), args=(dy, x, g, pre, w1, w2))


def tn_matmul(a, b, *, tm, tn, tk=2048, name, by_chip=False, rider=None):
    T, M = a.shape
    N = b.shape[1]
    tk = min(tk, T)
    nk = T // tk

    def body(a_ref, b_ref, o_ref):
        @pl.when(pl.program_id(2) == 0)
        def _():
            o_ref[...] = jnp.zeros_like(o_ref)

        o_ref[...] += _dot_tn(a_ref[...].astype(CDT), b_ref[...].astype(CDT))

    if by_chip:
        out_spec = pl.BlockSpec((None, tm, tn), lambda i, j, k: (j, i, 0))
        out_shape = jax.ShapeDtypeStruct((N // tn, M, tn), F32)
    else:
        out_spec = pl.BlockSpec((tm, tn), lambda i, j, k: (i, j))
        out_shape = jax.ShapeDtypeStruct((M, N), F32)
    res = _run(
        body, rider, name=name, grid=(M // tm, N // tn, nk),
        in_specs=[pl.BlockSpec((tk, tm), lambda i, j, k: (k, i)), pl.BlockSpec((tk, tn), lambda i, j, k: (k, j))],
        out_specs=[out_spec], out_shape=[out_shape], sem=("parallel", "parallel", "arbitrary"), args=(a, b))
    return res[0] if rider is None else (res[0][0], res[1])


def merge_bwd(dx1, attn, u3, rest, wa, wc, bc, wo, *, tm=512):
    T = dx1.shape[0]

    def body(dx_ref, at_ref, u_ref, ga_ref, gc_ref, wa_ref, wc_ref, bc_ref, wo_ref,
             mg_ref, dba_ref, dbc_ref, dat_ref, du_ref, dgate_ref, dbias_ref):
        @pl.when(pl.program_id(0) == 0)
        def _():
            dbias_ref[...] = jnp.zeros_like(dbias_ref)

        br_a = _dot(at_ref[...], wa_ref[...])
        br_c = _dot(u_ref[...], wc_ref[...]) + bc_ref[...]
        sa = _sig(ga_ref[...])
        sc = _sig(gc_ref[...])
        mg_ref[...] = (sa * br_a + sc * br_c).astype(CDT)
        dm = _dot_nt(dx_ref[...].astype(CDT), wo_ref[...])
        dba = dm * sa
        dbc = dm * sc
        dgate_ref[:, 0:D] = dm * br_a * sa * (1.0 - sa)
        dgate_ref[:, D:2 * D] = dm * br_c * sc * (1.0 - sc)
        dbias_ref[...] += _colsum(dbc)
        dba_b = dba.astype(CDT)
        dbc_b = dbc.astype(CDT)
        dba_ref[...] = dba_b
        dbc_ref[...] = dbc_b
        dat_ref[...] = _dot_nt(dba_b, wa_ref[...]).astype(CDT)
        du_ref[...] = _dot_nt(dbc_b, wc_ref[...])

    row = lambda i: (i, 0)
    return pl.pallas_call(
        body, name="merge_bwd", grid=(T // tm,),
        in_specs=[pl.BlockSpec((tm, D), row), pl.BlockSpec((tm, ATTN_W), row), pl.BlockSpec((tm, CONV_C), row),
                  pl.BlockSpec((tm, D), lambda i: (i, 1)), pl.BlockSpec((tm, D), lambda i: (i, 2)),
                  _const((ATTN_W, D), 1), _const((CONV_C, D), 1), _const((1, D)), _const((D, D), 1)],
        out_specs=[pl.BlockSpec((tm, D), row), pl.BlockSpec((tm, D), row), pl.BlockSpec((tm, D), row),
                   pl.BlockSpec((tm, ATTN_W), row), pl.BlockSpec((tm, CONV_C), row),
                   pl.BlockSpec((tm, 2 * D), row), _const((1, D))],
        out_shape=[jax.ShapeDtypeStruct((T, D), CDT), jax.ShapeDtypeStruct((T, D), CDT),
                   jax.ShapeDtypeStruct((T, D), CDT), jax.ShapeDtypeStruct((T, ATTN_W), CDT),
                   jax.ShapeDtypeStruct((T, CONV_C), F32), jax.ShapeDtypeStruct((T, 2 * D), F32),
                   jax.ShapeDtypeStruct((1, D), F32)],
        compiler_params=_cp("arbitrary"),
    )(dx1, attn, u3, rest, rest, wa, wc, bc, wo)


def conv_bwd_ln(du3, u1, lg, lb, *, tm=512, rider=None):
    T = du3.shape[0]

    def body(du_ref, u1_ref, lg_ref, lb_ref, du1_ref, dlg_ref, dlb_ref, dcb_ref):
        @pl.when(pl.program_id(0) == 0)
        def _():
            dlg_ref[...] = jnp.zeros_like(dlg_ref)
            dlb_ref[...] = jnp.zeros_like(dlb_ref)
            dcb_ref[...] = jnp.zeros_like(dcb_ref)

        dlg = jnp.zeros((1, CONV_C), F32)
        dlb = jnp.zeros((1, CONV_C), F32)
        dcb = jnp.zeros((1, CONV_C), F32)
        for c in range(tm // CONV_RC):
            r0 = c * CONV_RC
            n, rstd, u2 = _layer_norm(u1_ref[r0:r0 + CONV_RC, :], lg_ref[...], lb_ref[...])
            s = _sig(u2)
            du2 = du_ref[r0:r0 + CONV_RC, :] * (s + u2 * s * (1.0 - s))
            dn = du2 * lg_ref[...]
            du1 = rstd * (dn - jnp.mean(dn, axis=-1, keepdims=True) - n * jnp.mean(dn * n, axis=-1, keepdims=True))
            du1_ref[r0:r0 + CONV_RC, :] = du1
            dlg = dlg + _colsum(du2 * n)
            dlb = dlb + _colsum(du2)
            dcb = dcb + _colsum(du1)
        dlg_ref[...] += dlg
        dlb_ref[...] += dlb
        dcb_ref[...] += dcb

    row = lambda i: (i, 0)
    vec = jax.ShapeDtypeStruct((1, CONV_C), F32)
    return _run(
        body, rider, name="conv_bwd_ln", grid=(T // tm,),
        in_specs=[pl.BlockSpec((tm, CONV_C), row), pl.BlockSpec((tm, CONV_C), row), _const((1, CONV_C)), _const((1, CONV_C))],
        out_specs=[pl.BlockSpec((tm, CONV_C), row), _const((1, CONV_C)), _const((1, CONV_C)), _const((1, CONV_C))],
        out_shape=[jax.ShapeDtypeStruct((T, CONV_C), F32), vec, vec, vec],
        sem=("arbitrary",), args=(du3, u1, lg, lb))


def conv_bwd_taps(du1, rest, cw, *, tm=512):
    T = du1.shape[0]
    nt = T // tm

    def body(d_ref, hd_ref, a_ref, b_ref, ha_ref, hb_ref, w_ref, dglu_ref, dw_ref, ext, dext, dwacc):
        i = pl.program_id(0)

        @pl.when(i == 0)
        def _():
            dwacc[...] = jnp.zeros_like(dwacc)

        _fill_u0(ext, a_ref, b_ref, ha_ref, hb_ref, i == 0)
        dext[0:SUBLANES, :] = jnp.zeros((SUBLANES, CONV_C), F32)
        dext[SUBLANES:SUBLANES + tm, :] = d_ref[...]
        hd = hd_ref[...]
        dext[SUBLANES + tm:, :] = jnp.where(i == nt - 1, jnp.zeros_like(hd), hd)
        qrow = lax.broadcasted_iota(jnp.int32, (CONV_RC + SUBLANES, CONV_C), 0)
        for c in range(tm // CONV_RC):
            r0 = c * CONV_RC
            du0 = _shifted_taps(dext, w_ref, r0 + SUBLANES, CONV_RC, [CONV_K - 1 - k for k in range(CONV_K)])
            for b in range(SUBLANES):
                taps = [(k, HALO - (CONV_K - 1) + k - b) for k in range(CONV_K) if (HALO - (CONV_K - 1) + k) % SUBLANES == b]
                if b == 0:
                    rows = CONV_RC
                    dsh = dext[r0 + SUBLANES:r0 + SUBLANES + rows, :]
                else:
                    rows = CONV_RC + SUBLANES
                    dsh = dext[r0 + SUBLANES - b:r0 + SUBLANES - b + rows, :]
                    dsh = jnp.where((qrow >= b) & (qrow < CONV_RC + b), dsh, 0.0)
                for k, o8 in taps:
                    prod = dsh * ext[r0 + o8:r0 + o8 + rows, :]
                    dwacc[8 * k:8 * k + 8, :] += jnp.sum(prod.reshape(rows // SUBLANES, SUBLANES, CONV_C), axis=0)
            av = a_ref[r0:r0 + CONV_RC, :]
            sb = _sig(b_ref[r0:r0 + CONV_RC, :])
            dglu_ref[r0:r0 + CONV_RC, 0:CONV_C] = du0 * sb
            dglu_ref[r0:r0 + CONV_RC, CONV_C:2 * CONV_C] = du0 * av * sb * (1.0 - sb)

        @pl.when(i == nt - 1)
        def _():
            dw_ref[...] = jnp.zeros_like(dw_ref)
            for k in range(CONV_K):
                dw_ref[k:k + 1, :] = _colsum(dwacc[8 * k:8 * k + 8, :])

    row = lambda i: (i, 0)
    return pl.pallas_call(
        body, name="conv_bwd_taps", grid=(nt,),
        in_specs=[pl.BlockSpec((tm, CONV_C), row),
                  pl.BlockSpec((HALO, CONV_C), lambda i: (jnp.minimum((i + 1) * (tm // HALO), T // HALO - 1), 0)),
                  pl.BlockSpec((tm, CONV_C), row), pl.BlockSpec((tm, CONV_C), lambda i: (i, 1)),
                  _halo_before(tm, CONV_C, 0), _halo_before(tm, CONV_C, 1), _const((CONV_K, CONV_C))],
        out_specs=[pl.BlockSpec((tm, 2 * CONV_C), row), _const((HALO, CONV_C))],
        out_shape=[jax.ShapeDtypeStruct((T, 2 * CONV_C), F32), jax.ShapeDtypeStruct((HALO, CONV_C), F32)],
        scratch_shapes=[pltpu.VMEM((tm + HALO, CONV_C), F32), pltpu.VMEM((SUBLANES + tm + HALO, CONV_C), F32),
                        pltpu.VMEM((8 * CONV_K, CONV_C), F32)],
        compiler_params=_cp("arbitrary"),
    )(du1, du1, rest, rest, rest, rest, cw)


def attn_bwd(qkv, do, lse, sinks, *, tq=512, rider=None):
    T = qkv.shape[0]
    nb = tq // BLOCK

    def body(sink_ref, q_ref, kp_ref, kc_ref, vp_ref, vc_ref, do_ref, lse_ref,
             dq_ref, dkv_ref, spill_ref, dsink_ref, kext, vext, dkext, dvext, qs, dos):
        i = pl.program_id(0)

        @pl.when(i == 0)
        def _():
            dsink_ref[...] = jnp.zeros_like(dsink_ref)

        kext[0:BLOCK, :] = kp_ref[...]
        kext[BLOCK:, :] = kc_ref[...]
        vext[0:BLOCK, :] = vp_ref[...]
        vext[BLOCK:, :] = vc_ref[...]
        dkext[...] = jnp.zeros_like(dkext)
        dvext[...] = jnp.zeros_like(dvext)
        lane_l = lax.broadcasted_iota(jnp.int32, (BLOCK, 128), 1)
        lane_k = lax.broadcasted_iota(jnp.int32, (2 * BLOCK, KV_W), 1)

        def blk(b, dsink):
            r0 = pl.multiple_of(b * BLOCK, BLOCK)
            valid, distf = _attn_masks(jnp.logical_and(i == 0, b == 0))
            kc = kext[pl.ds(r0, 2 * BLOCK), :]
            vc = vext[pl.ds(r0, 2 * BLOCK), :]
            lse_t = lse_ref[pl.ds(r0, BLOCK), :]
            dk = jnp.zeros((2 * BLOCK, KV_W), F32)
            dv = jnp.zeros((2 * BLOCK, KV_W), F32)
            for g in range(N_KV):
                heads = range(GROUP * g, GROUP * (g + 1))
                _stack_heads(qs, q_ref, r0, g)
                _stack_heads(dos, do_ref, r0, g)
                qv = qs[...]
                dov = dos[...]
                s = _dot_nt(qv, kc) * SCALE - _per_head_column([SLOPES[h] for h in heads]) * distf
                s = jnp.where(valid, s, NEG)
                lse = jnp.concatenate(
                    [jnp.sum(jnp.where(lane_l == h, lse_t, 0.0), axis=-1, keepdims=True) for h in heads], axis=0)
                p = jnp.exp(s - lse)
                dp = _dot_nt(dov, vc)
                dd = jnp.sum(p * dp, axis=-1, keepdims=True)
                ds = (p * (dp - dd)).astype(CDT)
                keep = (lane_k >= HEAD_DIM) if g else (lane_k < HEAD_DIM)
                _unstack_heads(dq_ref, _dot(ds, jnp.where(keep, kc, jnp.zeros_like(kc))) * SCALE, r0, g)
                dk = dk + _dot_tn(ds, qv)
                dv = dv + _dot_tn(p.astype(CDT), dov)
                wsink = jnp.exp(_per_head_column([sink_ref[h] for h in heads]) - lse) * dd
                for i_h, h in enumerate(heads):
                    part = jnp.sum(wsink[i_h * BLOCK:(i_h + 1) * BLOCK, :], axis=0, keepdims=True)
                    dsink = dsink - jnp.where(lane_l[0:1, :] == h, part, 0.0)
            dkext[pl.ds(r0, 2 * BLOCK), :] += dk * SCALE
            dvext[pl.ds(r0, 2 * BLOCK), :] += dv
            return dsink

        dsink_ref[...] += lax.fori_loop(0, nb, blk, jnp.zeros((1, 128), F32))
        dkv_ref[:, 0:KV_W] = dkext[BLOCK:, :]
        dkv_ref[:, KV_W:2 * KV_W] = dvext[BLOCK:, :]
        spill_ref[:, 0:KV_W] = dkext[0:BLOCK, :]
        spill_ref[:, KV_W:2 * KV_W] = dvext[0:BLOCK, :]

    row = lambda i: (i, 0)
    return _run(
        body, rider, name="attn_bwd", grid=(T // tq,),
        in_specs=[pl.BlockSpec(memory_space=pltpu.SMEM)] + _qkv_specs(tq)
        + [pl.BlockSpec((tq, ATTN_W), row), pl.BlockSpec((tq, 128), row)],
        out_specs=[pl.BlockSpec((tq, ATTN_W), row), pl.BlockSpec((tq, 2 * KV_W), row),
                   pl.BlockSpec((BLOCK, 2 * KV_W), row), _const((1, 128))],
        out_shape=[jax.ShapeDtypeStruct((T, ATTN_W), F32), jax.ShapeDtypeStruct((T, 2 * KV_W), F32),
                   jax.ShapeDtypeStruct((T // tq * BLOCK, 2 * KV_W), F32), jax.ShapeDtypeStruct((1, 128), F32)],
        scratch_shapes=[pltpu.VMEM((tq + BLOCK, KV_W), CDT), pltpu.VMEM((tq + BLOCK, KV_W), CDT),
                        pltpu.VMEM((tq + BLOCK, KV_W), F32), pltpu.VMEM((tq + BLOCK, KV_W), F32),
                        pltpu.VMEM((STACK, 2 * HEAD_DIM), CDT), pltpu.VMEM((STACK, 2 * HEAD_DIM), CDT)],
        sem=("arbitrary",), args=(sinks, qkv, qkv, qkv, qkv, qkv, do, lse))


def inproj_bwd(dres, x, g, w, dq, dkv, spill, dglu, dgate, *, tm=512):
    T = x.shape[0]
    nt = T // tm

    def body(dr_ref, x_ref, g_ref, w_ref, dq_ref, dkv_ref, sp_ref, dglu_ref, dgate_ref,
             dx_ref, dp_ref, h_ref, dg_ref, db_ref):
        i = pl.program_id(0)

        @pl.when(i == 0)
        def _():
            dg_ref[...] = jnp.zeros_like(dg_ref)
            db_ref[...] = jnp.zeros_like(db_ref)

        sp = sp_ref[...]
        sp = jnp.where(i == nt - 1, jnp.zeros_like(sp), sp)
        pieces = ((0, ATTN_W, dq_ref), (QKV_W, 2 * CONV_C, dglu_ref), (QKV_W + 2 * CONV_C, 2 * D, dgate_ref))
        for c0, wd, ref in pieces:
            v = ref[...]
            db_ref[:, c0:c0 + wd] += _colsum(v)
            dp_ref[:, c0:c0 + wd] = v.astype(CDT)
        dkv = dkv_ref[...]
        db_ref[:, ATTN_W:QKV_W] += _colsum(dkv) + _colsum(sp)
        dp_ref[0:tm - BLOCK, ATTN_W:QKV_W] = dkv[0:tm - BLOCK, :].astype(CDT)
        dp_ref[tm - BLOCK:tm, ATTN_W:QKV_W] = (dkv[tm - BLOCK:tm, :] + sp).astype(CDT)
        dh = _dot(dp_ref[...], w_ref[...])
        xv = x_ref[...]
        gv = g_ref[...]
        dxn, dg_rows = _rms_bwd(xv, gv, dh)
        dx_ref[...] = dr_ref[...] + dxn
        dg_ref[...] += _colsum(dg_rows)
        r = lax.rsqrt(jnp.mean(xv * xv, axis=-1, keepdims=True) + EPS)
        h_ref[...] = (xv * r * gv).astype(CDT)

    row = lambda i: (i, 0)
    return pl.pallas_call(
        body, name="inproj_bwd", grid=(nt,),
        in_specs=[pl.BlockSpec((tm, D), row), pl.BlockSpec((tm, D), row), _const((1, D)), _const((IN_W, D), 1),
                  pl.BlockSpec((tm, ATTN_W), row), pl.BlockSpec((tm, 2 * KV_W), row),
                  pl.BlockSpec((BLOCK, 2 * KV_W), lambda i: (jnp.minimum(i + 1, nt - 1), 0)),
                  pl.BlockSpec((tm, 2 * CONV_C), row), pl.BlockSpec((tm, 2 * D), row)],
        out_specs=[pl.BlockSpec((tm, D), row), pl.BlockSpec((tm, IN_W), row), pl.BlockSpec((tm, D), row),
                   _const((1, D)), _const((1, IN_W))],
        out_shape=[jax.ShapeDtypeStruct((T, D), F32), jax.ShapeDtypeStruct((T, IN_W), CDT),
                   jax.ShapeDtypeStruct((T, D), CDT), jax.ShapeDtypeStruct((1, D), F32),
                   jax.ShapeDtypeStruct((1, IN_W), F32)],
        compiler_params=_cp("arbitrary"),
    )(dres, x, g, w, dq, dkv, spill, dglu, dgate)


ATTN_TILE = 256
MATRICES = ("w_in", "w_attn_proj", "w_conv_proj", "w_out", "w_mlp1", "w_mlp2")
SMALL = ("mix_norm_g", "b_in", "sinks", "conv_w", "conv_b", "conv_ln_g", "conv_ln_b", "b_conv_proj", "mlp_norm_g")


def forward_backward(x, tgt, hooks):
    def call(fn, kernel, l, *args, **kw):
        rider = hooks.rider(kernel, l)
        if rider is None:
            return fn(*args, **kw)
        outs, landed = fn(*args, rider=rider, **kw)
        hooks.landed(kernel, l, landed)
        return outs

    vec = hooks.vec
    saved = []
    for l in range(DEPTH):
        qkv, rest = call(rms_inproj, "rms_inproj", l, x, vec("mix_norm_g", l), hooks.w_in(l), vec("b_in", l))
        m = hooks.mats(l)
        attn, lse = attn_fwd(qkv, hooks.sinks(l), tq=ATTN_TILE)
        u3, u1 = conv_fwd(rest, hooks.taps(l), vec("conv_b", l), vec("conv_ln_g", l), vec("conv_ln_b", l))
        x1 = merge_out(x, attn, u3, rest, m["w_attn_proj"], m["w_conv_proj"], vec("b_conv_proj", l), m["w_out"])
        x2, pre = call(mlp_fwd, "mlp_fwd", l, x1, vec("mlp_norm_g", l), m["w_mlp1"], m["w_mlp2"])
        saved.append((x, qkv, rest, attn, lse, u3, u1, x1, pre))
        x = x2
    dx, dgf, loss = loss_head(x, hooks.final_g, tgt)
    small = {n: [None] * DEPTH for n in SMALL}
    small["final_norm_g"] = dgf
    for l in reversed(range(DEPTH)):
        x0, qkv, rest, attn, lse, u3, u1, x1, pre = saved[l]
        m = hooks.mats(l)
        dx1, dg2, h2, a, dpre = call(mlp_bwd, "mlp_bwd", l, dx, x1, vec("mlp_norm_g", l), pre, m["w_mlp1"], m["w_mlp2"])
        small["mlp_norm_g"][l] = dg2
        group = {}
        group["w_mlp1"] = call(tn_matmul, "tn_mlp1", l, h2, dpre, tm=1024, tn=1024, tk=4096, name="tn_mlp1", by_chip=True)
        group["w_mlp2"] = call(tn_matmul, "tn_mlp2", l, a, dx, tm=1024, tn=1024, name="tn_mlp2")
        merged, dba, dbc, dattn, du3, dgate, dbcp = merge_bwd(
            dx1, attn, u3, rest, m["w_attn_proj"], m["w_conv_proj"], vec("b_conv_proj", l), m["w_out"])
        small["b_conv_proj"][l] = dbcp
        group["w_out"] = tn_matmul(merged, dx1, tm=1024, tn=1024, name="tn_out")
        group["w_attn_proj"] = tn_matmul(attn, dba, tm=512, tn=256, tk=4096, name="tn_attn_proj", by_chip=True)
        group["w_conv_proj"] = tn_matmul(u3, dbc, tm=512, tn=256, tk=4096, name="tn_conv_proj", by_chip=True)
        hooks.grads(l, "A", group)
        du1, dlg, dlb, dcb = call(conv_bwd_ln, "conv_bwd_ln", l, du3, u1, vec("conv_ln_g", l), vec("conv_ln_b", l))
        small["conv_ln_g"][l], small["conv_ln_b"][l], small["conv_b"][l] = dlg, dlb, dcb
        dglu, dcw = conv_bwd_taps(du1, rest, hooks.taps(l))
        small["conv_w"][l] = dcw
        dq, dkv, spill, dsink = call(attn_bwd, "attn_bwd", l, qkv, dattn, lse, hooks.sinks(l), tq=ATTN_TILE)
        small["sinks"][l] = dsink
        dx, dproj, h, dg, db = inproj_bwd(dx1, x0, vec("mix_norm_g", l), hooks.w_in(l), dq, dkv, spill, dglu, dgate,
                                          tm=ATTN_TILE)
        small["mix_norm_g"][l], small["b_in"][l] = dg, db
        hooks.grads(l, "B", {"w_in": call(tn_matmul, "tn_in", l, dproj, h, tm=768, tn=1024, tk=4096, name="tn_in")})
    return loss, dx, small


class _LocalHooks:
    def __init__(self, p):
        self.p = p
        self.final_g = p["final_norm_g"]
        self.got = {n: [None] * DEPTH for n in MATRICES}

    def w_in(self, l):
        return self.p["w_in"][l].T

    def mats(self, l):
        return {n: self.p[n][l] for n in MATRICES}

    def vec(self, n, l):
        return self.p[n][l]

    def sinks(self, l):
        return self.p["sinks"][l]

    def taps(self, l):
        return self.p["conv_w"][l]

    def rider(self, kernel, l):
        return None

    def grads(self, l, group, g):
        for n, v in g.items():
            if v.ndim == 3:
                v = v.transpose(1, 0, 2).reshape(v.shape[1], -1)
            self.got[n][l] = v.T if n == "w_in" else v


def local_grads(x, tgt, p):
    hooks = _LocalHooks(p)
    loss, dx, small = forward_backward(x, tgt, hooks)
    small["conv_w"] = [g[0:CONV_K] for g in small["conv_w"]]
    small["sinks"] = [g[0, 0:N_Q] for g in small["sinks"]]
    return loss, dx, {**small, **hooks.got}


MESH = pl.DeviceIdType.MESH
N_CHIPS = 4
N_DEV = 8
FLAT_W = 1024
FLAT_PARTS = (("w_in", 960), ("w_attn_proj", 128), ("w_conv_proj", 128), ("w_out", 256), ("w_mlp1", 1024), ("w_mlp2", 1024))
FLAT_ROWS = sum(r for _, r in FLAT_PARTS)
W_IN_ROWS = FLAT_PARTS[0][1]
GROUP_A = (("w_mlp1", 1024), ("w_mlp2", 1024), ("w_out", 256), ("w_attn_proj", 128), ("w_conv_proj", 128))
COL_SHARDED = ("w_in", "w_attn_proj", "w_conv_proj", "w_mlp1")
FULL_SHAPES = {"w_in": (D, IN_W), "w_attn_proj": (ATTN_W, D), "w_conv_proj": (CONV_C, D), "w_out": (D, D),
               "w_mlp1": (D, D_FF), "w_mlp2": (D_FF, D)}


def _place():
    x, y, c = lax.axis_index("x"), lax.axis_index("y"), lax.axis_index("c")
    return x, y, c, 2 * x + y


def _peer_chips(x, y, j):
    return [((x, 1 - y), j ^ 1), ((1 - x, y), j ^ 2), ((1 - x, 1 - y), j ^ 3)]


def _remote(src, dst, sems, k, n, to):
    return pltpu.make_async_remote_copy(src_ref=src, dst_ref=dst, send_sem=sems.at[k], recv_sem=sems.at[n + k],
                                        device_id=to, device_id_type=MESH)


def _half(c, rows):
    h = rows // 2
    return pl.ds(pl.multiple_of(c * h, 16), h)


def gather_rider(wsh):
    R = wsh.shape[0]

    def plan(rins, routs, sems):
        (w_ref,), (out_ref,) = rins, routs
        x, y, c, j = _place()
        peers = _peer_chips(x, y, j)
        mine, other = _half(c, R), _half(1 - c, R)
        sent = [_remote(w_ref.at[mine], out_ref.at[j, mine], sems, k, 6, (*chip, c)) for k, (chip, _) in enumerate(peers)]
        landed = [_remote(w_ref.at[mine], out_ref.at[pj, mine], sems, k, 6, (x, y, c)) for k, (_, pj) in enumerate(peers)]
        passed = [_remote(out_ref.at[pj, mine], out_ref.at[pj, mine], sems, 3 + k, 6, (x, y, 1 - c))
                  for k, (_, pj) in enumerate(peers)]
        handed = [_remote(w_ref.at[mine], out_ref.at[pj, other], sems, 3 + k, 6, (x, y, c)) for k, (_, pj) in enumerate(peers)]
        return sent, landed, passed, handed

    def start(rins, routs, sems):
        for cp in plan(rins, routs, sems)[0]:
            cp.start()

    def finish(rins, routs, sems):
        sent, landed, passed, handed = plan(rins, routs, sems)
        for k in range(3):
            landed[k].wait_recv()
            passed[k].start()
        for cp in handed:
            cp.wait_recv()
        for cp in sent + passed:
            cp.wait_send()

    return Rider((wsh,), (jax.ShapeDtypeStruct((N_CHIPS,) + wsh.shape, wsh.dtype),), 12, start, finish)


def swap_rider(g):
    R = g.shape[1]

    def plan(rins, routs, sems):
        (g_ref,), (got_ref,) = rins, routs
        x, y, c, _ = _place()
        return _remote(g_ref.at[:, _half(1 - c, R), :], got_ref, sems, 0, 1, (x, y, 1 - c))

    def start(rins, routs, sems):
        plan(rins, routs, sems).start()

    def finish(rins, routs, sems):
        plan(rins, routs, sems).wait()

    return Rider((g,), (jax.ShapeDtypeStruct((N_CHIPS, R // 2, FLAT_W), g.dtype),), 2, start, finish)


def exchange_rider(pb):
    def plan(rins, routs, sems):
        (pb_ref,), (got_ref,) = rins, routs
        x, y, c, j = _place()
        peers = _peer_chips(x, y, j)
        sent = [_remote(pb_ref.at[pj], got_ref.at[j], sems, k, 3, (*chip, c)) for k, (chip, pj) in enumerate(peers)]
        landed = [_remote(pb_ref.at[pj], got_ref.at[pj], sems, k, 3, (x, y, c)) for k, (_, pj) in enumerate(peers)]
        return sent, landed

    def start(rins, routs, sems):
        for cp in plan(rins, routs, sems)[0]:
            cp.start()

    def finish(rins, routs, sems):
        sent, landed = plan(rins, routs, sems)
        for cp in landed:
            cp.wait_recv()
        for cp in sent:
            cp.wait_send()

    return Rider((pb,), (jax.ShapeDtypeStruct(pb.shape, pb.dtype),), 6, start, finish)


def share_rider(tot):
    def plan(rins, routs, sems):
        (t_ref,), (got_ref,) = rins, routs
        x, y, c, _ = _place()
        return _remote(t_ref, got_ref, sems, 0, 1, (x, y, 1 - c))

    def start(rins, routs, sems):
        plan(rins, routs, sems).start()

    def finish(rins, routs, sems):
        plan(rins, routs, sems).wait()

    return Rider((tot,), (jax.ShapeDtypeStruct(tot.shape, tot.dtype),), 2, start, finish)


def pair_sum(g, got):
    nj, R, W = g.shape
    h = R // 2
    tile = h // 2

    def body(g_ref, got_ref, pb_ref, own_ref):
        v = g_ref[...] + got_ref[...]
        pb_ref[...] = v.astype(pb_ref.dtype)

        @pl.when(pl.program_id(1) == _place()[3])
        def _():
            own_ref[...] = v

    return pl.pallas_call(
        body, name="pair_sum", grid=(h // tile, nj),
        in_specs=[pl.BlockSpec((None, tile, W), lambda r, j: (j, lax.axis_index("c") * (h // tile) + r, 0)),
                  pl.BlockSpec((None, tile, W), lambda r, j: (j, r, 0))],
        out_specs=[pl.BlockSpec((None, tile, W), lambda r, j: (j, r, 0)), pl.BlockSpec((tile, W), lambda r, j: (r, 0))],
        out_shape=[jax.ShapeDtypeStruct((nj, h, W), CDT), jax.ShapeDtypeStruct((h, W), F32)],
        compiler_params=_cp("arbitrary", "arbitrary"),
    )(g, got)


def total_sum(own, got):
    R, W = own.shape
    tile = R // 2

    def body(own_ref, a_ref, b_ref, c_ref, o_ref):
        o_ref[...] = ((own_ref[...] + a_ref[...].astype(F32)) + b_ref[...].astype(F32)) + c_ref[...].astype(F32)

    def slab(k):
        return pl.BlockSpec((None, tile, W), lambda r: (_place()[3] ^ (k + 1), r, 0))

    return pl.pallas_call(
        body, name="total_sum", grid=(R // tile,),
        in_specs=[pl.BlockSpec((tile, W), lambda r: (r, 0)), slab(0), slab(1), slab(2)],
        out_specs=pl.BlockSpec((tile, W), lambda r: (r, 0)),
        out_shape=jax.ShapeDtypeStruct((R, W), F32),
        compiler_params=_cp("arbitrary"),
    )(own, got, got, got)


def _all_peers(x, y, c):
    return [(x ^ (r >> 2), y ^ ((r >> 1) & 1), c ^ (r & 1)) for r in range(1, N_DEV)]


ROW_ITEMS = (("mix_norm_g", D), ("b_in", IN_W), ("sinks", N_Q), ("conv_b", CONV_C), ("conv_ln_g", CONV_C),
             ("conv_ln_b", CONV_C), ("b_conv_proj", D), ("mlp_norm_g", D))
TAPS_ROW = 16
TAPS_ROWS = 32
LAYER_ROWS = TAPS_ROW + TAPS_ROWS
FINAL_ROW = DEPTH * LAYER_ROWS
SMALL_ROWS = FINAL_ROW + SUBLANES


def _row_chunks():
    out, r = {}, 0
    for n, width in ROW_ITEMS:
        out[n] = [(r + i, FLAT_W * i, min(FLAT_W, width - FLAT_W * i)) for i in range(-(-width // FLAT_W))]
        r += len(out[n])
    assert r <= TAPS_ROW
    return out


def sum_small(gsm):
    chunks = _row_chunks()
    ins = []
    for l in range(DEPTH):
        ins += [gsm[n][l] for n, _ in ROW_ITEMS] + [gsm["conv_w"][l]]
    ins.append(gsm["final_norm_g"])
    n_in = len(ins)

    def body(*refs):
        in_refs, o_ref, buf, send_sems, recv_sems = refs[:n_in], refs[n_in], refs[n_in + 1], refs[n_in + 2], refs[n_in + 3]
        x, y, c, _ = _place()
        me = 4 * x + 2 * y + c
        mine = buf.at[me]
        mine[...] = jnp.zeros((SMALL_ROWS, FLAT_W), F32)
        k = 0
        for l in range(DEPTH):
            for n, _ in ROW_ITEMS:
                for r, c0, wd in chunks[n]:
                    mine[l * LAYER_ROWS + r:l * LAYER_ROWS + r + 1, 0:wd] = in_refs[k][:, c0:c0 + wd]
                k += 1
            mine[l * LAYER_ROWS + TAPS_ROW:(l + 1) * LAYER_ROWS, 0:CONV_C] = in_refs[k][...]
            k += 1
        mine[FINAL_ROW:FINAL_ROW + 1, :] = in_refs[k][...]
        peers = _all_peers(x, y, c)
        sends = [pltpu.make_async_remote_copy(src_ref=mine, dst_ref=mine, send_sem=send_sems.at[r], recv_sem=recv_sems.at[r],
                                              device_id=to, device_id_type=MESH) for r, to in enumerate(peers)]
        for cp in sends:
            cp.start()
        for r in range(N_DEV - 1):
            pltpu.make_async_remote_copy(src_ref=mine, dst_ref=buf.at[me ^ (r + 1)], send_sem=send_sems.at[r],
                                         recv_sem=recv_sems.at[r], device_id=(x, y, c), device_id_type=MESH).wait_recv()
        for cp in sends:
            cp.wait_send()
        acc = buf[0]
        for d in range(1, N_DEV):
            acc = acc + buf[d]
        o_ref[...] = acc

    vm = pl.BlockSpec(memory_space=pltpu.VMEM)
    return pl.pallas_call(
        body, name="sum_small", out_shape=jax.ShapeDtypeStruct((SMALL_ROWS, FLAT_W), F32),
        in_specs=[vm] * n_in, out_specs=vm,
        scratch_shapes=[pltpu.VMEM((N_DEV, SMALL_ROWS, FLAT_W), F32), pltpu.SemaphoreType.DMA((N_DEV - 1,)),
                        pltpu.SemaphoreType.DMA((N_DEV - 1,))],
    )(*ins)


def gather_taps(taps):
    shard = taps.shape[2]

    def body(t_ref, o_ref, buf, send_sems, recv_sems):
        x, y, c, j = _place()
        peers = _peer_chips(x, y, j)
        buf[j] = t_ref[...]
        sends = [pltpu.make_async_remote_copy(src_ref=t_ref, dst_ref=buf.at[j], send_sem=send_sems.at[k],
                                              recv_sem=recv_sems.at[k], device_id=(*chip, c), device_id_type=MESH)
                 for k, (chip, _) in enumerate(peers)]
        for cp in sends:
            cp.start()
        for k, (_, pj) in enumerate(peers):
            pltpu.make_async_remote_copy(src_ref=t_ref, dst_ref=buf.at[pj], send_sem=send_sems.at[k],
                                         recv_sem=recv_sems.at[k], device_id=(x, y, c), device_id_type=MESH).wait_recv()
        for cp in sends:
            cp.wait_send()
        for jj in range(N_CHIPS):
            o_ref[:, :, jj * shard:(jj + 1) * shard] = buf[jj]

    vm = pl.BlockSpec(memory_space=pltpu.VMEM)
    return pl.pallas_call(
        body, name="gather_taps", out_shape=jax.ShapeDtypeStruct(taps.shape[:2] + (N_CHIPS * shard,), taps.dtype),
        in_specs=[vm], out_specs=vm,
        scratch_shapes=[pltpu.VMEM((N_CHIPS,) + taps.shape, taps.dtype), pltpu.SemaphoreType.DMA((3,)),
                        pltpu.SemaphoreType.DMA((3,))],
    )(taps)


def _adam_math(w, g, m, v):
    nm = ADAM_B1 * m + (1.0 - ADAM_B1) * g
    nv = ADAM_B2 * v + (1.0 - ADAM_B2) * jnp.square(g)
    m_hat = nm / (1.0 - ADAM_B1 ** ADAM_STEP)
    v_hat = nv / (1.0 - ADAM_B2 ** ADAM_STEP)
    return -ADAM_LR * (m_hat / (jnp.sqrt(v_hat) + ADAM_EPS) + ADAM_WD * w), nm, nv


def adamw(w, g, m, v, *, name):
    L, R, C = w.shape
    tr = next(t for t in (512, 480, 256, 128) if R % t == 0)

    def body(w_ref, g_ref, m_ref, v_ref, d_ref, nm_ref, nv_ref):
        d_ref[...], nm_ref[...], nv_ref[...] = _adam_math(w_ref[...], g_ref[...], m_ref[...], v_ref[...])

    spec = pl.BlockSpec((None, tr, C), lambda l, i: (l, i, 0))
    out = jax.ShapeDtypeStruct((L, R, C), F32)
    return pl.pallas_call(
        body, name=name, grid=(L, R // tr), in_specs=[spec] * 4, out_specs=[spec] * 3, out_shape=[out] * 3,
        compiler_params=_cp("parallel", "parallel"),
    )(w, g, m, v)


def adamw_small(packed, w, m, v):
    chunks = _row_chunks()
    names = SMALL + ("final_norm_g",)
    as_2d = lambda a: a.reshape(1, -1) if a.ndim == 1 else a
    ins = [as_2d(t[n]) for n in names for t in (w, m, v)]
    shapes = [jax.ShapeDtypeStruct(as_2d(w[n]).shape, F32) for n in names for _ in range(4)]
    n_in = len(ins)

    def body(p_ref, *refs):
        in_refs, out_refs = refs[:n_in], refs[n_in:]
        chip = _place()[3]
        for i, n in enumerate(names):
            w_ref, m_ref, v_ref = in_refs[3 * i:3 * i + 3]
            outs = out_refs[4 * i:4 * i + 4]

            def step(at, g):
                res = (g,) + _adam_math(w_ref[at], g, m_ref[at], v_ref[at])
                for o_ref, val in zip(outs, res):
                    o_ref[at] = val

            if n == "final_norm_g":
                step((slice(None), slice(None)), p_ref[FINAL_ROW:FINAL_ROW + 1, :])
                continue
            for l in range(DEPTH):
                if n == "conv_w":
                    r0 = l * LAYER_ROWS + TAPS_ROW
                    shard = CONV_C // N_CHIPS
                    g = jnp.zeros((CONV_K, shard), F32)
                    for j in range(N_CHIPS):
                        g = jnp.where(chip == j, p_ref[r0:r0 + CONV_K, j * shard:(j + 1) * shard], g)
                    step((l,), g)
                else:
                    for r, c0, wd in chunks[n]:
                        step((slice(l, l + 1), slice(c0, c0 + wd)),
                             p_ref[l * LAYER_ROWS + r:l * LAYER_ROWS + r + 1, 0:wd])

    vm = pl.BlockSpec(memory_space=pltpu.VMEM)
    res = pl.pallas_call(
        body, name="adamw_small", out_shape=shapes,
        in_specs=[vm] + [vm] * n_in, out_specs=[vm] * len(shapes),
    )(packed, *ins)
    dicts = ({}, {}, {}, {})
    for i, n in enumerate(names):
        for d, val in zip(dicts, res[4 * i:4 * i + 4]):
            d[n] = val.reshape(w[n].shape)
    return dicts


def _flat_rows(name, shard):
    return shard.T if name == "w_in" else shard.reshape(-1, FLAT_W)


def _full_matrix(slabs, name):
    K, N = FULL_SHAPES[name]
    if name == "w_in":
        return slabs.reshape(N, K)
    if name in COL_SHARDED:
        return slabs.reshape(N_CHIPS, K, N // N_CHIPS).transpose(1, 0, 2).reshape(K, N)
    return slabs.reshape(K, N)


def _first_row(parts, name):
    r = 0
    for n, rows in parts:
        if n == name:
            return r, rows
        r += rows
    raise KeyError(name)


class _Exchange:
    CARRIERS = {
        ("conv_bwd_ln", 1): ((1, "A"), "swap"), ("attn_bwd", 1): ((1, "A"), "exchange"), ("tn_in", 1): ((1, "A"), "share"),
        ("mlp_bwd", 0): ((1, "B"), "swap"), ("tn_mlp1", 0): ((1, "B"), "exchange"), ("tn_mlp2", 0): ((1, "B"), "share"),
        ("conv_bwd_ln", 0): ((0, "A"), "swap"), ("attn_bwd", 0): ((0, "A"), "exchange"), ("tn_in", 0): ((0, "A"), "share"),
    }

    def __init__(self, w, ci, chip):
        self.w, self.ci, self.chip = w, ci, chip
        self.wsh = [jnp.concatenate([_flat_rows(n, w[n][l]) for n, _ in FLAT_PARTS], axis=0).astype(CDT)
                    for l in range(DEPTH)]
        self.final_g = w["final_norm_g"].reshape(1, D)
        self.slabs = {}
        self.full = {}
        self.units = {}
        self.reduced = {}
        self._landed_weights(0, 0, W_IN_ROWS, _run_alone(gather_rider(self.wsh[0][:W_IN_ROWS]), "gather_w_in")[0])
        self.all_taps = gather_taps(w["conv_w"])

    def _landed_weights(self, l, r0, r1, buf):
        own = self.wsh[l][r0:r1]
        self.slabs.setdefault(l, []).append((r0, lax.dynamic_update_slice(buf, own[None], (self.chip, 0, 0))))

    def _matrix(self, l, name):
        if (l, name) not in self.full:
            r, rows = _first_row(FLAT_PARTS, name)
            r0, buf = next((r0, buf) for r0, buf in self.slabs[l] if r0 <= r < r0 + buf.shape[1])
            self.full[(l, name)] = _full_matrix(buf[:, r - r0:r - r0 + rows], name)
        return self.full[(l, name)]

    def w_in(self, l):
        return self._matrix(l, "w_in")

    def mats(self, l):
        return {n: self._matrix(l, n) for n in MATRICES if n != "w_in"}

    def vec(self, n, l):
        return self.w[n][l].reshape(1, -1)

    def sinks(self, l):
        return self.w["sinks"][l]

    def taps(self, l):
        return self.all_taps[l]

    def rider(self, kernel, l):
        if (kernel, l) == ("rms_inproj", 0):
            return gather_rider(self.wsh[0][W_IN_ROWS:])
        if (kernel, l) == ("mlp_fwd", 0):
            return gather_rider(self.wsh[1])
        if (kernel, l) in self.CARRIERS:
            return self._stage(*self.CARRIERS[(kernel, l)])
        return None

    def landed(self, kernel, l, bufs):
        if (kernel, l) == ("rms_inproj", 0):
            self._landed_weights(0, W_IN_ROWS, FLAT_ROWS, bufs[0])
        elif (kernel, l) == ("mlp_fwd", 0):
            self._landed_weights(1, 0, FLAT_ROWS, bufs[0])
        else:
            self._stage_landed(*self.CARRIERS[(kernel, l)], bufs[0])

    def grads(self, l, group, g):
        if group == "A":
            flat = jnp.concatenate([g[n].reshape(N_CHIPS, rows, FLAT_W) for n, rows in GROUP_A], axis=1)
        else:
            flat = g["w_in"].reshape(N_CHIPS, W_IN_ROWS, FLAT_W)
        self.units[(l, group)] = {"g": flat}

    def _stage(self, key, stage):
        u = self.units[key]
        if stage == "swap":
            return swap_rider(u["g"])
        if stage == "exchange":
            u["pb"], u["own"] = pair_sum(u["g"], u["swap"])
            return exchange_rider(u["pb"])
        u["tot"] = total_sum(u["own"], u["exchange"])
        return share_rider(u["tot"])

    def _stage_landed(self, key, stage, buf):
        u = self.units[key]
        u[stage] = buf
        if stage == "share":
            tot = u["tot"]
            self.reduced[key] = jnp.where(self.ci == 0, jnp.concatenate([tot, buf]), jnp.concatenate([buf, tot]))

    def finish(self):
        key = (0, "B")
        for stage in ("swap", "exchange", "share"):
            self._stage_landed(key, stage, _run_alone(self._stage(key, stage), stage + "_last")[0])
        out = {}
        for n in MATRICES:
            per_layer = []
            for l in range(DEPTH):
                if n == "w_in":
                    per_layer.append(self.reduced[(l, "B")].T)
                    continue
                r, rows = _first_row(GROUP_A, n)
                per_layer.append(self.reduced[(l, "A")][r:r + rows].reshape(self.w[n].shape[1:]))
            out[n] = jnp.stack(per_layer)
        return out


WEIGHTS = ("mix_norm_g", "w_in", "b_in", "sinks", "conv_w", "conv_b", "conv_ln_g", "conv_ln_b", "w_attn_proj",
           "w_conv_proj", "b_conv_proj", "w_out", "mlp_norm_g", "w_mlp1", "w_mlp2", "final_norm_g")


def kernel(x, mix_norm_g, w_in, b_in, sinks, conv_w, conv_b, conv_ln_g, conv_ln_b, w_attn_proj, w_conv_proj, b_conv_proj, w_out, mlp_norm_g, w_mlp1, w_mlp2, final_norm_g, loss_target, m_mix_norm_g, m_w_in, m_b_in, m_sinks, m_conv_w, m_conv_b, m_conv_ln_g, m_conv_ln_b, m_w_attn_proj, m_w_conv_proj, m_b_conv_proj, m_w_out, m_mlp_norm_g, m_w_mlp1, m_w_mlp2, m_final_norm_g, v_mix_norm_g, v_w_in, v_b_in, v_sinks, v_conv_w, v_conv_b, v_conv_ln_g, v_conv_ln_b, v_w_attn_proj, v_w_conv_proj, v_b_conv_proj, v_w_out, v_mlp_norm_g, v_w_mlp1, v_w_mlp2, v_final_norm_g):
    w = dict(zip(WEIGHTS, (mix_norm_g, w_in, b_in, sinks, conv_w, conv_b, conv_ln_g, conv_ln_b, w_attn_proj, w_conv_proj,
                           b_conv_proj, w_out, mlp_norm_g, w_mlp1, w_mlp2, final_norm_g)))
    m = dict(zip(WEIGHTS, (m_mix_norm_g, m_w_in, m_b_in, m_sinks, m_conv_w, m_conv_b, m_conv_ln_g, m_conv_ln_b, m_w_attn_proj,
                           m_w_conv_proj, m_b_conv_proj, m_w_out, m_mlp_norm_g, m_w_mlp1, m_w_mlp2, m_final_norm_g)))
    v = dict(zip(WEIGHTS, (v_mix_norm_g, v_w_in, v_b_in, v_sinks, v_conv_w, v_conv_b, v_conv_ln_g, v_conv_ln_b, v_w_attn_proj,
                           v_w_conv_proj, v_b_conv_proj, v_w_out, v_mlp_norm_g, v_w_mlp1, v_w_mlp2, v_final_norm_g)))
    xi, yi, ci = lax.axis_index("x"), lax.axis_index("y"), lax.axis_index("c")
    chip = 2 * xi + yi

    hooks = _Exchange(w, ci, chip)
    loss, dx, gsm = forward_backward(x[0], loss_target[0], hooks)
    loss = lax.psum(loss[0, 0], ("x", "y", "c"))
    grads = hooks.finish()

    gsmall, delta, new_m, new_v = adamw_small(sum_small(gsm), w, m, v)
    grads.update(gsmall)
    for n in MATRICES:
        t = (lambda a: jnp.swapaxes(a, 1, 2)) if n == "w_in" else (lambda a: a)
        delta[n], new_m[n], new_v[n] = map(t, adamw(t(w[n]), t(grads[n]), t(m[n]), t(v[n]), name="adamw_" + n))

    return (loss, dx[None], *[grads[n] for n in WEIGHTS], *[delta[n] for n in WEIGHTS],
            *[new_m[n] for n in WEIGHTS], *[new_v[n] for n in WEIGHTS])
```

```python
import functools
import math
from typing import Callable, NamedTuple

import jax
import jax.numpy as jnp
import numpy as np
from jax import lax
from jax.experimental import pallas as pl
from jax.experimental.pallas import tpu as pltpu

F32 = jnp.float32
CDT = jnp.bfloat16

D = 1024
DEPTH = 2
N_Q = 8
HEAD_DIM = 64
ATTN_W = 512
KV_W = 128
BLOCK = 128
CONV_C = 512
CONV_K = 31
D_FF = 4096
IN_W = 3840
QKV_W = ATTN_W + 2 * KV_W
REST_W = IN_W - QKV_W
EPS = 1e-6
NEG = -1e30
SCALE = 1.0 / math.sqrt(HEAD_DIM)
SLOPES = [float(2.0 ** (-8.0 * (h + 1) / N_Q)) for h in range(N_Q)]
SUBLANES = 8
HALO = 32

ADAM_LR = 0.001
ADAM_B1 = 0.9
ADAM_B2 = 0.999
ADAM_EPS = 1e-08
ADAM_WD = 0.01
ADAM_STEP = 10

VMEM_LIMIT = 56 * 1024 * 1024


def _cp(*sem):
    return pltpu.CompilerParams(dimension_semantics=sem, vmem_limit_bytes=VMEM_LIMIT)


def _dot(a, b):
    return jnp.dot(a, b, preferred_element_type=F32)


def _dot_nt(a, b):
    return lax.dot_general(a, b, (((1,), (1,)), ((), ())), preferred_element_type=F32)


def _dot_tn(a, b):
    return lax.dot_general(a, b, (((0,), (0,)), ((), ())), preferred_element_type=F32)


def _sig(x):
    return 1.0 / (1.0 + jnp.exp(-x))


def _colsum(v):
    return jnp.sum(v, axis=0, keepdims=True)


def _const(shape, buffers=None):
    mode = {} if buffers is None else {"pipeline_mode": pl.Buffered(buffers)}
    return pl.BlockSpec(shape, lambda *_: (0,) * len(shape), **mode)


class Rider(NamedTuple):
    ins: tuple
    outs: tuple
    n_sems: int
    start: Callable
    finish: Callable


def _any():
    return pl.BlockSpec(memory_space=pl.ANY)


def _run(body, rider, *, name, grid, in_specs, out_specs, out_shape, args, sem, scratch_shapes=()):
    if rider is None:
        return pl.pallas_call(body, name=name, grid=grid, in_specs=list(in_specs), out_specs=list(out_specs),
                              out_shape=list(out_shape), scratch_shapes=list(scratch_shapes),
                              compiler_params=_cp(*sem))(*args)
    n_in, n_out, n_sc = len(in_specs), len(out_specs), len(scratch_shapes)
    r_in, r_out = len(rider.ins), len(rider.outs)

    def riding(*refs):
        ins, rins = refs[:n_in], refs[n_in:n_in + r_in]
        o0 = n_in + r_in
        outs, routs = refs[o0:o0 + n_out], refs[o0 + n_out:o0 + n_out + r_out]
        s0 = o0 + n_out + r_out
        scratch, sems = refs[s0:s0 + n_sc], refs[s0 + n_sc]
        first = functools.reduce(jnp.logical_and, [pl.program_id(a) == 0 for a in range(len(grid))])
        last = functools.reduce(jnp.logical_and, [pl.program_id(a) == grid[a] - 1 for a in range(len(grid))])

        @pl.when(first)
        def _():
            rider.start(rins, routs, sems)

        body(*ins, *outs, *scratch)

        @pl.when(last)
        def _():
            rider.finish(rins, routs, sems)

    res = pl.pallas_call(
        riding, name=name, grid=grid, in_specs=list(in_specs) + [_any()] * r_in,
        out_specs=list(out_specs) + [_any()] * r_out, out_shape=list(out_shape) + list(rider.outs),
        scratch_shapes=list(scratch_shapes) + [pltpu.SemaphoreType.DMA((rider.n_sems,))],
        compiler_params=_cp(*["arbitrary"] * len(grid)))(*args, *rider.ins)
    return res[:n_out], res[n_out:]


def _run_alone(rider, name):
    def body(*refs):
        r_in, r_out = len(rider.ins), len(rider.outs)
        rins, routs, sems = refs[:r_in], refs[r_in:r_in + r_out], refs[r_in + r_out]
        rider.start(rins, routs, sems)
        rider.finish(rins, routs, sems)

    return pl.pallas_call(
        body, name=name, in_specs=[_any()] * len(rider.ins), out_specs=[_any()] * len(rider.outs),
        out_shape=list(rider.outs), scratch_shapes=[pltpu.SemaphoreType.DMA((rider.n_sems,))])(*rider.ins)


def rms_inproj(x, g, wt, b, *, tm=512, rider=None):
    T = x.shape[0]

    def body(x_ref, g_ref, w_ref, b_ref, qkv_ref, rest_ref):
        xv = x_ref[...]
        r = lax.rsqrt(jnp.mean(xv * xv, axis=-1, keepdims=True) + EPS)
        h = (xv * r * g_ref[...]).astype(CDT)
        qkv_ref[...] = (_dot_nt(h, w_ref[0:QKV_W, :]) + b_ref[:, 0:QKV_W]).astype(qkv_ref.dtype)
        for j in range(REST_W // D):
            c0 = QKV_W + D * j
            rest_ref[:, D * j:D * (j + 1)] = _dot_nt(h, w_ref[c0:c0 + D, :]) + b_ref[:, c0:c0 + D]

    return _run(
        body, rider, name="rms_inproj", grid=(T // tm,),
        in_specs=[pl.BlockSpec((tm, D), lambda i: (i, 0)), _const((1, D)), _const((IN_W, D), 1), _const((1, IN_W))],
        out_specs=[pl.BlockSpec((tm, QKV_W), lambda i: (i, 0)), pl.BlockSpec((tm, REST_W), lambda i: (i, 0))],
        out_shape=[jax.ShapeDtypeStruct((T, QKV_W), CDT), jax.ShapeDtypeStruct((T, REST_W), F32)],
        sem=("parallel",), args=(x, g, wt, b))


def _lane_halves(shape):
    lane = lax.broadcasted_iota(jnp.int32, shape, 1)
    return lane < HEAD_DIM, lane >= HEAD_DIM


def _swap_halves(v):
    return pltpu.roll(v.astype(F32), HEAD_DIM, axis=1).astype(v.dtype)


N_KV = KV_W // HEAD_DIM
GROUP = N_Q // N_KV
STACK = GROUP * BLOCK


def _attn_masks(first):
    row = lax.broadcasted_iota(jnp.int32, (STACK, 2 * BLOCK), 0) & (BLOCK - 1)
    col = lax.broadcasted_iota(jnp.int32, (STACK, 2 * BLOCK), 1)
    dist = row + BLOCK - col
    valid = (dist >= 0) & (dist < BLOCK) & ((col >= BLOCK) | jnp.logical_not(first))
    return valid, dist.astype(F32)


def _per_head_column(vals):
    row = lax.broadcasted_iota(jnp.int32, (STACK, 1), 0)
    col = jnp.full((STACK, 1), vals[GROUP - 1], F32)
    for i in reversed(range(GROUP - 1)):
        col = jnp.where(row < (i + 1) * BLOCK, vals[i], col)
    return col


def _stack_heads(dst, src_ref, r0, g):
    lane = lax.broadcasted_iota(jnp.int32, (BLOCK, 2 * HEAD_DIM), 1)
    keep = (lane >= HEAD_DIM) if g else (lane < HEAD_DIM)
    for i in range(GROUP):
        h = GROUP * g + i
        tile = src_ref[pl.ds(r0, BLOCK), (h // 2) * 128:(h // 2 + 1) * 128]
        if h % 2 != g:
            tile = _swap_halves(tile)
        dst[i * BLOCK:(i + 1) * BLOCK, :] = jnp.where(keep, tile, jnp.zeros_like(tile))


def _unstack_heads(dst_ref, stacked, r0, g):
    lane = lax.broadcasted_iota(jnp.int32, (BLOCK, 2 * HEAD_DIM), 1)
    tiles = []
    for j in range(GROUP // 2):
        even = stacked[(2 * j) * BLOCK:(2 * j + 1) * BLOCK, :]
        odd = stacked[(2 * j + 1) * BLOCK:(2 * j + 2) * BLOCK, :]
        lo = _swap_halves(even) if g else even
        hi = odd if g else _swap_halves(odd)
        c0 = ((GROUP * g) // 2 + j) * 128
        tile = jnp.where(lane < HEAD_DIM, lo, hi)
        dst_ref[pl.ds(r0, BLOCK), c0:c0 + 128] = tile.astype(dst_ref.dtype)
        tiles.append((c0, tile))
    return tiles


def _qkv_specs(tq):
    nb = tq // BLOCK
    return [
        pl.BlockSpec((tq, ATTN_W), lambda i: (i, 0)),
        pl.BlockSpec((BLOCK, KV_W), lambda i: (jnp.maximum(i * nb - 1, 0), ATTN_W // KV_W)),
        pl.BlockSpec((tq, KV_W), lambda i: (i, ATTN_W // KV_W)),
        pl.BlockSpec((BLOCK, KV_W), lambda i: (jnp.maximum(i * nb - 1, 0), ATTN_W // KV_W + 1)),
        pl.BlockSpec((tq, KV_W), lambda i: (i, ATTN_W // KV_W + 1)),
    ]


def attn_fwd(qkv, sinks, *, tq=512):
    T = qkv.shape[0]
    nb = tq // BLOCK

    def body(sink_ref, q_ref, kp_ref, kc_ref, vp_ref, vc_ref, o_ref, lse_ref, kext, vext, qs):
        i = pl.program_id(0)
        kext[0:BLOCK, :] = kp_ref[...]
        kext[BLOCK:, :] = kc_ref[...]
        vext[0:BLOCK, :] = vp_ref[...]
        vext[BLOCK:, :] = vc_ref[...]
        lane_l = lax.broadcasted_iota(jnp.int32, (BLOCK, 128), 1)

        def blk(b, carry):
            r0 = pl.multiple_of(b * BLOCK, BLOCK)
            valid, distf = _attn_masks(jnp.logical_and(i == 0, b == 0))
            kc = kext[pl.ds(r0, 2 * BLOCK), :]
            vc = vext[pl.ds(r0, 2 * BLOCK), :]
            lse_t = jnp.zeros((BLOCK, 128), F32)
            for g in range(N_KV):
                heads = range(GROUP * g, GROUP * (g + 1))
                _stack_heads(qs, q_ref, r0, g)
                s = _dot_nt(qs[...], kc) * SCALE - _per_head_column([SLOPES[h] for h in heads]) * distf
                s = jnp.where(valid, s, NEG)
                sink = _per_head_column([sink_ref[h] for h in heads])
                m = jnp.maximum(jnp.max(s, axis=-1, keepdims=True), sink)
                p = jnp.exp(s - m)
                denom = jnp.sum(p, axis=-1, keepdims=True) + jnp.exp(sink - m)
                p = p / denom
                _unstack_heads(o_ref, _dot(p.astype(CDT), vc), r0, g)
                lse = m + jnp.log(denom)
                for i_h, h in enumerate(heads):
                    lse_t = jnp.where(lane_l == h, lse[i_h * BLOCK:(i_h + 1) * BLOCK, :], lse_t)
            lse_ref[pl.ds(r0, BLOCK), :] = lse_t
            return carry

        lax.fori_loop(0, nb, blk, 0)

    return pl.pallas_call(
        body, name="attn_fwd", grid=(T // tq,),
        in_specs=[pl.BlockSpec(memory_space=pltpu.SMEM)] + _qkv_specs(tq),
        out_specs=[pl.BlockSpec((tq, ATTN_W), lambda i: (i, 0)), pl.BlockSpec((tq, 128), lambda i: (i, 0))],
        out_shape=[jax.ShapeDtypeStruct((T, ATTN_W), CDT), jax.ShapeDtypeStruct((T, 128), F32)],
        scratch_shapes=[pltpu.VMEM((tq + BLOCK, KV_W), CDT), pltpu.VMEM((tq + BLOCK, KV_W), CDT),
                        pltpu.VMEM((STACK, 2 * HEAD_DIM), CDT)],
        compiler_params=_cp("parallel"),
    )(sinks, qkv, qkv, qkv, qkv, qkv)


def _halo_before(tm, width, col):
    return pl.BlockSpec((HALO, width), lambda i: (jnp.maximum(i * (tm // HALO) - 1, 0), col))


def _fill_u0(ext, a_ref, b_ref, ha_ref, hb_ref, first):
    hu = ha_ref[...] * _sig(hb_ref[...])
    ext[0:HALO, :] = jnp.where(first, jnp.zeros_like(hu), hu)
    ext[HALO:, :] = a_ref[...] * _sig(b_ref[...])


def _shifted_taps(src, w_ref, base, rc, offsets):
    acc = jnp.zeros((rc, CONV_C), F32)
    for b in range(SUBLANES):
        taps = [(k, o - b) for k, o in enumerate(offsets) if o % SUBLANES == b]
        if not taps:
            continue
        rows = rc if b == 0 else rc + SUBLANES
        part = jnp.zeros((rows, CONV_C), F32)
        for k, o8 in taps:
            part = part + w_ref[k:k + 1, :] * src[base + o8:base + o8 + rows, :]
        acc = acc + (part if b == 0 else part[b:b + rc, :])
    return acc


def _conv_rows(ext, w_ref, r0, rc):
    return _shifted_taps(ext, w_ref, r0, rc, [HALO - (CONV_K - 1) + k for k in range(CONV_K)])


def _layer_norm(u1, g, b):
    mu = jnp.mean(u1, axis=-1, keepdims=True)
    xc = u1 - mu
    rstd = lax.rsqrt(jnp.mean(xc * xc, axis=-1, keepdims=True) + EPS)
    n = xc * rstd
    return n, rstd, n * g + b


CONV_RC = 32


def conv_fwd(rest, cw, cb, lg, lb, *, tm=512):
    T = rest.shape[0]

    def body(a_ref, b_ref, ha_ref, hb_ref, w_ref, cb_ref, lg_ref, lb_ref, o_ref, u1_ref, ext):
        _fill_u0(ext, a_ref, b_ref, ha_ref, hb_ref, pl.program_id(0) == 0)
        for c in range(tm // CONV_RC):
            r0 = c * CONV_RC
            u1 = _conv_rows(ext, w_ref, r0, CONV_RC) + cb_ref[...]
            u1_ref[r0:r0 + CONV_RC, :] = u1
            _, _, u2 = _layer_norm(u1, lg_ref[...], lb_ref[...])
            o_ref[r0:r0 + CONV_RC, :] = (u2 * _sig(u2)).astype(o_ref.dtype)

    row = lambda i: (i, 0)
    return pl.pallas_call(
        body, name="conv_fwd", grid=(T // tm,),
        in_specs=[pl.BlockSpec((tm, CONV_C), row), pl.BlockSpec((tm, CONV_C), lambda i: (i, 1)),
                  _halo_before(tm, CONV_C, 0), _halo_before(tm, CONV_C, 1),
                  _const((CONV_K, CONV_C)), _const((1, CONV_C)), _const((1, CONV_C)), _const((1, CONV_C))],
        out_specs=[pl.BlockSpec((tm, CONV_C), row), pl.BlockSpec((tm, CONV_C), row)],
        out_shape=[jax.ShapeDtypeStruct((T, CONV_C), CDT), jax.ShapeDtypeStruct((T, CONV_C), F32)],
        scratch_shapes=[pltpu.VMEM((tm + HALO, CONV_C), F32)],
        compiler_params=_cp("parallel"),
    )(rest, rest, rest, rest, cw, cb, lg, lb)


def merge_out(x, attn, u3, rest, wa, wc, bc, wo, *, tm=512):
    T = x.shape[0]

    def body(x_ref, at_ref, u_ref, ga_ref, gc_ref, wa_ref, wc_ref, bc_ref, wo_ref, o_ref):
        br_a = _dot(at_ref[...], wa_ref[...])
        br_c = _dot(u_ref[...], wc_ref[...]) + bc_ref[...]
        merged = _sig(ga_ref[...]) * br_a + _sig(gc_ref[...]) * br_c
        o_ref[...] = x_ref[...] + _dot(merged.astype(CDT), wo_ref[...])

    return pl.pallas_call(
        body, name="merge_out", grid=(T // tm,),
        in_specs=[pl.BlockSpec((tm, D), lambda i: (i, 0)), pl.BlockSpec((tm, ATTN_W), lambda i: (i, 0)),
                  pl.BlockSpec((tm, CONV_C), lambda i: (i, 0)),
                  pl.BlockSpec((tm, D), lambda i: (i, 1)), pl.BlockSpec((tm, D), lambda i: (i, 2)),
                  _const((ATTN_W, D), 1), _const((CONV_C, D), 1), _const((1, D)), _const((D, D), 1)],
        out_specs=pl.BlockSpec((tm, D), lambda i: (i, 0)),
        out_shape=jax.ShapeDtypeStruct((T, D), F32),
        compiler_params=_cp("parallel"),
    )(x, attn, u3, rest, rest, wa, wc, bc, wo)


def mlp_fwd(x, g, w1, w2, *, tm=256, tf=D_FF, rider=None):
    T = x.shape[0]
    nf = D_FF // tf

    def body(x_ref, g_ref, w1_ref, w2_ref, o_ref, pre_ref, h_s, acc_s):
        f = pl.program_id(1)

        @pl.when(f == 0)
        def _():
            xv = x_ref[...]
            r = lax.rsqrt(jnp.mean(xv * xv, axis=-1, keepdims=True) + EPS)
            h_s[...] = (xv * r * g_ref[...]).astype(CDT)
            acc_s[...] = jnp.zeros_like(acc_s)

        pre = _dot(h_s[...], w1_ref[...])
        pre_ref[...] = pre
        a = jnp.square(jnp.maximum(pre, 0.0))
        acc_s[...] += _dot(a.astype(CDT), w2_ref[...])

        @pl.when(f == nf - 1)
        def _():
            o_ref[...] = x_ref[...] + acc_s[...]

    mode = {"pipeline_mode": pl.Buffered(1)} if nf == 1 else {}
    return _run(
        body, rider, name="mlp_fwd", grid=(T // tm, nf),
        in_specs=[pl.BlockSpec((tm, D), lambda i, f: (i, 0)), _const((1, D)),
                  pl.BlockSpec((D, tf), lambda i, f: (0, f), **mode), pl.BlockSpec((tf, D), lambda i, f: (f, 0), **mode)],
        out_specs=[pl.BlockSpec((tm, D), lambda i, f: (i, 0)), pl.BlockSpec((tm, tf), lambda i, f: (i, f))],
        out_shape=[jax.ShapeDtypeStruct((T, D), F32), jax.ShapeDtypeStruct((T, D_FF), F32)],
        scratch_shapes=[pltpu.VMEM((tm, D), CDT), pltpu.VMEM((tm, D), F32)],
        sem=("parallel", "arbitrary"), args=(x, g, w1, w2))


def _rms_bwd(xv, g, dh):
    r = lax.rsqrt(jnp.mean(xv * xv, axis=-1, keepdims=True) + EPS)
    xhat = xv * r
    dxh = dh * g
    dx = r * (dxh - xhat * jnp.mean(dxh * xhat, axis=-1, keepdims=True))
    return dx, dh * xhat


def loss_head(x, g, tgt, *, tm=512):
    T = x.shape[0]

    def body(x_ref, g_ref, t_ref, dx_ref, dg_ref, loss_ref):
        @pl.when(pl.program_id(0) == 0)
        def _():
            dg_ref[...] = jnp.zeros_like(dg_ref)
            loss_ref[...] = jnp.zeros_like(loss_ref)

        xv = x_ref[...]
        gv = g_ref[...]
        r = lax.rsqrt(jnp.mean(xv * xv, axis=-1, keepdims=True) + EPS)
        e = xv * r * gv - t_ref[...]
        loss_ref[...] += 0.5 * jnp.sum(jnp.mean(e * e, axis=-1, keepdims=True), axis=0, keepdims=True)
        dx, dg_rows = _rms_bwd(xv, gv, e * (1.0 / D))
        dx_ref[...] = dx
        dg_ref[...] += _colsum(dg_rows)

    return pl.pallas_call(
        body, name="loss_head", grid=(T // tm,),
        in_specs=[pl.BlockSpec((tm, D), lambda i: (i, 0)), _const((1, D)), pl.BlockSpec((tm, D), lambda i: (i, 0))],
        out_specs=[pl.BlockSpec((tm, D), lambda i: (i, 0)), _const((1, D)), _const((1, 128))],
        out_shape=[jax.ShapeDtypeStruct((T, D), F32), jax.ShapeDtypeStruct((1, D), F32),
                   jax.ShapeDtypeStruct((1, 128), F32)],
        compiler_params=_cp("arbitrary"),
    )(x, g, tgt)


def mlp_bwd(dy, x, g, pre, w1, w2, *, tm=256, tf=D_FF, rider=None):
    T = x.shape[0]
    nf = D_FF // tf

    def body(dy_ref, x_ref, g_ref, pre_ref, w1_ref, w2_ref, dx_ref, dg_ref, h_ref, a_ref, dpre_ref, dyb_s, acc_s):
        i, f = pl.program_id(0), pl.program_id(1)

        @pl.when(jnp.logical_and(i == 0, f == 0))
        def _():
            dg_ref[...] = jnp.zeros_like(dg_ref)

        @pl.when(f == 0)
        def _():
            dyb_s[...] = dy_ref[...].astype(CDT)
            acc_s[...] = jnp.zeros_like(acc_s)

        pre = pre_ref[...]
        rl = jnp.maximum(pre, 0.0)
        a_ref[...] = (rl * rl).astype(CDT)
        da = _dot_nt(dyb_s[...], w2_ref[...])
        dpre = (da * (2.0 * rl)).astype(CDT)
        dpre_ref[...] = dpre
        acc_s[...] += _dot_nt(dpre, w1_ref[...])

        @pl.when(f == nf - 1)
        def _():
            xv = x_ref[...]
            gv = g_ref[...]
            dxn, dg_rows = _rms_bwd(xv, gv, acc_s[...])
            dx_ref[...] = dy_ref[...] + dxn
            dg_ref[...] += _colsum(dg_rows)
            r = lax.rsqrt(jnp.mean(xv * xv, axis=-1, keepdims=True) + EPS)
            h_ref[...] = (xv * r * gv).astype(CDT)

    row = lambda i, f: (i, 0)
    mode = {"pipeline_mode": pl.Buffered(1)} if nf == 1 else {}
    return _run(
        body, rider, name="mlp_bwd", grid=(T // tm, nf),
        in_specs=[pl.BlockSpec((tm, D), row), pl.BlockSpec((tm, D), row), _const((1, D)),
                  pl.BlockSpec((tm, tf), lambda i, f: (i, f)),
                  pl.BlockSpec((D, tf), lambda i, f: (0, f), **mode), pl.BlockSpec((tf, D), lambda i, f: (f, 0), **mode)],
        out_specs=[pl.BlockSpec((tm, D), row), _const((1, D)), pl.BlockSpec((tm, D), row),
                   pl.BlockSpec((tm, tf), lambda i, f: (i, f)), pl.BlockSpec((tm, tf), lambda i, f: (i, f))],
        out_shape=[jax.ShapeDtypeStruct((T, D), F32), jax.ShapeDtypeStruct((1, D), F32),
                   jax.ShapeDtypeStruct((T, D), CDT), jax.ShapeDtypeStruct((T, D_FF), CDT),
                   jax.ShapeDtypeStruct((T, D_FF), CDT)],
        scratch_shapes=[pltpu.VMEM((tm, D), CDT), pltpu.VMEM((tm, D), F32)],
        sem=("arbitrary", "arbitrary"), args=(dy, x, g, pre, w1, w2))


def tn_matmul(a, b, *, tm, tn, tk=2048, name, by_chip=False, rider=None):
    T, M = a.shape
    N = b.shape[1]
    tk = min(tk, T)
    nk = T // tk

    def body(a_ref, b_ref, o_ref):
        @pl.when(pl.program_id(2) == 0)
        def _():
            o_ref[...] = jnp.zeros_like(o_ref)

        o_ref[...] += _dot_tn(a_ref[...].astype(CDT), b_ref[...].astype(CDT))

    if by_chip:
        out_spec = pl.BlockSpec((None, tm, tn), lambda i, j, k: (j, i, 0))
        out_shape = jax.ShapeDtypeStruct((N // tn, M, tn), F32)
    else:
        out_spec = pl.BlockSpec((tm, tn), lambda i, j, k: (i, j))
        out_shape = jax.ShapeDtypeStruct((M, N), F32)
    res = _run(
        body, rider, name=name, grid=(M // tm, N // tn, nk),
        in_specs=[pl.BlockSpec((tk, tm), lambda i, j, k: (k, i)), pl.BlockSpec((tk, tn), lambda i, j, k: (k, j))],
        out_specs=[out_spec], out_shape=[out_shape], sem=("parallel", "parallel", "arbitrary"), args=(a, b))
    return res[0] if rider is None else (res[0][0], res[1])


def merge_bwd(dx1, attn, u3, rest, wa, wc, bc, wo, *, tm=512):
    T = dx1.shape[0]

    def body(dx_ref, at_ref, u_ref, ga_ref, gc_ref, wa_ref, wc_ref, bc_ref, wo_ref,
             mg_ref, dba_ref, dbc_ref, dat_ref, du_ref, dgate_ref, dgsum_ref, dbias_ref):
        @pl.when(pl.program_id(0) == 0)
        def _():
            dbias_ref[...] = jnp.zeros_like(dbias_ref)
            dgsum_ref[...] = jnp.zeros_like(dgsum_ref)

        br_a = _dot(at_ref[...], wa_ref[...])
        br_c = _dot(u_ref[...], wc_ref[...]) + bc_ref[...]
        sa = _sig(ga_ref[...])
        sc = _sig(gc_ref[...])
        mg_ref[...] = (sa * br_a + sc * br_c).astype(CDT)
        dm = _dot_nt(dx_ref[...].astype(CDT), wo_ref[...])
        dba = dm * sa
        dbc = dm * sc
        dga = dm * br_a * sa * (1.0 - sa)
        dgc = dm * br_c * sc * (1.0 - sc)
        dgate_ref[:, 0:D] = dga.astype(CDT)
        dgate_ref[:, D:2 * D] = dgc.astype(CDT)
        dgsum_ref[:, 0:D] += _colsum(dga)
        dgsum_ref[:, D:2 * D] += _colsum(dgc)
        dbias_ref[...] += _colsum(dbc)
        dba_b = dba.astype(CDT)
        dbc_b = dbc.astype(CDT)
        dba_ref[...] = dba_b
        dbc_ref[...] = dbc_b
        dat_ref[...] = _dot_nt(dba_b, wa_ref[...]).astype(CDT)
        du_ref[...] = _dot_nt(dbc_b, wc_ref[...])

    row = lambda i: (i, 0)
    return pl.pallas_call(
        body, name="merge_bwd", grid=(T // tm,),
        in_specs=[pl.BlockSpec((tm, D), row), pl.BlockSpec((tm, ATTN_W), row), pl.BlockSpec((tm, CONV_C), row),
                  pl.BlockSpec((tm, D), lambda i: (i, 1)), pl.BlockSpec((tm, D), lambda i: (i, 2)),
                  _const((ATTN_W, D), 1), _const((CONV_C, D), 1), _const((1, D)), _const((D, D), 1)],
        out_specs=[pl.BlockSpec((tm, D), row), pl.BlockSpec((tm, D), row), pl.BlockSpec((tm, D), row),
                   pl.BlockSpec((tm, ATTN_W), row), pl.BlockSpec((tm, CONV_C), row),
                   pl.BlockSpec((tm, 2 * D), row), _const((1, 2 * D)), _const((1, D))],
        out_shape=[jax.ShapeDtypeStruct((T, D), CDT), jax.ShapeDtypeStruct((T, D), CDT),
                   jax.ShapeDtypeStruct((T, D), CDT), jax.ShapeDtypeStruct((T, ATTN_W), CDT),
                   jax.ShapeDtypeStruct((T, CONV_C), F32), jax.ShapeDtypeStruct((T, 2 * D), CDT),
                   jax.ShapeDtypeStruct((1, 2 * D), F32), jax.ShapeDtypeStruct((1, D), F32)],
        compiler_params=_cp("arbitrary"),
    )(dx1, attn, u3, rest, rest, wa, wc, bc, wo)


def conv_bwd_ln(du3, u1, lg, lb, *, tm=512, rider=None):
    T = du3.shape[0]

    def body(du_ref, u1_ref, lg_ref, lb_ref, du1_ref, dlg_ref, dlb_ref, dcb_ref):
        @pl.when(pl.program_id(0) == 0)
        def _():
            dlg_ref[...] = jnp.zeros_like(dlg_ref)
            dlb_ref[...] = jnp.zeros_like(dlb_ref)
            dcb_ref[...] = jnp.zeros_like(dcb_ref)

        dlg = jnp.zeros((1, CONV_C), F32)
        dlb = jnp.zeros((1, CONV_C), F32)
        dcb = jnp.zeros((1, CONV_C), F32)
        for c in range(tm // CONV_RC):
            r0 = c * CONV_RC
            n, rstd, u2 = _layer_norm(u1_ref[r0:r0 + CONV_RC, :], lg_ref[...], lb_ref[...])
            s = _sig(u2)
            du2 = du_ref[r0:r0 + CONV_RC, :] * (s + u2 * s * (1.0 - s))
            dn = du2 * lg_ref[...]
            du1 = rstd * (dn - jnp.mean(dn, axis=-1, keepdims=True) - n * jnp.mean(dn * n, axis=-1, keepdims=True))
            du1_ref[r0:r0 + CONV_RC, :] = du1
            dlg = dlg + _colsum(du2 * n)
            dlb = dlb + _colsum(du2)
            dcb = dcb + _colsum(du1)
        dlg_ref[...] += dlg
        dlb_ref[...] += dlb
        dcb_ref[...] += dcb

    row = lambda i: (i, 0)
    vec = jax.ShapeDtypeStruct((1, CONV_C), F32)
    return _run(
        body, rider, name="conv_bwd_ln", grid=(T // tm,),
        in_specs=[pl.BlockSpec((tm, CONV_C), row), pl.BlockSpec((tm, CONV_C), row), _const((1, CONV_C)), _const((1, CONV_C))],
        out_specs=[pl.BlockSpec((tm, CONV_C), row), _const((1, CONV_C)), _const((1, CONV_C)), _const((1, CONV_C))],
        out_shape=[jax.ShapeDtypeStruct((T, CONV_C), F32), vec, vec, vec],
        sem=("arbitrary",), args=(du3, u1, lg, lb))


def conv_bwd_taps(du1, rest, cw, *, tm=512):
    T = du1.shape[0]
    nt = T // tm

    def body(d_ref, hd_ref, a_ref, b_ref, ha_ref, hb_ref, w_ref, dglu_ref, dgsum_ref, dw_ref, ext, dext, dwacc):
        i = pl.program_id(0)

        @pl.when(i == 0)
        def _():
            dwacc[...] = jnp.zeros_like(dwacc)
            dgsum_ref[...] = jnp.zeros_like(dgsum_ref)

        sums = [jnp.zeros((SUBLANES, CONV_C), F32), jnp.zeros((SUBLANES, CONV_C), F32)]
        _fill_u0(ext, a_ref, b_ref, ha_ref, hb_ref, i == 0)
        dext[0:SUBLANES, :] = jnp.zeros((SUBLANES, CONV_C), F32)
        dext[SUBLANES:SUBLANES + tm, :] = d_ref[...]
        hd = hd_ref[...]
        dext[SUBLANES + tm:, :] = jnp.where(i == nt - 1, jnp.zeros_like(hd), hd)
        qrow = lax.broadcasted_iota(jnp.int32, (CONV_RC + SUBLANES, CONV_C), 0)
        for c in range(tm // CONV_RC):
            r0 = c * CONV_RC
            du0 = _shifted_taps(dext, w_ref, r0 + SUBLANES, CONV_RC, [CONV_K - 1 - k for k in range(CONV_K)])
            for b in range(SUBLANES):
                taps = [(k, HALO - (CONV_K - 1) + k - b) for k in range(CONV_K) if (HALO - (CONV_K - 1) + k) % SUBLANES == b]
                if b == 0:
                    rows = CONV_RC
                    dsh = dext[r0 + SUBLANES:r0 + SUBLANES + rows, :]
                else:
                    rows = CONV_RC + SUBLANES
                    dsh = dext[r0 + SUBLANES - b:r0 + SUBLANES - b + rows, :]
                    dsh = jnp.where((qrow >= b) & (qrow < CONV_RC + b), dsh, 0.0)
                for k, o8 in taps:
                    prod = dsh * ext[r0 + o8:r0 + o8 + rows, :]
                    dwacc[8 * k:8 * k + 8, :] += jnp.sum(prod.reshape(rows // SUBLANES, SUBLANES, CONV_C), axis=0)
            av = a_ref[r0:r0 + CONV_RC, :]
            sb = _sig(b_ref[r0:r0 + CONV_RC, :])
            for half, dg in enumerate((du0 * sb, du0 * av * sb * (1.0 - sb))):
                dglu_ref[r0:r0 + CONV_RC, half * CONV_C:(half + 1) * CONV_C] = dg.astype(CDT)
                sums[half] = sums[half] + jnp.sum(dg.reshape(CONV_RC // SUBLANES, SUBLANES, CONV_C), axis=0)
        for half in range(2):
            dgsum_ref[:, half * CONV_C:(half + 1) * CONV_C] += _colsum(sums[half])

        @pl.when(i == nt - 1)
        def _():
            dw_ref[...] = jnp.zeros_like(dw_ref)
            for k in range(CONV_K):
                dw_ref[k:k + 1, :] = _colsum(dwacc[8 * k:8 * k + 8, :])

    row = lambda i: (i, 0)
    return pl.pallas_call(
        body, name="conv_bwd_taps", grid=(nt,),
        in_specs=[pl.BlockSpec((tm, CONV_C), row),
                  pl.BlockSpec((HALO, CONV_C), lambda i: (jnp.minimum((i + 1) * (tm // HALO), T // HALO - 1), 0)),
                  pl.BlockSpec((tm, CONV_C), row), pl.BlockSpec((tm, CONV_C), lambda i: (i, 1)),
                  _halo_before(tm, CONV_C, 0), _halo_before(tm, CONV_C, 1), _const((CONV_K, CONV_C))],
        out_specs=[pl.BlockSpec((tm, 2 * CONV_C), row), _const((1, 2 * CONV_C)), _const((HALO, CONV_C))],
        out_shape=[jax.ShapeDtypeStruct((T, 2 * CONV_C), CDT), jax.ShapeDtypeStruct((1, 2 * CONV_C), F32),
                   jax.ShapeDtypeStruct((HALO, CONV_C), F32)],
        scratch_shapes=[pltpu.VMEM((tm + HALO, CONV_C), F32), pltpu.VMEM((SUBLANES + tm + HALO, CONV_C), F32),
                        pltpu.VMEM((8 * CONV_K, CONV_C), F32)],
        compiler_params=_cp("arbitrary"),
    )(du1, du1, rest, rest, rest, rest, cw)


def attn_bwd(qkv, do, lse, sinks, *, tq=512, rider=None):
    T = qkv.shape[0]
    nb = tq // BLOCK

    def body(sink_ref, q_ref, kp_ref, kc_ref, vp_ref, vc_ref, do_ref, lse_ref,
             dq_ref, dqsum_ref, dkv_ref, spill_ref, dsink_ref, kext, vext, dkext, dvext, qs, dos):
        i = pl.program_id(0)

        @pl.when(i == 0)
        def _():
            dsink_ref[...] = jnp.zeros_like(dsink_ref)
            dqsum_ref[...] = jnp.zeros_like(dqsum_ref)

        kext[0:BLOCK, :] = kp_ref[...]
        kext[BLOCK:, :] = kc_ref[...]
        vext[0:BLOCK, :] = vp_ref[...]
        vext[BLOCK:, :] = vc_ref[...]
        dkext[...] = jnp.zeros_like(dkext)
        dvext[...] = jnp.zeros_like(dvext)
        lane_l = lax.broadcasted_iota(jnp.int32, (BLOCK, 128), 1)
        lane_k = lax.broadcasted_iota(jnp.int32, (2 * BLOCK, KV_W), 1)

        def blk(b, dsink):
            r0 = pl.multiple_of(b * BLOCK, BLOCK)
            valid, distf = _attn_masks(jnp.logical_and(i == 0, b == 0))
            kc = kext[pl.ds(r0, 2 * BLOCK), :]
            vc = vext[pl.ds(r0, 2 * BLOCK), :]
            lse_t = lse_ref[pl.ds(r0, BLOCK), :]
            dk = jnp.zeros((2 * BLOCK, KV_W), F32)
            dv = jnp.zeros((2 * BLOCK, KV_W), F32)
            for g in range(N_KV):
                heads = range(GROUP * g, GROUP * (g + 1))
                _stack_heads(qs, q_ref, r0, g)
                _stack_heads(dos, do_ref, r0, g)
                qv = qs[...]
                dov = dos[...]
                s = _dot_nt(qv, kc) * SCALE - _per_head_column([SLOPES[h] for h in heads]) * distf
                s = jnp.where(valid, s, NEG)
                lse = jnp.concatenate(
                    [jnp.sum(jnp.where(lane_l == h, lse_t, 0.0), axis=-1, keepdims=True) for h in heads], axis=0)
                p = jnp.exp(s - lse)
                dp = _dot_nt(dov, vc)
                dd = jnp.sum(p * dp, axis=-1, keepdims=True)
                ds = (p * (dp - dd)).astype(CDT)
                keep = (lane_k >= HEAD_DIM) if g else (lane_k < HEAD_DIM)
                for c0, tile in _unstack_heads(dq_ref, _dot(ds, jnp.where(keep, kc, jnp.zeros_like(kc))) * SCALE, r0, g):
                    dqsum_ref[:, c0:c0 + 128] += _colsum(tile)
                dk = dk + _dot_tn(ds, qv)
                dv = dv + _dot_tn(p.astype(CDT), dov)
                wsink = jnp.exp(_per_head_column([sink_ref[h] for h in heads]) - lse) * dd
                for i_h, h in enumerate(heads):
                    part = jnp.sum(wsink[i_h * BLOCK:(i_h + 1) * BLOCK, :], axis=0, keepdims=True)
                    dsink = dsink - jnp.where(lane_l[0:1, :] == h, part, 0.0)
            dkext[pl.ds(r0, 2 * BLOCK), :] += dk * SCALE
            dvext[pl.ds(r0, 2 * BLOCK), :] += dv
            return dsink

        dsink_ref[...] += lax.fori_loop(0, nb, blk, jnp.zeros((1, 128), F32))
        dkv_ref[:, 0:KV_W] = dkext[BLOCK:, :]
        dkv_ref[:, KV_W:2 * KV_W] = dvext[BLOCK:, :]
        spill_ref[:, 0:KV_W] = dkext[0:BLOCK, :]
        spill_ref[:, KV_W:2 * KV_W] = dvext[0:BLOCK, :]

    row = lambda i: (i, 0)
    return _run(
        body, rider, name="attn_bwd", grid=(T // tq,),
        in_specs=[pl.BlockSpec(memory_space=pltpu.SMEM)] + _qkv_specs(tq)
        + [pl.BlockSpec((tq, ATTN_W), row), pl.BlockSpec((tq, 128), row)],
        out_specs=[pl.BlockSpec((tq, ATTN_W), row), _const((1, ATTN_W)), pl.BlockSpec((tq, 2 * KV_W), row),
                   pl.BlockSpec((BLOCK, 2 * KV_W), row), _const((1, 128))],
        out_shape=[jax.ShapeDtypeStruct((T, ATTN_W), CDT), jax.ShapeDtypeStruct((1, ATTN_W), F32),
                   jax.ShapeDtypeStruct((T, 2 * KV_W), F32),
                   jax.ShapeDtypeStruct((T // tq * BLOCK, 2 * KV_W), F32), jax.ShapeDtypeStruct((1, 128), F32)],
        scratch_shapes=[pltpu.VMEM((tq + BLOCK, KV_W), CDT), pltpu.VMEM((tq + BLOCK, KV_W), CDT),
                        pltpu.VMEM((tq + BLOCK, KV_W), F32), pltpu.VMEM((tq + BLOCK, KV_W), F32),
                        pltpu.VMEM((STACK, 2 * HEAD_DIM), CDT), pltpu.VMEM((STACK, 2 * HEAD_DIM), CDT)],
        sem=("arbitrary",), args=(sinks, qkv, qkv, qkv, qkv, qkv, do, lse))


def inproj_bwd(dres, x, g, w, dq, dkv, spill, dglu, dgate, sums, *, tm=512):
    T = x.shape[0]
    nt = T // tm
    pieces = ((0, ATTN_W), (QKV_W, 2 * CONV_C), (QKV_W + 2 * CONV_C, 2 * D))

    def body(dr_ref, x_ref, g_ref, w_ref, dq_ref, dkv_ref, sp_ref, dglu_ref, dgate_ref, sq_ref, sglu_ref, sgate_ref,
             dx_ref, dp_ref, h_ref, dg_ref, db_ref):
        i = pl.program_id(0)

        @pl.when(i == 0)
        def _():
            dg_ref[...] = jnp.zeros_like(dg_ref)
            db_ref[:, ATTN_W:QKV_W] = jnp.zeros((1, QKV_W - ATTN_W), F32)
            for (c0, wd), s_ref in zip(pieces, (sq_ref, sglu_ref, sgate_ref)):
                db_ref[:, c0:c0 + wd] = s_ref[...]

        sp = sp_ref[...]
        sp = jnp.where(i == nt - 1, jnp.zeros_like(sp), sp)
        dkv = dkv_ref[...]
        db_ref[:, ATTN_W:QKV_W] += _colsum(dkv) + _colsum(sp)
        dp_ref[0:tm - BLOCK, ATTN_W:QKV_W] = dkv[0:tm - BLOCK, :].astype(CDT)
        dp_ref[tm - BLOCK:tm, ATTN_W:QKV_W] = (dkv[tm - BLOCK:tm, :] + sp).astype(CDT)
        for (c0, wd), ref in zip(pieces, (dq_ref, dglu_ref, dgate_ref)):
            dp_ref[:, c0:c0 + wd] = ref[...]
        dh = _dot(dp_ref[...], w_ref[...])
        xv = x_ref[...]
        gv = g_ref[...]
        dxn, dg_rows = _rms_bwd(xv, gv, dh)
        dx_ref[...] = dr_ref[...] + dxn
        dg_ref[...] += _colsum(dg_rows)
        r = lax.rsqrt(jnp.mean(xv * xv, axis=-1, keepdims=True) + EPS)
        h_ref[...] = (xv * r * gv).astype(CDT)

    row = lambda i: (i, 0)
    return pl.pallas_call(
        body, name="inproj_bwd", grid=(nt,),
        in_specs=[pl.BlockSpec((tm, D), row), pl.BlockSpec((tm, D), row), _const((1, D)), _const((IN_W, D), 1),
                  pl.BlockSpec((tm, ATTN_W), row), pl.BlockSpec((tm, 2 * KV_W), row),
                  pl.BlockSpec((BLOCK, 2 * KV_W), lambda i: (jnp.minimum(i + 1, nt - 1), 0)),
                  pl.BlockSpec((tm, 2 * CONV_C), row), pl.BlockSpec((tm, 2 * D), row)]
        + [_const((1, wd)) for _, wd in pieces],
        out_specs=[pl.BlockSpec((tm, D), row), pl.BlockSpec((tm, IN_W), row), pl.BlockSpec((tm, D), row),
                   _const((1, D)), _const((1, IN_W))],
        out_shape=[jax.ShapeDtypeStruct((T, D), F32), jax.ShapeDtypeStruct((T, IN_W), CDT),
                   jax.ShapeDtypeStruct((T, D), CDT), jax.ShapeDtypeStruct((1, D), F32),
                   jax.ShapeDtypeStruct((1, IN_W), F32)],
        compiler_params=_cp("arbitrary"),
    )(dres, x, g, w, dq, dkv, spill, dglu, dgate, *sums)


ATTN_TILE = 256
MATRICES = ("w_in", "w_attn_proj", "w_conv_proj", "w_out", "w_mlp1", "w_mlp2")
SMALL = ("mix_norm_g", "b_in", "sinks", "conv_w", "conv_b", "conv_ln_g", "conv_ln_b", "b_conv_proj", "mlp_norm_g")


def forward_backward(x, tgt, hooks):
    def call(fn, kernel, l, *args, **kw):
        rider = hooks.rider(kernel, l)
        if rider is None:
            return fn(*args, **kw)
        outs, landed = fn(*args, rider=rider, **kw)
        hooks.landed(kernel, l, landed)
        return outs

    vec = hooks.vec
    saved = []
    for l in range(DEPTH):
        qkv, rest = call(rms_inproj, "rms_inproj", l, x, vec("mix_norm_g", l), hooks.w_in(l), vec("b_in", l))
        m = hooks.mats(l)
        attn, lse = attn_fwd(qkv, hooks.sinks(l), tq=ATTN_TILE)
        u3, u1 = conv_fwd(rest, hooks.taps(l), vec("conv_b", l), vec("conv_ln_g", l), vec("conv_ln_b", l))
        x1 = merge_out(x, attn, u3, rest, m["w_attn_proj"], m["w_conv_proj"], vec("b_conv_proj", l), m["w_out"])
        x2, pre = call(mlp_fwd, "mlp_fwd", l, x1, vec("mlp_norm_g", l), m["w_mlp1"], m["w_mlp2"])
        saved.append((x, qkv, rest, attn, lse, u3, u1, x1, pre))
        x = x2
    dx, dgf, loss = loss_head(x, hooks.final_g, tgt)
    small = {n: [None] * DEPTH for n in SMALL}
    small["final_norm_g"] = dgf
    for l in reversed(range(DEPTH)):
        x0, qkv, rest, attn, lse, u3, u1, x1, pre = saved[l]
        m = hooks.mats(l)
        dx1, dg2, h2, a, dpre = call(mlp_bwd, "mlp_bwd", l, dx, x1, vec("mlp_norm_g", l), pre, m["w_mlp1"], m["w_mlp2"])
        small["mlp_norm_g"][l] = dg2
        group = {}
        group["w_mlp1"] = call(tn_matmul, "tn_mlp1", l, h2, dpre, tm=1024, tn=1024, tk=4096, name="tn_mlp1", by_chip=True)
        group["w_mlp2"] = call(tn_matmul, "tn_mlp2", l, a, dx, tm=1024, tn=1024, name="tn_mlp2")
        merged, dba, dbc, dattn, du3, dgate, dgate_sum, dbcp = merge_bwd(
            dx1, attn, u3, rest, m["w_attn_proj"], m["w_conv_proj"], vec("b_conv_proj", l), m["w_out"])
        small["b_conv_proj"][l] = dbcp
        group["w_out"] = tn_matmul(merged, dx1, tm=1024, tn=1024, name="tn_out")
        group["w_attn_proj"] = tn_matmul(attn, dba, tm=512, tn=256, tk=4096, name="tn_attn_proj", by_chip=True)
        group["w_conv_proj"] = tn_matmul(u3, dbc, tm=512, tn=256, tk=4096, name="tn_conv_proj", by_chip=True)
        hooks.grads(l, "A", group)
        du1, dlg, dlb, dcb = call(conv_bwd_ln, "conv_bwd_ln", l, du3, u1, vec("conv_ln_g", l), vec("conv_ln_b", l))
        small["conv_ln_g"][l], small["conv_ln_b"][l], small["conv_b"][l] = dlg, dlb, dcb
        dglu, dglu_sum, dcw = conv_bwd_taps(du1, rest, hooks.taps(l))
        small["conv_w"][l] = dcw
        dq, dq_sum, dkv, spill, dsink = call(attn_bwd, "attn_bwd", l, qkv, dattn, lse, hooks.sinks(l), tq=ATTN_TILE)
        small["sinks"][l] = dsink
        dx, dproj, h, dg, db = inproj_bwd(dx1, x0, vec("mix_norm_g", l), hooks.w_in(l), dq, dkv, spill, dglu, dgate,
                                          (dq_sum, dglu_sum, dgate_sum), tm=ATTN_TILE)
        small["mix_norm_g"][l], small["b_in"][l] = dg, db
        hooks.grads(l, "B", {"w_in": call(tn_matmul, "tn_in", l, dproj, h, tm=768, tn=1024, tk=4096, name="tn_in")})
    return loss, dx, small


class _LocalHooks:
    def __init__(self, p):
        self.p = p
        self.final_g = p["final_norm_g"]
        self.got = {n: [None] * DEPTH for n in MATRICES}

    def w_in(self, l):
        return self.p["w_in"][l].T

    def mats(self, l):
        return {n: self.p[n][l] for n in MATRICES}

    def vec(self, n, l):
        return self.p[n][l]

    def sinks(self, l):
        return self.p["sinks"][l]

    def taps(self, l):
        return self.p["conv_w"][l]

    def rider(self, kernel, l):
        return None

    def grads(self, l, group, g):
        for n, v in g.items():
            if v.ndim == 3:
                v = v.transpose(1, 0, 2).reshape(v.shape[1], -1)
            self.got[n][l] = v.T if n == "w_in" else v


def local_grads(x, tgt, p):
    hooks = _LocalHooks(p)
    loss, dx, small = forward_backward(x, tgt, hooks)
    small["conv_w"] = [g[0:CONV_K] for g in small["conv_w"]]
    small["sinks"] = [g[0, 0:N_Q] for g in small["sinks"]]
    return loss, dx, {**small, **hooks.got}


MESH = pl.DeviceIdType.MESH
N_CHIPS = 4
N_DEV = 8
FLAT_W = 1024
FLAT_PARTS = (("w_in", 960), ("w_attn_proj", 128), ("w_conv_proj", 128), ("w_out", 256), ("w_mlp1", 1024), ("w_mlp2", 1024))
FLAT_ROWS = sum(r for _, r in FLAT_PARTS)
W_IN_ROWS = FLAT_PARTS[0][1]
GROUP_A = (("w_mlp1", 1024), ("w_mlp2", 1024), ("w_out", 256), ("w_attn_proj", 128), ("w_conv_proj", 128))
COL_SHARDED = ("w_in", "w_attn_proj", "w_conv_proj", "w_mlp1")
FULL_SHAPES = {"w_in": (D, IN_W), "w_attn_proj": (ATTN_W, D), "w_conv_proj": (CONV_C, D), "w_out": (D, D),
               "w_mlp1": (D, D_FF), "w_mlp2": (D_FF, D)}


def _place():
    x, y, c = lax.axis_index("x"), lax.axis_index("y"), lax.axis_index("c")
    return x, y, c, 2 * x + y


def _peer_chips(x, y, j):
    return [((x, 1 - y), j ^ 1), ((1 - x, y), j ^ 2), ((1 - x, 1 - y), j ^ 3)]


def _remote(src, dst, sems, k, n, to):
    return pltpu.make_async_remote_copy(src_ref=src, dst_ref=dst, send_sem=sems.at[k], recv_sem=sems.at[n + k],
                                        device_id=to, device_id_type=MESH)


def _half(c, rows):
    h = rows // 2
    return pl.ds(pl.multiple_of(c * h, 16), h)


def gather_rider(wsh):
    R = wsh.shape[0]

    def plan(rins, routs, sems):
        (w_ref,), (out_ref,) = rins, routs
        x, y, c, j = _place()
        peers = _peer_chips(x, y, j)
        mine, other = _half(c, R), _half(1 - c, R)
        sent = [_remote(w_ref.at[mine], out_ref.at[j, mine], sems, k, 6, (*chip, c)) for k, (chip, _) in enumerate(peers)]
        landed = [_remote(w_ref.at[mine], out_ref.at[pj, mine], sems, k, 6, (x, y, c)) for k, (_, pj) in enumerate(peers)]
        passed = [_remote(out_ref.at[pj, mine], out_ref.at[pj, mine], sems, 3 + k, 6, (x, y, 1 - c))
                  for k, (_, pj) in enumerate(peers)]
        handed = [_remote(w_ref.at[mine], out_ref.at[pj, other], sems, 3 + k, 6, (x, y, c)) for k, (_, pj) in enumerate(peers)]
        return sent, landed, passed, handed

    def start(rins, routs, sems):
        for cp in plan(rins, routs, sems)[0]:
            cp.start()

    def finish(rins, routs, sems):
        sent, landed, passed, handed = plan(rins, routs, sems)
        for k in range(3):
            landed[k].wait_recv()
            passed[k].start()
        for cp in handed:
            cp.wait_recv()
        for cp in sent + passed:
            cp.wait_send()

    return Rider((wsh,), (jax.ShapeDtypeStruct((N_CHIPS,) + wsh.shape, wsh.dtype),), 12, start, finish)


def swap_rider(g):
    R = g.shape[1]

    def plan(rins, routs, sems):
        (g_ref,), (got_ref,) = rins, routs
        x, y, c, _ = _place()
        return _remote(g_ref.at[:, _half(1 - c, R), :], got_ref, sems, 0, 1, (x, y, 1 - c))

    def start(rins, routs, sems):
        plan(rins, routs, sems).start()

    def finish(rins, routs, sems):
        plan(rins, routs, sems).wait()

    return Rider((g,), (jax.ShapeDtypeStruct((N_CHIPS, R // 2, FLAT_W), g.dtype),), 2, start, finish)


def exchange_rider(pb):
    def plan(rins, routs, sems):
        (pb_ref,), (got_ref,) = rins, routs
        x, y, c, j = _place()
        peers = _peer_chips(x, y, j)
        sent = [_remote(pb_ref.at[pj], got_ref.at[j], sems, k, 3, (*chip, c)) for k, (chip, pj) in enumerate(peers)]
        landed = [_remote(pb_ref.at[pj], got_ref.at[pj], sems, k, 3, (x, y, c)) for k, (_, pj) in enumerate(peers)]
        return sent, landed

    def start(rins, routs, sems):
        for cp in plan(rins, routs, sems)[0]:
            cp.start()

    def finish(rins, routs, sems):
        sent, landed = plan(rins, routs, sems)
        for cp in landed:
            cp.wait_recv()
        for cp in sent:
            cp.wait_send()

    return Rider((pb,), (jax.ShapeDtypeStruct(pb.shape, pb.dtype),), 6, start, finish)


def share_rider(tot):
    def plan(rins, routs, sems):
        (t_ref,), (got_ref,) = rins, routs
        x, y, c, _ = _place()
        return _remote(t_ref, got_ref, sems, 0, 1, (x, y, 1 - c))

    def start(rins, routs, sems):
        plan(rins, routs, sems).start()

    def finish(rins, routs, sems):
        plan(rins, routs, sems).wait()

    return Rider((tot,), (jax.ShapeDtypeStruct(tot.shape, tot.dtype),), 2, start, finish)


def pair_sum(g, got):
    nj, R, W = g.shape
    h = R // 2
    tile = h // 2

    def body(g_ref, got_ref, pb_ref, own_ref):
        v = g_ref[...] + got_ref[...]
        pb_ref[...] = v.astype(pb_ref.dtype)

        @pl.when(pl.program_id(1) == _place()[3])
        def _():
            own_ref[...] = v

    return pl.pallas_call(
        body, name="pair_sum", grid=(h // tile, nj),
        in_specs=[pl.BlockSpec((None, tile, W), lambda r, j: (j, lax.axis_index("c") * (h // tile) + r, 0)),
                  pl.BlockSpec((None, tile, W), lambda r, j: (j, r, 0))],
        out_specs=[pl.BlockSpec((None, tile, W), lambda r, j: (j, r, 0)), pl.BlockSpec((tile, W), lambda r, j: (r, 0))],
        out_shape=[jax.ShapeDtypeStruct((nj, h, W), CDT), jax.ShapeDtypeStruct((h, W), F32)],
        compiler_params=_cp("arbitrary", "arbitrary"),
    )(g, got)


def total_sum(own, got):
    R, W = own.shape
    tile = R // 2

    def body(own_ref, a_ref, b_ref, c_ref, o_ref):
        o_ref[...] = ((own_ref[...] + a_ref[...].astype(F32)) + b_ref[...].astype(F32)) + c_ref[...].astype(F32)

    def slab(k):
        return pl.BlockSpec((None, tile, W), lambda r: (_place()[3] ^ (k + 1), r, 0))

    return pl.pallas_call(
        body, name="total_sum", grid=(R // tile,),
        in_specs=[pl.BlockSpec((tile, W), lambda r: (r, 0)), slab(0), slab(1), slab(2)],
        out_specs=pl.BlockSpec((tile, W), lambda r: (r, 0)),
        out_shape=jax.ShapeDtypeStruct((R, W), F32),
        compiler_params=_cp("arbitrary"),
    )(own, got, got, got)


def _all_peers(x, y, c):
    return [(x ^ (r >> 2), y ^ ((r >> 1) & 1), c ^ (r & 1)) for r in range(1, N_DEV)]


ROW_ITEMS = (("mix_norm_g", D), ("b_in", IN_W), ("sinks", N_Q), ("conv_b", CONV_C), ("conv_ln_g", CONV_C),
             ("conv_ln_b", CONV_C), ("b_conv_proj", D), ("mlp_norm_g", D))
TAPS_ROW = 16
TAPS_ROWS = 32
LAYER_ROWS = TAPS_ROW + TAPS_ROWS
FINAL_ROW = DEPTH * LAYER_ROWS
SMALL_ROWS = FINAL_ROW + SUBLANES


def _row_chunks():
    out, r = {}, 0
    for n, width in ROW_ITEMS:
        out[n] = [(r + i, FLAT_W * i, min(FLAT_W, width - FLAT_W * i)) for i in range(-(-width // FLAT_W))]
        r += len(out[n])
    assert r <= TAPS_ROW
    return out


def sum_small(gsm):
    chunks = _row_chunks()
    ins = []
    for l in range(DEPTH):
        ins += [gsm[n][l] for n, _ in ROW_ITEMS] + [gsm["conv_w"][l]]
    ins.append(gsm["final_norm_g"])
    n_in = len(ins)

    def body(*refs):
        in_refs, o_ref, buf, send_sems, recv_sems = refs[:n_in], refs[n_in], refs[n_in + 1], refs[n_in + 2], refs[n_in + 3]
        x, y, c, _ = _place()
        me = 4 * x + 2 * y + c
        mine = buf.at[me]
        mine[...] = jnp.zeros((SMALL_ROWS, FLAT_W), F32)
        k = 0
        for l in range(DEPTH):
            for n, _ in ROW_ITEMS:
                for r, c0, wd in chunks[n]:
                    mine[l * LAYER_ROWS + r:l * LAYER_ROWS + r + 1, 0:wd] = in_refs[k][:, c0:c0 + wd]
                k += 1
            mine[l * LAYER_ROWS + TAPS_ROW:(l + 1) * LAYER_ROWS, 0:CONV_C] = in_refs[k][...]
            k += 1
        mine[FINAL_ROW:FINAL_ROW + 1, :] = in_refs[k][...]
        peers = _all_peers(x, y, c)
        sends = [pltpu.make_async_remote_copy(src_ref=mine, dst_ref=mine, send_sem=send_sems.at[r], recv_sem=recv_sems.at[r],
                                              device_id=to, device_id_type=MESH) for r, to in enumerate(peers)]
        for cp in sends:
            cp.start()
        for r in range(N_DEV - 1):
            pltpu.make_async_remote_copy(src_ref=mine, dst_ref=buf.at[me ^ (r + 1)], send_sem=send_sems.at[r],
                                         recv_sem=recv_sems.at[r], device_id=(x, y, c), device_id_type=MESH).wait_recv()
        for cp in sends:
            cp.wait_send()
        acc = buf[0]
        for d in range(1, N_DEV):
            acc = acc + buf[d]
        o_ref[...] = acc

    vm = pl.BlockSpec(memory_space=pltpu.VMEM)
    return pl.pallas_call(
        body, name="sum_small", out_shape=jax.ShapeDtypeStruct((SMALL_ROWS, FLAT_W), F32),
        in_specs=[vm] * n_in, out_specs=vm,
        scratch_shapes=[pltpu.VMEM((N_DEV, SMALL_ROWS, FLAT_W), F32), pltpu.SemaphoreType.DMA((N_DEV - 1,)),
                        pltpu.SemaphoreType.DMA((N_DEV - 1,))],
    )(*ins)


def gather_taps(taps):
    shard = taps.shape[2]

    def body(t_ref, o_ref, buf, send_sems, recv_sems):
        x, y, c, j = _place()
        peers = _peer_chips(x, y, j)
        buf[j] = t_ref[...]
        sends = [pltpu.make_async_remote_copy(src_ref=t_ref, dst_ref=buf.at[j], send_sem=send_sems.at[k],
                                              recv_sem=recv_sems.at[k], device_id=(*chip, c), device_id_type=MESH)
                 for k, (chip, _) in enumerate(peers)]
        for cp in sends:
            cp.start()
        for k, (_, pj) in enumerate(peers):
            pltpu.make_async_remote_copy(src_ref=t_ref, dst_ref=buf.at[pj], send_sem=send_sems.at[k],
                                         recv_sem=recv_sems.at[k], device_id=(x, y, c), device_id_type=MESH).wait_recv()
        for cp in sends:
            cp.wait_send()
        for jj in range(N_CHIPS):
            o_ref[:, :, jj * shard:(jj + 1) * shard] = buf[jj]

    vm = pl.BlockSpec(memory_space=pltpu.VMEM)
    return pl.pallas_call(
        body, name="gather_taps", out_shape=jax.ShapeDtypeStruct(taps.shape[:2] + (N_CHIPS * shard,), taps.dtype),
        in_specs=[vm], out_specs=vm,
        scratch_shapes=[pltpu.VMEM((N_CHIPS,) + taps.shape, taps.dtype), pltpu.SemaphoreType.DMA((3,)),
                        pltpu.SemaphoreType.DMA((3,))],
    )(taps)


def _adam_math(w, g, m, v):
    nm = ADAM_B1 * m + (1.0 - ADAM_B1) * g
    nv = ADAM_B2 * v + (1.0 - ADAM_B2) * jnp.square(g)
    m_hat = nm / (1.0 - ADAM_B1 ** ADAM_STEP)
    v_hat = nv / (1.0 - ADAM_B2 ** ADAM_STEP)
    return -ADAM_LR * (m_hat / (jnp.sqrt(v_hat) + ADAM_EPS) + ADAM_WD * w), nm, nv


def adamw(w, g, m, v, *, name):
    L, R, C = w.shape
    tr = next(t for t in (512, 480, 256, 128) if R % t == 0)

    def body(w_ref, g_ref, m_ref, v_ref, d_ref, nm_ref, nv_ref):
        d_ref[...], nm_ref[...], nv_ref[...] = _adam_math(w_ref[...], g_ref[...], m_ref[...], v_ref[...])

    spec = pl.BlockSpec((None, tr, C), lambda l, i: (l, i, 0))
    out = jax.ShapeDtypeStruct((L, R, C), F32)
    return pl.pallas_call(
        body, name=name, grid=(L, R // tr), in_specs=[spec] * 4, out_specs=[spec] * 3, out_shape=[out] * 3,
        compiler_params=_cp("parallel", "parallel"),
    )(w, g, m, v)


def adamw_small(packed, w, m, v):
    chunks = _row_chunks()
    names = SMALL + ("final_norm_g",)
    as_2d = lambda a: a.reshape(1, -1) if a.ndim == 1 else a
    ins = [as_2d(t[n]) for n in names for t in (w, m, v)]
    shapes = [jax.ShapeDtypeStruct(as_2d(w[n]).shape, F32) for n in names for _ in range(4)]
    n_in = len(ins)

    def body(p_ref, *refs):
        in_refs, out_refs = refs[:n_in], refs[n_in:]
        chip = _place()[3]
        for i, n in enumerate(names):
            w_ref, m_ref, v_ref = in_refs[3 * i:3 * i + 3]
            outs = out_refs[4 * i:4 * i + 4]

            def step(at, g):
                res = (g,) + _adam_math(w_ref[at], g, m_ref[at], v_ref[at])
                for o_ref, val in zip(outs, res):
                    o_ref[at] = val

            if n == "final_norm_g":
                step((slice(None), slice(None)), p_ref[FINAL_ROW:FINAL_ROW + 1, :])
                continue
            for l in range(DEPTH):
                if n == "conv_w":
                    r0 = l * LAYER_ROWS + TAPS_ROW
                    shard = CONV_C // N_CHIPS
                    g = jnp.zeros((CONV_K, shard), F32)
                    for j in range(N_CHIPS):
                        g = jnp.where(chip == j, p_ref[r0:r0 + CONV_K, j * shard:(j + 1) * shard], g)
                    step((l,), g)
                else:
                    for r, c0, wd in chunks[n]:
                        step((slice(l, l + 1), slice(c0, c0 + wd)),
                             p_ref[l * LAYER_ROWS + r:l * LAYER_ROWS + r + 1, 0:wd])

    vm = pl.BlockSpec(memory_space=pltpu.VMEM)
    res = pl.pallas_call(
        body, name="adamw_small", out_shape=shapes,
        in_specs=[vm] + [vm] * n_in, out_specs=[vm] * len(shapes),
    )(packed, *ins)
    dicts = ({}, {}, {}, {})
    for i, n in enumerate(names):
        for d, val in zip(dicts, res[4 * i:4 * i + 4]):
            d[n] = val.reshape(w[n].shape)
    return dicts


def _flat_rows(name, shard):
    return shard.T if name == "w_in" else shard.reshape(-1, FLAT_W)


def _full_matrix(slabs, name):
    K, N = FULL_SHAPES[name]
    if name == "w_in":
        return slabs.reshape(N, K)
    if name in COL_SHARDED:
        return slabs.reshape(N_CHIPS, K, N // N_CHIPS).transpose(1, 0, 2).reshape(K, N)
    return slabs.reshape(K, N)


def _first_row(parts, name):
    r = 0
    for n, rows in parts:
        if n == name:
            return r, rows
        r += rows
    raise KeyError(name)


class _Exchange:
    CARRIERS = {
        ("conv_bwd_ln", 1): ((1, "A"), "swap"), ("attn_bwd", 1): ((1, "A"), "exchange"), ("tn_in", 1): ((1, "A"), "share"),
        ("mlp_bwd", 0): ((1, "B"), "swap"), ("tn_mlp1", 0): ((1, "B"), "exchange"), ("tn_mlp2", 0): ((1, "B"), "share"),
        ("conv_bwd_ln", 0): ((0, "A"), "swap"), ("attn_bwd", 0): ((0, "A"), "exchange"), ("tn_in", 0): ((0, "A"), "share"),
    }

    def __init__(self, w, ci, chip):
        self.w, self.ci, self.chip = w, ci, chip
        self.wsh = [jnp.concatenate([_flat_rows(n, w[n][l]) for n, _ in FLAT_PARTS], axis=0).astype(CDT)
                    for l in range(DEPTH)]
        self.final_g = w["final_norm_g"].reshape(1, D)
        self.slabs = {}
        self.full = {}
        self.units = {}
        self.reduced = {}
        self._landed_weights(0, 0, W_IN_ROWS, _run_alone(gather_rider(self.wsh[0][:W_IN_ROWS]), "gather_w_in")[0])
        self.all_taps = gather_taps(w["conv_w"])

    def _landed_weights(self, l, r0, r1, buf):
        own = self.wsh[l][r0:r1]
        self.slabs.setdefault(l, []).append((r0, lax.dynamic_update_slice(buf, own[None], (self.chip, 0, 0))))

    def _matrix(self, l, name):
        if (l, name) not in self.full:
            r, rows = _first_row(FLAT_PARTS, name)
            r0, buf = next((r0, buf) for r0, buf in self.slabs[l] if r0 <= r < r0 + buf.shape[1])
            self.full[(l, name)] = _full_matrix(buf[:, r - r0:r - r0 + rows], name)
        return self.full[(l, name)]

    def w_in(self, l):
        return self._matrix(l, "w_in")

    def mats(self, l):
        return {n: self._matrix(l, n) for n in MATRICES if n != "w_in"}

    def vec(self, n, l):
        return self.w[n][l].reshape(1, -1)

    def sinks(self, l):
        return self.w["sinks"][l]

    def taps(self, l):
        return self.all_taps[l]

    def rider(self, kernel, l):
        if (kernel, l) == ("rms_inproj", 0):
            return gather_rider(self.wsh[0][W_IN_ROWS:])
        if (kernel, l) == ("mlp_fwd", 0):
            return gather_rider(self.wsh[1])
        if (kernel, l) in self.CARRIERS:
            return self._stage(*self.CARRIERS[(kernel, l)])
        return None

    def landed(self, kernel, l, bufs):
        if (kernel, l) == ("rms_inproj", 0):
            self._landed_weights(0, W_IN_ROWS, FLAT_ROWS, bufs[0])
        elif (kernel, l) == ("mlp_fwd", 0):
            self._landed_weights(1, 0, FLAT_ROWS, bufs[0])
        else:
            self._stage_landed(*self.CARRIERS[(kernel, l)], bufs[0])

    def grads(self, l, group, g):
        if group == "A":
            flat = jnp.concatenate([g[n].reshape(N_CHIPS, rows, FLAT_W) for n, rows in GROUP_A], axis=1)
        else:
            flat = g["w_in"].reshape(N_CHIPS, W_IN_ROWS, FLAT_W)
        self.units[(l, group)] = {"g": flat}

    def _stage(self, key, stage):
        u = self.units[key]
        if stage == "swap":
            return swap_rider(u["g"])
        if stage == "exchange":
            u["pb"], u["own"] = pair_sum(u["g"], u["swap"])
            return exchange_rider(u["pb"])
        u["tot"] = total_sum(u["own"], u["exchange"])
        return share_rider(u["tot"])

    def _stage_landed(self, key, stage, buf):
        u = self.units[key]
        u[stage] = buf
        if stage == "share":
            tot = u["tot"]
            self.reduced[key] = jnp.where(self.ci == 0, jnp.concatenate([tot, buf]), jnp.concatenate([buf, tot]))

    def finish(self):
        key = (0, "B")
        for stage in ("swap", "exchange", "share"):
            self._stage_landed(key, stage, _run_alone(self._stage(key, stage), stage + "_last")[0])
        out = {}
        for n in MATRICES:
            per_layer = []
            for l in range(DEPTH):
                if n == "w_in":
                    per_layer.append(self.reduced[(l, "B")].T)
                    continue
                r, rows = _first_row(GROUP_A, n)
                per_layer.append(self.reduced[(l, "A")][r:r + rows].reshape(self.w[n].shape[1:]))
            out[n] = jnp.stack(per_layer)
        return out


WEIGHTS = ("mix_norm_g", "w_in", "b_in", "sinks", "conv_w", "conv_b", "conv_ln_g", "conv_ln_b", "w_attn_proj",
           "w_conv_proj", "b_conv_proj", "w_out", "mlp_norm_g", "w_mlp1", "w_mlp2", "final_norm_g")


def kernel(x, mix_norm_g, w_in, b_in, sinks, conv_w, conv_b, conv_ln_g, conv_ln_b, w_attn_proj, w_conv_proj, b_conv_proj, w_out, mlp_norm_g, w_mlp1, w_mlp2, final_norm_g, loss_target, m_mix_norm_g, m_w_in, m_b_in, m_sinks, m_conv_w, m_conv_b, m_conv_ln_g, m_conv_ln_b, m_w_attn_proj, m_w_conv_proj, m_b_conv_proj, m_w_out, m_mlp_norm_g, m_w_mlp1, m_w_mlp2, m_final_norm_g, v_mix_norm_g, v_w_in, v_b_in, v_sinks, v_conv_w, v_conv_b, v_conv_ln_g, v_conv_ln_b, v_w_attn_proj, v_w_conv_proj, v_b_conv_proj, v_w_out, v_mlp_norm_g, v_w_mlp1, v_w_mlp2, v_final_norm_g):
    w = dict(zip(WEIGHTS, (mix_norm_g, w_in, b_in, sinks, conv_w, conv_b, conv_ln_g, conv_ln_b, w_attn_proj, w_conv_proj,
                           b_conv_proj, w_out, mlp_norm_g, w_mlp1, w_mlp2, final_norm_g)))
    m = dict(zip(WEIGHTS, (m_mix_norm_g, m_w_in, m_b_in, m_sinks, m_conv_w, m_conv_b, m_conv_ln_g, m_conv_ln_b, m_w_attn_proj,
                           m_w_conv_proj, m_b_conv_proj, m_w_out, m_mlp_norm_g, m_w_mlp1, m_w_mlp2, m_final_norm_g)))
    v = dict(zip(WEIGHTS, (v_mix_norm_g, v_w_in, v_b_in, v_sinks, v_conv_w, v_conv_b, v_conv_ln_g, v_conv_ln_b, v_w_attn_proj,
                           v_w_conv_proj, v_b_conv_proj, v_w_out, v_mlp_norm_g, v_w_mlp1, v_w_mlp2, v_final_norm_g)))
    xi, yi, ci = lax.axis_index("x"), lax.axis_index("y"), lax.axis_index("c")
    chip = 2 * xi + yi

    hooks = _Exchange(w, ci, chip)
    loss, dx, gsm = forward_backward(x[0], loss_target[0], hooks)
    loss = lax.psum(loss[0, 0], ("x", "y", "c"))
    grads = hooks.finish()

    gsmall, delta, new_m, new_v = adamw_small(sum_small(gsm), w, m, v)
    grads.update(gsmall)
    for n in MATRICES:
        t = (lambda a: jnp.swapaxes(a, 1, 2)) if n == "w_in" else (lambda a: a)
        delta[n], new_m[n], new_v[n] = map(t, adamw(t(w[n]), t(grads[n]), t(m[n]), t(v[n]), name="adamw_" + n))

    return (loss, dx[None], *[grads[n] for n in WEIGHTS], *[delta[n] for n in WEIGHTS],
            *[new_m[n] for n in WEIGHTS], *[new_v[n] for n in WEIGHTS])
```

```python
import functools
import math
from typing import Callable, NamedTuple

import jax
import jax.numpy as jnp
import numpy as np
from jax import lax
from jax.experimental import pallas as pl
from jax.experimental.pallas import tpu as pltpu

F32 = jnp.float32
CDT = jnp.bfloat16

D = 1024
DEPTH = 2
N_Q = 8
HEAD_DIM = 64
ATTN_W = 512
KV_W = 128
BLOCK = 128
CONV_C = 512
CONV_K = 31
D_FF = 4096
IN_W = 3840
QKV_W = ATTN_W + 2 * KV_W
REST_W = IN_W - QKV_W
EPS = 1e-6
NEG = -1e30
SCALE = 1.0 / math.sqrt(HEAD_DIM)
SLOPES = [float(2.0 ** (-8.0 * (h + 1) / N_Q)) for h in range(N_Q)]
SUBLANES = 8
HALO = 32

ADAM_LR = 0.001
ADAM_B1 = 0.9
ADAM_B2 = 0.999
ADAM_EPS = 1e-08
ADAM_WD = 0.01
ADAM_STEP = 10

VMEM_LIMIT = 56 * 1024 * 1024


def _cp(*sem):
    return pltpu.CompilerParams(dimension_semantics=sem, vmem_limit_bytes=VMEM_LIMIT)


def _dot(a, b):
    return jnp.dot(a, b, preferred_element_type=F32)


def _dot_nt(a, b):
    return lax.dot_general(a, b, (((1,), (1,)), ((), ())), preferred_element_type=F32)


def _dot_tn(a, b):
    return lax.dot_general(a, b, (((0,), (0,)), ((), ())), preferred_element_type=F32)


def _sig(x):
    return 1.0 / (1.0 + jnp.exp(-x))


def _colsum(v):
    return jnp.sum(v, axis=0, keepdims=True)


def _const(shape, buffers=None):
    mode = {} if buffers is None else {"pipeline_mode": pl.Buffered(buffers)}
    return pl.BlockSpec(shape, lambda *_: (0,) * len(shape), **mode)


class Rider(NamedTuple):
    ins: tuple
    outs: tuple
    n_sems: int
    start: Callable
    finish: Callable


def _any():
    return pl.BlockSpec(memory_space=pl.ANY)


def _run(body, rider, *, name, grid, in_specs, out_specs, out_shape, args, sem, scratch_shapes=()):
    if rider is None:
        return pl.pallas_call(body, name=name, grid=grid, in_specs=list(in_specs), out_specs=list(out_specs),
                              out_shape=list(out_shape), scratch_shapes=list(scratch_shapes),
                              compiler_params=_cp(*sem))(*args)
    n_in, n_out, n_sc = len(in_specs), len(out_specs), len(scratch_shapes)
    r_in, r_out = len(rider.ins), len(rider.outs)

    def riding(*refs):
        ins, rins = refs[:n_in], refs[n_in:n_in + r_in]
        o0 = n_in + r_in
        outs, routs = refs[o0:o0 + n_out], refs[o0 + n_out:o0 + n_out + r_out]
        s0 = o0 + n_out + r_out
        scratch, sems = refs[s0:s0 + n_sc], refs[s0 + n_sc]
        first = functools.reduce(jnp.logical_and, [pl.program_id(a) == 0 for a in range(len(grid))])
        last = functools.reduce(jnp.logical_and, [pl.program_id(a) == grid[a] - 1 for a in range(len(grid))])

        @pl.when(first)
        def _():
            rider.start(rins, routs, sems)

        body(*ins, *outs, *scratch)

        @pl.when(last)
        def _():
            rider.finish(rins, routs, sems)

    res = pl.pallas_call(
        riding, name=name, grid=grid, in_specs=list(in_specs) + [_any()] * r_in,
        out_specs=list(out_specs) + [_any()] * r_out, out_shape=list(out_shape) + list(rider.outs),
        scratch_shapes=list(scratch_shapes) + [pltpu.SemaphoreType.DMA((rider.n_sems,))],
        compiler_params=_cp(*["arbitrary"] * len(grid)))(*args, *rider.ins)
    return res[:n_out], res[n_out:]


def _run_alone(rider, name):
    def body(*refs):
        r_in, r_out = len(rider.ins), len(rider.outs)
        rins, routs, sems = refs[:r_in], refs[r_in:r_in + r_out], refs[r_in + r_out]
        rider.start(rins, routs, sems)
        rider.finish(rins, routs, sems)

    return pl.pallas_call(
        body, name=name, in_specs=[_any()] * len(rider.ins), out_specs=[_any()] * len(rider.outs),
        out_shape=list(rider.outs), scratch_shapes=[pltpu.SemaphoreType.DMA((rider.n_sems,))])(*rider.ins)


def rms_inproj(x, g, wt, b, *, tm=512, rider=None):
    T = x.shape[0]

    def body(x_ref, g_ref, w_ref, b_ref, qkv_ref, rest_ref):
        xv = x_ref[...]
        r = lax.rsqrt(jnp.mean(xv * xv, axis=-1, keepdims=True) + EPS)
        h = (xv * r * g_ref[...]).astype(CDT)
        qkv_ref[...] = (_dot_nt(h, w_ref[0:QKV_W, :]) + b_ref[:, 0:QKV_W]).astype(qkv_ref.dtype)
        for j in range(REST_W // D):
            c0 = QKV_W + D * j
            rest_ref[:, D * j:D * (j + 1)] = (_dot_nt(h, w_ref[c0:c0 + D, :]) + b_ref[:, c0:c0 + D]).astype(rest_ref.dtype)

    return _run(
        body, rider, name="rms_inproj", grid=(T // tm,),
        in_specs=[pl.BlockSpec((tm, D), lambda i: (i, 0)), _const((1, D)), _const((IN_W, D), 1), _const((1, IN_W))],
        out_specs=[pl.BlockSpec((tm, QKV_W), lambda i: (i, 0)), pl.BlockSpec((tm, REST_W), lambda i: (i, 0))],
        out_shape=[jax.ShapeDtypeStruct((T, QKV_W), CDT), jax.ShapeDtypeStruct((T, REST_W), CDT)],
        sem=("parallel",), args=(x, g, wt, b))


def _lane_halves(shape):
    lane = lax.broadcasted_iota(jnp.int32, shape, 1)
    return lane < HEAD_DIM, lane >= HEAD_DIM


def _swap_halves(v):
    return pltpu.roll(v.astype(F32), HEAD_DIM, axis=1).astype(v.dtype)


N_KV = KV_W // HEAD_DIM
GROUP = N_Q // N_KV
STACK = GROUP * BLOCK


def _score_bias():
    row = np.arange(STACK)[:, None] % BLOCK
    col = np.arange(2 * BLOCK)[None, :]
    dist = row + BLOCK - col
    window = (dist >= 0) & (dist < BLOCK)
    slopes = np.asarray(SLOPES, np.float32).reshape(N_KV, GROUP)
    out = np.empty((2, N_KV, STACK, 2 * BLOCK), np.float32)
    for first in range(2):
        valid = window & ((col >= BLOCK) | (first == 0))
        for g in range(N_KV):
            slope = np.repeat(slopes[g], BLOCK)[:, None]
            out[first, g] = np.where(valid, -(slope * dist.astype(np.float32)), np.float32(NEG))
    return jnp.asarray(out)


def _per_head_column(vals):
    row = lax.broadcasted_iota(jnp.int32, (STACK, 1), 0)
    col = jnp.full((STACK, 1), vals[GROUP - 1], F32)
    for i in reversed(range(GROUP - 1)):
        col = jnp.where(row < (i + 1) * BLOCK, vals[i], col)
    return col


def _stack_heads(dst, src_ref, r0, g, scale=None):
    lane = lax.broadcasted_iota(jnp.int32, (BLOCK, 2 * HEAD_DIM), 1)
    keep = (lane >= HEAD_DIM) if g else (lane < HEAD_DIM)
    for i in range(GROUP):
        h = GROUP * g + i
        tile = src_ref[pl.ds(r0, BLOCK), (h // 2) * 128:(h // 2 + 1) * 128]
        if h % 2 != g:
            tile = _swap_halves(tile)
        if scale is not None:
            tile = tile * jnp.asarray(scale, tile.dtype)
        dst[i * BLOCK:(i + 1) * BLOCK, :] = jnp.where(keep, tile, jnp.zeros_like(tile))


def _unstack_heads(dst_ref, stacked, r0, g):
    lane = lax.broadcasted_iota(jnp.int32, (BLOCK, 2 * HEAD_DIM), 1)
    tiles = []
    for j in range(GROUP // 2):
        even = stacked[(2 * j) * BLOCK:(2 * j + 1) * BLOCK, :]
        odd = stacked[(2 * j + 1) * BLOCK:(2 * j + 2) * BLOCK, :]
        lo = _swap_halves(even) if g else even
        hi = odd if g else _swap_halves(odd)
        c0 = ((GROUP * g) // 2 + j) * 128
        tile = jnp.where(lane < HEAD_DIM, lo, hi)
        dst_ref[pl.ds(r0, BLOCK), c0:c0 + 128] = tile.astype(dst_ref.dtype)
        tiles.append((c0, tile))
    return tiles


def _qkv_specs(tq):
    nb = tq // BLOCK
    return [
        pl.BlockSpec((tq, ATTN_W), lambda i: (i, 0)),
        pl.BlockSpec((BLOCK, KV_W), lambda i: (jnp.maximum(i * nb - 1, 0), ATTN_W // KV_W)),
        pl.BlockSpec((tq, KV_W), lambda i: (i, ATTN_W // KV_W)),
        pl.BlockSpec((BLOCK, KV_W), lambda i: (jnp.maximum(i * nb - 1, 0), ATTN_W // KV_W + 1)),
        pl.BlockSpec((tq, KV_W), lambda i: (i, ATTN_W // KV_W + 1)),
    ]


def attn_fwd(qkv, sinks, *, tq=512):
    T = qkv.shape[0]
    nb = tq // BLOCK

    def body(sink_ref, bias_ref, q_ref, kp_ref, kc_ref, vp_ref, vc_ref, o_ref, lse_ref, kext, vext, qs):
        i = pl.program_id(0)
        kext[0:BLOCK, :] = kp_ref[...]
        kext[BLOCK:, :] = kc_ref[...]
        vext[0:BLOCK, :] = vp_ref[...]
        vext[BLOCK:, :] = vc_ref[...]
        lane_l = lax.broadcasted_iota(jnp.int32, (BLOCK, 128), 1)

        def blk(b, carry):
            r0 = pl.multiple_of(b * BLOCK, BLOCK)
            first = jnp.logical_and(i == 0, b == 0).astype(jnp.int32)
            kc = kext[pl.ds(r0, 2 * BLOCK), :]
            vc = vext[pl.ds(r0, 2 * BLOCK), :]
            lse_t = jnp.zeros((BLOCK, 128), F32)
            for g in range(N_KV):
                heads = range(GROUP * g, GROUP * (g + 1))
                _stack_heads(qs.at[g], q_ref, r0, g, SCALE)
                s = _dot_nt(qs[g], kc) + bias_ref[first, g]
                sink = _per_head_column([sink_ref[h] for h in heads])
                m = jnp.maximum(jnp.max(s, axis=-1, keepdims=True), sink)
                p = jnp.exp(s - m)
                denom = jnp.sum(p, axis=-1, keepdims=True) + jnp.exp(sink - m)
                p = p / denom
                _unstack_heads(o_ref, _dot(p.astype(CDT), vc), r0, g)
                lse = m + jnp.log(denom)
                for i_h, h in enumerate(heads):
                    lse_t = jnp.where(lane_l == h, lse[i_h * BLOCK:(i_h + 1) * BLOCK, :], lse_t)
            lse_ref[pl.ds(r0, BLOCK), :] = lse_t
            return carry

        lax.fori_loop(0, nb, blk, 0)

    return pl.pallas_call(
        body, name="attn_fwd", grid=(T // tq,),
        in_specs=[pl.BlockSpec(memory_space=pltpu.SMEM), _const((2, N_KV, STACK, 2 * BLOCK), 1)] + _qkv_specs(tq),
        out_specs=[pl.BlockSpec((tq, ATTN_W), lambda i: (i, 0)), pl.BlockSpec((tq, 128), lambda i: (i, 0))],
        out_shape=[jax.ShapeDtypeStruct((T, ATTN_W), CDT), jax.ShapeDtypeStruct((T, 128), F32)],
        scratch_shapes=[pltpu.VMEM((tq + BLOCK, KV_W), CDT), pltpu.VMEM((tq + BLOCK, KV_W), CDT),
                        pltpu.VMEM((N_KV, STACK, 2 * HEAD_DIM), CDT)],
        compiler_params=_cp("parallel"),
    )(sinks, _score_bias(), qkv, qkv, qkv, qkv, qkv)


def _halo_before(tm, width, col):
    return pl.BlockSpec((HALO, width), lambda i: (jnp.maximum(i * (tm // HALO) - 1, 0), col))


def _fill_u0(ext, a_ref, b_ref, ha_ref, hb_ref, first):
    hu = ha_ref[...].astype(F32) * _sig(hb_ref[...].astype(F32))
    ext[0:HALO, :] = jnp.where(first, jnp.zeros_like(hu), hu)
    ext[HALO:, :] = a_ref[...].astype(F32) * _sig(b_ref[...].astype(F32))


def _shifted_taps(src, w_ref, base, rc, offsets):
    acc = jnp.zeros((rc, CONV_C), F32)
    for b in range(SUBLANES):
        taps = [(k, o - b) for k, o in enumerate(offsets) if o % SUBLANES == b]
        if not taps:
            continue
        rows = rc if b == 0 else rc + SUBLANES
        part = jnp.zeros((rows, CONV_C), F32)
        for k, o8 in taps:
            part = part + w_ref[k:k + 1, :] * src[base + o8:base + o8 + rows, :]
        acc = acc + (part if b == 0 else part[b:b + rc, :])
    return acc


def _conv_rows(ext, w_ref, r0, rc):
    return _shifted_taps(ext, w_ref, r0, rc, [HALO - (CONV_K - 1) + k for k in range(CONV_K)])


def _layer_norm(u1, g, b):
    mu = jnp.mean(u1, axis=-1, keepdims=True)
    xc = u1 - mu
    rstd = lax.rsqrt(jnp.mean(xc * xc, axis=-1, keepdims=True) + EPS)
    n = xc * rstd
    return n, rstd, n * g + b


CONV_RC = 32


def conv_fwd(rest, cw, cb, lg, lb, *, tm=512):
    T = rest.shape[0]

    def body(a_ref, b_ref, ha_ref, hb_ref, w_ref, cb_ref, lg_ref, lb_ref, o_ref, u1_ref, ext):
        _fill_u0(ext, a_ref, b_ref, ha_ref, hb_ref, pl.program_id(0) == 0)
        for c in range(tm // CONV_RC):
            r0 = c * CONV_RC
            u1 = _conv_rows(ext, w_ref, r0, CONV_RC) + cb_ref[...]
            u1_ref[r0:r0 + CONV_RC, :] = u1
            _, _, u2 = _layer_norm(u1, lg_ref[...], lb_ref[...])
            o_ref[r0:r0 + CONV_RC, :] = (u2 * _sig(u2)).astype(o_ref.dtype)

    row = lambda i: (i, 0)
    return pl.pallas_call(
        body, name="conv_fwd", grid=(T // tm,),
        in_specs=[pl.BlockSpec((tm, CONV_C), row), pl.BlockSpec((tm, CONV_C), lambda i: (i, 1)),
                  _halo_before(tm, CONV_C, 0), _halo_before(tm, CONV_C, 1),
                  _const((CONV_K, CONV_C)), _const((1, CONV_C)), _const((1, CONV_C)), _const((1, CONV_C))],
        out_specs=[pl.BlockSpec((tm, CONV_C), row), pl.BlockSpec((tm, CONV_C), row)],
        out_shape=[jax.ShapeDtypeStruct((T, CONV_C), CDT), jax.ShapeDtypeStruct((T, CONV_C), F32)],
        scratch_shapes=[pltpu.VMEM((tm + HALO, CONV_C), F32)],
        compiler_params=_cp("parallel"),
    )(rest, rest, rest, rest, cw, cb, lg, lb)


def merge_out(x, attn, u3, rest, wa, wc, bc, wo, *, tm=512):
    T = x.shape[0]

    def body(x_ref, at_ref, u_ref, ga_ref, gc_ref, wa_ref, wc_ref, bc_ref, wo_ref, o_ref):
        br_a = _dot(at_ref[...], wa_ref[...])
        br_c = _dot(u_ref[...], wc_ref[...]) + bc_ref[...]
        merged = _sig(ga_ref[...].astype(F32)) * br_a + _sig(gc_ref[...].astype(F32)) * br_c
        o_ref[...] = x_ref[...] + _dot(merged.astype(CDT), wo_ref[...])

    return pl.pallas_call(
        body, name="merge_out", grid=(T // tm,),
        in_specs=[pl.BlockSpec((tm, D), lambda i: (i, 0)), pl.BlockSpec((tm, ATTN_W), lambda i: (i, 0)),
                  pl.BlockSpec((tm, CONV_C), lambda i: (i, 0)),
                  pl.BlockSpec((tm, D), lambda i: (i, 1)), pl.BlockSpec((tm, D), lambda i: (i, 2)),
                  _const((ATTN_W, D), 1), _const((CONV_C, D), 1), _const((1, D)), _const((D, D), 1)],
        out_specs=pl.BlockSpec((tm, D), lambda i: (i, 0)),
        out_shape=jax.ShapeDtypeStruct((T, D), F32),
        compiler_params=_cp("parallel"),
    )(x, attn, u3, rest, rest, wa, wc, bc, wo)


def mlp_fwd(x, g, w1, w2, *, tm=256, tf=D_FF, rider=None):
    T = x.shape[0]
    nf = D_FF // tf

    def body(x_ref, g_ref, w1_ref, w2_ref, o_ref, a_ref, h_s, acc_s):
        f = pl.program_id(1)

        @pl.when(f == 0)
        def _():
            xv = x_ref[...]
            r = lax.rsqrt(jnp.mean(xv * xv, axis=-1, keepdims=True) + EPS)
            h_s[...] = (xv * r * g_ref[...]).astype(CDT)
            acc_s[...] = jnp.zeros_like(acc_s)

        a = jnp.square(jnp.maximum(_dot(h_s[...], w1_ref[...]), 0.0)).astype(CDT)
        a_ref[...] = a
        acc_s[...] += _dot(a, w2_ref[...])

        @pl.when(f == nf - 1)
        def _():
            o_ref[...] = x_ref[...] + acc_s[...]

    mode = {"pipeline_mode": pl.Buffered(1)} if nf == 1 else {}
    return _run(
        body, rider, name="mlp_fwd", grid=(T // tm, nf),
        in_specs=[pl.BlockSpec((tm, D), lambda i, f: (i, 0)), _const((1, D)),
                  pl.BlockSpec((D, tf), lambda i, f: (0, f), **mode), pl.BlockSpec((tf, D), lambda i, f: (f, 0), **mode)],
        out_specs=[pl.BlockSpec((tm, D), lambda i, f: (i, 0)), pl.BlockSpec((tm, tf), lambda i, f: (i, f))],
        out_shape=[jax.ShapeDtypeStruct((T, D), F32), jax.ShapeDtypeStruct((T, D_FF), CDT)],
        scratch_shapes=[pltpu.VMEM((tm, D), CDT), pltpu.VMEM((tm, D), F32)],
        sem=("parallel", "arbitrary"), args=(x, g, w1, w2))


def _rms_bwd(xv, g, dh):
    r = lax.rsqrt(jnp.mean(xv * xv, axis=-1, keepdims=True) + EPS)
    xhat = xv * r
    dxh = dh * g
    dx = r * (dxh - xhat * jnp.mean(dxh * xhat, axis=-1, keepdims=True))
    return dx, dh * xhat


def loss_head(x, g, tgt, *, tm=512):
    T = x.shape[0]

    def body(x_ref, g_ref, t_ref, dx_ref, dg_ref, loss_ref):
        @pl.when(pl.program_id(0) == 0)
        def _():
            dg_ref[...] = jnp.zeros_like(dg_ref)
            loss_ref[...] = jnp.zeros_like(loss_ref)

        xv = x_ref[...]
        gv = g_ref[...]
        r = lax.rsqrt(jnp.mean(xv * xv, axis=-1, keepdims=True) + EPS)
        e = xv * r * gv - t_ref[...]
        loss_ref[...] += 0.5 * jnp.sum(jnp.mean(e * e, axis=-1, keepdims=True), axis=0, keepdims=True)
        dx, dg_rows = _rms_bwd(xv, gv, e * (1.0 / D))
        dx_ref[...] = dx
        dg_ref[...] += _colsum(dg_rows)

    return pl.pallas_call(
        body, name="loss_head", grid=(T // tm,),
        in_specs=[pl.BlockSpec((tm, D), lambda i: (i, 0)), _const((1, D)), pl.BlockSpec((tm, D), lambda i: (i, 0))],
        out_specs=[pl.BlockSpec((tm, D), lambda i: (i, 0)), _const((1, D)), _const((1, 128))],
        out_shape=[jax.ShapeDtypeStruct((T, D), F32), jax.ShapeDtypeStruct((1, D), F32),
                   jax.ShapeDtypeStruct((1, 128), F32)],
        compiler_params=_cp("arbitrary"),
    )(x, g, tgt)


def mlp_bwd(dy, x, g, a, w1, w2, *, tm=256, tf=D_FF, rider=None):
    T = x.shape[0]
    nf = D_FF // tf

    def body(dy_ref, x_ref, g_ref, a_ref, w1_ref, w2_ref, dx_ref, dg_ref, h_ref, dpre_ref, dyb_s, acc_s):
        i, f = pl.program_id(0), pl.program_id(1)

        @pl.when(jnp.logical_and(i == 0, f == 0))
        def _():
            dg_ref[...] = jnp.zeros_like(dg_ref)

        @pl.when(f == 0)
        def _():
            dyb_s[...] = dy_ref[...].astype(CDT)
            acc_s[...] = jnp.zeros_like(acc_s)

        rl = jnp.sqrt(a_ref[...].astype(F32))
        da = _dot_nt(dyb_s[...], w2_ref[...])
        dpre = (da * (2.0 * rl)).astype(CDT)
        dpre_ref[...] = dpre
        acc_s[...] += _dot_nt(dpre, w1_ref[...])

        @pl.when(f == nf - 1)
        def _():
            xv = x_ref[...]
            gv = g_ref[...]
            dxn, dg_rows = _rms_bwd(xv, gv, acc_s[...])
            dx_ref[...] = dy_ref[...] + dxn
            dg_ref[...] += _colsum(dg_rows)
            r = lax.rsqrt(jnp.mean(xv * xv, axis=-1, keepdims=True) + EPS)
            h_ref[...] = (xv * r * gv).astype(CDT)

    row = lambda i, f: (i, 0)
    mode = {"pipeline_mode": pl.Buffered(1)} if nf == 1 else {}
    return _run(
        body, rider, name="mlp_bwd", grid=(T // tm, nf),
        in_specs=[pl.BlockSpec((tm, D), row), pl.BlockSpec((tm, D), row), _const((1, D)),
                  pl.BlockSpec((tm, tf), lambda i, f: (i, f)),
                  pl.BlockSpec((D, tf), lambda i, f: (0, f), **mode), pl.BlockSpec((tf, D), lambda i, f: (f, 0), **mode)],
        out_specs=[pl.BlockSpec((tm, D), row), _const((1, D)), pl.BlockSpec((tm, D), row),
                   pl.BlockSpec((tm, tf), lambda i, f: (i, f))],
        out_shape=[jax.ShapeDtypeStruct((T, D), F32), jax.ShapeDtypeStruct((1, D), F32),
                   jax.ShapeDtypeStruct((T, D), CDT), jax.ShapeDtypeStruct((T, D_FF), CDT)],
        scratch_shapes=[pltpu.VMEM((tm, D), CDT), pltpu.VMEM((tm, D), F32)],
        sem=("arbitrary", "arbitrary"), args=(dy, x, g, a, w1, w2))


def tn_matmul(a, b, *, tm, tn, tk=2048, name, by_chip=False, rider=None):
    T, M = a.shape
    N = b.shape[1]
    tk = min(tk, T)
    nk = T // tk

    def body(a_ref, b_ref, o_ref):
        @pl.when(pl.program_id(2) == 0)
        def _():
            o_ref[...] = jnp.zeros_like(o_ref)

        o_ref[...] += _dot_tn(a_ref[...].astype(CDT), b_ref[...].astype(CDT))

    if by_chip:
        out_spec = pl.BlockSpec((None, tm, tn), lambda i, j, k: (j, i, 0))
        out_shape = jax.ShapeDtypeStruct((N // tn, M, tn), F32)
    else:
        out_spec = pl.BlockSpec((tm, tn), lambda i, j, k: (i, j))
        out_shape = jax.ShapeDtypeStruct((M, N), F32)
    res = _run(
        body, rider, name=name, grid=(M // tm, N // tn, nk),
        in_specs=[pl.BlockSpec((tk, tm), lambda i, j, k: (k, i)), pl.BlockSpec((tk, tn), lambda i, j, k: (k, j))],
        out_specs=[out_spec], out_shape=[out_shape], sem=("parallel", "parallel", "arbitrary"), args=(a, b))
    return res[0] if rider is None else (res[0][0], res[1])


def merge_bwd(dx1, attn, u3, rest, wa, wc, bc, wo, *, tm=512):
    T = dx1.shape[0]

    def body(dx_ref, at_ref, u_ref, ga_ref, gc_ref, wa_ref, wc_ref, bc_ref, wo_ref,
             mg_ref, dba_ref, dbc_ref, dat_ref, du_ref, dgate_ref, dgsum_ref, dbias_ref):
        @pl.when(pl.program_id(0) == 0)
        def _():
            dbias_ref[...] = jnp.zeros_like(dbias_ref)
            dgsum_ref[...] = jnp.zeros_like(dgsum_ref)

        br_a = _dot(at_ref[...], wa_ref[...])
        br_c = _dot(u_ref[...], wc_ref[...]) + bc_ref[...]
        sa = _sig(ga_ref[...].astype(F32))
        sc = _sig(gc_ref[...].astype(F32))
        mg_ref[...] = (sa * br_a + sc * br_c).astype(CDT)
        dm = _dot_nt(dx_ref[...].astype(CDT), wo_ref[...])
        dba = dm * sa
        dbc = dm * sc
        dga = dm * br_a * sa * (1.0 - sa)
        dgc = dm * br_c * sc * (1.0 - sc)
        dgate_ref[:, 0:D] = dga.astype(CDT)
        dgate_ref[:, D:2 * D] = dgc.astype(CDT)
        dgsum_ref[:, 0:D] += _colsum(dga)
        dgsum_ref[:, D:2 * D] += _colsum(dgc)
        dbias_ref[...] += _colsum(dbc)
        dba_b = dba.astype(CDT)
        dbc_b = dbc.astype(CDT)
        dba_ref[...] = dba_b
        dbc_ref[...] = dbc_b
        dat_ref[...] = _dot_nt(dba_b, wa_ref[...]).astype(CDT)
        du_ref[...] = _dot_nt(dbc_b, wc_ref[...])

    row = lambda i: (i, 0)
    return pl.pallas_call(
        body, name="merge_bwd", grid=(T // tm,),
        in_specs=[pl.BlockSpec((tm, D), row), pl.BlockSpec((tm, ATTN_W), row), pl.BlockSpec((tm, CONV_C), row),
                  pl.BlockSpec((tm, D), lambda i: (i, 1)), pl.BlockSpec((tm, D), lambda i: (i, 2)),
                  _const((ATTN_W, D), 1), _const((CONV_C, D), 1), _const((1, D)), _const((D, D), 1)],
        out_specs=[pl.BlockSpec((tm, D), row), pl.BlockSpec((tm, D), row), pl.BlockSpec((tm, D), row),
                   pl.BlockSpec((tm, ATTN_W), row), pl.BlockSpec((tm, CONV_C), row),
                   pl.BlockSpec((tm, 2 * D), row), _const((1, 2 * D)), _const((1, D))],
        out_shape=[jax.ShapeDtypeStruct((T, D), CDT), jax.ShapeDtypeStruct((T, D), CDT),
                   jax.ShapeDtypeStruct((T, D), CDT), jax.ShapeDtypeStruct((T, ATTN_W), CDT),
                   jax.ShapeDtypeStruct((T, CONV_C), F32), jax.ShapeDtypeStruct((T, 2 * D), CDT),
                   jax.ShapeDtypeStruct((1, 2 * D), F32), jax.ShapeDtypeStruct((1, D), F32)],
        compiler_params=_cp("arbitrary"),
    )(dx1, attn, u3, rest, rest, wa, wc, bc, wo)


def conv_bwd_ln(du3, u1, lg, lb, *, tm=512, rider=None):
    T = du3.shape[0]

    def body(du_ref, u1_ref, lg_ref, lb_ref, du1_ref, dlg_ref, dlb_ref, dcb_ref):
        @pl.when(pl.program_id(0) == 0)
        def _():
            dlg_ref[...] = jnp.zeros_like(dlg_ref)
            dlb_ref[...] = jnp.zeros_like(dlb_ref)
            dcb_ref[...] = jnp.zeros_like(dcb_ref)

        dlg = jnp.zeros((1, CONV_C), F32)
        dlb = jnp.zeros((1, CONV_C), F32)
        dcb = jnp.zeros((1, CONV_C), F32)
        for c in range(tm // CONV_RC):
            r0 = c * CONV_RC
            n, rstd, u2 = _layer_norm(u1_ref[r0:r0 + CONV_RC, :], lg_ref[...], lb_ref[...])
            s = _sig(u2)
            du2 = du_ref[r0:r0 + CONV_RC, :] * (s + u2 * s * (1.0 - s))
            dn = du2 * lg_ref[...]
            du1 = rstd * (dn - jnp.mean(dn, axis=-1, keepdims=True) - n * jnp.mean(dn * n, axis=-1, keepdims=True))
            du1_ref[r0:r0 + CONV_RC, :] = du1
            dlg = dlg + _colsum(du2 * n)
            dlb = dlb + _colsum(du2)
            dcb = dcb + _colsum(du1)
        dlg_ref[...] += dlg
        dlb_ref[...] += dlb
        dcb_ref[...] += dcb

    row = lambda i: (i, 0)
    vec = jax.ShapeDtypeStruct((1, CONV_C), F32)
    return _run(
        body, rider, name="conv_bwd_ln", grid=(T // tm,),
        in_specs=[pl.BlockSpec((tm, CONV_C), row), pl.BlockSpec((tm, CONV_C), row), _const((1, CONV_C)), _const((1, CONV_C))],
        out_specs=[pl.BlockSpec((tm, CONV_C), row), _const((1, CONV_C)), _const((1, CONV_C)), _const((1, CONV_C))],
        out_shape=[jax.ShapeDtypeStruct((T, CONV_C), F32), vec, vec, vec],
        sem=("arbitrary",), args=(du3, u1, lg, lb))


def conv_bwd_taps(du1, rest, cw, *, tm=512):
    T = du1.shape[0]
    nt = T // tm

    def body(d_ref, hd_ref, a_ref, b_ref, ha_ref, hb_ref, w_ref, dglu_ref, dgsum_ref, dw_ref, ext, dext, dwacc):
        i = pl.program_id(0)

        @pl.when(i == 0)
        def _():
            dwacc[...] = jnp.zeros_like(dwacc)
            dgsum_ref[...] = jnp.zeros_like(dgsum_ref)

        sums = [jnp.zeros((SUBLANES, CONV_C), F32), jnp.zeros((SUBLANES, CONV_C), F32)]
        _fill_u0(ext, a_ref, b_ref, ha_ref, hb_ref, i == 0)
        dext[0:SUBLANES, :] = jnp.zeros((SUBLANES, CONV_C), F32)
        dext[SUBLANES:SUBLANES + tm, :] = d_ref[...]
        hd = hd_ref[...]
        dext[SUBLANES + tm:, :] = jnp.where(i == nt - 1, jnp.zeros_like(hd), hd)
        qrow = lax.broadcasted_iota(jnp.int32, (CONV_RC + SUBLANES, CONV_C), 0)
        for c in range(tm // CONV_RC):
            r0 = c * CONV_RC
            du0 = _shifted_taps(dext, w_ref, r0 + SUBLANES, CONV_RC, [CONV_K - 1 - k for k in range(CONV_K)])
            for b in range(SUBLANES):
                taps = [(k, HALO - (CONV_K - 1) + k - b) for k in range(CONV_K) if (HALO - (CONV_K - 1) + k) % SUBLANES == b]
                if b == 0:
                    rows = CONV_RC
                    dsh = dext[r0 + SUBLANES:r0 + SUBLANES + rows, :]
                else:
                    rows = CONV_RC + SUBLANES
                    dsh = dext[r0 + SUBLANES - b:r0 + SUBLANES - b + rows, :]
                    dsh = jnp.where((qrow >= b) & (qrow < CONV_RC + b), dsh, 0.0)
                for k, o8 in taps:
                    prod = dsh * ext[r0 + o8:r0 + o8 + rows, :]
                    dwacc[8 * k:8 * k + 8, :] += jnp.sum(prod.reshape(rows // SUBLANES, SUBLANES, CONV_C), axis=0)
            av = a_ref[r0:r0 + CONV_RC, :].astype(F32)
            sb = _sig(b_ref[r0:r0 + CONV_RC, :].astype(F32))
            for half, dg in enumerate((du0 * sb, du0 * av * sb * (1.0 - sb))):
                dglu_ref[r0:r0 + CONV_RC, half * CONV_C:(half + 1) * CONV_C] = dg.astype(CDT)
                sums[half] = sums[half] + jnp.sum(dg.reshape(CONV_RC // SUBLANES, SUBLANES, CONV_C), axis=0)
        for half in range(2):
            dgsum_ref[:, half * CONV_C:(half + 1) * CONV_C] += _colsum(sums[half])

        @pl.when(i == nt - 1)
        def _():
            dw_ref[...] = jnp.zeros_like(dw_ref)
            for k in range(CONV_K):
                dw_ref[k:k + 1, :] = _colsum(dwacc[8 * k:8 * k + 8, :])

    row = lambda i: (i, 0)
    return pl.pallas_call(
        body, name="conv_bwd_taps", grid=(nt,),
        in_specs=[pl.BlockSpec((tm, CONV_C), row),
                  pl.BlockSpec((HALO, CONV_C), lambda i: (jnp.minimum((i + 1) * (tm // HALO), T // HALO - 1), 0)),
                  pl.BlockSpec((tm, CONV_C), row), pl.BlockSpec((tm, CONV_C), lambda i: (i, 1)),
                  _halo_before(tm, CONV_C, 0), _halo_before(tm, CONV_C, 1), _const((CONV_K, CONV_C))],
        out_specs=[pl.BlockSpec((tm, 2 * CONV_C), row), _const((1, 2 * CONV_C)), _const((HALO, CONV_C))],
        out_shape=[jax.ShapeDtypeStruct((T, 2 * CONV_C), CDT), jax.ShapeDtypeStruct((1, 2 * CONV_C), F32),
                   jax.ShapeDtypeStruct((HALO, CONV_C), F32)],
        scratch_shapes=[pltpu.VMEM((tm + HALO, CONV_C), F32), pltpu.VMEM((SUBLANES + tm + HALO, CONV_C), F32),
                        pltpu.VMEM((8 * CONV_K, CONV_C), F32)],
        compiler_params=_cp("arbitrary"),
    )(du1, du1, rest, rest, rest, rest, cw)


def attn_bwd(qkv, do, lse, sinks, *, tq=512, rider=None):
    T = qkv.shape[0]
    nb = tq // BLOCK

    def body(sink_ref, bias_ref, q_ref, kp_ref, kc_ref, vp_ref, vc_ref, do_ref, lse_ref,
             dq_ref, dqsum_ref, dkv_ref, spill_ref, dsink_ref, kext, vext, dkext, dvext, qs, dos):
        i = pl.program_id(0)

        @pl.when(i == 0)
        def _():
            dsink_ref[...] = jnp.zeros_like(dsink_ref)
            dqsum_ref[...] = jnp.zeros_like(dqsum_ref)

        kext[0:BLOCK, :] = kp_ref[...]
        kext[BLOCK:, :] = kc_ref[...]
        vext[0:BLOCK, :] = vp_ref[...]
        vext[BLOCK:, :] = vc_ref[...]
        dkext[...] = jnp.zeros_like(dkext)
        dvext[...] = jnp.zeros_like(dvext)
        lane_l = lax.broadcasted_iota(jnp.int32, (BLOCK, 128), 1)
        lane_k = lax.broadcasted_iota(jnp.int32, (2 * BLOCK, KV_W), 1)

        def blk(b, dsink):
            r0 = pl.multiple_of(b * BLOCK, BLOCK)
            first = jnp.logical_and(i == 0, b == 0).astype(jnp.int32)
            kc = kext[pl.ds(r0, 2 * BLOCK), :]
            vc = vext[pl.ds(r0, 2 * BLOCK), :]
            lse_t = lse_ref[pl.ds(r0, BLOCK), :]
            dk = jnp.zeros((2 * BLOCK, KV_W), F32)
            dv = jnp.zeros((2 * BLOCK, KV_W), F32)
            for g in range(N_KV):
                heads = range(GROUP * g, GROUP * (g + 1))
                _stack_heads(qs.at[g], q_ref, r0, g, SCALE)
                _stack_heads(dos.at[g], do_ref, r0, g)
                qv = qs[g]
                dov = dos[g]
                s = _dot_nt(qv, kc) + bias_ref[first, g]
                lse = jnp.concatenate(
                    [jnp.sum(jnp.where(lane_l == h, lse_t, 0.0), axis=-1, keepdims=True) for h in heads], axis=0)
                p = jnp.exp(s - lse)
                dp = _dot_nt(dov, vc)
                dd = jnp.sum(p * dp, axis=-1, keepdims=True)
                ds = (p * (dp - dd)).astype(CDT)
                keep = (lane_k >= HEAD_DIM) if g else (lane_k < HEAD_DIM)
                for c0, tile in _unstack_heads(dq_ref, _dot(ds, jnp.where(keep, kc, jnp.zeros_like(kc))) * SCALE, r0, g):
                    dqsum_ref[:, c0:c0 + 128] += _colsum(tile)
                dk = dk + _dot_tn(ds, qv)
                dv = dv + _dot_tn(p.astype(CDT), dov)
                wsink = jnp.exp(_per_head_column([sink_ref[h] for h in heads]) - lse) * dd
                for i_h, h in enumerate(heads):
                    part = jnp.sum(wsink[i_h * BLOCK:(i_h + 1) * BLOCK, :], axis=0, keepdims=True)
                    dsink = dsink - jnp.where(lane_l[0:1, :] == h, part, 0.0)
            dkext[pl.ds(r0, 2 * BLOCK), :] += dk
            dvext[pl.ds(r0, 2 * BLOCK), :] += dv
            return dsink

        dsink_ref[...] += lax.fori_loop(0, nb, blk, jnp.zeros((1, 128), F32))
        dkv_ref[:, 0:KV_W] = dkext[BLOCK:, :]
        dkv_ref[:, KV_W:2 * KV_W] = dvext[BLOCK:, :]
        spill_ref[:, 0:KV_W] = dkext[0:BLOCK, :]
        spill_ref[:, KV_W:2 * KV_W] = dvext[0:BLOCK, :]

    row = lambda i: (i, 0)
    return _run(
        body, rider, name="attn_bwd", grid=(T // tq,),
        in_specs=[pl.BlockSpec(memory_space=pltpu.SMEM), _const((2, N_KV, STACK, 2 * BLOCK), 1)] + _qkv_specs(tq)
        + [pl.BlockSpec((tq, ATTN_W), row), pl.BlockSpec((tq, 128), row)],
        out_specs=[pl.BlockSpec((tq, ATTN_W), row), _const((1, ATTN_W)), pl.BlockSpec((tq, 2 * KV_W), row),
                   pl.BlockSpec((BLOCK, 2 * KV_W), row), _const((1, 128))],
        out_shape=[jax.ShapeDtypeStruct((T, ATTN_W), CDT), jax.ShapeDtypeStruct((1, ATTN_W), F32),
                   jax.ShapeDtypeStruct((T, 2 * KV_W), F32),
                   jax.ShapeDtypeStruct((T // tq * BLOCK, 2 * KV_W), F32), jax.ShapeDtypeStruct((1, 128), F32)],
        scratch_shapes=[pltpu.VMEM((tq + BLOCK, KV_W), CDT), pltpu.VMEM((tq + BLOCK, KV_W), CDT),
                        pltpu.VMEM((tq + BLOCK, KV_W), F32), pltpu.VMEM((tq + BLOCK, KV_W), F32),
                        pltpu.VMEM((N_KV, STACK, 2 * HEAD_DIM), CDT), pltpu.VMEM((N_KV, STACK, 2 * HEAD_DIM), CDT)],
        sem=("arbitrary",), args=(sinks, _score_bias(), qkv, qkv, qkv, qkv, qkv, do, lse))


def inproj_bwd(dres, x, g, w, dq, dkv, spill, dglu, dgate, sums, *, tm=512):
    T = x.shape[0]
    nt = T // tm
    pieces = ((0, ATTN_W), (QKV_W, 2 * CONV_C), (QKV_W + 2 * CONV_C, 2 * D))

    def body(dr_ref, x_ref, g_ref, w_ref, dq_ref, dkv_ref, sp_ref, dglu_ref, dgate_ref, sq_ref, sglu_ref, sgate_ref,
             dx_ref, dp_ref, h_ref, dg_ref, db_ref):
        i = pl.program_id(0)

        @pl.when(i == 0)
        def _():
            dg_ref[...] = jnp.zeros_like(dg_ref)
            db_ref[:, ATTN_W:QKV_W] = jnp.zeros((1, QKV_W - ATTN_W), F32)
            for (c0, wd), s_ref in zip(pieces, (sq_ref, sglu_ref, sgate_ref)):
                db_ref[:, c0:c0 + wd] = s_ref[...]

        sp = sp_ref[...]
        sp = jnp.where(i == nt - 1, jnp.zeros_like(sp), sp)
        dkv = dkv_ref[...]
        db_ref[:, ATTN_W:QKV_W] += _colsum(dkv) + _colsum(sp)
        dp_ref[0:tm - BLOCK, ATTN_W:QKV_W] = dkv[0:tm - BLOCK, :].astype(CDT)
        dp_ref[tm - BLOCK:tm, ATTN_W:QKV_W] = (dkv[tm - BLOCK:tm, :] + sp).astype(CDT)
        for (c0, wd), ref in zip(pieces, (dq_ref, dglu_ref, dgate_ref)):
            dp_ref[:, c0:c0 + wd] = ref[...]
        dh = _dot(dp_ref[...], w_ref[...])
        xv = x_ref[...]
        gv = g_ref[...]
        dxn, dg_rows = _rms_bwd(xv, gv, dh)
        dx_ref[...] = dr_ref[...] + dxn
        dg_ref[...] += _colsum(dg_rows)
        r = lax.rsqrt(jnp.mean(xv * xv, axis=-1, keepdims=True) + EPS)
        h_ref[...] = (xv * r * gv).astype(CDT)

    row = lambda i: (i, 0)
    return pl.pallas_call(
        body, name="inproj_bwd", grid=(nt,),
        in_specs=[pl.BlockSpec((tm, D), row), pl.BlockSpec((tm, D), row), _const((1, D)), _const((IN_W, D), 1),
                  pl.BlockSpec((tm, ATTN_W), row), pl.BlockSpec((tm, 2 * KV_W), row),
                  pl.BlockSpec((BLOCK, 2 * KV_W), lambda i: (jnp.minimum(i + 1, nt - 1), 0)),
                  pl.BlockSpec((tm, 2 * CONV_C), row), pl.BlockSpec((tm, 2 * D), row)]
        + [_const((1, wd)) for _, wd in pieces],
        out_specs=[pl.BlockSpec((tm, D), row), pl.BlockSpec((tm, IN_W), row), pl.BlockSpec((tm, D), row),
                   _const((1, D)), _const((1, IN_W))],
        out_shape=[jax.ShapeDtypeStruct((T, D), F32), jax.ShapeDtypeStruct((T, IN_W), CDT),
                   jax.ShapeDtypeStruct((T, D), CDT), jax.ShapeDtypeStruct((1, D), F32),
                   jax.ShapeDtypeStruct((1, IN_W), F32)],
        compiler_params=_cp("arbitrary"),
    )(dres, x, g, w, dq, dkv, spill, dglu, dgate, *sums)


ATTN_TILE = 256
MATRICES = ("w_in", "w_attn_proj", "w_conv_proj", "w_out", "w_mlp1", "w_mlp2")
SMALL = ("mix_norm_g", "b_in", "sinks", "conv_w", "conv_b", "conv_ln_g", "conv_ln_b", "b_conv_proj", "mlp_norm_g")


def forward_backward(x, tgt, hooks):
    def call(fn, kernel, l, *args, **kw):
        rider = hooks.rider(kernel, l)
        if rider is None:
            return fn(*args, **kw)
        outs, landed = fn(*args, rider=rider, **kw)
        hooks.landed(kernel, l, landed)
        return outs

    vec = hooks.vec
    saved = []
    for l in range(DEPTH):
        qkv, rest = call(rms_inproj, "rms_inproj", l, x, vec("mix_norm_g", l), hooks.w_in(l), vec("b_in", l))
        m = hooks.mats(l)
        attn, lse = attn_fwd(qkv, hooks.sinks(l), tq=ATTN_TILE)
        u3, u1 = conv_fwd(rest, hooks.taps(l), vec("conv_b", l), vec("conv_ln_g", l), vec("conv_ln_b", l))
        x1 = merge_out(x, attn, u3, rest, m["w_attn_proj"], m["w_conv_proj"], vec("b_conv_proj", l), m["w_out"])
        x2, a = call(mlp_fwd, "mlp_fwd", l, x1, vec("mlp_norm_g", l), m["w_mlp1"], m["w_mlp2"])
        saved.append((x, qkv, rest, attn, lse, u3, u1, x1, a))
        x = x2
    dx, dgf, loss = loss_head(x, hooks.final_g, tgt)
    small = {n: [None] * DEPTH for n in SMALL}
    small["final_norm_g"] = dgf
    for l in reversed(range(DEPTH)):
        x0, qkv, rest, attn, lse, u3, u1, x1, a = saved[l]
        m = hooks.mats(l)
        dx1, dg2, h2, dpre = call(mlp_bwd, "mlp_bwd", l, dx, x1, vec("mlp_norm_g", l), a, m["w_mlp1"], m["w_mlp2"])
        small["mlp_norm_g"][l] = dg2
        group = {}
        group["w_mlp1"] = call(tn_matmul, "tn_mlp1", l, h2, dpre, tm=1024, tn=1024, tk=4096, name="tn_mlp1", by_chip=True)
        group["w_mlp2"] = call(tn_matmul, "tn_mlp2", l, a, dx, tm=1024, tn=1024, name="tn_mlp2")
        merged, dba, dbc, dattn, du3, dgate, dgate_sum, dbcp = merge_bwd(
            dx1, attn, u3, rest, m["w_attn_proj"], m["w_conv_proj"], vec("b_conv_proj", l), m["w_out"])
        small["b_conv_proj"][l] = dbcp
        group["w_out"] = tn_matmul(merged, dx1, tm=1024, tn=1024, name="tn_out")
        group["w_attn_proj"] = tn_matmul(attn, dba, tm=512, tn=256, tk=4096, name="tn_attn_proj", by_chip=True)
        group["w_conv_proj"] = tn_matmul(u3, dbc, tm=512, tn=256, tk=4096, name="tn_conv_proj", by_chip=True)
        hooks.grads(l, "A", group)
        du1, dlg, dlb, dcb = call(conv_bwd_ln, "conv_bwd_ln", l, du3, u1, vec("conv_ln_g", l), vec("conv_ln_b", l))
        small["conv_ln_g"][l], small["conv_ln_b"][l], small["conv_b"][l] = dlg, dlb, dcb
        dglu, dglu_sum, dcw = conv_bwd_taps(du1, rest, hooks.taps(l))
        small["conv_w"][l] = dcw
        dq, dq_sum, dkv, spill, dsink = call(attn_bwd, "attn_bwd", l, qkv, dattn, lse, hooks.sinks(l), tq=ATTN_TILE)
        small["sinks"][l] = dsink
        dx, dproj, h, dg, db = inproj_bwd(dx1, x0, vec("mix_norm_g", l), hooks.w_in(l), dq, dkv, spill, dglu, dgate,
                                          (dq_sum, dglu_sum, dgate_sum), tm=ATTN_TILE)
        small["mix_norm_g"][l], small["b_in"][l] = dg, db
        hooks.grads(l, "B", {"w_in": call(tn_matmul, "tn_in", l, dproj, h, tm=768, tn=1024, tk=4096, name="tn_in")})
    return loss, dx, small


class _LocalHooks:
    def __init__(self, p):
        self.p = p
        self.final_g = p["final_norm_g"]
        self.got = {n: [None] * DEPTH for n in MATRICES}

    def w_in(self, l):
        return self.p["w_in"][l].T

    def mats(self, l):
        return {n: self.p[n][l] for n in MATRICES}

    def vec(self, n, l):
        return self.p[n][l]

    def sinks(self, l):
        return self.p["sinks"][l]

    def taps(self, l):
        return self.p["conv_w"][l]

    def rider(self, kernel, l):
        return None

    def grads(self, l, group, g):
        for n, v in g.items():
            if v.ndim == 3:
                v = v.transpose(1, 0, 2).reshape(v.shape[1], -1)
            self.got[n][l] = v.T if n == "w_in" else v


def local_grads(x, tgt, p):
    hooks = _LocalHooks(p)
    loss, dx, small = forward_backward(x, tgt, hooks)
    small["conv_w"] = [g[0:CONV_K] for g in small["conv_w"]]
    small["sinks"] = [g[0, 0:N_Q] for g in small["sinks"]]
    return loss, dx, {**small, **hooks.got}


MESH = pl.DeviceIdType.MESH
N_CHIPS = 4
N_DEV = 8
FLAT_W = 1024
FLAT_PARTS = (("w_in", 960), ("w_attn_proj", 128), ("w_conv_proj", 128), ("w_out", 256), ("w_mlp1", 1024), ("w_mlp2", 1024))
FLAT_ROWS = sum(r for _, r in FLAT_PARTS)
W_IN_ROWS = FLAT_PARTS[0][1]
GROUP_A = (("w_mlp1", 1024), ("w_mlp2", 1024), ("w_out", 256), ("w_attn_proj", 128), ("w_conv_proj", 128))
COL_SHARDED = ("w_in", "w_attn_proj", "w_conv_proj", "w_mlp1")
FULL_SHAPES = {"w_in": (D, IN_W), "w_attn_proj": (ATTN_W, D), "w_conv_proj": (CONV_C, D), "w_out": (D, D),
               "w_mlp1": (D, D_FF), "w_mlp2": (D_FF, D)}


def _place():
    x, y, c = lax.axis_index("x"), lax.axis_index("y"), lax.axis_index("c")
    return x, y, c, 2 * x + y


def _peer_chips(x, y, j):
    return [((x, 1 - y), j ^ 1), ((1 - x, y), j ^ 2), ((1 - x, 1 - y), j ^ 3)]


def _remote(src, dst, sems, k, n, to):
    return pltpu.make_async_remote_copy(src_ref=src, dst_ref=dst, send_sem=sems.at[k], recv_sem=sems.at[n + k],
                                        device_id=to, device_id_type=MESH)


def _half(c, rows):
    h = rows // 2
    return pl.ds(pl.multiple_of(c * h, 16), h)


def gather_rider(wsh):
    R = wsh.shape[0]

    def plan(rins, routs, sems):
        (w_ref,), (out_ref,) = rins, routs
        x, y, c, j = _place()
        peers = _peer_chips(x, y, j)
        mine, other = _half(c, R), _half(1 - c, R)
        sent = [_remote(w_ref.at[mine], out_ref.at[j, mine], sems, k, 6, (*chip, c)) for k, (chip, _) in enumerate(peers)]
        landed = [_remote(w_ref.at[mine], out_ref.at[pj, mine], sems, k, 6, (x, y, c)) for k, (_, pj) in enumerate(peers)]
        passed = [_remote(out_ref.at[pj, mine], out_ref.at[pj, mine], sems, 3 + k, 6, (x, y, 1 - c))
                  for k, (_, pj) in enumerate(peers)]
        handed = [_remote(w_ref.at[mine], out_ref.at[pj, other], sems, 3 + k, 6, (x, y, c)) for k, (_, pj) in enumerate(peers)]
        return sent, landed, passed, handed

    def start(rins, routs, sems):
        for cp in plan(rins, routs, sems)[0]:
            cp.start()

    def finish(rins, routs, sems):
        sent, landed, passed, handed = plan(rins, routs, sems)
        for k in range(3):
            landed[k].wait_recv()
            passed[k].start()
        for cp in handed:
            cp.wait_recv()
        for cp in sent + passed:
            cp.wait_send()

    return Rider((wsh,), (jax.ShapeDtypeStruct((N_CHIPS,) + wsh.shape, wsh.dtype),), 12, start, finish)


def swap_rider(g):
    R = g.shape[1]

    def plan(rins, routs, sems):
        (g_ref,), (got_ref,) = rins, routs
        x, y, c, _ = _place()
        return _remote(g_ref.at[:, _half(1 - c, R), :], got_ref, sems, 0, 1, (x, y, 1 - c))

    def start(rins, routs, sems):
        plan(rins, routs, sems).start()

    def finish(rins, routs, sems):
        plan(rins, routs, sems).wait()

    return Rider((g,), (jax.ShapeDtypeStruct((N_CHIPS, R // 2, FLAT_W), g.dtype),), 2, start, finish)


def exchange_rider(pb):
    def plan(rins, routs, sems):
        (pb_ref,), (got_ref,) = rins, routs
        x, y, c, j = _place()
        peers = _peer_chips(x, y, j)
        sent = [_remote(pb_ref.at[pj], got_ref.at[j], sems, k, 3, (*chip, c)) for k, (chip, pj) in enumerate(peers)]
        landed = [_remote(pb_ref.at[pj], got_ref.at[pj], sems, k, 3, (x, y, c)) for k, (_, pj) in enumerate(peers)]
        return sent, landed

    def start(rins, routs, sems):
        for cp in plan(rins, routs, sems)[0]:
            cp.start()

    def finish(rins, routs, sems):
        sent, landed = plan(rins, routs, sems)
        for cp in landed:
            cp.wait_recv()
        for cp in sent:
            cp.wait_send()

    return Rider((pb,), (jax.ShapeDtypeStruct(pb.shape, pb.dtype),), 6, start, finish)


def share_rider(tot):
    def plan(rins, routs, sems):
        (t_ref,), (got_ref,) = rins, routs
        x, y, c, _ = _place()
        return _remote(t_ref, got_ref, sems, 0, 1, (x, y, 1 - c))

    def start(rins, routs, sems):
        plan(rins, routs, sems).start()

    def finish(rins, routs, sems):
        plan(rins, routs, sems).wait()

    return Rider((tot,), (jax.ShapeDtypeStruct(tot.shape, tot.dtype),), 2, start, finish)


def pair_sum(g, got):
    nj, R, W = g.shape
    h = R // 2
    tile = h // 2

    def body(g_ref, got_ref, pb_ref, own_ref):
        v = g_ref[...] + got_ref[...]
        pb_ref[...] = v.astype(pb_ref.dtype)

        @pl.when(pl.program_id(1) == _place()[3])
        def _():
            own_ref[...] = v

    return pl.pallas_call(
        body, name="pair_sum", grid=(h // tile, nj),
        in_specs=[pl.BlockSpec((None, tile, W), lambda r, j: (j, lax.axis_index("c") * (h // tile) + r, 0)),
                  pl.BlockSpec((None, tile, W), lambda r, j: (j, r, 0))],
        out_specs=[pl.BlockSpec((None, tile, W), lambda r, j: (j, r, 0)), pl.BlockSpec((tile, W), lambda r, j: (r, 0))],
        out_shape=[jax.ShapeDtypeStruct((nj, h, W), CDT), jax.ShapeDtypeStruct((h, W), F32)],
        compiler_params=_cp("arbitrary", "arbitrary"),
    )(g, got)


def total_sum(own, got):
    R, W = own.shape
    tile = R // 2

    def body(own_ref, a_ref, b_ref, c_ref, o_ref):
        o_ref[...] = ((own_ref[...] + a_ref[...].astype(F32)) + b_ref[...].astype(F32)) + c_ref[...].astype(F32)

    def slab(k):
        return pl.BlockSpec((None, tile, W), lambda r: (_place()[3] ^ (k + 1), r, 0))

    return pl.pallas_call(
        body, name="total_sum", grid=(R // tile,),
        in_specs=[pl.BlockSpec((tile, W), lambda r: (r, 0)), slab(0), slab(1), slab(2)],
        out_specs=pl.BlockSpec((tile, W), lambda r: (r, 0)),
        out_shape=jax.ShapeDtypeStruct((R, W), F32),
        compiler_params=_cp("arbitrary"),
    )(own, got, got, got)


def _all_peers(x, y, c):
    return [(x ^ (r >> 2), y ^ ((r >> 1) & 1), c ^ (r & 1)) for r in range(1, N_DEV)]


ROW_ITEMS = (("mix_norm_g", D), ("b_in", IN_W), ("sinks", N_Q), ("conv_b", CONV_C), ("conv_ln_g", CONV_C),
             ("conv_ln_b", CONV_C), ("b_conv_proj", D), ("mlp_norm_g", D))
TAPS_ROW = 16
TAPS_ROWS = 32
LAYER_ROWS = TAPS_ROW + TAPS_ROWS
FINAL_ROW = DEPTH * LAYER_ROWS
SMALL_ROWS = FINAL_ROW + SUBLANES


def _row_chunks():
    out, r = {}, 0
    for n, width in ROW_ITEMS:
        out[n] = [(r + i, FLAT_W * i, min(FLAT_W, width - FLAT_W * i)) for i in range(-(-width // FLAT_W))]
        r += len(out[n])
    assert r <= TAPS_ROW
    return out


def sum_small(gsm):
    chunks = _row_chunks()
    ins = []
    for l in range(DEPTH):
        ins += [gsm[n][l] for n, _ in ROW_ITEMS] + [gsm["conv_w"][l]]
    ins.append(gsm["final_norm_g"])
    n_in = len(ins)

    def body(*refs):
        in_refs, o_ref, buf, send_sems, recv_sems = refs[:n_in], refs[n_in], refs[n_in + 1], refs[n_in + 2], refs[n_in + 3]
        x, y, c, _ = _place()
        me = 4 * x + 2 * y + c
        mine = buf.at[me]
        mine[...] = jnp.zeros((SMALL_ROWS, FLAT_W), F32)
        k = 0
        for l in range(DEPTH):
            for n, _ in ROW_ITEMS:
                for r, c0, wd in chunks[n]:
                    mine[l * LAYER_ROWS + r:l * LAYER_ROWS + r + 1, 0:wd] = in_refs[k][:, c0:c0 + wd]
                k += 1
            mine[l * LAYER_ROWS + TAPS_ROW:(l + 1) * LAYER_ROWS, 0:CONV_C] = in_refs[k][...]
            k += 1
        mine[FINAL_ROW:FINAL_ROW + 1, :] = in_refs[k][...]
        peers = _all_peers(x, y, c)
        sends = [pltpu.make_async_remote_copy(src_ref=mine, dst_ref=mine, send_sem=send_sems.at[r], recv_sem=recv_sems.at[r],
                                              device_id=to, device_id_type=MESH) for r, to in enumerate(peers)]
        for cp in sends:
            cp.start()
        for r in range(N_DEV - 1):
            pltpu.make_async_remote_copy(src_ref=mine, dst_ref=buf.at[me ^ (r + 1)], send_sem=send_sems.at[r],
                                         recv_sem=recv_sems.at[r], device_id=(x, y, c), device_id_type=MESH).wait_recv()
        for cp in sends:
            cp.wait_send()
        acc = buf[0]
        for d in range(1, N_DEV):
            acc = acc + buf[d]
        o_ref[...] = acc

    vm = pl.BlockSpec(memory_space=pltpu.VMEM)
    return pl.pallas_call(
        body, name="sum_small", out_shape=jax.ShapeDtypeStruct((SMALL_ROWS, FLAT_W), F32),
        in_specs=[vm] * n_in, out_specs=vm,
        scratch_shapes=[pltpu.VMEM((N_DEV, SMALL_ROWS, FLAT_W), F32), pltpu.SemaphoreType.DMA((N_DEV - 1,)),
                        pltpu.SemaphoreType.DMA((N_DEV - 1,))],
    )(*ins)


def gather_taps(taps):
    shard = taps.shape[2]

    def body(t_ref, o_ref, buf, send_sems, recv_sems):
        x, y, c, j = _place()
        peers = _peer_chips(x, y, j)
        buf[j] = t_ref[...]
        sends = [pltpu.make_async_remote_copy(src_ref=t_ref, dst_ref=buf.at[j], send_sem=send_sems.at[k],
                                              recv_sem=recv_sems.at[k], device_id=(*chip, c), device_id_type=MESH)
                 for k, (chip, _) in enumerate(peers)]
        for cp in sends:
            cp.start()
        for k, (_, pj) in enumerate(peers):
            pltpu.make_async_remote_copy(src_ref=t_ref, dst_ref=buf.at[pj], send_sem=send_sems.at[k],
                                         recv_sem=recv_sems.at[k], device_id=(x, y, c), device_id_type=MESH).wait_recv()
        for cp in sends:
            cp.wait_send()
        for jj in range(N_CHIPS):
            o_ref[:, :, jj * shard:(jj + 1) * shard] = buf[jj]

    vm = pl.BlockSpec(memory_space=pltpu.VMEM)
    return pl.pallas_call(
        body, name="gather_taps", out_shape=jax.ShapeDtypeStruct(taps.shape[:2] + (N_CHIPS * shard,), taps.dtype),
        in_specs=[vm], out_specs=vm,
        scratch_shapes=[pltpu.VMEM((N_CHIPS,) + taps.shape, taps.dtype), pltpu.SemaphoreType.DMA((3,)),
                        pltpu.SemaphoreType.DMA((3,))],
    )(taps)


def _adam_math(w, g, m, v):
    nm = ADAM_B1 * m + (1.0 - ADAM_B1) * g
    nv = ADAM_B2 * v + (1.0 - ADAM_B2) * jnp.square(g)
    m_hat = nm / (1.0 - ADAM_B1 ** ADAM_STEP)
    v_hat = nv / (1.0 - ADAM_B2 ** ADAM_STEP)
    return -ADAM_LR * (m_hat / (jnp.sqrt(v_hat) + ADAM_EPS) + ADAM_WD * w), nm, nv


def adamw(w, g, m, v, *, name):
    L, R, C = w.shape
    tr = next(t for t in (512, 480, 256, 128) if R % t == 0)

    def body(w_ref, g_ref, m_ref, v_ref, d_ref, nm_ref, nv_ref):
        d_ref[...], nm_ref[...], nv_ref[...] = _adam_math(w_ref[...], g_ref[...], m_ref[...], v_ref[...])

    spec = pl.BlockSpec((None, tr, C), lambda l, i: (l, i, 0))
    out = jax.ShapeDtypeStruct((L, R, C), F32)
    return pl.pallas_call(
        body, name=name, grid=(L, R // tr), in_specs=[spec] * 4, out_specs=[spec] * 3, out_shape=[out] * 3,
        compiler_params=_cp("parallel", "parallel"),
    )(w, g, m, v)


def adamw_small(packed, w, m, v):
    chunks = _row_chunks()
    names = SMALL + ("final_norm_g",)
    as_2d = lambda a: a.reshape(1, -1) if a.ndim == 1 else a
    ins = [as_2d(t[n]) for n in names for t in (w, m, v)]
    shapes = [jax.ShapeDtypeStruct(as_2d(w[n]).shape, F32) for n in names for _ in range(4)]
    n_in = len(ins)

    def body(p_ref, *refs):
        in_refs, out_refs = refs[:n_in], refs[n_in:]
        chip = _place()[3]
        for i, n in enumerate(names):
            w_ref, m_ref, v_ref = in_refs[3 * i:3 * i + 3]
            outs = out_refs[4 * i:4 * i + 4]

            def step(at, g):
                res = (g,) + _adam_math(w_ref[at], g, m_ref[at], v_ref[at])
                for o_ref, val in zip(outs, res):
                    o_ref[at] = val

            if n == "final_norm_g":
                step((slice(None), slice(None)), p_ref[FINAL_ROW:FINAL_ROW + 1, :])
                continue
            for l in range(DEPTH):
                if n == "conv_w":
                    r0 = l * LAYER_ROWS + TAPS_ROW
                    shard = CONV_C // N_CHIPS
                    g = jnp.zeros((CONV_K, shard), F32)
                    for j in range(N_CHIPS):
                        g = jnp.where(chip == j, p_ref[r0:r0 + CONV_K, j * shard:(j + 1) * shard], g)
                    step((l,), g)
                else:
                    for r, c0, wd in chunks[n]:
                        step((slice(l, l + 1), slice(c0, c0 + wd)),
                             p_ref[l * LAYER_ROWS + r:l * LAYER_ROWS + r + 1, 0:wd])

    vm = pl.BlockSpec(memory_space=pltpu.VMEM)
    res = pl.pallas_call(
        body, name="adamw_small", out_shape=shapes,
        in_specs=[vm] + [vm] * n_in, out_specs=[vm] * len(shapes),
    )(packed, *ins)
    dicts = ({}, {}, {}, {})
    for i, n in enumerate(names):
        for d, val in zip(dicts, res[4 * i:4 * i + 4]):
            d[n] = val.reshape(w[n].shape)
    return dicts


def _flat_rows(name, shard):
    return shard.T if name == "w_in" else shard.reshape(-1, FLAT_W)


def _full_matrix(slabs, name):
    K, N = FULL_SHAPES[name]
    if name == "w_in":
        return slabs.reshape(N, K)
    if name in COL_SHARDED:
        return slabs.reshape(N_CHIPS, K, N // N_CHIPS).transpose(1, 0, 2).reshape(K, N)
    return slabs.reshape(K, N)


def _first_row(parts, name):
    r = 0
    for n, rows in parts:
        if n == name:
            return r, rows
        r += rows
    raise KeyError(name)


class _Exchange:
    CARRIERS = {
        ("conv_bwd_ln", 1): ((1, "A"), "swap"), ("attn_bwd", 1): ((1, "A"), "exchange"), ("tn_in", 1): ((1, "A"), "share"),
        ("mlp_bwd", 0): ((1, "B"), "swap"), ("tn_mlp1", 0): ((1, "B"), "exchange"), ("tn_mlp2", 0): ((1, "B"), "share"),
        ("conv_bwd_ln", 0): ((0, "A"), "swap"), ("attn_bwd", 0): ((0, "A"), "exchange"), ("tn_in", 0): ((0, "A"), "share"),
    }

    def __init__(self, w, ci, chip):
        self.w, self.ci, self.chip = w, ci, chip
        self.wsh = [jnp.concatenate([_flat_rows(n, w[n][l]) for n, _ in FLAT_PARTS], axis=0).astype(CDT)
                    for l in range(DEPTH)]
        self.final_g = w["final_norm_g"].reshape(1, D)
        self.slabs = {}
        self.full = {}
        self.units = {}
        self.reduced = {}
        self._landed_weights(0, 0, W_IN_ROWS, _run_alone(gather_rider(self.wsh[0][:W_IN_ROWS]), "gather_w_in")[0])
        self.all_taps = gather_taps(w["conv_w"])

    def _landed_weights(self, l, r0, r1, buf):
        own = self.wsh[l][r0:r1]
        self.slabs.setdefault(l, []).append((r0, lax.dynamic_update_slice(buf, own[None], (self.chip, 0, 0))))

    def _matrix(self, l, name):
        if (l, name) not in self.full:
            r, rows = _first_row(FLAT_PARTS, name)
            r0, buf = next((r0, buf) for r0, buf in self.slabs[l] if r0 <= r < r0 + buf.shape[1])
            self.full[(l, name)] = _full_matrix(buf[:, r - r0:r - r0 + rows], name)
        return self.full[(l, name)]

    def w_in(self, l):
        return self._matrix(l, "w_in")

    def mats(self, l):
        return {n: self._matrix(l, n) for n in MATRICES if n != "w_in"}

    def vec(self, n, l):
        return self.w[n][l].reshape(1, -1)

    def sinks(self, l):
        return self.w["sinks"][l]

    def taps(self, l):
        return self.all_taps[l]

    def rider(self, kernel, l):
        if (kernel, l) == ("rms_inproj", 0):
            return gather_rider(self.wsh[0][W_IN_ROWS:])
        if (kernel, l) == ("mlp_fwd", 0):
            return gather_rider(self.wsh[1])
        if (kernel, l) in self.CARRIERS:
            return self._stage(*self.CARRIERS[(kernel, l)])
        return None

    def landed(self, kernel, l, bufs):
        if (kernel, l) == ("rms_inproj", 0):
            self._landed_weights(0, W_IN_ROWS, FLAT_ROWS, bufs[0])
        elif (kernel, l) == ("mlp_fwd", 0):
            self._landed_weights(1, 0, FLAT_ROWS, bufs[0])
        else:
            self._stage_landed(*self.CARRIERS[(kernel, l)], bufs[0])

    def grads(self, l, group, g):
        if group == "A":
            flat = jnp.concatenate([g[n].reshape(N_CHIPS, rows, FLAT_W) for n, rows in GROUP_A], axis=1)
        else:
            flat = g["w_in"].reshape(N_CHIPS, W_IN_ROWS, FLAT_W)
        self.units[(l, group)] = {"g": flat}

    def _stage(self, key, stage):
        u = self.units[key]
        if stage == "swap":
            return swap_rider(u["g"])
        if stage == "exchange":
            u["pb"], u["own"] = pair_sum(u["g"], u["swap"])
            return exchange_rider(u["pb"])
        u["tot"] = total_sum(u["own"], u["exchange"])
        return share_rider(u["tot"])

    def _stage_landed(self, key, stage, buf):
        u = self.units[key]
        u[stage] = buf
        if stage == "share":
            tot = u["tot"]
            self.reduced[key] = jnp.where(self.ci == 0, jnp.concatenate([tot, buf]), jnp.concatenate([buf, tot]))

    def finish(self):
        key = (0, "B")
        for stage in ("swap", "exchange", "share"):
            self._stage_landed(key, stage, _run_alone(self._stage(key, stage), stage + "_last")[0])
        out = {}
        for n in MATRICES:
            per_layer = []
            for l in range(DEPTH):
                if n == "w_in":
                    per_layer.append(self.reduced[(l, "B")].T)
                    continue
                r, rows = _first_row(GROUP_A, n)
                per_layer.append(self.reduced[(l, "A")][r:r + rows].reshape(self.w[n].shape[1:]))
            out[n] = jnp.stack(per_layer)
        return out


WEIGHTS = ("mix_norm_g", "w_in", "b_in", "sinks", "conv_w", "conv_b", "conv_ln_g", "conv_ln_b", "w_attn_proj",
           "w_conv_proj", "b_conv_proj", "w_out", "mlp_norm_g", "w_mlp1", "w_mlp2", "final_norm_g")


def kernel(x, mix_norm_g, w_in, b_in, sinks, conv_w, conv_b, conv_ln_g, conv_ln_b, w_attn_proj, w_conv_proj, b_conv_proj, w_out, mlp_norm_g, w_mlp1, w_mlp2, final_norm_g, loss_target, m_mix_norm_g, m_w_in, m_b_in, m_sinks, m_conv_w, m_conv_b, m_conv_ln_g, m_conv_ln_b, m_w_attn_proj, m_w_conv_proj, m_b_conv_proj, m_w_out, m_mlp_norm_g, m_w_mlp1, m_w_mlp2, m_final_norm_g, v_mix_norm_g, v_w_in, v_b_in, v_sinks, v_conv_w, v_conv_b, v_conv_ln_g, v_conv_ln_b, v_w_attn_proj, v_w_conv_proj, v_b_conv_proj, v_w_out, v_mlp_norm_g, v_w_mlp1, v_w_mlp2, v_final_norm_g):
    w = dict(zip(WEIGHTS, (mix_norm_g, w_in, b_in, sinks, conv_w, conv_b, conv_ln_g, conv_ln_b, w_attn_proj, w_conv_proj,
                           b_conv_proj, w_out, mlp_norm_g, w_mlp1, w_mlp2, final_norm_g)))
    m = dict(zip(WEIGHTS, (m_mix_norm_g, m_w_in, m_b_in, m_sinks, m_conv_w, m_conv_b, m_conv_ln_g, m_conv_ln_b, m_w_attn_proj,
                           m_w_conv_proj, m_b_conv_proj, m_w_out, m_mlp_norm_g, m_w_mlp1, m_w_mlp2, m_final_norm_g)))
    v = dict(zip(WEIGHTS, (v_mix_norm_g, v_w_in, v_b_in, v_sinks, v_conv_w, v_conv_b, v_conv_ln_g, v_conv_ln_b, v_w_attn_proj,
                           v_w_conv_proj, v_b_conv_proj, v_w_out, v_mlp_norm_g, v_w_mlp1, v_w_mlp2, v_final_norm_g)))
    xi, yi, ci = lax.axis_index("x"), lax.axis_index("y"), lax.axis_index("c")
    chip = 2 * xi + yi

    hooks = _Exchange(w, ci, chip)
    loss, dx, gsm = forward_backward(x[0], loss_target[0], hooks)
    loss = lax.psum(loss[0, 0], ("x", "y", "c"))
    grads = hooks.finish()

    gsmall, delta, new_m, new_v = adamw_small(sum_small(gsm), w, m, v)
    grads.update(gsmall)
    for n in MATRICES:
        t = (lambda a: jnp.swapaxes(a, 1, 2)) if n == "w_in" else (lambda a: a)
        delta[n], new_m[n], new_v[n] = map(t, adamw(t(w[n]), t(grads[n]), t(m[n]), t(v[n]), name="adamw_" + n))

    return (loss, dx[None], *[grads[n] for n in WEIGHTS], *[delta[n] for n in WEIGHTS],
            *[new_m[n] for n in WEIGHTS], *[new_v[n] for n in WEIGHTS])
```

```python
import functools
import math
from typing import Callable, NamedTuple

import jax
import jax.numpy as jnp
import numpy as np
from jax import lax
from jax.experimental import pallas as pl
from jax.experimental.pallas import tpu as pltpu

F32 = jnp.float32
CDT = jnp.bfloat16

D = 1024
DEPTH = 2
N_Q = 8
HEAD_DIM = 64
ATTN_W = 512
KV_W = 128
BLOCK = 128
CONV_C = 512
CONV_K = 31
D_FF = 4096
IN_W = 3840
QKV_W = ATTN_W + 2 * KV_W
REST_W = IN_W - QKV_W
EPS = 1e-6
NEG = -1e30
SCALE = 1.0 / math.sqrt(HEAD_DIM)
SLOPES = [float(2.0 ** (-8.0 * (h + 1) / N_Q)) for h in range(N_Q)]
SUBLANES = 8
HALO = 32

ADAM_LR = 0.001
ADAM_B1 = 0.9
ADAM_B2 = 0.999
ADAM_EPS = 1e-08
ADAM_WD = 0.01
ADAM_STEP = 10

VMEM_LIMIT = 56 * 1024 * 1024


def _cp(*sem):
    return pltpu.CompilerParams(dimension_semantics=sem, vmem_limit_bytes=VMEM_LIMIT)


def _dot(a, b):
    return jnp.dot(a, b, preferred_element_type=F32)


def _dot_nt(a, b):
    return lax.dot_general(a, b, (((1,), (1,)), ((), ())), preferred_element_type=F32)


def _dot_tn(a, b):
    return lax.dot_general(a, b, (((0,), (0,)), ((), ())), preferred_element_type=F32)


def _sig(x):
    return 1.0 / (1.0 + jnp.exp(-x))


def _colsum(v):
    return jnp.sum(v, axis=0, keepdims=True)


def _const(shape, buffers=None):
    mode = {} if buffers is None else {"pipeline_mode": pl.Buffered(buffers)}
    return pl.BlockSpec(shape, lambda *_: (0,) * len(shape), **mode)


class Rider(NamedTuple):
    ins: tuple
    outs: tuple
    n_sems: int
    start: Callable
    finish: Callable


def _any():
    return pl.BlockSpec(memory_space=pl.ANY)


def _run(body, rider, *, name, grid, in_specs, out_specs, out_shape, args, sem, scratch_shapes=()):
    if rider is None:
        return pl.pallas_call(body, name=name, grid=grid, in_specs=list(in_specs), out_specs=list(out_specs),
                              out_shape=list(out_shape), scratch_shapes=list(scratch_shapes),
                              compiler_params=_cp(*sem))(*args)
    n_in, n_out, n_sc = len(in_specs), len(out_specs), len(scratch_shapes)
    r_in, r_out = len(rider.ins), len(rider.outs)

    def riding(*refs):
        ins, rins = refs[:n_in], refs[n_in:n_in + r_in]
        o0 = n_in + r_in
        outs, routs = refs[o0:o0 + n_out], refs[o0 + n_out:o0 + n_out + r_out]
        s0 = o0 + n_out + r_out
        scratch, sems = refs[s0:s0 + n_sc], refs[s0 + n_sc]
        first = functools.reduce(jnp.logical_and, [pl.program_id(a) == 0 for a in range(len(grid))])
        last = functools.reduce(jnp.logical_and, [pl.program_id(a) == grid[a] - 1 for a in range(len(grid))])

        @pl.when(first)
        def _():
            rider.start(rins, routs, sems)

        body(*ins, *outs, *scratch)

        @pl.when(last)
        def _():
            rider.finish(rins, routs, sems)

    res = pl.pallas_call(
        riding, name=name, grid=grid, in_specs=list(in_specs) + [_any()] * r_in,
        out_specs=list(out_specs) + [_any()] * r_out, out_shape=list(out_shape) + list(rider.outs),
        scratch_shapes=list(scratch_shapes) + [pltpu.SemaphoreType.DMA((rider.n_sems,))],
        compiler_params=_cp(*["arbitrary"] * len(grid)))(*args, *rider.ins)
    return res[:n_out], res[n_out:]


def _run_alone(rider, name):
    def body(*refs):
        r_in, r_out = len(rider.ins), len(rider.outs)
        rins, routs, sems = refs[:r_in], refs[r_in:r_in + r_out], refs[r_in + r_out]
        rider.start(rins, routs, sems)
        rider.finish(rins, routs, sems)

    return pl.pallas_call(
        body, name=name, in_specs=[_any()] * len(rider.ins), out_specs=[_any()] * len(rider.outs),
        out_shape=list(rider.outs), scratch_shapes=[pltpu.SemaphoreType.DMA((rider.n_sems,))])(*rider.ins)


def rms_inproj(x, g, wt, b, *, tm=512, rider=None):
    T = x.shape[0]

    def body(x_ref, g_ref, w_ref, b_ref, qkv_ref, rest_ref):
        xv = x_ref[...]
        r = lax.rsqrt(jnp.mean(xv * xv, axis=-1, keepdims=True) + EPS)
        h = (xv * r * g_ref[...]).astype(CDT)
        qkv_ref[...] = (_dot_nt(h, w_ref[0:QKV_W, :]) + b_ref[:, 0:QKV_W]).astype(qkv_ref.dtype)
        for j in range(REST_W // D):
            c0 = QKV_W + D * j
            rest_ref[:, D * j:D * (j + 1)] = _dot_nt(h, w_ref[c0:c0 + D, :]) + b_ref[:, c0:c0 + D]

    return _run(
        body, rider, name="rms_inproj", grid=(T // tm,),
        in_specs=[pl.BlockSpec((tm, D), lambda i: (i, 0)), _const((1, D)), _const((IN_W, D), 1), _const((1, IN_W))],
        out_specs=[pl.BlockSpec((tm, QKV_W), lambda i: (i, 0)), pl.BlockSpec((tm, REST_W), lambda i: (i, 0))],
        out_shape=[jax.ShapeDtypeStruct((T, QKV_W), CDT), jax.ShapeDtypeStruct((T, REST_W), F32)],
        sem=("parallel",), args=(x, g, wt, b))


def _lane_halves(shape):
    lane = lax.broadcasted_iota(jnp.int32, shape, 1)
    return lane < HEAD_DIM, lane >= HEAD_DIM


def _swap_halves(v):
    return pltpu.roll(v.astype(F32), HEAD_DIM, axis=1).astype(v.dtype)


N_KV = KV_W // HEAD_DIM
GROUP = N_Q // N_KV
STACK = GROUP * BLOCK


def _score_bias():
    row = np.arange(STACK)[:, None] % BLOCK
    col = np.arange(2 * BLOCK)[None, :]
    dist = row + BLOCK - col
    window = (dist >= 0) & (dist < BLOCK)
    slopes = np.asarray(SLOPES, np.float32).reshape(N_KV, GROUP)
    out = np.empty((2, N_KV, STACK, 2 * BLOCK), np.float32)
    for first in range(2):
        valid = window & ((col >= BLOCK) | (first == 0))
        for g in range(N_KV):
            slope = np.repeat(slopes[g], BLOCK)[:, None]
            out[first, g] = np.where(valid, -(slope * dist.astype(np.float32)), np.float32(NEG))
    return jnp.asarray(out)


def _per_head_column(vals):
    row = lax.broadcasted_iota(jnp.int32, (STACK, 1), 0)
    col = jnp.full((STACK, 1), vals[GROUP - 1], F32)
    for i in reversed(range(GROUP - 1)):
        col = jnp.where(row < (i + 1) * BLOCK, vals[i], col)
    return col


def _stack_heads(dst, src_ref, r0, g, scale=None):
    lane = lax.broadcasted_iota(jnp.int32, (BLOCK, 2 * HEAD_DIM), 1)
    keep = (lane >= HEAD_DIM) if g else (lane < HEAD_DIM)
    for i in range(GROUP):
        h = GROUP * g + i
        tile = src_ref[pl.ds(r0, BLOCK), (h // 2) * 128:(h // 2 + 1) * 128]
        if h % 2 != g:
            tile = _swap_halves(tile)
        if scale is not None:
            tile = tile * jnp.asarray(scale, tile.dtype)
        dst[i * BLOCK:(i + 1) * BLOCK, :] = jnp.where(keep, tile, jnp.zeros_like(tile))


def _unstack_heads(dst_ref, stacked, r0, g):
    lane = lax.broadcasted_iota(jnp.int32, (BLOCK, 2 * HEAD_DIM), 1)
    tiles = []
    for j in range(GROUP // 2):
        even = stacked[(2 * j) * BLOCK:(2 * j + 1) * BLOCK, :]
        odd = stacked[(2 * j + 1) * BLOCK:(2 * j + 2) * BLOCK, :]
        lo = _swap_halves(even) if g else even
        hi = odd if g else _swap_halves(odd)
        c0 = ((GROUP * g) // 2 + j) * 128
        tile = jnp.where(lane < HEAD_DIM, lo, hi)
        dst_ref[pl.ds(r0, BLOCK), c0:c0 + 128] = tile.astype(dst_ref.dtype)
        tiles.append((c0, tile))
    return tiles


def _qkv_specs(tq):
    nb = tq // BLOCK
    return [
        pl.BlockSpec((tq, ATTN_W), lambda i: (i, 0)),
        pl.BlockSpec((BLOCK, KV_W), lambda i: (jnp.maximum(i * nb - 1, 0), ATTN_W // KV_W)),
        pl.BlockSpec((tq, KV_W), lambda i: (i, ATTN_W // KV_W)),
        pl.BlockSpec((BLOCK, KV_W), lambda i: (jnp.maximum(i * nb - 1, 0), ATTN_W // KV_W + 1)),
        pl.BlockSpec((tq, KV_W), lambda i: (i, ATTN_W // KV_W + 1)),
    ]


def attn_fwd(qkv, sinks, *, tq=512):
    T = qkv.shape[0]
    nb = tq // BLOCK

    def body(sink_ref, bias_ref, q_ref, kp_ref, kc_ref, vp_ref, vc_ref, o_ref, lse_ref, kext, vext, qs):
        i = pl.program_id(0)
        kext[0:BLOCK, :] = kp_ref[...]
        kext[BLOCK:, :] = kc_ref[...]
        vext[0:BLOCK, :] = vp_ref[...]
        vext[BLOCK:, :] = vc_ref[...]
        lane_l = lax.broadcasted_iota(jnp.int32, (BLOCK, 128), 1)

        def blk(b, carry):
            r0 = pl.multiple_of(b * BLOCK, BLOCK)
            first = jnp.logical_and(i == 0, b == 0).astype(jnp.int32)
            kc = kext[pl.ds(r0, 2 * BLOCK), :]
            vc = vext[pl.ds(r0, 2 * BLOCK), :]
            lse_t = jnp.zeros((BLOCK, 128), F32)
            for g in range(N_KV):
                heads = range(GROUP * g, GROUP * (g + 1))
                _stack_heads(qs.at[g], q_ref, r0, g, SCALE)
                s = _dot_nt(qs[g], kc) + bias_ref[first, g]
                sink = _per_head_column([sink_ref[h] for h in heads])
                m = jnp.maximum(jnp.max(s, axis=-1, keepdims=True), sink)
                p = jnp.exp(s - m)
                denom = jnp.sum(p, axis=-1, keepdims=True) + jnp.exp(sink - m)
                p = p / denom
                _unstack_heads(o_ref, _dot(p.astype(CDT), vc), r0, g)
                lse = m + jnp.log(denom)
                for i_h, h in enumerate(heads):
                    lse_t = jnp.where(lane_l == h, lse[i_h * BLOCK:(i_h + 1) * BLOCK, :], lse_t)
            lse_ref[pl.ds(r0, BLOCK), :] = lse_t
            return carry

        lax.fori_loop(0, nb, blk, 0)

    return pl.pallas_call(
        body, name="attn_fwd", grid=(T // tq,),
        in_specs=[pl.BlockSpec(memory_space=pltpu.SMEM), _const((2, N_KV, STACK, 2 * BLOCK), 1)] + _qkv_specs(tq),
        out_specs=[pl.BlockSpec((tq, ATTN_W), lambda i: (i, 0)), pl.BlockSpec((tq, 128), lambda i: (i, 0))],
        out_shape=[jax.ShapeDtypeStruct((T, ATTN_W), CDT), jax.ShapeDtypeStruct((T, 128), F32)],
        scratch_shapes=[pltpu.VMEM((tq + BLOCK, KV_W), CDT), pltpu.VMEM((tq + BLOCK, KV_W), CDT),
                        pltpu.VMEM((N_KV, STACK, 2 * HEAD_DIM), CDT)],
        compiler_params=_cp("parallel"),
    )(sinks, _score_bias(), qkv, qkv, qkv, qkv, qkv)


def _halo_before(tm, width, col):
    return pl.BlockSpec((HALO, width), lambda i: (jnp.maximum(i * (tm // HALO) - 1, 0), col))


def _fill_u0(ext, a_ref, b_ref, ha_ref, hb_ref, first):
    hu = ha_ref[...] * _sig(hb_ref[...])
    ext[0:HALO, :] = jnp.where(first, jnp.zeros_like(hu), hu)
    ext[HALO:, :] = a_ref[...] * _sig(b_ref[...])


def _shifted_taps(src, w_ref, base, rc, offsets):
    acc = jnp.zeros((rc, CONV_C), F32)
    for b in range(SUBLANES):
        taps = [(k, o - b) for k, o in enumerate(offsets) if o % SUBLANES == b]
        if not taps:
            continue
        rows = rc if b == 0 else rc + SUBLANES
        part = jnp.zeros((rows, CONV_C), F32)
        for k, o8 in taps:
            part = part + w_ref[k:k + 1, :] * src[base + o8:base + o8 + rows, :]
        acc = acc + (part if b == 0 else part[b:b + rc, :])
    return acc


def _conv_rows(ext, w_ref, r0, rc):
    return _shifted_taps(ext, w_ref, r0, rc, [HALO - (CONV_K - 1) + k for k in range(CONV_K)])


def _layer_norm(u1, g, b):
    mu = jnp.mean(u1, axis=-1, keepdims=True)
    xc = u1 - mu
    rstd = lax.rsqrt(jnp.mean(xc * xc, axis=-1, keepdims=True) + EPS)
    n = xc * rstd
    return n, rstd, n * g + b


CONV_RC = 32
CONV_RC_1PASS = 64


def conv_fwd(rest, cw, cb, lg, lb, *, tm=512):
    T = rest.shape[0]

    def body(a_ref, b_ref, ha_ref, hb_ref, w_ref, cb_ref, lg_ref, lb_ref, o_ref, u1_ref, ext):
        _fill_u0(ext, a_ref, b_ref, ha_ref, hb_ref, pl.program_id(0) == 0)
        rc = CONV_RC_1PASS
        for r0 in range(0, tm, rc):
            u1 = _conv_rows(ext, w_ref, r0, rc) + cb_ref[...]
            u1_ref[r0:r0 + rc, :] = u1
            _, _, u2 = _layer_norm(u1, lg_ref[...], lb_ref[...])
            o_ref[r0:r0 + rc, :] = (u2 * _sig(u2)).astype(o_ref.dtype)

    row = lambda i: (i, 0)
    return pl.pallas_call(
        body, name="conv_fwd", grid=(T // tm,),
        in_specs=[pl.BlockSpec((tm, CONV_C), row), pl.BlockSpec((tm, CONV_C), lambda i: (i, 1)),
                  _halo_before(tm, CONV_C, 0), _halo_before(tm, CONV_C, 1),
                  _const((CONV_K, CONV_C)), _const((1, CONV_C)), _const((1, CONV_C)), _const((1, CONV_C))],
        out_specs=[pl.BlockSpec((tm, CONV_C), row), pl.BlockSpec((tm, CONV_C), row)],
        out_shape=[jax.ShapeDtypeStruct((T, CONV_C), CDT), jax.ShapeDtypeStruct((T, CONV_C), F32)],
        scratch_shapes=[pltpu.VMEM((tm + HALO, CONV_C), F32)],
        compiler_params=_cp("parallel"),
    )(rest, rest, rest, rest, cw, cb, lg, lb)


def merge_out(x, attn, u3, rest, wa, wc, bc, wo, *, tm=512):
    T = x.shape[0]

    def body(x_ref, at_ref, u_ref, ga_ref, gc_ref, wa_ref, wc_ref, bc_ref, wo_ref, o_ref):
        br_a = _dot(at_ref[...], wa_ref[...])
        br_c = _dot(u_ref[...], wc_ref[...]) + bc_ref[...]
        merged = _sig(ga_ref[...]) * br_a + _sig(gc_ref[...]) * br_c
        o_ref[...] = x_ref[...] + _dot(merged.astype(CDT), wo_ref[...])

    return pl.pallas_call(
        body, name="merge_out", grid=(T // tm,),
        in_specs=[pl.BlockSpec((tm, D), lambda i: (i, 0)), pl.BlockSpec((tm, ATTN_W), lambda i: (i, 0)),
                  pl.BlockSpec((tm, CONV_C), lambda i: (i, 0)),
                  pl.BlockSpec((tm, D), lambda i: (i, 1)), pl.BlockSpec((tm, D), lambda i: (i, 2)),
                  _const((ATTN_W, D), 1), _const((CONV_C, D), 1), _const((1, D)), _const((D, D), 1)],
        out_specs=pl.BlockSpec((tm, D), lambda i: (i, 0)),
        out_shape=jax.ShapeDtypeStruct((T, D), F32),
        compiler_params=_cp("parallel"),
    )(x, attn, u3, rest, rest, wa, wc, bc, wo)


def mlp_fwd(x, g, w1, w2, *, tm=256, tf=D_FF, rider=None):
    T = x.shape[0]
    nf = D_FF // tf

    def body(x_ref, g_ref, w1_ref, w2_ref, o_ref, pre_ref, h_s, acc_s):
        f = pl.program_id(1)

        @pl.when(f == 0)
        def _():
            xv = x_ref[...]
            r = lax.rsqrt(jnp.mean(xv * xv, axis=-1, keepdims=True) + EPS)
            h_s[...] = (xv * r * g_ref[...]).astype(CDT)
            acc_s[...] = jnp.zeros_like(acc_s)

        pre = _dot(h_s[...], w1_ref[...])
        pre_ref[...] = pre
        a = jnp.square(jnp.maximum(pre, 0.0))
        acc_s[...] += _dot(a.astype(CDT), w2_ref[...])

        @pl.when(f == nf - 1)
        def _():
            o_ref[...] = x_ref[...] + acc_s[...]

    mode = {"pipeline_mode": pl.Buffered(1)} if nf == 1 else {}
    return _run(
        body, rider, name="mlp_fwd", grid=(T // tm, nf),
        in_specs=[pl.BlockSpec((tm, D), lambda i, f: (i, 0)), _const((1, D)),
                  pl.BlockSpec((D, tf), lambda i, f: (0, f), **mode), pl.BlockSpec((tf, D), lambda i, f: (f, 0), **mode)],
        out_specs=[pl.BlockSpec((tm, D), lambda i, f: (i, 0)), pl.BlockSpec((tm, tf), lambda i, f: (i, f))],
        out_shape=[jax.ShapeDtypeStruct((T, D), F32), jax.ShapeDtypeStruct((T, D_FF), F32)],
        scratch_shapes=[pltpu.VMEM((tm, D), CDT), pltpu.VMEM((tm, D), F32)],
        sem=("parallel", "arbitrary"), args=(x, g, w1, w2))


def _rms_bwd(xv, g, dh):
    r = lax.rsqrt(jnp.mean(xv * xv, axis=-1, keepdims=True) + EPS)
    xhat = xv * r
    dxh = dh * g
    dx = r * (dxh - xhat * jnp.mean(dxh * xhat, axis=-1, keepdims=True))
    return dx, dh * xhat


def loss_head(x, g, tgt, *, tm=512):
    T = x.shape[0]

    def body(x_ref, g_ref, t_ref, dx_ref, dg_ref, loss_ref):
        @pl.when(pl.program_id(0) == 0)
        def _():
            dg_ref[...] = jnp.zeros_like(dg_ref)
            loss_ref[...] = jnp.zeros_like(loss_ref)

        xv = x_ref[...]
        gv = g_ref[...]
        r = lax.rsqrt(jnp.mean(xv * xv, axis=-1, keepdims=True) + EPS)
        e = xv * r * gv - t_ref[...]
        loss_ref[...] += 0.5 * jnp.sum(jnp.mean(e * e, axis=-1, keepdims=True), axis=0, keepdims=True)
        dx, dg_rows = _rms_bwd(xv, gv, e * (1.0 / D))
        dx_ref[...] = dx
        dg_ref[...] += _colsum(dg_rows)

    return pl.pallas_call(
        body, name="loss_head", grid=(T // tm,),
        in_specs=[pl.BlockSpec((tm, D), lambda i: (i, 0)), _const((1, D)), pl.BlockSpec((tm, D), lambda i: (i, 0))],
        out_specs=[pl.BlockSpec((tm, D), lambda i: (i, 0)), _const((1, D)), _const((1, 128))],
        out_shape=[jax.ShapeDtypeStruct((T, D), F32), jax.ShapeDtypeStruct((1, D), F32),
                   jax.ShapeDtypeStruct((1, 128), F32)],
        compiler_params=_cp("arbitrary"),
    )(x, g, tgt)


def mlp_bwd(dy, x, g, pre, w1, w2, *, tm=256, tf=D_FF, rider=None):
    T = x.shape[0]
    nf = D_FF // tf

    def body(dy_ref, x_ref, g_ref, pre_ref, w1_ref, w2_ref, dx_ref, dg_ref, h_ref, a_ref, dpre_ref, dyb_s, acc_s):
        i, f = pl.program_id(0), pl.program_id(1)

        @pl.when(jnp.logical_and(i == 0, f == 0))
        def _():
            dg_ref[...] = jnp.zeros_like(dg_ref)

        @pl.when(f == 0)
        def _():
            dyb_s[...] = dy_ref[...].astype(CDT)
            acc_s[...] = jnp.zeros_like(acc_s)

        pre = pre_ref[...]
        rl = jnp.maximum(pre, 0.0)
        a_ref[...] = (rl * rl).astype(CDT)
        da = _dot_nt(dyb_s[...], w2_ref[...])
        dpre = (da * (2.0 * rl)).astype(CDT)
        dpre_ref[...] = dpre
        acc_s[...] += _dot_nt(dpre, w1_ref[...])

        @pl.when(f == nf - 1)
        def _():
            xv = x_ref[...]
            gv = g_ref[...]
            dxn, dg_rows = _rms_bwd(xv, gv, acc_s[...])
            dx_ref[...] = dy_ref[...] + dxn
            dg_ref[...] += _colsum(dg_rows)
            r = lax.rsqrt(jnp.mean(xv * xv, axis=-1, keepdims=True) + EPS)
            h_ref[...] = (xv * r * gv).astype(CDT)

    row = lambda i, f: (i, 0)
    mode = {"pipeline_mode": pl.Buffered(1)} if nf == 1 else {}
    return _run(
        body, rider, name="mlp_bwd", grid=(T // tm, nf),
        in_specs=[pl.BlockSpec((tm, D), row), pl.BlockSpec((tm, D), row), _const((1, D)),
                  pl.BlockSpec((tm, tf), lambda i, f: (i, f)),
                  pl.BlockSpec((D, tf), lambda i, f: (0, f), **mode), pl.BlockSpec((tf, D), lambda i, f: (f, 0), **mode)],
        out_specs=[pl.BlockSpec((tm, D), row), _const((1, D)), pl.BlockSpec((tm, D), row),
                   pl.BlockSpec((tm, tf), lambda i, f: (i, f)), pl.BlockSpec((tm, tf), lambda i, f: (i, f))],
        out_shape=[jax.ShapeDtypeStruct((T, D), F32), jax.ShapeDtypeStruct((1, D), F32),
                   jax.ShapeDtypeStruct((T, D), CDT), jax.ShapeDtypeStruct((T, D_FF), CDT),
                   jax.ShapeDtypeStruct((T, D_FF), CDT)],
        scratch_shapes=[pltpu.VMEM((tm, D), CDT), pltpu.VMEM((tm, D), F32)],
        sem=("arbitrary", "arbitrary"), args=(dy, x, g, pre, w1, w2))


def tn_matmul(a, b, *, tm, tn, tk=2048, name, by_chip=False, rider=None):
    T, M = a.shape
    N = b.shape[1]
    tk = min(tk, T)
    nk = T // tk

    def body(a_ref, b_ref, o_ref):
        @pl.when(pl.program_id(2) == 0)
        def _():
            o_ref[...] = jnp.zeros_like(o_ref)

        o_ref[...] += _dot_tn(a_ref[...].astype(CDT), b_ref[...].astype(CDT))

    if by_chip:
        out_spec = pl.BlockSpec((None, tm, tn), lambda i, j, k: (j, i, 0))
        out_shape = jax.ShapeDtypeStruct((N // tn, M, tn), F32)
    else:
        out_spec = pl.BlockSpec((tm, tn), lambda i, j, k: (i, j))
        out_shape = jax.ShapeDtypeStruct((M, N), F32)
    res = _run(
        body, rider, name=name, grid=(M // tm, N // tn, nk),
        in_specs=[pl.BlockSpec((tk, tm), lambda i, j, k: (k, i)), pl.BlockSpec((tk, tn), lambda i, j, k: (k, j))],
        out_specs=[out_spec], out_shape=[out_shape], sem=("parallel", "parallel", "arbitrary"), args=(a, b))
    return res[0] if rider is None else (res[0][0], res[1])


def merge_bwd(dx1, attn, u3, rest, wa, wc, bc, wo, *, tm=512):
    T = dx1.shape[0]

    def body(dx_ref, at_ref, u_ref, ga_ref, gc_ref, wa_ref, wc_ref, bc_ref, wo_ref,
             mg_ref, dba_ref, dbc_ref, dat_ref, du_ref, dgate_ref, dgsum_ref, dbias_ref):
        @pl.when(pl.program_id(0) == 0)
        def _():
            dbias_ref[...] = jnp.zeros_like(dbias_ref)
            dgsum_ref[...] = jnp.zeros_like(dgsum_ref)

        br_a = _dot(at_ref[...], wa_ref[...])
        br_c = _dot(u_ref[...], wc_ref[...]) + bc_ref[...]
        sa = _sig(ga_ref[...])
        sc = _sig(gc_ref[...])
        mg_ref[...] = (sa * br_a + sc * br_c).astype(CDT)
        dm = _dot_nt(dx_ref[...].astype(CDT), wo_ref[...])
        dba = dm * sa
        dbc = dm * sc
        dga = dm * br_a * sa * (1.0 - sa)
        dgc = dm * br_c * sc * (1.0 - sc)
        dgate_ref[:, 0:D] = dga.astype(CDT)
        dgate_ref[:, D:2 * D] = dgc.astype(CDT)
        dgsum_ref[:, 0:D] += _colsum(dga)
        dgsum_ref[:, D:2 * D] += _colsum(dgc)
        dbias_ref[...] += _colsum(dbc)
        dba_b = dba.astype(CDT)
        dbc_b = dbc.astype(CDT)
        dba_ref[...] = dba_b
        dbc_ref[...] = dbc_b
        dat_ref[...] = _dot_nt(dba_b, wa_ref[...]).astype(CDT)
        du_ref[...] = _dot_nt(dbc_b, wc_ref[...])

    row = lambda i: (i, 0)
    return pl.pallas_call(
        body, name="merge_bwd", grid=(T // tm,),
        in_specs=[pl.BlockSpec((tm, D), row), pl.BlockSpec((tm, ATTN_W), row), pl.BlockSpec((tm, CONV_C), row),
                  pl.BlockSpec((tm, D), lambda i: (i, 1)), pl.BlockSpec((tm, D), lambda i: (i, 2)),
                  _const((ATTN_W, D), 1), _const((CONV_C, D), 1), _const((1, D)), _const((D, D), 1)],
        out_specs=[pl.BlockSpec((tm, D), row), pl.BlockSpec((tm, D), row), pl.BlockSpec((tm, D), row),
                   pl.BlockSpec((tm, ATTN_W), row), pl.BlockSpec((tm, CONV_C), row),
                   pl.BlockSpec((tm, 2 * D), row), _const((1, 2 * D)), _const((1, D))],
        out_shape=[jax.ShapeDtypeStruct((T, D), CDT), jax.ShapeDtypeStruct((T, D), CDT),
                   jax.ShapeDtypeStruct((T, D), CDT), jax.ShapeDtypeStruct((T, ATTN_W), CDT),
                   jax.ShapeDtypeStruct((T, CONV_C), F32), jax.ShapeDtypeStruct((T, 2 * D), CDT),
                   jax.ShapeDtypeStruct((1, 2 * D), F32), jax.ShapeDtypeStruct((1, D), F32)],
        compiler_params=_cp("arbitrary"),
    )(dx1, attn, u3, rest, rest, wa, wc, bc, wo)


def conv_bwd_ln(du3, u1, lg, lb, *, tm=512, rider=None):
    T = du3.shape[0]

    def body(du_ref, u1_ref, lg_ref, lb_ref, du1_ref, dlg_ref, dlb_ref, dcb_ref):
        @pl.when(pl.program_id(0) == 0)
        def _():
            dlg_ref[...] = jnp.zeros_like(dlg_ref)
            dlb_ref[...] = jnp.zeros_like(dlb_ref)
            dcb_ref[...] = jnp.zeros_like(dcb_ref)

        dlg = jnp.zeros((1, CONV_C), F32)
        dlb = jnp.zeros((1, CONV_C), F32)
        dcb = jnp.zeros((1, CONV_C), F32)
        rc = CONV_RC_1PASS
        for r0 in range(0, tm, rc):
            n, rstd, u2 = _layer_norm(u1_ref[r0:r0 + rc, :], lg_ref[...], lb_ref[...])
            s = _sig(u2)
            du2 = du_ref[r0:r0 + rc, :] * (s + u2 * s * (1.0 - s))
            dn = du2 * lg_ref[...]
            du1 = rstd * (dn - jnp.mean(dn, axis=-1, keepdims=True) - n * jnp.mean(dn * n, axis=-1, keepdims=True))
            du1_ref[r0:r0 + rc, :] = du1
            dlg = dlg + _colsum(du2 * n)
            dlb = dlb + _colsum(du2)
            dcb = dcb + _colsum(du1)
        dlg_ref[...] += dlg
        dlb_ref[...] += dlb
        dcb_ref[...] += dcb

    row = lambda i: (i, 0)
    vec = jax.ShapeDtypeStruct((1, CONV_C), F32)
    return _run(
        body, rider, name="conv_bwd_ln", grid=(T // tm,),
        in_specs=[pl.BlockSpec((tm, CONV_C), row), pl.BlockSpec((tm, CONV_C), row), _const((1, CONV_C)), _const((1, CONV_C))],
        out_specs=[pl.BlockSpec((tm, CONV_C), row), _const((1, CONV_C)), _const((1, CONV_C)), _const((1, CONV_C))],
        out_shape=[jax.ShapeDtypeStruct((T, CONV_C), F32), vec, vec, vec],
        sem=("arbitrary",), args=(du3, u1, lg, lb))


def conv_bwd_taps(du1, rest, cw, *, tm=512):
    T = du1.shape[0]
    nt = T // tm

    def body(d_ref, hd_ref, a_ref, b_ref, ha_ref, hb_ref, w_ref, dglu_ref, dgsum_ref, dw_ref, ext, dext, dwacc):
        i = pl.program_id(0)

        @pl.when(i == 0)
        def _():
            dwacc[...] = jnp.zeros_like(dwacc)
            dgsum_ref[...] = jnp.zeros_like(dgsum_ref)

        sums = [jnp.zeros((SUBLANES, CONV_C), F32), jnp.zeros((SUBLANES, CONV_C), F32)]
        _fill_u0(ext, a_ref, b_ref, ha_ref, hb_ref, i == 0)
        dext[0:SUBLANES, :] = jnp.zeros((SUBLANES, CONV_C), F32)
        dext[SUBLANES:SUBLANES + tm, :] = d_ref[...]
        hd = hd_ref[...]
        dext[SUBLANES + tm:, :] = jnp.where(i == nt - 1, jnp.zeros_like(hd), hd)
        qrow = lax.broadcasted_iota(jnp.int32, (CONV_RC + SUBLANES, CONV_C), 0)
        for c in range(tm // CONV_RC):
            r0 = c * CONV_RC
            du0 = _shifted_taps(dext, w_ref, r0 + SUBLANES, CONV_RC, [CONV_K - 1 - k for k in range(CONV_K)])
            for b in range(SUBLANES):
                taps = [(k, HALO - (CONV_K - 1) + k - b) for k in range(CONV_K) if (HALO - (CONV_K - 1) + k) % SUBLANES == b]
                if b == 0:
                    rows = CONV_RC
                    dsh = dext[r0 + SUBLANES:r0 + SUBLANES + rows, :]
                else:
                    rows = CONV_RC + SUBLANES
                    dsh = dext[r0 + SUBLANES - b:r0 + SUBLANES - b + rows, :]
                    dsh = jnp.where((qrow >= b) & (qrow < CONV_RC + b), dsh, 0.0)
                for k, o8 in taps:
                    prod = dsh * ext[r0 + o8:r0 + o8 + rows, :]
                    dwacc[8 * k:8 * k + 8, :] += jnp.sum(prod.reshape(rows // SUBLANES, SUBLANES, CONV_C), axis=0)
            av = a_ref[r0:r0 + CONV_RC, :]
            sb = _sig(b_ref[r0:r0 + CONV_RC, :])
            for half, dg in enumerate((du0 * sb, du0 * av * sb * (1.0 - sb))):
                dglu_ref[r0:r0 + CONV_RC, half * CONV_C:(half + 1) * CONV_C] = dg.astype(CDT)
                sums[half] = sums[half] + jnp.sum(dg.reshape(CONV_RC // SUBLANES, SUBLANES, CONV_C), axis=0)
        for half in range(2):
            dgsum_ref[:, half * CONV_C:(half + 1) * CONV_C] += _colsum(sums[half])

        @pl.when(i == nt - 1)
        def _():
            dw_ref[...] = jnp.zeros_like(dw_ref)
            for k in range(CONV_K):
                dw_ref[k:k + 1, :] = _colsum(dwacc[8 * k:8 * k + 8, :])

    row = lambda i: (i, 0)
    return pl.pallas_call(
        body, name="conv_bwd_taps", grid=(nt,),
        in_specs=[pl.BlockSpec((tm, CONV_C), row),
                  pl.BlockSpec((HALO, CONV_C), lambda i: (jnp.minimum((i + 1) * (tm // HALO), T // HALO - 1), 0)),
                  pl.BlockSpec((tm, CONV_C), row), pl.BlockSpec((tm, CONV_C), lambda i: (i, 1)),
                  _halo_before(tm, CONV_C, 0), _halo_before(tm, CONV_C, 1), _const((CONV_K, CONV_C))],
        out_specs=[pl.BlockSpec((tm, 2 * CONV_C), row), _const((1, 2 * CONV_C)), _const((HALO, CONV_C))],
        out_shape=[jax.ShapeDtypeStruct((T, 2 * CONV_C), CDT), jax.ShapeDtypeStruct((1, 2 * CONV_C), F32),
                   jax.ShapeDtypeStruct((HALO, CONV_C), F32)],
        scratch_shapes=[pltpu.VMEM((tm + HALO, CONV_C), F32), pltpu.VMEM((SUBLANES + tm + HALO, CONV_C), F32),
                        pltpu.VMEM((8 * CONV_K, CONV_C), F32)],
        compiler_params=_cp("arbitrary"),
    )(du1, du1, rest, rest, rest, rest, cw)


def attn_bwd(qkv, do, lse, sinks, *, tq=512, rider=None):
    T = qkv.shape[0]
    nb = tq // BLOCK

    def body(sink_ref, bias_ref, q_ref, kp_ref, kc_ref, vp_ref, vc_ref, do_ref, lse_ref,
             dq_ref, dqsum_ref, dkv_ref, spill_ref, dsink_ref, kext, vext, dkext, dvext, qs, dos):
        i = pl.program_id(0)

        @pl.when(i == 0)
        def _():
            dsink_ref[...] = jnp.zeros_like(dsink_ref)
            dqsum_ref[...] = jnp.zeros_like(dqsum_ref)

        kext[0:BLOCK, :] = kp_ref[...]
        kext[BLOCK:, :] = kc_ref[...]
        vext[0:BLOCK, :] = vp_ref[...]
        vext[BLOCK:, :] = vc_ref[...]
        dkext[...] = jnp.zeros_like(dkext)
        dvext[...] = jnp.zeros_like(dvext)
        lane_l = lax.broadcasted_iota(jnp.int32, (BLOCK, 128), 1)
        lane_k = lax.broadcasted_iota(jnp.int32, (2 * BLOCK, KV_W), 1)

        def blk(b, dsink):
            r0 = pl.multiple_of(b * BLOCK, BLOCK)
            first = jnp.logical_and(i == 0, b == 0).astype(jnp.int32)
            kc = kext[pl.ds(r0, 2 * BLOCK), :]
            vc = vext[pl.ds(r0, 2 * BLOCK), :]
            lse_t = lse_ref[pl.ds(r0, BLOCK), :]
            dk = jnp.zeros((2 * BLOCK, KV_W), F32)
            dv = jnp.zeros((2 * BLOCK, KV_W), F32)
            for g in range(N_KV):
                heads = range(GROUP * g, GROUP * (g + 1))
                _stack_heads(qs.at[g], q_ref, r0, g, SCALE)
                _stack_heads(dos.at[g], do_ref, r0, g)
                qv = qs[g]
                dov = dos[g]
                s = _dot_nt(qv, kc) + bias_ref[first, g]
                lse = jnp.concatenate(
                    [jnp.sum(jnp.where(lane_l == h, lse_t, 0.0), axis=-1, keepdims=True) for h in heads], axis=0)
                p = jnp.exp(s - lse)
                dp = _dot_nt(dov, vc)
                dd = jnp.sum(p * dp, axis=-1, keepdims=True)
                ds = (p * (dp - dd)).astype(CDT)
                keep = (lane_k >= HEAD_DIM) if g else (lane_k < HEAD_DIM)
                for c0, tile in _unstack_heads(dq_ref, _dot(ds, jnp.where(keep, kc, jnp.zeros_like(kc))) * SCALE, r0, g):
                    dqsum_ref[:, c0:c0 + 128] += _colsum(tile)
                dk = dk + _dot_tn(ds, qv)
                dv = dv + _dot_tn(p.astype(CDT), dov)
                wsink = jnp.exp(_per_head_column([sink_ref[h] for h in heads]) - lse) * dd
                for i_h, h in enumerate(heads):
                    part = jnp.sum(wsink[i_h * BLOCK:(i_h + 1) * BLOCK, :], axis=0, keepdims=True)
                    dsink = dsink - jnp.where(lane_l[0:1, :] == h, part, 0.0)
            dkext[pl.ds(r0, 2 * BLOCK), :] += dk
            dvext[pl.ds(r0, 2 * BLOCK), :] += dv
            return dsink

        dsink_ref[...] += lax.fori_loop(0, nb, blk, jnp.zeros((1, 128), F32))
        dkv_ref[:, 0:KV_W] = dkext[BLOCK:, :]
        dkv_ref[:, KV_W:2 * KV_W] = dvext[BLOCK:, :]
        spill_ref[:, 0:KV_W] = dkext[0:BLOCK, :]
        spill_ref[:, KV_W:2 * KV_W] = dvext[0:BLOCK, :]

    row = lambda i: (i, 0)
    return _run(
        body, rider, name="attn_bwd", grid=(T // tq,),
        in_specs=[pl.BlockSpec(memory_space=pltpu.SMEM), _const((2, N_KV, STACK, 2 * BLOCK), 1)] + _qkv_specs(tq)
        + [pl.BlockSpec((tq, ATTN_W), row), pl.BlockSpec((tq, 128), row)],
        out_specs=[pl.BlockSpec((tq, ATTN_W), row), _const((1, ATTN_W)), pl.BlockSpec((tq, 2 * KV_W), row),
                   pl.BlockSpec((BLOCK, 2 * KV_W), row), _const((1, 128))],
        out_shape=[jax.ShapeDtypeStruct((T, ATTN_W), CDT), jax.ShapeDtypeStruct((1, ATTN_W), F32),
                   jax.ShapeDtypeStruct((T, 2 * KV_W), F32),
                   jax.ShapeDtypeStruct((T // tq * BLOCK, 2 * KV_W), F32), jax.ShapeDtypeStruct((1, 128), F32)],
        scratch_shapes=[pltpu.VMEM((tq + BLOCK, KV_W), CDT), pltpu.VMEM((tq + BLOCK, KV_W), CDT),
                        pltpu.VMEM((tq + BLOCK, KV_W), F32), pltpu.VMEM((tq + BLOCK, KV_W), F32),
                        pltpu.VMEM((N_KV, STACK, 2 * HEAD_DIM), CDT), pltpu.VMEM((N_KV, STACK, 2 * HEAD_DIM), CDT)],
        sem=("arbitrary",), args=(sinks, _score_bias(), qkv, qkv, qkv, qkv, qkv, do, lse))


def inproj_bwd(dres, x, g, w, dq, dkv, spill, dglu, dgate, sums, *, tm=512):
    T = x.shape[0]
    nt = T // tm
    pieces = ((0, ATTN_W), (QKV_W, 2 * CONV_C), (QKV_W + 2 * CONV_C, 2 * D))

    def body(dr_ref, x_ref, g_ref, w_ref, dq_ref, dkv_ref, sp_ref, dglu_ref, dgate_ref, sq_ref, sglu_ref, sgate_ref,
             dx_ref, dp_ref, h_ref, dg_ref, db_ref):
        i = pl.program_id(0)

        @pl.when(i == 0)
        def _():
            dg_ref[...] = jnp.zeros_like(dg_ref)
            db_ref[:, ATTN_W:QKV_W] = jnp.zeros((1, QKV_W - ATTN_W), F32)
            for (c0, wd), s_ref in zip(pieces, (sq_ref, sglu_ref, sgate_ref)):
                db_ref[:, c0:c0 + wd] = s_ref[...]

        sp = sp_ref[...]
        sp = jnp.where(i == nt - 1, jnp.zeros_like(sp), sp)
        dkv = dkv_ref[...]
        db_ref[:, ATTN_W:QKV_W] += _colsum(dkv) + _colsum(sp)
        dp_ref[0:tm - BLOCK, ATTN_W:QKV_W] = dkv[0:tm - BLOCK, :].astype(CDT)
        dp_ref[tm - BLOCK:tm, ATTN_W:QKV_W] = (dkv[tm - BLOCK:tm, :] + sp).astype(CDT)
        for (c0, wd), ref in zip(pieces, (dq_ref, dglu_ref, dgate_ref)):
            dp_ref[:, c0:c0 + wd] = ref[...]
        dh = _dot(dp_ref[...], w_ref[...])
        xv = x_ref[...]
        gv = g_ref[...]
        dxn, dg_rows = _rms_bwd(xv, gv, dh)
        dx_ref[...] = dr_ref[...] + dxn
        dg_ref[...] += _colsum(dg_rows)
        r = lax.rsqrt(jnp.mean(xv * xv, axis=-1, keepdims=True) + EPS)
        h_ref[...] = (xv * r * gv).astype(CDT)

    row = lambda i: (i, 0)
    return pl.pallas_call(
        body, name="inproj_bwd", grid=(nt,),
        in_specs=[pl.BlockSpec((tm, D), row), pl.BlockSpec((tm, D), row), _const((1, D)), _const((IN_W, D), 1),
                  pl.BlockSpec((tm, ATTN_W), row), pl.BlockSpec((tm, 2 * KV_W), row),
                  pl.BlockSpec((BLOCK, 2 * KV_W), lambda i: (jnp.minimum(i + 1, nt - 1), 0)),
                  pl.BlockSpec((tm, 2 * CONV_C), row), pl.BlockSpec((tm, 2 * D), row)]
        + [_const((1, wd)) for _, wd in pieces],
        out_specs=[pl.BlockSpec((tm, D), row), pl.BlockSpec((tm, IN_W), row), pl.BlockSpec((tm, D), row),
                   _const((1, D)), _const((1, IN_W))],
        out_shape=[jax.ShapeDtypeStruct((T, D), F32), jax.ShapeDtypeStruct((T, IN_W), CDT),
                   jax.ShapeDtypeStruct((T, D), CDT), jax.ShapeDtypeStruct((1, D), F32),
                   jax.ShapeDtypeStruct((1, IN_W), F32)],
        compiler_params=_cp("arbitrary"),
    )(dres, x, g, w, dq, dkv, spill, dglu, dgate, *sums)


ATTN_TILE = 256
MATRICES = ("w_in", "w_attn_proj", "w_conv_proj", "w_out", "w_mlp1", "w_mlp2")
SMALL = ("mix_norm_g", "b_in", "sinks", "conv_w", "conv_b", "conv_ln_g", "conv_ln_b", "b_conv_proj", "mlp_norm_g")


def forward_backward(x, tgt, hooks):
    def call(fn, kernel, l, *args, **kw):
        rider = hooks.rider(kernel, l)
        if rider is None:
            return fn(*args, **kw)
        outs, landed = fn(*args, rider=rider, **kw)
        hooks.landed(kernel, l, landed)
        return outs

    vec = hooks.vec
    saved = []
    for l in range(DEPTH):
        qkv, rest = call(rms_inproj, "rms_inproj", l, x, vec("mix_norm_g", l), hooks.w_in(l), vec("b_in", l))
        m = hooks.mats(l)
        attn, lse = attn_fwd(qkv, hooks.sinks(l), tq=ATTN_TILE)
        u3, u1 = conv_fwd(rest, hooks.taps(l), vec("conv_b", l), vec("conv_ln_g", l), vec("conv_ln_b", l))
        x1 = merge_out(x, attn, u3, rest, m["w_attn_proj"], m["w_conv_proj"], vec("b_conv_proj", l), m["w_out"])
        x2, pre = call(mlp_fwd, "mlp_fwd", l, x1, vec("mlp_norm_g", l), m["w_mlp1"], m["w_mlp2"])
        saved.append((x, qkv, rest, attn, lse, u3, u1, x1, pre))
        x = x2
    dx, dgf, loss = loss_head(x, hooks.final_g, tgt)
    small = {n: [None] * DEPTH for n in SMALL}
    small["final_norm_g"] = dgf
    for l in reversed(range(DEPTH)):
        x0, qkv, rest, attn, lse, u3, u1, x1, pre = saved[l]
        m = hooks.mats(l)
        dx1, dg2, h2, a, dpre = call(mlp_bwd, "mlp_bwd", l, dx, x1, vec("mlp_norm_g", l), pre, m["w_mlp1"], m["w_mlp2"])
        small["mlp_norm_g"][l] = dg2
        group = {}
        group["w_mlp1"] = call(tn_matmul, "tn_mlp1", l, h2, dpre, tm=1024, tn=1024, tk=4096, name="tn_mlp1", by_chip=True)
        group["w_mlp2"] = call(tn_matmul, "tn_mlp2", l, a, dx, tm=1024, tn=1024, name="tn_mlp2")
        merged, dba, dbc, dattn, du3, dgate, dgate_sum, dbcp = merge_bwd(
            dx1, attn, u3, rest, m["w_attn_proj"], m["w_conv_proj"], vec("b_conv_proj", l), m["w_out"])
        small["b_conv_proj"][l] = dbcp
        group["w_out"] = tn_matmul(merged, dx1, tm=1024, tn=1024, name="tn_out")
        group["w_attn_proj"] = tn_matmul(attn, dba, tm=512, tn=256, tk=4096, name="tn_attn_proj", by_chip=True)
        group["w_conv_proj"] = tn_matmul(u3, dbc, tm=512, tn=256, tk=4096, name="tn_conv_proj", by_chip=True)
        hooks.grads(l, "A", group)
        du1, dlg, dlb, dcb = call(conv_bwd_ln, "conv_bwd_ln", l, du3, u1, vec("conv_ln_g", l), vec("conv_ln_b", l))
        small["conv_ln_g"][l], small["conv_ln_b"][l], small["conv_b"][l] = dlg, dlb, dcb
        dglu, dglu_sum, dcw = conv_bwd_taps(du1, rest, hooks.taps(l))
        small["conv_w"][l] = dcw
        dq, dq_sum, dkv, spill, dsink = call(attn_bwd, "attn_bwd", l, qkv, dattn, lse, hooks.sinks(l), tq=ATTN_TILE)
        small["sinks"][l] = dsink
        dx, dproj, h, dg, db = inproj_bwd(dx1, x0, vec("mix_norm_g", l), hooks.w_in(l), dq, dkv, spill, dglu, dgate,
                                          (dq_sum, dglu_sum, dgate_sum), tm=ATTN_TILE)
        small["mix_norm_g"][l], small["b_in"][l] = dg, db
        hooks.grads(l, "B", {"w_in": call(tn_matmul, "tn_in", l, dproj, h, tm=768, tn=1024, tk=4096, name="tn_in")})
    return loss, dx, small


class _LocalHooks:
    def __init__(self, p):
        self.p = p
        self.final_g = p["final_norm_g"]
        self.got = {n: [None] * DEPTH for n in MATRICES}

    def w_in(self, l):
        return self.p["w_in"][l].T

    def mats(self, l):
        return {n: self.p[n][l] for n in MATRICES}

    def vec(self, n, l):
        return self.p[n][l]

    def sinks(self, l):
        return self.p["sinks"][l]

    def taps(self, l):
        return self.p["conv_w"][l]

    def rider(self, kernel, l):
        return None

    def grads(self, l, group, g):
        for n, v in g.items():
            if v.ndim == 3:
                v = v.transpose(1, 0, 2).reshape(v.shape[1], -1)
            self.got[n][l] = v.T if n == "w_in" else v


def local_grads(x, tgt, p):
    hooks = _LocalHooks(p)
    loss, dx, small = forward_backward(x, tgt, hooks)
    small["conv_w"] = [g[0:CONV_K] for g in small["conv_w"]]
    small["sinks"] = [g[0, 0:N_Q] for g in small["sinks"]]
    return loss, dx, {**small, **hooks.got}


MESH = pl.DeviceIdType.MESH
N_CHIPS = 4
N_DEV = 8
FLAT_W = 1024
FLAT_PARTS = (("w_in", 960), ("w_attn_proj", 128), ("w_conv_proj", 128), ("w_out", 256), ("w_mlp1", 1024), ("w_mlp2", 1024))
FLAT_ROWS = sum(r for _, r in FLAT_PARTS)
W_IN_ROWS = FLAT_PARTS[0][1]
GROUP_A = (("w_mlp1", 1024), ("w_mlp2", 1024), ("w_out", 256), ("w_attn_proj", 128), ("w_conv_proj", 128))
COL_SHARDED = ("w_in", "w_attn_proj", "w_conv_proj", "w_mlp1")
FULL_SHAPES = {"w_in": (D, IN_W), "w_attn_proj": (ATTN_W, D), "w_conv_proj": (CONV_C, D), "w_out": (D, D),
               "w_mlp1": (D, D_FF), "w_mlp2": (D_FF, D)}


def _place():
    x, y, c = lax.axis_index("x"), lax.axis_index("y"), lax.axis_index("c")
    return x, y, c, 2 * x + y


def _peer_chips(x, y, j):
    return [((x, 1 - y), j ^ 1), ((1 - x, y), j ^ 2), ((1 - x, 1 - y), j ^ 3)]


def _remote(src, dst, sems, k, n, to):
    return pltpu.make_async_remote_copy(src_ref=src, dst_ref=dst, send_sem=sems.at[k], recv_sem=sems.at[n + k],
                                        device_id=to, device_id_type=MESH)


def _half(c, rows):
    h = rows // 2
    return pl.ds(pl.multiple_of(c * h, 16), h)


def gather_rider(wsh):
    R = wsh.shape[0]
    n = 7

    def plan(rins, routs, sems):
        (w_ref,), (out_ref,) = rins, routs
        x, y, c, j = _place()
        peers = _peer_chips(x, y, j)
        mine, other = _half(c, R), _half(1 - c, R)
        sent = [_remote(w_ref.at[mine], out_ref.at[j, mine], sems, k, n, (*chip, c)) for k, (chip, _) in enumerate(peers)]
        sent.append(_remote(w_ref, out_ref.at[j], sems, 6, n, (x, y, 1 - c)))
        landed = [_remote(w_ref.at[mine], out_ref.at[pj, mine], sems, k, n, (x, y, c)) for k, (_, pj) in enumerate(peers)]
        passed = [_remote(out_ref.at[pj, mine], out_ref.at[pj, mine], sems, 3 + k, n, (x, y, 1 - c))
                  for k, (_, pj) in enumerate(peers)]
        handed = [_remote(w_ref.at[mine], out_ref.at[pj, other], sems, 3 + k, n, (x, y, c)) for k, (_, pj) in enumerate(peers)]
        handed.append(_remote(w_ref, out_ref.at[j], sems, 6, n, (x, y, c)))
        return sent, landed, passed, handed

    def start(rins, routs, sems):
        for cp in plan(rins, routs, sems)[0]:
            cp.start()

    def finish(rins, routs, sems):
        sent, landed, passed, handed = plan(rins, routs, sems)
        for k in range(3):
            landed[k].wait_recv()
            passed[k].start()
        for cp in handed:
            cp.wait_recv()
        for cp in sent + passed:
            cp.wait_send()

    return Rider((wsh,), (jax.ShapeDtypeStruct((N_CHIPS,) + wsh.shape, wsh.dtype),), 2 * n, start, finish)


def swap_rider(g):
    R = g.shape[1]

    def plan(rins, routs, sems):
        (g_ref,), (got_ref,) = rins, routs
        x, y, c, _ = _place()
        return _remote(g_ref.at[:, _half(1 - c, R), :], got_ref, sems, 0, 1, (x, y, 1 - c))

    def start(rins, routs, sems):
        plan(rins, routs, sems).start()

    def finish(rins, routs, sems):
        plan(rins, routs, sems).wait()

    return Rider((g,), (jax.ShapeDtypeStruct((N_CHIPS, R // 2, FLAT_W), g.dtype),), 2, start, finish)


def exchange_rider(pb):
    def plan(rins, routs, sems):
        (pb_ref,), (got_ref,) = rins, routs
        x, y, c, j = _place()
        peers = _peer_chips(x, y, j)
        sent = [_remote(pb_ref.at[pj], got_ref.at[j], sems, k, 3, (*chip, c)) for k, (chip, pj) in enumerate(peers)]
        landed = [_remote(pb_ref.at[pj], got_ref.at[pj], sems, k, 3, (x, y, c)) for k, (_, pj) in enumerate(peers)]
        return sent, landed

    def start(rins, routs, sems):
        for cp in plan(rins, routs, sems)[0]:
            cp.start()

    def finish(rins, routs, sems):
        sent, landed = plan(rins, routs, sems)
        for cp in landed:
            cp.wait_recv()
        for cp in sent:
            cp.wait_send()

    return Rider((pb,), (jax.ShapeDtypeStruct(pb.shape, pb.dtype),), 6, start, finish)


def share_rider(tot):
    def plan(rins, routs, sems):
        (t_ref,), (got_ref,) = rins, routs
        x, y, c, _ = _place()
        return _remote(t_ref, got_ref, sems, 0, 1, (x, y, 1 - c))

    def start(rins, routs, sems):
        plan(rins, routs, sems).start()

    def finish(rins, routs, sems):
        plan(rins, routs, sems).wait()

    return Rider((tot,), (jax.ShapeDtypeStruct(tot.shape, tot.dtype),), 2, start, finish)


def pair_sum(g, got):
    nj, R, W = g.shape
    h = R // 2
    tile = h // 2

    def body(g_ref, got_ref, pb_ref, own_ref):
        v = g_ref[...] + got_ref[...]
        pb_ref[...] = v.astype(pb_ref.dtype)

        @pl.when(pl.program_id(1) == _place()[3])
        def _():
            own_ref[...] = v

    return pl.pallas_call(
        body, name="pair_sum", grid=(h // tile, nj),
        in_specs=[pl.BlockSpec((None, tile, W), lambda r, j: (j, lax.axis_index("c") * (h // tile) + r, 0)),
                  pl.BlockSpec((None, tile, W), lambda r, j: (j, r, 0))],
        out_specs=[pl.BlockSpec((None, tile, W), lambda r, j: (j, r, 0)), pl.BlockSpec((tile, W), lambda r, j: (r, 0))],
        out_shape=[jax.ShapeDtypeStruct((nj, h, W), CDT), jax.ShapeDtypeStruct((h, W), F32)],
        compiler_params=_cp("arbitrary", "arbitrary"),
    )(g, got)


def total_sum(own, got):
    R, W = own.shape
    tile = R // 2

    def body(own_ref, a_ref, b_ref, c_ref, o_ref):
        o_ref[...] = ((own_ref[...] + a_ref[...].astype(F32)) + b_ref[...].astype(F32)) + c_ref[...].astype(F32)

    def slab(k):
        return pl.BlockSpec((None, tile, W), lambda r: (_place()[3] ^ (k + 1), r, 0))

    return pl.pallas_call(
        body, name="total_sum", grid=(R // tile,),
        in_specs=[pl.BlockSpec((tile, W), lambda r: (r, 0)), slab(0), slab(1), slab(2)],
        out_specs=pl.BlockSpec((tile, W), lambda r: (r, 0)),
        out_shape=jax.ShapeDtypeStruct((R, W), F32),
        compiler_params=_cp("arbitrary"),
    )(own, got, got, got)


def _all_peers(x, y, c):
    return [(x ^ (r >> 2), y ^ ((r >> 1) & 1), c ^ (r & 1)) for r in range(1, N_DEV)]


ROW_ITEMS = (("mix_norm_g", D), ("b_in", IN_W), ("sinks", N_Q), ("conv_b", CONV_C), ("conv_ln_g", CONV_C),
             ("conv_ln_b", CONV_C), ("b_conv_proj", D), ("mlp_norm_g", D))
TAPS_ROW = 16
TAPS_ROWS = 32
LAYER_ROWS = TAPS_ROW + TAPS_ROWS
FINAL_ROW = DEPTH * LAYER_ROWS
SMALL_ROWS = FINAL_ROW + SUBLANES


def _row_chunks():
    out, r = {}, 0
    for n, width in ROW_ITEMS:
        out[n] = [(r + i, FLAT_W * i, min(FLAT_W, width - FLAT_W * i)) for i in range(-(-width // FLAT_W))]
        r += len(out[n])
    assert r <= TAPS_ROW
    return out


def sum_small(gsm):
    chunks = _row_chunks()
    ins = []
    for l in range(DEPTH):
        ins += [gsm[n][l] for n, _ in ROW_ITEMS] + [gsm["conv_w"][l]]
    ins.append(gsm["final_norm_g"])
    n_in = len(ins)

    def body(*refs):
        in_refs, o_ref, buf, send_sems, recv_sems = refs[:n_in], refs[n_in], refs[n_in + 1], refs[n_in + 2], refs[n_in + 3]
        x, y, c, _ = _place()
        me = 4 * x + 2 * y + c
        mine = buf.at[me]
        mine[...] = jnp.zeros((SMALL_ROWS, FLAT_W), F32)
        k = 0
        for l in range(DEPTH):
            for n, _ in ROW_ITEMS:
                for r, c0, wd in chunks[n]:
                    mine[l * LAYER_ROWS + r:l * LAYER_ROWS + r + 1, 0:wd] = in_refs[k][:, c0:c0 + wd]
                k += 1
            mine[l * LAYER_ROWS + TAPS_ROW:(l + 1) * LAYER_ROWS, 0:CONV_C] = in_refs[k][...]
            k += 1
        mine[FINAL_ROW:FINAL_ROW + 1, :] = in_refs[k][...]
        peers = _all_peers(x, y, c)
        sends = [pltpu.make_async_remote_copy(src_ref=mine, dst_ref=mine, send_sem=send_sems.at[r], recv_sem=recv_sems.at[r],
                                              device_id=to, device_id_type=MESH) for r, to in enumerate(peers)]
        for cp in sends:
            cp.start()
        for r in range(N_DEV - 1):
            pltpu.make_async_remote_copy(src_ref=mine, dst_ref=buf.at[me ^ (r + 1)], send_sem=send_sems.at[r],
                                         recv_sem=recv_sems.at[r], device_id=(x, y, c), device_id_type=MESH).wait_recv()
        for cp in sends:
            cp.wait_send()
        acc = buf[0]
        for d in range(1, N_DEV):
            acc = acc + buf[d]
        o_ref[...] = acc

    vm = pl.BlockSpec(memory_space=pltpu.VMEM)
    return pl.pallas_call(
        body, name="sum_small", out_shape=jax.ShapeDtypeStruct((SMALL_ROWS, FLAT_W), F32),
        in_specs=[vm] * n_in, out_specs=vm,
        scratch_shapes=[pltpu.VMEM((N_DEV, SMALL_ROWS, FLAT_W), F32), pltpu.SemaphoreType.DMA((N_DEV - 1,)),
                        pltpu.SemaphoreType.DMA((N_DEV - 1,))],
    )(*ins)


def gather_taps(taps):
    shard = taps.shape[2]

    def body(t_ref, o_ref, buf, send_sems, recv_sems):
        x, y, c, j = _place()
        peers = _peer_chips(x, y, j)
        buf[j] = t_ref[...]
        sends = [pltpu.make_async_remote_copy(src_ref=t_ref, dst_ref=buf.at[j], send_sem=send_sems.at[k],
                                              recv_sem=recv_sems.at[k], device_id=(*chip, c), device_id_type=MESH)
                 for k, (chip, _) in enumerate(peers)]
        for cp in sends:
            cp.start()
        for k, (_, pj) in enumerate(peers):
            pltpu.make_async_remote_copy(src_ref=t_ref, dst_ref=buf.at[pj], send_sem=send_sems.at[k],
                                         recv_sem=recv_sems.at[k], device_id=(x, y, c), device_id_type=MESH).wait_recv()
        for cp in sends:
            cp.wait_send()
        for jj in range(N_CHIPS):
            o_ref[:, :, jj * shard:(jj + 1) * shard] = buf[jj]

    vm = pl.BlockSpec(memory_space=pltpu.VMEM)
    return pl.pallas_call(
        body, name="gather_taps", out_shape=jax.ShapeDtypeStruct(taps.shape[:2] + (N_CHIPS * shard,), taps.dtype),
        in_specs=[vm], out_specs=vm,
        scratch_shapes=[pltpu.VMEM((N_CHIPS,) + taps.shape, taps.dtype), pltpu.SemaphoreType.DMA((3,)),
                        pltpu.SemaphoreType.DMA((3,))],
    )(taps)


def _adam_math(w, g, m, v):
    nm = ADAM_B1 * m + (1.0 - ADAM_B1) * g
    nv = ADAM_B2 * v + (1.0 - ADAM_B2) * jnp.square(g)
    m_hat = nm / (1.0 - ADAM_B1 ** ADAM_STEP)
    v_hat = nv / (1.0 - ADAM_B2 ** ADAM_STEP)
    return -ADAM_LR * (m_hat / (jnp.sqrt(v_hat) + ADAM_EPS) + ADAM_WD * w), nm, nv


def adamw(w, g, m, v, *, name):
    L, R, C = w.shape
    tr = next(t for t in (512, 480, 256, 128) if R % t == 0)

    def body(w_ref, g_ref, m_ref, v_ref, d_ref, nm_ref, nv_ref):
        d_ref[...], nm_ref[...], nv_ref[...] = _adam_math(w_ref[...], g_ref[...], m_ref[...], v_ref[...])

    spec = pl.BlockSpec((None, tr, C), lambda l, i: (l, i, 0))
    out = jax.ShapeDtypeStruct((L, R, C), F32)
    return pl.pallas_call(
        body, name=name, grid=(L, R // tr), in_specs=[spec] * 4, out_specs=[spec] * 3, out_shape=[out] * 3,
        compiler_params=_cp("parallel", "parallel"),
    )(w, g, m, v)


def adamw_small(packed, w, m, v):
    chunks = _row_chunks()
    names = SMALL + ("final_norm_g",)
    as_2d = lambda a: a.reshape(1, -1) if a.ndim == 1 else a
    ins = [as_2d(t[n]) for n in names for t in (w, m, v)]
    shapes = [jax.ShapeDtypeStruct(as_2d(w[n]).shape, F32) for n in names for _ in range(4)]
    n_in = len(ins)

    def body(p_ref, *refs):
        in_refs, out_refs = refs[:n_in], refs[n_in:]
        chip = _place()[3]
        for i, n in enumerate(names):
            w_ref, m_ref, v_ref = in_refs[3 * i:3 * i + 3]
            outs = out_refs[4 * i:4 * i + 4]

            def step(at, g):
                res = (g,) + _adam_math(w_ref[at], g, m_ref[at], v_ref[at])
                for o_ref, val in zip(outs, res):
                    o_ref[at] = val

            if n == "final_norm_g":
                step((slice(None), slice(None)), p_ref[FINAL_ROW:FINAL_ROW + 1, :])
                continue
            for l in range(DEPTH):
                if n == "conv_w":
                    r0 = l * LAYER_ROWS + TAPS_ROW
                    shard = CONV_C // N_CHIPS
                    g = jnp.zeros((CONV_K, shard), F32)
                    for j in range(N_CHIPS):
                        g = jnp.where(chip == j, p_ref[r0:r0 + CONV_K, j * shard:(j + 1) * shard], g)
                    step((l,), g)
                else:
                    for r, c0, wd in chunks[n]:
                        step((slice(l, l + 1), slice(c0, c0 + wd)),
                             p_ref[l * LAYER_ROWS + r:l * LAYER_ROWS + r + 1, 0:wd])

    vm = pl.BlockSpec(memory_space=pltpu.VMEM)
    res = pl.pallas_call(
        body, name="adamw_small", out_shape=shapes,
        in_specs=[vm] + [vm] * n_in, out_specs=[vm] * len(shapes),
    )(packed, *ins)
    dicts = ({}, {}, {}, {})
    for i, n in enumerate(names):
        for d, val in zip(dicts, res[4 * i:4 * i + 4]):
            d[n] = val.reshape(w[n].shape)
    return dicts


def _flat_rows(name, shard):
    return shard.T if name == "w_in" else shard.reshape(-1, FLAT_W)


def _full_matrix(slabs, name):
    K, N = FULL_SHAPES[name]
    if name == "w_in":
        return slabs.reshape(N, K)
    if name in COL_SHARDED:
        return slabs.reshape(N_CHIPS, K, N // N_CHIPS).transpose(1, 0, 2).reshape(K, N)
    return slabs.reshape(K, N)


def _first_row(parts, name):
    r = 0
    for n, rows in parts:
        if n == name:
            return r, rows
        r += rows
    raise KeyError(name)


class _Exchange:
    CARRIERS = {
        ("conv_bwd_ln", 1): ((1, "A"), "swap"), ("attn_bwd", 1): ((1, "A"), "exchange"), ("tn_in", 1): ((1, "A"), "share"),
        ("mlp_bwd", 0): ((1, "B"), "swap"), ("tn_mlp1", 0): ((1, "B"), "exchange"), ("tn_mlp2", 0): ((1, "B"), "share"),
        ("conv_bwd_ln", 0): ((0, "A"), "swap"), ("attn_bwd", 0): ((0, "A"), "exchange"), ("tn_in", 0): ((0, "A"), "share"),
    }

    def __init__(self, w, ci, chip):
        self.w, self.ci, self.chip = w, ci, chip
        self.wsh = [jnp.concatenate([_flat_rows(n, w[n][l]) for n, _ in FLAT_PARTS], axis=0).astype(CDT)
                    for l in range(DEPTH)]
        self.final_g = w["final_norm_g"].reshape(1, D)
        self.slabs = {}
        self.full = {}
        self.units = {}
        self.reduced = {}
        self._landed_weights(0, 0, _run_alone(gather_rider(self.wsh[0][:W_IN_ROWS]), "gather_w_in")[0])
        self.all_taps = gather_taps(w["conv_w"])

    def _landed_weights(self, l, r0, buf):
        self.slabs.setdefault(l, []).append((r0, buf))

    def _matrix(self, l, name):
        if (l, name) not in self.full:
            r, rows = _first_row(FLAT_PARTS, name)
            r0, buf = next((r0, buf) for r0, buf in self.slabs[l] if r0 <= r < r0 + buf.shape[1])
            self.full[(l, name)] = _full_matrix(buf[:, r - r0:r - r0 + rows], name)
        return self.full[(l, name)]

    def w_in(self, l):
        return self._matrix(l, "w_in")

    def mats(self, l):
        return {n: self._matrix(l, n) for n in MATRICES if n != "w_in"}

    def vec(self, n, l):
        return self.w[n][l].reshape(1, -1)

    def sinks(self, l):
        return self.w["sinks"][l]

    def taps(self, l):
        return self.all_taps[l]

    def rider(self, kernel, l):
        if (kernel, l) == ("rms_inproj", 0):
            return gather_rider(self.wsh[0][W_IN_ROWS:])
        if (kernel, l) == ("mlp_fwd", 0):
            return gather_rider(self.wsh[1])
        if (kernel, l) in self.CARRIERS:
            return self._stage(*self.CARRIERS[(kernel, l)])
        return None

    def landed(self, kernel, l, bufs):
        if (kernel, l) == ("rms_inproj", 0):
            self._landed_weights(0, W_IN_ROWS, bufs[0])
        elif (kernel, l) == ("mlp_fwd", 0):
            self._landed_weights(1, 0, bufs[0])
        else:
            self._stage_landed(*self.CARRIERS[(kernel, l)], bufs[0])

    def grads(self, l, group, g):
        if group == "A":
            flat = jnp.concatenate([g[n].reshape(N_CHIPS, rows, FLAT_W) for n, rows in GROUP_A], axis=1)
        else:
            flat = g["w_in"].reshape(N_CHIPS, W_IN_ROWS, FLAT_W)
        self.units[(l, group)] = {"g": flat}

    def _stage(self, key, stage):
        u = self.units[key]
        if stage == "swap":
            return swap_rider(u["g"])
        if stage == "exchange":
            u["pb"], u["own"] = pair_sum(u["g"], u["swap"])
            return exchange_rider(u["pb"])
        u["tot"] = total_sum(u["own"], u["exchange"])
        return share_rider(u["tot"])

    def _stage_landed(self, key, stage, buf):
        u = self.units[key]
        u[stage] = buf
        if stage == "share":
            tot = u["tot"]
            self.reduced[key] = jnp.where(self.ci == 0, jnp.concatenate([tot, buf]), jnp.concatenate([buf, tot]))

    def finish(self):
        key = (0, "B")
        for stage in ("swap", "exchange", "share"):
            self._stage_landed(key, stage, _run_alone(self._stage(key, stage), stage + "_last")[0])
        out = {}
        for n in MATRICES:
            per_layer = []
            for l in range(DEPTH):
                if n == "w_in":
                    per_layer.append(self.reduced[(l, "B")].T)
                    continue
                r, rows = _first_row(GROUP_A, n)
                per_layer.append(self.reduced[(l, "A")][r:r + rows].reshape(self.w[n].shape[1:]))
            out[n] = jnp.stack(per_layer)
        return out


WEIGHTS = ("mix_norm_g", "w_in", "b_in", "sinks", "conv_w", "conv_b", "conv_ln_g", "conv_ln_b", "w_attn_proj",
           "w_conv_proj", "b_conv_proj", "w_out", "mlp_norm_g", "w_mlp1", "w_mlp2", "final_norm_g")


def kernel(x, mix_norm_g, w_in, b_in, sinks, conv_w, conv_b, conv_ln_g, conv_ln_b, w_attn_proj, w_conv_proj, b_conv_proj, w_out, mlp_norm_g, w_mlp1, w_mlp2, final_norm_g, loss_target, m_mix_norm_g, m_w_in, m_b_in, m_sinks, m_conv_w, m_conv_b, m_conv_ln_g, m_conv_ln_b, m_w_attn_proj, m_w_conv_proj, m_b_conv_proj, m_w_out, m_mlp_norm_g, m_w_mlp1, m_w_mlp2, m_final_norm_g, v_mix_norm_g, v_w_in, v_b_in, v_sinks, v_conv_w, v_conv_b, v_conv_ln_g, v_conv_ln_b, v_w_attn_proj, v_w_conv_proj, v_b_conv_proj, v_w_out, v_mlp_norm_g, v_w_mlp1, v_w_mlp2, v_final_norm_g):
    w = dict(zip(WEIGHTS, (mix_norm_g, w_in, b_in, sinks, conv_w, conv_b, conv_ln_g, conv_ln_b, w_attn_proj, w_conv_proj,
                           b_conv_proj, w_out, mlp_norm_g, w_mlp1, w_mlp2, final_norm_g)))
    m = dict(zip(WEIGHTS, (m_mix_norm_g, m_w_in, m_b_in, m_sinks, m_conv_w, m_conv_b, m_conv_ln_g, m_conv_ln_b, m_w_attn_proj,
                           m_w_conv_proj, m_b_conv_proj, m_w_out, m_mlp_norm_g, m_w_mlp1, m_w_mlp2, m_final_norm_g)))
    v = dict(zip(WEIGHTS, (v_mix_norm_g, v_w_in, v_b_in, v_sinks, v_conv_w, v_conv_b, v_conv_ln_g, v_conv_ln_b, v_w_attn_proj,
                           v_w_conv_proj, v_b_conv_proj, v_w_out, v_mlp_norm_g, v_w_mlp1, v_w_mlp2, v_final_norm_g)))
    xi, yi, ci = lax.axis_index("x"), lax.axis_index("y"), lax.axis_index("c")
    chip = 2 * xi + yi

    hooks = _Exchange(w, ci, chip)
    loss, dx, gsm = forward_backward(x[0], loss_target[0], hooks)
    loss = lax.psum(loss[0, 0], ("x", "y", "c"))
    grads = hooks.finish()

    gsmall, delta, new_m, new_v = adamw_small(sum_small(gsm), w, m, v)
    grads.update(gsmall)
    for n in MATRICES:
        t = (lambda a: jnp.swapaxes(a, 1, 2)) if n == "w_in" else (lambda a: a)
        delta[n], new_m[n], new_v[n] = map(t, adamw(t(w[n]), t(grads[n]), t(m[n]), t(v[n]), name="adamw_" + n))

    return (loss, dx[None], *[grads[n] for n in WEIGHTS], *[delta[n] for n in WEIGHTS],
            *[new_m[n] for n in WEIGHTS], *[new_v[n] for n in WEIGHTS])
```

```python
import functools
import math
from typing import Callable, NamedTuple

import jax
import jax.numpy as jnp
import numpy as np
from jax import lax
from jax.experimental import pallas as pl
from jax.experimental.pallas import tpu as pltpu

F32 = jnp.float32
CDT = jnp.bfloat16

D = 1024
DEPTH = 2
N_Q = 8
HEAD_DIM = 64
ATTN_W = 512
KV_W = 128
BLOCK = 128
CONV_C = 512
CONV_K = 31
D_FF = 4096
IN_W = 3840
QKV_W = ATTN_W + 2 * KV_W
REST_W = IN_W - QKV_W
EPS = 1e-6
NEG = -1e30
SCALE = 1.0 / math.sqrt(HEAD_DIM)
SLOPES = [float(2.0 ** (-8.0 * (h + 1) / N_Q)) for h in range(N_Q)]
SUBLANES = 8
HALO = 32

ADAM_LR = 0.001
ADAM_B1 = 0.9
ADAM_B2 = 0.999
ADAM_EPS = 1e-08
ADAM_WD = 0.01
ADAM_STEP = 10

VMEM_LIMIT = 56 * 1024 * 1024


def _cp(*sem):
    return pltpu.CompilerParams(dimension_semantics=sem, vmem_limit_bytes=VMEM_LIMIT)


def _dot(a, b):
    return jnp.dot(a, b, preferred_element_type=F32)


def _dot_nt(a, b):
    return lax.dot_general(a, b, (((1,), (1,)), ((), ())), preferred_element_type=F32)


def _dot_tn(a, b):
    return lax.dot_general(a, b, (((0,), (0,)), ((), ())), preferred_element_type=F32)


def _sig(x):
    return 1.0 / (1.0 + jnp.exp(-x))


def _colsum(v):
    return jnp.sum(v, axis=0, keepdims=True)


def _const(shape, buffers=None):
    mode = {} if buffers is None else {"pipeline_mode": pl.Buffered(buffers)}
    return pl.BlockSpec(shape, lambda *_: (0,) * len(shape), **mode)


class Rider(NamedTuple):
    ins: tuple
    outs: tuple
    n_sems: int
    start: Callable
    finish: Callable


def _any():
    return pl.BlockSpec(memory_space=pl.ANY)


def _run(body, rider, *, name, grid, in_specs, out_specs, out_shape, args, sem, scratch_shapes=()):
    if rider is None:
        return pl.pallas_call(body, name=name, grid=grid, in_specs=list(in_specs), out_specs=list(out_specs),
                              out_shape=list(out_shape), scratch_shapes=list(scratch_shapes),
                              compiler_params=_cp(*sem))(*args)
    n_in, n_out, n_sc = len(in_specs), len(out_specs), len(scratch_shapes)
    r_in, r_out = len(rider.ins), len(rider.outs)

    def riding(*refs):
        ins, rins = refs[:n_in], refs[n_in:n_in + r_in]
        o0 = n_in + r_in
        outs, routs = refs[o0:o0 + n_out], refs[o0 + n_out:o0 + n_out + r_out]
        s0 = o0 + n_out + r_out
        scratch, sems = refs[s0:s0 + n_sc], refs[s0 + n_sc]
        first = functools.reduce(jnp.logical_and, [pl.program_id(a) == 0 for a in range(len(grid))])
        last = functools.reduce(jnp.logical_and, [pl.program_id(a) == grid[a] - 1 for a in range(len(grid))])

        @pl.when(first)
        def _():
            rider.start(rins, routs, sems)

        body(*ins, *outs, *scratch)

        @pl.when(last)
        def _():
            rider.finish(rins, routs, sems)

    res = pl.pallas_call(
        riding, name=name, grid=grid, in_specs=list(in_specs) + [_any()] * r_in,
        out_specs=list(out_specs) + [_any()] * r_out, out_shape=list(out_shape) + list(rider.outs),
        scratch_shapes=list(scratch_shapes) + [pltpu.SemaphoreType.DMA((rider.n_sems,))],
        compiler_params=_cp(*["arbitrary"] * len(grid)))(*args, *rider.ins)
    return res[:n_out], res[n_out:]


def _run_alone(rider, name):
    def body(*refs):
        r_in, r_out = len(rider.ins), len(rider.outs)
        rins, routs, sems = refs[:r_in], refs[r_in:r_in + r_out], refs[r_in + r_out]
        rider.start(rins, routs, sems)
        rider.finish(rins, routs, sems)

    return pl.pallas_call(
        body, name=name, in_specs=[_any()] * len(rider.ins), out_specs=[_any()] * len(rider.outs),
        out_shape=list(rider.outs), scratch_shapes=[pltpu.SemaphoreType.DMA((rider.n_sems,))])(*rider.ins)


def rms_inproj(x, g, wt, b, *, tm=512, rider=None):
    T = x.shape[0]

    def body(x_ref, g_ref, w_ref, b_ref, qkv_ref, rest_ref):
        xv = x_ref[...]
        r = lax.rsqrt(jnp.mean(xv * xv, axis=-1, keepdims=True) + EPS)
        h = (xv * r * g_ref[...]).astype(CDT)
        qkv_ref[...] = (_dot_nt(h, w_ref[0:QKV_W, :]) + b_ref[:, 0:QKV_W]).astype(qkv_ref.dtype)
        for j in range(REST_W // D):
            c0 = QKV_W + D * j
            rest_ref[:, D * j:D * (j + 1)] = _dot_nt(h, w_ref[c0:c0 + D, :]) + b_ref[:, c0:c0 + D]

    return _run(
        body, rider, name="rms_inproj", grid=(T // tm,),
        in_specs=[pl.BlockSpec((tm, D), lambda i: (i, 0)), _const((1, D)), _const((IN_W, D), 1), _const((1, IN_W))],
        out_specs=[pl.BlockSpec((tm, QKV_W), lambda i: (i, 0)), pl.BlockSpec((tm, REST_W), lambda i: (i, 0))],
        out_shape=[jax.ShapeDtypeStruct((T, QKV_W), CDT), jax.ShapeDtypeStruct((T, REST_W), F32)],
        sem=("parallel",), args=(x, g, wt, b))


def _lane_halves(shape):
    lane = lax.broadcasted_iota(jnp.int32, shape, 1)
    return lane < HEAD_DIM, lane >= HEAD_DIM


def _swap_halves(v):
    return pltpu.roll(v.astype(F32), HEAD_DIM, axis=1).astype(v.dtype)


N_KV = KV_W // HEAD_DIM
GROUP = N_Q // N_KV
STACK = GROUP * BLOCK


def _score_bias():
    row = np.arange(STACK)[:, None] % BLOCK
    col = np.arange(2 * BLOCK)[None, :]
    dist = row + BLOCK - col
    window = (dist >= 0) & (dist < BLOCK)
    slopes = np.asarray(SLOPES, np.float32).reshape(N_KV, GROUP)
    out = np.empty((2, N_KV, STACK, 2 * BLOCK), np.float32)
    for first in range(2):
        valid = window & ((col >= BLOCK) | (first == 0))
        for g in range(N_KV):
            slope = np.repeat(slopes[g], BLOCK)[:, None]
            out[first, g] = np.where(valid, -(slope * dist.astype(np.float32)), np.float32(NEG))
    return jnp.asarray(out)


def _per_head_column(vals):
    row = lax.broadcasted_iota(jnp.int32, (STACK, 1), 0)
    col = jnp.full((STACK, 1), vals[GROUP - 1], F32)
    for i in reversed(range(GROUP - 1)):
        col = jnp.where(row < (i + 1) * BLOCK, vals[i], col)
    return col


def _stack_heads(dst, src_ref, r0, g, scale=None):
    lane = lax.broadcasted_iota(jnp.int32, (BLOCK, 2 * HEAD_DIM), 1)
    keep = (lane >= HEAD_DIM) if g else (lane < HEAD_DIM)
    for i in range(GROUP):
        h = GROUP * g + i
        tile = src_ref[pl.ds(r0, BLOCK), (h // 2) * 128:(h // 2 + 1) * 128]
        if h % 2 != g:
            tile = _swap_halves(tile)
        if scale is not None:
            tile = tile * jnp.asarray(scale, tile.dtype)
        dst[i * BLOCK:(i + 1) * BLOCK, :] = jnp.where(keep, tile, jnp.zeros_like(tile))


def _unstack_heads(dst_ref, stacked, r0, g):
    lane = lax.broadcasted_iota(jnp.int32, (BLOCK, 2 * HEAD_DIM), 1)
    tiles = []
    for j in range(GROUP // 2):
        even = stacked[(2 * j) * BLOCK:(2 * j + 1) * BLOCK, :]
        odd = stacked[(2 * j + 1) * BLOCK:(2 * j + 2) * BLOCK, :]
        lo = _swap_halves(even) if g else even
        hi = odd if g else _swap_halves(odd)
        c0 = ((GROUP * g) // 2 + j) * 128
        tile = jnp.where(lane < HEAD_DIM, lo, hi)
        dst_ref[pl.ds(r0, BLOCK), c0:c0 + 128] = tile.astype(dst_ref.dtype)
        tiles.append((c0, tile))
    return tiles


def _qkv_specs(tq):
    nb = tq // BLOCK
    return [
        pl.BlockSpec((tq, ATTN_W), lambda i: (i, 0)),
        pl.BlockSpec((BLOCK, KV_W), lambda i: (jnp.maximum(i * nb - 1, 0), ATTN_W // KV_W)),
        pl.BlockSpec((tq, KV_W), lambda i: (i, ATTN_W // KV_W)),
        pl.BlockSpec((BLOCK, KV_W), lambda i: (jnp.maximum(i * nb - 1, 0), ATTN_W // KV_W + 1)),
        pl.BlockSpec((tq, KV_W), lambda i: (i, ATTN_W // KV_W + 1)),
    ]


def attn_fwd(qkv, sinks, *, tq=512):
    T = qkv.shape[0]
    nb = tq // BLOCK

    def body(sink_ref, bias_ref, q_ref, kp_ref, kc_ref, vp_ref, vc_ref, o_ref, lse_ref, kext, vext, qs):
        i = pl.program_id(0)
        kext[0:BLOCK, :] = kp_ref[...]
        kext[BLOCK:, :] = kc_ref[...]
        vext[0:BLOCK, :] = vp_ref[...]
        vext[BLOCK:, :] = vc_ref[...]
        lane_l = lax.broadcasted_iota(jnp.int32, (BLOCK, 128), 1)

        def blk(b, carry):
            r0 = pl.multiple_of(b * BLOCK, BLOCK)
            first = jnp.logical_and(i == 0, b == 0).astype(jnp.int32)
            kc = kext[pl.ds(r0, 2 * BLOCK), :]
            vc = vext[pl.ds(r0, 2 * BLOCK), :]
            lse_t = jnp.zeros((BLOCK, 128), F32)
            for g in range(N_KV):
                heads = range(GROUP * g, GROUP * (g + 1))
                _stack_heads(qs.at[g], q_ref, r0, g, SCALE)
                s = _dot_nt(qs[g], kc) + bias_ref[first, g]
                sink = _per_head_column([sink_ref[h] for h in heads])
                m = jnp.maximum(jnp.max(s, axis=-1, keepdims=True), sink)
                p = jnp.exp(s - m)
                denom = jnp.sum(p, axis=-1, keepdims=True) + jnp.exp(sink - m)
                p = p / denom
                _unstack_heads(o_ref, _dot(p.astype(CDT), vc), r0, g)
                lse = m + jnp.log(denom)
                for i_h, h in enumerate(heads):
                    lse_t = jnp.where(lane_l == h, lse[i_h * BLOCK:(i_h + 1) * BLOCK, :], lse_t)
            lse_ref[pl.ds(r0, BLOCK), :] = lse_t
            return carry

        lax.fori_loop(0, nb, blk, 0)

    return pl.pallas_call(
        body, name="attn_fwd", grid=(T // tq,),
        in_specs=[pl.BlockSpec(memory_space=pltpu.SMEM), _const((2, N_KV, STACK, 2 * BLOCK), 1)] + _qkv_specs(tq),
        out_specs=[pl.BlockSpec((tq, ATTN_W), lambda i: (i, 0)), pl.BlockSpec((tq, 128), lambda i: (i, 0))],
        out_shape=[jax.ShapeDtypeStruct((T, ATTN_W), CDT), jax.ShapeDtypeStruct((T, 128), F32)],
        scratch_shapes=[pltpu.VMEM((tq + BLOCK, KV_W), CDT), pltpu.VMEM((tq + BLOCK, KV_W), CDT),
                        pltpu.VMEM((N_KV, STACK, 2 * HEAD_DIM), CDT)],
        compiler_params=_cp("parallel"),
    )(sinks, _score_bias(), qkv, qkv, qkv, qkv, qkv)


def _halo_before(tm, width, col):
    return pl.BlockSpec((HALO, width), lambda i: (jnp.maximum(i * (tm // HALO) - 1, 0), col))


def _fill_u0(ext, a_ref, b_ref, ha_ref, hb_ref, first):
    hu = ha_ref[...] * _sig(hb_ref[...])
    ext[0:HALO, :] = jnp.where(first, jnp.zeros_like(hu), hu)
    ext[HALO:, :] = a_ref[...] * _sig(b_ref[...])


def _shifted_taps(src, w_ref, base, rc, offsets):
    acc = jnp.zeros((rc, CONV_C), F32)
    for b in range(SUBLANES):
        taps = [(k, o - b) for k, o in enumerate(offsets) if o % SUBLANES == b]
        if not taps:
            continue
        rows = rc if b == 0 else rc + SUBLANES
        part = jnp.zeros((rows, CONV_C), F32)
        for k, o8 in taps:
            part = part + w_ref[k:k + 1, :] * src[base + o8:base + o8 + rows, :]
        acc = acc + (part if b == 0 else part[b:b + rc, :])
    return acc


def _conv_rows(ext, w_ref, r0, rc):
    return _shifted_taps(ext, w_ref, r0, rc, [HALO - (CONV_K - 1) + k for k in range(CONV_K)])


def _layer_norm(u1, g, b):
    mu = jnp.mean(u1, axis=-1, keepdims=True)
    xc = u1 - mu
    rstd = lax.rsqrt(jnp.mean(xc * xc, axis=-1, keepdims=True) + EPS)
    n = xc * rstd
    return n, rstd, n * g + b


CONV_RC = 32
CONV_RC_1PASS = 64


def conv_fwd(rest, cw, cb, lg, lb, *, tm=512):
    T = rest.shape[0]

    def body(a_ref, b_ref, ha_ref, hb_ref, w_ref, cb_ref, lg_ref, lb_ref, o_ref, u1_ref, ext):
        _fill_u0(ext, a_ref, b_ref, ha_ref, hb_ref, pl.program_id(0) == 0)
        rc = CONV_RC_1PASS
        for r0 in range(0, tm, rc):
            u1 = _conv_rows(ext, w_ref, r0, rc) + cb_ref[...]
            u1_ref[r0:r0 + rc, :] = u1
            _, _, u2 = _layer_norm(u1, lg_ref[...], lb_ref[...])
            o_ref[r0:r0 + rc, :] = (u2 * _sig(u2)).astype(o_ref.dtype)

    row = lambda i: (i, 0)
    return pl.pallas_call(
        body, name="conv_fwd", grid=(T // tm,),
        in_specs=[pl.BlockSpec((tm, CONV_C), row), pl.BlockSpec((tm, CONV_C), lambda i: (i, 1)),
                  _halo_before(tm, CONV_C, 0), _halo_before(tm, CONV_C, 1),
                  _const((CONV_K, CONV_C)), _const((1, CONV_C)), _const((1, CONV_C)), _const((1, CONV_C))],
        out_specs=[pl.BlockSpec((tm, CONV_C), row), pl.BlockSpec((tm, CONV_C), row)],
        out_shape=[jax.ShapeDtypeStruct((T, CONV_C), CDT), jax.ShapeDtypeStruct((T, CONV_C), F32)],
        scratch_shapes=[pltpu.VMEM((tm + HALO, CONV_C), F32)],
        compiler_params=_cp("parallel"),
    )(rest, rest, rest, rest, cw, cb, lg, lb)


def merge_out(x, attn, u3, rest, wa, wc, bc, wo, *, tm=512):
    T = x.shape[0]

    def body(x_ref, at_ref, u_ref, ga_ref, gc_ref, wa_ref, wc_ref, bc_ref, wo_ref, o_ref):
        br_a = _dot(at_ref[...], wa_ref[...])
        br_c = _dot(u_ref[...], wc_ref[...]) + bc_ref[...]
        merged = _sig(ga_ref[...]) * br_a + _sig(gc_ref[...]) * br_c
        o_ref[...] = x_ref[...] + _dot(merged.astype(CDT), wo_ref[...])

    return pl.pallas_call(
        body, name="merge_out", grid=(T // tm,),
        in_specs=[pl.BlockSpec((tm, D), lambda i: (i, 0)), pl.BlockSpec((tm, ATTN_W), lambda i: (i, 0)),
                  pl.BlockSpec((tm, CONV_C), lambda i: (i, 0)),
                  pl.BlockSpec((tm, D), lambda i: (i, 1)), pl.BlockSpec((tm, D), lambda i: (i, 2)),
                  _const((ATTN_W, D), 1), _const((CONV_C, D), 1), _const((1, D)), _const((D, D), 1)],
        out_specs=pl.BlockSpec((tm, D), lambda i: (i, 0)),
        out_shape=jax.ShapeDtypeStruct((T, D), F32),
        compiler_params=_cp("parallel"),
    )(x, attn, u3, rest, rest, wa, wc, bc, wo)


def _loss_and_grad(xv, gv, tgt):
    r = lax.rsqrt(jnp.mean(xv * xv, axis=-1, keepdims=True) + EPS)
    e = xv * r * gv - tgt
    dx, dg_rows = _rms_bwd(xv, gv, e * (1.0 / D))
    return 0.5 * jnp.mean(e * e, axis=-1, keepdims=True), dx, dg_rows


def mlp_fwd(x, g, w1, w2, *, head=None, tm=256, tf=D_FF, rider=None):
    T = x.shape[0]
    nf = D_FF // tf

    def body(x_ref, g_ref, w1_ref, w2_ref, *rest):
        if head is None:
            o_ref, pre_ref, h_s, acc_s = rest
        else:
            gf_ref, t_ref, pre_ref, dy_ref, dgf_ref, loss_ref, h_s, acc_s = rest
        i, f = pl.program_id(0), pl.program_id(1)

        @pl.when(f == 0)
        def _():
            xv = x_ref[...]
            r = lax.rsqrt(jnp.mean(xv * xv, axis=-1, keepdims=True) + EPS)
            h_s[...] = (xv * r * g_ref[...]).astype(CDT)
            acc_s[...] = jnp.zeros_like(acc_s)

        pre = _dot(h_s[...], w1_ref[...])
        pre_ref[...] = pre
        a = jnp.square(jnp.maximum(pre, 0.0))
        acc_s[...] += _dot(a.astype(CDT), w2_ref[...])

        @pl.when(f == nf - 1)
        def _():
            y = x_ref[...] + acc_s[...]
            if head is None:
                o_ref[...] = y
                return

            @pl.when(i == 0)
            def _():
                dgf_ref[...] = jnp.zeros_like(dgf_ref)
                loss_ref[...] = jnp.zeros_like(loss_ref)

            loss_rows, dy, dg_rows = _loss_and_grad(y, gf_ref[...], t_ref[...])
            dy_ref[...] = dy
            dgf_ref[...] += _colsum(dg_rows)
            loss_ref[...] += _colsum(loss_rows)

    mode = {"pipeline_mode": pl.Buffered(1)} if nf == 1 else {}
    row = pl.BlockSpec((tm, D), lambda i, f: (i, 0))
    pre_spec, pre_shape = pl.BlockSpec((tm, tf), lambda i, f: (i, f)), jax.ShapeDtypeStruct((T, D_FF), F32)
    in_specs = [row, _const((1, D)), pl.BlockSpec((D, tf), lambda i, f: (0, f), **mode),
                pl.BlockSpec((tf, D), lambda i, f: (f, 0), **mode)]
    scratch = [pltpu.VMEM((tm, D), CDT), pltpu.VMEM((tm, D), F32)]
    if head is None:
        return _run(body, rider, name="mlp_fwd", grid=(T // tm, nf), in_specs=in_specs, out_specs=[row, pre_spec],
                    out_shape=[jax.ShapeDtypeStruct((T, D), F32), pre_shape], scratch_shapes=scratch,
                    sem=("parallel", "arbitrary"), args=(x, g, w1, w2))
    return _run(body, rider, name="mlp_fwd_loss", grid=(T // tm, nf), in_specs=in_specs + [_const((1, D)), row],
                out_specs=[pre_spec, row, _const((1, D)), _const((1, 128))],
                out_shape=[pre_shape, jax.ShapeDtypeStruct((T, D), F32), jax.ShapeDtypeStruct((1, D), F32),
                           jax.ShapeDtypeStruct((1, 128), F32)],
                scratch_shapes=scratch, sem=("arbitrary", "arbitrary"), args=(x, g, w1, w2) + tuple(head))


def _rms_bwd(xv, g, dh):
    r = lax.rsqrt(jnp.mean(xv * xv, axis=-1, keepdims=True) + EPS)
    xhat = xv * r
    dxh = dh * g
    dx = r * (dxh - xhat * jnp.mean(dxh * xhat, axis=-1, keepdims=True))
    return dx, dh * xhat


def mlp_bwd(dy, x, g, pre, w1, w2, *, tm=256, tf=D_FF, rider=None):
    T = x.shape[0]
    nf = D_FF // tf

    def body(dy_ref, x_ref, g_ref, pre_ref, w1_ref, w2_ref, dx_ref, dg_ref, h_ref, a_ref, dpre_ref, dyb_s, acc_s):
        i, f = pl.program_id(0), pl.program_id(1)

        @pl.when(jnp.logical_and(i == 0, f == 0))
        def _():
            dg_ref[...] = jnp.zeros_like(dg_ref)

        @pl.when(f == 0)
        def _():
            dyb_s[...] = dy_ref[...].astype(CDT)
            acc_s[...] = jnp.zeros_like(acc_s)

        pre = pre_ref[...]
        rl = jnp.maximum(pre, 0.0)
        a_ref[...] = (rl * rl).astype(CDT)
        da = _dot_nt(dyb_s[...], w2_ref[...])
        dpre = (da * (2.0 * rl)).astype(CDT)
        dpre_ref[...] = dpre
        acc_s[...] += _dot_nt(dpre, w1_ref[...])

        @pl.when(f == nf - 1)
        def _():
            xv = x_ref[...]
            gv = g_ref[...]
            dxn, dg_rows = _rms_bwd(xv, gv, acc_s[...])
            dx_ref[...] = dy_ref[...] + dxn
            dg_ref[...] += _colsum(dg_rows)
            r = lax.rsqrt(jnp.mean(xv * xv, axis=-1, keepdims=True) + EPS)
            h_ref[...] = (xv * r * gv).astype(CDT)

    row = lambda i, f: (i, 0)
    mode = {"pipeline_mode": pl.Buffered(1)} if nf == 1 else {}
    return _run(
        body, rider, name="mlp_bwd", grid=(T // tm, nf),
        in_specs=[pl.BlockSpec((tm, D), row), pl.BlockSpec((tm, D), row), _const((1, D)),
                  pl.BlockSpec((tm, tf), lambda i, f: (i, f)),
                  pl.BlockSpec((D, tf), lambda i, f: (0, f), **mode), pl.BlockSpec((tf, D), lambda i, f: (f, 0), **mode)],
        out_specs=[pl.BlockSpec((tm, D), row), _const((1, D)), pl.BlockSpec((tm, D), row),
                   pl.BlockSpec((tm, tf), lambda i, f: (i, f)), pl.BlockSpec((tm, tf), lambda i, f: (i, f))],
        out_shape=[jax.ShapeDtypeStruct((T, D), F32), jax.ShapeDtypeStruct((1, D), F32),
                   jax.ShapeDtypeStruct((T, D), CDT), jax.ShapeDtypeStruct((T, D_FF), CDT),
                   jax.ShapeDtypeStruct((T, D_FF), CDT)],
        scratch_shapes=[pltpu.VMEM((tm, D), CDT), pltpu.VMEM((tm, D), F32)],
        sem=("arbitrary", "arbitrary"), args=(dy, x, g, pre, w1, w2))


def tn_matmul(a, b, *, tm, tn, tk=2048, name, by_chip=False, rider=None):
    T, M = a.shape
    N = b.shape[1]
    tk = min(tk, T)
    nk = T // tk

    def body(a_ref, b_ref, o_ref):
        @pl.when(pl.program_id(2) == 0)
        def _():
            o_ref[...] = jnp.zeros_like(o_ref)

        o_ref[...] += _dot_tn(a_ref[...].astype(CDT), b_ref[...].astype(CDT))

    if by_chip:
        out_spec = pl.BlockSpec((None, tm, tn), lambda i, j, k: (j, i, 0))
        out_shape = jax.ShapeDtypeStruct((N // tn, M, tn), F32)
    else:
        out_spec = pl.BlockSpec((tm, tn), lambda i, j, k: (i, j))
        out_shape = jax.ShapeDtypeStruct((M, N), F32)
    res = _run(
        body, rider, name=name, grid=(M // tm, N // tn, nk),
        in_specs=[pl.BlockSpec((tk, tm), lambda i, j, k: (k, i)), pl.BlockSpec((tk, tn), lambda i, j, k: (k, j))],
        out_specs=[out_spec], out_shape=[out_shape], sem=("parallel", "parallel", "arbitrary"), args=(a, b))
    return res[0] if rider is None else (res[0][0], res[1])


def merge_bwd(dx1, attn, u3, rest, wa, wc, bc, wo, *, tm=512):
    T = dx1.shape[0]

    def body(dx_ref, at_ref, u_ref, ga_ref, gc_ref, wa_ref, wc_ref, bc_ref, wo_ref,
             mg_ref, dba_ref, dbc_ref, dat_ref, du_ref, dgate_ref, dgsum_ref, dbias_ref):
        @pl.when(pl.program_id(0) == 0)
        def _():
            dbias_ref[...] = jnp.zeros_like(dbias_ref)
            dgsum_ref[...] = jnp.zeros_like(dgsum_ref)

        br_a = _dot(at_ref[...], wa_ref[...])
        br_c = _dot(u_ref[...], wc_ref[...]) + bc_ref[...]
        sa = _sig(ga_ref[...])
        sc = _sig(gc_ref[...])
        mg_ref[...] = (sa * br_a + sc * br_c).astype(CDT)
        dm = _dot_nt(dx_ref[...].astype(CDT), wo_ref[...])
        dba = dm * sa
        dbc = dm * sc
        dga = dm * br_a * sa * (1.0 - sa)
        dgc = dm * br_c * sc * (1.0 - sc)
        dgate_ref[:, 0:D] = dga.astype(CDT)
        dgate_ref[:, D:2 * D] = dgc.astype(CDT)
        dgsum_ref[:, 0:D] += _colsum(dga)
        dgsum_ref[:, D:2 * D] += _colsum(dgc)
        dbias_ref[...] += _colsum(dbc)
        dba_b = dba.astype(CDT)
        dbc_b = dbc.astype(CDT)
        dba_ref[...] = dba_b
        dbc_ref[...] = dbc_b
        dat_ref[...] = _dot_nt(dba_b, wa_ref[...]).astype(CDT)
        du_ref[...] = _dot_nt(dbc_b, wc_ref[...])

    row = lambda i: (i, 0)
    return pl.pallas_call(
        body, name="merge_bwd", grid=(T // tm,),
        in_specs=[pl.BlockSpec((tm, D), row), pl.BlockSpec((tm, ATTN_W), row), pl.BlockSpec((tm, CONV_C), row),
                  pl.BlockSpec((tm, D), lambda i: (i, 1)), pl.BlockSpec((tm, D), lambda i: (i, 2)),
                  _const((ATTN_W, D), 1), _const((CONV_C, D), 1), _const((1, D)), _const((D, D), 1)],
        out_specs=[pl.BlockSpec((tm, D), row), pl.BlockSpec((tm, D), row), pl.BlockSpec((tm, D), row),
                   pl.BlockSpec((tm, ATTN_W), row), pl.BlockSpec((tm, CONV_C), row),
                   pl.BlockSpec((tm, 2 * D), row), _const((1, 2 * D)), _const((1, D))],
        out_shape=[jax.ShapeDtypeStruct((T, D), CDT), jax.ShapeDtypeStruct((T, D), CDT),
                   jax.ShapeDtypeStruct((T, D), CDT), jax.ShapeDtypeStruct((T, ATTN_W), CDT),
                   jax.ShapeDtypeStruct((T, CONV_C), F32), jax.ShapeDtypeStruct((T, 2 * D), CDT),
                   jax.ShapeDtypeStruct((1, 2 * D), F32), jax.ShapeDtypeStruct((1, D), F32)],
        compiler_params=_cp("arbitrary"),
    )(dx1, attn, u3, rest, rest, wa, wc, bc, wo)


def conv_bwd_ln(du3, u1, lg, lb, *, tm=512, rider=None):
    T = du3.shape[0]

    def body(du_ref, u1_ref, lg_ref, lb_ref, du1_ref, dlg_ref, dlb_ref, dcb_ref):
        @pl.when(pl.program_id(0) == 0)
        def _():
            dlg_ref[...] = jnp.zeros_like(dlg_ref)
            dlb_ref[...] = jnp.zeros_like(dlb_ref)
            dcb_ref[...] = jnp.zeros_like(dcb_ref)

        dlg = jnp.zeros((1, CONV_C), F32)
        dlb = jnp.zeros((1, CONV_C), F32)
        dcb = jnp.zeros((1, CONV_C), F32)
        rc = CONV_RC_1PASS
        for r0 in range(0, tm, rc):
            n, rstd, u2 = _layer_norm(u1_ref[r0:r0 + rc, :], lg_ref[...], lb_ref[...])
            s = _sig(u2)
            du2 = du_ref[r0:r0 + rc, :] * (s + u2 * s * (1.0 - s))
            dn = du2 * lg_ref[...]
            du1 = rstd * (dn - jnp.mean(dn, axis=-1, keepdims=True) - n * jnp.mean(dn * n, axis=-1, keepdims=True))
            du1_ref[r0:r0 + rc, :] = du1
            dlg = dlg + _colsum(du2 * n)
            dlb = dlb + _colsum(du2)
            dcb = dcb + _colsum(du1)
        dlg_ref[...] += dlg
        dlb_ref[...] += dlb
        dcb_ref[...] += dcb

    row = lambda i: (i, 0)
    vec = jax.ShapeDtypeStruct((1, CONV_C), F32)
    return _run(
        body, rider, name="conv_bwd_ln", grid=(T // tm,),
        in_specs=[pl.BlockSpec((tm, CONV_C), row), pl.BlockSpec((tm, CONV_C), row), _const((1, CONV_C)), _const((1, CONV_C))],
        out_specs=[pl.BlockSpec((tm, CONV_C), row), _const((1, CONV_C)), _const((1, CONV_C)), _const((1, CONV_C))],
        out_shape=[jax.ShapeDtypeStruct((T, CONV_C), F32), vec, vec, vec],
        sem=("arbitrary",), args=(du3, u1, lg, lb))


def conv_bwd_taps(du1, rest, cw, *, tm=512):
    T = du1.shape[0]
    nt = T // tm

    def body(d_ref, hd_ref, a_ref, b_ref, ha_ref, hb_ref, w_ref, dglu_ref, dgsum_ref, dw_ref, ext, dext, dwacc):
        i = pl.program_id(0)

        @pl.when(i == 0)
        def _():
            dwacc[...] = jnp.zeros_like(dwacc)
            dgsum_ref[...] = jnp.zeros_like(dgsum_ref)

        sums = [jnp.zeros((SUBLANES, CONV_C), F32), jnp.zeros((SUBLANES, CONV_C), F32)]
        _fill_u0(ext, a_ref, b_ref, ha_ref, hb_ref, i == 0)
        dext[0:SUBLANES, :] = jnp.zeros((SUBLANES, CONV_C), F32)
        dext[SUBLANES:SUBLANES + tm, :] = d_ref[...]
        hd = hd_ref[...]
        dext[SUBLANES + tm:, :] = jnp.where(i == nt - 1, jnp.zeros_like(hd), hd)
        qrow = lax.broadcasted_iota(jnp.int32, (CONV_RC + SUBLANES, CONV_C), 0)
        for c in range(tm // CONV_RC):
            r0 = c * CONV_RC
            du0 = _shifted_taps(dext, w_ref, r0 + SUBLANES, CONV_RC, [CONV_K - 1 - k for k in range(CONV_K)])
            for b in range(SUBLANES):
                taps = [(k, HALO - (CONV_K - 1) + k - b) for k in range(CONV_K) if (HALO - (CONV_K - 1) + k) % SUBLANES == b]
                if b == 0:
                    rows = CONV_RC
                    dsh = dext[r0 + SUBLANES:r0 + SUBLANES + rows, :]
                else:
                    rows = CONV_RC + SUBLANES
                    dsh = dext[r0 + SUBLANES - b:r0 + SUBLANES - b + rows, :]
                    dsh = jnp.where((qrow >= b) & (qrow < CONV_RC + b), dsh, 0.0)
                for k, o8 in taps:
                    prod = dsh * ext[r0 + o8:r0 + o8 + rows, :]
                    dwacc[8 * k:8 * k + 8, :] += jnp.sum(prod.reshape(rows // SUBLANES, SUBLANES, CONV_C), axis=0)
            av = a_ref[r0:r0 + CONV_RC, :]
            sb = _sig(b_ref[r0:r0 + CONV_RC, :])
            for half, dg in enumerate((du0 * sb, du0 * av * sb * (1.0 - sb))):
                dglu_ref[r0:r0 + CONV_RC, half * CONV_C:(half + 1) * CONV_C] = dg.astype(CDT)
                sums[half] = sums[half] + jnp.sum(dg.reshape(CONV_RC // SUBLANES, SUBLANES, CONV_C), axis=0)
        for half in range(2):
            dgsum_ref[:, half * CONV_C:(half + 1) * CONV_C] += _colsum(sums[half])

        @pl.when(i == nt - 1)
        def _():
            dw_ref[...] = jnp.zeros_like(dw_ref)
            for k in range(CONV_K):
                dw_ref[k:k + 1, :] = _colsum(dwacc[8 * k:8 * k + 8, :])

    row = lambda i: (i, 0)
    return pl.pallas_call(
        body, name="conv_bwd_taps", grid=(nt,),
        in_specs=[pl.BlockSpec((tm, CONV_C), row),
                  pl.BlockSpec((HALO, CONV_C), lambda i: (jnp.minimum((i + 1) * (tm // HALO), T // HALO - 1), 0)),
                  pl.BlockSpec((tm, CONV_C), row), pl.BlockSpec((tm, CONV_C), lambda i: (i, 1)),
                  _halo_before(tm, CONV_C, 0), _halo_before(tm, CONV_C, 1), _const((CONV_K, CONV_C))],
        out_specs=[pl.BlockSpec((tm, 2 * CONV_C), row), _const((1, 2 * CONV_C)), _const((HALO, CONV_C))],
        out_shape=[jax.ShapeDtypeStruct((T, 2 * CONV_C), CDT), jax.ShapeDtypeStruct((1, 2 * CONV_C), F32),
                   jax.ShapeDtypeStruct((HALO, CONV_C), F32)],
        scratch_shapes=[pltpu.VMEM((tm + HALO, CONV_C), F32), pltpu.VMEM((SUBLANES + tm + HALO, CONV_C), F32),
                        pltpu.VMEM((8 * CONV_K, CONV_C), F32)],
        compiler_params=_cp("arbitrary"),
    )(du1, du1, rest, rest, rest, rest, cw)


def attn_bwd(qkv, do, lse, sinks, *, tq=512, rider=None):
    T = qkv.shape[0]
    nb = tq // BLOCK

    def body(sink_ref, bias_ref, q_ref, kp_ref, kc_ref, vp_ref, vc_ref, do_ref, lse_ref,
             dq_ref, dqsum_ref, dkv_ref, spill_ref, dsink_ref, kext, vext, dkext, dvext, qs, dos):
        i = pl.program_id(0)

        @pl.when(i == 0)
        def _():
            dsink_ref[...] = jnp.zeros_like(dsink_ref)
            dqsum_ref[...] = jnp.zeros_like(dqsum_ref)

        kext[0:BLOCK, :] = kp_ref[...]
        kext[BLOCK:, :] = kc_ref[...]
        vext[0:BLOCK, :] = vp_ref[...]
        vext[BLOCK:, :] = vc_ref[...]
        dkext[...] = jnp.zeros_like(dkext)
        dvext[...] = jnp.zeros_like(dvext)
        lane_l = lax.broadcasted_iota(jnp.int32, (BLOCK, 128), 1)
        lane_k = lax.broadcasted_iota(jnp.int32, (2 * BLOCK, KV_W), 1)

        def blk(b, dsink):
            r0 = pl.multiple_of(b * BLOCK, BLOCK)
            first = jnp.logical_and(i == 0, b == 0).astype(jnp.int32)
            kc = kext[pl.ds(r0, 2 * BLOCK), :]
            vc = vext[pl.ds(r0, 2 * BLOCK), :]
            lse_t = lse_ref[pl.ds(r0, BLOCK), :]
            dk = jnp.zeros((2 * BLOCK, KV_W), F32)
            dv = jnp.zeros((2 * BLOCK, KV_W), F32)
            for g in range(N_KV):
                heads = range(GROUP * g, GROUP * (g + 1))
                _stack_heads(qs.at[g], q_ref, r0, g, SCALE)
                _stack_heads(dos.at[g], do_ref, r0, g)
                qv = qs[g]
                dov = dos[g]
                s = _dot_nt(qv, kc) + bias_ref[first, g]
                lse = jnp.concatenate(
                    [jnp.sum(jnp.where(lane_l == h, lse_t, 0.0), axis=-1, keepdims=True) for h in heads], axis=0)
                p = jnp.exp(s - lse)
                dp = _dot_nt(dov, vc)
                dd = jnp.sum(p * dp, axis=-1, keepdims=True)
                ds = (p * (dp - dd)).astype(CDT)
                keep = (lane_k >= HEAD_DIM) if g else (lane_k < HEAD_DIM)
                for c0, tile in _unstack_heads(dq_ref, _dot(ds, jnp.where(keep, kc, jnp.zeros_like(kc))) * SCALE, r0, g):
                    dqsum_ref[:, c0:c0 + 128] += _colsum(tile)
                dk = dk + _dot_tn(ds, qv)
                dv = dv + _dot_tn(p.astype(CDT), dov)
                wsink = jnp.exp(_per_head_column([sink_ref[h] for h in heads]) - lse) * dd
                for i_h, h in enumerate(heads):
                    part = jnp.sum(wsink[i_h * BLOCK:(i_h + 1) * BLOCK, :], axis=0, keepdims=True)
                    dsink = dsink - jnp.where(lane_l[0:1, :] == h, part, 0.0)
            dkext[pl.ds(r0, 2 * BLOCK), :] += dk
            dvext[pl.ds(r0, 2 * BLOCK), :] += dv
            return dsink

        dsink_ref[...] += lax.fori_loop(0, nb, blk, jnp.zeros((1, 128), F32))
        dkv_ref[:, 0:KV_W] = dkext[BLOCK:, :]
        dkv_ref[:, KV_W:2 * KV_W] = dvext[BLOCK:, :]
        spill_ref[:, 0:KV_W] = dkext[0:BLOCK, :]
        spill_ref[:, KV_W:2 * KV_W] = dvext[0:BLOCK, :]

    row = lambda i: (i, 0)
    return _run(
        body, rider, name="attn_bwd", grid=(T // tq,),
        in_specs=[pl.BlockSpec(memory_space=pltpu.SMEM), _const((2, N_KV, STACK, 2 * BLOCK), 1)] + _qkv_specs(tq)
        + [pl.BlockSpec((tq, ATTN_W), row), pl.BlockSpec((tq, 128), row)],
        out_specs=[pl.BlockSpec((tq, ATTN_W), row), _const((1, ATTN_W)), pl.BlockSpec((tq, 2 * KV_W), row),
                   pl.BlockSpec((BLOCK, 2 * KV_W), row), _const((1, 128))],
        out_shape=[jax.ShapeDtypeStruct((T, ATTN_W), CDT), jax.ShapeDtypeStruct((1, ATTN_W), F32),
                   jax.ShapeDtypeStruct((T, 2 * KV_W), F32),
                   jax.ShapeDtypeStruct((T // tq * BLOCK, 2 * KV_W), F32), jax.ShapeDtypeStruct((1, 128), F32)],
        scratch_shapes=[pltpu.VMEM((tq + BLOCK, KV_W), CDT), pltpu.VMEM((tq + BLOCK, KV_W), CDT),
                        pltpu.VMEM((tq + BLOCK, KV_W), F32), pltpu.VMEM((tq + BLOCK, KV_W), F32),
                        pltpu.VMEM((N_KV, STACK, 2 * HEAD_DIM), CDT), pltpu.VMEM((N_KV, STACK, 2 * HEAD_DIM), CDT)],
        sem=("arbitrary",), args=(sinks, _score_bias(), qkv, qkv, qkv, qkv, qkv, do, lse))


def inproj_bwd(dres, x, g, w, dq, dkv, spill, dglu, dgate, sums, *, tm=512):
    T = x.shape[0]
    nt = T // tm
    pieces = ((0, ATTN_W), (QKV_W, 2 * CONV_C), (QKV_W + 2 * CONV_C, 2 * D))

    def body(dr_ref, x_ref, g_ref, w_ref, dq_ref, dkv_ref, sp_ref, dglu_ref, dgate_ref, sq_ref, sglu_ref, sgate_ref,
             dx_ref, dp_ref, h_ref, dg_ref, db_ref):
        i = pl.program_id(0)

        @pl.when(i == 0)
        def _():
            dg_ref[...] = jnp.zeros_like(dg_ref)
            db_ref[:, ATTN_W:QKV_W] = jnp.zeros((1, QKV_W - ATTN_W), F32)
            for (c0, wd), s_ref in zip(pieces, (sq_ref, sglu_ref, sgate_ref)):
                db_ref[:, c0:c0 + wd] = s_ref[...]

        sp = sp_ref[...]
        sp = jnp.where(i == nt - 1, jnp.zeros_like(sp), sp)
        dkv = dkv_ref[...]
        db_ref[:, ATTN_W:QKV_W] += _colsum(dkv) + _colsum(sp)
        dp_ref[0:tm - BLOCK, ATTN_W:QKV_W] = dkv[0:tm - BLOCK, :].astype(CDT)
        dp_ref[tm - BLOCK:tm, ATTN_W:QKV_W] = (dkv[tm - BLOCK:tm, :] + sp).astype(CDT)
        for (c0, wd), ref in zip(pieces, (dq_ref, dglu_ref, dgate_ref)):
            dp_ref[:, c0:c0 + wd] = ref[...]
        dh = _dot(dp_ref[...], w_ref[...])
        xv = x_ref[...]
        gv = g_ref[...]
        dxn, dg_rows = _rms_bwd(xv, gv, dh)
        dx_ref[...] = dr_ref[...] + dxn
        dg_ref[...] += _colsum(dg_rows)
        r = lax.rsqrt(jnp.mean(xv * xv, axis=-1, keepdims=True) + EPS)
        h_ref[...] = (xv * r * gv).astype(CDT)

    row = lambda i: (i, 0)
    return pl.pallas_call(
        body, name="inproj_bwd", grid=(nt,),
        in_specs=[pl.BlockSpec((tm, D), row), pl.BlockSpec((tm, D), row), _const((1, D)), _const((IN_W, D), 1),
                  pl.BlockSpec((tm, ATTN_W), row), pl.BlockSpec((tm, 2 * KV_W), row),
                  pl.BlockSpec((BLOCK, 2 * KV_W), lambda i: (jnp.minimum(i + 1, nt - 1), 0)),
                  pl.BlockSpec((tm, 2 * CONV_C), row), pl.BlockSpec((tm, 2 * D), row)]
        + [_const((1, wd)) for _, wd in pieces],
        out_specs=[pl.BlockSpec((tm, D), row), pl.BlockSpec((tm, IN_W), row), pl.BlockSpec((tm, D), row),
                   _const((1, D)), _const((1, IN_W))],
        out_shape=[jax.ShapeDtypeStruct((T, D), F32), jax.ShapeDtypeStruct((T, IN_W), CDT),
                   jax.ShapeDtypeStruct((T, D), CDT), jax.ShapeDtypeStruct((1, D), F32),
                   jax.ShapeDtypeStruct((1, IN_W), F32)],
        compiler_params=_cp("arbitrary"),
    )(dres, x, g, w, dq, dkv, spill, dglu, dgate, *sums)


ATTN_TILE = 256
MATRICES = ("w_in", "w_attn_proj", "w_conv_proj", "w_out", "w_mlp1", "w_mlp2")
SMALL = ("mix_norm_g", "b_in", "sinks", "conv_w", "conv_b", "conv_ln_g", "conv_ln_b", "b_conv_proj", "mlp_norm_g")


def forward_backward(x, tgt, hooks):
    def call(fn, kernel, l, *args, **kw):
        rider = hooks.rider(kernel, l)
        if rider is None:
            return fn(*args, **kw)
        outs, landed = fn(*args, rider=rider, **kw)
        hooks.landed(kernel, l, landed)
        return outs

    vec = hooks.vec
    saved = []
    for l in range(DEPTH):
        qkv, rest = call(rms_inproj, "rms_inproj", l, x, vec("mix_norm_g", l), hooks.w_in(l), vec("b_in", l))
        m = hooks.mats(l)
        attn, lse = attn_fwd(qkv, hooks.sinks(l), tq=ATTN_TILE)
        u3, u1 = conv_fwd(rest, hooks.taps(l), vec("conv_b", l), vec("conv_ln_g", l), vec("conv_ln_b", l))
        x1 = merge_out(x, attn, u3, rest, m["w_attn_proj"], m["w_conv_proj"], vec("b_conv_proj", l), m["w_out"])
        if l < DEPTH - 1:
            x_next, pre = call(mlp_fwd, "mlp_fwd", l, x1, vec("mlp_norm_g", l), m["w_mlp1"], m["w_mlp2"])
        else:
            x_next = None
            pre, dx, dgf, loss = call(mlp_fwd, "mlp_fwd", l, x1, vec("mlp_norm_g", l), m["w_mlp1"], m["w_mlp2"],
                                      head=(hooks.final_g, tgt))
        saved.append((x, qkv, rest, attn, lse, u3, u1, x1, pre))
        x = x_next
    small = {n: [None] * DEPTH for n in SMALL}
    small["final_norm_g"] = dgf
    for l in reversed(range(DEPTH)):
        x0, qkv, rest, attn, lse, u3, u1, x1, pre = saved[l]
        m = hooks.mats(l)
        dx1, dg2, h2, a, dpre = call(mlp_bwd, "mlp_bwd", l, dx, x1, vec("mlp_norm_g", l), pre, m["w_mlp1"], m["w_mlp2"])
        small["mlp_norm_g"][l] = dg2
        group = {}
        group["w_mlp1"] = call(tn_matmul, "tn_mlp1", l, h2, dpre, tm=1024, tn=1024, tk=4096, name="tn_mlp1", by_chip=True)
        group["w_mlp2"] = call(tn_matmul, "tn_mlp2", l, a, dx, tm=1024, tn=1024, name="tn_mlp2")
        merged, dba, dbc, dattn, du3, dgate, dgate_sum, dbcp = merge_bwd(
            dx1, attn, u3, rest, m["w_attn_proj"], m["w_conv_proj"], vec("b_conv_proj", l), m["w_out"])
        small["b_conv_proj"][l] = dbcp
        group["w_out"] = tn_matmul(merged, dx1, tm=1024, tn=1024, name="tn_out")
        group["w_attn_proj"] = tn_matmul(attn, dba, tm=512, tn=256, tk=4096, name="tn_attn_proj", by_chip=True)
        group["w_conv_proj"] = tn_matmul(u3, dbc, tm=512, tn=256, tk=4096, name="tn_conv_proj", by_chip=True)
        hooks.grads(l, "A", group)
        du1, dlg, dlb, dcb = call(conv_bwd_ln, "conv_bwd_ln", l, du3, u1, vec("conv_ln_g", l), vec("conv_ln_b", l))
        small["conv_ln_g"][l], small["conv_ln_b"][l], small["conv_b"][l] = dlg, dlb, dcb
        dglu, dglu_sum, dcw = conv_bwd_taps(du1, rest, hooks.taps(l))
        small["conv_w"][l] = dcw
        dq, dq_sum, dkv, spill, dsink = call(attn_bwd, "attn_bwd", l, qkv, dattn, lse, hooks.sinks(l), tq=ATTN_TILE)
        small["sinks"][l] = dsink
        dx, dproj, h, dg, db = inproj_bwd(dx1, x0, vec("mix_norm_g", l), hooks.w_in(l), dq, dkv, spill, dglu, dgate,
                                          (dq_sum, dglu_sum, dgate_sum), tm=ATTN_TILE)
        small["mix_norm_g"][l], small["b_in"][l] = dg, db
        hooks.grads(l, "B", {"w_in": call(tn_matmul, "tn_in", l, dproj, h, tm=768, tn=1024, tk=4096, name="tn_in")})
    return loss, dx, small


class _LocalHooks:
    def __init__(self, p):
        self.p = p
        self.final_g = p["final_norm_g"]
        self.got = {n: [None] * DEPTH for n in MATRICES}

    def w_in(self, l):
        return self.p["w_in"][l].T

    def mats(self, l):
        return {n: self.p[n][l] for n in MATRICES}

    def vec(self, n, l):
        return self.p[n][l]

    def sinks(self, l):
        return self.p["sinks"][l]

    def taps(self, l):
        return self.p["conv_w"][l]

    def rider(self, kernel, l):
        return None

    def grads(self, l, group, g):
        for n, v in g.items():
            if v.ndim == 3:
                v = v.transpose(1, 0, 2).reshape(v.shape[1], -1)
            self.got[n][l] = v.T if n == "w_in" else v


def local_grads(x, tgt, p):
    hooks = _LocalHooks(p)
    loss, dx, small = forward_backward(x, tgt, hooks)
    small["conv_w"] = [g[0:CONV_K] for g in small["conv_w"]]
    small["sinks"] = [g[0, 0:N_Q] for g in small["sinks"]]
    return loss, dx, {**small, **hooks.got}


MESH = pl.DeviceIdType.MESH
N_CHIPS = 4
N_DEV = 8
FLAT_W = 1024
FLAT_PARTS = (("w_in", 960), ("w_attn_proj", 128), ("w_conv_proj", 128), ("w_out", 256), ("w_mlp1", 1024), ("w_mlp2", 1024))
FLAT_ROWS = sum(r for _, r in FLAT_PARTS)
W_IN_ROWS = FLAT_PARTS[0][1]
GROUP_A = (("w_mlp1", 1024), ("w_mlp2", 1024), ("w_out", 256), ("w_attn_proj", 128), ("w_conv_proj", 128))
COL_SHARDED = ("w_in", "w_attn_proj", "w_conv_proj", "w_mlp1")
FULL_SHAPES = {"w_in": (D, IN_W), "w_attn_proj": (ATTN_W, D), "w_conv_proj": (CONV_C, D), "w_out": (D, D),
               "w_mlp1": (D, D_FF), "w_mlp2": (D_FF, D)}


def _place():
    x, y, c = lax.axis_index("x"), lax.axis_index("y"), lax.axis_index("c")
    return x, y, c, 2 * x + y


def _peer_chips(x, y, j):
    return [((x, 1 - y), j ^ 1), ((1 - x, y), j ^ 2), ((1 - x, 1 - y), j ^ 3)]


def _remote(src, dst, sems, k, n, to):
    return pltpu.make_async_remote_copy(src_ref=src, dst_ref=dst, send_sem=sems.at[k], recv_sem=sems.at[n + k],
                                        device_id=to, device_id_type=MESH)


def _half(c, rows):
    h = rows // 2
    return pl.ds(pl.multiple_of(c * h, 16), h)


def gather_rider(wsh):
    R = wsh.shape[0]
    n = 7

    def plan(rins, routs, sems):
        (w_ref,), (out_ref,) = rins, routs
        x, y, c, j = _place()
        peers = _peer_chips(x, y, j)
        mine, other = _half(c, R), _half(1 - c, R)
        sent = [_remote(w_ref.at[mine], out_ref.at[j, mine], sems, k, n, (*chip, c)) for k, (chip, _) in enumerate(peers)]
        sent.append(_remote(w_ref, out_ref.at[j], sems, 6, n, (x, y, 1 - c)))
        landed = [_remote(w_ref.at[mine], out_ref.at[pj, mine], sems, k, n, (x, y, c)) for k, (_, pj) in enumerate(peers)]
        passed = [_remote(out_ref.at[pj, mine], out_ref.at[pj, mine], sems, 3 + k, n, (x, y, 1 - c))
                  for k, (_, pj) in enumerate(peers)]
        handed = [_remote(w_ref.at[mine], out_ref.at[pj, other], sems, 3 + k, n, (x, y, c)) for k, (_, pj) in enumerate(peers)]
        handed.append(_remote(w_ref, out_ref.at[j], sems, 6, n, (x, y, c)))
        return sent, landed, passed, handed

    def start(rins, routs, sems):
        for cp in plan(rins, routs, sems)[0]:
            cp.start()

    def finish(rins, routs, sems):
        sent, landed, passed, handed = plan(rins, routs, sems)
        for k in range(3):
            landed[k].wait_recv()
            passed[k].start()
        for cp in handed:
            cp.wait_recv()
        for cp in sent + passed:
            cp.wait_send()

    return Rider((wsh,), (jax.ShapeDtypeStruct((N_CHIPS,) + wsh.shape, wsh.dtype),), 2 * n, start, finish)


def swap_rider(g):
    R = g.shape[1]

    def plan(rins, routs, sems):
        (g_ref,), (got_ref,) = rins, routs
        x, y, c, _ = _place()
        return _remote(g_ref.at[:, _half(1 - c, R), :], got_ref, sems, 0, 1, (x, y, 1 - c))

    def start(rins, routs, sems):
        plan(rins, routs, sems).start()

    def finish(rins, routs, sems):
        plan(rins, routs, sems).wait()

    return Rider((g,), (jax.ShapeDtypeStruct((N_CHIPS, R // 2, FLAT_W), g.dtype),), 2, start, finish)


def exchange_rider(pb):
    def plan(rins, routs, sems):
        (pb_ref,), (got_ref,) = rins, routs
        x, y, c, j = _place()
        peers = _peer_chips(x, y, j)
        sent = [_remote(pb_ref.at[pj], got_ref.at[j], sems, k, 3, (*chip, c)) for k, (chip, pj) in enumerate(peers)]
        landed = [_remote(pb_ref.at[pj], got_ref.at[pj], sems, k, 3, (x, y, c)) for k, (_, pj) in enumerate(peers)]
        return sent, landed

    def start(rins, routs, sems):
        for cp in plan(rins, routs, sems)[0]:
            cp.start()

    def finish(rins, routs, sems):
        sent, landed = plan(rins, routs, sems)
        for cp in landed:
            cp.wait_recv()
        for cp in sent:
            cp.wait_send()

    return Rider((pb,), (jax.ShapeDtypeStruct(pb.shape, pb.dtype),), 6, start, finish)


def share_rider(tot):
    def plan(rins, routs, sems):
        (t_ref,), (got_ref,) = rins, routs
        x, y, c, _ = _place()
        return _remote(t_ref, got_ref, sems, 0, 1, (x, y, 1 - c))

    def start(rins, routs, sems):
        plan(rins, routs, sems).start()

    def finish(rins, routs, sems):
        plan(rins, routs, sems).wait()

    return Rider((tot,), (jax.ShapeDtypeStruct(tot.shape, tot.dtype),), 2, start, finish)


def pair_sum(g, got):
    nj, R, W = g.shape
    h = R // 2
    tile = h // 2

    def body(g_ref, got_ref, pb_ref, own_ref):
        v = g_ref[...] + got_ref[...]
        pb_ref[...] = v.astype(pb_ref.dtype)

        @pl.when(pl.program_id(1) == _place()[3])
        def _():
            own_ref[...] = v

    return pl.pallas_call(
        body, name="pair_sum", grid=(h // tile, nj),
        in_specs=[pl.BlockSpec((None, tile, W), lambda r, j: (j, lax.axis_index("c") * (h // tile) + r, 0)),
                  pl.BlockSpec((None, tile, W), lambda r, j: (j, r, 0))],
        out_specs=[pl.BlockSpec((None, tile, W), lambda r, j: (j, r, 0)), pl.BlockSpec((tile, W), lambda r, j: (r, 0))],
        out_shape=[jax.ShapeDtypeStruct((nj, h, W), CDT), jax.ShapeDtypeStruct((h, W), F32)],
        compiler_params=_cp("arbitrary", "arbitrary"),
    )(g, got)


def total_sum(own, got):
    R, W = own.shape
    tile = R // 2

    def body(own_ref, a_ref, b_ref, c_ref, o_ref):
        o_ref[...] = ((own_ref[...] + a_ref[...].astype(F32)) + b_ref[...].astype(F32)) + c_ref[...].astype(F32)

    def slab(k):
        return pl.BlockSpec((None, tile, W), lambda r: (_place()[3] ^ (k + 1), r, 0))

    return pl.pallas_call(
        body, name="total_sum", grid=(R // tile,),
        in_specs=[pl.BlockSpec((tile, W), lambda r: (r, 0)), slab(0), slab(1), slab(2)],
        out_specs=pl.BlockSpec((tile, W), lambda r: (r, 0)),
        out_shape=jax.ShapeDtypeStruct((R, W), F32),
        compiler_params=_cp("arbitrary"),
    )(own, got, got, got)


def _all_peers(x, y, c):
    return [(x ^ (r >> 2), y ^ ((r >> 1) & 1), c ^ (r & 1)) for r in range(1, N_DEV)]


ROW_ITEMS = (("mix_norm_g", D), ("b_in", IN_W), ("sinks", N_Q), ("conv_b", CONV_C), ("conv_ln_g", CONV_C),
             ("conv_ln_b", CONV_C), ("b_conv_proj", D), ("mlp_norm_g", D))
TAPS_ROW = 16
TAPS_ROWS = 32
LAYER_ROWS = TAPS_ROW + TAPS_ROWS
FINAL_ROW = DEPTH * LAYER_ROWS
SMALL_ROWS = FINAL_ROW + SUBLANES


def _row_chunks():
    out, r = {}, 0
    for n, width in ROW_ITEMS:
        out[n] = [(r + i, FLAT_W * i, min(FLAT_W, width - FLAT_W * i)) for i in range(-(-width // FLAT_W))]
        r += len(out[n])
    assert r <= TAPS_ROW
    return out


def sum_small(gsm):
    chunks = _row_chunks()
    ins = []
    for l in range(DEPTH):
        ins += [gsm[n][l] for n, _ in ROW_ITEMS] + [gsm["conv_w"][l]]
    ins.append(gsm["final_norm_g"])
    n_in = len(ins)

    def body(*refs):
        in_refs, o_ref, buf, send_sems, recv_sems = refs[:n_in], refs[n_in], refs[n_in + 1], refs[n_in + 2], refs[n_in + 3]
        x, y, c, _ = _place()
        me = 4 * x + 2 * y + c
        mine = buf.at[me]
        mine[...] = jnp.zeros((SMALL_ROWS, FLAT_W), F32)
        k = 0
        for l in range(DEPTH):
            for n, _ in ROW_ITEMS:
                for r, c0, wd in chunks[n]:
                    mine[l * LAYER_ROWS + r:l * LAYER_ROWS + r + 1, 0:wd] = in_refs[k][:, c0:c0 + wd]
                k += 1
            mine[l * LAYER_ROWS + TAPS_ROW:(l + 1) * LAYER_ROWS, 0:CONV_C] = in_refs[k][...]
            k += 1
        mine[FINAL_ROW:FINAL_ROW + 1, :] = in_refs[k][...]
        peers = _all_peers(x, y, c)
        sends = [pltpu.make_async_remote_copy(src_ref=mine, dst_ref=mine, send_sem=send_sems.at[r], recv_sem=recv_sems.at[r],
                                              device_id=to, device_id_type=MESH) for r, to in enumerate(peers)]
        for cp in sends:
            cp.start()
        for r in range(N_DEV - 1):
            pltpu.make_async_remote_copy(src_ref=mine, dst_ref=buf.at[me ^ (r + 1)], send_sem=send_sems.at[r],
                                         recv_sem=recv_sems.at[r], device_id=(x, y, c), device_id_type=MESH).wait_recv()
        for cp in sends:
            cp.wait_send()
        acc = buf[0]
        for d in range(1, N_DEV):
            acc = acc + buf[d]
        o_ref[...] = acc

    vm = pl.BlockSpec(memory_space=pltpu.VMEM)
    return pl.pallas_call(
        body, name="sum_small", out_shape=jax.ShapeDtypeStruct((SMALL_ROWS, FLAT_W), F32),
        in_specs=[vm] * n_in, out_specs=vm,
        scratch_shapes=[pltpu.VMEM((N_DEV, SMALL_ROWS, FLAT_W), F32), pltpu.SemaphoreType.DMA((N_DEV - 1,)),
                        pltpu.SemaphoreType.DMA((N_DEV - 1,))],
    )(*ins)


def gather_taps(taps):
    shard = taps.shape[2]

    def body(t_ref, o_ref, buf, send_sems, recv_sems):
        x, y, c, j = _place()
        peers = _peer_chips(x, y, j)
        buf[j] = t_ref[...]
        sends = [pltpu.make_async_remote_copy(src_ref=t_ref, dst_ref=buf.at[j], send_sem=send_sems.at[k],
                                              recv_sem=recv_sems.at[k], device_id=(*chip, c), device_id_type=MESH)
                 for k, (chip, _) in enumerate(peers)]
        for cp in sends:
            cp.start()
        for k, (_, pj) in enumerate(peers):
            pltpu.make_async_remote_copy(src_ref=t_ref, dst_ref=buf.at[pj], send_sem=send_sems.at[k],
                                         recv_sem=recv_sems.at[k], device_id=(x, y, c), device_id_type=MESH).wait_recv()
        for cp in sends:
            cp.wait_send()
        for jj in range(N_CHIPS):
            o_ref[:, :, jj * shard:(jj + 1) * shard] = buf[jj]

    vm = pl.BlockSpec(memory_space=pltpu.VMEM)
    return pl.pallas_call(
        body, name="gather_taps", out_shape=jax.ShapeDtypeStruct(taps.shape[:2] + (N_CHIPS * shard,), taps.dtype),
        in_specs=[vm], out_specs=vm,
        scratch_shapes=[pltpu.VMEM((N_CHIPS,) + taps.shape, taps.dtype), pltpu.SemaphoreType.DMA((3,)),
                        pltpu.SemaphoreType.DMA((3,))],
    )(taps)


def _adam_math(w, g, m, v):
    nm = ADAM_B1 * m + (1.0 - ADAM_B1) * g
    nv = ADAM_B2 * v + (1.0 - ADAM_B2) * jnp.square(g)
    m_hat = nm / (1.0 - ADAM_B1 ** ADAM_STEP)
    v_hat = nv / (1.0 - ADAM_B2 ** ADAM_STEP)
    return -ADAM_LR * (m_hat / (jnp.sqrt(v_hat) + ADAM_EPS) + ADAM_WD * w), nm, nv


def adamw(w, g, m, v, *, name):
    L, R, C = w.shape
    tr = next(t for t in (512, 480, 256, 128) if R % t == 0)

    def body(w_ref, g_ref, m_ref, v_ref, d_ref, nm_ref, nv_ref):
        d_ref[...], nm_ref[...], nv_ref[...] = _adam_math(w_ref[...], g_ref[...], m_ref[...], v_ref[...])

    spec = pl.BlockSpec((None, tr, C), lambda l, i: (l, i, 0))
    out = jax.ShapeDtypeStruct((L, R, C), F32)
    return pl.pallas_call(
        body, name=name, grid=(L, R // tr), in_specs=[spec] * 4, out_specs=[spec] * 3, out_shape=[out] * 3,
        compiler_params=_cp("parallel", "parallel"),
    )(w, g, m, v)


def adamw_small(packed, w, m, v):
    chunks = _row_chunks()
    names = SMALL + ("final_norm_g",)
    as_2d = lambda a: a.reshape(1, -1) if a.ndim == 1 else a
    ins = [as_2d(t[n]) for n in names for t in (w, m, v)]
    shapes = [jax.ShapeDtypeStruct(as_2d(w[n]).shape, F32) for n in names for _ in range(4)]
    n_in = len(ins)

    def body(p_ref, *refs):
        in_refs, out_refs = refs[:n_in], refs[n_in:]
        chip = _place()[3]
        for i, n in enumerate(names):
            w_ref, m_ref, v_ref = in_refs[3 * i:3 * i + 3]
            outs = out_refs[4 * i:4 * i + 4]

            def step(at, g):
                res = (g,) + _adam_math(w_ref[at], g, m_ref[at], v_ref[at])
                for o_ref, val in zip(outs, res):
                    o_ref[at] = val

            if n == "final_norm_g":
                step((slice(None), slice(None)), p_ref[FINAL_ROW:FINAL_ROW + 1, :])
                continue
            for l in range(DEPTH):
                if n == "conv_w":
                    r0 = l * LAYER_ROWS + TAPS_ROW
                    shard = CONV_C // N_CHIPS
                    g = jnp.zeros((CONV_K, shard), F32)
                    for j in range(N_CHIPS):
                        g = jnp.where(chip == j, p_ref[r0:r0 + CONV_K, j * shard:(j + 1) * shard], g)
                    step((l,), g)
                else:
                    for r, c0, wd in chunks[n]:
                        step((slice(l, l + 1), slice(c0, c0 + wd)),
                             p_ref[l * LAYER_ROWS + r:l * LAYER_ROWS + r + 1, 0:wd])

    vm = pl.BlockSpec(memory_space=pltpu.VMEM)
    res = pl.pallas_call(
        body, name="adamw_small", out_shape=shapes,
        in_specs=[vm] + [vm] * n_in, out_specs=[vm] * len(shapes),
    )(packed, *ins)
    dicts = ({}, {}, {}, {})
    for i, n in enumerate(names):
        for d, val in zip(dicts, res[4 * i:4 * i + 4]):
            d[n] = val.reshape(w[n].shape)
    return dicts


def _flat_rows(name, shard):
    return shard.T if name == "w_in" else shard.reshape(-1, FLAT_W)


def _full_matrix(slabs, name):
    K, N = FULL_SHAPES[name]
    if name == "w_in":
        return slabs.reshape(N, K)
    if name in COL_SHARDED:
        return slabs.reshape(N_CHIPS, K, N // N_CHIPS).transpose(1, 0, 2).reshape(K, N)
    return slabs.reshape(K, N)


def _first_row(parts, name):
    r = 0
    for n, rows in parts:
        if n == name:
            return r, rows
        r += rows
    raise KeyError(name)


class _Exchange:
    CARRIERS = {
        ("conv_bwd_ln", 1): ((1, "A"), "swap"), ("attn_bwd", 1): ((1, "A"), "exchange"), ("tn_in", 1): ((1, "A"), "share"),
        ("mlp_bwd", 0): ((1, "B"), "swap"), ("tn_mlp1", 0): ((1, "B"), "exchange"), ("tn_mlp2", 0): ((1, "B"), "share"),
        ("conv_bwd_ln", 0): ((0, "A"), "swap"), ("attn_bwd", 0): ((0, "A"), "exchange"), ("tn_in", 0): ((0, "A"), "share"),
    }

    def __init__(self, w, ci, chip):
        self.w, self.ci, self.chip = w, ci, chip
        self.wsh = [jnp.concatenate([_flat_rows(n, w[n][l]) for n, _ in FLAT_PARTS], axis=0).astype(CDT)
                    for l in range(DEPTH)]
        self.final_g = w["final_norm_g"].reshape(1, D)
        self.slabs = {}
        self.full = {}
        self.units = {}
        self.reduced = {}
        self._landed_weights(0, 0, _run_alone(gather_rider(self.wsh[0][:W_IN_ROWS]), "gather_w_in")[0])
        self.all_taps = gather_taps(w["conv_w"])

    def _landed_weights(self, l, r0, buf):
        self.slabs.setdefault(l, []).append((r0, buf))

    def _matrix(self, l, name):
        if (l, name) not in self.full:
            r, rows = _first_row(FLAT_PARTS, name)
            r0, buf = next((r0, buf) for r0, buf in self.slabs[l] if r0 <= r < r0 + buf.shape[1])
            self.full[(l, name)] = _full_matrix(buf[:, r - r0:r - r0 + rows], name)
        return self.full[(l, name)]

    def w_in(self, l):
        return self._matrix(l, "w_in")

    def mats(self, l):
        return {n: self._matrix(l, n) for n in MATRICES if n != "w_in"}

    def vec(self, n, l):
        return self.w[n][l].reshape(1, -1)

    def sinks(self, l):
        return self.w["sinks"][l]

    def taps(self, l):
        return self.all_taps[l]

    def rider(self, kernel, l):
        if (kernel, l) == ("rms_inproj", 0):
            return gather_rider(self.wsh[0][W_IN_ROWS:])
        if (kernel, l) == ("mlp_fwd", 0):
            return gather_rider(self.wsh[1])
        if (kernel, l) in self.CARRIERS:
            return self._stage(*self.CARRIERS[(kernel, l)])
        return None

    def landed(self, kernel, l, bufs):
        if (kernel, l) == ("rms_inproj", 0):
            self._landed_weights(0, W_IN_ROWS, bufs[0])
        elif (kernel, l) == ("mlp_fwd", 0):
            self._landed_weights(1, 0, bufs[0])
        else:
            self._stage_landed(*self.CARRIERS[(kernel, l)], bufs[0])

    def grads(self, l, group, g):
        if group == "A":
            flat = jnp.concatenate([g[n].reshape(N_CHIPS, rows, FLAT_W) for n, rows in GROUP_A], axis=1)
        else:
            flat = g["w_in"].reshape(N_CHIPS, W_IN_ROWS, FLAT_W)
        self.units[(l, group)] = {"g": flat}

    def _stage(self, key, stage):
        u = self.units[key]
        if stage == "swap":
            return swap_rider(u["g"])
        if stage == "exchange":
            u["pb"], u["own"] = pair_sum(u["g"], u["swap"])
            return exchange_rider(u["pb"])
        u["tot"] = total_sum(u["own"], u["exchange"])
        return share_rider(u["tot"])

    def _stage_landed(self, key, stage, buf):
        u = self.units[key]
        u[stage] = buf
        if stage == "share":
            tot = u["tot"]
            self.reduced[key] = jnp.where(self.ci == 0, jnp.concatenate([tot, buf]), jnp.concatenate([buf, tot]))

    def finish(self):
        key = (0, "B")
        for stage in ("swap", "exchange", "share"):
            self._stage_landed(key, stage, _run_alone(self._stage(key, stage), stage + "_last")[0])
        out = {}
        for n in MATRICES:
            per_layer = []
            for l in range(DEPTH):
                if n == "w_in":
                    per_layer.append(self.reduced[(l, "B")].T)
                    continue
                r, rows = _first_row(GROUP_A, n)
                per_layer.append(self.reduced[(l, "A")][r:r + rows].reshape(self.w[n].shape[1:]))
            out[n] = jnp.stack(per_layer)
        return out


WEIGHTS = ("mix_norm_g", "w_in", "b_in", "sinks", "conv_w", "conv_b", "conv_ln_g", "conv_ln_b", "w_attn_proj",
           "w_conv_proj", "b_conv_proj", "w_out", "mlp_norm_g", "w_mlp1", "w_mlp2", "final_norm_g")


def kernel(x, mix_norm_g, w_in, b_in, sinks, conv_w, conv_b, conv_ln_g, conv_ln_b, w_attn_proj, w_conv_proj, b_conv_proj, w_out, mlp_norm_g, w_mlp1, w_mlp2, final_norm_g, loss_target, m_mix_norm_g, m_w_in, m_b_in, m_sinks, m_conv_w, m_conv_b, m_conv_ln_g, m_conv_ln_b, m_w_attn_proj, m_w_conv_proj, m_b_conv_proj, m_w_out, m_mlp_norm_g, m_w_mlp1, m_w_mlp2, m_final_norm_g, v_mix_norm_g, v_w_in, v_b_in, v_sinks, v_conv_w, v_conv_b, v_conv_ln_g, v_conv_ln_b, v_w_attn_proj, v_w_conv_proj, v_b_conv_proj, v_w_out, v_mlp_norm_g, v_w_mlp1, v_w_mlp2, v_final_norm_g):
    w = dict(zip(WEIGHTS, (mix_norm_g, w_in, b_in, sinks, conv_w, conv_b, conv_ln_g, conv_ln_b, w_attn_proj, w_conv_proj,
                           b_conv_proj, w_out, mlp_norm_g, w_mlp1, w_mlp2, final_norm_g)))
    m = dict(zip(WEIGHTS, (m_mix_norm_g, m_w_in, m_b_in, m_sinks, m_conv_w, m_conv_b, m_conv_ln_g, m_conv_ln_b, m_w_attn_proj,
                           m_w_conv_proj, m_b_conv_proj, m_w_out, m_mlp_norm_g, m_w_mlp1, m_w_mlp2, m_final_norm_g)))
    v = dict(zip(WEIGHTS, (v_mix_norm_g, v_w_in, v_b_in, v_sinks, v_conv_w, v_conv_b, v_conv_ln_g, v_conv_ln_b, v_w_attn_proj,
                           v_w_conv_proj, v_b_conv_proj, v_w_out, v_mlp_norm_g, v_w_mlp1, v_w_mlp2, v_final_norm_g)))
    xi, yi, ci = lax.axis_index("x"), lax.axis_index("y"), lax.axis_index("c")
    chip = 2 * xi + yi

    hooks = _Exchange(w, ci, chip)
    loss, dx, gsm = forward_backward(x[0], loss_target[0], hooks)
    loss = lax.psum(loss[0, 0], ("x", "y", "c"))
    grads = hooks.finish()

    gsmall, delta, new_m, new_v = adamw_small(sum_small(gsm), w, m, v)
    grads.update(gsmall)
    for n in MATRICES:
        t = (lambda a: jnp.swapaxes(a, 1, 2)) if n == "w_in" else (lambda a: a)
        delta[n], new_m[n], new_v[n] = map(t, adamw(t(w[n]), t(grads[n]), t(m[n]), t(v[n]), name="adamw_" + n))

    return (loss, dx[None], *[grads[n] for n in WEIGHTS], *[delta[n] for n in WEIGHTS],
            *[new_m[n] for n in WEIGHTS], *[new_v[n] for n in WEIGHTS])
```

```python
import functools
import math
from typing import Callable, NamedTuple

import jax
import jax.numpy as jnp
import numpy as np
from jax import lax
from jax.experimental import pallas as pl
from jax.experimental.pallas import tpu as pltpu

F32 = jnp.float32
CDT = jnp.bfloat16

D = 1024
DEPTH = 2
N_Q = 8
HEAD_DIM = 64
ATTN_W = 512
KV_W = 128
BLOCK = 128
CONV_C = 512
CONV_K = 31
D_FF = 4096
IN_W = 3840
QKV_W = ATTN_W + 2 * KV_W
REST_W = IN_W - QKV_W
EPS = 1e-6
NEG = -1e30
SCALE = 1.0 / math.sqrt(HEAD_DIM)
SLOPES = [float(2.0 ** (-8.0 * (h + 1) / N_Q)) for h in range(N_Q)]
SUBLANES = 8
HALO = 32

ADAM_LR = 0.001
ADAM_B1 = 0.9
ADAM_B2 = 0.999
ADAM_EPS = 1e-08
ADAM_WD = 0.01
ADAM_STEP = 10

VMEM_LIMIT = 56 * 1024 * 1024


def _cp(*sem):
    return pltpu.CompilerParams(dimension_semantics=sem, vmem_limit_bytes=VMEM_LIMIT)


def _dot(a, b):
    return jnp.dot(a, b, preferred_element_type=F32)


def _dot_nt(a, b):
    return lax.dot_general(a, b, (((1,), (1,)), ((), ())), preferred_element_type=F32)


def _dot_tn(a, b):
    return lax.dot_general(a, b, (((0,), (0,)), ((), ())), preferred_element_type=F32)


def _sig(x):
    return 1.0 / (1.0 + jnp.exp(-x))


def _colsum(v):
    return jnp.sum(v, axis=0, keepdims=True)


def _const(shape, buffers=None):
    mode = {} if buffers is None else {"pipeline_mode": pl.Buffered(buffers)}
    return pl.BlockSpec(shape, lambda *_: (0,) * len(shape), **mode)


class Rider(NamedTuple):
    ins: tuple
    outs: tuple
    n_sems: int
    start: Callable
    finish: Callable


def _any():
    return pl.BlockSpec(memory_space=pl.ANY)


def _run(body, rider, *, name, grid, in_specs, out_specs, out_shape, args, sem, scratch_shapes=()):
    if rider is None:
        return pl.pallas_call(body, name=name, grid=grid, in_specs=list(in_specs), out_specs=list(out_specs),
                              out_shape=list(out_shape), scratch_shapes=list(scratch_shapes),
                              compiler_params=_cp(*sem))(*args)
    n_in, n_out, n_sc = len(in_specs), len(out_specs), len(scratch_shapes)
    r_in, r_out = len(rider.ins), len(rider.outs)

    def riding(*refs):
        ins, rins = refs[:n_in], refs[n_in:n_in + r_in]
        o0 = n_in + r_in
        outs, routs = refs[o0:o0 + n_out], refs[o0 + n_out:o0 + n_out + r_out]
        s0 = o0 + n_out + r_out
        scratch, sems = refs[s0:s0 + n_sc], refs[s0 + n_sc]
        first = functools.reduce(jnp.logical_and, [pl.program_id(a) == 0 for a in range(len(grid))])
        last = functools.reduce(jnp.logical_and, [pl.program_id(a) == grid[a] - 1 for a in range(len(grid))])

        @pl.when(first)
        def _():
            rider.start(rins, routs, sems)

        body(*ins, *outs, *scratch)

        @pl.when(last)
        def _():
            rider.finish(rins, routs, sems)

    res = pl.pallas_call(
        riding, name=name, grid=grid, in_specs=list(in_specs) + [_any()] * r_in,
        out_specs=list(out_specs) + [_any()] * r_out, out_shape=list(out_shape) + list(rider.outs),
        scratch_shapes=list(scratch_shapes) + [pltpu.SemaphoreType.DMA((rider.n_sems,))],
        compiler_params=_cp(*["arbitrary"] * len(grid)))(*args, *rider.ins)
    return res[:n_out], res[n_out:]


def _run_alone(rider, name):
    def body(*refs):
        r_in, r_out = len(rider.ins), len(rider.outs)
        rins, routs, sems = refs[:r_in], refs[r_in:r_in + r_out], refs[r_in + r_out]
        rider.start(rins, routs, sems)
        rider.finish(rins, routs, sems)

    return pl.pallas_call(
        body, name=name, in_specs=[_any()] * len(rider.ins), out_specs=[_any()] * len(rider.outs),
        out_shape=list(rider.outs), scratch_shapes=[pltpu.SemaphoreType.DMA((rider.n_sems,))])(*rider.ins)


def rms_inproj(x, g, wt, b, *, tm=512, rider=None):
    T = x.shape[0]

    def body(x_ref, g_ref, w_ref, b_ref, qkv_ref, rest_ref):
        xv = x_ref[...]
        r = lax.rsqrt(jnp.mean(xv * xv, axis=-1, keepdims=True) + EPS)
        h = (xv * r * g_ref[...]).astype(CDT)
        qkv_ref[...] = (_dot_nt(h, w_ref[0:QKV_W, :]) + b_ref[:, 0:QKV_W]).astype(qkv_ref.dtype)
        for j in range(REST_W // D):
            c0 = QKV_W + D * j
            rest_ref[:, D * j:D * (j + 1)] = _dot_nt(h, w_ref[c0:c0 + D, :]) + b_ref[:, c0:c0 + D]

    return _run(
        body, rider, name="rms_inproj", grid=(T // tm,),
        in_specs=[pl.BlockSpec((tm, D), lambda i: (i, 0)), _const((1, D)), _const((IN_W, D), 1), _const((1, IN_W))],
        out_specs=[pl.BlockSpec((tm, QKV_W), lambda i: (i, 0)), pl.BlockSpec((tm, REST_W), lambda i: (i, 0))],
        out_shape=[jax.ShapeDtypeStruct((T, QKV_W), CDT), jax.ShapeDtypeStruct((T, REST_W), F32)],
        sem=("parallel",), args=(x, g, wt, b))


def _lane_halves(shape):
    lane = lax.broadcasted_iota(jnp.int32, shape, 1)
    return lane < HEAD_DIM, lane >= HEAD_DIM


def _swap_halves(v):
    return pltpu.roll(v.astype(F32), HEAD_DIM, axis=1).astype(v.dtype)


N_KV = KV_W // HEAD_DIM
GROUP = N_Q // N_KV
STACK = GROUP * BLOCK


def _score_bias():
    row = np.arange(STACK)[:, None] % BLOCK
    col = np.arange(2 * BLOCK)[None, :]
    dist = row + BLOCK - col
    window = (dist >= 0) & (dist < BLOCK)
    slopes = np.asarray(SLOPES, np.float32).reshape(N_KV, GROUP)
    out = np.empty((2, N_KV, STACK, 2 * BLOCK), np.float32)
    for first in range(2):
        valid = window & ((col >= BLOCK) | (first == 0))
        for g in range(N_KV):
            slope = np.repeat(slopes[g], BLOCK)[:, None]
            out[first, g] = np.where(valid, -(slope * dist.astype(np.float32)), np.float32(NEG))
    return jnp.asarray(out)


def _per_head_column(vals):
    row = lax.broadcasted_iota(jnp.int32, (STACK, 1), 0)
    col = jnp.full((STACK, 1), vals[GROUP - 1], F32)
    for i in reversed(range(GROUP - 1)):
        col = jnp.where(row < (i + 1) * BLOCK, vals[i], col)
    return col


def _stack_heads(dst, src_ref, r0, g, scale=None):
    lane = lax.broadcasted_iota(jnp.int32, (BLOCK, 2 * HEAD_DIM), 1)
    keep = (lane >= HEAD_DIM) if g else (lane < HEAD_DIM)
    for i in range(GROUP):
        h = GROUP * g + i
        tile = src_ref[pl.ds(r0, BLOCK), (h // 2) * 128:(h // 2 + 1) * 128]
        if h % 2 != g:
            tile = _swap_halves(tile)
        if scale is not None:
            tile = tile * jnp.asarray(scale, tile.dtype)
        dst[i * BLOCK:(i + 1) * BLOCK, :] = jnp.where(keep, tile, jnp.zeros_like(tile))


def _unstack_heads(dst_ref, stacked, r0, g):
    lane = lax.broadcasted_iota(jnp.int32, (BLOCK, 2 * HEAD_DIM), 1)
    tiles = []
    for j in range(GROUP // 2):
        even = stacked[(2 * j) * BLOCK:(2 * j + 1) * BLOCK, :]
        odd = stacked[(2 * j + 1) * BLOCK:(2 * j + 2) * BLOCK, :]
        lo = _swap_halves(even) if g else even
        hi = odd if g else _swap_halves(odd)
        c0 = ((GROUP * g) // 2 + j) * 128
        tile = jnp.where(lane < HEAD_DIM, lo, hi)
        dst_ref[pl.ds(r0, BLOCK), c0:c0 + 128] = tile.astype(dst_ref.dtype)
        tiles.append((c0, tile))
    return tiles


def _qkv_specs(tq):
    nb = tq // BLOCK
    return [
        pl.BlockSpec((tq, ATTN_W), lambda i: (i, 0)),
        pl.BlockSpec((BLOCK, KV_W), lambda i: (jnp.maximum(i * nb - 1, 0), ATTN_W // KV_W)),
        pl.BlockSpec((tq, KV_W), lambda i: (i, ATTN_W // KV_W)),
        pl.BlockSpec((BLOCK, KV_W), lambda i: (jnp.maximum(i * nb - 1, 0), ATTN_W // KV_W + 1)),
        pl.BlockSpec((tq, KV_W), lambda i: (i, ATTN_W // KV_W + 1)),
    ]


def attn_fwd(qkv, sinks, *, tq=512):
    T = qkv.shape[0]
    nb = tq // BLOCK

    def body(sink_ref, bias_ref, q_ref, kp_ref, kc_ref, vp_ref, vc_ref, o_ref, lse_ref, kext, vext, qs):
        i = pl.program_id(0)
        kext[0:BLOCK, :] = kp_ref[...]
        kext[BLOCK:, :] = kc_ref[...]
        vext[0:BLOCK, :] = vp_ref[...]
        vext[BLOCK:, :] = vc_ref[...]
        lane_l = lax.broadcasted_iota(jnp.int32, (BLOCK, 128), 1)

        def blk(b, carry):
            r0 = pl.multiple_of(b * BLOCK, BLOCK)
            first = jnp.logical_and(i == 0, b == 0).astype(jnp.int32)
            kc = kext[pl.ds(r0, 2 * BLOCK), :]
            vc = vext[pl.ds(r0, 2 * BLOCK), :]
            lse_t = jnp.zeros((BLOCK, 128), F32)
            for g in range(N_KV):
                heads = range(GROUP * g, GROUP * (g + 1))
                _stack_heads(qs.at[g], q_ref, r0, g, SCALE)
                s = _dot_nt(qs[g], kc) + bias_ref[first, g]
                sink = _per_head_column([sink_ref[h] for h in heads])
                m = jnp.maximum(jnp.max(s, axis=-1, keepdims=True), sink)
                p = jnp.exp(s - m)
                denom = jnp.sum(p, axis=-1, keepdims=True) + jnp.exp(sink - m)
                p = p / denom
                _unstack_heads(o_ref, _dot(p.astype(CDT), vc), r0, g)
                lse = m + jnp.log(denom)
                for i_h, h in enumerate(heads):
                    lse_t = jnp.where(lane_l == h, lse[i_h * BLOCK:(i_h + 1) * BLOCK, :], lse_t)
            lse_ref[pl.ds(r0, BLOCK), :] = lse_t
            return carry

        lax.fori_loop(0, nb, blk, 0)

    return pl.pallas_call(
        body, name="attn_fwd", grid=(T // tq,),
        in_specs=[pl.BlockSpec(memory_space=pltpu.SMEM), _const((2, N_KV, STACK, 2 * BLOCK), 1)] + _qkv_specs(tq),
        out_specs=[pl.BlockSpec((tq, ATTN_W), lambda i: (i, 0)), pl.BlockSpec((tq, 128), lambda i: (i, 0))],
        out_shape=[jax.ShapeDtypeStruct((T, ATTN_W), CDT), jax.ShapeDtypeStruct((T, 128), F32)],
        scratch_shapes=[pltpu.VMEM((tq + BLOCK, KV_W), CDT), pltpu.VMEM((tq + BLOCK, KV_W), CDT),
                        pltpu.VMEM((N_KV, STACK, 2 * HEAD_DIM), CDT)],
        compiler_params=_cp("parallel"),
    )(sinks, _score_bias(), qkv, qkv, qkv, qkv, qkv)


def _halo_before(tm, width, col):
    return pl.BlockSpec((HALO, width), lambda i: (jnp.maximum(i * (tm // HALO) - 1, 0), col))


def _fill_u0(ext, a_ref, b_ref, ha_ref, hb_ref, first):
    hu = ha_ref[...] * _sig(hb_ref[...])
    ext[0:HALO, :] = jnp.where(first, jnp.zeros_like(hu), hu)
    ext[HALO:, :] = a_ref[...] * _sig(b_ref[...])


def _shifted_taps(src, w_ref, base, rc, offsets):
    acc = jnp.zeros((rc, CONV_C), F32)
    for b in range(SUBLANES):
        taps = [(k, o - b) for k, o in enumerate(offsets) if o % SUBLANES == b]
        if not taps:
            continue
        rows = rc if b == 0 else rc + SUBLANES
        part = jnp.zeros((rows, CONV_C), F32)
        for k, o8 in taps:
            part = part + w_ref[k:k + 1, :] * src[base + o8:base + o8 + rows, :]
        acc = acc + (part if b == 0 else part[b:b + rc, :])
    return acc


def _shift_copies(dst, src, rows):
    for b in range(1, SUBLANES):
        for r in range(0, rows, 64):
            n = min(64, rows - r)
            dst[b - 1, r:r + n, :] = src[r + b:r + b + n, :]


def _window(src, copies, off, r0, n):
    b = off % SUBLANES
    a = r0 + off - b
    return src[a:a + n, :] if b == 0 else copies[b - 1, a:a + n, :]


def _conv_rows(ext, w_ref, r0, rc):
    return _shifted_taps(ext, w_ref, r0, rc, [HALO - (CONV_K - 1) + k for k in range(CONV_K)])


def _layer_norm(u1, g, b):
    mu = jnp.mean(u1, axis=-1, keepdims=True)
    xc = u1 - mu
    rstd = lax.rsqrt(jnp.mean(xc * xc, axis=-1, keepdims=True) + EPS)
    n = xc * rstd
    return n, rstd, n * g + b


CONV_RC = 32
CONV_RC_1PASS = 64


def conv_fwd(rest, cw, cb, lg, lb, *, tm=512):
    T = rest.shape[0]

    def body(a_ref, b_ref, ha_ref, hb_ref, w_ref, cb_ref, lg_ref, lb_ref, o_ref, u1_ref, ext):
        _fill_u0(ext, a_ref, b_ref, ha_ref, hb_ref, pl.program_id(0) == 0)
        rc = CONV_RC_1PASS
        for r0 in range(0, tm, rc):
            u1 = _conv_rows(ext, w_ref, r0, rc) + cb_ref[...]
            u1_ref[r0:r0 + rc, :] = u1
            _, _, u2 = _layer_norm(u1, lg_ref[...], lb_ref[...])
            o_ref[r0:r0 + rc, :] = (u2 * _sig(u2)).astype(o_ref.dtype)

    row = lambda i: (i, 0)
    return pl.pallas_call(
        body, name="conv_fwd", grid=(T // tm,),
        in_specs=[pl.BlockSpec((tm, CONV_C), row), pl.BlockSpec((tm, CONV_C), lambda i: (i, 1)),
                  _halo_before(tm, CONV_C, 0), _halo_before(tm, CONV_C, 1),
                  _const((CONV_K, CONV_C)), _const((1, CONV_C)), _const((1, CONV_C)), _const((1, CONV_C))],
        out_specs=[pl.BlockSpec((tm, CONV_C), row), pl.BlockSpec((tm, CONV_C), row)],
        out_shape=[jax.ShapeDtypeStruct((T, CONV_C), CDT), jax.ShapeDtypeStruct((T, CONV_C), F32)],
        scratch_shapes=[pltpu.VMEM((tm + HALO, CONV_C), F32)],
        compiler_params=_cp("parallel"),
    )(rest, rest, rest, rest, cw, cb, lg, lb)


def merge_out(x, attn, u3, rest, wa, wc, bc, wo, *, tm=512):
    T = x.shape[0]

    def body(x_ref, at_ref, u_ref, ga_ref, gc_ref, wa_ref, wc_ref, bc_ref, wo_ref, o_ref):
        br_a = _dot(at_ref[...], wa_ref[...])
        br_c = _dot(u_ref[...], wc_ref[...]) + bc_ref[...]
        merged = _sig(ga_ref[...]) * br_a + _sig(gc_ref[...]) * br_c
        o_ref[...] = x_ref[...] + _dot(merged.astype(CDT), wo_ref[...])

    return pl.pallas_call(
        body, name="merge_out", grid=(T // tm,),
        in_specs=[pl.BlockSpec((tm, D), lambda i: (i, 0)), pl.BlockSpec((tm, ATTN_W), lambda i: (i, 0)),
                  pl.BlockSpec((tm, CONV_C), lambda i: (i, 0)),
                  pl.BlockSpec((tm, D), lambda i: (i, 1)), pl.BlockSpec((tm, D), lambda i: (i, 2)),
                  _const((ATTN_W, D), 1), _const((CONV_C, D), 1), _const((1, D)), _const((D, D), 1)],
        out_specs=pl.BlockSpec((tm, D), lambda i: (i, 0)),
        out_shape=jax.ShapeDtypeStruct((T, D), F32),
        compiler_params=_cp("parallel"),
    )(x, attn, u3, rest, rest, wa, wc, bc, wo)


def _loss_and_grad(xv, gv, tgt):
    r = lax.rsqrt(jnp.mean(xv * xv, axis=-1, keepdims=True) + EPS)
    e = xv * r * gv - tgt
    dx, dg_rows = _rms_bwd(xv, gv, e * (1.0 / D))
    return 0.5 * jnp.mean(e * e, axis=-1, keepdims=True), dx, dg_rows


def mlp_fwd(x, g, w1, w2, *, head=None, tm=256, tf=D_FF, rider=None):
    T = x.shape[0]
    nf = D_FF // tf

    def body(x_ref, g_ref, w1_ref, w2_ref, *rest):
        if head is None:
            o_ref, pre_ref, h_s, acc_s = rest
        else:
            gf_ref, t_ref, pre_ref, dy_ref, dgf_ref, loss_ref, h_s, acc_s = rest
        i, f = pl.program_id(0), pl.program_id(1)

        @pl.when(f == 0)
        def _():
            xv = x_ref[...]
            r = lax.rsqrt(jnp.mean(xv * xv, axis=-1, keepdims=True) + EPS)
            h_s[...] = (xv * r * g_ref[...]).astype(CDT)
            acc_s[...] = jnp.zeros_like(acc_s)

        pre = _dot(h_s[...], w1_ref[...])
        pre_ref[...] = pre
        a = jnp.square(jnp.maximum(pre, 0.0))
        acc_s[...] += _dot(a.astype(CDT), w2_ref[...])

        @pl.when(f == nf - 1)
        def _():
            y = x_ref[...] + acc_s[...]
            if head is None:
                o_ref[...] = y
                return

            @pl.when(i == 0)
            def _():
                dgf_ref[...] = jnp.zeros_like(dgf_ref)
                loss_ref[...] = jnp.zeros_like(loss_ref)

            loss_rows, dy, dg_rows = _loss_and_grad(y, gf_ref[...], t_ref[...])
            dy_ref[...] = dy
            dgf_ref[...] += _colsum(dg_rows)
            loss_ref[...] += _colsum(loss_rows)

    mode = {"pipeline_mode": pl.Buffered(1)} if nf == 1 else {}
    row = pl.BlockSpec((tm, D), lambda i, f: (i, 0))
    pre_spec, pre_shape = pl.BlockSpec((tm, tf), lambda i, f: (i, f)), jax.ShapeDtypeStruct((T, D_FF), F32)
    in_specs = [row, _const((1, D)), pl.BlockSpec((D, tf), lambda i, f: (0, f), **mode),
                pl.BlockSpec((tf, D), lambda i, f: (f, 0), **mode)]
    scratch = [pltpu.VMEM((tm, D), CDT), pltpu.VMEM((tm, D), F32)]
    if head is None:
        return _run(body, rider, name="mlp_fwd", grid=(T // tm, nf), in_specs=in_specs, out_specs=[row, pre_spec],
                    out_shape=[jax.ShapeDtypeStruct((T, D), F32), pre_shape], scratch_shapes=scratch,
                    sem=("parallel", "arbitrary"), args=(x, g, w1, w2))
    return _run(body, rider, name="mlp_fwd_loss", grid=(T // tm, nf), in_specs=in_specs + [_const((1, D)), row],
                out_specs=[pre_spec, row, _const((1, D)), _const((1, 128))],
                out_shape=[pre_shape, jax.ShapeDtypeStruct((T, D), F32), jax.ShapeDtypeStruct((1, D), F32),
                           jax.ShapeDtypeStruct((1, 128), F32)],
                scratch_shapes=scratch, sem=("arbitrary", "arbitrary"), args=(x, g, w1, w2) + tuple(head))


def _rms_bwd(xv, g, dh):
    r = lax.rsqrt(jnp.mean(xv * xv, axis=-1, keepdims=True) + EPS)
    xhat = xv * r
    dxh = dh * g
    dx = r * (dxh - xhat * jnp.mean(dxh * xhat, axis=-1, keepdims=True))
    return dx, dh * xhat


def mlp_bwd(dy, x, g, pre, w1, w2, *, tm=256, tf=D_FF, rider=None):
    T = x.shape[0]
    nf = D_FF // tf

    def body(dy_ref, x_ref, g_ref, pre_ref, w1_ref, w2_ref, dx_ref, dg_ref, h_ref, a_ref, dpre_ref, dyb_s, acc_s):
        i, f = pl.program_id(0), pl.program_id(1)

        @pl.when(jnp.logical_and(i == 0, f == 0))
        def _():
            dg_ref[...] = jnp.zeros_like(dg_ref)

        @pl.when(f == 0)
        def _():
            dyb_s[...] = dy_ref[...].astype(CDT)
            acc_s[...] = jnp.zeros_like(acc_s)

        pre = pre_ref[...]
        rl = jnp.maximum(pre, 0.0)
        a_ref[...] = (rl * rl).astype(CDT)
        da = _dot_nt(dyb_s[...], w2_ref[...])
        dpre = (da * (2.0 * rl)).astype(CDT)
        dpre_ref[...] = dpre
        acc_s[...] += _dot_nt(dpre, w1_ref[...])

        @pl.when(f == nf - 1)
        def _():
            xv = x_ref[...]
            gv = g_ref[...]
            dxn, dg_rows = _rms_bwd(xv, gv, acc_s[...])
            dx_ref[...] = dy_ref[...] + dxn
            dg_ref[...] += _colsum(dg_rows)
            r = lax.rsqrt(jnp.mean(xv * xv, axis=-1, keepdims=True) + EPS)
            h_ref[...] = (xv * r * gv).astype(CDT)

    row = lambda i, f: (i, 0)
    mode = {"pipeline_mode": pl.Buffered(1)} if nf == 1 else {}
    return _run(
        body, rider, name="mlp_bwd", grid=(T // tm, nf),
        in_specs=[pl.BlockSpec((tm, D), row), pl.BlockSpec((tm, D), row), _const((1, D)),
                  pl.BlockSpec((tm, tf), lambda i, f: (i, f)),
                  pl.BlockSpec((D, tf), lambda i, f: (0, f), **mode), pl.BlockSpec((tf, D), lambda i, f: (f, 0), **mode)],
        out_specs=[pl.BlockSpec((tm, D), row), _const((1, D)), pl.BlockSpec((tm, D), row),
                   pl.BlockSpec((tm, tf), lambda i, f: (i, f)), pl.BlockSpec((tm, tf), lambda i, f: (i, f))],
        out_shape=[jax.ShapeDtypeStruct((T, D), F32), jax.ShapeDtypeStruct((1, D), F32),
                   jax.ShapeDtypeStruct((T, D), CDT), jax.ShapeDtypeStruct((T, D_FF), CDT),
                   jax.ShapeDtypeStruct((T, D_FF), CDT)],
        scratch_shapes=[pltpu.VMEM((tm, D), CDT), pltpu.VMEM((tm, D), F32)],
        sem=("arbitrary", "arbitrary"), args=(dy, x, g, pre, w1, w2))


def tn_matmul(a, b, *, tm, tn, tk=2048, name, by_chip=False, rider=None):
    T, M = a.shape
    N = b.shape[1]
    tk = min(tk, T)
    nk = T // tk

    def body(a_ref, b_ref, o_ref):
        @pl.when(pl.program_id(2) == 0)
        def _():
            o_ref[...] = jnp.zeros_like(o_ref)

        o_ref[...] += _dot_tn(a_ref[...].astype(CDT), b_ref[...].astype(CDT))

    if by_chip:
        out_spec = pl.BlockSpec((None, tm, tn), lambda i, j, k: (j, i, 0))
        out_shape = jax.ShapeDtypeStruct((N // tn, M, tn), F32)
    else:
        out_spec = pl.BlockSpec((tm, tn), lambda i, j, k: (i, j))
        out_shape = jax.ShapeDtypeStruct((M, N), F32)
    res = _run(
        body, rider, name=name, grid=(M // tm, N // tn, nk),
        in_specs=[pl.BlockSpec((tk, tm), lambda i, j, k: (k, i)), pl.BlockSpec((tk, tn), lambda i, j, k: (k, j))],
        out_specs=[out_spec], out_shape=[out_shape], sem=("parallel", "parallel", "arbitrary"), args=(a, b))
    return res[0] if rider is None else (res[0][0], res[1])


def merge_bwd(dx1, attn, u3, rest, wa, wc, bc, wo, *, tm=512):
    T = dx1.shape[0]

    def body(dx_ref, at_ref, u_ref, ga_ref, gc_ref, wa_ref, wc_ref, bc_ref, wo_ref,
             mg_ref, dba_ref, dbc_ref, dat_ref, du_ref, dgate_ref, dgsum_ref, dbias_ref):
        @pl.when(pl.program_id(0) == 0)
        def _():
            dbias_ref[...] = jnp.zeros_like(dbias_ref)
            dgsum_ref[...] = jnp.zeros_like(dgsum_ref)

        br_a = _dot(at_ref[...], wa_ref[...])
        br_c = _dot(u_ref[...], wc_ref[...]) + bc_ref[...]
        sa = _sig(ga_ref[...])
        sc = _sig(gc_ref[...])
        mg_ref[...] = (sa * br_a + sc * br_c).astype(CDT)
        dm = _dot_nt(dx_ref[...].astype(CDT), wo_ref[...])
        dba = dm * sa
        dbc = dm * sc
        dga = dm * br_a * sa * (1.0 - sa)
        dgc = dm * br_c * sc * (1.0 - sc)
        dgate_ref[:, 0:D] = dga.astype(CDT)
        dgate_ref[:, D:2 * D] = dgc.astype(CDT)
        dgsum_ref[:, 0:D] += _colsum(dga)
        dgsum_ref[:, D:2 * D] += _colsum(dgc)
        dbias_ref[...] += _colsum(dbc)
        dba_b = dba.astype(CDT)
        dbc_b = dbc.astype(CDT)
        dba_ref[...] = dba_b
        dbc_ref[...] = dbc_b
        dat_ref[...] = _dot_nt(dba_b, wa_ref[...]).astype(CDT)
        du_ref[...] = _dot_nt(dbc_b, wc_ref[...])

    row = lambda i: (i, 0)
    return pl.pallas_call(
        body, name="merge_bwd", grid=(T // tm,),
        in_specs=[pl.BlockSpec((tm, D), row), pl.BlockSpec((tm, ATTN_W), row), pl.BlockSpec((tm, CONV_C), row),
                  pl.BlockSpec((tm, D), lambda i: (i, 1)), pl.BlockSpec((tm, D), lambda i: (i, 2)),
                  _const((ATTN_W, D), 1), _const((CONV_C, D), 1), _const((1, D)), _const((D, D), 1)],
        out_specs=[pl.BlockSpec((tm, D), row), pl.BlockSpec((tm, D), row), pl.BlockSpec((tm, D), row),
                   pl.BlockSpec((tm, ATTN_W), row), pl.BlockSpec((tm, CONV_C), row),
                   pl.BlockSpec((tm, 2 * D), row), _const((1, 2 * D)), _const((1, D))],
        out_shape=[jax.ShapeDtypeStruct((T, D), CDT), jax.ShapeDtypeStruct((T, D), CDT),
                   jax.ShapeDtypeStruct((T, D), CDT), jax.ShapeDtypeStruct((T, ATTN_W), CDT),
                   jax.ShapeDtypeStruct((T, CONV_C), F32), jax.ShapeDtypeStruct((T, 2 * D), CDT),
                   jax.ShapeDtypeStruct((1, 2 * D), F32), jax.ShapeDtypeStruct((1, D), F32)],
        compiler_params=_cp("arbitrary"),
    )(dx1, attn, u3, rest, rest, wa, wc, bc, wo)


def conv_bwd_ln(du3, u1, lg, lb, *, tm=512, rider=None):
    T = du3.shape[0]

    def body(du_ref, u1_ref, lg_ref, lb_ref, du1_ref, dlg_ref, dlb_ref, dcb_ref):
        @pl.when(pl.program_id(0) == 0)
        def _():
            dlg_ref[...] = jnp.zeros_like(dlg_ref)
            dlb_ref[...] = jnp.zeros_like(dlb_ref)
            dcb_ref[...] = jnp.zeros_like(dcb_ref)

        dlg = jnp.zeros((1, CONV_C), F32)
        dlb = jnp.zeros((1, CONV_C), F32)
        dcb = jnp.zeros((1, CONV_C), F32)
        rc = CONV_RC_1PASS
        for r0 in range(0, tm, rc):
            n, rstd, u2 = _layer_norm(u1_ref[r0:r0 + rc, :], lg_ref[...], lb_ref[...])
            s = _sig(u2)
            du2 = du_ref[r0:r0 + rc, :] * (s + u2 * s * (1.0 - s))
            dn = du2 * lg_ref[...]
            du1 = rstd * (dn - jnp.mean(dn, axis=-1, keepdims=True) - n * jnp.mean(dn * n, axis=-1, keepdims=True))
            du1_ref[r0:r0 + rc, :] = du1
            dlg = dlg + _colsum(du2 * n)
            dlb = dlb + _colsum(du2)
            dcb = dcb + _colsum(du1)
        dlg_ref[...] += dlg
        dlb_ref[...] += dlb
        dcb_ref[...] += dcb

    row = lambda i: (i, 0)
    vec = jax.ShapeDtypeStruct((1, CONV_C), F32)
    return _run(
        body, rider, name="conv_bwd_ln", grid=(T // tm,),
        in_specs=[pl.BlockSpec((tm, CONV_C), row), pl.BlockSpec((tm, CONV_C), row), _const((1, CONV_C)), _const((1, CONV_C))],
        out_specs=[pl.BlockSpec((tm, CONV_C), row), _const((1, CONV_C)), _const((1, CONV_C)), _const((1, CONV_C))],
        out_shape=[jax.ShapeDtypeStruct((T, CONV_C), F32), vec, vec, vec],
        sem=("arbitrary",), args=(du3, u1, lg, lb))


def conv_bwd_taps(du1, rest, cw, *, tm=512):
    T = du1.shape[0]
    nt = T // tm

    def body(d_ref, hd_ref, a_ref, b_ref, ha_ref, hb_ref, w_ref, dglu_ref, dgsum_ref, dw_ref, ext, dext, dcopies, dwacc):
        i = pl.program_id(0)

        @pl.when(i == 0)
        def _():
            dwacc[...] = jnp.zeros_like(dwacc)
            dgsum_ref[...] = jnp.zeros_like(dgsum_ref)

        rc = CONV_RC
        groups = lambda v: jnp.sum(v.reshape(v.shape[0] // SUBLANES, SUBLANES, CONV_C), axis=0)
        sums = [jnp.zeros((SUBLANES, CONV_C), F32), jnp.zeros((SUBLANES, CONV_C), F32)]
        _fill_u0(ext, a_ref, b_ref, ha_ref, hb_ref, i == 0)
        dext[0:SUBLANES, :] = jnp.zeros((SUBLANES, CONV_C), F32)
        dext[SUBLANES:SUBLANES + tm, :] = d_ref[...]
        hd = hd_ref[...]
        dext[SUBLANES + tm:, :] = jnp.where(i == nt - 1, jnp.zeros_like(hd), hd)
        _shift_copies(dcopies, dext, tm + HALO)
        for r0 in range(0, tm, rc):
            du0 = jnp.zeros((rc, CONV_C), F32)
            for k in range(CONV_K):
                du0 = du0 + w_ref[k:k + 1, :] * _window(dext, dcopies, SUBLANES + CONV_K - 1 - k, r0, rc)
            av = a_ref[r0:r0 + rc, :]
            sb = _sig(b_ref[r0:r0 + rc, :])
            for half, dg in enumerate((du0 * sb, du0 * av * sb * (1.0 - sb))):
                dglu_ref[r0:r0 + rc, half * CONV_C:(half + 1) * CONV_C] = dg.astype(CDT)
                sums[half] = sums[half] + groups(dg)
        for half in range(2):
            dgsum_ref[:, half * CONV_C:(half + 1) * CONV_C] += _colsum(sums[half])
        tail = lax.broadcasted_iota(jnp.int32, (SUBLANES, CONV_C), 0)
        for b in range(SUBLANES):
            taps = [(k, HALO - (CONV_K - 1) + k - b) for k in range(CONV_K) if (HALO - (CONV_K - 1) + k) % SUBLANES == b]
            accs = [jnp.zeros((SUBLANES, CONV_C), F32) for _ in taps]
            for r0 in list(range(0, tm, rc)) + ([tm] if b else []):
                n = rc if r0 < tm else SUBLANES
                dwin = _window(dext, dcopies, SUBLANES - b, r0, n)
                if r0 == tm:
                    dwin = jnp.where(tail < b, dwin, 0.0)
                for j, (k, o8) in enumerate(taps):
                    accs[j] = accs[j] + groups(dwin * ext[r0 + o8:r0 + o8 + n, :])
            for j, (k, _) in enumerate(taps):
                dwacc[8 * k:8 * k + 8, :] += accs[j]

        @pl.when(i == nt - 1)
        def _():
            dw_ref[...] = jnp.zeros_like(dw_ref)
            for k in range(CONV_K):
                dw_ref[k:k + 1, :] = _colsum(dwacc[8 * k:8 * k + 8, :])

    row = lambda i: (i, 0)
    return pl.pallas_call(
        body, name="conv_bwd_taps", grid=(nt,),
        in_specs=[pl.BlockSpec((tm, CONV_C), row),
                  pl.BlockSpec((HALO, CONV_C), lambda i: (jnp.minimum((i + 1) * (tm // HALO), T // HALO - 1), 0)),
                  pl.BlockSpec((tm, CONV_C), row), pl.BlockSpec((tm, CONV_C), lambda i: (i, 1)),
                  _halo_before(tm, CONV_C, 0), _halo_before(tm, CONV_C, 1), _const((CONV_K, CONV_C))],
        out_specs=[pl.BlockSpec((tm, 2 * CONV_C), row), _const((1, 2 * CONV_C)), _const((HALO, CONV_C))],
        out_shape=[jax.ShapeDtypeStruct((T, 2 * CONV_C), CDT), jax.ShapeDtypeStruct((1, 2 * CONV_C), F32),
                   jax.ShapeDtypeStruct((HALO, CONV_C), F32)],
        scratch_shapes=[pltpu.VMEM((tm + HALO, CONV_C), F32), pltpu.VMEM((SUBLANES + tm + HALO, CONV_C), F32),
                        pltpu.VMEM((SUBLANES - 1, tm + HALO, CONV_C), F32), pltpu.VMEM((8 * CONV_K, CONV_C), F32)],
        compiler_params=_cp("arbitrary"),
    )(du1, du1, rest, rest, rest, rest, cw)


def attn_bwd(qkv, do, lse, sinks, *, tq=512, rider=None):
    T = qkv.shape[0]
    nb = tq // BLOCK

    def body(sink_ref, bias_ref, q_ref, kp_ref, kc_ref, vp_ref, vc_ref, do_ref, lse_ref,
             dq_ref, dqsum_ref, dkv_ref, spill_ref, dsink_ref, kext, vext, dkext, dvext, qs, dos):
        i = pl.program_id(0)

        @pl.when(i == 0)
        def _():
            dsink_ref[...] = jnp.zeros_like(dsink_ref)
            dqsum_ref[...] = jnp.zeros_like(dqsum_ref)

        kext[0:BLOCK, :] = kp_ref[...]
        kext[BLOCK:, :] = kc_ref[...]
        vext[0:BLOCK, :] = vp_ref[...]
        vext[BLOCK:, :] = vc_ref[...]
        dkext[...] = jnp.zeros_like(dkext)
        dvext[...] = jnp.zeros_like(dvext)
        lane_l = lax.broadcasted_iota(jnp.int32, (BLOCK, 128), 1)
        lane_k = lax.broadcasted_iota(jnp.int32, (2 * BLOCK, KV_W), 1)

        def blk(b, dsink):
            r0 = pl.multiple_of(b * BLOCK, BLOCK)
            first = jnp.logical_and(i == 0, b == 0).astype(jnp.int32)
            kc = kext[pl.ds(r0, 2 * BLOCK), :]
            vc = vext[pl.ds(r0, 2 * BLOCK), :]
            lse_t = lse_ref[pl.ds(r0, BLOCK), :]
            dk = jnp.zeros((2 * BLOCK, KV_W), F32)
            dv = jnp.zeros((2 * BLOCK, KV_W), F32)
            for g in range(N_KV):
                heads = range(GROUP * g, GROUP * (g + 1))
                _stack_heads(qs.at[g], q_ref, r0, g, SCALE)
                _stack_heads(dos.at[g], do_ref, r0, g)
                qv = qs[g]
                dov = dos[g]
                s = _dot_nt(qv, kc) + bias_ref[first, g]
                lse = jnp.concatenate(
                    [jnp.sum(jnp.where(lane_l == h, lse_t, 0.0), axis=-1, keepdims=True) for h in heads], axis=0)
                p = jnp.exp(s - lse)
                dp = _dot_nt(dov, vc)
                dd = jnp.sum(p * dp, axis=-1, keepdims=True)
                ds = (p * (dp - dd)).astype(CDT)
                keep = (lane_k >= HEAD_DIM) if g else (lane_k < HEAD_DIM)
                for c0, tile in _unstack_heads(dq_ref, _dot(ds, jnp.where(keep, kc, jnp.zeros_like(kc))) * SCALE, r0, g):
                    dqsum_ref[:, c0:c0 + 128] += _colsum(tile)
                dk = dk + _dot_tn(ds, qv)
                dv = dv + _dot_tn(p.astype(CDT), dov)
                wsink = jnp.exp(_per_head_column([sink_ref[h] for h in heads]) - lse) * dd
                for i_h, h in enumerate(heads):
                    part = jnp.sum(wsink[i_h * BLOCK:(i_h + 1) * BLOCK, :], axis=0, keepdims=True)
                    dsink = dsink - jnp.where(lane_l[0:1, :] == h, part, 0.0)
            dkext[pl.ds(r0, 2 * BLOCK), :] += dk
            dvext[pl.ds(r0, 2 * BLOCK), :] += dv
            return dsink

        dsink_ref[...] += lax.fori_loop(0, nb, blk, jnp.zeros((1, 128), F32))
        dkv_ref[:, 0:KV_W] = dkext[BLOCK:, :]
        dkv_ref[:, KV_W:2 * KV_W] = dvext[BLOCK:, :]
        spill_ref[:, 0:KV_W] = dkext[0:BLOCK, :]
        spill_ref[:, KV_W:2 * KV_W] = dvext[0:BLOCK, :]

    row = lambda i: (i, 0)
    return _run(
        body, rider, name="attn_bwd", grid=(T // tq,),
        in_specs=[pl.BlockSpec(memory_space=pltpu.SMEM), _const((2, N_KV, STACK, 2 * BLOCK), 1)] + _qkv_specs(tq)
        + [pl.BlockSpec((tq, ATTN_W), row), pl.BlockSpec((tq, 128), row)],
        out_specs=[pl.BlockSpec((tq, ATTN_W), row), _const((1, ATTN_W)), pl.BlockSpec((tq, 2 * KV_W), row),
                   pl.BlockSpec((BLOCK, 2 * KV_W), row), _const((1, 128))],
        out_shape=[jax.ShapeDtypeStruct((T, ATTN_W), CDT), jax.ShapeDtypeStruct((1, ATTN_W), F32),
                   jax.ShapeDtypeStruct((T, 2 * KV_W), F32),
                   jax.ShapeDtypeStruct((T // tq * BLOCK, 2 * KV_W), F32), jax.ShapeDtypeStruct((1, 128), F32)],
        scratch_shapes=[pltpu.VMEM((tq + BLOCK, KV_W), CDT), pltpu.VMEM((tq + BLOCK, KV_W), CDT),
                        pltpu.VMEM((tq + BLOCK, KV_W), F32), pltpu.VMEM((tq + BLOCK, KV_W), F32),
                        pltpu.VMEM((N_KV, STACK, 2 * HEAD_DIM), CDT), pltpu.VMEM((N_KV, STACK, 2 * HEAD_DIM), CDT)],
        sem=("arbitrary",), args=(sinks, _score_bias(), qkv, qkv, qkv, qkv, qkv, do, lse))


def inproj_bwd(dres, x, g, w, dq, dkv, spill, dglu, dgate, sums, *, tm=512):
    T = x.shape[0]
    nt = T // tm
    pieces = ((0, ATTN_W), (QKV_W, 2 * CONV_C), (QKV_W + 2 * CONV_C, 2 * D))

    def body(dr_ref, x_ref, g_ref, w_ref, dq_ref, dkv_ref, sp_ref, dglu_ref, dgate_ref, sq_ref, sglu_ref, sgate_ref,
             dx_ref, dp_ref, h_ref, dg_ref, db_ref):
        i = pl.program_id(0)

        @pl.when(i == 0)
        def _():
            dg_ref[...] = jnp.zeros_like(dg_ref)
            db_ref[:, ATTN_W:QKV_W] = jnp.zeros((1, QKV_W - ATTN_W), F32)
            for (c0, wd), s_ref in zip(pieces, (sq_ref, sglu_ref, sgate_ref)):
                db_ref[:, c0:c0 + wd] = s_ref[...]

        sp = sp_ref[...]
        sp = jnp.where(i == nt - 1, jnp.zeros_like(sp), sp)
        dkv = dkv_ref[...]
        db_ref[:, ATTN_W:QKV_W] += _colsum(dkv) + _colsum(sp)
        dp_ref[0:tm - BLOCK, ATTN_W:QKV_W] = dkv[0:tm - BLOCK, :].astype(CDT)
        dp_ref[tm - BLOCK:tm, ATTN_W:QKV_W] = (dkv[tm - BLOCK:tm, :] + sp).astype(CDT)
        for (c0, wd), ref in zip(pieces, (dq_ref, dglu_ref, dgate_ref)):
            dp_ref[:, c0:c0 + wd] = ref[...]
        dh = _dot(dp_ref[...], w_ref[...])
        xv = x_ref[...]
        gv = g_ref[...]
        dxn, dg_rows = _rms_bwd(xv, gv, dh)
        dx_ref[...] = dr_ref[...] + dxn
        dg_ref[...] += _colsum(dg_rows)
        r = lax.rsqrt(jnp.mean(xv * xv, axis=-1, keepdims=True) + EPS)
        h_ref[...] = (xv * r * gv).astype(CDT)

    row = lambda i: (i, 0)
    return pl.pallas_call(
        body, name="inproj_bwd", grid=(nt,),
        in_specs=[pl.BlockSpec((tm, D), row), pl.BlockSpec((tm, D), row), _const((1, D)), _const((IN_W, D), 1),
                  pl.BlockSpec((tm, ATTN_W), row), pl.BlockSpec((tm, 2 * KV_W), row),
                  pl.BlockSpec((BLOCK, 2 * KV_W), lambda i: (jnp.minimum(i + 1, nt - 1), 0)),
                  pl.BlockSpec((tm, 2 * CONV_C), row), pl.BlockSpec((tm, 2 * D), row)]
        + [_const((1, wd)) for _, wd in pieces],
        out_specs=[pl.BlockSpec((tm, D), row), pl.BlockSpec((tm, IN_W), row), pl.BlockSpec((tm, D), row),
                   _const((1, D)), _const((1, IN_W))],
        out_shape=[jax.ShapeDtypeStruct((T, D), F32), jax.ShapeDtypeStruct((T, IN_W), CDT),
                   jax.ShapeDtypeStruct((T, D), CDT), jax.ShapeDtypeStruct((1, D), F32),
                   jax.ShapeDtypeStruct((1, IN_W), F32)],
        compiler_params=_cp("arbitrary"),
    )(dres, x, g, w, dq, dkv, spill, dglu, dgate, *sums)


ATTN_TILE = 256
MATRICES = ("w_in", "w_attn_proj", "w_conv_proj", "w_out", "w_mlp1", "w_mlp2")
SMALL = ("mix_norm_g", "b_in", "sinks", "conv_w", "conv_b", "conv_ln_g", "conv_ln_b", "b_conv_proj", "mlp_norm_g")


def forward_backward(x, tgt, hooks):
    def call(fn, kernel, l, *args, **kw):
        rider = hooks.rider(kernel, l)
        if rider is None:
            return fn(*args, **kw)
        outs, landed = fn(*args, rider=rider, **kw)
        hooks.landed(kernel, l, landed)
        return outs

    vec = hooks.vec
    saved = []
    for l in range(DEPTH):
        qkv, rest = call(rms_inproj, "rms_inproj", l, x, vec("mix_norm_g", l), hooks.w_in(l), vec("b_in", l))
        m = hooks.mats(l)
        attn, lse = attn_fwd(qkv, hooks.sinks(l), tq=ATTN_TILE)
        u3, u1 = conv_fwd(rest, hooks.taps(l), vec("conv_b", l), vec("conv_ln_g", l), vec("conv_ln_b", l))
        x1 = merge_out(x, attn, u3, rest, m["w_attn_proj"], m["w_conv_proj"], vec("b_conv_proj", l), m["w_out"])
        if l < DEPTH - 1:
            x_next, pre = call(mlp_fwd, "mlp_fwd", l, x1, vec("mlp_norm_g", l), m["w_mlp1"], m["w_mlp2"])
        else:
            x_next = None
            pre, dx, dgf, loss = call(mlp_fwd, "mlp_fwd", l, x1, vec("mlp_norm_g", l), m["w_mlp1"], m["w_mlp2"],
                                      head=(hooks.final_g, tgt))
        saved.append((x, qkv, rest, attn, lse, u3, u1, x1, pre))
        x = x_next
    small = {n: [None] * DEPTH for n in SMALL}
    small["final_norm_g"] = dgf
    for l in reversed(range(DEPTH)):
        x0, qkv, rest, attn, lse, u3, u1, x1, pre = saved[l]
        m = hooks.mats(l)
        dx1, dg2, h2, a, dpre = call(mlp_bwd, "mlp_bwd", l, dx, x1, vec("mlp_norm_g", l), pre, m["w_mlp1"], m["w_mlp2"])
        small["mlp_norm_g"][l] = dg2
        group = {}
        group["w_mlp1"] = call(tn_matmul, "tn_mlp1", l, h2, dpre, tm=1024, tn=1024, tk=4096, name="tn_mlp1", by_chip=True)
        group["w_mlp2"] = call(tn_matmul, "tn_mlp2", l, a, dx, tm=1024, tn=1024, name="tn_mlp2")
        merged, dba, dbc, dattn, du3, dgate, dgate_sum, dbcp = merge_bwd(
            dx1, attn, u3, rest, m["w_attn_proj"], m["w_conv_proj"], vec("b_conv_proj", l), m["w_out"])
        small["b_conv_proj"][l] = dbcp
        group["w_out"] = tn_matmul(merged, dx1, tm=1024, tn=1024, name="tn_out")
        group["w_attn_proj"] = tn_matmul(attn, dba, tm=512, tn=256, tk=4096, name="tn_attn_proj", by_chip=True)
        group["w_conv_proj"] = tn_matmul(u3, dbc, tm=512, tn=256, tk=4096, name="tn_conv_proj", by_chip=True)
        hooks.grads(l, "A", group)
        du1, dlg, dlb, dcb = call(conv_bwd_ln, "conv_bwd_ln", l, du3, u1, vec("conv_ln_g", l), vec("conv_ln_b", l))
        small["conv_ln_g"][l], small["conv_ln_b"][l], small["conv_b"][l] = dlg, dlb, dcb
        dglu, dglu_sum, dcw = conv_bwd_taps(du1, rest, hooks.taps(l))
        small["conv_w"][l] = dcw
        dq, dq_sum, dkv, spill, dsink = call(attn_bwd, "attn_bwd", l, qkv, dattn, lse, hooks.sinks(l), tq=ATTN_TILE)
        small["sinks"][l] = dsink
        dx, dproj, h, dg, db = inproj_bwd(dx1, x0, vec("mix_norm_g", l), hooks.w_in(l), dq, dkv, spill, dglu, dgate,
                                          (dq_sum, dglu_sum, dgate_sum), tm=ATTN_TILE)
        small["mix_norm_g"][l], small["b_in"][l] = dg, db
        hooks.grads(l, "B", {"w_in": call(tn_matmul, "tn_in", l, dproj, h, tm=768, tn=1024, tk=4096, name="tn_in")})
    return loss, dx, small


class _LocalHooks:
    def __init__(self, p):
        self.p = p
        self.final_g = p["final_norm_g"]
        self.got = {n: [None] * DEPTH for n in MATRICES}

    def w_in(self, l):
        return self.p["w_in"][l].T

    def mats(self, l):
        return {n: self.p[n][l] for n in MATRICES}

    def vec(self, n, l):
        return self.p[n][l]

    def sinks(self, l):
        return self.p["sinks"][l]

    def taps(self, l):
        return self.p["conv_w"][l]

    def rider(self, kernel, l):
        return None

    def grads(self, l, group, g):
        for n, v in g.items():
            if v.ndim == 3:
                v = v.transpose(1, 0, 2).reshape(v.shape[1], -1)
            self.got[n][l] = v.T if n == "w_in" else v


def local_grads(x, tgt, p):
    hooks = _LocalHooks(p)
    loss, dx, small = forward_backward(x, tgt, hooks)
    small["conv_w"] = [g[0:CONV_K] for g in small["conv_w"]]
    small["sinks"] = [g[0, 0:N_Q] for g in small["sinks"]]
    return loss, dx, {**small, **hooks.got}


MESH = pl.DeviceIdType.MESH
N_CHIPS = 4
N_DEV = 8
FLAT_W = 1024
FLAT_PARTS = (("w_in", 960), ("w_attn_proj", 128), ("w_conv_proj", 128), ("w_out", 256), ("w_mlp1", 1024), ("w_mlp2", 1024))
FLAT_ROWS = sum(r for _, r in FLAT_PARTS)
W_IN_ROWS = FLAT_PARTS[0][1]
GROUP_A = (("w_mlp1", 1024), ("w_mlp2", 1024), ("w_out", 256), ("w_attn_proj", 128), ("w_conv_proj", 128))
COL_SHARDED = ("w_in", "w_attn_proj", "w_conv_proj", "w_mlp1")
FULL_SHAPES = {"w_in": (D, IN_W), "w_attn_proj": (ATTN_W, D), "w_conv_proj": (CONV_C, D), "w_out": (D, D),
               "w_mlp1": (D, D_FF), "w_mlp2": (D_FF, D)}


def _place():
    x, y, c = lax.axis_index("x"), lax.axis_index("y"), lax.axis_index("c")
    return x, y, c, 2 * x + y


def _peer_chips(x, y, j):
    return [((x, 1 - y), j ^ 1), ((1 - x, y), j ^ 2), ((1 - x, 1 - y), j ^ 3)]


def _remote(src, dst, sems, k, n, to):
    return pltpu.make_async_remote_copy(src_ref=src, dst_ref=dst, send_sem=sems.at[k], recv_sem=sems.at[n + k],
                                        device_id=to, device_id_type=MESH)


def _half(c, rows):
    h = rows // 2
    return pl.ds(pl.multiple_of(c * h, 16), h)


def gather_rider(wsh):
    R = wsh.shape[0]
    n = 7

    def plan(rins, routs, sems):
        (w_ref,), (out_ref,) = rins, routs
        x, y, c, j = _place()
        peers = _peer_chips(x, y, j)
        mine, other = _half(c, R), _half(1 - c, R)
        sent = [_remote(w_ref.at[mine], out_ref.at[j, mine], sems, k, n, (*chip, c)) for k, (chip, _) in enumerate(peers)]
        sent.append(_remote(w_ref, out_ref.at[j], sems, 6, n, (x, y, 1 - c)))
        landed = [_remote(w_ref.at[mine], out_ref.at[pj, mine], sems, k, n, (x, y, c)) for k, (_, pj) in enumerate(peers)]
        passed = [_remote(out_ref.at[pj, mine], out_ref.at[pj, mine], sems, 3 + k, n, (x, y, 1 - c))
                  for k, (_, pj) in enumerate(peers)]
        handed = [_remote(w_ref.at[mine], out_ref.at[pj, other], sems, 3 + k, n, (x, y, c)) for k, (_, pj) in enumerate(peers)]
        handed.append(_remote(w_ref, out_ref.at[j], sems, 6, n, (x, y, c)))
        return sent, landed, passed, handed

    def start(rins, routs, sems):
        for cp in plan(rins, routs, sems)[0]:
            cp.start()

    def finish(rins, routs, sems):
        sent, landed, passed, handed = plan(rins, routs, sems)
        for k in range(3):
            landed[k].wait_recv()
            passed[k].start()
        for cp in handed:
            cp.wait_recv()
        for cp in sent + passed:
            cp.wait_send()

    return Rider((wsh,), (jax.ShapeDtypeStruct((N_CHIPS,) + wsh.shape, wsh.dtype),), 2 * n, start, finish)


def swap_rider(g):
    R = g.shape[1]

    def plan(rins, routs, sems):
        (g_ref,), (got_ref,) = rins, routs
        x, y, c, _ = _place()
        return _remote(g_ref.at[:, _half(1 - c, R), :], got_ref, sems, 0, 1, (x, y, 1 - c))

    def start(rins, routs, sems):
        plan(rins, routs, sems).start()

    def finish(rins, routs, sems):
        plan(rins, routs, sems).wait()

    return Rider((g,), (jax.ShapeDtypeStruct((N_CHIPS, R // 2, FLAT_W), g.dtype),), 2, start, finish)


def exchange_rider(pb):
    def plan(rins, routs, sems):
        (pb_ref,), (got_ref,) = rins, routs
        x, y, c, j = _place()
        peers = _peer_chips(x, y, j)
        sent = [_remote(pb_ref.at[pj], got_ref.at[j], sems, k, 3, (*chip, c)) for k, (chip, pj) in enumerate(peers)]
        landed = [_remote(pb_ref.at[pj], got_ref.at[pj], sems, k, 3, (x, y, c)) for k, (_, pj) in enumerate(peers)]
        return sent, landed

    def start(rins, routs, sems):
        for cp in plan(rins, routs, sems)[0]:
            cp.start()

    def finish(rins, routs, sems):
        sent, landed = plan(rins, routs, sems)
        for cp in landed:
            cp.wait_recv()
        for cp in sent:
            cp.wait_send()

    return Rider((pb,), (jax.ShapeDtypeStruct(pb.shape, pb.dtype),), 6, start, finish)


def share_rider(tot):
    def plan(rins, routs, sems):
        (t_ref,), (got_ref,) = rins, routs
        x, y, c, _ = _place()
        return _remote(t_ref, got_ref, sems, 0, 1, (x, y, 1 - c))

    def start(rins, routs, sems):
        plan(rins, routs, sems).start()

    def finish(rins, routs, sems):
        plan(rins, routs, sems).wait()

    return Rider((tot,), (jax.ShapeDtypeStruct(tot.shape, tot.dtype),), 2, start, finish)


def pair_sum(g, got):
    nj, R, W = g.shape
    h = R // 2
    tile = h // 2

    def body(g_ref, got_ref, pb_ref, own_ref):
        v = g_ref[...] + got_ref[...]
        pb_ref[...] = v.astype(pb_ref.dtype)

        @pl.when(pl.program_id(1) == _place()[3])
        def _():
            own_ref[...] = v

    return pl.pallas_call(
        body, name="pair_sum", grid=(h // tile, nj),
        in_specs=[pl.BlockSpec((None, tile, W), lambda r, j: (j, lax.axis_index("c") * (h // tile) + r, 0)),
                  pl.BlockSpec((None, tile, W), lambda r, j: (j, r, 0))],
        out_specs=[pl.BlockSpec((None, tile, W), lambda r, j: (j, r, 0)), pl.BlockSpec((tile, W), lambda r, j: (r, 0))],
        out_shape=[jax.ShapeDtypeStruct((nj, h, W), CDT), jax.ShapeDtypeStruct((h, W), F32)],
        compiler_params=_cp("arbitrary", "arbitrary"),
    )(g, got)


def total_sum(own, got):
    R, W = own.shape
    tile = R // 2

    def body(own_ref, a_ref, b_ref, c_ref, o_ref):
        o_ref[...] = ((own_ref[...] + a_ref[...].astype(F32)) + b_ref[...].astype(F32)) + c_ref[...].astype(F32)

    def slab(k):
        return pl.BlockSpec((None, tile, W), lambda r: (_place()[3] ^ (k + 1), r, 0))

    return pl.pallas_call(
        body, name="total_sum", grid=(R // tile,),
        in_specs=[pl.BlockSpec((tile, W), lambda r: (r, 0)), slab(0), slab(1), slab(2)],
        out_specs=pl.BlockSpec((tile, W), lambda r: (r, 0)),
        out_shape=jax.ShapeDtypeStruct((R, W), F32),
        compiler_params=_cp("arbitrary"),
    )(own, got, got, got)


def _all_peers(x, y, c):
    return [(x ^ (r >> 2), y ^ ((r >> 1) & 1), c ^ (r & 1)) for r in range(1, N_DEV)]


ROW_ITEMS = (("mix_norm_g", D), ("b_in", IN_W), ("sinks", N_Q), ("conv_b", CONV_C), ("conv_ln_g", CONV_C),
             ("conv_ln_b", CONV_C), ("b_conv_proj", D), ("mlp_norm_g", D))
TAPS_ROW = 16
TAPS_ROWS = 32
LAYER_ROWS = TAPS_ROW + TAPS_ROWS
FINAL_ROW = DEPTH * LAYER_ROWS
SMALL_ROWS = FINAL_ROW + SUBLANES


def _row_chunks():
    out, r = {}, 0
    for n, width in ROW_ITEMS:
        out[n] = [(r + i, FLAT_W * i, min(FLAT_W, width - FLAT_W * i)) for i in range(-(-width // FLAT_W))]
        r += len(out[n])
    assert r <= TAPS_ROW
    return out


def sum_small(gsm):
    chunks = _row_chunks()
    ins = []
    for l in range(DEPTH):
        ins += [gsm[n][l] for n, _ in ROW_ITEMS] + [gsm["conv_w"][l]]
    ins.append(gsm["final_norm_g"])
    n_in = len(ins)

    def body(*refs):
        in_refs, o_ref, buf, send_sems, recv_sems = refs[:n_in], refs[n_in], refs[n_in + 1], refs[n_in + 2], refs[n_in + 3]
        x, y, c, _ = _place()
        me = 4 * x + 2 * y + c
        mine = buf.at[me]
        mine[...] = jnp.zeros((SMALL_ROWS, FLAT_W), F32)
        k = 0
        for l in range(DEPTH):
            for n, _ in ROW_ITEMS:
                for r, c0, wd in chunks[n]:
                    mine[l * LAYER_ROWS + r:l * LAYER_ROWS + r + 1, 0:wd] = in_refs[k][:, c0:c0 + wd]
                k += 1
            mine[l * LAYER_ROWS + TAPS_ROW:(l + 1) * LAYER_ROWS, 0:CONV_C] = in_refs[k][...]
            k += 1
        mine[FINAL_ROW:FINAL_ROW + 1, :] = in_refs[k][...]
        peers = _all_peers(x, y, c)
        sends = [pltpu.make_async_remote_copy(src_ref=mine, dst_ref=mine, send_sem=send_sems.at[r], recv_sem=recv_sems.at[r],
                                              device_id=to, device_id_type=MESH) for r, to in enumerate(peers)]
        for cp in sends:
            cp.start()
        for r in range(N_DEV - 1):
            pltpu.make_async_remote_copy(src_ref=mine, dst_ref=buf.at[me ^ (r + 1)], send_sem=send_sems.at[r],
                                         recv_sem=recv_sems.at[r], device_id=(x, y, c), device_id_type=MESH).wait_recv()
        for cp in sends:
            cp.wait_send()
        acc = buf[0]
        for d in range(1, N_DEV):
            acc = acc + buf[d]
        o_ref[...] = acc

    vm = pl.BlockSpec(memory_space=pltpu.VMEM)
    return pl.pallas_call(
        body, name="sum_small", out_shape=jax.ShapeDtypeStruct((SMALL_ROWS, FLAT_W), F32),
        in_specs=[vm] * n_in, out_specs=vm,
        scratch_shapes=[pltpu.VMEM((N_DEV, SMALL_ROWS, FLAT_W), F32), pltpu.SemaphoreType.DMA((N_DEV - 1,)),
                        pltpu.SemaphoreType.DMA((N_DEV - 1,))],
    )(*ins)


def gather_taps(taps):
    shard = taps.shape[2]

    def body(t_ref, o_ref, buf, send_sems, recv_sems):
        x, y, c, j = _place()
        peers = _peer_chips(x, y, j)
        buf[j] = t_ref[...]
        sends = [pltpu.make_async_remote_copy(src_ref=t_ref, dst_ref=buf.at[j], send_sem=send_sems.at[k],
                                              recv_sem=recv_sems.at[k], device_id=(*chip, c), device_id_type=MESH)
                 for k, (chip, _) in enumerate(peers)]
        for cp in sends:
            cp.start()
        for k, (_, pj) in enumerate(peers):
            pltpu.make_async_remote_copy(src_ref=t_ref, dst_ref=buf.at[pj], send_sem=send_sems.at[k],
                                         recv_sem=recv_sems.at[k], device_id=(x, y, c), device_id_type=MESH).wait_recv()
        for cp in sends:
            cp.wait_send()
        for jj in range(N_CHIPS):
            o_ref[:, :, jj * shard:(jj + 1) * shard] = buf[jj]

    vm = pl.BlockSpec(memory_space=pltpu.VMEM)
    return pl.pallas_call(
        body, name="gather_taps", out_shape=jax.ShapeDtypeStruct(taps.shape[:2] + (N_CHIPS * shard,), taps.dtype),
        in_specs=[vm], out_specs=vm,
        scratch_shapes=[pltpu.VMEM((N_CHIPS,) + taps.shape, taps.dtype), pltpu.SemaphoreType.DMA((3,)),
                        pltpu.SemaphoreType.DMA((3,))],
    )(taps)


def _adam_math(w, g, m, v):
    nm = ADAM_B1 * m + (1.0 - ADAM_B1) * g
    nv = ADAM_B2 * v + (1.0 - ADAM_B2) * jnp.square(g)
    m_hat = nm / (1.0 - ADAM_B1 ** ADAM_STEP)
    v_hat = nv / (1.0 - ADAM_B2 ** ADAM_STEP)
    return -ADAM_LR * (m_hat / (jnp.sqrt(v_hat) + ADAM_EPS) + ADAM_WD * w), nm, nv


def adamw(w, g, m, v, *, name):
    L, R, C = w.shape
    tr = next(t for t in (512, 480, 256, 128) if R % t == 0)

    def body(w_ref, g_ref, m_ref, v_ref, d_ref, nm_ref, nv_ref):
        d_ref[...], nm_ref[...], nv_ref[...] = _adam_math(w_ref[...], g_ref[...], m_ref[...], v_ref[...])

    spec = pl.BlockSpec((None, tr, C), lambda l, i: (l, i, 0))
    out = jax.ShapeDtypeStruct((L, R, C), F32)
    return pl.pallas_call(
        body, name=name, grid=(L, R // tr), in_specs=[spec] * 4, out_specs=[spec] * 3, out_shape=[out] * 3,
        compiler_params=_cp("parallel", "parallel"),
    )(w, g, m, v)


def adamw_small(packed, w, m, v):
    chunks = _row_chunks()
    names = SMALL + ("final_norm_g",)
    as_2d = lambda a: a.reshape(1, -1) if a.ndim == 1 else a
    ins = [as_2d(t[n]) for n in names for t in (w, m, v)]
    shapes = [jax.ShapeDtypeStruct(as_2d(w[n]).shape, F32) for n in names for _ in range(4)]
    n_in = len(ins)

    def body(p_ref, *refs):
        in_refs, out_refs = refs[:n_in], refs[n_in:]
        chip = _place()[3]
        for i, n in enumerate(names):
            w_ref, m_ref, v_ref = in_refs[3 * i:3 * i + 3]
            outs = out_refs[4 * i:4 * i + 4]

            def step(at, g):
                res = (g,) + _adam_math(w_ref[at], g, m_ref[at], v_ref[at])
                for o_ref, val in zip(outs, res):
                    o_ref[at] = val

            if n == "final_norm_g":
                step((slice(None), slice(None)), p_ref[FINAL_ROW:FINAL_ROW + 1, :])
                continue
            for l in range(DEPTH):
                if n == "conv_w":
                    r0 = l * LAYER_ROWS + TAPS_ROW
                    shard = CONV_C // N_CHIPS
                    g = jnp.zeros((CONV_K, shard), F32)
                    for j in range(N_CHIPS):
                        g = jnp.where(chip == j, p_ref[r0:r0 + CONV_K, j * shard:(j + 1) * shard], g)
                    step((l,), g)
                else:
                    for r, c0, wd in chunks[n]:
                        step((slice(l, l + 1), slice(c0, c0 + wd)),
                             p_ref[l * LAYER_ROWS + r:l * LAYER_ROWS + r + 1, 0:wd])

    vm = pl.BlockSpec(memory_space=pltpu.VMEM)
    res = pl.pallas_call(
        body, name="adamw_small", out_shape=shapes,
        in_specs=[vm] + [vm] * n_in, out_specs=[vm] * len(shapes),
    )(packed, *ins)
    dicts = ({}, {}, {}, {})
    for i, n in enumerate(names):
        for d, val in zip(dicts, res[4 * i:4 * i + 4]):
            d[n] = val.reshape(w[n].shape)
    return dicts


def _flat_rows(name, shard):
    return shard.T if name == "w_in" else shard.reshape(-1, FLAT_W)


def _full_matrix(slabs, name):
    K, N = FULL_SHAPES[name]
    if name == "w_in":
        return slabs.reshape(N, K)
    if name in COL_SHARDED:
        return slabs.reshape(N_CHIPS, K, N // N_CHIPS).transpose(1, 0, 2).reshape(K, N)
    return slabs.reshape(K, N)


def _first_row(parts, name):
    r = 0
    for n, rows in parts:
        if n == name:
            return r, rows
        r += rows
    raise KeyError(name)


class _Exchange:
    CARRIERS = {
        ("conv_bwd_ln", 1): ((1, "A"), "swap"), ("attn_bwd", 1): ((1, "A"), "exchange"), ("tn_in", 1): ((1, "A"), "share"),
        ("mlp_bwd", 0): ((1, "B"), "swap"), ("tn_mlp1", 0): ((1, "B"), "exchange"), ("tn_mlp2", 0): ((1, "B"), "share"),
        ("conv_bwd_ln", 0): ((0, "A"), "swap"), ("attn_bwd", 0): ((0, "A"), "exchange"), ("tn_in", 0): ((0, "A"), "share"),
    }

    def __init__(self, w, ci, chip):
        self.w, self.ci, self.chip = w, ci, chip
        self.wsh = [jnp.concatenate([_flat_rows(n, w[n][l]) for n, _ in FLAT_PARTS], axis=0).astype(CDT)
                    for l in range(DEPTH)]
        self.final_g = w["final_norm_g"].reshape(1, D)
        self.slabs = {}
        self.full = {}
        self.units = {}
        self.reduced = {}
        self._landed_weights(0, 0, _run_alone(gather_rider(self.wsh[0][:W_IN_ROWS]), "gather_w_in")[0])
        self.all_taps = gather_taps(w["conv_w"])

    def _landed_weights(self, l, r0, buf):
        self.slabs.setdefault(l, []).append((r0, buf))

    def _matrix(self, l, name):
        if (l, name) not in self.full:
            r, rows = _first_row(FLAT_PARTS, name)
            r0, buf = next((r0, buf) for r0, buf in self.slabs[l] if r0 <= r < r0 + buf.shape[1])
            self.full[(l, name)] = _full_matrix(buf[:, r - r0:r - r0 + rows], name)
        return self.full[(l, name)]

    def w_in(self, l):
        return self._matrix(l, "w_in")

    def mats(self, l):
        return {n: self._matrix(l, n) for n in MATRICES if n != "w_in"}

    def vec(self, n, l):
        return self.w[n][l].reshape(1, -1)

    def sinks(self, l):
        return self.w["sinks"][l]

    def taps(self, l):
        return self.all_taps[l]

    def rider(self, kernel, l):
        if (kernel, l) == ("rms_inproj", 0):
            return gather_rider(self.wsh[0][W_IN_ROWS:])
        if (kernel, l) == ("mlp_fwd", 0):
            return gather_rider(self.wsh[1])
        if (kernel, l) in self.CARRIERS:
            return self._stage(*self.CARRIERS[(kernel, l)])
        return None

    def landed(self, kernel, l, bufs):
        if (kernel, l) == ("rms_inproj", 0):
            self._landed_weights(0, W_IN_ROWS, bufs[0])
        elif (kernel, l) == ("mlp_fwd", 0):
            self._landed_weights(1, 0, bufs[0])
        else:
            self._stage_landed(*self.CARRIERS[(kernel, l)], bufs[0])

    def grads(self, l, group, g):
        if group == "A":
            flat = jnp.concatenate([g[n].reshape(N_CHIPS, rows, FLAT_W) for n, rows in GROUP_A], axis=1)
        else:
            flat = g["w_in"].reshape(N_CHIPS, W_IN_ROWS, FLAT_W)
        self.units[(l, group)] = {"g": flat}

    def _stage(self, key, stage):
        u = self.units[key]
        if stage == "swap":
            return swap_rider(u["g"])
        if stage == "exchange":
            u["pb"], u["own"] = pair_sum(u["g"], u["swap"])
            return exchange_rider(u["pb"])
        u["tot"] = total_sum(u["own"], u["exchange"])
        return share_rider(u["tot"])

    def _stage_landed(self, key, stage, buf):
        u = self.units[key]
        u[stage] = buf
        if stage == "share":
            tot = u["tot"]
            self.reduced[key] = jnp.where(self.ci == 0, jnp.concatenate([tot, buf]), jnp.concatenate([buf, tot]))

    def finish(self):
        key = (0, "B")
        for stage in ("swap", "exchange", "share"):
            self._stage_landed(key, stage, _run_alone(self._stage(key, stage), stage + "_last")[0])
        out = {}
        for n in MATRICES:
            per_layer = []
            for l in range(DEPTH):
                if n == "w_in":
                    per_layer.append(self.reduced[(l, "B")].T)
                    continue
                r, rows = _first_row(GROUP_A, n)
                per_layer.append(self.reduced[(l, "A")][r:r + rows].reshape(self.w[n].shape[1:]))
            out[n] = jnp.stack(per_layer)
        return out


WEIGHTS = ("mix_norm_g", "w_in", "b_in", "sinks", "conv_w", "conv_b", "conv_ln_g", "conv_ln_b", "w_attn_proj",
           "w_conv_proj", "b_conv_proj", "w_out", "mlp_norm_g", "w_mlp1", "w_mlp2", "final_norm_g")


def kernel(x, mix_norm_g, w_in, b_in, sinks, conv_w, conv_b, conv_ln_g, conv_ln_b, w_attn_proj, w_conv_proj, b_conv_proj, w_out, mlp_norm_g, w_mlp1, w_mlp2, final_norm_g, loss_target, m_mix_norm_g, m_w_in, m_b_in, m_sinks, m_conv_w, m_conv_b, m_conv_ln_g, m_conv_ln_b, m_w_attn_proj, m_w_conv_proj, m_b_conv_proj, m_w_out, m_mlp_norm_g, m_w_mlp1, m_w_mlp2, m_final_norm_g, v_mix_norm_g, v_w_in, v_b_in, v_sinks, v_conv_w, v_conv_b, v_conv_ln_g, v_conv_ln_b, v_w_attn_proj, v_w_conv_proj, v_b_conv_proj, v_w_out, v_mlp_norm_g, v_w_mlp1, v_w_mlp2, v_final_norm_g):
    w = dict(zip(WEIGHTS, (mix_norm_g, w_in, b_in, sinks, conv_w, conv_b, conv_ln_g, conv_ln_b, w_attn_proj, w_conv_proj,
                           b_conv_proj, w_out, mlp_norm_g, w_mlp1, w_mlp2, final_norm_g)))
    m = dict(zip(WEIGHTS, (m_mix_norm_g, m_w_in, m_b_in, m_sinks, m_conv_w, m_conv_b, m_conv_ln_g, m_conv_ln_b, m_w_attn_proj,
                           m_w_conv_proj, m_b_conv_proj, m_w_out, m_mlp_norm_g, m_w_mlp1, m_w_mlp2, m_final_norm_g)))
    v = dict(zip(WEIGHTS, (v_mix_norm_g, v_w_in, v_b_in, v_sinks, v_conv_w, v_conv_b, v_conv_ln_g, v_conv_ln_b, v_w_attn_proj,
                           v_w_conv_proj, v_b_conv_proj, v_w_out, v_mlp_norm_g, v_w_mlp1, v_w_mlp2, v_final_norm_g)))
    xi, yi, ci = lax.axis_index("x"), lax.axis_index("y"), lax.axis_index("c")
    chip = 2 * xi + yi

    hooks = _Exchange(w, ci, chip)
    loss, dx, gsm = forward_backward(x[0], loss_target[0], hooks)
    loss = lax.psum(loss[0, 0], ("x", "y", "c"))
    grads = hooks.finish()

    gsmall, delta, new_m, new_v = adamw_small(sum_small(gsm), w, m, v)
    grads.update(gsmall)
    for n in MATRICES:
        t = (lambda a: jnp.swapaxes(a, 1, 2)) if n == "w_in" else (lambda a: a)
        delta[n], new_m[n], new_v[n] = map(t, adamw(t(w[n]), t(grads[n]), t(m[n]), t(v[n]), name="adamw_" + n))

    return (loss, dx[None], *[grads[n] for n in WEIGHTS], *[delta[n] for n in WEIGHTS],
            *[new_m[n] for n in WEIGHTS], *[new_v[n] for n in WEIGHTS])
```

```python
import functools
import math
from typing import Callable, NamedTuple, Optional

import jax
import jax.numpy as jnp
import numpy as np
from jax import lax
from jax.experimental import pallas as pl
from jax.experimental.pallas import tpu as pltpu

F32 = jnp.float32
CDT = jnp.bfloat16

D = 1024
DEPTH = 2
N_Q = 8
HEAD_DIM = 64
ATTN_W = 512
KV_W = 128
BLOCK = 128
CONV_C = 512
CONV_K = 31
D_FF = 4096
IN_W = 3840
QKV_W = ATTN_W + 2 * KV_W
REST_W = IN_W - QKV_W
EPS = 1e-6
NEG = -1e30
SCALE = 1.0 / math.sqrt(HEAD_DIM)
SLOPES = [float(2.0 ** (-8.0 * (h + 1) / N_Q)) for h in range(N_Q)]
SUBLANES = 8
HALO = 32

ADAM_LR = 0.001
ADAM_B1 = 0.9
ADAM_B2 = 0.999
ADAM_EPS = 1e-08
ADAM_WD = 0.01
ADAM_STEP = 10

VMEM_LIMIT = 56 * 1024 * 1024


def _cp(*sem):
    return pltpu.CompilerParams(dimension_semantics=sem, vmem_limit_bytes=VMEM_LIMIT)


def _dot(a, b):
    return jnp.dot(a, b, preferred_element_type=F32)


def _dot_nt(a, b):
    return lax.dot_general(a, b, (((1,), (1,)), ((), ())), preferred_element_type=F32)


def _dot_tn(a, b):
    return lax.dot_general(a, b, (((0,), (0,)), ((), ())), preferred_element_type=F32)


def _sig(x):
    return 1.0 / (1.0 + jnp.exp(-x))


def _colsum(v):
    return jnp.sum(v, axis=0, keepdims=True)


def _const(shape, buffers=None):
    mode = {} if buffers is None else {"pipeline_mode": pl.Buffered(buffers)}
    return pl.BlockSpec(shape, lambda *_: (0,) * len(shape), **mode)


class Rider(NamedTuple):
    ins: tuple
    outs: tuple
    n_sems: int
    start: Callable
    finish: Callable
    late: Optional[Callable] = None


def _any():
    return pl.BlockSpec(memory_space=pl.ANY)


def _run(body, rider, *, name, grid, in_specs, out_specs, out_shape, args, sem, scratch_shapes=()):
    if rider is None:
        return pl.pallas_call(body, name=name, grid=grid, in_specs=list(in_specs), out_specs=list(out_specs),
                              out_shape=list(out_shape), scratch_shapes=list(scratch_shapes),
                              compiler_params=_cp(*sem))(*args)
    n_in, n_out, n_sc = len(in_specs), len(out_specs), len(scratch_shapes)
    r_in, r_out = len(rider.ins), len(rider.outs)

    def riding(*refs):
        ins, rins = refs[:n_in], refs[n_in:n_in + r_in]
        o0 = n_in + r_in
        outs, routs = refs[o0:o0 + n_out], refs[o0 + n_out:o0 + n_out + r_out]
        s0 = o0 + n_out + r_out
        scratch, sems = refs[s0:s0 + n_sc], refs[s0 + n_sc]
        first = functools.reduce(jnp.logical_and, [pl.program_id(a) == 0 for a in range(len(grid))])
        last = functools.reduce(jnp.logical_and, [pl.program_id(a) == grid[a] - 1 for a in range(len(grid))])

        @pl.when(first)
        def _():
            rider.start(rins, routs, sems)

        if rider.late is not None:
            late_step = grid[0] - max(1, grid[0] // 8)
            others = [pl.program_id(a) == 0 for a in range(1, len(grid))]

            @pl.when(functools.reduce(jnp.logical_and, others, pl.program_id(0) == late_step))
            def _():
                rider.late(rins, routs, sems)

        body(*ins, *outs, *scratch)

        @pl.when(last)
        def _():
            rider.finish(rins, routs, sems)

    res = pl.pallas_call(
        riding, name=name, grid=grid, in_specs=list(in_specs) + [_any()] * r_in,
        out_specs=list(out_specs) + [_any()] * r_out, out_shape=list(out_shape) + list(rider.outs),
        scratch_shapes=list(scratch_shapes) + [pltpu.SemaphoreType.DMA((rider.n_sems,))],
        compiler_params=_cp(*["arbitrary"] * len(grid)))(*args, *rider.ins)
    return res[:n_out], res[n_out:]


def _run_alone(rider, name):
    def body(*refs):
        r_in, r_out = len(rider.ins), len(rider.outs)
        rins, routs, sems = refs[:r_in], refs[r_in:r_in + r_out], refs[r_in + r_out]
        rider.start(rins, routs, sems)
        if rider.late is not None:
            rider.late(rins, routs, sems)
        rider.finish(rins, routs, sems)

    return pl.pallas_call(
        body, name=name, in_specs=[_any()] * len(rider.ins), out_specs=[_any()] * len(rider.outs),
        out_shape=list(rider.outs), scratch_shapes=[pltpu.SemaphoreType.DMA((rider.n_sems,))])(*rider.ins)


def rms_inproj(x, g, wt, b, *, tm=512, rider=None):
    T = x.shape[0]

    def body(x_ref, g_ref, w_ref, b_ref, qkv_ref, rest_ref):
        xv = x_ref[...]
        r = lax.rsqrt(jnp.mean(xv * xv, axis=-1, keepdims=True) + EPS)
        h = (xv * r * g_ref[...]).astype(CDT)
        qkv_ref[...] = (_dot_nt(h, w_ref[0:QKV_W, :]) + b_ref[:, 0:QKV_W]).astype(qkv_ref.dtype)
        for j in range(REST_W // D):
            c0 = QKV_W + D * j
            rest_ref[:, D * j:D * (j + 1)] = _dot_nt(h, w_ref[c0:c0 + D, :]) + b_ref[:, c0:c0 + D]

    return _run(
        body, rider, name="rms_inproj", grid=(T // tm,),
        in_specs=[pl.BlockSpec((tm, D), lambda i: (i, 0)), _const((1, D)), _const((IN_W, D), 1), _const((1, IN_W))],
        out_specs=[pl.BlockSpec((tm, QKV_W), lambda i: (i, 0)), pl.BlockSpec((tm, REST_W), lambda i: (i, 0))],
        out_shape=[jax.ShapeDtypeStruct((T, QKV_W), CDT), jax.ShapeDtypeStruct((T, REST_W), F32)],
        sem=("parallel",), args=(x, g, wt, b))


def _lane_halves(shape):
    lane = lax.broadcasted_iota(jnp.int32, shape, 1)
    return lane < HEAD_DIM, lane >= HEAD_DIM


def _swap_halves(v):
    return pltpu.roll(v.astype(F32), HEAD_DIM, axis=1).astype(v.dtype)


N_KV = KV_W // HEAD_DIM
GROUP = N_Q // N_KV
STACK = GROUP * BLOCK


def _score_bias():
    row = np.arange(STACK)[:, None] % BLOCK
    col = np.arange(2 * BLOCK)[None, :]
    dist = row + BLOCK - col
    window = (dist >= 0) & (dist < BLOCK)
    slopes = np.asarray(SLOPES, np.float32).reshape(N_KV, GROUP)
    out = np.empty((2, N_KV, STACK, 2 * BLOCK), np.float32)
    for first in range(2):
        valid = window & ((col >= BLOCK) | (first == 0))
        for g in range(N_KV):
            slope = np.repeat(slopes[g], BLOCK)[:, None]
            out[first, g] = np.where(valid, -(slope * dist.astype(np.float32)), np.float32(NEG))
    return jnp.asarray(out)


def _per_head_column(vals):
    row = lax.broadcasted_iota(jnp.int32, (STACK, 1), 0)
    col = jnp.full((STACK, 1), vals[GROUP - 1], F32)
    for i in reversed(range(GROUP - 1)):
        col = jnp.where(row < (i + 1) * BLOCK, vals[i], col)
    return col


def _stack_heads(dst, src_ref, r0, g, scale=None):
    lane = lax.broadcasted_iota(jnp.int32, (BLOCK, 2 * HEAD_DIM), 1)
    keep = (lane >= HEAD_DIM) if g else (lane < HEAD_DIM)
    for i in range(GROUP):
        h = GROUP * g + i
        tile = src_ref[pl.ds(r0, BLOCK), (h // 2) * 128:(h // 2 + 1) * 128]
        if h % 2 != g:
            tile = _swap_halves(tile)
        if scale is not None:
            tile = tile * jnp.asarray(scale, tile.dtype)
        dst[i * BLOCK:(i + 1) * BLOCK, :] = jnp.where(keep, tile, jnp.zeros_like(tile))


def _unstack_heads(dst_ref, stacked, r0, g):
    lane = lax.broadcasted_iota(jnp.int32, (BLOCK, 2 * HEAD_DIM), 1)
    tiles = []
    for j in range(GROUP // 2):
        even = stacked[(2 * j) * BLOCK:(2 * j + 1) * BLOCK, :]
        odd = stacked[(2 * j + 1) * BLOCK:(2 * j + 2) * BLOCK, :]
        lo = _swap_halves(even) if g else even
        hi = odd if g else _swap_halves(odd)
        c0 = ((GROUP * g) // 2 + j) * 128
        tile = jnp.where(lane < HEAD_DIM, lo, hi)
        dst_ref[pl.ds(r0, BLOCK), c0:c0 + 128] = tile.astype(dst_ref.dtype)
        tiles.append((c0, tile))
    return tiles


def _qkv_specs(tq):
    nb = tq // BLOCK
    return [
        pl.BlockSpec((tq, ATTN_W), lambda i: (i, 0)),
        pl.BlockSpec((BLOCK, KV_W), lambda i: (jnp.maximum(i * nb - 1, 0), ATTN_W // KV_W)),
        pl.BlockSpec((tq, KV_W), lambda i: (i, ATTN_W // KV_W)),
        pl.BlockSpec((BLOCK, KV_W), lambda i: (jnp.maximum(i * nb - 1, 0), ATTN_W // KV_W + 1)),
        pl.BlockSpec((tq, KV_W), lambda i: (i, ATTN_W // KV_W + 1)),
    ]


def attn_fwd(qkv, sinks, *, tq=512):
    T = qkv.shape[0]
    nb = tq // BLOCK

    def body(sink_ref, bias_ref, q_ref, kp_ref, kc_ref, vp_ref, vc_ref, o_ref, lse_ref, kext, vext, qs):
        i = pl.program_id(0)
        kext[0:BLOCK, :] = kp_ref[...]
        kext[BLOCK:, :] = kc_ref[...]
        vext[0:BLOCK, :] = vp_ref[...]
        vext[BLOCK:, :] = vc_ref[...]
        lane_l = lax.broadcasted_iota(jnp.int32, (BLOCK, 128), 1)

        def blk(b, carry):
            r0 = pl.multiple_of(b * BLOCK, BLOCK)
            first = jnp.logical_and(i == 0, b == 0).astype(jnp.int32)
            kc = kext[pl.ds(r0, 2 * BLOCK), :]
            vc = vext[pl.ds(r0, 2 * BLOCK), :]
            lse_t = jnp.zeros((BLOCK, 128), F32)
            for g in range(N_KV):
                heads = range(GROUP * g, GROUP * (g + 1))
                _stack_heads(qs.at[g], q_ref, r0, g, SCALE)
                s = _dot_nt(qs[g], kc) + bias_ref[first, g]
                sink = _per_head_column([sink_ref[h] for h in heads])
                m = jnp.maximum(jnp.max(s, axis=-1, keepdims=True), sink)
                p = jnp.exp(s - m)
                denom = jnp.sum(p, axis=-1, keepdims=True) + jnp.exp(sink - m)
                p = p / denom
                _unstack_heads(o_ref, _dot(p.astype(CDT), vc), r0, g)
                lse = m + jnp.log(denom)
                for i_h, h in enumerate(heads):
                    lse_t = jnp.where(lane_l == h, lse[i_h * BLOCK:(i_h + 1) * BLOCK, :], lse_t)
            lse_ref[pl.ds(r0, BLOCK), :] = lse_t
            return carry

        lax.fori_loop(0, nb, blk, 0)

    return pl.pallas_call(
        body, name="attn_fwd", grid=(T // tq,),
        in_specs=[pl.BlockSpec(memory_space=pltpu.SMEM), _const((2, N_KV, STACK, 2 * BLOCK), 1)] + _qkv_specs(tq),
        out_specs=[pl.BlockSpec((tq, ATTN_W), lambda i: (i, 0)), pl.BlockSpec((tq, 128), lambda i: (i, 0))],
        out_shape=[jax.ShapeDtypeStruct((T, ATTN_W), CDT), jax.ShapeDtypeStruct((T, 128), F32)],
        scratch_shapes=[pltpu.VMEM((tq + BLOCK, KV_W), CDT), pltpu.VMEM((tq + BLOCK, KV_W), CDT),
                        pltpu.VMEM((N_KV, STACK, 2 * HEAD_DIM), CDT)],
        compiler_params=_cp("parallel"),
    )(sinks, _score_bias(), qkv, qkv, qkv, qkv, qkv)


def _halo_before(tm, width, col):
    return pl.BlockSpec((HALO, width), lambda i: (jnp.maximum(i * (tm // HALO) - 1, 0), col))


def _fill_u0(ext, a_ref, b_ref, ha_ref, hb_ref, first):
    hu = ha_ref[...] * _sig(hb_ref[...])
    ext[0:HALO, :] = jnp.where(first, jnp.zeros_like(hu), hu)
    ext[HALO:, :] = a_ref[...] * _sig(b_ref[...])


def _shifted_taps(src, w_ref, base, rc, offsets):
    acc = jnp.zeros((rc, CONV_C), F32)
    for b in range(SUBLANES):
        taps = [(k, o - b) for k, o in enumerate(offsets) if o % SUBLANES == b]
        if not taps:
            continue
        rows = rc if b == 0 else rc + SUBLANES
        part = jnp.zeros((rows, CONV_C), F32)
        for k, o8 in taps:
            part = part + w_ref[k:k + 1, :] * src[base + o8:base + o8 + rows, :]
        acc = acc + (part if b == 0 else part[b:b + rc, :])
    return acc


def _shift_copies(dst, src, rows):
    for b in range(1, SUBLANES):
        for r in range(0, rows, 64):
            n = min(64, rows - r)
            dst[b - 1, r:r + n, :] = src[r + b:r + b + n, :]


def _window(src, copies, off, r0, n):
    b = off % SUBLANES
    a = r0 + off - b
    return src[a:a + n, :] if b == 0 else copies[b - 1, a:a + n, :]


def _conv_rows(ext, w_ref, r0, rc):
    return _shifted_taps(ext, w_ref, r0, rc, [HALO - (CONV_K - 1) + k for k in range(CONV_K)])


def _layer_norm(u1, g, b):
    mu = jnp.mean(u1, axis=-1, keepdims=True)
    xc = u1 - mu
    rstd = lax.rsqrt(jnp.mean(xc * xc, axis=-1, keepdims=True) + EPS)
    n = xc * rstd
    return n, rstd, n * g + b


CONV_RC = 32
CONV_RC_1PASS = 64


def conv_fwd(rest, cw, cb, lg, lb, *, tm=512):
    T = rest.shape[0]

    def body(a_ref, b_ref, ha_ref, hb_ref, w_ref, cb_ref, lg_ref, lb_ref, o_ref, u1_ref, ext):
        _fill_u0(ext, a_ref, b_ref, ha_ref, hb_ref, pl.program_id(0) == 0)
        rc = CONV_RC_1PASS
        for r0 in range(0, tm, rc):
            u1 = _conv_rows(ext, w_ref, r0, rc) + cb_ref[...]
            u1_ref[r0:r0 + rc, :] = u1
            _, _, u2 = _layer_norm(u1, lg_ref[...], lb_ref[...])
            o_ref[r0:r0 + rc, :] = (u2 * _sig(u2)).astype(o_ref.dtype)

    row = lambda i: (i, 0)
    return pl.pallas_call(
        body, name="conv_fwd", grid=(T // tm,),
        in_specs=[pl.BlockSpec((tm, CONV_C), row), pl.BlockSpec((tm, CONV_C), lambda i: (i, 1)),
                  _halo_before(tm, CONV_C, 0), _halo_before(tm, CONV_C, 1),
                  _const((CONV_K, CONV_C)), _const((1, CONV_C)), _const((1, CONV_C)), _const((1, CONV_C))],
        out_specs=[pl.BlockSpec((tm, CONV_C), row), pl.BlockSpec((tm, CONV_C), row)],
        out_shape=[jax.ShapeDtypeStruct((T, CONV_C), CDT), jax.ShapeDtypeStruct((T, CONV_C), F32)],
        scratch_shapes=[pltpu.VMEM((tm + HALO, CONV_C), F32)],
        compiler_params=_cp("parallel"),
    )(rest, rest, rest, rest, cw, cb, lg, lb)


def merge_out(x, attn, u3, rest, wa, wc, bc, wo, *, tm=512):
    T = x.shape[0]

    def body(x_ref, at_ref, u_ref, ga_ref, gc_ref, wa_ref, wc_ref, bc_ref, wo_ref, o_ref):
        br_a = _dot(at_ref[...], wa_ref[...])
        br_c = _dot(u_ref[...], wc_ref[...]) + bc_ref[...]
        merged = _sig(ga_ref[...]) * br_a + _sig(gc_ref[...]) * br_c
        o_ref[...] = x_ref[...] + _dot(merged.astype(CDT), wo_ref[...])

    return pl.pallas_call(
        body, name="merge_out", grid=(T // tm,),
        in_specs=[pl.BlockSpec((tm, D), lambda i: (i, 0)), pl.BlockSpec((tm, ATTN_W), lambda i: (i, 0)),
                  pl.BlockSpec((tm, CONV_C), lambda i: (i, 0)),
                  pl.BlockSpec((tm, D), lambda i: (i, 1)), pl.BlockSpec((tm, D), lambda i: (i, 2)),
                  _const((ATTN_W, D), 1), _const((CONV_C, D), 1), _const((1, D)), _const((D, D), 1)],
        out_specs=pl.BlockSpec((tm, D), lambda i: (i, 0)),
        out_shape=jax.ShapeDtypeStruct((T, D), F32),
        compiler_params=_cp("parallel"),
    )(x, attn, u3, rest, rest, wa, wc, bc, wo)


def _loss_and_grad(xv, gv, tgt):
    r = lax.rsqrt(jnp.mean(xv * xv, axis=-1, keepdims=True) + EPS)
    e = xv * r * gv - tgt
    dx, dg_rows = _rms_bwd(xv, gv, e * (1.0 / D))
    return 0.5 * jnp.mean(e * e, axis=-1, keepdims=True), dx, dg_rows


def mlp_fwd(x, g, w1, w2, *, head=None, tm=256, tf=D_FF, rider=None):
    T = x.shape[0]
    nf = D_FF // tf

    def body(x_ref, g_ref, w1_ref, w2_ref, *rest):
        if head is None:
            o_ref, pre_ref, h_s, acc_s = rest
        else:
            gf_ref, t_ref, pre_ref, dy_ref, dgf_ref, loss_ref, h_s, acc_s = rest
        i, f = pl.program_id(0), pl.program_id(1)

        @pl.when(f == 0)
        def _():
            xv = x_ref[...]
            r = lax.rsqrt(jnp.mean(xv * xv, axis=-1, keepdims=True) + EPS)
            h_s[...] = (xv * r * g_ref[...]).astype(CDT)
            acc_s[...] = jnp.zeros_like(acc_s)

        pre = _dot(h_s[...], w1_ref[...])
        pre_ref[...] = pre
        a = jnp.square(jnp.maximum(pre, 0.0))
        acc_s[...] += _dot(a.astype(CDT), w2_ref[...])

        @pl.when(f == nf - 1)
        def _():
            y = x_ref[...] + acc_s[...]
            if head is None:
                o_ref[...] = y
                return

            @pl.when(i == 0)
            def _():
                dgf_ref[...] = jnp.zeros_like(dgf_ref)
                loss_ref[...] = jnp.zeros_like(loss_ref)

            loss_rows, dy, dg_rows = _loss_and_grad(y, gf_ref[...], t_ref[...])
            dy_ref[...] = dy
            dgf_ref[...] += _colsum(dg_rows)
            loss_ref[...] += _colsum(loss_rows)

    mode = {"pipeline_mode": pl.Buffered(1)} if nf == 1 else {}
    row = pl.BlockSpec((tm, D), lambda i, f: (i, 0))
    pre_spec, pre_shape = pl.BlockSpec((tm, tf), lambda i, f: (i, f)), jax.ShapeDtypeStruct((T, D_FF), F32)
    in_specs = [row, _const((1, D)), pl.BlockSpec((D, tf), lambda i, f: (0, f), **mode),
                pl.BlockSpec((tf, D), lambda i, f: (f, 0), **mode)]
    scratch = [pltpu.VMEM((tm, D), CDT), pltpu.VMEM((tm, D), F32)]
    if head is None:
        return _run(body, rider, name="mlp_fwd", grid=(T // tm, nf), in_specs=in_specs, out_specs=[row, pre_spec],
                    out_shape=[jax.ShapeDtypeStruct((T, D), F32), pre_shape], scratch_shapes=scratch,
                    sem=("parallel", "arbitrary"), args=(x, g, w1, w2))
    return _run(body, rider, name="mlp_fwd_loss", grid=(T // tm, nf), in_specs=in_specs + [_const((1, D)), row],
                out_specs=[pre_spec, row, _const((1, D)), _const((1, 128))],
                out_shape=[pre_shape, jax.ShapeDtypeStruct((T, D), F32), jax.ShapeDtypeStruct((1, D), F32),
                           jax.ShapeDtypeStruct((1, 128), F32)],
                scratch_shapes=scratch, sem=("arbitrary", "arbitrary"), args=(x, g, w1, w2) + tuple(head))


def _rms_bwd(xv, g, dh):
    r = lax.rsqrt(jnp.mean(xv * xv, axis=-1, keepdims=True) + EPS)
    xhat = xv * r
    dxh = dh * g
    dx = r * (dxh - xhat * jnp.mean(dxh * xhat, axis=-1, keepdims=True))
    return dx, dh * xhat


def mlp_bwd(dy, x, g, pre, w1, w2, *, tm=256, tf=D_FF, rider=None):
    T = x.shape[0]
    nf = D_FF // tf

    def body(dy_ref, x_ref, g_ref, pre_ref, w1_ref, w2_ref, dx_ref, dg_ref, h_ref, a_ref, dpre_ref, dyb_s, acc_s):
        i, f = pl.program_id(0), pl.program_id(1)

        @pl.when(jnp.logical_and(i == 0, f == 0))
        def _():
            dg_ref[...] = jnp.zeros_like(dg_ref)

        @pl.when(f == 0)
        def _():
            dyb_s[...] = dy_ref[...].astype(CDT)
            acc_s[...] = jnp.zeros_like(acc_s)

        pre = pre_ref[...]
        rl = jnp.maximum(pre, 0.0)
        a_ref[...] = (rl * rl).astype(CDT)
        da = _dot_nt(dyb_s[...], w2_ref[...])
        dpre = (da * (2.0 * rl)).astype(CDT)
        dpre_ref[...] = dpre
        acc_s[...] += _dot_nt(dpre, w1_ref[...])

        @pl.when(f == nf - 1)
        def _():
            xv = x_ref[...]
            gv = g_ref[...]
            dxn, dg_rows = _rms_bwd(xv, gv, acc_s[...])
            dx_ref[...] = dy_ref[...] + dxn
            dg_ref[...] += _colsum(dg_rows)
            r = lax.rsqrt(jnp.mean(xv * xv, axis=-1, keepdims=True) + EPS)
            h_ref[...] = (xv * r * gv).astype(CDT)

    row = lambda i, f: (i, 0)
    mode = {"pipeline_mode": pl.Buffered(1)} if nf == 1 else {}
    return _run(
        body, rider, name="mlp_bwd", grid=(T // tm, nf),
        in_specs=[pl.BlockSpec((tm, D), row), pl.BlockSpec((tm, D), row), _const((1, D)),
                  pl.BlockSpec((tm, tf), lambda i, f: (i, f)),
                  pl.BlockSpec((D, tf), lambda i, f: (0, f), **mode), pl.BlockSpec((tf, D), lambda i, f: (f, 0), **mode)],
        out_specs=[pl.BlockSpec((tm, D), row), _const((1, D)), pl.BlockSpec((tm, D), row),
                   pl.BlockSpec((tm, tf), lambda i, f: (i, f)), pl.BlockSpec((tm, tf), lambda i, f: (i, f))],
        out_shape=[jax.ShapeDtypeStruct((T, D), F32), jax.ShapeDtypeStruct((1, D), F32),
                   jax.ShapeDtypeStruct((T, D), CDT), jax.ShapeDtypeStruct((T, D_FF), CDT),
                   jax.ShapeDtypeStruct((T, D_FF), CDT)],
        scratch_shapes=[pltpu.VMEM((tm, D), CDT), pltpu.VMEM((tm, D), F32)],
        sem=("arbitrary", "arbitrary"), args=(dy, x, g, pre, w1, w2))


def tn_matmul(a, b, *, tm, tn, tk=2048, name, by_chip=False, rider=None):
    T, M = a.shape
    N = b.shape[1]
    tk = min(tk, T)
    nk = T // tk

    def body(a_ref, b_ref, o_ref):
        @pl.when(pl.program_id(2) == 0)
        def _():
            o_ref[...] = jnp.zeros_like(o_ref)

        o_ref[...] += _dot_tn(a_ref[...].astype(CDT), b_ref[...].astype(CDT))

    if by_chip:
        out_spec = pl.BlockSpec((None, tm, tn), lambda i, j, k: (j, i, 0))
        out_shape = jax.ShapeDtypeStruct((N // tn, M, tn), F32)
    else:
        out_spec = pl.BlockSpec((tm, tn), lambda i, j, k: (i, j))
        out_shape = jax.ShapeDtypeStruct((M, N), F32)
    res = _run(
        body, rider, name=name, grid=(M // tm, N // tn, nk),
        in_specs=[pl.BlockSpec((tk, tm), lambda i, j, k: (k, i)), pl.BlockSpec((tk, tn), lambda i, j, k: (k, j))],
        out_specs=[out_spec], out_shape=[out_shape], sem=("parallel", "parallel", "arbitrary"), args=(a, b))
    return res[0] if rider is None else (res[0][0], res[1])


def merge_bwd(dx1, attn, u3, rest, wa, wc, bc, wo, *, tm=512):
    T = dx1.shape[0]

    def body(dx_ref, at_ref, u_ref, ga_ref, gc_ref, wa_ref, wc_ref, bc_ref, wo_ref,
             mg_ref, dba_ref, dbc_ref, dat_ref, du_ref, dgate_ref, dgsum_ref, dbias_ref):
        @pl.when(pl.program_id(0) == 0)
        def _():
            dbias_ref[...] = jnp.zeros_like(dbias_ref)
            dgsum_ref[...] = jnp.zeros_like(dgsum_ref)

        br_a = _dot(at_ref[...], wa_ref[...])
        br_c = _dot(u_ref[...], wc_ref[...]) + bc_ref[...]
        sa = _sig(ga_ref[...])
        sc = _sig(gc_ref[...])
        mg_ref[...] = (sa * br_a + sc * br_c).astype(CDT)
        dm = _dot_nt(dx_ref[...].astype(CDT), wo_ref[...])
        dba = dm * sa
        dbc = dm * sc
        dga = dm * br_a * sa * (1.0 - sa)
        dgc = dm * br_c * sc * (1.0 - sc)
        dgate_ref[:, 0:D] = dga.astype(CDT)
        dgate_ref[:, D:2 * D] = dgc.astype(CDT)
        dgsum_ref[:, 0:D] += _colsum(dga)
        dgsum_ref[:, D:2 * D] += _colsum(dgc)
        dbias_ref[...] += _colsum(dbc)
        dba_b = dba.astype(CDT)
        dbc_b = dbc.astype(CDT)
        dba_ref[...] = dba_b
        dbc_ref[...] = dbc_b
        dat_ref[...] = _dot_nt(dba_b, wa_ref[...]).astype(CDT)
        du_ref[...] = _dot_nt(dbc_b, wc_ref[...])

    row = lambda i: (i, 0)
    return pl.pallas_call(
        body, name="merge_bwd", grid=(T // tm,),
        in_specs=[pl.BlockSpec((tm, D), row), pl.BlockSpec((tm, ATTN_W), row), pl.BlockSpec((tm, CONV_C), row),
                  pl.BlockSpec((tm, D), lambda i: (i, 1)), pl.BlockSpec((tm, D), lambda i: (i, 2)),
                  _const((ATTN_W, D), 1), _const((CONV_C, D), 1), _const((1, D)), _const((D, D), 1)],
        out_specs=[pl.BlockSpec((tm, D), row), pl.BlockSpec((tm, D), row), pl.BlockSpec((tm, D), row),
                   pl.BlockSpec((tm, ATTN_W), row), pl.BlockSpec((tm, CONV_C), row),
                   pl.BlockSpec((tm, 2 * D), row), _const((1, 2 * D)), _const((1, D))],
        out_shape=[jax.ShapeDtypeStruct((T, D), CDT), jax.ShapeDtypeStruct((T, D), CDT),
                   jax.ShapeDtypeStruct((T, D), CDT), jax.ShapeDtypeStruct((T, ATTN_W), CDT),
                   jax.ShapeDtypeStruct((T, CONV_C), F32), jax.ShapeDtypeStruct((T, 2 * D), CDT),
                   jax.ShapeDtypeStruct((1, 2 * D), F32), jax.ShapeDtypeStruct((1, D), F32)],
        compiler_params=_cp("arbitrary"),
    )(dx1, attn, u3, rest, rest, wa, wc, bc, wo)


def conv_bwd_ln(du3, u1, lg, lb, *, tm=512, rider=None):
    T = du3.shape[0]

    def body(du_ref, u1_ref, lg_ref, lb_ref, du1_ref, dlg_ref, dlb_ref, dcb_ref):
        @pl.when(pl.program_id(0) == 0)
        def _():
            dlg_ref[...] = jnp.zeros_like(dlg_ref)
            dlb_ref[...] = jnp.zeros_like(dlb_ref)
            dcb_ref[...] = jnp.zeros_like(dcb_ref)

        dlg = jnp.zeros((1, CONV_C), F32)
        dlb = jnp.zeros((1, CONV_C), F32)
        dcb = jnp.zeros((1, CONV_C), F32)
        rc = CONV_RC_1PASS
        for r0 in range(0, tm, rc):
            n, rstd, u2 = _layer_norm(u1_ref[r0:r0 + rc, :], lg_ref[...], lb_ref[...])
            s = _sig(u2)
            du2 = du_ref[r0:r0 + rc, :] * (s + u2 * s * (1.0 - s))
            dn = du2 * lg_ref[...]
            du1 = rstd * (dn - jnp.mean(dn, axis=-1, keepdims=True) - n * jnp.mean(dn * n, axis=-1, keepdims=True))
            du1_ref[r0:r0 + rc, :] = du1
            dlg = dlg + _colsum(du2 * n)
            dlb = dlb + _colsum(du2)
            dcb = dcb + _colsum(du1)
        dlg_ref[...] += dlg
        dlb_ref[...] += dlb
        dcb_ref[...] += dcb

    row = lambda i: (i, 0)
    vec = jax.ShapeDtypeStruct((1, CONV_C), F32)
    return _run(
        body, rider, name="conv_bwd_ln", grid=(T // tm,),
        in_specs=[pl.BlockSpec((tm, CONV_C), row), pl.BlockSpec((tm, CONV_C), row), _const((1, CONV_C)), _const((1, CONV_C))],
        out_specs=[pl.BlockSpec((tm, CONV_C), row), _const((1, CONV_C)), _const((1, CONV_C)), _const((1, CONV_C))],
        out_shape=[jax.ShapeDtypeStruct((T, CONV_C), F32), vec, vec, vec],
        sem=("arbitrary",), args=(du3, u1, lg, lb))


def conv_bwd_taps(du1, rest, cw, *, tm=512):
    T = du1.shape[0]
    nt = T // tm

    def body(d_ref, hd_ref, a_ref, b_ref, ha_ref, hb_ref, w_ref, dglu_ref, dgsum_ref, dw_ref, ext, dext, dcopies, dwacc):
        i = pl.program_id(0)

        @pl.when(i == 0)
        def _():
            dwacc[...] = jnp.zeros_like(dwacc)
            dgsum_ref[...] = jnp.zeros_like(dgsum_ref)

        rc = CONV_RC
        groups = lambda v: jnp.sum(v.reshape(v.shape[0] // SUBLANES, SUBLANES, CONV_C), axis=0)
        sums = [jnp.zeros((SUBLANES, CONV_C), F32), jnp.zeros((SUBLANES, CONV_C), F32)]
        _fill_u0(ext, a_ref, b_ref, ha_ref, hb_ref, i == 0)
        dext[0:SUBLANES, :] = jnp.zeros((SUBLANES, CONV_C), F32)
        dext[SUBLANES:SUBLANES + tm, :] = d_ref[...]
        hd = hd_ref[...]
        dext[SUBLANES + tm:, :] = jnp.where(i == nt - 1, jnp.zeros_like(hd), hd)
        _shift_copies(dcopies, dext, tm + HALO)
        for r0 in range(0, tm, rc):
            du0 = jnp.zeros((rc, CONV_C), F32)
            for k in range(CONV_K):
                du0 = du0 + w_ref[k:k + 1, :] * _window(dext, dcopies, SUBLANES + CONV_K - 1 - k, r0, rc)
            av = a_ref[r0:r0 + rc, :]
            sb = _sig(b_ref[r0:r0 + rc, :])
            for half, dg in enumerate((du0 * sb, du0 * av * sb * (1.0 - sb))):
                dglu_ref[r0:r0 + rc, half * CONV_C:(half + 1) * CONV_C] = dg.astype(CDT)
                sums[half] = sums[half] + groups(dg)
        for half in range(2):
            dgsum_ref[:, half * CONV_C:(half + 1) * CONV_C] += _colsum(sums[half])
        tail = lax.broadcasted_iota(jnp.int32, (SUBLANES, CONV_C), 0)
        for b in range(SUBLANES):
            taps = [(k, HALO - (CONV_K - 1) + k - b) for k in range(CONV_K) if (HALO - (CONV_K - 1) + k) % SUBLANES == b]
            accs = [jnp.zeros((SUBLANES, CONV_C), F32) for _ in taps]
            for r0 in list(range(0, tm, rc)) + ([tm] if b else []):
                n = rc if r0 < tm else SUBLANES
                dwin = _window(dext, dcopies, SUBLANES - b, r0, n)
                if r0 == tm:
                    dwin = jnp.where(tail < b, dwin, 0.0)
                for j, (k, o8) in enumerate(taps):
                    accs[j] = accs[j] + groups(dwin * ext[r0 + o8:r0 + o8 + n, :])
            for j, (k, _) in enumerate(taps):
                dwacc[8 * k:8 * k + 8, :] += accs[j]

        @pl.when(i == nt - 1)
        def _():
            dw_ref[...] = jnp.zeros_like(dw_ref)
            for k in range(CONV_K):
                dw_ref[k:k + 1, :] = _colsum(dwacc[8 * k:8 * k + 8, :])

    row = lambda i: (i, 0)
    return pl.pallas_call(
        body, name="conv_bwd_taps", grid=(nt,),
        in_specs=[pl.BlockSpec((tm, CONV_C), row),
                  pl.BlockSpec((HALO, CONV_C), lambda i: (jnp.minimum((i + 1) * (tm // HALO), T // HALO - 1), 0)),
                  pl.BlockSpec((tm, CONV_C), row), pl.BlockSpec((tm, CONV_C), lambda i: (i, 1)),
                  _halo_before(tm, CONV_C, 0), _halo_before(tm, CONV_C, 1), _const((CONV_K, CONV_C))],
        out_specs=[pl.BlockSpec((tm, 2 * CONV_C), row), _const((1, 2 * CONV_C)), _const((HALO, CONV_C))],
        out_shape=[jax.ShapeDtypeStruct((T, 2 * CONV_C), CDT), jax.ShapeDtypeStruct((1, 2 * CONV_C), F32),
                   jax.ShapeDtypeStruct((HALO, CONV_C), F32)],
        scratch_shapes=[pltpu.VMEM((tm + HALO, CONV_C), F32), pltpu.VMEM((SUBLANES + tm + HALO, CONV_C), F32),
                        pltpu.VMEM((SUBLANES - 1, tm + HALO, CONV_C), F32), pltpu.VMEM((8 * CONV_K, CONV_C), F32)],
        compiler_params=_cp("arbitrary"),
    )(du1, du1, rest, rest, rest, rest, cw)


def attn_bwd(qkv, do, lse, sinks, *, tq=512, rider=None):
    T = qkv.shape[0]
    nb = tq // BLOCK

    def body(sink_ref, bias_ref, q_ref, kp_ref, kc_ref, vp_ref, vc_ref, do_ref, lse_ref,
             dq_ref, dqsum_ref, dkv_ref, spill_ref, dsink_ref, kext, vext, dkext, dvext, qs, dos):
        i = pl.program_id(0)

        @pl.when(i == 0)
        def _():
            dsink_ref[...] = jnp.zeros_like(dsink_ref)
            dqsum_ref[...] = jnp.zeros_like(dqsum_ref)

        kext[0:BLOCK, :] = kp_ref[...]
        kext[BLOCK:, :] = kc_ref[...]
        vext[0:BLOCK, :] = vp_ref[...]
        vext[BLOCK:, :] = vc_ref[...]
        dkext[...] = jnp.zeros_like(dkext)
        dvext[...] = jnp.zeros_like(dvext)
        lane_l = lax.broadcasted_iota(jnp.int32, (BLOCK, 128), 1)
        lane_k = lax.broadcasted_iota(jnp.int32, (2 * BLOCK, KV_W), 1)

        def blk(b, dsink):
            r0 = pl.multiple_of(b * BLOCK, BLOCK)
            first = jnp.logical_and(i == 0, b == 0).astype(jnp.int32)
            kc = kext[pl.ds(r0, 2 * BLOCK), :]
            vc = vext[pl.ds(r0, 2 * BLOCK), :]
            lse_t = lse_ref[pl.ds(r0, BLOCK), :]
            dk = jnp.zeros((2 * BLOCK, KV_W), F32)
            dv = jnp.zeros((2 * BLOCK, KV_W), F32)
            for g in range(N_KV):
                heads = range(GROUP * g, GROUP * (g + 1))
                _stack_heads(qs.at[g], q_ref, r0, g, SCALE)
                _stack_heads(dos.at[g], do_ref, r0, g)
                qv = qs[g]
                dov = dos[g]
                s = _dot_nt(qv, kc) + bias_ref[first, g]
                lse = jnp.concatenate(
                    [jnp.sum(jnp.where(lane_l == h, lse_t, 0.0), axis=-1, keepdims=True) for h in heads], axis=0)
                p = jnp.exp(s - lse)
                dp = _dot_nt(dov, vc)
                dd = jnp.sum(p * dp, axis=-1, keepdims=True)
                ds = (p * (dp - dd)).astype(CDT)
                keep = (lane_k >= HEAD_DIM) if g else (lane_k < HEAD_DIM)
                for c0, tile in _unstack_heads(dq_ref, _dot(ds, jnp.where(keep, kc, jnp.zeros_like(kc))) * SCALE, r0, g):
                    dqsum_ref[:, c0:c0 + 128] += _colsum(tile)
                dk = dk + _dot_tn(ds, qv)
                dv = dv + _dot_tn(p.astype(CDT), dov)
                wsink = jnp.exp(_per_head_column([sink_ref[h] for h in heads]) - lse) * dd
                for i_h, h in enumerate(heads):
                    part = jnp.sum(wsink[i_h * BLOCK:(i_h + 1) * BLOCK, :], axis=0, keepdims=True)
                    dsink = dsink - jnp.where(lane_l[0:1, :] == h, part, 0.0)
            dkext[pl.ds(r0, 2 * BLOCK), :] += dk
            dvext[pl.ds(r0, 2 * BLOCK), :] += dv
            return dsink

        dsink_ref[...] += lax.fori_loop(0, nb, blk, jnp.zeros((1, 128), F32))
        dkv_ref[:, 0:KV_W] = dkext[BLOCK:, :]
        dkv_ref[:, KV_W:2 * KV_W] = dvext[BLOCK:, :]
        spill_ref[:, 0:KV_W] = dkext[0:BLOCK, :]
        spill_ref[:, KV_W:2 * KV_W] = dvext[0:BLOCK, :]

    row = lambda i: (i, 0)
    return _run(
        body, rider, name="attn_bwd", grid=(T // tq,),
        in_specs=[pl.BlockSpec(memory_space=pltpu.SMEM), _const((2, N_KV, STACK, 2 * BLOCK), 1)] + _qkv_specs(tq)
        + [pl.BlockSpec((tq, ATTN_W), row), pl.BlockSpec((tq, 128), row)],
        out_specs=[pl.BlockSpec((tq, ATTN_W), row), _const((1, ATTN_W)), pl.BlockSpec((tq, 2 * KV_W), row),
                   pl.BlockSpec((BLOCK, 2 * KV_W), row), _const((1, 128))],
        out_shape=[jax.ShapeDtypeStruct((T, ATTN_W), CDT), jax.ShapeDtypeStruct((1, ATTN_W), F32),
                   jax.ShapeDtypeStruct((T, 2 * KV_W), F32),
                   jax.ShapeDtypeStruct((T // tq * BLOCK, 2 * KV_W), F32), jax.ShapeDtypeStruct((1, 128), F32)],
        scratch_shapes=[pltpu.VMEM((tq + BLOCK, KV_W), CDT), pltpu.VMEM((tq + BLOCK, KV_W), CDT),
                        pltpu.VMEM((tq + BLOCK, KV_W), F32), pltpu.VMEM((tq + BLOCK, KV_W), F32),
                        pltpu.VMEM((N_KV, STACK, 2 * HEAD_DIM), CDT), pltpu.VMEM((N_KV, STACK, 2 * HEAD_DIM), CDT)],
        sem=("arbitrary",), args=(sinks, _score_bias(), qkv, qkv, qkv, qkv, qkv, do, lse))


def inproj_bwd(dres, x, g, w, dq, dkv, spill, dglu, dgate, sums, *, tm=512):
    T = x.shape[0]
    nt = T // tm
    pieces = ((0, ATTN_W), (QKV_W, 2 * CONV_C), (QKV_W + 2 * CONV_C, 2 * D))

    def body(dr_ref, x_ref, g_ref, w_ref, dq_ref, dkv_ref, sp_ref, dglu_ref, dgate_ref, sq_ref, sglu_ref, sgate_ref,
             dx_ref, dp_ref, h_ref, dg_ref, db_ref):
        i = pl.program_id(0)

        @pl.when(i == 0)
        def _():
            dg_ref[...] = jnp.zeros_like(dg_ref)
            db_ref[:, ATTN_W:QKV_W] = jnp.zeros((1, QKV_W - ATTN_W), F32)
            for (c0, wd), s_ref in zip(pieces, (sq_ref, sglu_ref, sgate_ref)):
                db_ref[:, c0:c0 + wd] = s_ref[...]

        sp = sp_ref[...]
        sp = jnp.where(i == nt - 1, jnp.zeros_like(sp), sp)
        dkv = dkv_ref[...]
        db_ref[:, ATTN_W:QKV_W] += _colsum(dkv) + _colsum(sp)
        dp_ref[0:tm - BLOCK, ATTN_W:QKV_W] = dkv[0:tm - BLOCK, :].astype(CDT)
        dp_ref[tm - BLOCK:tm, ATTN_W:QKV_W] = (dkv[tm - BLOCK:tm, :] + sp).astype(CDT)
        for (c0, wd), ref in zip(pieces, (dq_ref, dglu_ref, dgate_ref)):
            dp_ref[:, c0:c0 + wd] = ref[...]
        dh = _dot(dp_ref[...], w_ref[...])
        xv = x_ref[...]
        gv = g_ref[...]
        dxn, dg_rows = _rms_bwd(xv, gv, dh)
        dx_ref[...] = dr_ref[...] + dxn
        dg_ref[...] += _colsum(dg_rows)
        r = lax.rsqrt(jnp.mean(xv * xv, axis=-1, keepdims=True) + EPS)
        h_ref[...] = (xv * r * gv).astype(CDT)

    row = lambda i: (i, 0)
    return pl.pallas_call(
        body, name="inproj_bwd", grid=(nt,),
        in_specs=[pl.BlockSpec((tm, D), row), pl.BlockSpec((tm, D), row), _const((1, D)), _const((IN_W, D), 1),
                  pl.BlockSpec((tm, ATTN_W), row), pl.BlockSpec((tm, 2 * KV_W), row),
                  pl.BlockSpec((BLOCK, 2 * KV_W), lambda i: (jnp.minimum(i + 1, nt - 1), 0)),
                  pl.BlockSpec((tm, 2 * CONV_C), row), pl.BlockSpec((tm, 2 * D), row)]
        + [_const((1, wd)) for _, wd in pieces],
        out_specs=[pl.BlockSpec((tm, D), row), pl.BlockSpec((tm, IN_W), row), pl.BlockSpec((tm, D), row),
                   _const((1, D)), _const((1, IN_W))],
        out_shape=[jax.ShapeDtypeStruct((T, D), F32), jax.ShapeDtypeStruct((T, IN_W), CDT),
                   jax.ShapeDtypeStruct((T, D), CDT), jax.ShapeDtypeStruct((1, D), F32),
                   jax.ShapeDtypeStruct((1, IN_W), F32)],
        compiler_params=_cp("arbitrary"),
    )(dres, x, g, w, dq, dkv, spill, dglu, dgate, *sums)


ATTN_TILE = 512
MATRICES = ("w_in", "w_attn_proj", "w_conv_proj", "w_out", "w_mlp1", "w_mlp2")
SMALL = ("mix_norm_g", "b_in", "sinks", "conv_w", "conv_b", "conv_ln_g", "conv_ln_b", "b_conv_proj", "mlp_norm_g")


def forward_backward(x, tgt, hooks):
    def call(fn, kernel, l, *args, **kw):
        rider = hooks.rider(kernel, l)
        if rider is None:
            return fn(*args, **kw)
        outs, landed = fn(*args, rider=rider, **kw)
        hooks.landed(kernel, l, landed)
        return outs

    vec = hooks.vec
    saved = []
    for l in range(DEPTH):
        qkv, rest = call(rms_inproj, "rms_inproj", l, x, vec("mix_norm_g", l), hooks.w_in(l), vec("b_in", l))
        m = hooks.mats(l)
        attn, lse = attn_fwd(qkv, hooks.sinks(l), tq=ATTN_TILE)
        u3, u1 = conv_fwd(rest, hooks.taps(l), vec("conv_b", l), vec("conv_ln_g", l), vec("conv_ln_b", l))
        x1 = merge_out(x, attn, u3, rest, m["w_attn_proj"], m["w_conv_proj"], vec("b_conv_proj", l), m["w_out"])
        if l < DEPTH - 1:
            x_next, pre = call(mlp_fwd, "mlp_fwd", l, x1, vec("mlp_norm_g", l), m["w_mlp1"], m["w_mlp2"])
        else:
            x_next = None
            pre, dx, dgf, loss = call(mlp_fwd, "mlp_fwd", l, x1, vec("mlp_norm_g", l), m["w_mlp1"], m["w_mlp2"],
                                      head=(hooks.final_g, tgt))
        saved.append((x, qkv, rest, attn, lse, u3, u1, x1, pre))
        x = x_next
    small = {n: [None] * DEPTH for n in SMALL}
    small["final_norm_g"] = dgf
    for l in reversed(range(DEPTH)):
        x0, qkv, rest, attn, lse, u3, u1, x1, pre = saved[l]
        m = hooks.mats(l)
        dx1, dg2, h2, a, dpre = call(mlp_bwd, "mlp_bwd", l, dx, x1, vec("mlp_norm_g", l), pre, m["w_mlp1"], m["w_mlp2"])
        small["mlp_norm_g"][l] = dg2
        group = {}
        group["w_mlp1"] = call(tn_matmul, "tn_mlp1", l, h2, dpre, tm=1024, tn=1024, tk=4096, name="tn_mlp1", by_chip=True)
        group["w_mlp2"] = call(tn_matmul, "tn_mlp2", l, a, dx, tm=1024, tn=1024, name="tn_mlp2")
        merged, dba, dbc, dattn, du3, dgate, dgate_sum, dbcp = merge_bwd(
            dx1, attn, u3, rest, m["w_attn_proj"], m["w_conv_proj"], vec("b_conv_proj", l), m["w_out"])
        small["b_conv_proj"][l] = dbcp
        group["w_out"] = tn_matmul(merged, dx1, tm=1024, tn=1024, name="tn_out")
        group["w_attn_proj"] = tn_matmul(attn, dba, tm=512, tn=256, tk=4096, name="tn_attn_proj", by_chip=True)
        group["w_conv_proj"] = tn_matmul(u3, dbc, tm=512, tn=256, tk=4096, name="tn_conv_proj", by_chip=True)
        hooks.grads(l, "A", group)
        du1, dlg, dlb, dcb = call(conv_bwd_ln, "conv_bwd_ln", l, du3, u1, vec("conv_ln_g", l), vec("conv_ln_b", l))
        small["conv_ln_g"][l], small["conv_ln_b"][l], small["conv_b"][l] = dlg, dlb, dcb
        dglu, dglu_sum, dcw = conv_bwd_taps(du1, rest, hooks.taps(l))
        small["conv_w"][l] = dcw
        dq, dq_sum, dkv, spill, dsink = call(attn_bwd, "attn_bwd", l, qkv, dattn, lse, hooks.sinks(l), tq=ATTN_TILE)
        small["sinks"][l] = dsink
        dx, dproj, h, dg, db = inproj_bwd(dx1, x0, vec("mix_norm_g", l), hooks.w_in(l), dq, dkv, spill, dglu, dgate,
                                          (dq_sum, dglu_sum, dgate_sum), tm=ATTN_TILE)
        small["mix_norm_g"][l], small["b_in"][l] = dg, db
        hooks.grads(l, "B", {"w_in": call(tn_matmul, "tn_in", l, dproj, h, tm=768, tn=1024, tk=4096, name="tn_in")})
    return loss, dx, small


class _LocalHooks:
    def __init__(self, p):
        self.p = p
        self.final_g = p["final_norm_g"]
        self.got = {n: [None] * DEPTH for n in MATRICES}

    def w_in(self, l):
        return self.p["w_in"][l].T

    def mats(self, l):
        return {n: self.p[n][l] for n in MATRICES}

    def vec(self, n, l):
        return self.p[n][l]

    def sinks(self, l):
        return self.p["sinks"][l]

    def taps(self, l):
        return self.p["conv_w"][l]

    def rider(self, kernel, l):
        return None

    def grads(self, l, group, g):
        for n, v in g.items():
            if v.ndim == 3:
                v = v.transpose(1, 0, 2).reshape(v.shape[1], -1)
            self.got[n][l] = v.T if n == "w_in" else v


def local_grads(x, tgt, p):
    hooks = _LocalHooks(p)
    loss, dx, small = forward_backward(x, tgt, hooks)
    small["conv_w"] = [g[0:CONV_K] for g in small["conv_w"]]
    small["sinks"] = [g[0, 0:N_Q] for g in small["sinks"]]
    return loss, dx, {**small, **hooks.got}


MESH = pl.DeviceIdType.MESH
N_CHIPS = 4
N_DEV = 8
FLAT_W = 1024
FLAT_PARTS = (("w_in", 960), ("w_attn_proj", 128), ("w_conv_proj", 128), ("w_out", 256), ("w_mlp1", 1024), ("w_mlp2", 1024))
FLAT_ROWS = sum(r for _, r in FLAT_PARTS)
W_IN_ROWS = FLAT_PARTS[0][1]
GROUP_A = (("w_mlp1", 1024), ("w_mlp2", 1024), ("w_out", 256), ("w_attn_proj", 128), ("w_conv_proj", 128))
COL_SHARDED = ("w_in", "w_attn_proj", "w_conv_proj", "w_mlp1")
FULL_SHAPES = {"w_in": (D, IN_W), "w_attn_proj": (ATTN_W, D), "w_conv_proj": (CONV_C, D), "w_out": (D, D),
               "w_mlp1": (D, D_FF), "w_mlp2": (D_FF, D)}


def _place():
    x, y, c = lax.axis_index("x"), lax.axis_index("y"), lax.axis_index("c")
    return x, y, c, 2 * x + y


def _peer_chips(x, y, j):
    return [((x, 1 - y), j ^ 1), ((1 - x, y), j ^ 2), ((1 - x, 1 - y), j ^ 3)]


def _remote(src, dst, sems, k, n, to):
    return pltpu.make_async_remote_copy(src_ref=src, dst_ref=dst, send_sem=sems.at[k], recv_sem=sems.at[n + k],
                                        device_id=to, device_id_type=MESH)


def _half(c, rows):
    h = rows // 2
    return pl.ds(pl.multiple_of(c * h, 16), h)


def gather_rider(wsh):
    R = wsh.shape[0]
    n = 7

    def plan(rins, routs, sems):
        (w_ref,), (out_ref,) = rins, routs
        x, y, c, j = _place()
        peers = _peer_chips(x, y, j)
        mine, other = _half(c, R), _half(1 - c, R)
        sent = [_remote(w_ref.at[mine], out_ref.at[j, mine], sems, k, n, (*chip, c)) for k, (chip, _) in enumerate(peers)]
        sent.append(_remote(w_ref, out_ref.at[j], sems, 6, n, (x, y, 1 - c)))
        landed = [_remote(w_ref.at[mine], out_ref.at[pj, mine], sems, k, n, (x, y, c)) for k, (_, pj) in enumerate(peers)]
        passed = [_remote(out_ref.at[pj, mine], out_ref.at[pj, mine], sems, 3 + k, n, (x, y, 1 - c))
                  for k, (_, pj) in enumerate(peers)]
        handed = [_remote(w_ref.at[mine], out_ref.at[pj, other], sems, 3 + k, n, (x, y, c)) for k, (_, pj) in enumerate(peers)]
        handed.append(_remote(w_ref, out_ref.at[j], sems, 6, n, (x, y, c)))
        return sent, landed, passed, handed

    def start(rins, routs, sems):
        for cp in plan(rins, routs, sems)[0]:
            cp.start()

    def late(rins, routs, sems):
        _, landed, passed, _ = plan(rins, routs, sems)
        for k in range(3):
            landed[k].wait_recv()
            passed[k].start()

    def finish(rins, routs, sems):
        sent, _, passed, handed = plan(rins, routs, sems)
        for cp in handed:
            cp.wait_recv()
        for cp in sent + passed:
            cp.wait_send()

    return Rider((wsh,), (jax.ShapeDtypeStruct((N_CHIPS,) + wsh.shape, wsh.dtype),), 2 * n, start, finish, late)


def swap_rider(g):
    R = g.shape[1]

    def plan(rins, routs, sems):
        (g_ref,), (got_ref,) = rins, routs
        x, y, c, _ = _place()
        return _remote(g_ref.at[:, _half(1 - c, R), :], got_ref, sems, 0, 1, (x, y, 1 - c))

    def start(rins, routs, sems):
        plan(rins, routs, sems).start()

    def finish(rins, routs, sems):
        plan(rins, routs, sems).wait()

    return Rider((g,), (jax.ShapeDtypeStruct((N_CHIPS, R // 2, FLAT_W), g.dtype),), 2, start, finish)


def exchange_rider(pb):
    def plan(rins, routs, sems):
        (pb_ref,), (got_ref,) = rins, routs
        x, y, c, j = _place()
        peers = _peer_chips(x, y, j)
        sent = [_remote(pb_ref.at[pj], got_ref.at[j], sems, k, 3, (*chip, c)) for k, (chip, pj) in enumerate(peers)]
        landed = [_remote(pb_ref.at[pj], got_ref.at[pj], sems, k, 3, (x, y, c)) for k, (_, pj) in enumerate(peers)]
        return sent, landed

    def start(rins, routs, sems):
        for cp in plan(rins, routs, sems)[0]:
            cp.start()

    def finish(rins, routs, sems):
        sent, landed = plan(rins, routs, sems)
        for cp in landed:
            cp.wait_recv()
        for cp in sent:
            cp.wait_send()

    return Rider((pb,), (jax.ShapeDtypeStruct(pb.shape, pb.dtype),), 6, start, finish)


def share_rider(tot):
    def plan(rins, routs, sems):
        (t_ref,), (got_ref,) = rins, routs
        x, y, c, _ = _place()
        return _remote(t_ref, got_ref, sems, 0, 1, (x, y, 1 - c))

    def start(rins, routs, sems):
        plan(rins, routs, sems).start()

    def finish(rins, routs, sems):
        plan(rins, routs, sems).wait()

    return Rider((tot,), (jax.ShapeDtypeStruct(tot.shape, tot.dtype),), 2, start, finish)


def pair_sum(g, got):
    nj, R, W = g.shape
    h = R // 2
    tile = h // 2

    def body(g_ref, got_ref, pb_ref, own_ref):
        v = g_ref[...] + got_ref[...]
        pb_ref[...] = v.astype(pb_ref.dtype)

        @pl.when(pl.program_id(1) == _place()[3])
        def _():
            own_ref[...] = v

    return pl.pallas_call(
        body, name="pair_sum", grid=(h // tile, nj),
        in_specs=[pl.BlockSpec((None, tile, W), lambda r, j: (j, lax.axis_index("c") * (h // tile) + r, 0)),
                  pl.BlockSpec((None, tile, W), lambda r, j: (j, r, 0))],
        out_specs=[pl.BlockSpec((None, tile, W), lambda r, j: (j, r, 0)), pl.BlockSpec((tile, W), lambda r, j: (r, 0))],
        out_shape=[jax.ShapeDtypeStruct((nj, h, W), CDT), jax.ShapeDtypeStruct((h, W), F32)],
        compiler_params=_cp("arbitrary", "arbitrary"),
    )(g, got)


def total_sum(own, got):
    R, W = own.shape
    tile = R // 2

    def body(own_ref, a_ref, b_ref, c_ref, o_ref):
        o_ref[...] = ((own_ref[...] + a_ref[...].astype(F32)) + b_ref[...].astype(F32)) + c_ref[...].astype(F32)

    def slab(k):
        return pl.BlockSpec((None, tile, W), lambda r: (_place()[3] ^ (k + 1), r, 0))

    return pl.pallas_call(
        body, name="total_sum", grid=(R // tile,),
        in_specs=[pl.BlockSpec((tile, W), lambda r: (r, 0)), slab(0), slab(1), slab(2)],
        out_specs=pl.BlockSpec((tile, W), lambda r: (r, 0)),
        out_shape=jax.ShapeDtypeStruct((R, W), F32),
        compiler_params=_cp("arbitrary"),
    )(own, got, got, got)


def _all_peers(x, y, c):
    return [(x ^ (r >> 2), y ^ ((r >> 1) & 1), c ^ (r & 1)) for r in range(1, N_DEV)]


ROW_ITEMS = (("mix_norm_g", D), ("b_in", IN_W), ("sinks", N_Q), ("conv_b", CONV_C), ("conv_ln_g", CONV_C),
             ("conv_ln_b", CONV_C), ("b_conv_proj", D), ("mlp_norm_g", D))
TAPS_ROW = 16
TAPS_ROWS = 32
LAYER_ROWS = TAPS_ROW + TAPS_ROWS
FINAL_ROW = DEPTH * LAYER_ROWS
SMALL_ROWS = FINAL_ROW + SUBLANES


def _row_chunks():
    out, r = {}, 0
    for n, width in ROW_ITEMS:
        out[n] = [(r + i, FLAT_W * i, min(FLAT_W, width - FLAT_W * i)) for i in range(-(-width // FLAT_W))]
        r += len(out[n])
    assert r <= TAPS_ROW
    return out


def sum_small(gsm):
    chunks = _row_chunks()
    ins = []
    for l in range(DEPTH):
        ins += [gsm[n][l] for n, _ in ROW_ITEMS] + [gsm["conv_w"][l]]
    ins.append(gsm["final_norm_g"])
    n_in = len(ins)

    def body(*refs):
        in_refs, o_ref, buf, send_sems, recv_sems = refs[:n_in], refs[n_in], refs[n_in + 1], refs[n_in + 2], refs[n_in + 3]
        x, y, c, _ = _place()
        me = 4 * x + 2 * y + c
        mine = buf.at[me]
        mine[...] = jnp.zeros((SMALL_ROWS, FLAT_W), F32)
        k = 0
        for l in range(DEPTH):
            for n, _ in ROW_ITEMS:
                for r, c0, wd in chunks[n]:
                    mine[l * LAYER_ROWS + r:l * LAYER_ROWS + r + 1, 0:wd] = in_refs[k][:, c0:c0 + wd]
                k += 1
            mine[l * LAYER_ROWS + TAPS_ROW:(l + 1) * LAYER_ROWS, 0:CONV_C] = in_refs[k][...]
            k += 1
        mine[FINAL_ROW:FINAL_ROW + 1, :] = in_refs[k][...]
        peers = _all_peers(x, y, c)
        sends = [pltpu.make_async_remote_copy(src_ref=mine, dst_ref=mine, send_sem=send_sems.at[r], recv_sem=recv_sems.at[r],
                                              device_id=to, device_id_type=MESH) for r, to in enumerate(peers)]
        for cp in sends:
            cp.start()
        for r in range(N_DEV - 1):
            pltpu.make_async_remote_copy(src_ref=mine, dst_ref=buf.at[me ^ (r + 1)], send_sem=send_sems.at[r],
                                         recv_sem=recv_sems.at[r], device_id=(x, y, c), device_id_type=MESH).wait_recv()
        for cp in sends:
            cp.wait_send()
        acc = buf[0]
        for d in range(1, N_DEV):
            acc = acc + buf[d]
        o_ref[...] = acc

    vm = pl.BlockSpec(memory_space=pltpu.VMEM)
    return pl.pallas_call(
        body, name="sum_small", out_shape=jax.ShapeDtypeStruct((SMALL_ROWS, FLAT_W), F32),
        in_specs=[vm] * n_in, out_specs=vm,
        scratch_shapes=[pltpu.VMEM((N_DEV, SMALL_ROWS, FLAT_W), F32), pltpu.SemaphoreType.DMA((N_DEV - 1,)),
                        pltpu.SemaphoreType.DMA((N_DEV - 1,))],
    )(*ins)


def gather_taps(taps):
    shard = taps.shape[2]

    def body(t_ref, o_ref, buf, send_sems, recv_sems):
        x, y, c, j = _place()
        peers = _peer_chips(x, y, j)
        buf[j] = t_ref[...]
        sends = [pltpu.make_async_remote_copy(src_ref=t_ref, dst_ref=buf.at[j], send_sem=send_sems.at[k],
                                              recv_sem=recv_sems.at[k], device_id=(*chip, c), device_id_type=MESH)
                 for k, (chip, _) in enumerate(peers)]
        for cp in sends:
            cp.start()
        for k, (_, pj) in enumerate(peers):
            pltpu.make_async_remote_copy(src_ref=t_ref, dst_ref=buf.at[pj], send_sem=send_sems.at[k],
                                         recv_sem=recv_sems.at[k], device_id=(x, y, c), device_id_type=MESH).wait_recv()
        for cp in sends:
            cp.wait_send()
        for jj in range(N_CHIPS):
            o_ref[:, :, jj * shard:(jj + 1) * shard] = buf[jj]

    vm = pl.BlockSpec(memory_space=pltpu.VMEM)
    return pl.pallas_call(
        body, name="gather_taps", out_shape=jax.ShapeDtypeStruct(taps.shape[:2] + (N_CHIPS * shard,), taps.dtype),
        in_specs=[vm], out_specs=vm,
        scratch_shapes=[pltpu.VMEM((N_CHIPS,) + taps.shape, taps.dtype), pltpu.SemaphoreType.DMA((3,)),
                        pltpu.SemaphoreType.DMA((3,))],
    )(taps)


def _adam_math(w, g, m, v):
    nm = ADAM_B1 * m + (1.0 - ADAM_B1) * g
    nv = ADAM_B2 * v + (1.0 - ADAM_B2) * jnp.square(g)
    m_hat = nm / (1.0 - ADAM_B1 ** ADAM_STEP)
    v_hat = nv / (1.0 - ADAM_B2 ** ADAM_STEP)
    return -ADAM_LR * (m_hat / (jnp.sqrt(v_hat) + ADAM_EPS) + ADAM_WD * w), nm, nv


def adamw(w, g, m, v, *, name):
    L, R, C = w.shape
    tr = next(t for t in (512, 480, 256, 128) if R % t == 0)

    def body(w_ref, g_ref, m_ref, v_ref, d_ref, nm_ref, nv_ref):
        d_ref[...], nm_ref[...], nv_ref[...] = _adam_math(w_ref[...], g_ref[...], m_ref[...], v_ref[...])

    spec = pl.BlockSpec((None, tr, C), lambda l, i: (l, i, 0))
    out = jax.ShapeDtypeStruct((L, R, C), F32)
    return pl.pallas_call(
        body, name=name, grid=(L, R // tr), in_specs=[spec] * 4, out_specs=[spec] * 3, out_shape=[out] * 3,
        compiler_params=_cp("parallel", "parallel"),
    )(w, g, m, v)


def adamw_small(packed, w, m, v):
    chunks = _row_chunks()
    names = SMALL + ("final_norm_g",)
    as_2d = lambda a: a.reshape(1, -1) if a.ndim == 1 else a
    ins = [as_2d(t[n]) for n in names for t in (w, m, v)]
    shapes = [jax.ShapeDtypeStruct(as_2d(w[n]).shape, F32) for n in names for _ in range(4)]
    n_in = len(ins)

    def body(p_ref, *refs):
        in_refs, out_refs = refs[:n_in], refs[n_in:]
        chip = _place()[3]
        for i, n in enumerate(names):
            w_ref, m_ref, v_ref = in_refs[3 * i:3 * i + 3]
            outs = out_refs[4 * i:4 * i + 4]

            def step(at, g):
                res = (g,) + _adam_math(w_ref[at], g, m_ref[at], v_ref[at])
                for o_ref, val in zip(outs, res):
                    o_ref[at] = val

            if n == "final_norm_g":
                step((slice(None), slice(None)), p_ref[FINAL_ROW:FINAL_ROW + 1, :])
                continue
            for l in range(DEPTH):
                if n == "conv_w":
                    r0 = l * LAYER_ROWS + TAPS_ROW
                    shard = CONV_C // N_CHIPS
                    g = jnp.zeros((CONV_K, shard), F32)
                    for j in range(N_CHIPS):
                        g = jnp.where(chip == j, p_ref[r0:r0 + CONV_K, j * shard:(j + 1) * shard], g)
                    step((l,), g)
                else:
                    for r, c0, wd in chunks[n]:
                        step((slice(l, l + 1), slice(c0, c0 + wd)),
                             p_ref[l * LAYER_ROWS + r:l * LAYER_ROWS + r + 1, 0:wd])

    vm = pl.BlockSpec(memory_space=pltpu.VMEM)
    res = pl.pallas_call(
        body, name="adamw_small", out_shape=shapes,
        in_specs=[vm] + [vm] * n_in, out_specs=[vm] * len(shapes),
    )(packed, *ins)
    dicts = ({}, {}, {}, {})
    for i, n in enumerate(names):
        for d, val in zip(dicts, res[4 * i:4 * i + 4]):
            d[n] = val.reshape(w[n].shape)
    return dicts


def _flat_rows(name, shard):
    return shard.T if name == "w_in" else shard.reshape(-1, FLAT_W)


def _full_matrix(slabs, name):
    K, N = FULL_SHAPES[name]
    if name == "w_in":
        return slabs.reshape(N, K)
    if name in COL_SHARDED:
        return slabs.reshape(N_CHIPS, K, N // N_CHIPS).transpose(1, 0, 2).reshape(K, N)
    return slabs.reshape(K, N)


def _first_row(parts, name):
    r = 0
    for n, rows in parts:
        if n == name:
            return r, rows
        r += rows
    raise KeyError(name)


class _Exchange:
    CARRIERS = {
        ("conv_bwd_ln", 1): ((1, "A"), "swap"), ("attn_bwd", 1): ((1, "A"), "exchange"), ("tn_in", 1): ((1, "A"), "share"),
        ("mlp_bwd", 0): ((1, "B"), "swap"), ("tn_mlp1", 0): ((1, "B"), "exchange"), ("tn_mlp2", 0): ((1, "B"), "share"),
        ("conv_bwd_ln", 0): ((0, "A"), "swap"), ("attn_bwd", 0): ((0, "A"), "exchange"), ("tn_in", 0): ((0, "A"), "share"),
    }

    def __init__(self, w, ci, chip):
        self.w, self.ci, self.chip = w, ci, chip
        self.wsh = [jnp.concatenate([_flat_rows(n, w[n][l]) for n, _ in FLAT_PARTS], axis=0).astype(CDT)
                    for l in range(DEPTH)]
        self.final_g = w["final_norm_g"].reshape(1, D)
        self.slabs = {}
        self.full = {}
        self.units = {}
        self.reduced = {}
        self._landed_weights(0, 0, _run_alone(gather_rider(self.wsh[0][:W_IN_ROWS]), "gather_w_in")[0])
        self.all_taps = gather_taps(w["conv_w"])

    def _landed_weights(self, l, r0, buf):
        self.slabs.setdefault(l, []).append((r0, buf))

    def _matrix(self, l, name):
        if (l, name) not in self.full:
            r, rows = _first_row(FLAT_PARTS, name)
            r0, buf = next((r0, buf) for r0, buf in self.slabs[l] if r0 <= r < r0 + buf.shape[1])
            self.full[(l, name)] = _full_matrix(buf[:, r - r0:r - r0 + rows], name)
        return self.full[(l, name)]

    def w_in(self, l):
        return self._matrix(l, "w_in")

    def mats(self, l):
        return {n: self._matrix(l, n) for n in MATRICES if n != "w_in"}

    def vec(self, n, l):
        return self.w[n][l].reshape(1, -1)

    def sinks(self, l):
        return self.w["sinks"][l]

    def taps(self, l):
        return self.all_taps[l]

    def rider(self, kernel, l):
        if (kernel, l) == ("rms_inproj", 0):
            return gather_rider(self.wsh[0][W_IN_ROWS:])
        if (kernel, l) == ("mlp_fwd", 0):
            return gather_rider(self.wsh[1])
        if (kernel, l) in self.CARRIERS:
            return self._stage(*self.CARRIERS[(kernel, l)])
        return None

    def landed(self, kernel, l, bufs):
        if (kernel, l) == ("rms_inproj", 0):
            self._landed_weights(0, W_IN_ROWS, bufs[0])
        elif (kernel, l) == ("mlp_fwd", 0):
            self._landed_weights(1, 0, bufs[0])
        else:
            self._stage_landed(*self.CARRIERS[(kernel, l)], bufs[0])

    def grads(self, l, group, g):
        if group == "A":
            flat = jnp.concatenate([g[n].reshape(N_CHIPS, rows, FLAT_W) for n, rows in GROUP_A], axis=1)
        else:
            flat = g["w_in"].reshape(N_CHIPS, W_IN_ROWS, FLAT_W)
        self.units[(l, group)] = {"g": flat}

    def _stage(self, key, stage):
        u = self.units[key]
        if stage == "swap":
            return swap_rider(u["g"])
        if stage == "exchange":
            u["pb"], u["own"] = pair_sum(u["g"], u["swap"])
            return exchange_rider(u["pb"])
        u["tot"] = total_sum(u["own"], u["exchange"])
        return share_rider(u["tot"])

    def _stage_landed(self, key, stage, buf):
        u = self.units[key]
        u[stage] = buf
        if stage == "share":
            tot = u["tot"]
            self.reduced[key] = jnp.where(self.ci == 0, jnp.concatenate([tot, buf]), jnp.concatenate([buf, tot]))

    def finish(self):
        key = (0, "B")
        for stage in ("swap", "exchange", "share"):
            self._stage_landed(key, stage, _run_alone(self._stage(key, stage), stage + "_last")[0])
        out = {}
        for n in MATRICES:
            per_layer = []
            for l in range(DEPTH):
                if n == "w_in":
                    per_layer.append(self.reduced[(l, "B")].T)
                    continue
                r, rows = _first_row(GROUP_A, n)
                per_layer.append(self.reduced[(l, "A")][r:r + rows].reshape(self.w[n].shape[1:]))
            out[n] = jnp.stack(per_layer)
        return out


WEIGHTS = ("mix_norm_g", "w_in", "b_in", "sinks", "conv_w", "conv_b", "conv_ln_g", "conv_ln_b", "w_attn_proj",
           "w_conv_proj", "b_conv_proj", "w_out", "mlp_norm_g", "w_mlp1", "w_mlp2", "final_norm_g")


def kernel(x, mix_norm_g, w_in, b_in, sinks, conv_w, conv_b, conv_ln_g, conv_ln_b, w_attn_proj, w_conv_proj, b_conv_proj, w_out, mlp_norm_g, w_mlp1, w_mlp2, final_norm_g, loss_target, m_mix_norm_g, m_w_in, m_b_in, m_sinks, m_conv_w, m_conv_b, m_conv_ln_g, m_conv_ln_b, m_w_attn_proj, m_w_conv_proj, m_b_conv_proj, m_w_out, m_mlp_norm_g, m_w_mlp1, m_w_mlp2, m_final_norm_g, v_mix_norm_g, v_w_in, v_b_in, v_sinks, v_conv_w, v_conv_b, v_conv_ln_g, v_conv_ln_b, v_w_attn_proj, v_w_conv_proj, v_b_conv_proj, v_w_out, v_mlp_norm_g, v_w_mlp1, v_w_mlp2, v_final_norm_g):
    w = dict(zip(WEIGHTS, (mix_norm_g, w_in, b_in, sinks, conv_w, conv_b, conv_ln_g, conv_ln_b, w_attn_proj, w_conv_proj,
                           b_conv_proj, w_out, mlp_norm_g, w_mlp1, w_mlp2, final_norm_g)))
    m = dict(zip(WEIGHTS, (m_mix_norm_g, m_w_in, m_b_in, m_sinks, m_conv_w, m_conv_b, m_conv_ln_g, m_conv_ln_b, m_w_attn_proj,
                           m_w_conv_proj, m_b_conv_proj, m_w_out, m_mlp_norm_g, m_w_mlp1, m_w_mlp2, m_final_norm_g)))
    v = dict(zip(WEIGHTS, (v_mix_norm_g, v_w_in, v_b_in, v_sinks, v_conv_w, v_conv_b, v_conv_ln_g, v_conv_ln_b, v_w_attn_proj,
                           v_w_conv_proj, v_b_conv_proj, v_w_out, v_mlp_norm_g, v_w_mlp1, v_w_mlp2, v_final_norm_g)))
    xi, yi, ci = lax.axis_index("x"), lax.axis_index("y"), lax.axis_index("c")
    chip = 2 * xi + yi

    hooks = _Exchange(w, ci, chip)
    loss, dx, gsm = forward_backward(x[0], loss_target[0], hooks)
    loss = lax.psum(loss[0, 0], ("x", "y", "c"))
    grads = hooks.finish()

    gsmall, delta, new_m, new_v = adamw_small(sum_small(gsm), w, m, v)
    grads.update(gsmall)
    for n in MATRICES:
        t = (lambda a: jnp.swapaxes(a, 1, 2)) if n == "w_in" else (lambda a: a)
        delta[n], new_m[n], new_v[n] = map(t, adamw(t(w[n]), t(grads[n]), t(m[n]), t(v[n]), name="adamw_" + n))

    return (loss, dx[None], *[grads[n] for n in WEIGHTS], *[delta[n] for n in WEIGHTS],
            *[new_m[n] for n in WEIGHTS], *[new_v[n] for n in WEIGHTS])
```

```python
import functools
import math
from typing import Callable, NamedTuple, Optional

import jax
import jax.numpy as jnp
import numpy as np
from jax import lax
from jax.experimental import pallas as pl
from jax.experimental.pallas import tpu as pltpu

F32 = jnp.float32
CDT = jnp.bfloat16

D = 1024
DEPTH = 2
N_Q = 8
HEAD_DIM = 64
ATTN_W = 512
KV_W = 128
BLOCK = 128
CONV_C = 512
CONV_K = 31
D_FF = 4096
IN_W = 3840
QKV_W = ATTN_W + 2 * KV_W
REST_W = IN_W - QKV_W
EPS = 1e-6
NEG = -1e30
SCALE = 1.0 / math.sqrt(HEAD_DIM)
SLOPES = [float(2.0 ** (-8.0 * (h + 1) / N_Q)) for h in range(N_Q)]
SUBLANES = 8
HALO = 32

ADAM_LR = 0.001
ADAM_B1 = 0.9
ADAM_B2 = 0.999
ADAM_EPS = 1e-08
ADAM_WD = 0.01
ADAM_STEP = 10

VMEM_LIMIT = 56 * 1024 * 1024


def _cp(*sem):
    return pltpu.CompilerParams(dimension_semantics=sem, vmem_limit_bytes=VMEM_LIMIT)


def _dot(a, b):
    return jnp.dot(a, b, preferred_element_type=F32)


def _dot_nt(a, b):
    return lax.dot_general(a, b, (((1,), (1,)), ((), ())), preferred_element_type=F32)


def _dot_tn(a, b):
    return lax.dot_general(a, b, (((0,), (0,)), ((), ())), preferred_element_type=F32)


def _sig(x):
    return 1.0 / (1.0 + jnp.exp(-x))


def _colsum(v):
    return jnp.sum(v, axis=0, keepdims=True)


def _const(shape, buffers=None):
    mode = {} if buffers is None else {"pipeline_mode": pl.Buffered(buffers)}
    return pl.BlockSpec(shape, lambda *_: (0,) * len(shape), **mode)


class Rider(NamedTuple):
    ins: tuple
    outs: tuple
    n_sems: int
    start: Callable
    finish: Callable
    late: Optional[Callable] = None


def _any():
    return pl.BlockSpec(memory_space=pl.ANY)


def _run(body, rider, *, name, grid, in_specs, out_specs, out_shape, args, sem, scratch_shapes=()):
    if rider is None:
        return pl.pallas_call(body, name=name, grid=grid, in_specs=list(in_specs), out_specs=list(out_specs),
                              out_shape=list(out_shape), scratch_shapes=list(scratch_shapes),
                              compiler_params=_cp(*sem))(*args)
    n_in, n_out, n_sc = len(in_specs), len(out_specs), len(scratch_shapes)
    r_in, r_out = len(rider.ins), len(rider.outs)

    def riding(*refs):
        ins, rins = refs[:n_in], refs[n_in:n_in + r_in]
        o0 = n_in + r_in
        outs, routs = refs[o0:o0 + n_out], refs[o0 + n_out:o0 + n_out + r_out]
        s0 = o0 + n_out + r_out
        scratch, sems = refs[s0:s0 + n_sc], refs[s0 + n_sc]
        first = functools.reduce(jnp.logical_and, [pl.program_id(a) == 0 for a in range(len(grid))])
        last = functools.reduce(jnp.logical_and, [pl.program_id(a) == grid[a] - 1 for a in range(len(grid))])

        @pl.when(first)
        def _():
            rider.start(rins, routs, sems)

        if rider.late is not None:
            late_step = grid[0] - max(1, grid[0] // 8)
            others = [pl.program_id(a) == 0 for a in range(1, len(grid))]

            @pl.when(functools.reduce(jnp.logical_and, others, pl.program_id(0) == late_step))
            def _():
                rider.late(rins, routs, sems)

        body(*ins, *outs, *scratch)

        @pl.when(last)
        def _():
            rider.finish(rins, routs, sems)

    res = pl.pallas_call(
        riding, name=name, grid=grid, in_specs=list(in_specs) + [_any()] * r_in,
        out_specs=list(out_specs) + [_any()] * r_out, out_shape=list(out_shape) + list(rider.outs),
        scratch_shapes=list(scratch_shapes) + [pltpu.SemaphoreType.DMA((rider.n_sems,))],
        compiler_params=_cp(*["arbitrary"] * len(grid)))(*args, *rider.ins)
    return res[:n_out], res[n_out:]


def _run_alone(rider, name):
    def body(*refs):
        r_in, r_out = len(rider.ins), len(rider.outs)
        rins, routs, sems = refs[:r_in], refs[r_in:r_in + r_out], refs[r_in + r_out]
        rider.start(rins, routs, sems)
        if rider.late is not None:
            rider.late(rins, routs, sems)
        rider.finish(rins, routs, sems)

    return pl.pallas_call(
        body, name=name, in_specs=[_any()] * len(rider.ins), out_specs=[_any()] * len(rider.outs),
        out_shape=list(rider.outs), scratch_shapes=[pltpu.SemaphoreType.DMA((rider.n_sems,))])(*rider.ins)


def rms_inproj(x, g, wt, b, *, tm=512, rider=None):
    T = x.shape[0]

    def body(x_ref, g_ref, w_ref, b_ref, qkv_ref, rest_ref):
        xv = x_ref[...]
        r = lax.rsqrt(jnp.mean(xv * xv, axis=-1, keepdims=True) + EPS)
        h = (xv * r * g_ref[...]).astype(CDT)
        qkv_ref[...] = (_dot_nt(h, w_ref[0:QKV_W, :]) + b_ref[:, 0:QKV_W]).astype(qkv_ref.dtype)
        for j in range(REST_W // D):
            c0 = QKV_W + D * j
            rest_ref[:, D * j:D * (j + 1)] = _dot_nt(h, w_ref[c0:c0 + D, :]) + b_ref[:, c0:c0 + D]

    return _run(
        body, rider, name="rms_inproj", grid=(T // tm,),
        in_specs=[pl.BlockSpec((tm, D), lambda i: (i, 0)), _const((1, D)), _const((IN_W, D), 1), _const((1, IN_W))],
        out_specs=[pl.BlockSpec((tm, QKV_W), lambda i: (i, 0)), pl.BlockSpec((tm, REST_W), lambda i: (i, 0))],
        out_shape=[jax.ShapeDtypeStruct((T, QKV_W), CDT), jax.ShapeDtypeStruct((T, REST_W), F32)],
        sem=("parallel",), args=(x, g, wt, b))


def _lane_halves(shape):
    lane = lax.broadcasted_iota(jnp.int32, shape, 1)
    return lane < HEAD_DIM, lane >= HEAD_DIM


def _swap_halves(v):
    return pltpu.roll(v.astype(F32), HEAD_DIM, axis=1).astype(v.dtype)


N_KV = KV_W // HEAD_DIM
GROUP = N_Q // N_KV
STACK = GROUP * BLOCK


def _score_bias():
    row = np.arange(STACK)[:, None] % BLOCK
    col = np.arange(2 * BLOCK)[None, :]
    dist = row + BLOCK - col
    window = (dist >= 0) & (dist < BLOCK)
    slopes = np.asarray(SLOPES, np.float32).reshape(N_KV, GROUP)
    out = np.empty((2, N_KV, STACK, 2 * BLOCK), np.float32)
    for first in range(2):
        valid = window & ((col >= BLOCK) | (first == 0))
        for g in range(N_KV):
            slope = np.repeat(slopes[g], BLOCK)[:, None]
            out[first, g] = np.where(valid, -(slope * dist.astype(np.float32)), np.float32(NEG))
    return jnp.asarray(out)


def _per_head_column(vals):
    row = lax.broadcasted_iota(jnp.int32, (STACK, 1), 0)
    col = jnp.full((STACK, 1), vals[GROUP - 1], F32)
    for i in reversed(range(GROUP - 1)):
        col = jnp.where(row < (i + 1) * BLOCK, vals[i], col)
    return col


def _stack_heads(dst, src_ref, r0, g, scale=None):
    lane = lax.broadcasted_iota(jnp.int32, (BLOCK, 2 * HEAD_DIM), 1)
    keep = (lane >= HEAD_DIM) if g else (lane < HEAD_DIM)
    for i in range(GROUP):
        h = GROUP * g + i
        tile = src_ref[pl.ds(r0, BLOCK), (h // 2) * 128:(h // 2 + 1) * 128]
        if h % 2 != g:
            tile = _swap_halves(tile)
        if scale is not None:
            tile = tile * jnp.asarray(scale, tile.dtype)
        dst[i * BLOCK:(i + 1) * BLOCK, :] = jnp.where(keep, tile, jnp.zeros_like(tile))


def _unstack_heads(dst_ref, stacked, r0, g):
    lane = lax.broadcasted_iota(jnp.int32, (BLOCK, 2 * HEAD_DIM), 1)
    tiles = []
    for j in range(GROUP // 2):
        even = stacked[(2 * j) * BLOCK:(2 * j + 1) * BLOCK, :]
        odd = stacked[(2 * j + 1) * BLOCK:(2 * j + 2) * BLOCK, :]
        lo = _swap_halves(even) if g else even
        hi = odd if g else _swap_halves(odd)
        c0 = ((GROUP * g) // 2 + j) * 128
        tile = jnp.where(lane < HEAD_DIM, lo, hi)
        dst_ref[pl.ds(r0, BLOCK), c0:c0 + 128] = tile.astype(dst_ref.dtype)
        tiles.append((c0, tile))
    return tiles


def _qkv_specs(tq):
    nb = tq // BLOCK
    return [
        pl.BlockSpec((tq, ATTN_W), lambda i: (i, 0)),
        pl.BlockSpec((BLOCK, KV_W), lambda i: (jnp.maximum(i * nb - 1, 0), ATTN_W // KV_W)),
        pl.BlockSpec((tq, KV_W), lambda i: (i, ATTN_W // KV_W)),
        pl.BlockSpec((BLOCK, KV_W), lambda i: (jnp.maximum(i * nb - 1, 0), ATTN_W // KV_W + 1)),
        pl.BlockSpec((tq, KV_W), lambda i: (i, ATTN_W // KV_W + 1)),
    ]


def attn_fwd(qkv, sinks, *, tq=512):
    T = qkv.shape[0]
    nb = tq // BLOCK

    def body(sink_ref, bias_ref, q_ref, kp_ref, kc_ref, vp_ref, vc_ref, o_ref, lse_ref, kext, vext, qs):
        i = pl.program_id(0)
        kext[0:BLOCK, :] = kp_ref[...]
        kext[BLOCK:, :] = kc_ref[...]
        vext[0:BLOCK, :] = vp_ref[...]
        vext[BLOCK:, :] = vc_ref[...]
        lane_l = lax.broadcasted_iota(jnp.int32, (BLOCK, 128), 1)

        def blk(b, carry):
            r0 = pl.multiple_of(b * BLOCK, BLOCK)
            first = jnp.logical_and(i == 0, b == 0).astype(jnp.int32)
            kc = kext[pl.ds(r0, 2 * BLOCK), :]
            vc = vext[pl.ds(r0, 2 * BLOCK), :]
            lse_t = jnp.zeros((BLOCK, 128), F32)
            for g in range(N_KV):
                heads = range(GROUP * g, GROUP * (g + 1))
                _stack_heads(qs.at[g], q_ref, r0, g, SCALE)
                s = _dot_nt(qs[g], kc) + bias_ref[first, g]
                sink = _per_head_column([sink_ref[h] for h in heads])
                m = jnp.maximum(jnp.max(s, axis=-1, keepdims=True), sink)
                p = jnp.exp(s - m)
                denom = jnp.sum(p, axis=-1, keepdims=True) + jnp.exp(sink - m)
                p = p / denom
                _unstack_heads(o_ref, _dot(p.astype(CDT), vc), r0, g)
                lse = m + jnp.log(denom)
                for i_h, h in enumerate(heads):
                    lse_t = jnp.where(lane_l == h, lse[i_h * BLOCK:(i_h + 1) * BLOCK, :], lse_t)
            lse_ref[pl.ds(r0, BLOCK), :] = lse_t
            return carry

        lax.fori_loop(0, nb, blk, 0)

    return pl.pallas_call(
        body, name="attn_fwd", grid=(T // tq,),
        in_specs=[pl.BlockSpec(memory_space=pltpu.SMEM), _const((2, N_KV, STACK, 2 * BLOCK), 1)] + _qkv_specs(tq),
        out_specs=[pl.BlockSpec((tq, ATTN_W), lambda i: (i, 0)), pl.BlockSpec((tq, 128), lambda i: (i, 0))],
        out_shape=[jax.ShapeDtypeStruct((T, ATTN_W), CDT), jax.ShapeDtypeStruct((T, 128), F32)],
        scratch_shapes=[pltpu.VMEM((tq + BLOCK, KV_W), CDT), pltpu.VMEM((tq + BLOCK, KV_W), CDT),
                        pltpu.VMEM((N_KV, STACK, 2 * HEAD_DIM), CDT)],
        compiler_params=_cp("parallel"),
    )(sinks, _score_bias(), qkv, qkv, qkv, qkv, qkv)


def _halo_before(tm, width, col):
    return pl.BlockSpec((HALO, width), lambda i: (jnp.maximum(i * (tm // HALO) - 1, 0), col))


def _fill_u0(ext, a_ref, b_ref, ha_ref, hb_ref, first):
    hu = ha_ref[...] * _sig(hb_ref[...])
    ext[0:HALO, :] = jnp.where(first, jnp.zeros_like(hu), hu)
    ext[HALO:, :] = a_ref[...] * _sig(b_ref[...])


def _shifted_taps(src, w_ref, base, rc, offsets):
    acc = jnp.zeros((rc, CONV_C), F32)
    for b in range(SUBLANES):
        taps = [(k, o - b) for k, o in enumerate(offsets) if o % SUBLANES == b]
        if not taps:
            continue
        rows = rc if b == 0 else rc + SUBLANES
        part = jnp.zeros((rows, CONV_C), F32)
        for k, o8 in taps:
            part = part + w_ref[k:k + 1, :] * src[base + o8:base + o8 + rows, :]
        acc = acc + (part if b == 0 else part[b:b + rc, :])
    return acc


def _shift_copies(dst, src, rows):
    for b in range(1, SUBLANES):
        for r in range(0, rows, 64):
            n = min(64, rows - r)
            dst[b - 1, r:r + n, :] = src[r + b:r + b + n, :]


def _window(src, copies, off, r0, n):
    b = off % SUBLANES
    a = r0 + off - b
    return src[a:a + n, :] if b == 0 else copies[b - 1, a:a + n, :]


def _conv_rows(ext, w_ref, r0, rc):
    return _shifted_taps(ext, w_ref, r0, rc, [HALO - (CONV_K - 1) + k for k in range(CONV_K)])


def _layer_norm(u1, g, b):
    mu = jnp.mean(u1, axis=-1, keepdims=True)
    xc = u1 - mu
    rstd = lax.rsqrt(jnp.mean(xc * xc, axis=-1, keepdims=True) + EPS)
    n = xc * rstd
    return n, rstd, n * g + b


CONV_RC = 32
CONV_RC_1PASS = 64


def conv_fwd(rest, cw, cb, lg, lb, *, tm=512):
    T = rest.shape[0]

    def body(a_ref, b_ref, ha_ref, hb_ref, w_ref, cb_ref, lg_ref, lb_ref, o_ref, u1_ref, ext):
        _fill_u0(ext, a_ref, b_ref, ha_ref, hb_ref, pl.program_id(0) == 0)
        rc = CONV_RC_1PASS
        for r0 in range(0, tm, rc):
            u1 = _conv_rows(ext, w_ref, r0, rc) + cb_ref[...]
            u1_ref[r0:r0 + rc, :] = u1
            _, _, u2 = _layer_norm(u1, lg_ref[...], lb_ref[...])
            o_ref[r0:r0 + rc, :] = (u2 * _sig(u2)).astype(o_ref.dtype)

    row = lambda i: (i, 0)
    return pl.pallas_call(
        body, name="conv_fwd", grid=(T // tm,),
        in_specs=[pl.BlockSpec((tm, CONV_C), row), pl.BlockSpec((tm, CONV_C), lambda i: (i, 1)),
                  _halo_before(tm, CONV_C, 0), _halo_before(tm, CONV_C, 1),
                  _const((CONV_K, CONV_C)), _const((1, CONV_C)), _const((1, CONV_C)), _const((1, CONV_C))],
        out_specs=[pl.BlockSpec((tm, CONV_C), row), pl.BlockSpec((tm, CONV_C), row)],
        out_shape=[jax.ShapeDtypeStruct((T, CONV_C), CDT), jax.ShapeDtypeStruct((T, CONV_C), F32)],
        scratch_shapes=[pltpu.VMEM((tm + HALO, CONV_C), F32)],
        compiler_params=_cp("parallel"),
    )(rest, rest, rest, rest, cw, cb, lg, lb)


def merge_out(x, attn, u3, rest, wa, wc, bc, wo, *, tm=512):
    T = x.shape[0]

    def body(x_ref, at_ref, u_ref, ga_ref, gc_ref, wa_ref, wc_ref, bc_ref, wo_ref, o_ref):
        br_a = _dot(at_ref[...], wa_ref[...])
        br_c = _dot(u_ref[...], wc_ref[...]) + bc_ref[...]
        merged = _sig(ga_ref[...]) * br_a + _sig(gc_ref[...]) * br_c
        o_ref[...] = x_ref[...] + _dot(merged.astype(CDT), wo_ref[...])

    return pl.pallas_call(
        body, name="merge_out", grid=(T // tm,),
        in_specs=[pl.BlockSpec((tm, D), lambda i: (i, 0)), pl.BlockSpec((tm, ATTN_W), lambda i: (i, 0)),
                  pl.BlockSpec((tm, CONV_C), lambda i: (i, 0)),
                  pl.BlockSpec((tm, D), lambda i: (i, 1)), pl.BlockSpec((tm, D), lambda i: (i, 2)),
                  _const((ATTN_W, D), 1), _const((CONV_C, D), 1), _const((1, D)), _const((D, D), 1)],
        out_specs=pl.BlockSpec((tm, D), lambda i: (i, 0)),
        out_shape=jax.ShapeDtypeStruct((T, D), F32),
        compiler_params=_cp("parallel"),
    )(x, attn, u3, rest, rest, wa, wc, bc, wo)


def _loss_and_grad(xv, gv, tgt):
    r = lax.rsqrt(jnp.mean(xv * xv, axis=-1, keepdims=True) + EPS)
    e = xv * r * gv - tgt
    dx, dg_rows = _rms_bwd(xv, gv, e * (1.0 / D))
    return 0.5 * jnp.mean(e * e, axis=-1, keepdims=True), dx, dg_rows


def mlp_fwd(x, g, w1, w2, *, head=None, tm=256, tf=D_FF, rider=None):
    T = x.shape[0]
    nf = D_FF // tf

    def body(x_ref, g_ref, w1_ref, w2_ref, *rest):
        if head is None:
            o_ref, pre_ref, h_s, acc_s = rest
        else:
            gf_ref, t_ref, pre_ref, dy_ref, dgf_ref, loss_ref, h_s, acc_s = rest
        i, f = pl.program_id(0), pl.program_id(1)

        @pl.when(f == 0)
        def _():
            xv = x_ref[...]
            r = lax.rsqrt(jnp.mean(xv * xv, axis=-1, keepdims=True) + EPS)
            h_s[...] = (xv * r * g_ref[...]).astype(CDT)
            acc_s[...] = jnp.zeros_like(acc_s)

        pre = _dot(h_s[...], w1_ref[...])
        pre_ref[...] = pre
        a = jnp.square(jnp.maximum(pre, 0.0))
        acc_s[...] += _dot(a.astype(CDT), w2_ref[...])

        @pl.when(f == nf - 1)
        def _():
            y = x_ref[...] + acc_s[...]
            if head is None:
                o_ref[...] = y
                return

            @pl.when(i == 0)
            def _():
                dgf_ref[...] = jnp.zeros_like(dgf_ref)
                loss_ref[...] = jnp.zeros_like(loss_ref)

            loss_rows, dy, dg_rows = _loss_and_grad(y, gf_ref[...], t_ref[...])
            dy_ref[...] = dy
            dgf_ref[...] += _colsum(dg_rows)
            loss_ref[...] += _colsum(loss_rows)

    mode = {"pipeline_mode": pl.Buffered(1)} if nf == 1 else {}
    row = pl.BlockSpec((tm, D), lambda i, f: (i, 0))
    pre_spec, pre_shape = pl.BlockSpec((tm, tf), lambda i, f: (i, f)), jax.ShapeDtypeStruct((T, D_FF), F32)
    in_specs = [row, _const((1, D)), pl.BlockSpec((D, tf), lambda i, f: (0, f), **mode),
                pl.BlockSpec((tf, D), lambda i, f: (f, 0), **mode)]
    scratch = [pltpu.VMEM((tm, D), CDT), pltpu.VMEM((tm, D), F32)]
    if head is None:
        return _run(body, rider, name="mlp_fwd", grid=(T // tm, nf), in_specs=in_specs, out_specs=[row, pre_spec],
                    out_shape=[jax.ShapeDtypeStruct((T, D), F32), pre_shape], scratch_shapes=scratch,
                    sem=("parallel", "arbitrary"), args=(x, g, w1, w2))
    return _run(body, rider, name="mlp_fwd_loss", grid=(T // tm, nf), in_specs=in_specs + [_const((1, D)), row],
                out_specs=[pre_spec, row, _const((1, D)), _const((1, 128))],
                out_shape=[pre_shape, jax.ShapeDtypeStruct((T, D), F32), jax.ShapeDtypeStruct((1, D), F32),
                           jax.ShapeDtypeStruct((1, 128), F32)],
                scratch_shapes=scratch, sem=("arbitrary", "arbitrary"), args=(x, g, w1, w2) + tuple(head))


def _rms_bwd(xv, g, dh):
    r = lax.rsqrt(jnp.mean(xv * xv, axis=-1, keepdims=True) + EPS)
    xhat = xv * r
    dxh = dh * g
    dx = r * (dxh - xhat * jnp.mean(dxh * xhat, axis=-1, keepdims=True))
    return dx, dh * xhat


def mlp_bwd(dy, x, g, pre, w1, w2, *, tm=256, tf=D_FF, rider=None):
    T = x.shape[0]
    nf = D_FF // tf

    def body(dy_ref, x_ref, g_ref, pre_ref, w1_ref, w2_ref, dx_ref, dg_ref, h_ref, a_ref, dpre_ref, dyb_ref, dyb_s, acc_s):
        i, f = pl.program_id(0), pl.program_id(1)

        @pl.when(jnp.logical_and(i == 0, f == 0))
        def _():
            dg_ref[...] = jnp.zeros_like(dg_ref)

        @pl.when(f == 0)
        def _():
            dyb_s[...] = dy_ref[...].astype(CDT)
            dyb_ref[...] = dyb_s[...]
            acc_s[...] = jnp.zeros_like(acc_s)

        pre = pre_ref[...]
        rl = jnp.maximum(pre, 0.0)
        a_ref[...] = (rl * rl).astype(CDT)
        da = _dot_nt(dyb_s[...], w2_ref[...])
        dpre = (da * (2.0 * rl)).astype(CDT)
        dpre_ref[...] = dpre
        acc_s[...] += _dot_nt(dpre, w1_ref[...])

        @pl.when(f == nf - 1)
        def _():
            xv = x_ref[...]
            gv = g_ref[...]
            dxn, dg_rows = _rms_bwd(xv, gv, acc_s[...])
            dx_ref[...] = dy_ref[...] + dxn
            dg_ref[...] += _colsum(dg_rows)
            r = lax.rsqrt(jnp.mean(xv * xv, axis=-1, keepdims=True) + EPS)
            h_ref[...] = (xv * r * gv).astype(CDT)

    row = lambda i, f: (i, 0)
    mode = {"pipeline_mode": pl.Buffered(1)} if nf == 1 else {}
    return _run(
        body, rider, name="mlp_bwd", grid=(T // tm, nf),
        in_specs=[pl.BlockSpec((tm, D), row), pl.BlockSpec((tm, D), row), _const((1, D)),
                  pl.BlockSpec((tm, tf), lambda i, f: (i, f)),
                  pl.BlockSpec((D, tf), lambda i, f: (0, f), **mode), pl.BlockSpec((tf, D), lambda i, f: (f, 0), **mode)],
        out_specs=[pl.BlockSpec((tm, D), row), _const((1, D)), pl.BlockSpec((tm, D), row),
                   pl.BlockSpec((tm, tf), lambda i, f: (i, f)), pl.BlockSpec((tm, tf), lambda i, f: (i, f)),
                   pl.BlockSpec((tm, D), row)],
        out_shape=[jax.ShapeDtypeStruct((T, D), F32), jax.ShapeDtypeStruct((1, D), F32),
                   jax.ShapeDtypeStruct((T, D), CDT), jax.ShapeDtypeStruct((T, D_FF), CDT),
                   jax.ShapeDtypeStruct((T, D_FF), CDT), jax.ShapeDtypeStruct((T, D), CDT)],
        scratch_shapes=[pltpu.VMEM((tm, D), CDT), pltpu.VMEM((tm, D), F32)],
        sem=("arbitrary", "arbitrary"), args=(dy, x, g, pre, w1, w2))


def tn_matmul(a, b, *, tm, tn, tk=2048, name, by_chip=False, rider=None):
    T, M = a.shape
    N = b.shape[1]
    tk = min(tk, T)
    nk = T // tk

    def body(a_ref, b_ref, o_ref):
        @pl.when(pl.program_id(2) == 0)
        def _():
            o_ref[...] = jnp.zeros_like(o_ref)

        o_ref[...] += _dot_tn(a_ref[...].astype(CDT), b_ref[...].astype(CDT))

    if by_chip:
        out_spec = pl.BlockSpec((None, tm, tn), lambda i, j, k: (j, i, 0))
        out_shape = jax.ShapeDtypeStruct((N // tn, M, tn), F32)
    else:
        out_spec = pl.BlockSpec((tm, tn), lambda i, j, k: (i, j))
        out_shape = jax.ShapeDtypeStruct((M, N), F32)
    res = _run(
        body, rider, name=name, grid=(M // tm, N // tn, nk),
        in_specs=[pl.BlockSpec((tk, tm), lambda i, j, k: (k, i)), pl.BlockSpec((tk, tn), lambda i, j, k: (k, j))],
        out_specs=[out_spec], out_shape=[out_shape], sem=("parallel", "parallel", "arbitrary"), args=(a, b))
    return res[0] if rider is None else (res[0][0], res[1])


def merge_bwd(dx1, attn, u3, rest, wa, wc, bc, wo, *, tm=512):
    T = dx1.shape[0]

    def body(dx_ref, at_ref, u_ref, ga_ref, gc_ref, wa_ref, wc_ref, bc_ref, wo_ref,
             mg_ref, dba_ref, dbc_ref, dat_ref, du_ref, dgate_ref, dgsum_ref, dbias_ref):
        @pl.when(pl.program_id(0) == 0)
        def _():
            dbias_ref[...] = jnp.zeros_like(dbias_ref)
            dgsum_ref[...] = jnp.zeros_like(dgsum_ref)

        br_a = _dot(at_ref[...], wa_ref[...])
        br_c = _dot(u_ref[...], wc_ref[...]) + bc_ref[...]
        sa = _sig(ga_ref[...])
        sc = _sig(gc_ref[...])
        mg_ref[...] = (sa * br_a + sc * br_c).astype(CDT)
        dm = _dot_nt(dx_ref[...].astype(CDT), wo_ref[...])
        dba = dm * sa
        dbc = dm * sc
        dga = dm * br_a * sa * (1.0 - sa)
        dgc = dm * br_c * sc * (1.0 - sc)
        dgate_ref[:, 0:D] = dga.astype(CDT)
        dgate_ref[:, D:2 * D] = dgc.astype(CDT)
        dgsum_ref[:, 0:D] += _colsum(dga)
        dgsum_ref[:, D:2 * D] += _colsum(dgc)
        dbias_ref[...] += _colsum(dbc)
        dba_b = dba.astype(CDT)
        dbc_b = dbc.astype(CDT)
        dba_ref[...] = dba_b
        dbc_ref[...] = dbc_b
        dat_ref[...] = _dot_nt(dba_b, wa_ref[...]).astype(CDT)
        du_ref[...] = _dot_nt(dbc_b, wc_ref[...])

    row = lambda i: (i, 0)
    return pl.pallas_call(
        body, name="merge_bwd", grid=(T // tm,),
        in_specs=[pl.BlockSpec((tm, D), row), pl.BlockSpec((tm, ATTN_W), row), pl.BlockSpec((tm, CONV_C), row),
                  pl.BlockSpec((tm, D), lambda i: (i, 1)), pl.BlockSpec((tm, D), lambda i: (i, 2)),
                  _const((ATTN_W, D), 1), _const((CONV_C, D), 1), _const((1, D)), _const((D, D), 1)],
        out_specs=[pl.BlockSpec((tm, D), row), pl.BlockSpec((tm, D), row), pl.BlockSpec((tm, D), row),
                   pl.BlockSpec((tm, ATTN_W), row), pl.BlockSpec((tm, CONV_C), row),
                   pl.BlockSpec((tm, 2 * D), row), _const((1, 2 * D)), _const((1, D))],
        out_shape=[jax.ShapeDtypeStruct((T, D), CDT), jax.ShapeDtypeStruct((T, D), CDT),
                   jax.ShapeDtypeStruct((T, D), CDT), jax.ShapeDtypeStruct((T, ATTN_W), CDT),
                   jax.ShapeDtypeStruct((T, CONV_C), F32), jax.ShapeDtypeStruct((T, 2 * D), CDT),
                   jax.ShapeDtypeStruct((1, 2 * D), F32), jax.ShapeDtypeStruct((1, D), F32)],
        compiler_params=_cp("arbitrary"),
    )(dx1, attn, u3, rest, rest, wa, wc, bc, wo)


def conv_bwd_ln(du3, u1, lg, lb, *, tm=512, rider=None):
    T = du3.shape[0]

    def body(du_ref, u1_ref, lg_ref, lb_ref, du1_ref, dlg_ref, dlb_ref, dcb_ref):
        @pl.when(pl.program_id(0) == 0)
        def _():
            dlg_ref[...] = jnp.zeros_like(dlg_ref)
            dlb_ref[...] = jnp.zeros_like(dlb_ref)
            dcb_ref[...] = jnp.zeros_like(dcb_ref)

        dlg = jnp.zeros((1, CONV_C), F32)
        dlb = jnp.zeros((1, CONV_C), F32)
        dcb = jnp.zeros((1, CONV_C), F32)
        rc = CONV_RC_1PASS
        for r0 in range(0, tm, rc):
            n, rstd, u2 = _layer_norm(u1_ref[r0:r0 + rc, :], lg_ref[...], lb_ref[...])
            s = _sig(u2)
            du2 = du_ref[r0:r0 + rc, :] * (s + u2 * s * (1.0 - s))
            dn = du2 * lg_ref[...]
            du1 = rstd * (dn - jnp.mean(dn, axis=-1, keepdims=True) - n * jnp.mean(dn * n, axis=-1, keepdims=True))
            du1_ref[r0:r0 + rc, :] = du1
            dlg = dlg + _colsum(du2 * n)
            dlb = dlb + _colsum(du2)
            dcb = dcb + _colsum(du1)
        dlg_ref[...] += dlg
        dlb_ref[...] += dlb
        dcb_ref[...] += dcb

    row = lambda i: (i, 0)
    vec = jax.ShapeDtypeStruct((1, CONV_C), F32)
    return _run(
        body, rider, name="conv_bwd_ln", grid=(T // tm,),
        in_specs=[pl.BlockSpec((tm, CONV_C), row), pl.BlockSpec((tm, CONV_C), row), _const((1, CONV_C)), _const((1, CONV_C))],
        out_specs=[pl.BlockSpec((tm, CONV_C), row), _const((1, CONV_C)), _const((1, CONV_C)), _const((1, CONV_C))],
        out_shape=[jax.ShapeDtypeStruct((T, CONV_C), F32), vec, vec, vec],
        sem=("arbitrary",), args=(du3, u1, lg, lb))


def conv_bwd_taps(du1, rest, cw, *, tm=512):
    T = du1.shape[0]
    nt = T // tm

    def body(d_ref, hd_ref, a_ref, b_ref, ha_ref, hb_ref, w_ref, dglu_ref, dgsum_ref, dw_ref, ext, dext, dcopies, dwacc):
        i = pl.program_id(0)

        @pl.when(i == 0)
        def _():
            dwacc[...] = jnp.zeros_like(dwacc)
            dgsum_ref[...] = jnp.zeros_like(dgsum_ref)

        rc = CONV_RC
        groups = lambda v: jnp.sum(v.reshape(v.shape[0] // SUBLANES, SUBLANES, CONV_C), axis=0)
        sums = [jnp.zeros((SUBLANES, CONV_C), F32), jnp.zeros((SUBLANES, CONV_C), F32)]
        _fill_u0(ext, a_ref, b_ref, ha_ref, hb_ref, i == 0)
        dext[0:SUBLANES, :] = jnp.zeros((SUBLANES, CONV_C), F32)
        dext[SUBLANES:SUBLANES + tm, :] = d_ref[...]
        hd = hd_ref[...]
        dext[SUBLANES + tm:, :] = jnp.where(i == nt - 1, jnp.zeros_like(hd), hd)
        _shift_copies(dcopies, dext, tm + HALO)
        for r0 in range(0, tm, rc):
            du0 = jnp.zeros((rc, CONV_C), F32)
            for k in range(CONV_K):
                du0 = du0 + w_ref[k:k + 1, :] * _window(dext, dcopies, SUBLANES + CONV_K - 1 - k, r0, rc)
            av = a_ref[r0:r0 + rc, :]
            sb = _sig(b_ref[r0:r0 + rc, :])
            for half, dg in enumerate((du0 * sb, du0 * av * sb * (1.0 - sb))):
                dglu_ref[r0:r0 + rc, half * CONV_C:(half + 1) * CONV_C] = dg.astype(CDT)
                sums[half] = sums[half] + groups(dg)
        for half in range(2):
            dgsum_ref[:, half * CONV_C:(half + 1) * CONV_C] += _colsum(sums[half])
        tail = lax.broadcasted_iota(jnp.int32, (SUBLANES, CONV_C), 0)
        for b in range(SUBLANES):
            taps = [(k, HALO - (CONV_K - 1) + k - b) for k in range(CONV_K) if (HALO - (CONV_K - 1) + k) % SUBLANES == b]
            accs = [jnp.zeros((SUBLANES, CONV_C), F32) for _ in taps]
            for r0 in list(range(0, tm, rc)) + ([tm] if b else []):
                n = rc if r0 < tm else SUBLANES
                dwin = _window(dext, dcopies, SUBLANES - b, r0, n)
                if r0 == tm:
                    dwin = jnp.where(tail < b, dwin, 0.0)
                for j, (k, o8) in enumerate(taps):
                    accs[j] = accs[j] + groups(dwin * ext[r0 + o8:r0 + o8 + n, :])
            for j, (k, _) in enumerate(taps):
                dwacc[8 * k:8 * k + 8, :] += accs[j]

        @pl.when(i == nt - 1)
        def _():
            dw_ref[...] = jnp.zeros_like(dw_ref)
            for k in range(CONV_K):
                dw_ref[k:k + 1, :] = _colsum(dwacc[8 * k:8 * k + 8, :])

    row = lambda i: (i, 0)
    return pl.pallas_call(
        body, name="conv_bwd_taps", grid=(nt,),
        in_specs=[pl.BlockSpec((tm, CONV_C), row),
                  pl.BlockSpec((HALO, CONV_C), lambda i: (jnp.minimum((i + 1) * (tm // HALO), T // HALO - 1), 0)),
                  pl.BlockSpec((tm, CONV_C), row), pl.BlockSpec((tm, CONV_C), lambda i: (i, 1)),
                  _halo_before(tm, CONV_C, 0), _halo_before(tm, CONV_C, 1), _const((CONV_K, CONV_C))],
        out_specs=[pl.BlockSpec((tm, 2 * CONV_C), row), _const((1, 2 * CONV_C)), _const((HALO, CONV_C))],
        out_shape=[jax.ShapeDtypeStruct((T, 2 * CONV_C), CDT), jax.ShapeDtypeStruct((1, 2 * CONV_C), F32),
                   jax.ShapeDtypeStruct((HALO, CONV_C), F32)],
        scratch_shapes=[pltpu.VMEM((tm + HALO, CONV_C), F32), pltpu.VMEM((SUBLANES + tm + HALO, CONV_C), F32),
                        pltpu.VMEM((SUBLANES - 1, tm + HALO, CONV_C), F32), pltpu.VMEM((8 * CONV_K, CONV_C), F32)],
        compiler_params=_cp("arbitrary"),
    )(du1, du1, rest, rest, rest, rest, cw)


def attn_bwd(qkv, do, lse, sinks, *, tq=512, rider=None):
    T = qkv.shape[0]
    nb = tq // BLOCK

    def body(sink_ref, bias_ref, q_ref, kp_ref, kc_ref, vp_ref, vc_ref, do_ref, lse_ref,
             dq_ref, dqsum_ref, dkv_ref, spill_ref, dsink_ref, kext, vext, dkext, dvext, qs, dos):
        i = pl.program_id(0)

        @pl.when(i == 0)
        def _():
            dsink_ref[...] = jnp.zeros_like(dsink_ref)
            dqsum_ref[...] = jnp.zeros_like(dqsum_ref)

        kext[0:BLOCK, :] = kp_ref[...]
        kext[BLOCK:, :] = kc_ref[...]
        vext[0:BLOCK, :] = vp_ref[...]
        vext[BLOCK:, :] = vc_ref[...]
        dkext[...] = jnp.zeros_like(dkext)
        dvext[...] = jnp.zeros_like(dvext)
        lane_l = lax.broadcasted_iota(jnp.int32, (BLOCK, 128), 1)
        lane_k = lax.broadcasted_iota(jnp.int32, (2 * BLOCK, KV_W), 1)

        def blk(b, dsink):
            r0 = pl.multiple_of(b * BLOCK, BLOCK)
            first = jnp.logical_and(i == 0, b == 0).astype(jnp.int32)
            kc = kext[pl.ds(r0, 2 * BLOCK), :]
            vc = vext[pl.ds(r0, 2 * BLOCK), :]
            lse_t = lse_ref[pl.ds(r0, BLOCK), :]
            dk = jnp.zeros((2 * BLOCK, KV_W), F32)
            dv = jnp.zeros((2 * BLOCK, KV_W), F32)
            for g in range(N_KV):
                heads = range(GROUP * g, GROUP * (g + 1))
                _stack_heads(qs.at[g], q_ref, r0, g, SCALE)
                _stack_heads(dos.at[g], do_ref, r0, g)
                qv = qs[g]
                dov = dos[g]
                s = _dot_nt(qv, kc) + bias_ref[first, g]
                lse = jnp.concatenate(
                    [jnp.sum(jnp.where(lane_l == h, lse_t, 0.0), axis=-1, keepdims=True) for h in heads], axis=0)
                p = jnp.exp(s - lse)
                dp = _dot_nt(dov, vc)
                dd = jnp.sum(p * dp, axis=-1, keepdims=True)
                ds = (p * (dp - dd)).astype(CDT)
                keep = (lane_k >= HEAD_DIM) if g else (lane_k < HEAD_DIM)
                for c0, tile in _unstack_heads(dq_ref, _dot(ds, jnp.where(keep, kc, jnp.zeros_like(kc))) * SCALE, r0, g):
                    dqsum_ref[:, c0:c0 + 128] += _colsum(tile)
                dk = dk + _dot_tn(ds, qv)
                dv = dv + _dot_tn(p.astype(CDT), dov)
                wsink = jnp.exp(_per_head_column([sink_ref[h] for h in heads]) - lse) * dd
                for i_h, h in enumerate(heads):
                    part = jnp.sum(wsink[i_h * BLOCK:(i_h + 1) * BLOCK, :], axis=0, keepdims=True)
                    dsink = dsink - jnp.where(lane_l[0:1, :] == h, part, 0.0)
            dkext[pl.ds(r0, 2 * BLOCK), :] += dk
            dvext[pl.ds(r0, 2 * BLOCK), :] += dv
            return dsink

        dsink_ref[...] += lax.fori_loop(0, nb, blk, jnp.zeros((1, 128), F32))
        dkv_ref[:, 0:KV_W] = dkext[BLOCK:, :]
        dkv_ref[:, KV_W:2 * KV_W] = dvext[BLOCK:, :]
        spill_ref[:, 0:KV_W] = dkext[0:BLOCK, :]
        spill_ref[:, KV_W:2 * KV_W] = dvext[0:BLOCK, :]

    row = lambda i: (i, 0)
    return _run(
        body, rider, name="attn_bwd", grid=(T // tq,),
        in_specs=[pl.BlockSpec(memory_space=pltpu.SMEM), _const((2, N_KV, STACK, 2 * BLOCK), 1)] + _qkv_specs(tq)
        + [pl.BlockSpec((tq, ATTN_W), row), pl.BlockSpec((tq, 128), row)],
        out_specs=[pl.BlockSpec((tq, ATTN_W), row), _const((1, ATTN_W)), pl.BlockSpec((tq, 2 * KV_W), row),
                   pl.BlockSpec((BLOCK, 2 * KV_W), row), _const((1, 128))],
        out_shape=[jax.ShapeDtypeStruct((T, ATTN_W), CDT), jax.ShapeDtypeStruct((1, ATTN_W), F32),
                   jax.ShapeDtypeStruct((T, 2 * KV_W), F32),
                   jax.ShapeDtypeStruct((T // tq * BLOCK, 2 * KV_W), F32), jax.ShapeDtypeStruct((1, 128), F32)],
        scratch_shapes=[pltpu.VMEM((tq + BLOCK, KV_W), CDT), pltpu.VMEM((tq + BLOCK, KV_W), CDT),
                        pltpu.VMEM((tq + BLOCK, KV_W), F32), pltpu.VMEM((tq + BLOCK, KV_W), F32),
                        pltpu.VMEM((N_KV, STACK, 2 * HEAD_DIM), CDT), pltpu.VMEM((N_KV, STACK, 2 * HEAD_DIM), CDT)],
        sem=("arbitrary",), args=(sinks, _score_bias(), qkv, qkv, qkv, qkv, qkv, do, lse))


def inproj_bwd(dres, x, g, w, dq, dkv, spill, dglu, dgate, sums, *, tm=512):
    T = x.shape[0]
    nt = T // tm
    pieces = ((0, ATTN_W), (QKV_W, 2 * CONV_C), (QKV_W + 2 * CONV_C, 2 * D))

    def body(dr_ref, x_ref, g_ref, w_ref, dq_ref, dkv_ref, sp_ref, dglu_ref, dgate_ref, sq_ref, sglu_ref, sgate_ref,
             dx_ref, dp_ref, h_ref, dg_ref, db_ref):
        i = pl.program_id(0)

        @pl.when(i == 0)
        def _():
            dg_ref[...] = jnp.zeros_like(dg_ref)
            db_ref[:, ATTN_W:QKV_W] = jnp.zeros((1, QKV_W - ATTN_W), F32)
            for (c0, wd), s_ref in zip(pieces, (sq_ref, sglu_ref, sgate_ref)):
                db_ref[:, c0:c0 + wd] = s_ref[...]

        sp = sp_ref[...]
        sp = jnp.where(i == nt - 1, jnp.zeros_like(sp), sp)
        dkv = dkv_ref[...]
        db_ref[:, ATTN_W:QKV_W] += _colsum(dkv) + _colsum(sp)
        dp_ref[0:tm - BLOCK, ATTN_W:QKV_W] = dkv[0:tm - BLOCK, :].astype(CDT)
        dp_ref[tm - BLOCK:tm, ATTN_W:QKV_W] = (dkv[tm - BLOCK:tm, :] + sp).astype(CDT)
        for (c0, wd), ref in zip(pieces, (dq_ref, dglu_ref, dgate_ref)):
            dp_ref[:, c0:c0 + wd] = ref[...]
        dh = _dot(dp_ref[...], w_ref[...])
        xv = x_ref[...]
        gv = g_ref[...]
        dxn, dg_rows = _rms_bwd(xv, gv, dh)
        dx_ref[...] = dr_ref[...] + dxn
        dg_ref[...] += _colsum(dg_rows)
        r = lax.rsqrt(jnp.mean(xv * xv, axis=-1, keepdims=True) + EPS)
        h_ref[...] = (xv * r * gv).astype(CDT)

    row = lambda i: (i, 0)
    return pl.pallas_call(
        body, name="inproj_bwd", grid=(nt,),
        in_specs=[pl.BlockSpec((tm, D), row), pl.BlockSpec((tm, D), row), _const((1, D)), _const((IN_W, D), 1),
                  pl.BlockSpec((tm, ATTN_W), row), pl.BlockSpec((tm, 2 * KV_W), row),
                  pl.BlockSpec((BLOCK, 2 * KV_W), lambda i: (jnp.minimum(i + 1, nt - 1), 0)),
                  pl.BlockSpec((tm, 2 * CONV_C), row), pl.BlockSpec((tm, 2 * D), row)]
        + [_const((1, wd)) for _, wd in pieces],
        out_specs=[pl.BlockSpec((tm, D), row), pl.BlockSpec((tm, IN_W), row), pl.BlockSpec((tm, D), row),
                   _const((1, D)), _const((1, IN_W))],
        out_shape=[jax.ShapeDtypeStruct((T, D), F32), jax.ShapeDtypeStruct((T, IN_W), CDT),
                   jax.ShapeDtypeStruct((T, D), CDT), jax.ShapeDtypeStruct((1, D), F32),
                   jax.ShapeDtypeStruct((1, IN_W), F32)],
        compiler_params=_cp("arbitrary"),
    )(dres, x, g, w, dq, dkv, spill, dglu, dgate, *sums)


ATTN_TILE = 512
MATRICES = ("w_in", "w_attn_proj", "w_conv_proj", "w_out", "w_mlp1", "w_mlp2")
SMALL = ("mix_norm_g", "b_in", "sinks", "conv_w", "conv_b", "conv_ln_g", "conv_ln_b", "b_conv_proj", "mlp_norm_g")


def forward_backward(x, tgt, hooks):
    def call(fn, kernel, l, *args, **kw):
        rider = hooks.rider(kernel, l)
        if rider is None:
            return fn(*args, **kw)
        outs, landed = fn(*args, rider=rider, **kw)
        hooks.landed(kernel, l, landed)
        return outs

    vec = hooks.vec
    saved = []
    for l in range(DEPTH):
        qkv, rest = call(rms_inproj, "rms_inproj", l, x, vec("mix_norm_g", l), hooks.w_in(l), vec("b_in", l))
        m = hooks.mats(l)
        attn, lse = attn_fwd(qkv, hooks.sinks(l), tq=ATTN_TILE)
        u3, u1 = conv_fwd(rest, hooks.taps(l), vec("conv_b", l), vec("conv_ln_g", l), vec("conv_ln_b", l))
        x1 = merge_out(x, attn, u3, rest, m["w_attn_proj"], m["w_conv_proj"], vec("b_conv_proj", l), m["w_out"])
        if l < DEPTH - 1:
            x_next, pre = call(mlp_fwd, "mlp_fwd", l, x1, vec("mlp_norm_g", l), m["w_mlp1"], m["w_mlp2"])
        else:
            x_next = None
            pre, dx, dgf, loss = call(mlp_fwd, "mlp_fwd", l, x1, vec("mlp_norm_g", l), m["w_mlp1"], m["w_mlp2"],
                                      head=(hooks.final_g, tgt))
        saved.append((x, qkv, rest, attn, lse, u3, u1, x1, pre))
        x = x_next
    small = {n: [None] * DEPTH for n in SMALL}
    small["final_norm_g"] = dgf
    for l in reversed(range(DEPTH)):
        x0, qkv, rest, attn, lse, u3, u1, x1, pre = saved[l]
        m = hooks.mats(l)
        dx1, dg2, h2, a, dpre, dxb = call(mlp_bwd, "mlp_bwd", l, dx, x1, vec("mlp_norm_g", l), pre, m["w_mlp1"], m["w_mlp2"])
        small["mlp_norm_g"][l] = dg2
        group = {}
        group["w_mlp1"] = call(tn_matmul, "tn_mlp1", l, h2, dpre, tm=1024, tn=1024, tk=4096, name="tn_mlp1", by_chip=True)
        group["w_mlp2"] = call(tn_matmul, "tn_mlp2", l, a, dxb, tm=1024, tn=1024, tk=4096, name="tn_mlp2")
        merged, dba, dbc, dattn, du3, dgate, dgate_sum, dbcp = merge_bwd(
            dx1, attn, u3, rest, m["w_attn_proj"], m["w_conv_proj"], vec("b_conv_proj", l), m["w_out"])
        small["b_conv_proj"][l] = dbcp
        group["w_out"] = tn_matmul(merged, dx1, tm=1024, tn=1024, name="tn_out")
        group["w_attn_proj"] = tn_matmul(attn, dba, tm=512, tn=256, tk=4096, name="tn_attn_proj", by_chip=True)
        group["w_conv_proj"] = tn_matmul(u3, dbc, tm=512, tn=256, tk=4096, name="tn_conv_proj", by_chip=True)
        hooks.grads(l, "A", group)
        du1, dlg, dlb, dcb = call(conv_bwd_ln, "conv_bwd_ln", l, du3, u1, vec("conv_ln_g", l), vec("conv_ln_b", l))
        small["conv_ln_g"][l], small["conv_ln_b"][l], small["conv_b"][l] = dlg, dlb, dcb
        dglu, dglu_sum, dcw = conv_bwd_taps(du1, rest, hooks.taps(l))
        small["conv_w"][l] = dcw
        dq, dq_sum, dkv, spill, dsink = call(attn_bwd, "attn_bwd", l, qkv, dattn, lse, hooks.sinks(l), tq=ATTN_TILE)
        small["sinks"][l] = dsink
        dx, dproj, h, dg, db = inproj_bwd(dx1, x0, vec("mix_norm_g", l), hooks.w_in(l), dq, dkv, spill, dglu, dgate,
                                          (dq_sum, dglu_sum, dgate_sum), tm=ATTN_TILE)
        small["mix_norm_g"][l], small["b_in"][l] = dg, db
        hooks.grads(l, "B", {"w_in": call(tn_matmul, "tn_in", l, dproj, h, tm=768, tn=1024, tk=4096, name="tn_in")})
    return loss, dx, small


class _LocalHooks:
    def __init__(self, p):
        self.p = p
        self.final_g = p["final_norm_g"]
        self.got = {n: [None] * DEPTH for n in MATRICES}

    def w_in(self, l):
        return self.p["w_in"][l].T

    def mats(self, l):
        return {n: self.p[n][l] for n in MATRICES}

    def vec(self, n, l):
        return self.p[n][l]

    def sinks(self, l):
        return self.p["sinks"][l]

    def taps(self, l):
        return self.p["conv_w"][l]

    def rider(self, kernel, l):
        return None

    def grads(self, l, group, g):
        for n, v in g.items():
            if v.ndim == 3:
                v = v.transpose(1, 0, 2).reshape(v.shape[1], -1)
            self.got[n][l] = v.T if n == "w_in" else v


def local_grads(x, tgt, p):
    hooks = _LocalHooks(p)
    loss, dx, small = forward_backward(x, tgt, hooks)
    small["conv_w"] = [g[0:CONV_K] for g in small["conv_w"]]
    small["sinks"] = [g[0, 0:N_Q] for g in small["sinks"]]
    return loss, dx, {**small, **hooks.got}


MESH = pl.DeviceIdType.MESH
N_CHIPS = 4
N_DEV = 8
FLAT_W = 1024
FLAT_PARTS = (("w_in", 960), ("w_attn_proj", 128), ("w_conv_proj", 128), ("w_out", 256), ("w_mlp1", 1024), ("w_mlp2", 1024))
FLAT_ROWS = sum(r for _, r in FLAT_PARTS)
W_IN_ROWS = FLAT_PARTS[0][1]
GROUP_A = (("w_mlp1", 1024), ("w_mlp2", 1024), ("w_out", 256), ("w_attn_proj", 128), ("w_conv_proj", 128))
COL_SHARDED = ("w_in", "w_attn_proj", "w_conv_proj", "w_mlp1")
FULL_SHAPES = {"w_in": (D, IN_W), "w_attn_proj": (ATTN_W, D), "w_conv_proj": (CONV_C, D), "w_out": (D, D),
               "w_mlp1": (D, D_FF), "w_mlp2": (D_FF, D)}


def _place():
    x, y, c = lax.axis_index("x"), lax.axis_index("y"), lax.axis_index("c")
    return x, y, c, 2 * x + y


def _peer_chips(x, y, j):
    return [((x, 1 - y), j ^ 1), ((1 - x, y), j ^ 2), ((1 - x, 1 - y), j ^ 3)]


def _remote(src, dst, sems, k, n, to):
    return pltpu.make_async_remote_copy(src_ref=src, dst_ref=dst, send_sem=sems.at[k], recv_sem=sems.at[n + k],
                                        device_id=to, device_id_type=MESH)


def _half(c, rows):
    h = rows // 2
    return pl.ds(pl.multiple_of(c * h, 16), h)


def gather_rider(wsh):
    R = wsh.shape[0]
    n = 7

    def plan(rins, routs, sems):
        (w_ref,), (out_ref,) = rins, routs
        x, y, c, j = _place()
        peers = _peer_chips(x, y, j)
        mine, other = _half(c, R), _half(1 - c, R)
        sent = [_remote(w_ref.at[mine], out_ref.at[j, mine], sems, k, n, (*chip, c)) for k, (chip, _) in enumerate(peers)]
        sent.append(_remote(w_ref, out_ref.at[j], sems, 6, n, (x, y, 1 - c)))
        landed = [_remote(w_ref.at[mine], out_ref.at[pj, mine], sems, k, n, (x, y, c)) for k, (_, pj) in enumerate(peers)]
        passed = [_remote(out_ref.at[pj, mine], out_ref.at[pj, mine], sems, 3 + k, n, (x, y, 1 - c))
                  for k, (_, pj) in enumerate(peers)]
        handed = [_remote(w_ref.at[mine], out_ref.at[pj, other], sems, 3 + k, n, (x, y, c)) for k, (_, pj) in enumerate(peers)]
        handed.append(_remote(w_ref, out_ref.at[j], sems, 6, n, (x, y, c)))
        return sent, landed, passed, handed

    def start(rins, routs, sems):
        for cp in plan(rins, routs, sems)[0]:
            cp.start()

    def late(rins, routs, sems):
        _, landed, passed, _ = plan(rins, routs, sems)
        for k in range(3):
            landed[k].wait_recv()
            passed[k].start()

    def finish(rins, routs, sems):
        sent, _, passed, handed = plan(rins, routs, sems)
        for cp in handed:
            cp.wait_recv()
        for cp in sent + passed:
            cp.wait_send()

    return Rider((wsh,), (jax.ShapeDtypeStruct((N_CHIPS,) + wsh.shape, wsh.dtype),), 2 * n, start, finish, late)


def swap_rider(g):
    R = g.shape[1]

    def plan(rins, routs, sems):
        (g_ref,), (got_ref,) = rins, routs
        x, y, c, _ = _place()
        return _remote(g_ref.at[:, _half(1 - c, R), :], got_ref, sems, 0, 1, (x, y, 1 - c))

    def start(rins, routs, sems):
        plan(rins, routs, sems).start()

    def finish(rins, routs, sems):
        plan(rins, routs, sems).wait()

    return Rider((g,), (jax.ShapeDtypeStruct((N_CHIPS, R // 2, FLAT_W), g.dtype),), 2, start, finish)


def exchange_rider(pb):
    def plan(rins, routs, sems):
        (pb_ref,), (got_ref,) = rins, routs
        x, y, c, j = _place()
        peers = _peer_chips(x, y, j)
        sent = [_remote(pb_ref.at[pj], got_ref.at[j], sems, k, 3, (*chip, c)) for k, (chip, pj) in enumerate(peers)]
        landed = [_remote(pb_ref.at[pj], got_ref.at[pj], sems, k, 3, (x, y, c)) for k, (_, pj) in enumerate(peers)]
        return sent, landed

    def start(rins, routs, sems):
        for cp in plan(rins, routs, sems)[0]:
            cp.start()

    def finish(rins, routs, sems):
        sent, landed = plan(rins, routs, sems)
        for cp in landed:
            cp.wait_recv()
        for cp in sent:
            cp.wait_send()

    return Rider((pb,), (jax.ShapeDtypeStruct(pb.shape, pb.dtype),), 6, start, finish)


def share_rider(tot):
    def plan(rins, routs, sems):
        (t_ref,), (got_ref,) = rins, routs
        x, y, c, _ = _place()
        return _remote(t_ref, got_ref, sems, 0, 1, (x, y, 1 - c))

    def start(rins, routs, sems):
        plan(rins, routs, sems).start()

    def finish(rins, routs, sems):
        plan(rins, routs, sems).wait()

    return Rider((tot,), (jax.ShapeDtypeStruct(tot.shape, tot.dtype),), 2, start, finish)


def pair_sum(g, got):
    nj, R, W = g.shape
    h = R // 2
    tile = h // 2

    def body(g_ref, got_ref, pb_ref, own_ref):
        v = g_ref[...] + got_ref[...]
        pb_ref[...] = v.astype(pb_ref.dtype)

        @pl.when(pl.program_id(1) == _place()[3])
        def _():
            own_ref[...] = v

    return pl.pallas_call(
        body, name="pair_sum", grid=(h // tile, nj),
        in_specs=[pl.BlockSpec((None, tile, W), lambda r, j: (j, lax.axis_index("c") * (h // tile) + r, 0)),
                  pl.BlockSpec((None, tile, W), lambda r, j: (j, r, 0))],
        out_specs=[pl.BlockSpec((None, tile, W), lambda r, j: (j, r, 0)), pl.BlockSpec((tile, W), lambda r, j: (r, 0))],
        out_shape=[jax.ShapeDtypeStruct((nj, h, W), CDT), jax.ShapeDtypeStruct((h, W), F32)],
        compiler_params=_cp("arbitrary", "arbitrary"),
    )(g, got)


def total_sum(own, got):
    R, W = own.shape
    tile = R // 2

    def body(own_ref, a_ref, b_ref, c_ref, o_ref):
        o_ref[...] = ((own_ref[...] + a_ref[...].astype(F32)) + b_ref[...].astype(F32)) + c_ref[...].astype(F32)

    def slab(k):
        return pl.BlockSpec((None, tile, W), lambda r: (_place()[3] ^ (k + 1), r, 0))

    return pl.pallas_call(
        body, name="total_sum", grid=(R // tile,),
        in_specs=[pl.BlockSpec((tile, W), lambda r: (r, 0)), slab(0), slab(1), slab(2)],
        out_specs=pl.BlockSpec((tile, W), lambda r: (r, 0)),
        out_shape=jax.ShapeDtypeStruct((R, W), F32),
        compiler_params=_cp("arbitrary"),
    )(own, got, got, got)


def _all_peers(x, y, c):
    return [(x ^ (r >> 2), y ^ ((r >> 1) & 1), c ^ (r & 1)) for r in range(1, N_DEV)]


ROW_ITEMS = (("mix_norm_g", D), ("b_in", IN_W), ("sinks", N_Q), ("conv_b", CONV_C), ("conv_ln_g", CONV_C),
             ("conv_ln_b", CONV_C), ("b_conv_proj", D), ("mlp_norm_g", D))
TAPS_ROW = 16
TAPS_ROWS = 32
LAYER_ROWS = TAPS_ROW + TAPS_ROWS
FINAL_ROW = DEPTH * LAYER_ROWS
SMALL_ROWS = FINAL_ROW + SUBLANES


def _row_chunks():
    out, r = {}, 0
    for n, width in ROW_ITEMS:
        out[n] = [(r + i, FLAT_W * i, min(FLAT_W, width - FLAT_W * i)) for i in range(-(-width // FLAT_W))]
        r += len(out[n])
    assert r <= TAPS_ROW
    return out


def sum_small(gsm):
    chunks = _row_chunks()
    ins = []
    for l in range(DEPTH):
        ins += [gsm[n][l] for n, _ in ROW_ITEMS] + [gsm["conv_w"][l]]
    ins.append(gsm["final_norm_g"])
    n_in = len(ins)

    def body(*refs):
        in_refs, o_ref, buf, send_sems, recv_sems = refs[:n_in], refs[n_in], refs[n_in + 1], refs[n_in + 2], refs[n_in + 3]
        x, y, c, _ = _place()
        me = 4 * x + 2 * y + c
        mine = buf.at[me]
        mine[...] = jnp.zeros((SMALL_ROWS, FLAT_W), F32)
        k = 0
        for l in range(DEPTH):
            for n, _ in ROW_ITEMS:
                for r, c0, wd in chunks[n]:
                    mine[l * LAYER_ROWS + r:l * LAYER_ROWS + r + 1, 0:wd] = in_refs[k][:, c0:c0 + wd]
                k += 1
            mine[l * LAYER_ROWS + TAPS_ROW:(l + 1) * LAYER_ROWS, 0:CONV_C] = in_refs[k][...]
            k += 1
        mine[FINAL_ROW:FINAL_ROW + 1, :] = in_refs[k][...]
        peers = _all_peers(x, y, c)
        sends = [pltpu.make_async_remote_copy(src_ref=mine, dst_ref=mine, send_sem=send_sems.at[r], recv_sem=recv_sems.at[r],
                                              device_id=to, device_id_type=MESH) for r, to in enumerate(peers)]
        for cp in sends:
            cp.start()
        for r in range(N_DEV - 1):
            pltpu.make_async_remote_copy(src_ref=mine, dst_ref=buf.at[me ^ (r + 1)], send_sem=send_sems.at[r],
                                         recv_sem=recv_sems.at[r], device_id=(x, y, c), device_id_type=MESH).wait_recv()
        for cp in sends:
            cp.wait_send()
        acc = buf[0]
        for d in range(1, N_DEV):
            acc = acc + buf[d]
        o_ref[...] = acc

    vm = pl.BlockSpec(memory_space=pltpu.VMEM)
    return pl.pallas_call(
        body, name="sum_small", out_shape=jax.ShapeDtypeStruct((SMALL_ROWS, FLAT_W), F32),
        in_specs=[vm] * n_in, out_specs=vm,
        scratch_shapes=[pltpu.VMEM((N_DEV, SMALL_ROWS, FLAT_W), F32), pltpu.SemaphoreType.DMA((N_DEV - 1,)),
                        pltpu.SemaphoreType.DMA((N_DEV - 1,))],
    )(*ins)


def gather_taps(taps):
    shard = taps.shape[2]

    def body(t_ref, o_ref, buf, send_sems, recv_sems):
        x, y, c, j = _place()
        peers = _peer_chips(x, y, j)
        buf[j] = t_ref[...]
        sends = [pltpu.make_async_remote_copy(src_ref=t_ref, dst_ref=buf.at[j], send_sem=send_sems.at[k],
                                              recv_sem=recv_sems.at[k], device_id=(*chip, c), device_id_type=MESH)
                 for k, (chip, _) in enumerate(peers)]
        for cp in sends:
            cp.start()
        for k, (_, pj) in enumerate(peers):
            pltpu.make_async_remote_copy(src_ref=t_ref, dst_ref=buf.at[pj], send_sem=send_sems.at[k],
                                         recv_sem=recv_sems.at[k], device_id=(x, y, c), device_id_type=MESH).wait_recv()
        for cp in sends:
            cp.wait_send()
        for jj in range(N_CHIPS):
            o_ref[:, :, jj * shard:(jj + 1) * shard] = buf[jj]

    vm = pl.BlockSpec(memory_space=pltpu.VMEM)
    return pl.pallas_call(
        body, name="gather_taps", out_shape=jax.ShapeDtypeStruct(taps.shape[:2] + (N_CHIPS * shard,), taps.dtype),
        in_specs=[vm], out_specs=vm,
        scratch_shapes=[pltpu.VMEM((N_CHIPS,) + taps.shape, taps.dtype), pltpu.SemaphoreType.DMA((3,)),
                        pltpu.SemaphoreType.DMA((3,))],
    )(taps)


def _adam_math(w, g, m, v):
    nm = ADAM_B1 * m + (1.0 - ADAM_B1) * g
    nv = ADAM_B2 * v + (1.0 - ADAM_B2) * jnp.square(g)
    m_hat = nm / (1.0 - ADAM_B1 ** ADAM_STEP)
    v_hat = nv / (1.0 - ADAM_B2 ** ADAM_STEP)
    return -ADAM_LR * (m_hat / (jnp.sqrt(v_hat) + ADAM_EPS) + ADAM_WD * w), nm, nv


def adamw(w, g, m, v, *, name):
    L, R, C = w.shape
    tr = next(t for t in (512, 480, 256, 128) if R % t == 0)

    def body(w_ref, g_ref, m_ref, v_ref, d_ref, nm_ref, nv_ref):
        d_ref[...], nm_ref[...], nv_ref[...] = _adam_math(w_ref[...], g_ref[...], m_ref[...], v_ref[...])

    spec = pl.BlockSpec((None, tr, C), lambda l, i: (l, i, 0))
    out = jax.ShapeDtypeStruct((L, R, C), F32)
    return pl.pallas_call(
        body, name=name, grid=(L, R // tr), in_specs=[spec] * 4, out_specs=[spec] * 3, out_shape=[out] * 3,
        compiler_params=_cp("parallel", "parallel"),
    )(w, g, m, v)


def adamw_small(packed, w, m, v):
    chunks = _row_chunks()
    names = SMALL + ("final_norm_g",)
    as_2d = lambda a: a.reshape(1, -1) if a.ndim == 1 else a
    ins = [as_2d(t[n]) for n in names for t in (w, m, v)]
    shapes = [jax.ShapeDtypeStruct(as_2d(w[n]).shape, F32) for n in names for _ in range(4)]
    n_in = len(ins)

    def body(p_ref, *refs):
        in_refs, out_refs = refs[:n_in], refs[n_in:]
        chip = _place()[3]
        for i, n in enumerate(names):
            w_ref, m_ref, v_ref = in_refs[3 * i:3 * i + 3]
            outs = out_refs[4 * i:4 * i + 4]

            def step(at, g):
                res = (g,) + _adam_math(w_ref[at], g, m_ref[at], v_ref[at])
                for o_ref, val in zip(outs, res):
                    o_ref[at] = val

            if n == "final_norm_g":
                step((slice(None), slice(None)), p_ref[FINAL_ROW:FINAL_ROW + 1, :])
                continue
            for l in range(DEPTH):
                if n == "conv_w":
                    r0 = l * LAYER_ROWS + TAPS_ROW
                    shard = CONV_C // N_CHIPS
                    g = jnp.zeros((CONV_K, shard), F32)
                    for j in range(N_CHIPS):
                        g = jnp.where(chip == j, p_ref[r0:r0 + CONV_K, j * shard:(j + 1) * shard], g)
                    step((l,), g)
                else:
                    for r, c0, wd in chunks[n]:
                        step((slice(l, l + 1), slice(c0, c0 + wd)),
                             p_ref[l * LAYER_ROWS + r:l * LAYER_ROWS + r + 1, 0:wd])

    vm = pl.BlockSpec(memory_space=pltpu.VMEM)
    res = pl.pallas_call(
        body, name="adamw_small", out_shape=shapes,
        in_specs=[vm] + [vm] * n_in, out_specs=[vm] * len(shapes),
    )(packed, *ins)
    dicts = ({}, {}, {}, {})
    for i, n in enumerate(names):
        for d, val in zip(dicts, res[4 * i:4 * i + 4]):
            d[n] = val.reshape(w[n].shape)
    return dicts


def _flat_rows(name, shard):
    return shard.T if name == "w_in" else shard.reshape(-1, FLAT_W)


def _full_matrix(slabs, name):
    K, N = FULL_SHAPES[name]
    if name == "w_in":
        return slabs.reshape(N, K)
    if name in COL_SHARDED:
        return slabs.reshape(N_CHIPS, K, N // N_CHIPS).transpose(1, 0, 2).reshape(K, N)
    return slabs.reshape(K, N)


def _first_row(parts, name):
    r = 0
    for n, rows in parts:
        if n == name:
            return r, rows
        r += rows
    raise KeyError(name)


class _Exchange:
    CARRIERS = {
        ("conv_bwd_ln", 1): ((1, "A"), "swap"), ("attn_bwd", 1): ((1, "A"), "exchange"), ("tn_in", 1): ((1, "A"), "share"),
        ("mlp_bwd", 0): ((1, "B"), "swap"), ("tn_mlp1", 0): ((1, "B"), "exchange"), ("tn_mlp2", 0): ((1, "B"), "share"),
        ("conv_bwd_ln", 0): ((0, "A"), "swap"), ("attn_bwd", 0): ((0, "A"), "exchange"), ("tn_in", 0): ((0, "A"), "share"),
    }

    def __init__(self, w, ci, chip):
        self.w, self.ci, self.chip = w, ci, chip
        self.wsh = [jnp.concatenate([_flat_rows(n, w[n][l]) for n, _ in FLAT_PARTS], axis=0).astype(CDT)
                    for l in range(DEPTH)]
        self.final_g = w["final_norm_g"].reshape(1, D)
        self.slabs = {}
        self.full = {}
        self.units = {}
        self.reduced = {}
        self._landed_weights(0, 0, _run_alone(gather_rider(self.wsh[0][:W_IN_ROWS]), "gather_w_in")[0])
        self.all_taps = gather_taps(w["conv_w"])

    def _landed_weights(self, l, r0, buf):
        self.slabs.setdefault(l, []).append((r0, buf))

    def _matrix(self, l, name):
        if (l, name) not in self.full:
            r, rows = _first_row(FLAT_PARTS, name)
            r0, buf = next((r0, buf) for r0, buf in self.slabs[l] if r0 <= r < r0 + buf.shape[1])
            self.full[(l, name)] = _full_matrix(buf[:, r - r0:r - r0 + rows], name)
        return self.full[(l, name)]

    def w_in(self, l):
        return self._matrix(l, "w_in")

    def mats(self, l):
        return {n: self._matrix(l, n) for n in MATRICES if n != "w_in"}

    def vec(self, n, l):
        return self.w[n][l].reshape(1, -1)

    def sinks(self, l):
        return self.w["sinks"][l]

    def taps(self, l):
        return self.all_taps[l]

    def rider(self, kernel, l):
        if (kernel, l) == ("rms_inproj", 0):
            return gather_rider(self.wsh[0][W_IN_ROWS:])
        if (kernel, l) == ("mlp_fwd", 0):
            return gather_rider(self.wsh[1])
        if (kernel, l) in self.CARRIERS:
            return self._stage(*self.CARRIERS[(kernel, l)])
        return None

    def landed(self, kernel, l, bufs):
        if (kernel, l) == ("rms_inproj", 0):
            self._landed_weights(0, W_IN_ROWS, bufs[0])
        elif (kernel, l) == ("mlp_fwd", 0):
            self._landed_weights(1, 0, bufs[0])
        else:
            self._stage_landed(*self.CARRIERS[(kernel, l)], bufs[0])

    def grads(self, l, group, g):
        if group == "A":
            flat = jnp.concatenate([g[n].reshape(N_CHIPS, rows, FLAT_W) for n, rows in GROUP_A], axis=1)
        else:
            flat = g["w_in"].reshape(N_CHIPS, W_IN_ROWS, FLAT_W)
        self.units[(l, group)] = {"g": flat}

    def _stage(self, key, stage):
        u = self.units[key]
        if stage == "swap":
            return swap_rider(u["g"])
        if stage == "exchange":
            u["pb"], u["own"] = pair_sum(u["g"], u["swap"])
            return exchange_rider(u["pb"])
        u["tot"] = total_sum(u["own"], u["exchange"])
        return share_rider(u["tot"])

    def _stage_landed(self, key, stage, buf):
        u = self.units[key]
        u[stage] = buf
        if stage == "share":
            tot = u["tot"]
            self.reduced[key] = jnp.where(self.ci == 0, jnp.concatenate([tot, buf]), jnp.concatenate([buf, tot]))

    def finish(self):
        key = (0, "B")
        for stage in ("swap", "exchange", "share"):
            self._stage_landed(key, stage, _run_alone(self._stage(key, stage), stage + "_last")[0])
        out = {}
        for n in MATRICES:
            per_layer = []
            for l in range(DEPTH):
                if n == "w_in":
                    per_layer.append(self.reduced[(l, "B")].T)
                    continue
                r, rows = _first_row(GROUP_A, n)
                per_layer.append(self.reduced[(l, "A")][r:r + rows].reshape(self.w[n].shape[1:]))
            out[n] = jnp.stack(per_layer)
        return out


WEIGHTS = ("mix_norm_g", "w_in", "b_in", "sinks", "conv_w", "conv_b", "conv_ln_g", "conv_ln_b", "w_attn_proj",
           "w_conv_proj", "b_conv_proj", "w_out", "mlp_norm_g", "w_mlp1", "w_mlp2", "final_norm_g")


def kernel(x, mix_norm_g, w_in, b_in, sinks, conv_w, conv_b, conv_ln_g, conv_ln_b, w_attn_proj, w_conv_proj, b_conv_proj, w_out, mlp_norm_g, w_mlp1, w_mlp2, final_norm_g, loss_target, m_mix_norm_g, m_w_in, m_b_in, m_sinks, m_conv_w, m_conv_b, m_conv_ln_g, m_conv_ln_b, m_w_attn_proj, m_w_conv_proj, m_b_conv_proj, m_w_out, m_mlp_norm_g, m_w_mlp1, m_w_mlp2, m_final_norm_g, v_mix_norm_g, v_w_in, v_b_in, v_sinks, v_conv_w, v_conv_b, v_conv_ln_g, v_conv_ln_b, v_w_attn_proj, v_w_conv_proj, v_b_conv_proj, v_w_out, v_mlp_norm_g, v_w_mlp1, v_w_mlp2, v_final_norm_g):
    w = dict(zip(WEIGHTS, (mix_norm_g, w_in, b_in, sinks, conv_w, conv_b, conv_ln_g, conv_ln_b, w_attn_proj, w_conv_proj,
                           b_conv_proj, w_out, mlp_norm_g, w_mlp1, w_mlp2, final_norm_g)))
    m = dict(zip(WEIGHTS, (m_mix_norm_g, m_w_in, m_b_in, m_sinks, m_conv_w, m_conv_b, m_conv_ln_g, m_conv_ln_b, m_w_attn_proj,
                           m_w_conv_proj, m_b_conv_proj, m_w_out, m_mlp_norm_g, m_w_mlp1, m_w_mlp2, m_final_norm_g)))
    v = dict(zip(WEIGHTS, (v_mix_norm_g, v_w_in, v_b_in, v_sinks, v_conv_w, v_conv_b, v_conv_ln_g, v_conv_ln_b, v_w_attn_proj,
                           v_w_conv_proj, v_b_conv_proj, v_w_out, v_mlp_norm_g, v_w_mlp1, v_w_mlp2, v_final_norm_g)))
    xi, yi, ci = lax.axis_index("x"), lax.axis_index("y"), lax.axis_index("c")
    chip = 2 * xi + yi

    hooks = _Exchange(w, ci, chip)
    loss, dx, gsm = forward_backward(x[0], loss_target[0], hooks)
    loss = lax.psum(loss[0, 0], ("x", "y", "c"))
    grads = hooks.finish()

    gsmall, delta, new_m, new_v = adamw_small(sum_small(gsm), w, m, v)
    grads.update(gsmall)
    for n in MATRICES:
        t = (lambda a: jnp.swapaxes(a, 1, 2)) if n == "w_in" else (lambda a: a)
        delta[n], new_m[n], new_v[n] = map(t, adamw(t(w[n]), t(grads[n]), t(m[n]), t(v[n]), name="adamw_" + n))

    return (loss, dx[None], *[grads[n] for n in WEIGHTS], *[delta[n] for n in WEIGHTS],
            *[new_m[n] for n in WEIGHTS], *[new_v[n] for n in WEIGHTS])
```

```python
import functools
import math
from typing import Callable, NamedTuple, Optional

import jax
import jax.numpy as jnp
import numpy as np
from jax import lax
from jax.experimental import pallas as pl
from jax.experimental.pallas import tpu as pltpu

F32 = jnp.float32
CDT = jnp.bfloat16

D = 1024
DEPTH = 2
N_Q = 8
HEAD_DIM = 64
ATTN_W = 512
KV_W = 128
BLOCK = 128
CONV_C = 512
CONV_K = 31
D_FF = 4096
IN_W = 3840
QKV_W = ATTN_W + 2 * KV_W
REST_W = IN_W - QKV_W
EPS = 1e-6
NEG = -1e30
SCALE = 1.0 / math.sqrt(HEAD_DIM)
SLOPES = [float(2.0 ** (-8.0 * (h + 1) / N_Q)) for h in range(N_Q)]
SUBLANES = 8
HALO = 32

ADAM_LR = 0.001
ADAM_B1 = 0.9
ADAM_B2 = 0.999
ADAM_EPS = 1e-08
ADAM_WD = 0.01
ADAM_STEP = 10

VMEM_LIMIT = 56 * 1024 * 1024


def _cp(*sem):
    return pltpu.CompilerParams(dimension_semantics=sem, vmem_limit_bytes=VMEM_LIMIT)


def _dot(a, b):
    return jnp.dot(a, b, preferred_element_type=F32)


def _dot_nt(a, b):
    return lax.dot_general(a, b, (((1,), (1,)), ((), ())), preferred_element_type=F32)


def _dot_tn(a, b):
    return lax.dot_general(a, b, (((0,), (0,)), ((), ())), preferred_element_type=F32)


def _sig(x):
    return 1.0 / (1.0 + jnp.exp(-x))


def _colsum(v):
    return jnp.sum(v, axis=0, keepdims=True)


def _const(shape, buffers=None):
    mode = {} if buffers is None else {"pipeline_mode": pl.Buffered(buffers)}
    return pl.BlockSpec(shape, lambda *_: (0,) * len(shape), **mode)


class Rider(NamedTuple):
    ins: tuple
    outs: tuple
    n_sems: int
    start: Callable
    finish: Callable
    late: Optional[Callable] = None


def _any():
    return pl.BlockSpec(memory_space=pl.ANY)


def _run(body, rider, *, name, grid, in_specs, out_specs, out_shape, args, sem, scratch_shapes=()):
    if rider is None:
        return pl.pallas_call(body, name=name, grid=grid, in_specs=list(in_specs), out_specs=list(out_specs),
                              out_shape=list(out_shape), scratch_shapes=list(scratch_shapes),
                              compiler_params=_cp(*sem))(*args)
    n_in, n_out, n_sc = len(in_specs), len(out_specs), len(scratch_shapes)
    r_in, r_out = len(rider.ins), len(rider.outs)

    def riding(*refs):
        ins, rins = refs[:n_in], refs[n_in:n_in + r_in]
        o0 = n_in + r_in
        outs, routs = refs[o0:o0 + n_out], refs[o0 + n_out:o0 + n_out + r_out]
        s0 = o0 + n_out + r_out
        scratch, sems = refs[s0:s0 + n_sc], refs[s0 + n_sc]
        first = functools.reduce(jnp.logical_and, [pl.program_id(a) == 0 for a in range(len(grid))])
        last = functools.reduce(jnp.logical_and, [pl.program_id(a) == grid[a] - 1 for a in range(len(grid))])

        @pl.when(first)
        def _():
            rider.start(rins, routs, sems)

        if rider.late is not None:
            late_step = grid[0] - max(1, grid[0] // 8)
            others = [pl.program_id(a) == 0 for a in range(1, len(grid))]

            @pl.when(functools.reduce(jnp.logical_and, others, pl.program_id(0) == late_step))
            def _():
                rider.late(rins, routs, sems)

        body(*ins, *outs, *scratch)

        @pl.when(last)
        def _():
            rider.finish(rins, routs, sems)

    res = pl.pallas_call(
        riding, name=name, grid=grid, in_specs=list(in_specs) + [_any()] * r_in,
        out_specs=list(out_specs) + [_any()] * r_out, out_shape=list(out_shape) + list(rider.outs),
        scratch_shapes=list(scratch_shapes) + [pltpu.SemaphoreType.DMA((rider.n_sems,))],
        compiler_params=_cp(*["arbitrary"] * len(grid)))(*args, *rider.ins)
    return res[:n_out], res[n_out:]


def _run_alone(rider, name):
    def body(*refs):
        r_in, r_out = len(rider.ins), len(rider.outs)
        rins, routs, sems = refs[:r_in], refs[r_in:r_in + r_out], refs[r_in + r_out]
        rider.start(rins, routs, sems)
        if rider.late is not None:
            rider.late(rins, routs, sems)
        rider.finish(rins, routs, sems)

    return pl.pallas_call(
        body, name=name, in_specs=[_any()] * len(rider.ins), out_specs=[_any()] * len(rider.outs),
        out_shape=list(rider.outs), scratch_shapes=[pltpu.SemaphoreType.DMA((rider.n_sems,))])(*rider.ins)


def rms_inproj(x, g, wt, b, *, tm=512, rider=None):
    T = x.shape[0]

    def body(x_ref, g_ref, w_ref, b_ref, qkv_ref, rest_ref):
        xv = x_ref[...]
        r = lax.rsqrt(jnp.mean(xv * xv, axis=-1, keepdims=True) + EPS)
        h = (xv * r * g_ref[...]).astype(CDT)
        qkv_ref[...] = (_dot_nt(h, w_ref[0:QKV_W, :]) + b_ref[:, 0:QKV_W]).astype(qkv_ref.dtype)
        for j in range(REST_W // D):
            c0 = QKV_W + D * j
            rest_ref[:, D * j:D * (j + 1)] = _dot_nt(h, w_ref[c0:c0 + D, :]) + b_ref[:, c0:c0 + D]

    return _run(
        body, rider, name="rms_inproj", grid=(T // tm,),
        in_specs=[pl.BlockSpec((tm, D), lambda i: (i, 0)), _const((1, D)), _const((IN_W, D), 1), _const((1, IN_W))],
        out_specs=[pl.BlockSpec((tm, QKV_W), lambda i: (i, 0)), pl.BlockSpec((tm, REST_W), lambda i: (i, 0))],
        out_shape=[jax.ShapeDtypeStruct((T, QKV_W), CDT), jax.ShapeDtypeStruct((T, REST_W), F32)],
        sem=("parallel",), args=(x, g, wt, b))


def _lane_halves(shape):
    lane = lax.broadcasted_iota(jnp.int32, shape, 1)
    return lane < HEAD_DIM, lane >= HEAD_DIM


def _swap_halves(v):
    return pltpu.roll(v.astype(F32), HEAD_DIM, axis=1).astype(v.dtype)


N_KV = KV_W // HEAD_DIM
GROUP = N_Q // N_KV
STACK = GROUP * BLOCK


def _score_bias():
    row = np.arange(STACK)[:, None] % BLOCK
    col = np.arange(2 * BLOCK)[None, :]
    dist = row + BLOCK - col
    window = (dist >= 0) & (dist < BLOCK)
    slopes = np.asarray(SLOPES, np.float32).reshape(N_KV, GROUP)
    out = np.empty((2, N_KV, STACK, 2 * BLOCK), np.float32)
    for first in range(2):
        valid = window & ((col >= BLOCK) | (first == 0))
        for g in range(N_KV):
            slope = np.repeat(slopes[g], BLOCK)[:, None]
            out[first, g] = np.where(valid, -(slope * dist.astype(np.float32)), np.float32(NEG))
    return jnp.asarray(out)


def _per_head_column(vals):
    row = lax.broadcasted_iota(jnp.int32, (STACK, 1), 0)
    col = jnp.full((STACK, 1), vals[GROUP - 1], F32)
    for i in reversed(range(GROUP - 1)):
        col = jnp.where(row < (i + 1) * BLOCK, vals[i], col)
    return col


def _stack_heads(dst, src_ref, r0, g, scale=None):
    lane = lax.broadcasted_iota(jnp.int32, (BLOCK, 2 * HEAD_DIM), 1)
    keep = (lane >= HEAD_DIM) if g else (lane < HEAD_DIM)
    for i in range(GROUP):
        h = GROUP * g + i
        tile = src_ref[pl.ds(r0, BLOCK), (h // 2) * 128:(h // 2 + 1) * 128]
        if h % 2 != g:
            tile = _swap_halves(tile)
        if scale is not None:
            tile = tile * jnp.asarray(scale, tile.dtype)
        dst[i * BLOCK:(i + 1) * BLOCK, :] = jnp.where(keep, tile, jnp.zeros_like(tile))


def _unstack_heads(dst_ref, stacked, r0, g):
    lane = lax.broadcasted_iota(jnp.int32, (BLOCK, 2 * HEAD_DIM), 1)
    tiles = []
    for j in range(GROUP // 2):
        even = stacked[(2 * j) * BLOCK:(2 * j + 1) * BLOCK, :]
        odd = stacked[(2 * j + 1) * BLOCK:(2 * j + 2) * BLOCK, :]
        lo = _swap_halves(even) if g else even
        hi = odd if g else _swap_halves(odd)
        c0 = ((GROUP * g) // 2 + j) * 128
        tile = jnp.where(lane < HEAD_DIM, lo, hi)
        dst_ref[pl.ds(r0, BLOCK), c0:c0 + 128] = tile.astype(dst_ref.dtype)
        tiles.append((c0, tile))
    return tiles


def _qkv_specs(tq):
    nb = tq // BLOCK
    return [
        pl.BlockSpec((tq, ATTN_W), lambda i: (i, 0)),
        pl.BlockSpec((BLOCK, KV_W), lambda i: (jnp.maximum(i * nb - 1, 0), ATTN_W // KV_W)),
        pl.BlockSpec((tq, KV_W), lambda i: (i, ATTN_W // KV_W)),
        pl.BlockSpec((BLOCK, KV_W), lambda i: (jnp.maximum(i * nb - 1, 0), ATTN_W // KV_W + 1)),
        pl.BlockSpec((tq, KV_W), lambda i: (i, ATTN_W // KV_W + 1)),
    ]


def attn_fwd(qkv, sinks, *, tq=512):
    T = qkv.shape[0]
    nb = tq // BLOCK

    def body(sink_ref, bias_ref, q_ref, kp_ref, kc_ref, vp_ref, vc_ref, o_ref, lse_ref, kext, vext, qs):
        i = pl.program_id(0)
        kext[0:BLOCK, :] = kp_ref[...]
        kext[BLOCK:, :] = kc_ref[...]
        vext[0:BLOCK, :] = vp_ref[...]
        vext[BLOCK:, :] = vc_ref[...]
        lane_l = lax.broadcasted_iota(jnp.int32, (BLOCK, 128), 1)

        def blk(b, carry):
            r0 = pl.multiple_of(b * BLOCK, BLOCK)
            first = jnp.logical_and(i == 0, b == 0).astype(jnp.int32)
            kc = kext[pl.ds(r0, 2 * BLOCK), :]
            vc = vext[pl.ds(r0, 2 * BLOCK), :]
            lse_t = jnp.zeros((BLOCK, 128), F32)
            for g in range(N_KV):
                heads = range(GROUP * g, GROUP * (g + 1))
                _stack_heads(qs.at[g], q_ref, r0, g, SCALE)
                s = _dot_nt(qs[g], kc) + bias_ref[first, g]
                sink = _per_head_column([sink_ref[h] for h in heads])
                m = jnp.maximum(jnp.max(s, axis=-1, keepdims=True), sink)
                p = jnp.exp(s - m)
                denom = jnp.sum(p, axis=-1, keepdims=True) + jnp.exp(sink - m)
                p = p / denom
                _unstack_heads(o_ref, _dot(p.astype(CDT), vc), r0, g)
                lse = m + jnp.log(denom)
                for i_h, h in enumerate(heads):
                    lse_t = jnp.where(lane_l == h, lse[i_h * BLOCK:(i_h + 1) * BLOCK, :], lse_t)
            lse_ref[pl.ds(r0, BLOCK), :] = lse_t
            return carry

        lax.fori_loop(0, nb, blk, 0)

    return pl.pallas_call(
        body, name="attn_fwd", grid=(T // tq,),
        in_specs=[pl.BlockSpec(memory_space=pltpu.SMEM), _const((2, N_KV, STACK, 2 * BLOCK), 1)] + _qkv_specs(tq),
        out_specs=[pl.BlockSpec((tq, ATTN_W), lambda i: (i, 0)), pl.BlockSpec((tq, 128), lambda i: (i, 0))],
        out_shape=[jax.ShapeDtypeStruct((T, ATTN_W), CDT), jax.ShapeDtypeStruct((T, 128), F32)],
        scratch_shapes=[pltpu.VMEM((tq + BLOCK, KV_W), CDT), pltpu.VMEM((tq + BLOCK, KV_W), CDT),
                        pltpu.VMEM((N_KV, STACK, 2 * HEAD_DIM), CDT)],
        compiler_params=_cp("parallel"),
    )(sinks, _score_bias(), qkv, qkv, qkv, qkv, qkv)


def _halo_before(tm, width, col):
    return pl.BlockSpec((HALO, width), lambda i: (jnp.maximum(i * (tm // HALO) - 1, 0), col))


def _fill_u0(ext, a_ref, b_ref, ha_ref, hb_ref, first):
    hu = ha_ref[...] * _sig(hb_ref[...])
    ext[0:HALO, :] = jnp.where(first, jnp.zeros_like(hu), hu)
    ext[HALO:, :] = a_ref[...] * _sig(b_ref[...])


def _shifted_taps(src, w_ref, base, rc, offsets):
    acc = jnp.zeros((rc, CONV_C), F32)
    for b in range(SUBLANES):
        taps = [(k, o - b) for k, o in enumerate(offsets) if o % SUBLANES == b]
        if not taps:
            continue
        rows = rc if b == 0 else rc + SUBLANES
        part = jnp.zeros((rows, CONV_C), F32)
        for k, o8 in taps:
            part = part + w_ref[k:k + 1, :] * src[base + o8:base + o8 + rows, :]
        acc = acc + (part if b == 0 else part[b:b + rc, :])
    return acc


def _shift_copies(dst, src, rows):
    for b in range(1, SUBLANES):
        for r in range(0, rows, 64):
            n = min(64, rows - r)
            dst[b - 1, r:r + n, :] = src[r + b:r + b + n, :]


def _window(src, copies, off, r0, n):
    b = off % SUBLANES
    a = r0 + off - b
    return src[a:a + n, :] if b == 0 else copies[b - 1, a:a + n, :]


def _conv_rows(ext, w_ref, r0, rc):
    return _shifted_taps(ext, w_ref, r0, rc, [HALO - (CONV_K - 1) + k for k in range(CONV_K)])


def _layer_norm(u1, g, b):
    mu = jnp.mean(u1, axis=-1, keepdims=True)
    xc = u1 - mu
    rstd = lax.rsqrt(jnp.mean(xc * xc, axis=-1, keepdims=True) + EPS)
    n = xc * rstd
    return n, rstd, n * g + b


CONV_RC = 32
CONV_RC_1PASS = 64


def conv_fwd(rest, cw, cb, lg, lb, *, tm=512):
    T = rest.shape[0]

    def body(a_ref, b_ref, ha_ref, hb_ref, w_ref, cb_ref, lg_ref, lb_ref, o_ref, u1_ref, ext):
        _fill_u0(ext, a_ref, b_ref, ha_ref, hb_ref, pl.program_id(0) == 0)
        rc = CONV_RC_1PASS
        for r0 in range(0, tm, rc):
            u1 = _conv_rows(ext, w_ref, r0, rc) + cb_ref[...]
            u1_ref[r0:r0 + rc, :] = u1
            _, _, u2 = _layer_norm(u1, lg_ref[...], lb_ref[...])
            o_ref[r0:r0 + rc, :] = (u2 * _sig(u2)).astype(o_ref.dtype)

    row = lambda i: (i, 0)
    return pl.pallas_call(
        body, name="conv_fwd", grid=(T // tm,),
        in_specs=[pl.BlockSpec((tm, CONV_C), row), pl.BlockSpec((tm, CONV_C), lambda i: (i, 1)),
                  _halo_before(tm, CONV_C, 0), _halo_before(tm, CONV_C, 1),
                  _const((CONV_K, CONV_C)), _const((1, CONV_C)), _const((1, CONV_C)), _const((1, CONV_C))],
        out_specs=[pl.BlockSpec((tm, CONV_C), row), pl.BlockSpec((tm, CONV_C), row)],
        out_shape=[jax.ShapeDtypeStruct((T, CONV_C), CDT), jax.ShapeDtypeStruct((T, CONV_C), F32)],
        scratch_shapes=[pltpu.VMEM((tm + HALO, CONV_C), F32)],
        compiler_params=_cp("parallel"),
    )(rest, rest, rest, rest, cw, cb, lg, lb)


def merge_out(x, attn, u3, rest, wa, wc, bc, wo, *, tm=512):
    T = x.shape[0]

    def body(x_ref, at_ref, u_ref, ga_ref, gc_ref, wa_ref, wc_ref, bc_ref, wo_ref, o_ref):
        br_a = _dot(at_ref[...], wa_ref[...])
        br_c = _dot(u_ref[...], wc_ref[...]) + bc_ref[...]
        merged = _sig(ga_ref[...]) * br_a + _sig(gc_ref[...]) * br_c
        o_ref[...] = x_ref[...] + _dot(merged.astype(CDT), wo_ref[...])

    return pl.pallas_call(
        body, name="merge_out", grid=(T // tm,),
        in_specs=[pl.BlockSpec((tm, D), lambda i: (i, 0)), pl.BlockSpec((tm, ATTN_W), lambda i: (i, 0)),
                  pl.BlockSpec((tm, CONV_C), lambda i: (i, 0)),
                  pl.BlockSpec((tm, D), lambda i: (i, 1)), pl.BlockSpec((tm, D), lambda i: (i, 2)),
                  _const((ATTN_W, D), 1), _const((CONV_C, D), 1), _const((1, D)), _const((D, D), 1)],
        out_specs=pl.BlockSpec((tm, D), lambda i: (i, 0)),
        out_shape=jax.ShapeDtypeStruct((T, D), F32),
        compiler_params=_cp("parallel"),
    )(x, attn, u3, rest, rest, wa, wc, bc, wo)


def _loss_and_grad(xv, gv, tgt):
    r = lax.rsqrt(jnp.mean(xv * xv, axis=-1, keepdims=True) + EPS)
    e = xv * r * gv - tgt
    dx, dg_rows = _rms_bwd(xv, gv, e * (1.0 / D))
    return 0.5 * jnp.mean(e * e, axis=-1, keepdims=True), dx, dg_rows


def mlp_fwd(x, g, w1, w2, *, head=None, tm=256, tf=D_FF, rider=None):
    T = x.shape[0]
    nf = D_FF // tf

    def body(x_ref, g_ref, w1_ref, w2_ref, *rest):
        if head is None:
            o_ref, pre_ref, h_s, acc_s = rest
        else:
            gf_ref, t_ref, pre_ref, dy_ref, dgf_ref, loss_ref, h_s, acc_s = rest
        i, f = pl.program_id(0), pl.program_id(1)

        @pl.when(f == 0)
        def _():
            xv = x_ref[...]
            r = lax.rsqrt(jnp.mean(xv * xv, axis=-1, keepdims=True) + EPS)
            h_s[...] = (xv * r * g_ref[...]).astype(CDT)
            acc_s[...] = jnp.zeros_like(acc_s)

        pre = _dot(h_s[...], w1_ref[...])
        pre_ref[...] = pre
        a = jnp.square(jnp.maximum(pre, 0.0))
        acc_s[...] += _dot(a.astype(CDT), w2_ref[...])

        @pl.when(f == nf - 1)
        def _():
            y = x_ref[...] + acc_s[...]
            if head is None:
                o_ref[...] = y
                return

            @pl.when(i == 0)
            def _():
                dgf_ref[...] = jnp.zeros_like(dgf_ref)
                loss_ref[...] = jnp.zeros_like(loss_ref)

            loss_rows, dy, dg_rows = _loss_and_grad(y, gf_ref[...], t_ref[...])
            dy_ref[...] = dy
            dgf_ref[...] += _colsum(dg_rows)
            loss_ref[...] += _colsum(loss_rows)

    mode = {"pipeline_mode": pl.Buffered(1)} if nf == 1 else {}
    row = pl.BlockSpec((tm, D), lambda i, f: (i, 0))
    pre_spec, pre_shape = pl.BlockSpec((tm, tf), lambda i, f: (i, f)), jax.ShapeDtypeStruct((T, D_FF), F32)
    in_specs = [row, _const((1, D)), pl.BlockSpec((D, tf), lambda i, f: (0, f), **mode),
                pl.BlockSpec((tf, D), lambda i, f: (f, 0), **mode)]
    scratch = [pltpu.VMEM((tm, D), CDT), pltpu.VMEM((tm, D), F32)]
    if head is None:
        return _run(body, rider, name="mlp_fwd", grid=(T // tm, nf), in_specs=in_specs, out_specs=[row, pre_spec],
                    out_shape=[jax.ShapeDtypeStruct((T, D), F32), pre_shape], scratch_shapes=scratch,
                    sem=("parallel", "arbitrary"), args=(x, g, w1, w2))
    return _run(body, rider, name="mlp_fwd_loss", grid=(T // tm, nf), in_specs=in_specs + [_const((1, D)), row],
                out_specs=[pre_spec, row, _const((1, D)), _const((1, 128))],
                out_shape=[pre_shape, jax.ShapeDtypeStruct((T, D), F32), jax.ShapeDtypeStruct((1, D), F32),
                           jax.ShapeDtypeStruct((1, 128), F32)],
                scratch_shapes=scratch, sem=("arbitrary", "arbitrary"), args=(x, g, w1, w2) + tuple(head))


def _rms_bwd(xv, g, dh):
    r = lax.rsqrt(jnp.mean(xv * xv, axis=-1, keepdims=True) + EPS)
    xhat = xv * r
    dxh = dh * g
    dx = r * (dxh - xhat * jnp.mean(dxh * xhat, axis=-1, keepdims=True))
    return dx, dh * xhat


def mlp_bwd(dy, x, g, pre, w1, w2, *, tm=256, tf=D_FF, rider=None):
    T = x.shape[0]
    nf = D_FF // tf

    def body(dy_ref, x_ref, g_ref, pre_ref, w1_ref, w2_ref, dx_ref, dg_ref, h_ref, a_ref, dpre_ref, dyb_ref, dyb_s, acc_s):
        i, f = pl.program_id(0), pl.program_id(1)

        @pl.when(jnp.logical_and(i == 0, f == 0))
        def _():
            dg_ref[...] = jnp.zeros_like(dg_ref)

        @pl.when(f == 0)
        def _():
            dyb_s[...] = dy_ref[...].astype(CDT)
            dyb_ref[...] = dyb_s[...]
            acc_s[...] = jnp.zeros_like(acc_s)

        pre = pre_ref[...]
        rl = jnp.maximum(pre, 0.0)
        a_ref[...] = (rl * rl).astype(CDT)
        da = _dot_nt(dyb_s[...], w2_ref[...])
        dpre = (da * (2.0 * rl)).astype(CDT)
        dpre_ref[...] = dpre
        acc_s[...] += _dot_nt(dpre, w1_ref[...])

        @pl.when(f == nf - 1)
        def _():
            xv = x_ref[...]
            gv = g_ref[...]
            dxn, dg_rows = _rms_bwd(xv, gv, acc_s[...])
            dx_ref[...] = dy_ref[...] + dxn
            dg_ref[...] += _colsum(dg_rows)
            r = lax.rsqrt(jnp.mean(xv * xv, axis=-1, keepdims=True) + EPS)
            h_ref[...] = (xv * r * gv).astype(CDT)

    row = lambda i, f: (i, 0)
    mode = {"pipeline_mode": pl.Buffered(1)} if nf == 1 else {}
    return _run(
        body, rider, name="mlp_bwd", grid=(T // tm, nf),
        in_specs=[pl.BlockSpec((tm, D), row), pl.BlockSpec((tm, D), row), _const((1, D)),
                  pl.BlockSpec((tm, tf), lambda i, f: (i, f)),
                  pl.BlockSpec((D, tf), lambda i, f: (0, f), **mode), pl.BlockSpec((tf, D), lambda i, f: (f, 0), **mode)],
        out_specs=[pl.BlockSpec((tm, D), row), _const((1, D)), pl.BlockSpec((tm, D), row),
                   pl.BlockSpec((tm, tf), lambda i, f: (i, f)), pl.BlockSpec((tm, tf), lambda i, f: (i, f)),
                   pl.BlockSpec((tm, D), row)],
        out_shape=[jax.ShapeDtypeStruct((T, D), F32), jax.ShapeDtypeStruct((1, D), F32),
                   jax.ShapeDtypeStruct((T, D), CDT), jax.ShapeDtypeStruct((T, D_FF), CDT),
                   jax.ShapeDtypeStruct((T, D_FF), CDT), jax.ShapeDtypeStruct((T, D), CDT)],
        scratch_shapes=[pltpu.VMEM((tm, D), CDT), pltpu.VMEM((tm, D), F32)],
        sem=("arbitrary", "arbitrary"), args=(dy, x, g, pre, w1, w2))


def tn_matmul(a, b, *, tm, tn, tk=2048, name, by_chip=False, rider=None):
    T, M = a.shape
    N = b.shape[1]
    tk = min(tk, T)
    nk = T // tk
    split = by_chip and tn == N
    shard = N // N_CHIPS

    def body(a_ref, b_ref, o_ref):
        @pl.when(pl.program_id(2) == 0)
        def _():
            o_ref[...] = jnp.zeros_like(o_ref)

        prod = _dot_tn(a_ref[...].astype(CDT), b_ref[...].astype(CDT))
        if split:
            for j in range(N_CHIPS):
                o_ref[j] += prod[:, j * shard:(j + 1) * shard]
        else:
            o_ref[...] += prod

    if split:
        out_spec = pl.BlockSpec((N_CHIPS, tm, shard), lambda i, j, k: (0, i, 0))
        out_shape = jax.ShapeDtypeStruct((N_CHIPS, M, shard), F32)
    elif by_chip:
        out_spec = pl.BlockSpec((None, tm, tn), lambda i, j, k: (j, i, 0))
        out_shape = jax.ShapeDtypeStruct((N // tn, M, tn), F32)
    else:
        out_spec = pl.BlockSpec((tm, tn), lambda i, j, k: (i, j))
        out_shape = jax.ShapeDtypeStruct((M, N), F32)
    res = _run(
        body, rider, name=name, grid=(M // tm, N // tn, nk),
        in_specs=[pl.BlockSpec((tk, tm), lambda i, j, k: (k, i)), pl.BlockSpec((tk, tn), lambda i, j, k: (k, j))],
        out_specs=[out_spec], out_shape=[out_shape], sem=("parallel", "parallel", "arbitrary"), args=(a, b))
    return res[0] if rider is None else (res[0][0], res[1])


def merge_bwd(dx1, attn, u3, rest, wa, wc, bc, wo, *, tm=512):
    T = dx1.shape[0]

    def body(dx_ref, at_ref, u_ref, ga_ref, gc_ref, wa_ref, wc_ref, bc_ref, wo_ref,
             mg_ref, dba_ref, dbc_ref, dat_ref, du_ref, dgate_ref, dgsum_ref, dbias_ref):
        @pl.when(pl.program_id(0) == 0)
        def _():
            dbias_ref[...] = jnp.zeros_like(dbias_ref)
            dgsum_ref[...] = jnp.zeros_like(dgsum_ref)

        br_a = _dot(at_ref[...], wa_ref[...])
        br_c = _dot(u_ref[...], wc_ref[...]) + bc_ref[...]
        sa = _sig(ga_ref[...])
        sc = _sig(gc_ref[...])
        mg_ref[...] = (sa * br_a + sc * br_c).astype(CDT)
        dm = _dot_nt(dx_ref[...].astype(CDT), wo_ref[...])
        dba = dm * sa
        dbc = dm * sc
        dga = dm * br_a * sa * (1.0 - sa)
        dgc = dm * br_c * sc * (1.0 - sc)
        dgate_ref[:, 0:D] = dga.astype(CDT)
        dgate_ref[:, D:2 * D] = dgc.astype(CDT)
        dgsum_ref[:, 0:D] += _colsum(dga)
        dgsum_ref[:, D:2 * D] += _colsum(dgc)
        dbias_ref[...] += _colsum(dbc)
        dba_b = dba.astype(CDT)
        dbc_b = dbc.astype(CDT)
        dba_ref[...] = dba_b
        dbc_ref[...] = dbc_b
        dat_ref[...] = _dot_nt(dba_b, wa_ref[...]).astype(CDT)
        du_ref[...] = _dot_nt(dbc_b, wc_ref[...])

    row = lambda i: (i, 0)
    return pl.pallas_call(
        body, name="merge_bwd", grid=(T // tm,),
        in_specs=[pl.BlockSpec((tm, D), row), pl.BlockSpec((tm, ATTN_W), row), pl.BlockSpec((tm, CONV_C), row),
                  pl.BlockSpec((tm, D), lambda i: (i, 1)), pl.BlockSpec((tm, D), lambda i: (i, 2)),
                  _const((ATTN_W, D), 1), _const((CONV_C, D), 1), _const((1, D)), _const((D, D), 1)],
        out_specs=[pl.BlockSpec((tm, D), row), pl.BlockSpec((tm, D), row), pl.BlockSpec((tm, D), row),
                   pl.BlockSpec((tm, ATTN_W), row), pl.BlockSpec((tm, CONV_C), row),
                   pl.BlockSpec((tm, 2 * D), row), _const((1, 2 * D)), _const((1, D))],
        out_shape=[jax.ShapeDtypeStruct((T, D), CDT), jax.ShapeDtypeStruct((T, D), CDT),
                   jax.ShapeDtypeStruct((T, D), CDT), jax.ShapeDtypeStruct((T, ATTN_W), CDT),
                   jax.ShapeDtypeStruct((T, CONV_C), F32), jax.ShapeDtypeStruct((T, 2 * D), CDT),
                   jax.ShapeDtypeStruct((1, 2 * D), F32), jax.ShapeDtypeStruct((1, D), F32)],
        compiler_params=_cp("arbitrary"),
    )(dx1, attn, u3, rest, rest, wa, wc, bc, wo)


def conv_bwd_ln(du3, u1, lg, lb, *, tm=512, rider=None):
    T = du3.shape[0]

    def body(du_ref, u1_ref, lg_ref, lb_ref, du1_ref, dlg_ref, dlb_ref, dcb_ref):
        @pl.when(pl.program_id(0) == 0)
        def _():
            dlg_ref[...] = jnp.zeros_like(dlg_ref)
            dlb_ref[...] = jnp.zeros_like(dlb_ref)
            dcb_ref[...] = jnp.zeros_like(dcb_ref)

        dlg = jnp.zeros((1, CONV_C), F32)
        dlb = jnp.zeros((1, CONV_C), F32)
        dcb = jnp.zeros((1, CONV_C), F32)
        rc = CONV_RC_1PASS
        for r0 in range(0, tm, rc):
            n, rstd, u2 = _layer_norm(u1_ref[r0:r0 + rc, :], lg_ref[...], lb_ref[...])
            s = _sig(u2)
            du2 = du_ref[r0:r0 + rc, :] * (s + u2 * s * (1.0 - s))
            dn = du2 * lg_ref[...]
            du1 = rstd * (dn - jnp.mean(dn, axis=-1, keepdims=True) - n * jnp.mean(dn * n, axis=-1, keepdims=True))
            du1_ref[r0:r0 + rc, :] = du1
            dlg = dlg + _colsum(du2 * n)
            dlb = dlb + _colsum(du2)
            dcb = dcb + _colsum(du1)
        dlg_ref[...] += dlg
        dlb_ref[...] += dlb
        dcb_ref[...] += dcb

    row = lambda i: (i, 0)
    vec = jax.ShapeDtypeStruct((1, CONV_C), F32)
    return _run(
        body, rider, name="conv_bwd_ln", grid=(T // tm,),
        in_specs=[pl.BlockSpec((tm, CONV_C), row), pl.BlockSpec((tm, CONV_C), row), _const((1, CONV_C)), _const((1, CONV_C))],
        out_specs=[pl.BlockSpec((tm, CONV_C), row), _const((1, CONV_C)), _const((1, CONV_C)), _const((1, CONV_C))],
        out_shape=[jax.ShapeDtypeStruct((T, CONV_C), F32), vec, vec, vec],
        sem=("arbitrary",), args=(du3, u1, lg, lb))


def conv_bwd_taps(du1, rest, cw, *, tm=512):
    T = du1.shape[0]
    nt = T // tm

    def body(d_ref, hd_ref, a_ref, b_ref, ha_ref, hb_ref, w_ref, dglu_ref, dgsum_ref, dw_ref, ext, dext, dcopies, dwacc):
        i = pl.program_id(0)

        @pl.when(i == 0)
        def _():
            dwacc[...] = jnp.zeros_like(dwacc)
            dgsum_ref[...] = jnp.zeros_like(dgsum_ref)

        rc = CONV_RC
        groups = lambda v: jnp.sum(v.reshape(v.shape[0] // SUBLANES, SUBLANES, CONV_C), axis=0)
        sums = [jnp.zeros((SUBLANES, CONV_C), F32), jnp.zeros((SUBLANES, CONV_C), F32)]
        _fill_u0(ext, a_ref, b_ref, ha_ref, hb_ref, i == 0)
        dext[0:SUBLANES, :] = jnp.zeros((SUBLANES, CONV_C), F32)
        dext[SUBLANES:SUBLANES + tm, :] = d_ref[...]
        hd = hd_ref[...]
        dext[SUBLANES + tm:, :] = jnp.where(i == nt - 1, jnp.zeros_like(hd), hd)
        _shift_copies(dcopies, dext, tm + HALO)
        for r0 in range(0, tm, rc):
            du0 = jnp.zeros((rc, CONV_C), F32)
            for k in range(CONV_K):
                du0 = du0 + w_ref[k:k + 1, :] * _window(dext, dcopies, SUBLANES + CONV_K - 1 - k, r0, rc)
            av = a_ref[r0:r0 + rc, :]
            sb = _sig(b_ref[r0:r0 + rc, :])
            for half, dg in enumerate((du0 * sb, du0 * av * sb * (1.0 - sb))):
                dglu_ref[r0:r0 + rc, half * CONV_C:(half + 1) * CONV_C] = dg.astype(CDT)
                sums[half] = sums[half] + groups(dg)
        for half in range(2):
            dgsum_ref[:, half * CONV_C:(half + 1) * CONV_C] += _colsum(sums[half])
        tail = lax.broadcasted_iota(jnp.int32, (SUBLANES, CONV_C), 0)
        for b in range(SUBLANES):
            taps = [(k, HALO - (CONV_K - 1) + k - b) for k in range(CONV_K) if (HALO - (CONV_K - 1) + k) % SUBLANES == b]
            accs = [jnp.zeros((SUBLANES, CONV_C), F32) for _ in taps]
            for r0 in list(range(0, tm, rc)) + ([tm] if b else []):
                n = rc if r0 < tm else SUBLANES
                dwin = _window(dext, dcopies, SUBLANES - b, r0, n)
                if r0 == tm:
                    dwin = jnp.where(tail < b, dwin, 0.0)
                for j, (k, o8) in enumerate(taps):
                    accs[j] = accs[j] + groups(dwin * ext[r0 + o8:r0 + o8 + n, :])
            for j, (k, _) in enumerate(taps):
                dwacc[8 * k:8 * k + 8, :] += accs[j]

        @pl.when(i == nt - 1)
        def _():
            dw_ref[...] = jnp.zeros_like(dw_ref)
            for k in range(CONV_K):
                dw_ref[k:k + 1, :] = _colsum(dwacc[8 * k:8 * k + 8, :])

    row = lambda i: (i, 0)
    return pl.pallas_call(
        body, name="conv_bwd_taps", grid=(nt,),
        in_specs=[pl.BlockSpec((tm, CONV_C), row),
                  pl.BlockSpec((HALO, CONV_C), lambda i: (jnp.minimum((i + 1) * (tm // HALO), T // HALO - 1), 0)),
                  pl.BlockSpec((tm, CONV_C), row), pl.BlockSpec((tm, CONV_C), lambda i: (i, 1)),
                  _halo_before(tm, CONV_C, 0), _halo_before(tm, CONV_C, 1), _const((CONV_K, CONV_C))],
        out_specs=[pl.BlockSpec((tm, 2 * CONV_C), row), _const((1, 2 * CONV_C)), _const((HALO, CONV_C))],
        out_shape=[jax.ShapeDtypeStruct((T, 2 * CONV_C), CDT), jax.ShapeDtypeStruct((1, 2 * CONV_C), F32),
                   jax.ShapeDtypeStruct((HALO, CONV_C), F32)],
        scratch_shapes=[pltpu.VMEM((tm + HALO, CONV_C), F32), pltpu.VMEM((SUBLANES + tm + HALO, CONV_C), F32),
                        pltpu.VMEM((SUBLANES - 1, tm + HALO, CONV_C), F32), pltpu.VMEM((8 * CONV_K, CONV_C), F32)],
        compiler_params=_cp("arbitrary"),
    )(du1, du1, rest, rest, rest, rest, cw)


def attn_bwd(qkv, do, lse, sinks, *, tq=512, rider=None):
    T = qkv.shape[0]
    nb = tq // BLOCK

    def body(sink_ref, bias_ref, q_ref, kp_ref, kc_ref, vp_ref, vc_ref, do_ref, lse_ref,
             dq_ref, dqsum_ref, dkv_ref, spill_ref, dsink_ref, kext, vext, dkext, dvext, qs, dos):
        i = pl.program_id(0)

        @pl.when(i == 0)
        def _():
            dsink_ref[...] = jnp.zeros_like(dsink_ref)
            dqsum_ref[...] = jnp.zeros_like(dqsum_ref)

        kext[0:BLOCK, :] = kp_ref[...]
        kext[BLOCK:, :] = kc_ref[...]
        vext[0:BLOCK, :] = vp_ref[...]
        vext[BLOCK:, :] = vc_ref[...]
        dkext[...] = jnp.zeros_like(dkext)
        dvext[...] = jnp.zeros_like(dvext)
        lane_l = lax.broadcasted_iota(jnp.int32, (BLOCK, 128), 1)
        lane_k = lax.broadcasted_iota(jnp.int32, (2 * BLOCK, KV_W), 1)

        def blk(b, dsink):
            r0 = pl.multiple_of(b * BLOCK, BLOCK)
            first = jnp.logical_and(i == 0, b == 0).astype(jnp.int32)
            kc = kext[pl.ds(r0, 2 * BLOCK), :]
            vc = vext[pl.ds(r0, 2 * BLOCK), :]
            lse_t = lse_ref[pl.ds(r0, BLOCK), :]
            dk = jnp.zeros((2 * BLOCK, KV_W), F32)
            dv = jnp.zeros((2 * BLOCK, KV_W), F32)
            for g in range(N_KV):
                heads = range(GROUP * g, GROUP * (g + 1))
                _stack_heads(qs.at[g], q_ref, r0, g, SCALE)
                _stack_heads(dos.at[g], do_ref, r0, g)
                qv = qs[g]
                dov = dos[g]
                s = _dot_nt(qv, kc) + bias_ref[first, g]
                lse = jnp.concatenate(
                    [jnp.sum(jnp.where(lane_l == h, lse_t, 0.0), axis=-1, keepdims=True) for h in heads], axis=0)
                p = jnp.exp(s - lse)
                dp = _dot_nt(dov, vc)
                dd = jnp.sum(p * dp, axis=-1, keepdims=True)
                ds = (p * (dp - dd)).astype(CDT)
                keep = (lane_k >= HEAD_DIM) if g else (lane_k < HEAD_DIM)
                for c0, tile in _unstack_heads(dq_ref, _dot(ds, jnp.where(keep, kc, jnp.zeros_like(kc))) * SCALE, r0, g):
                    dqsum_ref[:, c0:c0 + 128] += _colsum(tile)
                dk = dk + _dot_tn(ds, qv)
                dv = dv + _dot_tn(p.astype(CDT), dov)
                wsink = jnp.exp(_per_head_column([sink_ref[h] for h in heads]) - lse) * dd
                for i_h, h in enumerate(heads):
                    part = jnp.sum(wsink[i_h * BLOCK:(i_h + 1) * BLOCK, :], axis=0, keepdims=True)
                    dsink = dsink - jnp.where(lane_l[0:1, :] == h, part, 0.0)
            dkext[pl.ds(r0, 2 * BLOCK), :] += dk
            dvext[pl.ds(r0, 2 * BLOCK), :] += dv
            return dsink

        dsink_ref[...] += lax.fori_loop(0, nb, blk, jnp.zeros((1, 128), F32))
        dkv_ref[:, 0:KV_W] = dkext[BLOCK:, :]
        dkv_ref[:, KV_W:2 * KV_W] = dvext[BLOCK:, :]
        spill_ref[:, 0:KV_W] = dkext[0:BLOCK, :]
        spill_ref[:, KV_W:2 * KV_W] = dvext[0:BLOCK, :]

    row = lambda i: (i, 0)
    return _run(
        body, rider, name="attn_bwd", grid=(T // tq,),
        in_specs=[pl.BlockSpec(memory_space=pltpu.SMEM), _const((2, N_KV, STACK, 2 * BLOCK), 1)] + _qkv_specs(tq)
        + [pl.BlockSpec((tq, ATTN_W), row), pl.BlockSpec((tq, 128), row)],
        out_specs=[pl.BlockSpec((tq, ATTN_W), row), _const((1, ATTN_W)), pl.BlockSpec((tq, 2 * KV_W), row),
                   pl.BlockSpec((BLOCK, 2 * KV_W), row), _const((1, 128))],
        out_shape=[jax.ShapeDtypeStruct((T, ATTN_W), CDT), jax.ShapeDtypeStruct((1, ATTN_W), F32),
                   jax.ShapeDtypeStruct((T, 2 * KV_W), F32),
                   jax.ShapeDtypeStruct((T // tq * BLOCK, 2 * KV_W), F32), jax.ShapeDtypeStruct((1, 128), F32)],
        scratch_shapes=[pltpu.VMEM((tq + BLOCK, KV_W), CDT), pltpu.VMEM((tq + BLOCK, KV_W), CDT),
                        pltpu.VMEM((tq + BLOCK, KV_W), F32), pltpu.VMEM((tq + BLOCK, KV_W), F32),
                        pltpu.VMEM((N_KV, STACK, 2 * HEAD_DIM), CDT), pltpu.VMEM((N_KV, STACK, 2 * HEAD_DIM), CDT)],
        sem=("arbitrary",), args=(sinks, _score_bias(), qkv, qkv, qkv, qkv, qkv, do, lse))


def inproj_bwd(dres, x, g, w, dq, dkv, spill, dglu, dgate, sums, *, tm=512):
    T = x.shape[0]
    nt = T // tm
    pieces = ((0, ATTN_W), (QKV_W, 2 * CONV_C), (QKV_W + 2 * CONV_C, 2 * D))

    def body(dr_ref, x_ref, g_ref, w_ref, dq_ref, dkv_ref, sp_ref, dglu_ref, dgate_ref, sq_ref, sglu_ref, sgate_ref,
             dx_ref, dp_ref, h_ref, dg_ref, db_ref):
        i = pl.program_id(0)

        @pl.when(i == 0)
        def _():
            dg_ref[...] = jnp.zeros_like(dg_ref)
            db_ref[:, ATTN_W:QKV_W] = jnp.zeros((1, QKV_W - ATTN_W), F32)
            for (c0, wd), s_ref in zip(pieces, (sq_ref, sglu_ref, sgate_ref)):
                db_ref[:, c0:c0 + wd] = s_ref[...]

        sp = sp_ref[...]
        sp = jnp.where(i == nt - 1, jnp.zeros_like(sp), sp)
        dkv = dkv_ref[...]
        db_ref[:, ATTN_W:QKV_W] += _colsum(dkv) + _colsum(sp)
        dp_ref[0:tm - BLOCK, ATTN_W:QKV_W] = dkv[0:tm - BLOCK, :].astype(CDT)
        dp_ref[tm - BLOCK:tm, ATTN_W:QKV_W] = (dkv[tm - BLOCK:tm, :] + sp).astype(CDT)
        for (c0, wd), ref in zip(pieces, (dq_ref, dglu_ref, dgate_ref)):
            dp_ref[:, c0:c0 + wd] = ref[...]
        dh = _dot(dp_ref[...], w_ref[...])
        xv = x_ref[...]
        gv = g_ref[...]
        dxn, dg_rows = _rms_bwd(xv, gv, dh)
        dx_ref[...] = dr_ref[...] + dxn
        dg_ref[...] += _colsum(dg_rows)
        r = lax.rsqrt(jnp.mean(xv * xv, axis=-1, keepdims=True) + EPS)
        h_ref[...] = (xv * r * gv).astype(CDT)

    row = lambda i: (i, 0)
    return pl.pallas_call(
        body, name="inproj_bwd", grid=(nt,),
        in_specs=[pl.BlockSpec((tm, D), row), pl.BlockSpec((tm, D), row), _const((1, D)), _const((IN_W, D), 1),
                  pl.BlockSpec((tm, ATTN_W), row), pl.BlockSpec((tm, 2 * KV_W), row),
                  pl.BlockSpec((BLOCK, 2 * KV_W), lambda i: (jnp.minimum(i + 1, nt - 1), 0)),
                  pl.BlockSpec((tm, 2 * CONV_C), row), pl.BlockSpec((tm, 2 * D), row)]
        + [_const((1, wd)) for _, wd in pieces],
        out_specs=[pl.BlockSpec((tm, D), row), pl.BlockSpec((tm, IN_W), row), pl.BlockSpec((tm, D), row),
                   _const((1, D)), _const((1, IN_W))],
        out_shape=[jax.ShapeDtypeStruct((T, D), F32), jax.ShapeDtypeStruct((T, IN_W), CDT),
                   jax.ShapeDtypeStruct((T, D), CDT), jax.ShapeDtypeStruct((1, D), F32),
                   jax.ShapeDtypeStruct((1, IN_W), F32)],
        compiler_params=_cp("arbitrary"),
    )(dres, x, g, w, dq, dkv, spill, dglu, dgate, *sums)


ATTN_TILE = 512
MATRICES = ("w_in", "w_attn_proj", "w_conv_proj", "w_out", "w_mlp1", "w_mlp2")
SMALL = ("mix_norm_g", "b_in", "sinks", "conv_w", "conv_b", "conv_ln_g", "conv_ln_b", "b_conv_proj", "mlp_norm_g")


def forward_backward(x, tgt, hooks):
    def call(fn, kernel, l, *args, **kw):
        rider = hooks.rider(kernel, l)
        if rider is None:
            return fn(*args, **kw)
        outs, landed = fn(*args, rider=rider, **kw)
        hooks.landed(kernel, l, landed)
        return outs

    vec = hooks.vec
    saved = []
    for l in range(DEPTH):
        qkv, rest = call(rms_inproj, "rms_inproj", l, x, vec("mix_norm_g", l), hooks.w_in(l), vec("b_in", l))
        m = hooks.mats(l)
        attn, lse = attn_fwd(qkv, hooks.sinks(l), tq=ATTN_TILE)
        u3, u1 = conv_fwd(rest, hooks.taps(l), vec("conv_b", l), vec("conv_ln_g", l), vec("conv_ln_b", l))
        x1 = merge_out(x, attn, u3, rest, m["w_attn_proj"], m["w_conv_proj"], vec("b_conv_proj", l), m["w_out"])
        if l < DEPTH - 1:
            x_next, pre = call(mlp_fwd, "mlp_fwd", l, x1, vec("mlp_norm_g", l), m["w_mlp1"], m["w_mlp2"])
        else:
            x_next = None
            pre, dx, dgf, loss = call(mlp_fwd, "mlp_fwd", l, x1, vec("mlp_norm_g", l), m["w_mlp1"], m["w_mlp2"],
                                      head=(hooks.final_g, tgt))
        saved.append((x, qkv, rest, attn, lse, u3, u1, x1, pre))
        x = x_next
    small = {n: [None] * DEPTH for n in SMALL}
    small["final_norm_g"] = dgf
    for l in reversed(range(DEPTH)):
        x0, qkv, rest, attn, lse, u3, u1, x1, pre = saved[l]
        m = hooks.mats(l)
        dx1, dg2, h2, a, dpre, dxb = call(mlp_bwd, "mlp_bwd", l, dx, x1, vec("mlp_norm_g", l), pre, m["w_mlp1"], m["w_mlp2"])
        small["mlp_norm_g"][l] = dg2
        group = {}
        group["w_mlp1"] = call(tn_matmul, "tn_mlp1", l, h2, dpre, tm=1024, tn=1024, tk=4096, name="tn_mlp1", by_chip=True)
        group["w_mlp2"] = call(tn_matmul, "tn_mlp2", l, a, dxb, tm=1024, tn=1024, tk=4096, name="tn_mlp2")
        merged, dba, dbc, dattn, du3, dgate, dgate_sum, dbcp = merge_bwd(
            dx1, attn, u3, rest, m["w_attn_proj"], m["w_conv_proj"], vec("b_conv_proj", l), m["w_out"])
        small["b_conv_proj"][l] = dbcp
        group["w_out"] = tn_matmul(merged, dx1, tm=1024, tn=1024, name="tn_out")
        group["w_attn_proj"] = tn_matmul(attn, dba, tm=512, tn=D, tk=4096, name="tn_attn_proj", by_chip=True)
        group["w_conv_proj"] = tn_matmul(u3, dbc, tm=512, tn=D, tk=4096, name="tn_conv_proj", by_chip=True)
        hooks.grads(l, "A", group)
        du1, dlg, dlb, dcb = call(conv_bwd_ln, "conv_bwd_ln", l, du3, u1, vec("conv_ln_g", l), vec("conv_ln_b", l))
        small["conv_ln_g"][l], small["conv_ln_b"][l], small["conv_b"][l] = dlg, dlb, dcb
        dglu, dglu_sum, dcw = conv_bwd_taps(du1, rest, hooks.taps(l))
        small["conv_w"][l] = dcw
        dq, dq_sum, dkv, spill, dsink = call(attn_bwd, "attn_bwd", l, qkv, dattn, lse, hooks.sinks(l), tq=ATTN_TILE)
        small["sinks"][l] = dsink
        dx, dproj, h, dg, db = inproj_bwd(dx1, x0, vec("mix_norm_g", l), hooks.w_in(l), dq, dkv, spill, dglu, dgate,
                                          (dq_sum, dglu_sum, dgate_sum), tm=ATTN_TILE)
        small["mix_norm_g"][l], small["b_in"][l] = dg, db
        hooks.grads(l, "B", {"w_in": call(tn_matmul, "tn_in", l, dproj, h, tm=768, tn=1024, tk=4096, name="tn_in")})
    return loss, dx, small


class _LocalHooks:
    def __init__(self, p):
        self.p = p
        self.final_g = p["final_norm_g"]
        self.got = {n: [None] * DEPTH for n in MATRICES}

    def w_in(self, l):
        return self.p["w_in"][l].T

    def mats(self, l):
        return {n: self.p[n][l] for n in MATRICES}

    def vec(self, n, l):
        return self.p[n][l]

    def sinks(self, l):
        return self.p["sinks"][l]

    def taps(self, l):
        return self.p["conv_w"][l]

    def rider(self, kernel, l):
        return None

    def grads(self, l, group, g):
        for n, v in g.items():
            if v.ndim == 3:
                v = v.transpose(1, 0, 2).reshape(v.shape[1], -1)
            self.got[n][l] = v.T if n == "w_in" else v


def local_grads(x, tgt, p):
    hooks = _LocalHooks(p)
    loss, dx, small = forward_backward(x, tgt, hooks)
    small["conv_w"] = [g[0:CONV_K] for g in small["conv_w"]]
    small["sinks"] = [g[0, 0:N_Q] for g in small["sinks"]]
    return loss, dx, {**small, **hooks.got}


MESH = pl.DeviceIdType.MESH
N_CHIPS = 4
N_DEV = 8
FLAT_W = 1024
FLAT_PARTS = (("w_in", 960), ("w_attn_proj", 128), ("w_conv_proj", 128), ("w_out", 256), ("w_mlp1", 1024), ("w_mlp2", 1024))
FLAT_ROWS = sum(r for _, r in FLAT_PARTS)
W_IN_ROWS = FLAT_PARTS[0][1]
GROUP_A = (("w_mlp1", 1024), ("w_mlp2", 1024), ("w_out", 256), ("w_attn_proj", 128), ("w_conv_proj", 128))
COL_SHARDED = ("w_in", "w_attn_proj", "w_conv_proj", "w_mlp1")
FULL_SHAPES = {"w_in": (D, IN_W), "w_attn_proj": (ATTN_W, D), "w_conv_proj": (CONV_C, D), "w_out": (D, D),
               "w_mlp1": (D, D_FF), "w_mlp2": (D_FF, D)}


def _place():
    x, y, c = lax.axis_index("x"), lax.axis_index("y"), lax.axis_index("c")
    return x, y, c, 2 * x + y


def _peer_chips(x, y, j):
    return [((x, 1 - y), j ^ 1), ((1 - x, y), j ^ 2), ((1 - x, 1 - y), j ^ 3)]


def _remote(src, dst, sems, k, n, to):
    return pltpu.make_async_remote_copy(src_ref=src, dst_ref=dst, send_sem=sems.at[k], recv_sem=sems.at[n + k],
                                        device_id=to, device_id_type=MESH)


def _half(c, rows):
    h = rows // 2
    return pl.ds(pl.multiple_of(c * h, 16), h)


def gather_rider(wsh):
    R = wsh.shape[0]
    n = 7

    def plan(rins, routs, sems):
        (w_ref,), (out_ref,) = rins, routs
        x, y, c, j = _place()
        peers = _peer_chips(x, y, j)
        mine, other = _half(c, R), _half(1 - c, R)
        sent = [_remote(w_ref.at[mine], out_ref.at[j, mine], sems, k, n, (*chip, c)) for k, (chip, _) in enumerate(peers)]
        sent.append(_remote(w_ref, out_ref.at[j], sems, 6, n, (x, y, 1 - c)))
        landed = [_remote(w_ref.at[mine], out_ref.at[pj, mine], sems, k, n, (x, y, c)) for k, (_, pj) in enumerate(peers)]
        passed = [_remote(out_ref.at[pj, mine], out_ref.at[pj, mine], sems, 3 + k, n, (x, y, 1 - c))
                  for k, (_, pj) in enumerate(peers)]
        handed = [_remote(w_ref.at[mine], out_ref.at[pj, other], sems, 3 + k, n, (x, y, c)) for k, (_, pj) in enumerate(peers)]
        handed.append(_remote(w_ref, out_ref.at[j], sems, 6, n, (x, y, c)))
        return sent, landed, passed, handed

    def start(rins, routs, sems):
        for cp in plan(rins, routs, sems)[0]:
            cp.start()

    def late(rins, routs, sems):
        _, landed, passed, _ = plan(rins, routs, sems)
        for k in range(3):
            landed[k].wait_recv()
            passed[k].start()

    def finish(rins, routs, sems):
        sent, _, passed, handed = plan(rins, routs, sems)
        for cp in handed:
            cp.wait_recv()
        for cp in sent + passed:
            cp.wait_send()

    return Rider((wsh,), (jax.ShapeDtypeStruct((N_CHIPS,) + wsh.shape, wsh.dtype),), 2 * n, start, finish, late)


def swap_rider(g):
    R = g.shape[1]

    def plan(rins, routs, sems):
        (g_ref,), (got_ref,) = rins, routs
        x, y, c, _ = _place()
        return _remote(g_ref.at[:, _half(1 - c, R), :], got_ref, sems, 0, 1, (x, y, 1 - c))

    def start(rins, routs, sems):
        plan(rins, routs, sems).start()

    def finish(rins, routs, sems):
        plan(rins, routs, sems).wait()

    return Rider((g,), (jax.ShapeDtypeStruct((N_CHIPS, R // 2, FLAT_W), g.dtype),), 2, start, finish)


def exchange_rider(pb):
    def plan(rins, routs, sems):
        (pb_ref,), (got_ref,) = rins, routs
        x, y, c, j = _place()
        peers = _peer_chips(x, y, j)
        sent = [_remote(pb_ref.at[pj], got_ref.at[j], sems, k, 3, (*chip, c)) for k, (chip, pj) in enumerate(peers)]
        landed = [_remote(pb_ref.at[pj], got_ref.at[pj], sems, k, 3, (x, y, c)) for k, (_, pj) in enumerate(peers)]
        return sent, landed

    def start(rins, routs, sems):
        for cp in plan(rins, routs, sems)[0]:
            cp.start()

    def finish(rins, routs, sems):
        sent, landed = plan(rins, routs, sems)
        for cp in landed:
            cp.wait_recv()
        for cp in sent:
            cp.wait_send()

    return Rider((pb,), (jax.ShapeDtypeStruct(pb.shape, pb.dtype),), 6, start, finish)


def share_rider(tot):
    def plan(rins, routs, sems):
        (t_ref,), (got_ref,) = rins, routs
        x, y, c, _ = _place()
        return _remote(t_ref, got_ref, sems, 0, 1, (x, y, 1 - c))

    def start(rins, routs, sems):
        plan(rins, routs, sems).start()

    def finish(rins, routs, sems):
        plan(rins, routs, sems).wait()

    return Rider((tot,), (jax.ShapeDtypeStruct(tot.shape, tot.dtype),), 2, start, finish)


def pair_sum(g, got):
    nj, R, W = g.shape
    h = R // 2
    tile = h // 2

    def body(g_ref, got_ref, pb_ref, own_ref):
        v = g_ref[...] + got_ref[...]
        pb_ref[...] = v.astype(pb_ref.dtype)

        @pl.when(pl.program_id(1) == _place()[3])
        def _():
            own_ref[...] = v

    return pl.pallas_call(
        body, name="pair_sum", grid=(h // tile, nj),
        in_specs=[pl.BlockSpec((None, tile, W), lambda r, j: (j, lax.axis_index("c") * (h // tile) + r, 0)),
                  pl.BlockSpec((None, tile, W), lambda r, j: (j, r, 0))],
        out_specs=[pl.BlockSpec((None, tile, W), lambda r, j: (j, r, 0)), pl.BlockSpec((tile, W), lambda r, j: (r, 0))],
        out_shape=[jax.ShapeDtypeStruct((nj, h, W), CDT), jax.ShapeDtypeStruct((h, W), F32)],
        compiler_params=_cp("arbitrary", "arbitrary"),
    )(g, got)


def total_sum(own, got):
    R, W = own.shape
    tile = R // 2

    def body(own_ref, a_ref, b_ref, c_ref, o_ref):
        o_ref[...] = ((own_ref[...] + a_ref[...].astype(F32)) + b_ref[...].astype(F32)) + c_ref[...].astype(F32)

    def slab(k):
        return pl.BlockSpec((None, tile, W), lambda r: (_place()[3] ^ (k + 1), r, 0))

    return pl.pallas_call(
        body, name="total_sum", grid=(R // tile,),
        in_specs=[pl.BlockSpec((tile, W), lambda r: (r, 0)), slab(0), slab(1), slab(2)],
        out_specs=pl.BlockSpec((tile, W), lambda r: (r, 0)),
        out_shape=jax.ShapeDtypeStruct((R, W), F32),
        compiler_params=_cp("arbitrary"),
    )(own, got, got, got)


def _all_peers(x, y, c):
    return [(x ^ (r >> 2), y ^ ((r >> 1) & 1), c ^ (r & 1)) for r in range(1, N_DEV)]


ROW_ITEMS = (("mix_norm_g", D), ("b_in", IN_W), ("sinks", N_Q), ("conv_b", CONV_C), ("conv_ln_g", CONV_C),
             ("conv_ln_b", CONV_C), ("b_conv_proj", D), ("mlp_norm_g", D))
TAPS_ROW = 16
TAPS_ROWS = 32
LAYER_ROWS = TAPS_ROW + TAPS_ROWS
FINAL_ROW = DEPTH * LAYER_ROWS
SMALL_ROWS = FINAL_ROW + SUBLANES


def _row_chunks():
    out, r = {}, 0
    for n, width in ROW_ITEMS:
        out[n] = [(r + i, FLAT_W * i, min(FLAT_W, width - FLAT_W * i)) for i in range(-(-width // FLAT_W))]
        r += len(out[n])
    assert r <= TAPS_ROW
    return out


def sum_small(gsm):
    chunks = _row_chunks()
    ins = []
    for l in range(DEPTH):
        ins += [gsm[n][l] for n, _ in ROW_ITEMS] + [gsm["conv_w"][l]]
    ins.append(gsm["final_norm_g"])
    n_in = len(ins)

    def body(*refs):
        in_refs, o_ref, buf, send_sems, recv_sems = refs[:n_in], refs[n_in], refs[n_in + 1], refs[n_in + 2], refs[n_in + 3]
        x, y, c, _ = _place()
        me = 4 * x + 2 * y + c
        mine = buf.at[me]
        mine[...] = jnp.zeros((SMALL_ROWS, FLAT_W), F32)
        k = 0
        for l in range(DEPTH):
            for n, _ in ROW_ITEMS:
                for r, c0, wd in chunks[n]:
                    mine[l * LAYER_ROWS + r:l * LAYER_ROWS + r + 1, 0:wd] = in_refs[k][:, c0:c0 + wd]
                k += 1
            mine[l * LAYER_ROWS + TAPS_ROW:(l + 1) * LAYER_ROWS, 0:CONV_C] = in_refs[k][...]
            k += 1
        mine[FINAL_ROW:FINAL_ROW + 1, :] = in_refs[k][...]
        peers = _all_peers(x, y, c)
        sends = [pltpu.make_async_remote_copy(src_ref=mine, dst_ref=mine, send_sem=send_sems.at[r], recv_sem=recv_sems.at[r],
                                              device_id=to, device_id_type=MESH) for r, to in enumerate(peers)]
        for cp in sends:
            cp.start()
        for r in range(N_DEV - 1):
            pltpu.make_async_remote_copy(src_ref=mine, dst_ref=buf.at[me ^ (r + 1)], send_sem=send_sems.at[r],
                                         recv_sem=recv_sems.at[r], device_id=(x, y, c), device_id_type=MESH).wait_recv()
        for cp in sends:
            cp.wait_send()
        acc = buf[0]
        for d in range(1, N_DEV):
            acc = acc + buf[d]
        o_ref[...] = acc

    vm = pl.BlockSpec(memory_space=pltpu.VMEM)
    return pl.pallas_call(
        body, name="sum_small", out_shape=jax.ShapeDtypeStruct((SMALL_ROWS, FLAT_W), F32),
        in_specs=[vm] * n_in, out_specs=vm,
        scratch_shapes=[pltpu.VMEM((N_DEV, SMALL_ROWS, FLAT_W), F32), pltpu.SemaphoreType.DMA((N_DEV - 1,)),
                        pltpu.SemaphoreType.DMA((N_DEV - 1,))],
    )(*ins)


def gather_taps(taps):
    shard = taps.shape[2]

    def body(t_ref, o_ref, buf, send_sems, recv_sems):
        x, y, c, j = _place()
        peers = _peer_chips(x, y, j)
        buf[j] = t_ref[...]
        sends = [pltpu.make_async_remote_copy(src_ref=t_ref, dst_ref=buf.at[j], send_sem=send_sems.at[k],
                                              recv_sem=recv_sems.at[k], device_id=(*chip, c), device_id_type=MESH)
                 for k, (chip, _) in enumerate(peers)]
        for cp in sends:
            cp.start()
        for k, (_, pj) in enumerate(peers):
            pltpu.make_async_remote_copy(src_ref=t_ref, dst_ref=buf.at[pj], send_sem=send_sems.at[k],
                                         recv_sem=recv_sems.at[k], device_id=(x, y, c), device_id_type=MESH).wait_recv()
        for cp in sends:
            cp.wait_send()
        for jj in range(N_CHIPS):
            o_ref[:, :, jj * shard:(jj + 1) * shard] = buf[jj]

    vm = pl.BlockSpec(memory_space=pltpu.VMEM)
    return pl.pallas_call(
        body, name="gather_taps", out_shape=jax.ShapeDtypeStruct(taps.shape[:2] + (N_CHIPS * shard,), taps.dtype),
        in_specs=[vm], out_specs=vm,
        scratch_shapes=[pltpu.VMEM((N_CHIPS,) + taps.shape, taps.dtype), pltpu.SemaphoreType.DMA((3,)),
                        pltpu.SemaphoreType.DMA((3,))],
    )(taps)


def _adam_math(w, g, m, v):
    nm = ADAM_B1 * m + (1.0 - ADAM_B1) * g
    nv = ADAM_B2 * v + (1.0 - ADAM_B2) * jnp.square(g)
    m_hat = nm / (1.0 - ADAM_B1 ** ADAM_STEP)
    v_hat = nv / (1.0 - ADAM_B2 ** ADAM_STEP)
    return -ADAM_LR * (m_hat / (jnp.sqrt(v_hat) + ADAM_EPS) + ADAM_WD * w), nm, nv


def adamw(w, g, m, v, *, name):
    L, R, C = w.shape
    tr = next(t for t in (512, 480, 256, 128) if R % t == 0)

    def body(w_ref, g_ref, m_ref, v_ref, d_ref, nm_ref, nv_ref):
        d_ref[...], nm_ref[...], nv_ref[...] = _adam_math(w_ref[...], g_ref[...], m_ref[...], v_ref[...])

    spec = pl.BlockSpec((None, tr, C), lambda l, i: (l, i, 0))
    out = jax.ShapeDtypeStruct((L, R, C), F32)
    return pl.pallas_call(
        body, name=name, grid=(L, R // tr), in_specs=[spec] * 4, out_specs=[spec] * 3, out_shape=[out] * 3,
        compiler_params=_cp("parallel", "parallel"),
    )(w, g, m, v)


def adamw_small(packed, w, m, v):
    chunks = _row_chunks()
    names = SMALL + ("final_norm_g",)
    as_2d = lambda a: a.reshape(1, -1) if a.ndim == 1 else a
    ins = [as_2d(t[n]) for n in names for t in (w, m, v)]
    shapes = [jax.ShapeDtypeStruct(as_2d(w[n]).shape, F32) for n in names for _ in range(4)]
    n_in = len(ins)

    def body(p_ref, *refs):
        in_refs, out_refs = refs[:n_in], refs[n_in:]
        chip = _place()[3]
        for i, n in enumerate(names):
            w_ref, m_ref, v_ref = in_refs[3 * i:3 * i + 3]
            outs = out_refs[4 * i:4 * i + 4]

            def step(at, g):
                res = (g,) + _adam_math(w_ref[at], g, m_ref[at], v_ref[at])
                for o_ref, val in zip(outs, res):
                    o_ref[at] = val

            if n == "final_norm_g":
                step((slice(None), slice(None)), p_ref[FINAL_ROW:FINAL_ROW + 1, :])
                continue
            for l in range(DEPTH):
                if n == "conv_w":
                    r0 = l * LAYER_ROWS + TAPS_ROW
                    shard = CONV_C // N_CHIPS
                    g = jnp.zeros((CONV_K, shard), F32)
                    for j in range(N_CHIPS):
                        g = jnp.where(chip == j, p_ref[r0:r0 + CONV_K, j * shard:(j + 1) * shard], g)
                    step((l,), g)
                else:
                    for r, c0, wd in chunks[n]:
                        step((slice(l, l + 1), slice(c0, c0 + wd)),
                             p_ref[l * LAYER_ROWS + r:l * LAYER_ROWS + r + 1, 0:wd])

    vm = pl.BlockSpec(memory_space=pltpu.VMEM)
    res = pl.pallas_call(
        body, name="adamw_small", out_shape=shapes,
        in_specs=[vm] + [vm] * n_in, out_specs=[vm] * len(shapes),
    )(packed, *ins)
    dicts = ({}, {}, {}, {})
    for i, n in enumerate(names):
        for d, val in zip(dicts, res[4 * i:4 * i + 4]):
            d[n] = val.reshape(w[n].shape)
    return dicts


def _flat_rows(name, shard):
    return shard.T if name == "w_in" else shard.reshape(-1, FLAT_W)


def _full_matrix(slabs, name):
    K, N = FULL_SHAPES[name]
    if name == "w_in":
        return slabs.reshape(N, K)
    if name in COL_SHARDED:
        return slabs.reshape(N_CHIPS, K, N // N_CHIPS).transpose(1, 0, 2).reshape(K, N)
    return slabs.reshape(K, N)


def _first_row(parts, name):
    r = 0
    for n, rows in parts:
        if n == name:
            return r, rows
        r += rows
    raise KeyError(name)


class _Exchange:
    CARRIERS = {
        ("conv_bwd_ln", 1): ((1, "A"), "swap"), ("attn_bwd", 1): ((1, "A"), "exchange"), ("tn_in", 1): ((1, "A"), "share"),
        ("mlp_bwd", 0): ((1, "B"), "swap"), ("tn_mlp1", 0): ((1, "B"), "exchange"), ("tn_mlp2", 0): ((1, "B"), "share"),
        ("conv_bwd_ln", 0): ((0, "A"), "swap"), ("attn_bwd", 0): ((0, "A"), "exchange"), ("tn_in", 0): ((0, "A"), "share"),
    }

    def __init__(self, w, ci, chip):
        self.w, self.ci, self.chip = w, ci, chip
        self.wsh = [jnp.concatenate([_flat_rows(n, w[n][l]) for n, _ in FLAT_PARTS], axis=0).astype(CDT)
                    for l in range(DEPTH)]
        self.final_g = w["final_norm_g"].reshape(1, D)
        self.slabs = {}
        self.full = {}
        self.units = {}
        self.reduced = {}
        self._landed_weights(0, 0, _run_alone(gather_rider(self.wsh[0][:W_IN_ROWS]), "gather_w_in")[0])
        self.all_taps = gather_taps(w["conv_w"])

    def _landed_weights(self, l, r0, buf):
        self.slabs.setdefault(l, []).append((r0, buf))

    def _matrix(self, l, name):
        if (l, name) not in self.full:
            r, rows = _first_row(FLAT_PARTS, name)
            r0, buf = next((r0, buf) for r0, buf in self.slabs[l] if r0 <= r < r0 + buf.shape[1])
            self.full[(l, name)] = _full_matrix(buf[:, r - r0:r - r0 + rows], name)
        return self.full[(l, name)]

    def w_in(self, l):
        return self._matrix(l, "w_in")

    def mats(self, l):
        return {n: self._matrix(l, n) for n in MATRICES if n != "w_in"}

    def vec(self, n, l):
        return self.w[n][l].reshape(1, -1)

    def sinks(self, l):
        return self.w["sinks"][l]

    def taps(self, l):
        return self.all_taps[l]

    def rider(self, kernel, l):
        if (kernel, l) == ("rms_inproj", 0):
            return gather_rider(self.wsh[0][W_IN_ROWS:])
        if (kernel, l) == ("mlp_fwd", 0):
            return gather_rider(self.wsh[1])
        if (kernel, l) in self.CARRIERS:
            return self._stage(*self.CARRIERS[(kernel, l)])
        return None

    def landed(self, kernel, l, bufs):
        if (kernel, l) == ("rms_inproj", 0):
            self._landed_weights(0, W_IN_ROWS, bufs[0])
        elif (kernel, l) == ("mlp_fwd", 0):
            self._landed_weights(1, 0, bufs[0])
        else:
            self._stage_landed(*self.CARRIERS[(kernel, l)], bufs[0])

    def grads(self, l, group, g):
        if group == "A":
            flat = jnp.concatenate([g[n].reshape(N_CHIPS, rows, FLAT_W) for n, rows in GROUP_A], axis=1)
        else:
            flat = g["w_in"].reshape(N_CHIPS, W_IN_ROWS, FLAT_W)
        self.units[(l, group)] = {"g": flat}

    def _stage(self, key, stage):
        u = self.units[key]
        if stage == "swap":
            return swap_rider(u["g"])
        if stage == "exchange":
            u["pb"], u["own"] = pair_sum(u["g"], u["swap"])
            return exchange_rider(u["pb"])
        u["tot"] = total_sum(u["own"], u["exchange"])
        return share_rider(u["tot"])

    def _stage_landed(self, key, stage, buf):
        u = self.units[key]
        u[stage] = buf
        if stage == "share":
            tot = u["tot"]
            self.reduced[key] = jnp.where(self.ci == 0, jnp.concatenate([tot, buf]), jnp.concatenate([buf, tot]))

    def finish(self):
        key = (0, "B")
        for stage in ("swap", "exchange", "share"):
            self._stage_landed(key, stage, _run_alone(self._stage(key, stage), stage + "_last")[0])
        out = {}
        for n in MATRICES:
            per_layer = []
            for l in range(DEPTH):
                if n == "w_in":
                    per_layer.append(self.reduced[(l, "B")].T)
                    continue
                r, rows = _first_row(GROUP_A, n)
                per_layer.append(self.reduced[(l, "A")][r:r + rows].reshape(self.w[n].shape[1:]))
            out[n] = jnp.stack(per_layer)
        return out


WEIGHTS = ("mix_norm_g", "w_in", "b_in", "sinks", "conv_w", "conv_b", "conv_ln_g", "conv_ln_b", "w_attn_proj",
           "w_conv_proj", "b_conv_proj", "w_out", "mlp_norm_g", "w_mlp1", "w_mlp2", "final_norm_g")


def kernel(x, mix_norm_g, w_in, b_in, sinks, conv_w, conv_b, conv_ln_g, conv_ln_b, w_attn_proj, w_conv_proj, b_conv_proj, w_out, mlp_norm_g, w_mlp1, w_mlp2, final_norm_g, loss_target, m_mix_norm_g, m_w_in, m_b_in, m_sinks, m_conv_w, m_conv_b, m_conv_ln_g, m_conv_ln_b, m_w_attn_proj, m_w_conv_proj, m_b_conv_proj, m_w_out, m_mlp_norm_g, m_w_mlp1, m_w_mlp2, m_final_norm_g, v_mix_norm_g, v_w_in, v_b_in, v_sinks, v_conv_w, v_conv_b, v_conv_ln_g, v_conv_ln_b, v_w_attn_proj, v_w_conv_proj, v_b_conv_proj, v_w_out, v_mlp_norm_g, v_w_mlp1, v_w_mlp2, v_final_norm_g):
    w = dict(zip(WEIGHTS, (mix_norm_g, w_in, b_in, sinks, conv_w, conv_b, conv_ln_g, conv_ln_b, w_attn_proj, w_conv_proj,
                           b_conv_proj, w_out, mlp_norm_g, w_mlp1, w_mlp2, final_norm_g)))
    m = dict(zip(WEIGHTS, (m_mix_norm_g, m_w_in, m_b_in, m_sinks, m_conv_w, m_conv_b, m_conv_ln_g, m_conv_ln_b, m_w_attn_proj,
                           m_w_conv_proj, m_b_conv_proj, m_w_out, m_mlp_norm_g, m_w_mlp1, m_w_mlp2, m_final_norm_g)))
    v = dict(zip(WEIGHTS, (v_mix_norm_g, v_w_in, v_b_in, v_sinks, v_conv_w, v_conv_b, v_conv_ln_g, v_conv_ln_b, v_w_attn_proj,
                           v_w_conv_proj, v_b_conv_proj, v_w_out, v_mlp_norm_g, v_w_mlp1, v_w_mlp2, v_final_norm_g)))
    xi, yi, ci = lax.axis_index("x"), lax.axis_index("y"), lax.axis_index("c")
    chip = 2 * xi + yi

    hooks = _Exchange(w, ci, chip)
    loss, dx, gsm = forward_backward(x[0], loss_target[0], hooks)
    loss = lax.psum(loss[0, 0], ("x", "y", "c"))
    grads = hooks.finish()

    gsmall, delta, new_m, new_v = adamw_small(sum_small(gsm), w, m, v)
    grads.update(gsmall)
    for n in MATRICES:
        t = (lambda a: jnp.swapaxes(a, 1, 2)) if n == "w_in" else (lambda a: a)
        delta[n], new_m[n], new_v[n] = map(t, adamw(t(w[n]), t(grads[n]), t(m[n]), t(v[n]), name="adamw_" + n))

    return (loss, dx[None], *[grads[n] for n in WEIGHTS], *[delta[n] for n in WEIGHTS],
            *[new_m[n] for n in WEIGHTS], *[new_v[n] for n in WEIGHTS])
```

```python
import functools
import math
from typing import Callable, NamedTuple, Optional

import jax
import jax.numpy as jnp
import numpy as np
from jax import lax
from jax.experimental import pallas as pl
from jax.experimental.pallas import tpu as pltpu

F32 = jnp.float32
CDT = jnp.bfloat16

D = 1024
DEPTH = 2
N_Q = 8
HEAD_DIM = 64
ATTN_W = 512
KV_W = 128
BLOCK = 128
CONV_C = 512
CONV_K = 31
D_FF = 4096
IN_W = 3840
QKV_W = ATTN_W + 2 * KV_W
REST_W = IN_W - QKV_W
EPS = 1e-6
NEG = -1e30
SCALE = 1.0 / math.sqrt(HEAD_DIM)
SLOPES = [float(2.0 ** (-8.0 * (h + 1) / N_Q)) for h in range(N_Q)]
SUBLANES = 8
HALO = 32

ADAM_LR = 0.001
ADAM_B1 = 0.9
ADAM_B2 = 0.999
ADAM_EPS = 1e-08
ADAM_WD = 0.01
ADAM_STEP = 10

VMEM_LIMIT = 56 * 1024 * 1024


def _cp(*sem):
    return pltpu.CompilerParams(dimension_semantics=sem, vmem_limit_bytes=VMEM_LIMIT)


def _dot(a, b):
    return jnp.dot(a, b, preferred_element_type=F32)


def _dot_nt(a, b):
    return lax.dot_general(a, b, (((1,), (1,)), ((), ())), preferred_element_type=F32)


def _dot_tn(a, b):
    return lax.dot_general(a, b, (((0,), (0,)), ((), ())), preferred_element_type=F32)


def _sig(x):
    return 1.0 / (1.0 + jnp.exp(-x))


def _colsum(v):
    return jnp.sum(v, axis=0, keepdims=True)


def _const(shape, buffers=None):
    mode = {} if buffers is None else {"pipeline_mode": pl.Buffered(buffers)}
    return pl.BlockSpec(shape, lambda *_: (0,) * len(shape), **mode)


class Rider(NamedTuple):
    ins: tuple
    outs: tuple
    n_sems: int
    start: Callable
    finish: Callable
    late: Optional[Callable] = None


def _any():
    return pl.BlockSpec(memory_space=pl.ANY)


def _run(body, rider, *, name, grid, in_specs, out_specs, out_shape, args, sem, scratch_shapes=()):
    if rider is None:
        return pl.pallas_call(body, name=name, grid=grid, in_specs=list(in_specs), out_specs=list(out_specs),
                              out_shape=list(out_shape), scratch_shapes=list(scratch_shapes),
                              compiler_params=_cp(*sem))(*args)
    n_in, n_out, n_sc = len(in_specs), len(out_specs), len(scratch_shapes)
    r_in, r_out = len(rider.ins), len(rider.outs)

    def riding(*refs):
        ins, rins = refs[:n_in], refs[n_in:n_in + r_in]
        o0 = n_in + r_in
        outs, routs = refs[o0:o0 + n_out], refs[o0 + n_out:o0 + n_out + r_out]
        s0 = o0 + n_out + r_out
        scratch, sems = refs[s0:s0 + n_sc], refs[s0 + n_sc]
        first = functools.reduce(jnp.logical_and, [pl.program_id(a) == 0 for a in range(len(grid))])
        last = functools.reduce(jnp.logical_and, [pl.program_id(a) == grid[a] - 1 for a in range(len(grid))])

        @pl.when(first)
        def _():
            rider.start(rins, routs, sems)

        if rider.late is not None:
            late_step = grid[0] - max(1, grid[0] // 8)
            others = [pl.program_id(a) == 0 for a in range(1, len(grid))]

            @pl.when(functools.reduce(jnp.logical_and, others, pl.program_id(0) == late_step))
            def _():
                rider.late(rins, routs, sems)

        body(*ins, *outs, *scratch)

        @pl.when(last)
        def _():
            rider.finish(rins, routs, sems)

    res = pl.pallas_call(
        riding, name=name, grid=grid, in_specs=list(in_specs) + [_any()] * r_in,
        out_specs=list(out_specs) + [_any()] * r_out, out_shape=list(out_shape) + list(rider.outs),
        scratch_shapes=list(scratch_shapes) + [pltpu.SemaphoreType.DMA((rider.n_sems,))],
        compiler_params=_cp(*["arbitrary"] * len(grid)))(*args, *rider.ins)
    return res[:n_out], res[n_out:]


def _run_alone(rider, name):
    def body(*refs):
        r_in, r_out = len(rider.ins), len(rider.outs)
        rins, routs, sems = refs[:r_in], refs[r_in:r_in + r_out], refs[r_in + r_out]
        rider.start(rins, routs, sems)
        if rider.late is not None:
            rider.late(rins, routs, sems)
        rider.finish(rins, routs, sems)

    return pl.pallas_call(
        body, name=name, in_specs=[_any()] * len(rider.ins), out_specs=[_any()] * len(rider.outs),
        out_shape=list(rider.outs), scratch_shapes=[pltpu.SemaphoreType.DMA((rider.n_sems,))])(*rider.ins)


def rms_inproj(x, g, wt, b, *, tm=512, rider=None):
    T = x.shape[0]

    def body(x_ref, g_ref, w_ref, b_ref, qkv_ref, rest_ref):
        xv = x_ref[...]
        r = lax.rsqrt(jnp.mean(xv * xv, axis=-1, keepdims=True) + EPS)
        h = (xv * r * g_ref[...]).astype(CDT)
        qkv_ref[...] = (_dot_nt(h, w_ref[0:QKV_W, :]) + b_ref[:, 0:QKV_W]).astype(qkv_ref.dtype)
        for j in range(REST_W // D):
            c0 = QKV_W + D * j
            rest_ref[:, D * j:D * (j + 1)] = _dot_nt(h, w_ref[c0:c0 + D, :]) + b_ref[:, c0:c0 + D]

    return _run(
        body, rider, name="rms_inproj", grid=(T // tm,),
        in_specs=[pl.BlockSpec((tm, D), lambda i: (i, 0)), _const((1, D)), _const((IN_W, D), 1), _const((1, IN_W))],
        out_specs=[pl.BlockSpec((tm, QKV_W), lambda i: (i, 0)), pl.BlockSpec((tm, REST_W), lambda i: (i, 0))],
        out_shape=[jax.ShapeDtypeStruct((T, QKV_W), CDT), jax.ShapeDtypeStruct((T, REST_W), F32)],
        sem=("parallel",), args=(x, g, wt, b))


def _swap_halves(v):
    return pltpu.roll(v.astype(F32), HEAD_DIM, axis=1).astype(v.dtype)


N_KV = KV_W // HEAD_DIM
GROUP = N_Q // N_KV
STACK = GROUP * BLOCK


def _score_bias():
    row = np.arange(STACK)[:, None] % BLOCK
    col = np.arange(2 * BLOCK)[None, :]
    dist = row + BLOCK - col
    window = (dist >= 0) & (dist < BLOCK)
    slopes = np.asarray(SLOPES, np.float32).reshape(N_KV, GROUP)
    out = np.empty((2, N_KV, STACK, 2 * BLOCK), np.float32)
    for first in range(2):
        valid = window & ((col >= BLOCK) | (first == 0))
        for g in range(N_KV):
            slope = np.repeat(slopes[g], BLOCK)[:, None]
            out[first, g] = np.where(valid, -(slope * dist.astype(np.float32)), np.float32(NEG))
    return jnp.asarray(out)


def _per_head_column(vals):
    row = lax.broadcasted_iota(jnp.int32, (STACK, 1), 0)
    col = jnp.full((STACK, 1), vals[GROUP - 1], F32)
    for i in reversed(range(GROUP - 1)):
        col = jnp.where(row < (i + 1) * BLOCK, vals[i], col)
    return col


def _stack_heads(dst, src_ref, r0, g, scale=None):
    lane = lax.broadcasted_iota(jnp.int32, (BLOCK, 2 * HEAD_DIM), 1)
    keep = (lane >= HEAD_DIM) if g else (lane < HEAD_DIM)
    for i in range(GROUP):
        h = GROUP * g + i
        tile = src_ref[pl.ds(r0, BLOCK), (h // 2) * 128:(h // 2 + 1) * 128]
        if h % 2 != g:
            tile = _swap_halves(tile)
        if scale is not None:
            tile = tile * jnp.asarray(scale, tile.dtype)
        dst[i * BLOCK:(i + 1) * BLOCK, :] = jnp.where(keep, tile, jnp.zeros_like(tile))


def _unstack_heads(dst_ref, stacked, r0, g):
    lane = lax.broadcasted_iota(jnp.int32, (BLOCK, 2 * HEAD_DIM), 1)
    tiles = []
    for j in range(GROUP // 2):
        even = stacked[(2 * j) * BLOCK:(2 * j + 1) * BLOCK, :]
        odd = stacked[(2 * j + 1) * BLOCK:(2 * j + 2) * BLOCK, :]
        lo = _swap_halves(even) if g else even
        hi = odd if g else _swap_halves(odd)
        c0 = ((GROUP * g) // 2 + j) * 128
        tile = jnp.where(lane < HEAD_DIM, lo, hi)
        dst_ref[pl.ds(r0, BLOCK), c0:c0 + 128] = tile.astype(dst_ref.dtype)
        tiles.append((c0, tile))
    return tiles


def _qkv_specs(tq):
    nb = tq // BLOCK
    return [
        pl.BlockSpec((tq, ATTN_W), lambda i: (i, 0)),
        pl.BlockSpec((BLOCK, KV_W), lambda i: (jnp.maximum(i * nb - 1, 0), ATTN_W // KV_W)),
        pl.BlockSpec((tq, KV_W), lambda i: (i, ATTN_W // KV_W)),
        pl.BlockSpec((BLOCK, KV_W), lambda i: (jnp.maximum(i * nb - 1, 0), ATTN_W // KV_W + 1)),
        pl.BlockSpec((tq, KV_W), lambda i: (i, ATTN_W // KV_W + 1)),
    ]


def attn_fwd(qkv, sinks, *, tq=512):
    T = qkv.shape[0]
    nb = tq // BLOCK

    def body(sink_ref, bias_ref, q_ref, kp_ref, kc_ref, vp_ref, vc_ref, o_ref, lse_ref, kext, vext, qs):
        i = pl.program_id(0)
        kext[0:BLOCK, :] = kp_ref[...]
        kext[BLOCK:, :] = kc_ref[...]
        vext[0:BLOCK, :] = vp_ref[...]
        vext[BLOCK:, :] = vc_ref[...]
        lane_l = lax.broadcasted_iota(jnp.int32, (BLOCK, 128), 1)

        def blk(b, carry):
            r0 = pl.multiple_of(b * BLOCK, BLOCK)
            first = jnp.logical_and(i == 0, b == 0).astype(jnp.int32)
            kc = kext[pl.ds(r0, 2 * BLOCK), :]
            vc = vext[pl.ds(r0, 2 * BLOCK), :]
            lse_t = jnp.zeros((BLOCK, 128), F32)
            for g in range(N_KV):
                heads = range(GROUP * g, GROUP * (g + 1))
                _stack_heads(qs.at[g], q_ref, r0, g, SCALE)
                s = _dot_nt(qs[g], kc) + bias_ref[first, g]
                sink = _per_head_column([sink_ref[h] for h in heads])
                m = jnp.maximum(jnp.max(s, axis=-1, keepdims=True), sink)
                p = jnp.exp(s - m)
                denom = jnp.sum(p, axis=-1, keepdims=True) + jnp.exp(sink - m)
                p = p / denom
                _unstack_heads(o_ref, _dot(p.astype(CDT), vc), r0, g)
                lse = m + jnp.log(denom)
                for i_h, h in enumerate(heads):
                    lse_t = jnp.where(lane_l == h, lse[i_h * BLOCK:(i_h + 1) * BLOCK, :], lse_t)
            lse_ref[pl.ds(r0, BLOCK), :] = lse_t
            return carry

        lax.fori_loop(0, nb, blk, 0)

    return pl.pallas_call(
        body, name="attn_fwd", grid=(T // tq,),
        in_specs=[pl.BlockSpec(memory_space=pltpu.SMEM), _const((2, N_KV, STACK, 2 * BLOCK), 1)] + _qkv_specs(tq),
        out_specs=[pl.BlockSpec((tq, ATTN_W), lambda i: (i, 0)), pl.BlockSpec((tq, 128), lambda i: (i, 0))],
        out_shape=[jax.ShapeDtypeStruct((T, ATTN_W), CDT), jax.ShapeDtypeStruct((T, 128), F32)],
        scratch_shapes=[pltpu.VMEM((tq + BLOCK, KV_W), CDT), pltpu.VMEM((tq + BLOCK, KV_W), CDT),
                        pltpu.VMEM((N_KV, STACK, 2 * HEAD_DIM), CDT)],
        compiler_params=_cp("parallel"),
    )(sinks, _score_bias(), qkv, qkv, qkv, qkv, qkv)


def _halo_before(tm, width, col):
    return pl.BlockSpec((HALO, width), lambda i: (jnp.maximum(i * (tm // HALO) - 1, 0), col))


def _fill_u0(ext, a_ref, b_ref, ha_ref, hb_ref, first):
    hu = ha_ref[...] * _sig(hb_ref[...])
    ext[0:HALO, :] = jnp.where(first, jnp.zeros_like(hu), hu)
    ext[HALO:, :] = a_ref[...] * _sig(b_ref[...])


def _shifted_taps(src, w_ref, base, rc, offsets):
    acc = jnp.zeros((rc, CONV_C), F32)
    for b in range(SUBLANES):
        taps = [(k, o - b) for k, o in enumerate(offsets) if o % SUBLANES == b]
        if not taps:
            continue
        rows = rc if b == 0 else rc + SUBLANES
        part = jnp.zeros((rows, CONV_C), F32)
        for k, o8 in taps:
            part = part + w_ref[k:k + 1, :] * src[base + o8:base + o8 + rows, :]
        acc = acc + (part if b == 0 else part[b:b + rc, :])
    return acc


def _shift_copies(dst, src, rows):
    for b in range(1, SUBLANES):
        for r in range(0, rows, 64):
            n = min(64, rows - r)
            dst[b - 1, r:r + n, :] = src[r + b:r + b + n, :]


def _window(src, copies, off, r0, n):
    b = off % SUBLANES
    a = r0 + off - b
    return src[a:a + n, :] if b == 0 else copies[b - 1, a:a + n, :]


def _conv_rows(ext, w_ref, r0, rc):
    return _shifted_taps(ext, w_ref, r0, rc, [HALO - (CONV_K - 1) + k for k in range(CONV_K)])


def _layer_norm(u1, g, b):
    mu = jnp.mean(u1, axis=-1, keepdims=True)
    xc = u1 - mu
    rstd = lax.rsqrt(jnp.mean(xc * xc, axis=-1, keepdims=True) + EPS)
    n = xc * rstd
    return n, rstd, n * g + b


CONV_RC = 32
CONV_RC_1PASS = 64


def conv_fwd(rest, cw, cb, lg, lb, *, tm=512):
    T = rest.shape[0]

    def body(a_ref, b_ref, ha_ref, hb_ref, w_ref, cb_ref, lg_ref, lb_ref, o_ref, u1_ref, ext):
        _fill_u0(ext, a_ref, b_ref, ha_ref, hb_ref, pl.program_id(0) == 0)
        rc = CONV_RC_1PASS
        for r0 in range(0, tm, rc):
            u1 = _conv_rows(ext, w_ref, r0, rc) + cb_ref[...]
            u1_ref[r0:r0 + rc, :] = u1
            _, _, u2 = _layer_norm(u1, lg_ref[...], lb_ref[...])
            o_ref[r0:r0 + rc, :] = (u2 * _sig(u2)).astype(o_ref.dtype)

    row = lambda i: (i, 0)
    return pl.pallas_call(
        body, name="conv_fwd", grid=(T // tm,),
        in_specs=[pl.BlockSpec((tm, CONV_C), row), pl.BlockSpec((tm, CONV_C), lambda i: (i, 1)),
                  _halo_before(tm, CONV_C, 0), _halo_before(tm, CONV_C, 1),
                  _const((CONV_K, CONV_C)), _const((1, CONV_C)), _const((1, CONV_C)), _const((1, CONV_C))],
        out_specs=[pl.BlockSpec((tm, CONV_C), row), pl.BlockSpec((tm, CONV_C), row)],
        out_shape=[jax.ShapeDtypeStruct((T, CONV_C), CDT), jax.ShapeDtypeStruct((T, CONV_C), F32)],
        scratch_shapes=[pltpu.VMEM((tm + HALO, CONV_C), F32)],
        compiler_params=_cp("parallel"),
    )(rest, rest, rest, rest, cw, cb, lg, lb)


def merge_out(x, attn, u3, rest, wa, wc, bc, wo, *, tm=512):
    T = x.shape[0]

    def body(x_ref, at_ref, u_ref, ga_ref, gc_ref, wa_ref, wc_ref, bc_ref, wo_ref, o_ref):
        br_a = _dot(at_ref[...], wa_ref[...])
        br_c = _dot(u_ref[...], wc_ref[...]) + bc_ref[...]
        merged = _sig(ga_ref[...]) * br_a + _sig(gc_ref[...]) * br_c
        o_ref[...] = x_ref[...] + _dot(merged.astype(CDT), wo_ref[...])

    return pl.pallas_call(
        body, name="merge_out", grid=(T // tm,),
        in_specs=[pl.BlockSpec((tm, D), lambda i: (i, 0)), pl.BlockSpec((tm, ATTN_W), lambda i: (i, 0)),
                  pl.BlockSpec((tm, CONV_C), lambda i: (i, 0)),
                  pl.BlockSpec((tm, D), lambda i: (i, 1)), pl.BlockSpec((tm, D), lambda i: (i, 2)),
                  _const((ATTN_W, D), 1), _const((CONV_C, D), 1), _const((1, D)), _const((D, D), 1)],
        out_specs=pl.BlockSpec((tm, D), lambda i: (i, 0)),
        out_shape=jax.ShapeDtypeStruct((T, D), F32),
        compiler_params=_cp("parallel"),
    )(x, attn, u3, rest, rest, wa, wc, bc, wo)


def _loss_and_grad(xv, gv, tgt):
    r = lax.rsqrt(jnp.mean(xv * xv, axis=-1, keepdims=True) + EPS)
    e = xv * r * gv - tgt
    dx, dg_rows = _rms_bwd(xv, gv, e * (1.0 / D))
    return 0.5 * jnp.mean(e * e, axis=-1, keepdims=True), dx, dg_rows


def mlp_fwd(x, g, w1, w2, *, head=None, tm=256, tf=D_FF, rider=None):
    T = x.shape[0]
    nf = D_FF // tf

    def body(x_ref, g_ref, w1_ref, w2_ref, *rest):
        if head is None:
            o_ref, pre_ref, h_s, acc_s = rest
        else:
            gf_ref, t_ref, pre_ref, dy_ref, dgf_ref, loss_ref, h_s, acc_s = rest
        i, f = pl.program_id(0), pl.program_id(1)

        @pl.when(f == 0)
        def _():
            xv = x_ref[...]
            r = lax.rsqrt(jnp.mean(xv * xv, axis=-1, keepdims=True) + EPS)
            h_s[...] = (xv * r * g_ref[...]).astype(CDT)
            acc_s[...] = jnp.zeros_like(acc_s)

        pre = _dot(h_s[...], w1_ref[...])
        pre_ref[...] = pre
        a = jnp.square(jnp.maximum(pre, 0.0))
        acc_s[...] += _dot(a.astype(CDT), w2_ref[...])

        @pl.when(f == nf - 1)
        def _():
            y = x_ref[...] + acc_s[...]
            if head is None:
                o_ref[...] = y
                return

            @pl.when(i == 0)
            def _():
                dgf_ref[...] = jnp.zeros_like(dgf_ref)
                loss_ref[...] = jnp.zeros_like(loss_ref)

            loss_rows, dy, dg_rows = _loss_and_grad(y, gf_ref[...], t_ref[...])
            dy_ref[...] = dy
            dgf_ref[...] += _colsum(dg_rows)
            loss_ref[...] += _colsum(loss_rows)

    mode = {"pipeline_mode": pl.Buffered(1)} if nf == 1 else {}
    row = pl.BlockSpec((tm, D), lambda i, f: (i, 0))
    pre_spec, pre_shape = pl.BlockSpec((tm, tf), lambda i, f: (i, f)), jax.ShapeDtypeStruct((T, D_FF), F32)
    in_specs = [row, _const((1, D)), pl.BlockSpec((D, tf), lambda i, f: (0, f), **mode),
                pl.BlockSpec((tf, D), lambda i, f: (f, 0), **mode)]
    scratch = [pltpu.VMEM((tm, D), CDT), pltpu.VMEM((tm, D), F32)]
    if head is None:
        return _run(body, rider, name="mlp_fwd", grid=(T // tm, nf), in_specs=in_specs, out_specs=[row, pre_spec],
                    out_shape=[jax.ShapeDtypeStruct((T, D), F32), pre_shape], scratch_shapes=scratch,
                    sem=("parallel", "arbitrary"), args=(x, g, w1, w2))
    return _run(body, rider, name="mlp_fwd_loss", grid=(T // tm, nf), in_specs=in_specs + [_const((1, D)), row],
                out_specs=[pre_spec, row, _const((1, D)), _const((1, 128))],
                out_shape=[pre_shape, jax.ShapeDtypeStruct((T, D), F32), jax.ShapeDtypeStruct((1, D), F32),
                           jax.ShapeDtypeStruct((1, 128), F32)],
                scratch_shapes=scratch, sem=("arbitrary", "arbitrary"), args=(x, g, w1, w2) + tuple(head))


def _rms_bwd(xv, g, dh):
    r = lax.rsqrt(jnp.mean(xv * xv, axis=-1, keepdims=True) + EPS)
    xhat = xv * r
    dxh = dh * g
    dx = r * (dxh - xhat * jnp.mean(dxh * xhat, axis=-1, keepdims=True))
    return dx, dh * xhat


def mlp_bwd(dy, x, g, pre, w1, w2, *, tm=256, tf=D_FF, rider=None):
    T = x.shape[0]
    nf = D_FF // tf

    def body(dy_ref, x_ref, g_ref, pre_ref, w1_ref, w2_ref, dx_ref, dg_ref, h_ref, a_ref, dpre_ref, dyb_ref, dyb_s, acc_s):
        i, f = pl.program_id(0), pl.program_id(1)

        @pl.when(jnp.logical_and(i == 0, f == 0))
        def _():
            dg_ref[...] = jnp.zeros_like(dg_ref)

        @pl.when(f == 0)
        def _():
            dyb_s[...] = dy_ref[...].astype(CDT)
            dyb_ref[...] = dyb_s[...]
            acc_s[...] = jnp.zeros_like(acc_s)

        pre = pre_ref[...]
        rl = jnp.maximum(pre, 0.0)
        a_ref[...] = (rl * rl).astype(CDT)
        da = _dot_nt(dyb_s[...], w2_ref[...])
        dpre = (da * (2.0 * rl)).astype(CDT)
        dpre_ref[...] = dpre
        acc_s[...] += _dot_nt(dpre, w1_ref[...])

        @pl.when(f == nf - 1)
        def _():
            xv = x_ref[...]
            gv = g_ref[...]
            dxn, dg_rows = _rms_bwd(xv, gv, acc_s[...])
            dx_ref[...] = dy_ref[...] + dxn
            dg_ref[...] += _colsum(dg_rows)
            r = lax.rsqrt(jnp.mean(xv * xv, axis=-1, keepdims=True) + EPS)
            h_ref[...] = (xv * r * gv).astype(CDT)

    row = lambda i, f: (i, 0)
    mode = {"pipeline_mode": pl.Buffered(1)} if nf == 1 else {}
    return _run(
        body, rider, name="mlp_bwd", grid=(T // tm, nf),
        in_specs=[pl.BlockSpec((tm, D), row), pl.BlockSpec((tm, D), row), _const((1, D)),
                  pl.BlockSpec((tm, tf), lambda i, f: (i, f)),
                  pl.BlockSpec((D, tf), lambda i, f: (0, f), **mode), pl.BlockSpec((tf, D), lambda i, f: (f, 0), **mode)],
        out_specs=[pl.BlockSpec((tm, D), row), _const((1, D)), pl.BlockSpec((tm, D), row),
                   pl.BlockSpec((tm, tf), lambda i, f: (i, f)), pl.BlockSpec((tm, tf), lambda i, f: (i, f)),
                   pl.BlockSpec((tm, D), row)],
        out_shape=[jax.ShapeDtypeStruct((T, D), F32), jax.ShapeDtypeStruct((1, D), F32),
                   jax.ShapeDtypeStruct((T, D), CDT), jax.ShapeDtypeStruct((T, D_FF), CDT),
                   jax.ShapeDtypeStruct((T, D_FF), CDT), jax.ShapeDtypeStruct((T, D), CDT)],
        scratch_shapes=[pltpu.VMEM((tm, D), CDT), pltpu.VMEM((tm, D), F32)],
        sem=("arbitrary", "arbitrary"), args=(dy, x, g, pre, w1, w2))


def tn_matmul(a, b, *, tm, tn, tk=2048, name, by_chip=False, rider=None):
    T, M = a.shape
    N = b.shape[1]
    tk = min(tk, T)
    nk = T // tk
    split = by_chip and tn == N
    shard = N // N_CHIPS

    def body(a_ref, b_ref, o_ref):
        @pl.when(pl.program_id(2) == 0)
        def _():
            o_ref[...] = jnp.zeros_like(o_ref)

        prod = _dot_tn(a_ref[...].astype(CDT), b_ref[...].astype(CDT))
        if split:
            for j in range(N_CHIPS):
                o_ref[j] += prod[:, j * shard:(j + 1) * shard]
        else:
            o_ref[...] += prod

    if split:
        out_spec = pl.BlockSpec((N_CHIPS, tm, shard), lambda i, j, k: (0, i, 0))
        out_shape = jax.ShapeDtypeStruct((N_CHIPS, M, shard), F32)
    elif by_chip:
        out_spec = pl.BlockSpec((None, tm, tn), lambda i, j, k: (j, i, 0))
        out_shape = jax.ShapeDtypeStruct((N // tn, M, tn), F32)
    else:
        out_spec = pl.BlockSpec((tm, tn), lambda i, j, k: (i, j))
        out_shape = jax.ShapeDtypeStruct((M, N), F32)
    res = _run(
        body, rider, name=name, grid=(M // tm, N // tn, nk),
        in_specs=[pl.BlockSpec((tk, tm), lambda i, j, k: (k, i)), pl.BlockSpec((tk, tn), lambda i, j, k: (k, j))],
        out_specs=[out_spec], out_shape=[out_shape], sem=("parallel", "parallel", "arbitrary"), args=(a, b))
    return res[0] if rider is None else (res[0][0], res[1])


def merge_bwd(dx1, attn, u3, rest, wa, wc, bc, wo, *, tm=512):
    T = dx1.shape[0]

    def body(dx_ref, at_ref, u_ref, ga_ref, gc_ref, wa_ref, wc_ref, bc_ref, wo_ref,
             mg_ref, dba_ref, dbc_ref, dat_ref, du_ref, dgate_ref, dgsum_ref, dbias_ref):
        @pl.when(pl.program_id(0) == 0)
        def _():
            dbias_ref[...] = jnp.zeros_like(dbias_ref)
            dgsum_ref[...] = jnp.zeros_like(dgsum_ref)

        br_a = _dot(at_ref[...], wa_ref[...])
        br_c = _dot(u_ref[...], wc_ref[...]) + bc_ref[...]
        sa = _sig(ga_ref[...])
        sc = _sig(gc_ref[...])
        mg_ref[...] = (sa * br_a + sc * br_c).astype(CDT)
        dm = _dot_nt(dx_ref[...].astype(CDT), wo_ref[...])
        dba = dm * sa
        dbc = dm * sc
        dga = dm * br_a * sa * (1.0 - sa)
        dgc = dm * br_c * sc * (1.0 - sc)
        dgate_ref[:, 0:D] = dga.astype(CDT)
        dgate_ref[:, D:2 * D] = dgc.astype(CDT)
        dgsum_ref[:, 0:D] += _colsum(dga)
        dgsum_ref[:, D:2 * D] += _colsum(dgc)
        dbias_ref[...] += _colsum(dbc)
        dba_b = dba.astype(CDT)
        dbc_b = dbc.astype(CDT)
        dba_ref[...] = dba_b
        dbc_ref[...] = dbc_b
        dat_ref[...] = _dot_nt(dba_b, wa_ref[...]).astype(CDT)
        du_ref[...] = _dot_nt(dbc_b, wc_ref[...])

    row = lambda i: (i, 0)
    return pl.pallas_call(
        body, name="merge_bwd", grid=(T // tm,),
        in_specs=[pl.BlockSpec((tm, D), row), pl.BlockSpec((tm, ATTN_W), row), pl.BlockSpec((tm, CONV_C), row),
                  pl.BlockSpec((tm, D), lambda i: (i, 1)), pl.BlockSpec((tm, D), lambda i: (i, 2)),
                  _const((ATTN_W, D), 1), _const((CONV_C, D), 1), _const((1, D)), _const((D, D), 1)],
        out_specs=[pl.BlockSpec((tm, D), row), pl.BlockSpec((tm, D), row), pl.BlockSpec((tm, D), row),
                   pl.BlockSpec((tm, ATTN_W), row), pl.BlockSpec((tm, CONV_C), row),
                   pl.BlockSpec((tm, 2 * D), row), _const((1, 2 * D)), _const((1, D))],
        out_shape=[jax.ShapeDtypeStruct((T, D), CDT), jax.ShapeDtypeStruct((T, D), CDT),
                   jax.ShapeDtypeStruct((T, D), CDT), jax.ShapeDtypeStruct((T, ATTN_W), CDT),
                   jax.ShapeDtypeStruct((T, CONV_C), F32), jax.ShapeDtypeStruct((T, 2 * D), CDT),
                   jax.ShapeDtypeStruct((1, 2 * D), F32), jax.ShapeDtypeStruct((1, D), F32)],
        compiler_params=_cp("arbitrary"),
    )(dx1, attn, u3, rest, rest, wa, wc, bc, wo)


def conv_bwd_ln(du3, u1, lg, lb, *, tm=512, rider=None):
    T = du3.shape[0]

    def body(du_ref, u1_ref, lg_ref, lb_ref, du1_ref, dlg_ref, dlb_ref, dcb_ref):
        @pl.when(pl.program_id(0) == 0)
        def _():
            dlg_ref[...] = jnp.zeros_like(dlg_ref)
            dlb_ref[...] = jnp.zeros_like(dlb_ref)
            dcb_ref[...] = jnp.zeros_like(dcb_ref)

        dlg = jnp.zeros((1, CONV_C), F32)
        dlb = jnp.zeros((1, CONV_C), F32)
        dcb = jnp.zeros((1, CONV_C), F32)
        rc = CONV_RC_1PASS
        for r0 in range(0, tm, rc):
            n, rstd, u2 = _layer_norm(u1_ref[r0:r0 + rc, :], lg_ref[...], lb_ref[...])
            s = _sig(u2)
            du2 = du_ref[r0:r0 + rc, :] * (s + u2 * s * (1.0 - s))
            dn = du2 * lg_ref[...]
            du1 = rstd * (dn - jnp.mean(dn, axis=-1, keepdims=True) - n * jnp.mean(dn * n, axis=-1, keepdims=True))
            du1_ref[r0:r0 + rc, :] = du1
            dlg = dlg + _colsum(du2 * n)
            dlb = dlb + _colsum(du2)
            dcb = dcb + _colsum(du1)
        dlg_ref[...] += dlg
        dlb_ref[...] += dlb
        dcb_ref[...] += dcb

    row = lambda i: (i, 0)
    vec = jax.ShapeDtypeStruct((1, CONV_C), F32)
    return _run(
        body, rider, name="conv_bwd_ln", grid=(T // tm,),
        in_specs=[pl.BlockSpec((tm, CONV_C), row), pl.BlockSpec((tm, CONV_C), row), _const((1, CONV_C)), _const((1, CONV_C))],
        out_specs=[pl.BlockSpec((tm, CONV_C), row), _const((1, CONV_C)), _const((1, CONV_C)), _const((1, CONV_C))],
        out_shape=[jax.ShapeDtypeStruct((T, CONV_C), F32), vec, vec, vec],
        sem=("arbitrary",), args=(du3, u1, lg, lb))


def conv_bwd_taps(du1, rest, cw, *, tm=512):
    T = du1.shape[0]
    nt = T // tm

    def body(d_ref, hd_ref, a_ref, b_ref, ha_ref, hb_ref, w_ref, dglu_ref, dgsum_ref, dw_ref, ext, dext, dcopies, dwacc):
        i = pl.program_id(0)

        @pl.when(i == 0)
        def _():
            dwacc[...] = jnp.zeros_like(dwacc)
            dgsum_ref[...] = jnp.zeros_like(dgsum_ref)

        rc = CONV_RC
        groups = lambda v: jnp.sum(v.reshape(v.shape[0] // SUBLANES, SUBLANES, CONV_C), axis=0)
        sums = [jnp.zeros((SUBLANES, CONV_C), F32), jnp.zeros((SUBLANES, CONV_C), F32)]
        _fill_u0(ext, a_ref, b_ref, ha_ref, hb_ref, i == 0)
        dext[0:SUBLANES, :] = jnp.zeros((SUBLANES, CONV_C), F32)
        dext[SUBLANES:SUBLANES + tm, :] = d_ref[...]
        hd = hd_ref[...]
        dext[SUBLANES + tm:, :] = jnp.where(i == nt - 1, jnp.zeros_like(hd), hd)
        _shift_copies(dcopies, dext, tm + HALO)
        for r0 in range(0, tm, rc):
            du0 = jnp.zeros((rc, CONV_C), F32)
            for k in range(CONV_K):
                du0 = du0 + w_ref[k:k + 1, :] * _window(dext, dcopies, SUBLANES + CONV_K - 1 - k, r0, rc)
            av = a_ref[r0:r0 + rc, :]
            sb = _sig(b_ref[r0:r0 + rc, :])
            for half, dg in enumerate((du0 * sb, du0 * av * sb * (1.0 - sb))):
                dglu_ref[r0:r0 + rc, half * CONV_C:(half + 1) * CONV_C] = dg.astype(CDT)
                sums[half] = sums[half] + groups(dg)
        for half in range(2):
            dgsum_ref[:, half * CONV_C:(half + 1) * CONV_C] += _colsum(sums[half])
        tail = lax.broadcasted_iota(jnp.int32, (SUBLANES, CONV_C), 0)
        for b in range(SUBLANES):
            taps = [(k, HALO - (CONV_K - 1) + k - b) for k in range(CONV_K) if (HALO - (CONV_K - 1) + k) % SUBLANES == b]
            accs = [jnp.zeros((SUBLANES, CONV_C), F32) for _ in taps]
            for r0 in list(range(0, tm, rc)) + ([tm] if b else []):
                n = rc if r0 < tm else SUBLANES
                dwin = _window(dext, dcopies, SUBLANES - b, r0, n)
                if r0 == tm:
                    dwin = jnp.where(tail < b, dwin, 0.0)
                for j, (k, o8) in enumerate(taps):
                    accs[j] = accs[j] + groups(dwin * ext[r0 + o8:r0 + o8 + n, :])
            for j, (k, _) in enumerate(taps):
                dwacc[8 * k:8 * k + 8, :] += accs[j]

        @pl.when(i == nt - 1)
        def _():
            dw_ref[...] = jnp.zeros_like(dw_ref)
            for k in range(CONV_K):
                dw_ref[k:k + 1, :] = _colsum(dwacc[8 * k:8 * k + 8, :])

    row = lambda i: (i, 0)
    return pl.pallas_call(
        body, name="conv_bwd_taps", grid=(nt,),
        in_specs=[pl.BlockSpec((tm, CONV_C), row),
                  pl.BlockSpec((HALO, CONV_C), lambda i: (jnp.minimum((i + 1) * (tm // HALO), T // HALO - 1), 0)),
                  pl.BlockSpec((tm, CONV_C), row), pl.BlockSpec((tm, CONV_C), lambda i: (i, 1)),
                  _halo_before(tm, CONV_C, 0), _halo_before(tm, CONV_C, 1), _const((CONV_K, CONV_C))],
        out_specs=[pl.BlockSpec((tm, 2 * CONV_C), row), _const((1, 2 * CONV_C)), _const((HALO, CONV_C))],
        out_shape=[jax.ShapeDtypeStruct((T, 2 * CONV_C), CDT), jax.ShapeDtypeStruct((1, 2 * CONV_C), F32),
                   jax.ShapeDtypeStruct((HALO, CONV_C), F32)],
        scratch_shapes=[pltpu.VMEM((tm + HALO, CONV_C), F32), pltpu.VMEM((SUBLANES + tm + HALO, CONV_C), F32),
                        pltpu.VMEM((SUBLANES - 1, tm + HALO, CONV_C), F32), pltpu.VMEM((8 * CONV_K, CONV_C), F32)],
        compiler_params=_cp("arbitrary"),
    )(du1, du1, rest, rest, rest, rest, cw)


def attn_bwd(qkv, do, lse, sinks, *, tq=512, rider=None):
    T = qkv.shape[0]
    nb = tq // BLOCK

    def body(sink_ref, bias_ref, q_ref, kp_ref, kc_ref, vp_ref, vc_ref, do_ref, lse_ref,
             dq_ref, dqsum_ref, dkv_ref, spill_ref, dsink_ref, kext, vext, dkext, dvext, qs, dos):
        i = pl.program_id(0)

        @pl.when(i == 0)
        def _():
            dsink_ref[...] = jnp.zeros_like(dsink_ref)
            dqsum_ref[...] = jnp.zeros_like(dqsum_ref)

        kext[0:BLOCK, :] = kp_ref[...]
        kext[BLOCK:, :] = kc_ref[...]
        vext[0:BLOCK, :] = vp_ref[...]
        vext[BLOCK:, :] = vc_ref[...]
        dkext[...] = jnp.zeros_like(dkext)
        dvext[...] = jnp.zeros_like(dvext)
        lane_l = lax.broadcasted_iota(jnp.int32, (BLOCK, 128), 1)
        lane_k = lax.broadcasted_iota(jnp.int32, (2 * BLOCK, KV_W), 1)

        def blk(b, dsink):
            r0 = pl.multiple_of(b * BLOCK, BLOCK)
            first = jnp.logical_and(i == 0, b == 0).astype(jnp.int32)
            kc = kext[pl.ds(r0, 2 * BLOCK), :]
            vc = vext[pl.ds(r0, 2 * BLOCK), :]
            lse_t = lse_ref[pl.ds(r0, BLOCK), :]
            dk = jnp.zeros((2 * BLOCK, KV_W), F32)
            dv = jnp.zeros((2 * BLOCK, KV_W), F32)
            for g in range(N_KV):
                heads = range(GROUP * g, GROUP * (g + 1))
                _stack_heads(qs.at[g], q_ref, r0, g, SCALE)
                _stack_heads(dos.at[g], do_ref, r0, g)
                qv = qs[g]
                dov = dos[g]
                s = _dot_nt(qv, kc) + bias_ref[first, g]
                lse = jnp.concatenate(
                    [jnp.sum(jnp.where(lane_l == h, lse_t, 0.0), axis=-1, keepdims=True) for h in heads], axis=0)
                p = jnp.exp(s - lse)
                dp = _dot_nt(dov, vc)
                dd = jnp.sum(p * dp, axis=-1, keepdims=True)
                ds = (p * (dp - dd)).astype(CDT)
                keep = (lane_k >= HEAD_DIM) if g else (lane_k < HEAD_DIM)
                for c0, tile in _unstack_heads(dq_ref, _dot(ds, jnp.where(keep, kc, jnp.zeros_like(kc))) * SCALE, r0, g):
                    dqsum_ref[:, c0:c0 + 128] += _colsum(tile)
                dk = dk + _dot_tn(ds, qv)
                dv = dv + _dot_tn(p.astype(CDT), dov)
                wsink = jnp.exp(_per_head_column([sink_ref[h] for h in heads]) - lse) * dd
                for i_h, h in enumerate(heads):
                    part = jnp.sum(wsink[i_h * BLOCK:(i_h + 1) * BLOCK, :], axis=0, keepdims=True)
                    dsink = dsink - jnp.where(lane_l[0:1, :] == h, part, 0.0)
            dkext[pl.ds(r0, 2 * BLOCK), :] += dk
            dvext[pl.ds(r0, 2 * BLOCK), :] += dv
            return dsink

        dsink_ref[...] += lax.fori_loop(0, nb, blk, jnp.zeros((1, 128), F32))
        dkv_ref[:, 0:KV_W] = dkext[BLOCK:, :]
        dkv_ref[:, KV_W:2 * KV_W] = dvext[BLOCK:, :]
        spill_ref[:, 0:KV_W] = dkext[0:BLOCK, :]
        spill_ref[:, KV_W:2 * KV_W] = dvext[0:BLOCK, :]

    row = lambda i: (i, 0)
    return _run(
        body, rider, name="attn_bwd", grid=(T // tq,),
        in_specs=[pl.BlockSpec(memory_space=pltpu.SMEM), _const((2, N_KV, STACK, 2 * BLOCK), 1)] + _qkv_specs(tq)
        + [pl.BlockSpec((tq, ATTN_W), row), pl.BlockSpec((tq, 128), row)],
        out_specs=[pl.BlockSpec((tq, ATTN_W), row), _const((1, ATTN_W)), pl.BlockSpec((tq, 2 * KV_W), row),
                   pl.BlockSpec((BLOCK, 2 * KV_W), row), _const((1, 128))],
        out_shape=[jax.ShapeDtypeStruct((T, ATTN_W), CDT), jax.ShapeDtypeStruct((1, ATTN_W), F32),
                   jax.ShapeDtypeStruct((T, 2 * KV_W), F32),
                   jax.ShapeDtypeStruct((T // tq * BLOCK, 2 * KV_W), F32), jax.ShapeDtypeStruct((1, 128), F32)],
        scratch_shapes=[pltpu.VMEM((tq + BLOCK, KV_W), CDT), pltpu.VMEM((tq + BLOCK, KV_W), CDT),
                        pltpu.VMEM((tq + BLOCK, KV_W), F32), pltpu.VMEM((tq + BLOCK, KV_W), F32),
                        pltpu.VMEM((N_KV, STACK, 2 * HEAD_DIM), CDT), pltpu.VMEM((N_KV, STACK, 2 * HEAD_DIM), CDT)],
        sem=("arbitrary",), args=(sinks, _score_bias(), qkv, qkv, qkv, qkv, qkv, do, lse))


def inproj_bwd(dres, x, g, w, dq, dkv, spill, dglu, dgate, sums, *, tm=512):
    T = x.shape[0]
    nt = T // tm
    pieces = ((0, ATTN_W), (QKV_W, 2 * CONV_C), (QKV_W + 2 * CONV_C, 2 * D))

    def body(dr_ref, x_ref, g_ref, w_ref, dq_ref, dkv_ref, sp_ref, dglu_ref, dgate_ref, sq_ref, sglu_ref, sgate_ref,
             dx_ref, dp_ref, h_ref, dg_ref, db_ref):
        i = pl.program_id(0)

        @pl.when(i == 0)
        def _():
            dg_ref[...] = jnp.zeros_like(dg_ref)
            db_ref[:, ATTN_W:QKV_W] = jnp.zeros((1, QKV_W - ATTN_W), F32)
            for (c0, wd), s_ref in zip(pieces, (sq_ref, sglu_ref, sgate_ref)):
                db_ref[:, c0:c0 + wd] = s_ref[...]

        sp = sp_ref[...]
        sp = jnp.where(i == nt - 1, jnp.zeros_like(sp), sp)
        dkv = dkv_ref[...]
        db_ref[:, ATTN_W:QKV_W] += _colsum(dkv) + _colsum(sp)
        dp_ref[0:tm - BLOCK, ATTN_W:QKV_W] = dkv[0:tm - BLOCK, :].astype(CDT)
        dp_ref[tm - BLOCK:tm, ATTN_W:QKV_W] = (dkv[tm - BLOCK:tm, :] + sp).astype(CDT)
        for (c0, wd), ref in zip(pieces, (dq_ref, dglu_ref, dgate_ref)):
            dp_ref[:, c0:c0 + wd] = ref[...]
        dh = _dot(dp_ref[...], w_ref[...])
        xv = x_ref[...]
        gv = g_ref[...]
        dxn, dg_rows = _rms_bwd(xv, gv, dh)
        dx_ref[...] = dr_ref[...] + dxn
        dg_ref[...] += _colsum(dg_rows)
        r = lax.rsqrt(jnp.mean(xv * xv, axis=-1, keepdims=True) + EPS)
        h_ref[...] = (xv * r * gv).astype(CDT)

    row = lambda i: (i, 0)
    return pl.pallas_call(
        body, name="inproj_bwd", grid=(nt,),
        in_specs=[pl.BlockSpec((tm, D), row), pl.BlockSpec((tm, D), row), _const((1, D)), _const((IN_W, D), 1),
                  pl.BlockSpec((tm, ATTN_W), row), pl.BlockSpec((tm, 2 * KV_W), row),
                  pl.BlockSpec((BLOCK, 2 * KV_W), lambda i: (jnp.minimum(i + 1, nt - 1), 0)),
                  pl.BlockSpec((tm, 2 * CONV_C), row), pl.BlockSpec((tm, 2 * D), row)]
        + [_const((1, wd)) for _, wd in pieces],
        out_specs=[pl.BlockSpec((tm, D), row), pl.BlockSpec((tm, IN_W), row), pl.BlockSpec((tm, D), row),
                   _const((1, D)), _const((1, IN_W))],
        out_shape=[jax.ShapeDtypeStruct((T, D), F32), jax.ShapeDtypeStruct((T, IN_W), CDT),
                   jax.ShapeDtypeStruct((T, D), CDT), jax.ShapeDtypeStruct((1, D), F32),
                   jax.ShapeDtypeStruct((1, IN_W), F32)],
        compiler_params=_cp("arbitrary"),
    )(dres, x, g, w, dq, dkv, spill, dglu, dgate, *sums)


ATTN_TILE = 512
MATRICES = ("w_in", "w_attn_proj", "w_conv_proj", "w_out", "w_mlp1", "w_mlp2")
SMALL = ("mix_norm_g", "b_in", "sinks", "conv_w", "conv_b", "conv_ln_g", "conv_ln_b", "b_conv_proj", "mlp_norm_g")


def forward_backward(x, tgt, hooks):
    def call(fn, kernel, l, *args, **kw):
        rider = hooks.rider(kernel, l)
        if rider is None:
            return fn(*args, **kw)
        outs, landed = fn(*args, rider=rider, **kw)
        hooks.landed(kernel, l, landed)
        return outs

    vec = hooks.vec
    saved = []
    for l in range(DEPTH):
        qkv, rest = call(rms_inproj, "rms_inproj", l, x, vec("mix_norm_g", l), hooks.w_in(l), vec("b_in", l))
        m = hooks.mats(l)
        attn, lse = attn_fwd(qkv, hooks.sinks(l), tq=ATTN_TILE)
        u3, u1 = conv_fwd(rest, hooks.taps(l), vec("conv_b", l), vec("conv_ln_g", l), vec("conv_ln_b", l))
        x1 = merge_out(x, attn, u3, rest, m["w_attn_proj"], m["w_conv_proj"], vec("b_conv_proj", l), m["w_out"])
        if l < DEPTH - 1:
            x_next, pre = call(mlp_fwd, "mlp_fwd", l, x1, vec("mlp_norm_g", l), m["w_mlp1"], m["w_mlp2"])
        else:
            x_next = None
            pre, dx, dgf, loss = call(mlp_fwd, "mlp_fwd", l, x1, vec("mlp_norm_g", l), m["w_mlp1"], m["w_mlp2"],
                                      head=(hooks.final_g, tgt))
        saved.append((x, qkv, rest, attn, lse, u3, u1, x1, pre))
        x = x_next
    small = {n: [None] * DEPTH for n in SMALL}
    small["final_norm_g"] = dgf
    for l in reversed(range(DEPTH)):
        x0, qkv, rest, attn, lse, u3, u1, x1, pre = saved[l]
        m = hooks.mats(l)
        dx1, dg2, h2, a, dpre, dxb = call(mlp_bwd, "mlp_bwd", l, dx, x1, vec("mlp_norm_g", l), pre, m["w_mlp1"], m["w_mlp2"])
        small["mlp_norm_g"][l] = dg2
        group = {}
        group["w_mlp1"] = call(tn_matmul, "tn_mlp1", l, h2, dpre, tm=1024, tn=1024, tk=4096, name="tn_mlp1", by_chip=True)
        group["w_mlp2"] = call(tn_matmul, "tn_mlp2", l, a, dxb, tm=1024, tn=1024, tk=4096, name="tn_mlp2")
        merged, dba, dbc, dattn, du3, dgate, dgate_sum, dbcp = merge_bwd(
            dx1, attn, u3, rest, m["w_attn_proj"], m["w_conv_proj"], vec("b_conv_proj", l), m["w_out"])
        small["b_conv_proj"][l] = dbcp
        group["w_out"] = tn_matmul(merged, dx1, tm=1024, tn=1024, name="tn_out")
        group["w_attn_proj"] = tn_matmul(attn, dba, tm=512, tn=D, tk=4096, name="tn_attn_proj", by_chip=True)
        group["w_conv_proj"] = tn_matmul(u3, dbc, tm=512, tn=D, tk=4096, name="tn_conv_proj", by_chip=True)
        hooks.grads(l, "A", group)
        du1, dlg, dlb, dcb = call(conv_bwd_ln, "conv_bwd_ln", l, du3, u1, vec("conv_ln_g", l), vec("conv_ln_b", l))
        small["conv_ln_g"][l], small["conv_ln_b"][l], small["conv_b"][l] = dlg, dlb, dcb
        dglu, dglu_sum, dcw = conv_bwd_taps(du1, rest, hooks.taps(l))
        small["conv_w"][l] = dcw
        dq, dq_sum, dkv, spill, dsink = call(attn_bwd, "attn_bwd", l, qkv, dattn, lse, hooks.sinks(l), tq=ATTN_TILE)
        small["sinks"][l] = dsink
        dx, dproj, h, dg, db = inproj_bwd(dx1, x0, vec("mix_norm_g", l), hooks.w_in(l), dq, dkv, spill, dglu, dgate,
                                          (dq_sum, dglu_sum, dgate_sum), tm=ATTN_TILE)
        small["mix_norm_g"][l], small["b_in"][l] = dg, db
        hooks.grads(l, "B", {"w_in": call(tn_matmul, "tn_in", l, dproj, h, tm=768, tn=1024, tk=4096, name="tn_in")})
    return loss, dx, small


class _LocalHooks:
    def __init__(self, p):
        self.p = p
        self.final_g = p["final_norm_g"]
        self.got = {n: [None] * DEPTH for n in MATRICES}

    def w_in(self, l):
        return self.p["w_in"][l].T

    def mats(self, l):
        return {n: self.p[n][l] for n in MATRICES}

    def vec(self, n, l):
        return self.p[n][l]

    def sinks(self, l):
        return self.p["sinks"][l]

    def taps(self, l):
        return self.p["conv_w"][l]

    def rider(self, kernel, l):
        return None

    def grads(self, l, group, g):
        for n, v in g.items():
            if v.ndim == 3:
                v = v.transpose(1, 0, 2).reshape(v.shape[1], -1)
            self.got[n][l] = v.T if n == "w_in" else v


def local_grads(x, tgt, p):
    hooks = _LocalHooks(p)
    loss, dx, small = forward_backward(x, tgt, hooks)
    small["conv_w"] = [g[0:CONV_K] for g in small["conv_w"]]
    small["sinks"] = [g[0, 0:N_Q] for g in small["sinks"]]
    return loss, dx, {**small, **hooks.got}


MESH = pl.DeviceIdType.MESH
N_CHIPS = 4
N_DEV = 8
FLAT_W = 1024
FLAT_PARTS = (("w_in", 960), ("w_attn_proj", 128), ("w_conv_proj", 128), ("w_out", 256), ("w_mlp1", 1024), ("w_mlp2", 1024))
W_IN_ROWS = FLAT_PARTS[0][1]
GROUP_A = (("w_mlp1", 1024), ("w_mlp2", 1024), ("w_out", 256), ("w_attn_proj", 128), ("w_conv_proj", 128))
COL_SHARDED = ("w_in", "w_attn_proj", "w_conv_proj", "w_mlp1")
FULL_SHAPES = {"w_in": (D, IN_W), "w_attn_proj": (ATTN_W, D), "w_conv_proj": (CONV_C, D), "w_out": (D, D),
               "w_mlp1": (D, D_FF), "w_mlp2": (D_FF, D)}


def _place():
    x, y, c = lax.axis_index("x"), lax.axis_index("y"), lax.axis_index("c")
    return x, y, c, 2 * x + y


def _peer_chips(x, y, j):
    return [((x, 1 - y), j ^ 1), ((1 - x, y), j ^ 2), ((1 - x, 1 - y), j ^ 3)]


def _remote(src, dst, sems, k, n, to):
    return pltpu.make_async_remote_copy(src_ref=src, dst_ref=dst, send_sem=sems.at[k], recv_sem=sems.at[n + k],
                                        device_id=to, device_id_type=MESH)


def _half(c, rows):
    h = rows // 2
    return pl.ds(pl.multiple_of(c * h, 16), h)


def gather_rider(wsh):
    R = wsh.shape[0]
    n = 7

    def plan(rins, routs, sems):
        (w_ref,), (out_ref,) = rins, routs
        x, y, c, j = _place()
        peers = _peer_chips(x, y, j)
        mine, other = _half(c, R), _half(1 - c, R)
        sent = [_remote(w_ref.at[mine], out_ref.at[j, mine], sems, k, n, (*chip, c)) for k, (chip, _) in enumerate(peers)]
        sent.append(_remote(w_ref, out_ref.at[j], sems, 6, n, (x, y, 1 - c)))
        landed = [_remote(w_ref.at[mine], out_ref.at[pj, mine], sems, k, n, (x, y, c)) for k, (_, pj) in enumerate(peers)]
        passed = [_remote(out_ref.at[pj, mine], out_ref.at[pj, mine], sems, 3 + k, n, (x, y, 1 - c))
                  for k, (_, pj) in enumerate(peers)]
        handed = [_remote(w_ref.at[mine], out_ref.at[pj, other], sems, 3 + k, n, (x, y, c)) for k, (_, pj) in enumerate(peers)]
        handed.append(_remote(w_ref, out_ref.at[j], sems, 6, n, (x, y, c)))
        return sent, landed, passed, handed

    def start(rins, routs, sems):
        for cp in plan(rins, routs, sems)[0]:
            cp.start()

    def late(rins, routs, sems):
        _, landed, passed, _ = plan(rins, routs, sems)
        for k in range(3):
            landed[k].wait_recv()
            passed[k].start()

    def finish(rins, routs, sems):
        sent, _, passed, handed = plan(rins, routs, sems)
        for cp in handed:
            cp.wait_recv()
        for cp in sent + passed:
            cp.wait_send()

    return Rider((wsh,), (jax.ShapeDtypeStruct((N_CHIPS,) + wsh.shape, wsh.dtype),), 2 * n, start, finish, late)


def swap_rider(g):
    R = g.shape[1]

    def plan(rins, routs, sems):
        (g_ref,), (got_ref,) = rins, routs
        x, y, c, _ = _place()
        return _remote(g_ref.at[:, _half(1 - c, R), :], got_ref, sems, 0, 1, (x, y, 1 - c))

    def start(rins, routs, sems):
        plan(rins, routs, sems).start()

    def finish(rins, routs, sems):
        plan(rins, routs, sems).wait()

    return Rider((g,), (jax.ShapeDtypeStruct((N_CHIPS, R // 2, FLAT_W), g.dtype),), 2, start, finish)


def exchange_rider(pb):
    def plan(rins, routs, sems):
        (pb_ref,), (got_ref,) = rins, routs
        x, y, c, j = _place()
        peers = _peer_chips(x, y, j)
        sent = [_remote(pb_ref.at[pj], got_ref.at[j], sems, k, 3, (*chip, c)) for k, (chip, pj) in enumerate(peers)]
        landed = [_remote(pb_ref.at[pj], got_ref.at[pj], sems, k, 3, (x, y, c)) for k, (_, pj) in enumerate(peers)]
        return sent, landed

    def start(rins, routs, sems):
        for cp in plan(rins, routs, sems)[0]:
            cp.start()

    def finish(rins, routs, sems):
        sent, landed = plan(rins, routs, sems)
        for cp in landed:
            cp.wait_recv()
        for cp in sent:
            cp.wait_send()

    return Rider((pb,), (jax.ShapeDtypeStruct(pb.shape, pb.dtype),), 6, start, finish)


def share_rider(tot):
    def plan(rins, routs, sems):
        (t_ref,), (got_ref,) = rins, routs
        x, y, c, _ = _place()
        return _remote(t_ref, got_ref, sems, 0, 1, (x, y, 1 - c))

    def start(rins, routs, sems):
        plan(rins, routs, sems).start()

    def finish(rins, routs, sems):
        plan(rins, routs, sems).wait()

    return Rider((tot,), (jax.ShapeDtypeStruct(tot.shape, tot.dtype),), 2, start, finish)


def pair_sum(g, got):
    nj, R, W = g.shape
    h = R // 2
    tile = h // 2

    def body(g_ref, got_ref, pb_ref, own_ref):
        v = g_ref[...] + got_ref[...]
        pb_ref[...] = v.astype(pb_ref.dtype)

        @pl.when(pl.program_id(1) == _place()[3])
        def _():
            own_ref[...] = v

    return pl.pallas_call(
        body, name="pair_sum", grid=(h // tile, nj),
        in_specs=[pl.BlockSpec((None, tile, W), lambda r, j: (j, lax.axis_index("c") * (h // tile) + r, 0)),
                  pl.BlockSpec((None, tile, W), lambda r, j: (j, r, 0))],
        out_specs=[pl.BlockSpec((None, tile, W), lambda r, j: (j, r, 0)), pl.BlockSpec((tile, W), lambda r, j: (r, 0))],
        out_shape=[jax.ShapeDtypeStruct((nj, h, W), CDT), jax.ShapeDtypeStruct((h, W), F32)],
        compiler_params=_cp("arbitrary", "arbitrary"),
    )(g, got)


def total_sum(own, got):
    R, W = own.shape
    tile = R // 2

    def body(own_ref, a_ref, b_ref, c_ref, o_ref):
        o_ref[...] = ((own_ref[...] + a_ref[...].astype(F32)) + b_ref[...].astype(F32)) + c_ref[...].astype(F32)

    def slab(k):
        return pl.BlockSpec((None, tile, W), lambda r: (_place()[3] ^ (k + 1), r, 0))

    return pl.pallas_call(
        body, name="total_sum", grid=(R // tile,),
        in_specs=[pl.BlockSpec((tile, W), lambda r: (r, 0)), slab(0), slab(1), slab(2)],
        out_specs=pl.BlockSpec((tile, W), lambda r: (r, 0)),
        out_shape=jax.ShapeDtypeStruct((R, W), F32),
        compiler_params=_cp("arbitrary"),
    )(own, got, got, got)


def _all_peers(x, y, c):
    return [(x ^ (r >> 2), y ^ ((r >> 1) & 1), c ^ (r & 1)) for r in range(1, N_DEV)]


ROW_ITEMS = (("mix_norm_g", D), ("b_in", IN_W), ("sinks", N_Q), ("conv_b", CONV_C), ("conv_ln_g", CONV_C),
             ("conv_ln_b", CONV_C), ("b_conv_proj", D), ("mlp_norm_g", D))
TAPS_ROW = 16
TAPS_ROWS = 32
LAYER_ROWS = TAPS_ROW + TAPS_ROWS
FINAL_ROW = DEPTH * LAYER_ROWS
SMALL_ROWS = FINAL_ROW + SUBLANES


def _row_chunks():
    out, r = {}, 0
    for n, width in ROW_ITEMS:
        out[n] = [(r + i, FLAT_W * i, min(FLAT_W, width - FLAT_W * i)) for i in range(-(-width // FLAT_W))]
        r += len(out[n])
    assert r <= TAPS_ROW
    return out


def sum_small(gsm, rider):
    chunks = _row_chunks()
    ins = []
    for l in range(DEPTH):
        ins += [gsm[n][l] for n, _ in ROW_ITEMS] + [gsm["conv_w"][l]]
    ins.append(gsm["final_norm_g"])
    n_in = len(ins)

    def body(*refs):
        in_refs, rin, o_ref, rout = refs[:n_in], refs[n_in], refs[n_in + 1], refs[n_in + 2]
        buf, send_sems, recv_sems, rsems = refs[n_in + 3:n_in + 7]
        rider.start((rin,), (rout,), rsems)
        x, y, c, _ = _place()
        me = 4 * x + 2 * y + c
        mine = buf.at[me]
        mine[...] = jnp.zeros((SMALL_ROWS, FLAT_W), F32)
        k = 0
        for l in range(DEPTH):
            for n, _ in ROW_ITEMS:
                for r, c0, wd in chunks[n]:
                    mine[l * LAYER_ROWS + r:l * LAYER_ROWS + r + 1, 0:wd] = in_refs[k][:, c0:c0 + wd]
                k += 1
            mine[l * LAYER_ROWS + TAPS_ROW:(l + 1) * LAYER_ROWS, 0:CONV_C] = in_refs[k][...]
            k += 1
        mine[FINAL_ROW:FINAL_ROW + 1, :] = in_refs[k][...]
        peers = _all_peers(x, y, c)
        sends = [pltpu.make_async_remote_copy(src_ref=mine, dst_ref=mine, send_sem=send_sems.at[r], recv_sem=recv_sems.at[r],
                                              device_id=to, device_id_type=MESH) for r, to in enumerate(peers)]
        for cp in sends:
            cp.start()
        for r in range(N_DEV - 1):
            pltpu.make_async_remote_copy(src_ref=mine, dst_ref=buf.at[me ^ (r + 1)], send_sem=send_sems.at[r],
                                         recv_sem=recv_sems.at[r], device_id=(x, y, c), device_id_type=MESH).wait_recv()
        for cp in sends:
            cp.wait_send()
        acc = buf[0]
        for d in range(1, N_DEV):
            acc = acc + buf[d]
        o_ref[...] = acc
        rider.finish((rin,), (rout,), rsems)

    vm = pl.BlockSpec(memory_space=pltpu.VMEM)
    return pl.pallas_call(
        body, name="sum_small", out_shape=[jax.ShapeDtypeStruct((SMALL_ROWS, FLAT_W), F32), rider.outs[0]],
        in_specs=[vm] * n_in + [_any()], out_specs=[vm, _any()],
        scratch_shapes=[pltpu.VMEM((N_DEV, SMALL_ROWS, FLAT_W), F32), pltpu.SemaphoreType.DMA((N_DEV - 1,)),
                        pltpu.SemaphoreType.DMA((N_DEV - 1,)), pltpu.SemaphoreType.DMA((rider.n_sems,))],
    )(*ins, rider.ins[0])


def gather_taps(taps):
    shard = taps.shape[2]

    def body(t_ref, o_ref, buf, send_sems, recv_sems):
        x, y, c, j = _place()
        peers = _peer_chips(x, y, j)
        buf[j] = t_ref[...]
        sends = [pltpu.make_async_remote_copy(src_ref=t_ref, dst_ref=buf.at[j], send_sem=send_sems.at[k],
                                              recv_sem=recv_sems.at[k], device_id=(*chip, c), device_id_type=MESH)
                 for k, (chip, _) in enumerate(peers)]
        for cp in sends:
            cp.start()
        for k, (_, pj) in enumerate(peers):
            pltpu.make_async_remote_copy(src_ref=t_ref, dst_ref=buf.at[pj], send_sem=send_sems.at[k],
                                         recv_sem=recv_sems.at[k], device_id=(x, y, c), device_id_type=MESH).wait_recv()
        for cp in sends:
            cp.wait_send()
        for jj in range(N_CHIPS):
            o_ref[:, :, jj * shard:(jj + 1) * shard] = buf[jj]

    vm = pl.BlockSpec(memory_space=pltpu.VMEM)
    return pl.pallas_call(
        body, name="gather_taps", out_shape=jax.ShapeDtypeStruct(taps.shape[:2] + (N_CHIPS * shard,), taps.dtype),
        in_specs=[vm], out_specs=vm,
        scratch_shapes=[pltpu.VMEM((N_CHIPS,) + taps.shape, taps.dtype), pltpu.SemaphoreType.DMA((3,)),
                        pltpu.SemaphoreType.DMA((3,))],
    )(taps)


def _adam_math(w, g, m, v):
    nm = ADAM_B1 * m + (1.0 - ADAM_B1) * g
    nv = ADAM_B2 * v + (1.0 - ADAM_B2) * jnp.square(g)
    m_hat = nm / (1.0 - ADAM_B1 ** ADAM_STEP)
    v_hat = nv / (1.0 - ADAM_B2 ** ADAM_STEP)
    return -ADAM_LR * (m_hat / (jnp.sqrt(v_hat) + ADAM_EPS) + ADAM_WD * w), nm, nv


def adamw(w, g, m, v, *, name):
    L, R, C = w.shape
    tr = next(t for t in (512, 480, 256, 128) if R % t == 0)

    def body(w_ref, g_ref, m_ref, v_ref, d_ref, nm_ref, nv_ref):
        d_ref[...], nm_ref[...], nv_ref[...] = _adam_math(w_ref[...], g_ref[...], m_ref[...], v_ref[...])

    spec = pl.BlockSpec((None, tr, C), lambda l, i: (l, i, 0))
    out = jax.ShapeDtypeStruct((L, R, C), F32)
    return pl.pallas_call(
        body, name=name, grid=(L, R // tr), in_specs=[spec] * 4, out_specs=[spec] * 3, out_shape=[out] * 3,
        compiler_params=_cp("parallel", "parallel"),
    )(w, g, m, v)


def adamw_small(packed, w, m, v):
    chunks = _row_chunks()
    names = SMALL + ("final_norm_g",)
    as_2d = lambda a: a.reshape(1, -1) if a.ndim == 1 else a
    ins = [as_2d(t[n]) for n in names for t in (w, m, v)]
    shapes = [jax.ShapeDtypeStruct(as_2d(w[n]).shape, F32) for n in names for _ in range(4)]
    n_in = len(ins)

    def body(p_ref, *refs):
        in_refs, out_refs = refs[:n_in], refs[n_in:]
        chip = _place()[3]
        for i, n in enumerate(names):
            w_ref, m_ref, v_ref = in_refs[3 * i:3 * i + 3]
            outs = out_refs[4 * i:4 * i + 4]

            def step(at, g):
                res = (g,) + _adam_math(w_ref[at], g, m_ref[at], v_ref[at])
                for o_ref, val in zip(outs, res):
                    o_ref[at] = val

            if n == "final_norm_g":
                step((slice(None), slice(None)), p_ref[FINAL_ROW:FINAL_ROW + 1, :])
                continue
            for l in range(DEPTH):
                if n == "conv_w":
                    r0 = l * LAYER_ROWS + TAPS_ROW
                    shard = CONV_C // N_CHIPS
                    g = jnp.zeros((CONV_K, shard), F32)
                    for j in range(N_CHIPS):
                        g = jnp.where(chip == j, p_ref[r0:r0 + CONV_K, j * shard:(j + 1) * shard], g)
                    step((l,), g)
                else:
                    for r, c0, wd in chunks[n]:
                        step((slice(l, l + 1), slice(c0, c0 + wd)),
                             p_ref[l * LAYER_ROWS + r:l * LAYER_ROWS + r + 1, 0:wd])

    vm = pl.BlockSpec(memory_space=pltpu.VMEM)
    res = pl.pallas_call(
        body, name="adamw_small", out_shape=shapes,
        in_specs=[vm] + [vm] * n_in, out_specs=[vm] * len(shapes),
    )(packed, *ins)
    dicts = ({}, {}, {}, {})
    for i, n in enumerate(names):
        for d, val in zip(dicts, res[4 * i:4 * i + 4]):
            d[n] = val.reshape(w[n].shape)
    return dicts


def _flat_rows(name, shard):
    return shard.T if name == "w_in" else shard.reshape(-1, FLAT_W)


def _full_matrix(slabs, name):
    K, N = FULL_SHAPES[name]
    if name == "w_in":
        return slabs.reshape(N, K)
    if name in COL_SHARDED:
        return slabs.reshape(N_CHIPS, K, N // N_CHIPS).transpose(1, 0, 2).reshape(K, N)
    return slabs.reshape(K, N)


def _first_row(parts, name):
    r = 0
    for n, rows in parts:
        if n == name:
            return r, rows
        r += rows
    raise KeyError(name)


class _Exchange:
    CARRIERS = {
        ("conv_bwd_ln", 1): ((1, "A"), "swap"), ("attn_bwd", 1): ((1, "A"), "exchange"), ("tn_in", 1): ((1, "A"), "share"),
        ("mlp_bwd", 0): ((1, "B"), "swap"), ("tn_mlp1", 0): ((1, "B"), "exchange"), ("tn_mlp2", 0): ((1, "B"), "share"),
        ("conv_bwd_ln", 0): ((0, "A"), "swap"), ("attn_bwd", 0): ((0, "A"), "exchange"), ("tn_in", 0): ((0, "A"), "share"),
    }

    def __init__(self, w, ci, chip):
        self.w, self.ci, self.chip = w, ci, chip
        self.wsh = [jnp.concatenate([_flat_rows(n, w[n][l]) for n, _ in FLAT_PARTS], axis=0).astype(CDT)
                    for l in range(DEPTH)]
        self.final_g = w["final_norm_g"].reshape(1, D)
        self.slabs = {}
        self.full = {}
        self.units = {}
        self.reduced = {}
        self._landed_weights(0, 0, _run_alone(gather_rider(self.wsh[0][:W_IN_ROWS]), "gather_w_in")[0])
        self.all_taps = gather_taps(w["conv_w"])

    def _landed_weights(self, l, r0, buf):
        self.slabs.setdefault(l, []).append((r0, buf))

    def _matrix(self, l, name):
        if (l, name) not in self.full:
            r, rows = _first_row(FLAT_PARTS, name)
            r0, buf = next((r0, buf) for r0, buf in self.slabs[l] if r0 <= r < r0 + buf.shape[1])
            self.full[(l, name)] = _full_matrix(buf[:, r - r0:r - r0 + rows], name)
        return self.full[(l, name)]

    def w_in(self, l):
        return self._matrix(l, "w_in")

    def mats(self, l):
        return {n: self._matrix(l, n) for n in MATRICES if n != "w_in"}

    def vec(self, n, l):
        return self.w[n][l].reshape(1, -1)

    def sinks(self, l):
        return self.w["sinks"][l]

    def taps(self, l):
        return self.all_taps[l]

    def rider(self, kernel, l):
        if (kernel, l) == ("rms_inproj", 0):
            return gather_rider(self.wsh[0][W_IN_ROWS:])
        if (kernel, l) == ("mlp_fwd", 0):
            return gather_rider(self.wsh[1])
        if (kernel, l) in self.CARRIERS:
            return self._stage(*self.CARRIERS[(kernel, l)])
        return None

    def landed(self, kernel, l, bufs):
        if (kernel, l) == ("rms_inproj", 0):
            self._landed_weights(0, W_IN_ROWS, bufs[0])
        elif (kernel, l) == ("mlp_fwd", 0):
            self._landed_weights(1, 0, bufs[0])
        else:
            self._stage_landed(*self.CARRIERS[(kernel, l)], bufs[0])

    def grads(self, l, group, g):
        if group == "A":
            flat = jnp.concatenate([g[n].reshape(N_CHIPS, rows, FLAT_W) for n, rows in GROUP_A], axis=1)
        else:
            flat = g["w_in"].reshape(N_CHIPS, W_IN_ROWS, FLAT_W)
        self.units[(l, group)] = {"g": flat}

    def _stage(self, key, stage):
        u = self.units[key]
        if stage == "swap":
            return swap_rider(u["g"])
        if stage == "exchange":
            u["pb"], u["own"] = pair_sum(u["g"], u["swap"])
            return exchange_rider(u["pb"])
        u["tot"] = total_sum(u["own"], u["exchange"])
        return share_rider(u["tot"])

    def _stage_landed(self, key, stage, buf):
        u = self.units[key]
        u[stage] = buf
        if stage == "share":
            tot = u["tot"]
            self.reduced[key] = jnp.where(self.ci == 0, jnp.concatenate([tot, buf]), jnp.concatenate([buf, tot]))

    def finish(self, gsm):
        key = (0, "B")
        self._stage_landed(key, "swap", _run_alone(self._stage(key, "swap"), "swap_last")[0])
        packed, landed = sum_small(gsm, self._stage(key, "exchange"))
        self._stage_landed(key, "exchange", landed)
        self._stage_landed(key, "share", _run_alone(self._stage(key, "share"), "share_last")[0])
        return packed, self._shard_grads()

    def _shard_grads(self):
        out = {}
        for n in MATRICES:
            per_layer = []
            for l in range(DEPTH):
                if n == "w_in":
                    per_layer.append(self.reduced[(l, "B")].T)
                    continue
                r, rows = _first_row(GROUP_A, n)
                per_layer.append(self.reduced[(l, "A")][r:r + rows].reshape(self.w[n].shape[1:]))
            out[n] = jnp.stack(per_layer)
        return out


WEIGHTS = ("mix_norm_g", "w_in", "b_in", "sinks", "conv_w", "conv_b", "conv_ln_g", "conv_ln_b", "w_attn_proj",
           "w_conv_proj", "b_conv_proj", "w_out", "mlp_norm_g", "w_mlp1", "w_mlp2", "final_norm_g")


def kernel(x, mix_norm_g, w_in, b_in, sinks, conv_w, conv_b, conv_ln_g, conv_ln_b, w_attn_proj, w_conv_proj, b_conv_proj, w_out, mlp_norm_g, w_mlp1, w_mlp2, final_norm_g, loss_target, m_mix_norm_g, m_w_in, m_b_in, m_sinks, m_conv_w, m_conv_b, m_conv_ln_g, m_conv_ln_b, m_w_attn_proj, m_w_conv_proj, m_b_conv_proj, m_w_out, m_mlp_norm_g, m_w_mlp1, m_w_mlp2, m_final_norm_g, v_mix_norm_g, v_w_in, v_b_in, v_sinks, v_conv_w, v_conv_b, v_conv_ln_g, v_conv_ln_b, v_w_attn_proj, v_w_conv_proj, v_b_conv_proj, v_w_out, v_mlp_norm_g, v_w_mlp1, v_w_mlp2, v_final_norm_g):
    w = dict(zip(WEIGHTS, (mix_norm_g, w_in, b_in, sinks, conv_w, conv_b, conv_ln_g, conv_ln_b, w_attn_proj, w_conv_proj,
                           b_conv_proj, w_out, mlp_norm_g, w_mlp1, w_mlp2, final_norm_g)))
    m = dict(zip(WEIGHTS, (m_mix_norm_g, m_w_in, m_b_in, m_sinks, m_conv_w, m_conv_b, m_conv_ln_g, m_conv_ln_b, m_w_attn_proj,
                           m_w_conv_proj, m_b_conv_proj, m_w_out, m_mlp_norm_g, m_w_mlp1, m_w_mlp2, m_final_norm_g)))
    v = dict(zip(WEIGHTS, (v_mix_norm_g, v_w_in, v_b_in, v_sinks, v_conv_w, v_conv_b, v_conv_ln_g, v_conv_ln_b, v_w_attn_proj,
                           v_w_conv_proj, v_b_conv_proj, v_w_out, v_mlp_norm_g, v_w_mlp1, v_w_mlp2, v_final_norm_g)))
    xi, yi, ci = lax.axis_index("x"), lax.axis_index("y"), lax.axis_index("c")
    chip = 2 * xi + yi

    hooks = _Exchange(w, ci, chip)
    loss, dx, gsm = forward_backward(x[0], loss_target[0], hooks)
    loss = lax.psum(loss[0, 0], ("x", "y", "c"))
    packed, grads = hooks.finish(gsm)

    gsmall, delta, new_m, new_v = adamw_small(packed, w, m, v)
    grads.update(gsmall)
    for n in MATRICES:
        t = (lambda a: jnp.swapaxes(a, 1, 2)) if n == "w_in" else (lambda a: a)
        delta[n], new_m[n], new_v[n] = map(t, adamw(t(w[n]), t(grads[n]), t(m[n]), t(v[n]), name="adamw_" + n))

    return (loss, dx[None], *[grads[n] for n in WEIGHTS], *[delta[n] for n in WEIGHTS],
            *[new_m[n] for n in WEIGHTS], *[new_v[n] for n in WEIGHTS])
```
